```python
import jax, jax.numpy as jnp
from jax import lax
import numpy as np

D_MODEL = 2048
BATCH = 8
SEQ = 4096
DEPTH = 4

MIX_WIDTH = D_MODEL
EPS = 1e-6
BLOCK = 128
NEG = -1e30

SWA_WIDTH = MIX_WIDTH // 2
SWA_HEAD_DIM = 64
SWA_Q_HEADS = SWA_WIDTH // SWA_HEAD_DIM
SWA_KV_HEADS = 2
SWA_GROUP = SWA_Q_HEADS // SWA_KV_HEADS
WINDOW = 128

MLA_WIDTH = MIX_WIDTH - SWA_WIDTH
MLA_V_DIM = 128
MLA_HEADS = MLA_WIDTH // MLA_V_DIM
MLA_NOPE_DIM = 128
MLA_ROPE_DIM = 64
MLA_QK_DIM = MLA_NOPE_DIM + MLA_ROPE_DIM
Q_LORA_RANK = 384
KV_LORA_RANK = 256
ROPE_THETA = 10000.0

A_Q = SWA_Q_HEADS * SWA_HEAD_DIM
A_KV = SWA_KV_HEADS * SWA_HEAD_DIM
A_GATE = SWA_WIDTH
B_GATE = MLA_WIDTH
IN_WIDTH = A_Q + 2 * A_KV + A_GATE + Q_LORA_RANK + KV_LORA_RANK + MLA_ROPE_DIM + B_GATE
SPLIT_POINTS = (
    A_Q,
    A_Q + A_KV,
    A_Q + 2 * A_KV,
    A_Q + 2 * A_KV + A_GATE,
    A_Q + 2 * A_KV + A_GATE + Q_LORA_RANK,
    A_Q + 2 * A_KV + A_GATE + Q_LORA_RANK + KV_LORA_RANK,
    A_Q + 2 * A_KV + A_GATE + Q_LORA_RANK + KV_LORA_RANK + MLA_ROPE_DIM,
)

kernel_name = "hybrid_swa_sink_alibi_mla_gated_trunk"


def rmsnorm(x, g):
    xf = x.astype(jnp.float32)
    y = xf * lax.rsqrt(jnp.mean(xf * xf, axis=-1, keepdims=True) + EPS) * g.astype(jnp.float32)
    return y.astype(x.dtype)


def rope(x, cos, sin):
    half = x.shape[-1] // 2
    x1, x2 = x[..., :half], x[..., half:]
    cos = cos.astype(x.dtype)
    sin = sin.astype(x.dtype)
    return jnp.concatenate([x1 * cos - x2 * sin, x2 * cos + x1 * sin], axis=-1)


def swa_attention(q, k, v, sinks):
    b, s = q.shape[0], q.shape[1]
    nb = s // BLOCK
    qb = q.reshape(b, nb, BLOCK, SWA_KV_HEADS, SWA_GROUP, SWA_HEAD_DIM)
    pad = ((0, 0), (BLOCK, 0), (0, 0), (0, 0))
    kp = jnp.pad(k, pad).reshape(b, nb + 1, BLOCK, SWA_KV_HEADS, SWA_HEAD_DIM)
    vp = jnp.pad(v, pad).reshape(b, nb + 1, BLOCK, SWA_KV_HEADS, SWA_HEAD_DIM)
    kk = jnp.concatenate([kp[:, :-1], kp[:, 1:]], axis=2)
    vv = jnp.concatenate([vp[:, :-1], vp[:, 1:]], axis=2)
    scores = jnp.einsum('bnqhgd,bnkhd->bnhgqk', qb, kk).astype(jnp.float32) * (SWA_HEAD_DIM ** -0.5)
    qi = jnp.arange(BLOCK)[:, None]
    ki = jnp.arange(2 * BLOCK)[None, :]
    delta = BLOCK + qi - ki
    key_pos = (jnp.arange(nb)[:, None] - 1) * BLOCK + jnp.arange(2 * BLOCK)[None, :]
    valid = ((delta >= 0) & (delta < WINDOW))[None] & (key_pos >= 0)[:, None, :]
    slopes = jnp.exp2(-8.0 * jnp.arange(1, SWA_Q_HEADS + 1, dtype=jnp.float32) / SWA_Q_HEADS)
    slopes = slopes.reshape(SWA_KV_HEADS, SWA_GROUP)
    alibi = -slopes[:, :, None, None] * delta.astype(jnp.float32)[None, None]
    scores = jnp.where(valid[None, :, None, None], scores + alibi, NEG)
    sink = sinks.astype(jnp.float32).reshape(SWA_KV_HEADS, SWA_GROUP, 1, 1)
    sink = jnp.broadcast_to(sink, scores.shape[:-1] + (1,))
    probs = jax.nn.softmax(jnp.concatenate([scores, sink], axis=-1), axis=-1)[..., :-1]
    out = jnp.einsum('bnhgqk,bnkhd->bnqhgd', probs.astype(v.dtype), vv)
    return out.reshape(b, s, SWA_WIDTH)


def mla_attention(c_q, c_kv, k_rope, q_a_g, kv_a_g, w_q_b, w_kv_b, cos, sin):
    b, s = c_q.shape[0], c_q.shape[1]
    q = (rmsnorm(c_q, q_a_g) @ w_q_b).reshape(b, s, MLA_HEADS, MLA_QK_DIM)
    q_nope = q[..., :MLA_NOPE_DIM]
    q_rope = rope(q[..., MLA_NOPE_DIM:], cos[:, None, :], sin[:, None, :])
    kv = (rmsnorm(c_kv, kv_a_g) @ w_kv_b).reshape(b, s, MLA_HEADS, MLA_NOPE_DIM + MLA_V_DIM)
    k_nope = kv[..., :MLA_NOPE_DIM]
    v = kv[..., MLA_NOPE_DIM:]
    k_r = rope(k_rope, cos, sin)
    nb = s // BLOCK
    qn_b = q_nope.reshape(b, nb, BLOCK, MLA_HEADS, MLA_NOPE_DIM).transpose(1, 0, 2, 3, 4)
    qr_b = q_rope.reshape(b, nb, BLOCK, MLA_HEADS, MLA_ROPE_DIM).transpose(1, 0, 2, 3, 4)
    key_pos = jnp.arange(s)
    scale = MLA_QK_DIM ** -0.5

    def one_block(args):
        qn, qr, i = args
        sc = (jnp.einsum('bqhd,bkhd->bhqk', qn, k_nope)
              + jnp.einsum('bqhd,bkd->bhqk', qr, k_r)).astype(jnp.float32) * scale
        q_pos = i * BLOCK + jnp.arange(BLOCK)
        mask = key_pos[None, :] <= q_pos[:, None]
        p = jax.nn.softmax(jnp.where(mask, sc, NEG), axis=-1)
        return jnp.einsum('bhqk,bkhd->bqhd', p.astype(v.dtype), v)

    out = lax.map(one_block, (qn_b, qr_b, jnp.arange(nb)))
    return out.transpose(1, 0, 2, 3, 4).reshape(b, s, MLA_WIDTH)


def _fwd_setup_inputs(seed: int = 0) -> dict:
    key = jax.random.key(seed)
    ks = jax.random.split(key, 11)
    f32 = jnp.float32
    x = jax.random.normal(ks[0], (BATCH, SEQ, D_MODEL), f32)
    attn_norm_g = 1.0 + 0.02 * jax.random.normal(ks[1], (DEPTH, D_MODEL), f32)
    w_in = jax.random.normal(ks[2], (DEPTH, D_MODEL, IN_WIDTH), f32) * D_MODEL ** -0.5
    swa_sinks = 0.5 * jax.random.normal(ks[3], (DEPTH, SWA_Q_HEADS), f32)
    q_a_norm_g = 1.0 + 0.02 * jax.random.normal(ks[4], (DEPTH, Q_LORA_RANK), f32)
    kv_a_norm_g = 1.0 + 0.02 * jax.random.normal(ks[5], (DEPTH, KV_LORA_RANK), f32)
    w_q_b = jax.random.normal(ks[6], (DEPTH, Q_LORA_RANK, MLA_HEADS * MLA_QK_DIM), f32) * Q_LORA_RANK ** -0.5
    w_kv_b = jax.random.normal(ks[7], (DEPTH, KV_LORA_RANK, MLA_HEADS * (MLA_NOPE_DIM + MLA_V_DIM)), f32) * KV_LORA_RANK ** -0.5
    w_out = jax.random.normal(ks[8], (DEPTH, MIX_WIDTH, D_MODEL), f32) * MIX_WIDTH ** -0.5
    final_norm_g = 1.0 + 0.02 * jax.random.normal(ks[9], (D_MODEL,), f32)
    return {"x": x, "attn_norm_g": attn_norm_g, "w_in": w_in, "swa_sinks": swa_sinks,
            "q_a_norm_g": q_a_norm_g, "kv_a_norm_g": kv_a_norm_g, "w_q_b": w_q_b,
            "w_kv_b": w_kv_b, "w_out": w_out, "final_norm_g": final_norm_g}


def _fwd_reference(x, attn_norm_g, w_in, swa_sinks, q_a_norm_g, kv_a_norm_g, w_q_b, w_kv_b, w_out, final_norm_g):
    b, s = x.shape[0], x.shape[1]
    pos = jnp.arange(s, dtype=jnp.float32)
    inv_freq = ROPE_THETA ** (-jnp.arange(0, MLA_ROPE_DIM, 2, dtype=jnp.float32) / MLA_ROPE_DIM)
    ang = pos[:, None] * inv_freq[None, :]
    cos, sin = jnp.cos(ang), jnp.sin(ang)
    for l in range(DEPTH):
        h = rmsnorm(x, attn_norm_g[l])
        proj = h @ w_in[l]
        qa, ka, va, ga, cq, ckv, kr, gb = jnp.split(proj, SPLIT_POINTS, axis=-1)
        ya = swa_attention(qa.reshape(b, s, SWA_Q_HEADS, SWA_HEAD_DIM),
                           ka.reshape(b, s, SWA_KV_HEADS, SWA_HEAD_DIM),
                           va.reshape(b, s, SWA_KV_HEADS, SWA_HEAD_DIM),
                           swa_sinks[l]) * jax.nn.silu(ga)
        yb = mla_attention(cq, ckv, kr, q_a_norm_g[l], kv_a_norm_g[l], w_q_b[l], w_kv_b[l],
                           cos, sin) * jax.nn.silu(gb)
        x = x + jnp.concatenate([ya, yb], axis=-1) @ w_out[l]
    return rmsnorm(x, final_norm_g)


import jax as _jax
import jax.numpy as _jnp

TWIN_FORMAT = 'train_step'
FWD_PARAMS = ['x', 'attn_norm_g', 'w_in', 'swa_sinks', 'q_a_norm_g', 'kv_a_norm_g', 'w_q_b', 'w_kv_b', 'w_out', 'final_norm_g']
TWIN_WEIGHTS = ['attn_norm_g', 'w_in', 'swa_sinks', 'q_a_norm_g', 'kv_a_norm_g', 'w_q_b', 'w_kv_b', 'w_out', 'final_norm_g']
TWIN_DIFF_INPUT = 'x'
TWIN_INPUTS = ['x', 'attn_norm_g', 'w_in', 'swa_sinks', 'q_a_norm_g', 'kv_a_norm_g', 'w_q_b', 'w_kv_b', 'w_out', 'final_norm_g', 'loss_target', 'm_attn_norm_g', 'm_w_in', 'm_swa_sinks', 'm_q_a_norm_g', 'm_kv_a_norm_g', 'm_w_q_b', 'm_w_kv_b', 'm_w_out', 'm_final_norm_g', 'v_attn_norm_g', 'v_w_in', 'v_swa_sinks', 'v_q_a_norm_g', 'v_kv_a_norm_g', 'v_w_q_b', 'v_w_kv_b', 'v_w_out', 'v_final_norm_g']
TWIN_OUTPUTS = ['loss', 'grad_x', 'grad_attn_norm_g', 'grad_w_in', 'grad_swa_sinks', 'grad_q_a_norm_g', 'grad_kv_a_norm_g', 'grad_w_q_b', 'grad_w_kv_b', 'grad_w_out', 'grad_final_norm_g', 'delta_attn_norm_g', 'delta_w_in', 'delta_swa_sinks', 'delta_q_a_norm_g', 'delta_kv_a_norm_g', 'delta_w_q_b', 'delta_w_kv_b', 'delta_w_out', 'delta_final_norm_g', 'new_m_attn_norm_g', 'new_m_w_in', 'new_m_swa_sinks', 'new_m_q_a_norm_g', 'new_m_kv_a_norm_g', 'new_m_w_q_b', 'new_m_w_kv_b', 'new_m_w_out', 'new_m_final_norm_g', 'new_v_attn_norm_g', 'new_v_w_in', 'new_v_swa_sinks', 'new_v_q_a_norm_g', 'new_v_kv_a_norm_g', 'new_v_w_q_b', 'new_v_w_kv_b', 'new_v_w_out', 'new_v_final_norm_g']
TWIN_LEAF_KINDS = {'loss': 'loss', 'grad_x': 'grad_x', 'grad_attn_norm_g': 'grad_w', 'grad_w_in': 'grad_w', 'grad_swa_sinks': 'grad_w', 'grad_q_a_norm_g': 'grad_w', 'grad_kv_a_norm_g': 'grad_w', 'grad_w_q_b': 'grad_w', 'grad_w_kv_b': 'grad_w', 'grad_w_out': 'grad_w', 'grad_final_norm_g': 'grad_w', 'delta_attn_norm_g': 'delta_w', 'delta_w_in': 'delta_w', 'delta_swa_sinks': 'delta_w', 'delta_q_a_norm_g': 'delta_w', 'delta_kv_a_norm_g': 'delta_w', 'delta_w_q_b': 'delta_w', 'delta_w_kv_b': 'delta_w', 'delta_w_out': 'delta_w', 'delta_final_norm_g': 'delta_w', 'new_m_attn_norm_g': 'new_m', 'new_m_w_in': 'new_m', 'new_m_swa_sinks': 'new_m', 'new_m_q_a_norm_g': 'new_m', 'new_m_kv_a_norm_g': 'new_m', 'new_m_w_q_b': 'new_m', 'new_m_w_kv_b': 'new_m', 'new_m_w_out': 'new_m', 'new_m_final_norm_g': 'new_m', 'new_v_attn_norm_g': 'new_v', 'new_v_w_in': 'new_v', 'new_v_swa_sinks': 'new_v', 'new_v_q_a_norm_g': 'new_v', 'new_v_kv_a_norm_g': 'new_v', 'new_v_w_q_b': 'new_v', 'new_v_w_kv_b': 'new_v', 'new_v_w_out': 'new_v', 'new_v_final_norm_g': 'new_v'}


def _forward(args):
    return _fwd_reference(*[args[k] for k in FWD_PARAMS])


def _output_shape():
    def fwd():
        inp = _fwd_setup_inputs(0)
        return _fwd_reference(*[inp[k] for k in FWD_PARAMS])
    out = _jax.eval_shape(fwd)
    return out.shape, out.dtype

N_MICROBATCH = 1
ADAM_LR = 0.001
ADAM_B1 = 0.9
ADAM_B2 = 0.999
ADAM_EPS = 1e-08
ADAM_WD = 0.01
ADAM_STEP = 10
PER_EXAMPLE_BATCH_AXIS = {'x': 0, 'loss_target': 0}
SHARED_INPUTS = []
_WEIGHT_DTYPES = {'attn_norm_g': _jnp.float32, 'w_in': _jnp.float32, 'swa_sinks': _jnp.float32, 'q_a_norm_g': _jnp.float32, 'kv_a_norm_g': _jnp.float32, 'w_q_b': _jnp.float32, 'w_kv_b': _jnp.float32, 'w_out': _jnp.float32, 'final_norm_g': _jnp.float32}
MOMENT_SCALE = {'attn_norm_g': 2.609600e-02, 'w_in': 1.858586e-02, 'swa_sinks': 2.096256e-02, 'q_a_norm_g': 1.606421e-02, 'kv_a_norm_g': 3.036348e-02, 'w_q_b': 7.947457e-03, 'w_kv_b': 1.026613e-02, 'w_out': 1.453169e-02, 'final_norm_g': 1.598694e+01}


def _to_microbatches(a, axis):
    t = _jnp.moveaxis(a, axis, 0)
    t = t.reshape((N_MICROBATCH, t.shape[0] // N_MICROBATCH) + t.shape[1:])
    return _jnp.moveaxis(t, 1, axis + 1)


def setup_inputs(seed: int = 0) -> dict:
    inp = _fwd_setup_inputs(seed)
    key = _jax.random.fold_in(_jax.random.key(seed), 7919)
    shape, _ = _output_shape()
    out = dict(inp)
    out["loss_target"] = _jax.random.normal(_jax.random.fold_in(key, 0), shape, _jnp.float32)
    for i, name in enumerate(TWIN_WEIGHTS):
        w = inp[name].astype(_jnp.float32)
        if MOMENT_SCALE is None:
            s = _jnp.sqrt(_jnp.mean(_jnp.square(w)) + 1e-30)
        else:
            s = MOMENT_SCALE[name]
        km, kv = _jax.random.split(_jax.random.fold_in(key, i + 1))
        out[name] = w
        out["m_" + name] = s * _jax.random.normal(km, w.shape, _jnp.float32)
        out["v_" + name] = (s * s) * _jax.random.uniform(kv, w.shape, _jnp.float32, 0.5, 1.5)
    if N_MICROBATCH > 1:
        for name, axis in PER_EXAMPLE_BATCH_AXIS.items():
            out[name] = _to_microbatches(out[name], axis)
    return {'x': out['x'], 'attn_norm_g': out['attn_norm_g'], 'w_in': out['w_in'], 'swa_sinks': out['swa_sinks'], 'q_a_norm_g': out['q_a_norm_g'], 'kv_a_norm_g': out['kv_a_norm_g'], 'w_q_b': out['w_q_b'], 'w_kv_b': out['w_kv_b'], 'w_out': out['w_out'], 'final_norm_g': out['final_norm_g'], 'loss_target': out['loss_target'], 'm_attn_norm_g': out['m_attn_norm_g'], 'm_w_in': out['m_w_in'], 'm_swa_sinks': out['m_swa_sinks'], 'm_q_a_norm_g': out['m_q_a_norm_g'], 'm_kv_a_norm_g': out['m_kv_a_norm_g'], 'm_w_q_b': out['m_w_q_b'], 'm_w_kv_b': out['m_w_kv_b'], 'm_w_out': out['m_w_out'], 'm_final_norm_g': out['m_final_norm_g'], 'v_attn_norm_g': out['v_attn_norm_g'], 'v_w_in': out['v_w_in'], 'v_swa_sinks': out['v_swa_sinks'], 'v_q_a_norm_g': out['v_q_a_norm_g'], 'v_kv_a_norm_g': out['v_kv_a_norm_g'], 'v_w_q_b': out['v_w_q_b'], 'v_w_kv_b': out['v_w_kv_b'], 'v_w_out': out['v_w_out'], 'v_final_norm_g': out['v_final_norm_g']}


def _loss(weights, diff, rest, loss_target):
    with _jax.named_scope("forward"):
        args = {**rest, TWIN_DIFF_INPUT: diff, **{k: w.astype(_WEIGHT_DTYPES[k]) for k, w in weights.items()}}
        y = _forward(args)
    with _jax.named_scope("loss_head"):
        err = _jnp.square(y.astype(_jnp.float32) - loss_target)
        return 0.5 * _jnp.sum(_jnp.mean(err, axis=-1)) if err.ndim else 0.5 * err


def _adamw(w, g, m, v):
    m = ADAM_B1 * m + (1.0 - ADAM_B1) * g
    v = ADAM_B2 * v + (1.0 - ADAM_B2) * _jnp.square(g)
    m_hat = m / (1.0 - ADAM_B1 ** ADAM_STEP)
    v_hat = v / (1.0 - ADAM_B2 ** ADAM_STEP)
    delta = -ADAM_LR * (m_hat / (_jnp.sqrt(v_hat) + ADAM_EPS) + ADAM_WD * w)
    return delta, m, v


def reference(x, attn_norm_g, w_in, swa_sinks, q_a_norm_g, kv_a_norm_g, w_q_b, w_kv_b, w_out, final_norm_g, loss_target, m_attn_norm_g, m_w_in, m_swa_sinks, m_q_a_norm_g, m_kv_a_norm_g, m_w_q_b, m_w_kv_b, m_w_out, m_final_norm_g, v_attn_norm_g, v_w_in, v_swa_sinks, v_q_a_norm_g, v_kv_a_norm_g, v_w_q_b, v_w_kv_b, v_w_out, v_final_norm_g):
    given = dict(x=x, attn_norm_g=attn_norm_g, w_in=w_in, swa_sinks=swa_sinks, q_a_norm_g=q_a_norm_g, kv_a_norm_g=kv_a_norm_g, w_q_b=w_q_b, w_kv_b=w_kv_b, w_out=w_out, final_norm_g=final_norm_g, loss_target=loss_target, m_attn_norm_g=m_attn_norm_g, m_w_in=m_w_in, m_swa_sinks=m_swa_sinks, m_q_a_norm_g=m_q_a_norm_g, m_kv_a_norm_g=m_kv_a_norm_g, m_w_q_b=m_w_q_b, m_w_kv_b=m_w_kv_b, m_w_out=m_w_out, m_final_norm_g=m_final_norm_g, v_attn_norm_g=v_attn_norm_g, v_w_in=v_w_in, v_swa_sinks=v_swa_sinks, v_q_a_norm_g=v_q_a_norm_g, v_kv_a_norm_g=v_kv_a_norm_g, v_w_q_b=v_w_q_b, v_w_kv_b=v_w_kv_b, v_w_out=v_w_out, v_final_norm_g=v_final_norm_g)
    weights = {n: given[n] for n in TWIN_WEIGHTS}
    shared = {n: given[n] for n in SHARED_INPUTS}
    per_example = {n: given[n] for n in ['x']}
    grad_fn = _jax.value_and_grad(_loss, argnums=(0, 1))

    def one_microbatch(ex, loss_target):
        ex = dict(ex)
        diff = ex.pop(TWIN_DIFF_INPUT)
        return grad_fn(weights, diff, {**shared, **ex}, loss_target)

    if N_MICROBATCH == 1:
        loss, (grad_w, grad_x) = one_microbatch(per_example, given["loss_target"])
    else:
        def body(carry, xs):
            loss_sum, grad_sum = carry
            l_k, (gw_k, gx_k) = one_microbatch(xs[0], xs[1])
            with _jax.named_scope("update"):
                return (loss_sum + l_k, _jax.tree.map(_jnp.add, grad_sum, gw_k)), gx_k

        init = (_jnp.zeros((), _jnp.float32), _jax.tree.map(_jnp.zeros_like, weights))
        (loss, grad_w), grad_x = _jax.lax.scan(body, init, (per_example, given["loss_target"]))
    with _jax.named_scope("update"):
        delta_w, new_m, new_v = {}, {}, {}
        for n in TWIN_WEIGHTS:
            delta_w[n], new_m[n], new_v[n] = _adamw(weights[n], grad_w[n], given["m_" + n], given["v_" + n])
    return (loss, grad_x, *[grad_w[n] for n in TWIN_WEIGHTS], *[delta_w[n] for n in TWIN_WEIGHTS],
            *[new_m[n] for n in TWIN_WEIGHTS], *[new_v[n] for n in TWIN_WEIGHTS])
```

```python
import functools
import math

import jax
import jax.numpy as jnp
from jax import lax
from jax.experimental import pallas as pl
from jax.experimental.pallas import tpu as pltpu

F32 = jnp.float32
MXU_DTYPE = jnp.bfloat16
WIRE_DTYPE = jnp.bfloat16

EPS = 1e-6
NEG = -1e30
BLOCK = 128
D_MODEL = 2048
SWA_HEADS = 16
MLA_HEADS = 8
Q_RANK = 384
KV_RANK = 256
IN_WIDTH = 4032
MLA_SCALE = 192 ** -0.5
SWA_SCALE = 64 ** -0.5
SLOPES = tuple(2.0 ** (-8.0 * (h + 1) / SWA_HEADS) for h in range(SWA_HEADS))

P_WIDTH = 4096
QA_OFF, GA_OFF, GB_OFF, CQ_OFF, KA_OFF, CKV_OFF, VA_OFF, KR_OFF = 0, 1024, 2048, 3072, 3456, 3584, 3840, 3968

ADAM_LR, ADAM_B1, ADAM_B2, ADAM_EPS, ADAM_WD, ADAM_STEP = 0.001, 0.9, 0.999, 1e-08, 0.01, 10

LANES = 1024
VMEM_LIMIT = 56 * 1024 * 1024
MESH = pl.DeviceIdType.MESH
ANY = pl.BlockSpec(memory_space=pl.ANY)


def _cparams(*sem):
    return pltpu.CompilerParams(dimension_semantics=sem, vmem_limit_bytes=VMEM_LIMIT)


def _dot(a, b, ca, cb):
    return lax.dot_general(a, b, (((ca,), (cb,)), ((), ())), preferred_element_type=F32)


def _matmul(a, b, *, name, ta=False, tb=False, out_dtype=F32, add=None, tm=1024, tn=1024, tk=1024):
    (kdim, m) = a.shape if ta else a.shape[::-1]
    (n, k2) = b.shape if tb else b.shape[::-1]
    assert kdim == k2, (a.shape, b.shape)
    tm, tn, tk = min(tm, m), min(tn, n), min(tk, kdim)
    assert m % tm == 0 and n % tn == 0 and kdim % tk == 0
    nk = kdim // tk

    def body(*refs):
        if add is None:
            a_ref, b_ref, o_ref, acc = refs
        else:
            a_ref, b_ref, add_ref, o_ref, acc = refs
        k = pl.program_id(2)

        @pl.when(k == 0)
        def _():
            acc[...] = jnp.zeros_like(acc)

        acc[...] += _dot(a_ref[...].astype(MXU_DTYPE), b_ref[...].astype(MXU_DTYPE), 0 if ta else 1, 1 if tb else 0)

        @pl.when(k == nk - 1)
        def _():
            r = acc[...]
            if add is not None:
                r = add_ref[...] + r
            o_ref[...] = r.astype(out_dtype)

    a_spec = pl.BlockSpec((tk, tm), lambda i, j, k: (k, i)) if ta else pl.BlockSpec((tm, tk), lambda i, j, k: (i, k))
    b_spec = pl.BlockSpec((tn, tk), lambda i, j, k: (j, k)) if tb else pl.BlockSpec((tk, tn), lambda i, j, k: (k, j))
    in_specs, args = [a_spec, b_spec], [a, b]
    if add is not None:
        in_specs.append(pl.BlockSpec((tm, tn), lambda i, j, k: (i, j)))
        args.append(add)
    return pl.pallas_call(
        body, name=name, grid=(m // tm, n // tn, nk), in_specs=in_specs,
        out_specs=pl.BlockSpec((tm, tn), lambda i, j, k: (i, j)),
        out_shape=jax.ShapeDtypeStruct((m, n), out_dtype),
        scratch_shapes=[pltpu.VMEM((tm, tn), F32)],
        compiler_params=_cparams("parallel", "parallel", "arbitrary"),
    )(*args)


def _rmsnorm_fwd(x, g, *, name):
    s, d = x.shape
    tm = min(512, s)

    def body(x_ref, g_ref, h_ref):
        xv = x_ref[...]
        r = lax.rsqrt(jnp.mean(xv * xv, axis=-1, keepdims=True) + EPS)
        h_ref[...] = (xv * r * g_ref[...]).astype(MXU_DTYPE)

    return pl.pallas_call(
        body, name=name, grid=(s // tm,),
        in_specs=[pl.BlockSpec((tm, d), lambda i: (i, 0)), pl.BlockSpec((1, d), lambda i: (0, 0))],
        out_specs=pl.BlockSpec((tm, d), lambda i: (i, 0)),
        out_shape=jax.ShapeDtypeStruct((s, d), MXU_DTYPE),
        compiler_params=_cparams("parallel"),
    )(x, g)


def _rmsnorm_bwd(dh, x, g, dres, *, name):
    s, d = x.shape
    tm = min(512, s)

    def body(dh_ref, x_ref, g_ref, dres_ref, dx_ref, dg_ref):
        @pl.when(pl.program_id(0) == 0)
        def _():
            dg_ref[...] = jnp.zeros_like(dg_ref)

        xv = x_ref[...]
        r = lax.rsqrt(jnp.mean(xv * xv, axis=-1, keepdims=True) + EPS)
        xn = xv * r
        dy = dh_ref[...]
        dg_ref[...] += jnp.sum(dy * xn, axis=0, keepdims=True)
        u = dy * g_ref[...]
        dx_ref[...] = dres_ref[...] + r * (u - xn * jnp.mean(u * xn, axis=-1, keepdims=True))

    row = pl.BlockSpec((tm, d), lambda i: (i, 0))
    vec = pl.BlockSpec((1, d), lambda i: (0, 0))
    return pl.pallas_call(
        body, name=name, grid=(s // tm,), in_specs=[row, row, vec, row], out_specs=[row, vec],
        out_shape=[jax.ShapeDtypeStruct((s, d), F32), jax.ShapeDtypeStruct((1, d), F32)],
        compiler_params=_cparams("arbitrary"),
    )(dh, x, g, dres)


def _final_loss(x, g, tgt, *, name):
    s, d = x.shape
    tm = min(512, s)

    def body(x_ref, g_ref, t_ref, dx_ref, dg_ref, loss_ref):
        @pl.when(pl.program_id(0) == 0)
        def _():
            dg_ref[...] = jnp.zeros_like(dg_ref)
            loss_ref[...] = jnp.zeros_like(loss_ref)

        xv = x_ref[...]
        gv = g_ref[...]
        r = lax.rsqrt(jnp.mean(xv * xv, axis=-1, keepdims=True) + EPS)
        xn = xv * r
        err = xn * gv - t_ref[...]
        sq = jnp.sum(jnp.sum(err * err, axis=-1, keepdims=True), axis=0, keepdims=True)
        loss_ref[...] += (0.5 / d) * sq
        dy = err * (1.0 / d)
        dg_ref[...] += jnp.sum(dy * xn, axis=0, keepdims=True)
        u = dy * gv
        dx_ref[...] = r * (u - xn * jnp.mean(u * xn, axis=-1, keepdims=True))

    row = pl.BlockSpec((tm, d), lambda i: (i, 0))
    vec = pl.BlockSpec((1, d), lambda i: (0, 0))
    return pl.pallas_call(
        body, name=name, grid=(s // tm,), in_specs=[row, vec, row],
        out_specs=[row, vec, pl.BlockSpec((1, 128), lambda i: (0, 0))],
        out_shape=[jax.ShapeDtypeStruct((s, d), F32), jax.ShapeDtypeStruct((1, d), F32),
                   jax.ShapeDtypeStruct((1, 128), F32)],
        compiler_params=_cparams("arbitrary"),
    )(x, g, tgt)


def _swa_keys(kp_ref, kc_ref):
    kk = jnp.concatenate([kp_ref[...], kc_ref[...]], axis=0)
    kr = pltpu.roll(kk, 64, 1)
    lo = lax.broadcasted_iota(jnp.int32, kk.shape, 1) < 64
    return [jnp.where(lo, kk, kr).astype(MXU_DTYPE), jnp.where(lo, kr, kk).astype(MXU_DTYPE)]


def _swa_mask(n):
    qi = lax.broadcasted_iota(jnp.int32, (BLOCK, 2 * BLOCK), 0)
    ki = lax.broadcasted_iota(jnp.int32, (BLOCK, 2 * BLOCK), 1)
    delta = BLOCK + qi - ki
    valid = (delta >= 0) & (delta < BLOCK) & ((ki >= BLOCK) | (n > 0))
    return valid, delta.astype(F32)


def _swa_scores(qm, keys, h, valid, deltaf):
    s = _dot(qm, keys, 1, 1) * SWA_SCALE
    return jnp.where(valid, s - SLOPES[h] * deltaf, NEG)


def _swa_specs(nb):
    kcol, vcol = KA_OFF // BLOCK, VA_OFF // BLOCK
    last = nb - 1
    cur = lambda n: jnp.minimum(n, last)
    prev = lambda n: jnp.maximum(jnp.minimum(n, last) - 1, 0)
    return [
        pl.BlockSpec(memory_space=pltpu.SMEM),
        pl.BlockSpec((BLOCK, 1024), lambda n: (cur(n), QA_OFF // 1024)),
        pl.BlockSpec((BLOCK, BLOCK), lambda n: (cur(n), kcol)),
        pl.BlockSpec((BLOCK, BLOCK), lambda n: (prev(n), kcol)),
        pl.BlockSpec((BLOCK, BLOCK), lambda n: (cur(n), vcol)),
        pl.BlockSpec((BLOCK, BLOCK), lambda n: (prev(n), vcol)),
    ]


def _swa_fwd(proj, sinks, *, name):
    s = proj.shape[0]
    nb = s // BLOCK

    def body(sink_ref, q_ref, kc_ref, kp_ref, vc_ref, vp_ref, o_ref, lse_ref):
        n = pl.program_id(0)
        keys = _swa_keys(kp_ref, kc_ref)
        vals = _swa_keys(vp_ref, vc_ref)
        valid, deltaf = _swa_mask(n)
        lane = lax.broadcasted_iota(jnp.int32, (BLOCK, BLOCK), 1)
        lo = lane < 64
        lse_acc = jnp.zeros((BLOCK, BLOCK), F32)
        for i in range(SWA_HEADS // 2):
            j = i // 4
            qp = q_ref[:, i * 128:(i + 1) * 128]
            outs = []
            for half in range(2):
                h = 2 * i + half
                qm = jnp.where(lo if half == 0 else ~lo, qp, 0.0).astype(MXU_DTYPE)
                sc = _swa_scores(qm, keys[j], h, valid, deltaf)
                sink = sink_ref[h]
                m = jnp.maximum(jnp.max(sc, axis=-1, keepdims=True), sink)
                p = jnp.exp(sc - m)
                l = jnp.sum(p, axis=-1, keepdims=True) + jnp.exp(sink - m)
                outs.append(jnp.dot((p / l).astype(MXU_DTYPE), vals[j], preferred_element_type=F32))
                lse_acc = jnp.where(lane == h, m + jnp.log(l), lse_acc)
            o_ref[:, i * 128:(i + 1) * 128] = jnp.where(lo, outs[0], outs[1])
        lse_ref[...] = lse_acc

    return pl.pallas_call(
        body, name=name, grid=(nb,), in_specs=_swa_specs(nb),
        out_specs=[pl.BlockSpec((BLOCK, 1024), lambda n: (n, 0)), pl.BlockSpec((BLOCK, BLOCK), lambda n: (n, 0))],
        out_shape=[jax.ShapeDtypeStruct((s, 1024), F32), jax.ShapeDtypeStruct((s, BLOCK), F32)],
        compiler_params=_cparams("parallel"),
    )(sinks, proj, proj, proj, proj, proj)


def _swa_bwd(proj, sinks, lse, do, *, name):
    s = proj.shape[0]
    nb = s // BLOCK
    last = nb - 1

    def body(sink_ref, q_ref, kc_ref, kp_ref, vc_ref, vp_ref, lse_ref, do_ref,
             dq_ref, dk_ref, dv_ref, dsink_ref, carry_k, carry_v):
        n = pl.program_id(0)

        @pl.when(n == 0)
        def _():
            carry_k[...] = jnp.zeros_like(carry_k)
            carry_v[...] = jnp.zeros_like(carry_v)
            dsink_ref[...] = jnp.zeros_like(dsink_ref)

        @pl.when(n < nb)
        def _():
            keys = _swa_keys(kp_ref, kc_ref)
            vals = _swa_keys(vp_ref, vc_ref)
            valid, deltaf = _swa_mask(n)
            lane = lax.broadcasted_iota(jnp.int32, (BLOCK, BLOCK), 1)
            lane1 = lax.broadcasted_iota(jnp.int32, (1, BLOCK), 1)
            lo = lane < 64
            lse_blk = lse_ref[...]
            acc_k = [jnp.zeros((2 * BLOCK, BLOCK), F32) for _ in range(2)]
            acc_v = [jnp.zeros((2 * BLOCK, BLOCK), F32) for _ in range(2)]
            dsink = jnp.zeros((1, BLOCK), F32)
            for i in range(SWA_HEADS // 2):
                j = i // 4
                qp = q_ref[:, i * 128:(i + 1) * 128]
                dop = do_ref[:, i * 128:(i + 1) * 128].astype(F32)
                dqs = []
                for half in range(2):
                    h = 2 * i + half
                    msk = lo if half == 0 else ~lo
                    qm = jnp.where(msk, qp, 0.0).astype(MXU_DTYPE)
                    dom = jnp.where(msk, dop, 0.0).astype(MXU_DTYPE)
                    sc = _swa_scores(qm, keys[j], h, valid, deltaf)
                    lse_h = jnp.sum(jnp.where(lane == h, lse_blk, 0.0), axis=-1, keepdims=True)
                    p = jnp.exp(sc - lse_h)
                    dp = _dot(dom, vals[j], 1, 1)
                    dlt = jnp.sum(dp * p, axis=-1, keepdims=True)
                    ds = (p * (dp - dlt) * SWA_SCALE).astype(MXU_DTYPE)
                    psink = jnp.exp(sink_ref[h] - lse_h)
                    dsink = jnp.where(lane1 == h, -jnp.sum(psink * dlt, axis=0, keepdims=True), dsink)
                    dqs.append(jnp.dot(ds, keys[j], preferred_element_type=F32))
                    acc_k[j] += _dot(ds, qm, 0, 0)
                    acc_v[j] += _dot(p.astype(MXU_DTYPE), dom, 0, 0)
                dq_ref[:, i * 128:(i + 1) * 128] = jnp.where(lo, dqs[0], dqs[1]).astype(dq_ref.dtype)
            lo2 = lax.broadcasted_iota(jnp.int32, (2 * BLOCK, BLOCK), 1) < 64
            fold = lambda acc: jnp.where(lo2, acc[0] + pltpu.roll(acc[0], 64, 1), acc[1] + pltpu.roll(acc[1], 64, 1))
            dkk, dvv = fold(acc_k), fold(acc_v)
            dk_ref[...] = (carry_k[...] + dkk[:BLOCK]).astype(dk_ref.dtype)
            dv_ref[...] = (carry_v[...] + dvv[:BLOCK]).astype(dv_ref.dtype)
            carry_k[...] = dkk[BLOCK:]
            carry_v[...] = dvv[BLOCK:]
            dsink_ref[...] += jnp.broadcast_to(dsink, dsink_ref.shape)

        @pl.when(n == nb)
        def _():
            dk_ref[...] = carry_k[...].astype(dk_ref.dtype)
            dv_ref[...] = carry_v[...].astype(dv_ref.dtype)

    cur = lambda n: jnp.minimum(n, last)
    lag = lambda n: jnp.maximum(n - 1, 0)
    return pl.pallas_call(
        body, name=name, grid=(nb + 1,),
        in_specs=_swa_specs(nb) + [pl.BlockSpec((BLOCK, BLOCK), lambda n: (cur(n), 0)),
                                   pl.BlockSpec((BLOCK, 1024), lambda n: (cur(n), 0))],
        out_specs=[pl.BlockSpec((BLOCK, 1024), lambda n: (cur(n), 0)),
                   pl.BlockSpec((BLOCK, BLOCK), lambda n: (lag(n), 0)),
                   pl.BlockSpec((BLOCK, BLOCK), lambda n: (lag(n), 0)),
                   pl.BlockSpec((8, BLOCK), lambda n: (0, 0))],
        out_shape=[jax.ShapeDtypeStruct((s, 1024), MXU_DTYPE), jax.ShapeDtypeStruct((s, BLOCK), MXU_DTYPE),
                   jax.ShapeDtypeStruct((s, BLOCK), MXU_DTYPE), jax.ShapeDtypeStruct((8, BLOCK), F32)],
        scratch_shapes=[pltpu.VMEM((BLOCK, BLOCK), F32), pltpu.VMEM((BLOCK, BLOCK), F32)],
        compiler_params=_cparams("arbitrary"),
    )(sinks, proj, proj, proj, proj, proj, lse, do)


def _rope_partner(v, first, width):
    lane = lax.broadcasted_iota(jnp.int32, v.shape, 1)
    in_a = (lane >= first) & (lane < first + 32)
    in_b = (lane >= first + 32) & (lane < first + 64)
    return jnp.where(in_a, pltpu.roll(v, width - 32, 1), jnp.where(in_b, pltpu.roll(v, 32, 1), 0.0))


def _mla_qkv_fwd(proj, gq, gkv, wq, wkv, tq_c, tq_s, tk_c, tk_s, *, name):
    s = proj.shape[0]
    tm = min(256, s)

    def body(cq_ref, ckv_ref, kr_ref, gq_ref, gkv_ref, wq_ref, wkv_ref, qc_ref, qs_ref, kc_ref, ks_ref,
             qcat_ref, kcat_ref, v_ref, cqn_ref, ckvn_ref):
        cq = cq_ref[...]
        cqn = (cq * lax.rsqrt(jnp.mean(cq * cq, axis=-1, keepdims=True) + EPS) * gq_ref[...]).astype(MXU_DTYPE)
        cqn_ref[...] = cqn
        qpre = jnp.dot(cqn, wq_ref[...], preferred_element_type=F32)
        qc, qs = qc_ref[...], qs_ref[...]
        for hh in range(MLA_HEADS):
            blk = qpre[:, hh * 256:(hh + 1) * 256]
            qcat_ref[:, hh * 256:(hh + 1) * 256] = (blk * qc + _rope_partner(blk, 128, 256) * qs).astype(MXU_DTYPE)
        ckv = ckv_ref[...]
        ckvn = (ckv * lax.rsqrt(jnp.mean(ckv * ckv, axis=-1, keepdims=True) + EPS) * gkv_ref[...]).astype(MXU_DTYPE)
        ckvn_ref[...] = ckvn
        kv = jnp.dot(ckvn, wkv_ref[...], preferred_element_type=F32)
        v_ref[...] = kv[:, 1024:].astype(MXU_DTYPE)
        kr = kr_ref[...]
        krr = (kr * kc_ref[...] + _rope_partner(kr, 0, 128) * ks_ref[...]).astype(MXU_DTYPE)
        for hh in range(MLA_HEADS):
            kcat_ref[:, hh * 256:hh * 256 + 128] = kv[:, hh * 128:(hh + 1) * 128].astype(MXU_DTYPE)
            kcat_ref[:, hh * 256 + 128:(hh + 1) * 256] = krr

    row = lambda w, c: pl.BlockSpec((tm, w), lambda i: (i, c))
    full = lambda a: pl.BlockSpec(a.shape, lambda i: (0, 0))
    return pl.pallas_call(
        body, name=name, grid=(s // tm,),
        in_specs=[row(Q_RANK, CQ_OFF // Q_RANK), row(KV_RANK, CKV_OFF // KV_RANK), row(128, KR_OFF // 128),
                  full(gq), full(gkv), full(wq), full(wkv), row(256, 0), row(256, 0), row(128, 0), row(128, 0)],
        out_specs=[row(2048, 0), row(2048, 0), row(1024, 0), row(Q_RANK, 0), row(KV_RANK, 0)],
        out_shape=[jax.ShapeDtypeStruct((s, 2048), MXU_DTYPE), jax.ShapeDtypeStruct((s, 2048), MXU_DTYPE),
                   jax.ShapeDtypeStruct((s, 1024), MXU_DTYPE), jax.ShapeDtypeStruct((s, Q_RANK), MXU_DTYPE),
                   jax.ShapeDtypeStruct((s, KV_RANK), MXU_DTYPE)],
        compiler_params=_cparams("parallel"),
    )(proj, proj, proj, gq, gkv, wq, wkv, tq_c, tq_s, tk_c, tk_s)


def _norm_bwd(x, g, dy):
    r = lax.rsqrt(jnp.mean(x * x, axis=-1, keepdims=True) + EPS)
    xn = x * r
    u = dy * g
    return r * (u - xn * jnp.mean(u * xn, axis=-1, keepdims=True)), jnp.sum(dy * xn, axis=0, keepdims=True)


def _mla_qkv_bwd(proj, cqn, ckvn, dqcat, dkcat, dv, gq, gkv, wq, wkv, tq_c, tq_s, tk_c, tk_s, *, name):
    s = proj.shape[0]
    tm = min(256, s)

    def body(cq_ref, ckv_ref, cqn_ref, ckvn_ref, dq_ref, dk_ref, dv_ref, gq_ref, gkv_ref, wq_ref, wkv_ref,
             qc_ref, qs_ref, kc_ref, ks_ref,
             dcq_ref, dckv_ref, dkr_ref, dwq_ref, dwkv_ref, dgq_ref, dgkv_ref, dqpre, dkv):
        @pl.when(pl.program_id(0) == 0)
        def _():
            for r in (dwq_ref, dwkv_ref, dgq_ref, dgkv_ref):
                r[...] = jnp.zeros_like(r)

        qc, qs = qc_ref[...], qs_ref[...]
        dkrr = jnp.zeros((tm, 128), F32)
        for hh in range(MLA_HEADS):
            blk = dq_ref[:, hh * 256:(hh + 1) * 256]
            dqpre[:, hh * 256:(hh + 1) * 256] = (blk * qc + _rope_partner(blk * qs, 128, 256)).astype(MXU_DTYPE)
            dkv[:, hh * 128:(hh + 1) * 128] = dk_ref[:, hh * 256:hh * 256 + 128].astype(MXU_DTYPE)
            dkrr = dkrr + dk_ref[:, hh * 256 + 128:(hh + 1) * 256]
        dkv[:, 1024:] = dv_ref[...].astype(MXU_DTYPE)
        dkr_ref[...] = (dkrr * kc_ref[...] + _rope_partner(dkrr * ks_ref[...], 0, 128)).astype(dkr_ref.dtype)

        dq_b = dqpre[...]
        dwq_ref[...] += _dot(cqn_ref[...], dq_b, 0, 0)
        dcq, dgq = _norm_bwd(cq_ref[...], gq_ref[...], _dot(dq_b, wq_ref[...], 1, 1))
        dcq_ref[...] = dcq.astype(dcq_ref.dtype)
        dgq_ref[...] += dgq

        dkv_b = dkv[...]
        dwkv_ref[...] += _dot(ckvn_ref[...], dkv_b, 0, 0)
        dckv, dgkv = _norm_bwd(ckv_ref[...], gkv_ref[...], _dot(dkv_b, wkv_ref[...], 1, 1))
        dckv_ref[...] = dckv.astype(dckv_ref.dtype)
        dgkv_ref[...] += dgkv

    row = lambda w, c: pl.BlockSpec((tm, w), lambda i: (i, c))
    full = lambda shape: pl.BlockSpec(shape, lambda i: (0, 0))
    return pl.pallas_call(
        body, name=name, grid=(s // tm,),
        in_specs=[row(Q_RANK, CQ_OFF // Q_RANK), row(KV_RANK, CKV_OFF // KV_RANK), row(Q_RANK, 0), row(KV_RANK, 0),
                  row(2048, 0), row(2048, 0), row(1024, 0), full(gq.shape), full(gkv.shape), full(wq.shape),
                  full(wkv.shape), row(256, 0), row(256, 0), row(128, 0), row(128, 0)],
        out_specs=[row(Q_RANK, 0), row(KV_RANK, 0), row(128, 0), full(wq.shape), full(wkv.shape),
                   full(gq.shape), full(gkv.shape)],
        out_shape=[jax.ShapeDtypeStruct((s, Q_RANK), MXU_DTYPE), jax.ShapeDtypeStruct((s, KV_RANK), MXU_DTYPE),
                   jax.ShapeDtypeStruct((s, 128), MXU_DTYPE), jax.ShapeDtypeStruct(wq.shape, F32),
                   jax.ShapeDtypeStruct(wkv.shape, F32), jax.ShapeDtypeStruct(gq.shape, F32),
                   jax.ShapeDtypeStruct(gkv.shape, F32)],
        scratch_shapes=[pltpu.VMEM((tm, 2048), MXU_DTYPE), pltpu.VMEM((tm, 2048), MXU_DTYPE)],
        compiler_params=_cparams("arbitrary"),
    )(proj, proj, cqn, ckvn, dqcat, dkcat, dv, gq, gkv, wq, wkv, tq_c, tq_s, tk_c, tk_s)


def _causal_mask(t):
    return lax.broadcasted_iota(jnp.int32, (t, t), 1) <= lax.broadcasted_iota(jnp.int32, (t, t), 0)


def _mla_fwd(qcat, kcat, v, *, name):
    s = qcat.shape[0]
    t = min(512, s)
    nq = s // t

    def body(q_ref, k_ref, v_ref, o_ref, lse_ref, m_s, l_s, acc_s):
        h, ki, qi = pl.program_id(1), pl.program_id(2), pl.program_id(0)

        @pl.when(ki == 0)
        def _():
            m_s[...] = jnp.full_like(m_s, NEG)
            l_s[...] = jnp.zeros_like(l_s)
            acc_s[...] = jnp.zeros_like(acc_s)

        @pl.when((ki == 0) & (h == 0))
        def _():
            lse_ref[...] = jnp.zeros_like(lse_ref)

        def step(masked):
            sc = _dot(q_ref[...], k_ref[...], 1, 1) * MLA_SCALE
            if masked:
                sc = jnp.where(_causal_mask(t), sc, NEG)
            m_prev = m_s[...]
            m_new = jnp.maximum(m_prev, jnp.max(sc, axis=-1, keepdims=True))
            alpha = jnp.exp(m_prev - m_new)
            p = jnp.exp(sc - m_new)
            l_s[...] = alpha * l_s[...] + jnp.sum(p, axis=-1, keepdims=True)
            acc_s[...] = alpha * acc_s[...] + jnp.dot(p.astype(MXU_DTYPE), v_ref[...], preferred_element_type=F32)
            m_s[...] = m_new

        @pl.when(ki < qi)
        def _():
            step(False)

        @pl.when(ki == qi)
        def _():
            step(True)
            l = l_s[...]
            o_ref[...] = acc_s[...] / l
            lane = lax.broadcasted_iota(jnp.int32, (t, 128), 1)
            lse_ref[...] = jnp.where(lane == h, m_s[...] + jnp.log(l), lse_ref[...])

    kmap = lambda qi, h, ki: (jnp.minimum(ki, qi), h)
    return pl.pallas_call(
        body, name=name, grid=(nq, MLA_HEADS, nq),
        in_specs=[pl.BlockSpec((t, 256), lambda qi, h, ki: (qi, h)), pl.BlockSpec((t, 256), kmap),
                  pl.BlockSpec((t, 128), kmap)],
        out_specs=[pl.BlockSpec((t, 128), lambda qi, h, ki: (qi, h)), pl.BlockSpec((t, 128), lambda qi, h, ki: (qi, 0))],
        out_shape=[jax.ShapeDtypeStruct((s, 1024), F32), jax.ShapeDtypeStruct((s, 128), F32)],
        scratch_shapes=[pltpu.VMEM((t, 1), F32), pltpu.VMEM((t, 1), F32), pltpu.VMEM((t, 128), F32)],
        compiler_params=_cparams("parallel", "arbitrary", "arbitrary"),
    )(qcat, kcat, v)


def _mla_bwd(qcat, kcat, v, do, lse, delta, *, name):
    s = qcat.shape[0]
    t = min(512, s)
    nq = s // t

    def body(q_ref, k_ref, v_ref, do_ref, lse_ref, dl_ref, dq_ref, dk_ref, dv_ref, dk_acc, dv_acc):
        h, ki, qi = pl.program_id(0), pl.program_id(1), pl.program_id(2)

        @pl.when((ki == 0) & (qi == 0))
        def _():
            dq_ref[...] = jnp.zeros_like(dq_ref)

        @pl.when(qi == 0)
        def _():
            dk_acc[...] = jnp.zeros_like(dk_acc)
            dv_acc[...] = jnp.zeros_like(dv_acc)

        def step(masked):
            q, k, dob = q_ref[...], k_ref[...], do_ref[...]
            lane = lax.broadcasted_iota(jnp.int32, (t, 128), 1)
            pick = lambda r: jnp.sum(jnp.where(lane == h, r[...], 0.0), axis=-1, keepdims=True)
            sc = _dot(q, k, 1, 1) * MLA_SCALE
            if masked:
                sc = jnp.where(_causal_mask(t), sc, NEG)
            p = jnp.exp(sc - pick(lse_ref))
            dp = _dot(dob, v_ref[...], 1, 1)
            ds = (p * (dp - pick(dl_ref)) * MLA_SCALE).astype(MXU_DTYPE)
            dv_acc[...] += _dot(p.astype(MXU_DTYPE), dob, 0, 0)
            dk_acc[...] += _dot(ds, q, 0, 0)
            rows = pl.ds(pl.multiple_of(qi * t, t), t)
            dq_ref[rows, :] += jnp.dot(ds, k, preferred_element_type=F32)

        @pl.when(qi > ki)
        def _():
            step(False)

        @pl.when(qi == ki)
        def _():
            step(True)

        @pl.when(qi == nq - 1)
        def _():
            dk_ref[...] = dk_acc[...]
            dv_ref[...] = dv_acc[...]

    qmap = lambda h, ki, qi: (jnp.maximum(qi, ki), h)
    q0map = lambda h, ki, qi: (jnp.maximum(qi, ki), 0)
    kmap = lambda h, ki, qi: (ki, h)
    return pl.pallas_call(
        body, name=name, grid=(MLA_HEADS, nq, nq),
        in_specs=[pl.BlockSpec((t, 256), qmap), pl.BlockSpec((t, 256), kmap), pl.BlockSpec((t, 128), kmap),
                  pl.BlockSpec((t, 128), qmap), pl.BlockSpec((t, 128), q0map), pl.BlockSpec((t, 128), q0map)],
        out_specs=[pl.BlockSpec((s, 256), lambda h, ki, qi: (0, h)), pl.BlockSpec((t, 256), kmap),
                   pl.BlockSpec((t, 128), kmap)],
        out_shape=[jax.ShapeDtypeStruct((s, 2048), F32), jax.ShapeDtypeStruct((s, 2048), F32),
                   jax.ShapeDtypeStruct((s, 1024), F32)],
        scratch_shapes=[pltpu.VMEM((t, 256), F32), pltpu.VMEM((t, 128), F32)],
        compiler_params=_cparams("parallel", "arbitrary", "arbitrary"),
    )(qcat, kcat, v, do, lse, delta)


def _gate_specs(tm):
    half = lambda c: pl.BlockSpec((tm, 1024), lambda i: (i, c))
    return half(0), half(GA_OFF // 1024), half(GB_OFF // 1024)


def _gate_fwd(oa, ob, proj, *, name):
    s = oa.shape[0]
    tm = min(512, s)

    def body(oa_ref, ob_ref, ga_ref, gb_ref, y_ref):
        ga, gb = ga_ref[...], gb_ref[...]
        y_ref[:, :1024] = (oa_ref[...] * (ga * jax.nn.sigmoid(ga))).astype(MXU_DTYPE)
        y_ref[:, 1024:] = (ob_ref[...] * (gb * jax.nn.sigmoid(gb))).astype(MXU_DTYPE)

    o_spec, ga_spec, gb_spec = _gate_specs(tm)
    return pl.pallas_call(
        body, name=name, grid=(s // tm,), in_specs=[o_spec, o_spec, ga_spec, gb_spec],
        out_specs=pl.BlockSpec((tm, 2048), lambda i: (i, 0)),
        out_shape=jax.ShapeDtypeStruct((s, 2048), MXU_DTYPE),
        compiler_params=_cparams("parallel"),
    )(oa, ob, proj, proj)


def _gate_bwd(dy, oa, ob, proj, *, name):
    s = oa.shape[0]
    tm = min(512, s)

    def body(dy_ref, oa_ref, ob_ref, ga_ref, gb_ref, doa_ref, dob_ref, dga_ref, dgb_ref, dl_ref):
        def branch(dyv, o, g, do_ref, dg_ref):
            sg = jax.nn.sigmoid(g)
            do = dyv * (g * sg)
            do_ref[...] = do.astype(MXU_DTYPE)
            dg_ref[...] = (dyv * o * (sg * (1.0 + g * (1.0 - sg)))).astype(MXU_DTYPE)
            return do

        branch(dy_ref[:, :1024], oa_ref[...], ga_ref[...], doa_ref, dga_ref)
        ob = ob_ref[...]
        prod = branch(dy_ref[:, 1024:], ob, gb_ref[...], dob_ref, dgb_ref) * ob
        lane = lax.broadcasted_iota(jnp.int32, (tm, 128), 1)
        acc = jnp.zeros((tm, 128), F32)
        for hh in range(MLA_HEADS):
            acc = jnp.where(lane == hh, jnp.sum(prod[:, hh * 128:(hh + 1) * 128], axis=-1, keepdims=True), acc)
        dl_ref[...] = acc

    o_spec, ga_spec, gb_spec = _gate_specs(tm)
    return pl.pallas_call(
        body, name=name, grid=(s // tm,),
        in_specs=[pl.BlockSpec((tm, 2048), lambda i: (i, 0)), o_spec, o_spec, ga_spec, gb_spec],
        out_specs=[o_spec, o_spec, o_spec, o_spec, pl.BlockSpec((tm, 128), lambda i: (i, 0))],
        out_shape=[jax.ShapeDtypeStruct((s, 1024), MXU_DTYPE)] * 4 + [jax.ShapeDtypeStruct((s, 128), F32)],
        compiler_params=_cparams("parallel"),
    )(dy, oa, ob, proj, proj)


def _row_block(rows, cols, itemsize=4, budget=1 << 20):
    for tr in (2048, 1024, 512, 256, 128, 64, 32, 16, 8):
        if rows % tr == 0 and tr * cols * itemsize <= budget:
            return tr
    return rows


def _adamw(w, g, m, v, *, name):
    shape = w.shape
    cols = shape[-1]
    w2, g2, m2, v2 = (a.reshape(-1, cols) for a in (w, g, m, v))
    rows = w2.shape[0]
    tr = _row_block(rows, cols)

    def body(w_ref, g_ref, m_ref, v_ref, d_ref, mo_ref, vo_ref):
        gv = g_ref[...]
        mn = ADAM_B1 * m_ref[...] + (1.0 - ADAM_B1) * gv
        vn = ADAM_B2 * v_ref[...] + (1.0 - ADAM_B2) * jnp.square(gv)
        m_hat = mn / (1.0 - ADAM_B1 ** ADAM_STEP)
        v_hat = vn / (1.0 - ADAM_B2 ** ADAM_STEP)
        d_ref[...] = -ADAM_LR * (m_hat / (jnp.sqrt(v_hat) + ADAM_EPS) + ADAM_WD * w_ref[...])
        mo_ref[...] = mn
        vo_ref[...] = vn

    spec = pl.BlockSpec((tr, cols), lambda i: (i, 0))
    outs = pl.pallas_call(
        body, name=name, grid=(rows // tr,), in_specs=[spec] * 4, out_specs=[spec] * 3,
        out_shape=[jax.ShapeDtypeStruct((rows, cols), F32)] * 3,
        compiler_params=_cparams("parallel"),
    )(w2, g2, m2, v2)
    return tuple(o.reshape(shape) for o in outs)


def _pair_sum(keep, recv, *, name):
    n, rows, cols = keep.shape
    tr = _row_block(rows, cols)

    def body(a_ref, b_ref, o_ref):
        o_ref[...] = (a_ref[...] + b_ref[...].astype(F32)).astype(WIRE_DTYPE)

    spec = pl.BlockSpec((1, tr, cols), lambda k, i: (k, i, 0))
    return pl.pallas_call(
        body, name=name, grid=(n, rows // tr), in_specs=[spec, spec], out_specs=spec,
        out_shape=jax.ShapeDtypeStruct(keep.shape, WIRE_DTYPE),
        compiler_params=_cparams("parallel", "parallel"),
    )(keep, recv)


def _chip_sum(kidx, keep_own, recv_own, parts, *, name):
    n, rows, cols = parts.shape
    tr = _row_block(rows, cols)

    def body(k_ref, a_ref, b_ref, t_ref, o_ref):
        own = a_ref[...] + b_ref[...].astype(F32)
        total = jnp.zeros_like(own)
        for j in range(n):
            total = total + jnp.where(k_ref[0] == j, own, t_ref[j].astype(F32))
        o_ref[...] = total

    spec = pl.BlockSpec((tr, cols), lambda i: (i, 0))
    return pl.pallas_call(
        body, name=name, grid=(rows // tr,),
        in_specs=[pl.BlockSpec(memory_space=pltpu.SMEM), spec, spec, pl.BlockSpec((n, tr, cols), lambda i: (0, i, 0))],
        out_specs=spec, out_shape=jax.ShapeDtypeStruct((rows, cols), F32),
        compiler_params=_cparams("parallel"),
    )(kidx, keep_own, recv_own, parts)


def _place():
    x, y, c = lax.axis_index("x"), lax.axis_index("y"), lax.axis_index("c")
    chips = [(1 - x, y), (x, 1 - y), (1 - x, 1 - y)]
    return x, y, c, chips


def _comm_gather_weights(pack, *, name):
    rows, cols = pack.shape
    rh = rows // 2

    def body(src_ref, out_ref, send_sems, recv_sems, local_sem):
        x, y, c, chips = _place()
        k = 2 * x + y
        mine = pl.ds(c * rh, rh)
        other = pl.ds((1 - c) * rh, rh)

        def copy(sem, src, slot, rws, to):
            return pltpu.make_async_remote_copy(src_ref=src, dst_ref=out_ref.at[slot, rws], send_sem=send_sems.at[sem],
                                                recv_sem=recv_sems.at[sem], device_id=to, device_id_type=MESH)

        local = pltpu.make_async_copy(src_ref, out_ref.at[k], local_sem)
        local.start()
        first = [copy(j, src_ref.at[mine], k, mine, (*chip, c)) for j, chip in enumerate(chips)]
        for cp in first:
            cp.start()
        passed = []
        for j, (px, py) in enumerate(chips):
            slot = 2 * px + py
            copy(j, src_ref.at[mine], slot, mine, (px, py, c)).wait_recv()
            cp = copy(3 + j, out_ref.at[slot, mine], slot, mine, (x, y, 1 - c))
            cp.start()
            passed.append(cp)
        for j, (px, py) in enumerate(chips):
            slot = 2 * px + py
            copy(3 + j, out_ref.at[slot, other], slot, other, (x, y, 1 - c)).wait_recv()
        for cp in first + passed:
            cp.wait_send()
        local.wait()

    return pl.pallas_call(
        body, name=name, in_specs=[ANY], out_specs=ANY,
        out_shape=jax.ShapeDtypeStruct((4, rows, cols), pack.dtype),
        scratch_shapes=[pltpu.SemaphoreType.DMA((6,)), pltpu.SemaphoreType.DMA((6,)), pltpu.SemaphoreType.DMA],
    )(pack)


def _comm_swap_sibling(buf, *, name):
    def body(src_ref, out_ref, send_sem, recv_sem):
        x, y, c, _ = _place()
        cp = pltpu.make_async_remote_copy(src_ref=src_ref, dst_ref=out_ref, send_sem=send_sem, recv_sem=recv_sem,
                                          device_id=(x, y, 1 - c), device_id_type=MESH)
        cp.start()
        cp.wait()

    return pl.pallas_call(
        body, name=name, in_specs=[ANY], out_specs=ANY, out_shape=jax.ShapeDtypeStruct(buf.shape, buf.dtype),
        scratch_shapes=[pltpu.SemaphoreType.DMA, pltpu.SemaphoreType.DMA],
    )(buf)


def _comm_scatter_chips(parts, *, name):
    def body(src_ref, out_ref, send_sems, recv_sems, local_sem):
        x, y, c, chips = _place()
        k = 2 * x + y
        local = pltpu.make_async_copy(src_ref.at[k], out_ref.at[k], local_sem)
        local.start()
        sends = []
        for j, (px, py) in enumerate(chips):
            cp = pltpu.make_async_remote_copy(src_ref=src_ref.at[2 * px + py], dst_ref=out_ref.at[k], send_sem=send_sems.at[j],
                                              recv_sem=recv_sems.at[j], device_id=(px, py, c), device_id_type=MESH)
            cp.start()
            sends.append(cp)
        for j, (px, py) in enumerate(chips):
            pltpu.make_async_remote_copy(src_ref=src_ref.at[k], dst_ref=out_ref.at[2 * px + py], send_sem=send_sems.at[j],
                                         recv_sem=recv_sems.at[j], device_id=(px, py, c), device_id_type=MESH).wait_recv()
        for cp in sends:
            cp.wait_send()
        local.wait()

    return pl.pallas_call(
        body, name=name, in_specs=[ANY], out_specs=ANY, out_shape=jax.ShapeDtypeStruct(parts.shape, parts.dtype),
        scratch_shapes=[pltpu.SemaphoreType.DMA((3,)), pltpu.SemaphoreType.DMA((3,)), pltpu.SemaphoreType.DMA],
    )(parts)


def _comm_join_halves(half, *, name):
    rows, cols = half.shape

    def body(src_ref, out_ref, send_sem, recv_sem, local_sem):
        x, y, c, _ = _place()
        local = pltpu.make_async_copy(src_ref, out_ref.at[c], local_sem)
        local.start()
        cp = pltpu.make_async_remote_copy(src_ref=src_ref, dst_ref=out_ref.at[c], send_sem=send_sem, recv_sem=recv_sem,
                                          device_id=(x, y, 1 - c), device_id_type=MESH)
        cp.start()
        pltpu.make_async_remote_copy(src_ref=src_ref, dst_ref=out_ref.at[1 - c], send_sem=send_sem, recv_sem=recv_sem,
                                     device_id=(x, y, 1 - c), device_id_type=MESH).wait_recv()
        cp.wait_send()
        local.wait()

    return pl.pallas_call(
        body, name=name, in_specs=[ANY], out_specs=ANY, out_shape=jax.ShapeDtypeStruct((2, rows, cols), half.dtype),
        scratch_shapes=[pltpu.SemaphoreType.DMA, pltpu.SemaphoreType.DMA, pltpu.SemaphoreType.DMA],
    )(half)


def _comm_allreduce_small(part, *, name):
    rows, cols = part.shape

    def body(p_ref, o_ref, buf, send_sems, recv_sems):
        x, y, c, _ = _place()
        me = 4 * x + 2 * y + c
        buf[me] = p_ref[...]
        flip = lambda v, bit: 1 - v if bit else v
        peers = [(flip(x, d & 4), flip(y, d & 2), flip(c, d & 1)) for d in range(1, 8)]
        sends = []
        for j, peer in enumerate(peers):
            cp = pltpu.make_async_remote_copy(src_ref=buf.at[me], dst_ref=buf.at[me], send_sem=send_sems.at[j],
                                              recv_sem=recv_sems.at[j], device_id=peer, device_id_type=MESH)
            cp.start()
            sends.append(cp)
        for j, (px, py, pc) in enumerate(peers):
            pltpu.make_async_remote_copy(src_ref=buf.at[me], dst_ref=buf.at[4 * px + 2 * py + pc], send_sem=send_sems.at[j],
                                         recv_sem=recv_sems.at[j], device_id=(px, py, pc), device_id_type=MESH).wait_recv()
        for cp in sends:
            cp.wait_send()
        total = buf[0]
        for i in range(1, 8):
            total = total + buf[i]
        o_ref[...] = total

    vm = pl.BlockSpec(memory_space=pltpu.VMEM)
    return pl.pallas_call(
        body, name=name, in_specs=[vm], out_specs=vm, out_shape=jax.ShapeDtypeStruct((rows, cols), F32),
        scratch_shapes=[pltpu.VMEM((8, rows, cols), F32), pltpu.SemaphoreType.DMA((7,)), pltpu.SemaphoreType.DMA((7,))],
    )(part)


def _pad_in_cols(w):
    qa, ka, va, ga, cq, ckv, kr, gb = jnp.split(w, (1024, 1152, 1280, 2304, 2688, 2944, 3008), axis=-1)
    return jnp.concatenate([qa, ga, gb, cq, ka, ckv, va, kr, jnp.zeros_like(kr)], axis=-1)


def _unpad_in_cols(w):
    qa, ga, gb, cq, ka, ckv, va, kr = (w[..., o:o + n] for o, n in (
        (QA_OFF, 1024), (GA_OFF, 1024), (GB_OFF, 1024), (CQ_OFF, Q_RANK), (KA_OFF, 128), (CKV_OFF, KV_RANK),
        (VA_OFF, 128), (KR_OFF, 64)))
    return jnp.concatenate([qa, ka, va, ga, cq, ckv, kr, gb], axis=-1)


def _pad_q_cols(w):
    w = w.reshape(w.shape[:-1] + (MLA_HEADS, 192))
    return jnp.pad(w, [(0, 0)] * (w.ndim - 1) + [(0, 64)]).reshape(w.shape[:-2] + (MLA_HEADS * 256,))


def _unpad_q_cols(w):
    return w.reshape(w.shape[:-1] + (MLA_HEADS, 256))[..., :192].reshape(w.shape[:-1] + (MLA_HEADS * 192,))


def _perm_kv_cols(w):
    w = w.reshape(w.shape[:-1] + (MLA_HEADS, 2, 128))
    return jnp.swapaxes(w, -3, -2).reshape(w.shape[:-3] + (2048,))


def _unperm_kv_cols(w):
    w = w.reshape(w.shape[:-1] + (2, MLA_HEADS, 128))
    return jnp.swapaxes(w, -3, -2).reshape(w.shape[:-3] + (2048,))


def _pack(w_in, w_q_b, w_kv_b, w_out):
    lead = w_in.shape[:-2]
    parts = [a.reshape(lead + (-1, LANES)) for a in (w_in, w_q_b, w_kv_b, w_out)]
    out = jnp.concatenate(parts, axis=-2)
    return out.reshape(lead[:-1] + (-1, LANES))


def _unpack(p, depth, shapes):
    lead = p.shape[:-2]
    p = p.reshape(lead + (depth, -1, LANES))
    outs, at = [], 0
    for shp in shapes:
        r = shp[0] * shp[1] // LANES
        outs.append(p[..., at:at + r, :].reshape(lead + (depth,) + tuple(shp)))
        at += r
    return outs


def _rope_tables(s):
    pos = jnp.arange(s, dtype=F32)
    inv_freq = 10000.0 ** (-jnp.arange(0, 64, 2, dtype=F32) / 64)
    ang = pos[:, None] * inv_freq[None, :]
    cos, sin = jnp.cos(ang), jnp.sin(ang)
    z64 = jnp.zeros((s, 64), F32)
    tk_c = jnp.concatenate([cos, cos, z64], axis=-1)
    tk_s = jnp.concatenate([-sin, sin, z64], axis=-1)
    tq_c = jnp.concatenate([jnp.ones((s, 128), F32), tk_c], axis=-1)
    tq_s = jnp.concatenate([jnp.zeros((s, 128), F32), tk_s], axis=-1)
    return tq_c, tq_s, tk_c, tk_s


def _device_step(xs, tgt, attn_g, sinks, gq, gkv, final_g, w_in_p, w_q_p, w_kv_p, w_o):
    depth = len(w_in_p)
    s = xs.shape[0]
    tabs = _rope_tables(s)
    saved = []
    x = xs
    for l in range(depth):
        h = _rmsnorm_fwd(x, attn_g[l:l + 1], name=f"norm_fwd{l}")
        proj = _matmul(h, w_in_p[l], name=f"in_proj{l}")
        oa, lse_a = _swa_fwd(proj, sinks[l], name=f"swa_fwd{l}")
        qcat, kcat, v, cqn, ckvn = _mla_qkv_fwd(proj, gq[l:l + 1], gkv[l:l + 1], w_q_p[l], w_kv_p[l], *tabs,
                                                name=f"mla_qkv_fwd{l}")
        ob, lse_b = _mla_fwd(qcat, kcat, v, name=f"mla_fwd{l}")
        y = _gate_fwd(oa, ob, proj, name=f"gate_fwd{l}")
        x_next = _matmul(y, w_o[l], add=x, name=f"out_proj{l}")
        saved.append((x, h, proj, oa, lse_a, qcat, kcat, v, cqn, ckvn, ob, lse_b, y))
        x = x_next

    dx, d_final_g, loss = _final_loss(x, final_g, tgt, name="final_loss")

    d_attn_g, d_sinks, d_gq, d_gkv = [None] * depth, [None] * depth, [None] * depth, [None] * depth
    d_w_in, d_w_q, d_w_kv, d_w_o = [None] * depth, [None] * depth, [None] * depth, [None] * depth
    for l in reversed(range(depth)):
        x, h, proj, oa, lse_a, qcat, kcat, v, cqn, ckvn, ob, lse_b, y = saved[l]
        dy = _matmul(dx, w_o[l], tb=True, name=f"out_proj_dx{l}")
        d_w_o[l] = _matmul(y, dx, ta=True, name=f"out_proj_dw{l}")
        doa, dob, dga, dgb, delta_b = _gate_bwd(dy, oa, ob, proj, name=f"gate_bwd{l}")
        dqa, dka, dva, dsink = _swa_bwd(proj, sinks[l], lse_a, doa, name=f"swa_bwd{l}")
        dqc, dkc, dv = _mla_bwd(qcat, kcat, v, dob, lse_b, delta_b, name=f"mla_bwd{l}")
        dcq, dckv, dkr, d_w_q[l], d_w_kv[l], dgq_l, dgkv_l = _mla_qkv_bwd(
            proj, cqn, ckvn, dqc, dkc, dv, gq[l:l + 1], gkv[l:l + 1], w_q_p[l], w_kv_p[l], *tabs, name=f"mla_qkv_bwd{l}")
        dproj = jnp.concatenate([dqa, dga, dgb, dcq, dka, dckv, dva, dkr], axis=-1)
        dh = _matmul(dproj, w_in_p[l], tb=True, name=f"in_proj_dx{l}")
        d_w_in[l] = _matmul(h, dproj, ta=True, name=f"in_proj_dw{l}")
        dx, dg_l = _rmsnorm_bwd(dh, x, attn_g[l:l + 1], dx, name=f"norm_bwd{l}")
        d_attn_g[l], d_sinks[l], d_gq[l], d_gkv[l] = dg_l, dsink[0:1, :SWA_HEADS], dgq_l, dgkv_l

    cat = lambda parts: jnp.concatenate(parts, axis=0)
    return (loss, dx, cat(d_attn_g), cat(d_sinks), cat(d_gq), cat(d_gkv), d_final_g, d_w_in, d_w_q, d_w_kv, d_w_o)


def kernel(x, attn_norm_g, w_in, swa_sinks, q_a_norm_g, kv_a_norm_g, w_q_b, w_kv_b, w_out, final_norm_g, loss_target, m_attn_norm_g, m_w_in, m_swa_sinks, m_q_a_norm_g, m_kv_a_norm_g, m_w_q_b, m_w_kv_b, m_w_out, m_final_norm_g, v_attn_norm_g, v_w_in, v_swa_sinks, v_q_a_norm_g, v_kv_a_norm_g, v_w_q_b, v_w_kv_b, v_w_out, v_final_norm_g):
    depth = w_in.shape[0]
    shard_shapes = [w_in.shape[1:], w_q_b.shape[1:], w_kv_b.shape[1:], w_out.shape[1:]]
    c = lax.axis_index("c")
    chip = 2 * lax.axis_index("x") + lax.axis_index("y")

    gathered = _comm_gather_weights(_pack(w_in, w_q_b, w_kv_b, w_out).astype(WIRE_DTYPE), name="comm_gather_weights")
    g_in, g_q, g_kv, g_o = _unpack(gathered, depth, shard_shapes)
    cols = lambda a: jnp.moveaxis(a, 0, 2).reshape(a.shape[1:3] + (-1,))
    full_in, full_q, full_kv = _pad_in_cols(cols(g_in)), _pad_q_cols(cols(g_q)), _perm_kv_cols(cols(g_kv))
    full_o = jnp.moveaxis(g_o, 0, 1).reshape(depth, -1, g_o.shape[-1])
    per_layer = lambda a: [a[l] for l in range(depth)]

    (loss, dx, d_attn_g, d_sinks, d_gq, d_gkv, d_final_g, d_w_in, d_w_q, d_w_kv, d_w_o) = _device_step(
        x[0], loss_target[0], attn_norm_g, swa_sinks, q_a_norm_g, kv_a_norm_g, final_norm_g.reshape(1, -1),
        per_layer(full_in), per_layer(full_q), per_layer(full_kv), per_layer(full_o))

    split_cols = lambda a, n: jnp.moveaxis(a.reshape(a.shape[:2] + (4, n)), 2, 0)
    gs_in = split_cols(_unpad_in_cols(jnp.stack(d_w_in)), shard_shapes[0][1])
    gs_q = split_cols(_unpad_q_cols(jnp.stack(d_w_q)), shard_shapes[1][1])
    gs_kv = split_cols(_unperm_kv_cols(jnp.stack(d_w_kv)), shard_shapes[2][1])
    gs_o = jnp.moveaxis(jnp.stack(d_w_o).reshape(depth, 4, -1, D_MODEL), 1, 0)
    gpack = _pack(gs_in, gs_q, gs_kv, gs_o)
    rh = gpack.shape[1] // 2
    halves = gpack.reshape(4, 2, rh, LANES)
    keep = lax.dynamic_index_in_dim(halves, c, axis=1, keepdims=False)
    give = lax.dynamic_index_in_dim(halves, 1 - c, axis=1, keepdims=False).astype(WIRE_DTYPE)
    recv = _comm_swap_sibling(give, name="comm_swap_sibling")
    pair = _pair_sum(keep, recv, name="pair_sum")
    parts = _comm_scatter_chips(pair, name="comm_scatter_chips")
    own = lambda a: lax.dynamic_index_in_dim(a, chip, axis=0, keepdims=False)
    half = _chip_sum(chip.reshape(1).astype(jnp.int32), own(keep), own(recv), parts, name="chip_sum")
    reduced = _comm_join_halves(half, name="comm_join_halves").reshape(2 * rh, LANES)
    g_w_in, g_w_q_b, g_w_kv_b, g_w_out = _unpack(reduced, depth, shard_shapes)

    small = [d_attn_g, d_sinks, d_gq, d_gkv, d_final_g, loss[:, :1]]
    flat = jnp.concatenate([a.reshape(-1) for a in small])
    n_small = flat.shape[0]
    rows = -(-n_small // 1024) * 8
    total = _comm_allreduce_small(jnp.pad(flat, (0, rows * 128 - n_small)).reshape(rows, 128),
                                  name="comm_allreduce_small").reshape(-1)
    outs, at = [], 0
    for a in small:
        outs.append(total[at:at + a.size].reshape(a.shape))
        at += a.size
    g_attn_g, g_sinks, g_gq, g_gkv, g_final_g, loss_total = outs
    g_final_g = g_final_g.reshape(final_norm_g.shape)

    weights = [attn_norm_g, w_in, swa_sinks, q_a_norm_g, kv_a_norm_g, w_q_b, w_kv_b, w_out, final_norm_g]
    grads = [g_attn_g, g_w_in, g_sinks, g_gq, g_gkv, g_w_q_b, g_w_kv_b, g_w_out, g_final_g]
    ms = [m_attn_norm_g, m_w_in, m_swa_sinks, m_q_a_norm_g, m_kv_a_norm_g, m_w_q_b, m_w_kv_b, m_w_out, m_final_norm_g]
    vs = [v_attn_norm_g, v_w_in, v_swa_sinks, v_q_a_norm_g, v_kv_a_norm_g, v_w_q_b, v_w_kv_b, v_w_out, v_final_norm_g]
    as2d = lambda a: a.reshape(1, -1) if a.ndim == 1 else a
    deltas, new_m, new_v = [], [], []
    for i, (w, g, m, v) in enumerate(zip(weights, grads, ms, vs)):
        d, mn, vn = _adamw(as2d(w), as2d(g), as2d(m), as2d(v), name=f"adamw{i}")
        deltas.append(d.reshape(w.shape))
        new_m.append(mn.reshape(w.shape))
        new_v.append(vn.reshape(w.shape))

    return (loss_total.reshape(()), dx[None], *grads, *deltas, *new_m, *new_v)
```

```python
import functools
import math

import jax
import jax.numpy as jnp
from jax import lax
from jax.experimental import pallas as pl
from jax.experimental.pallas import tpu as pltpu

F32 = jnp.float32
MXU_DTYPE = jnp.bfloat16
WIRE_DTYPE = jnp.bfloat16

EPS = 1e-6
NEG = -1e30
BLOCK = 128
D_MODEL = 2048
SWA_HEADS = 16
MLA_HEADS = 8
Q_RANK = 384
KV_RANK = 256
IN_WIDTH = 4032
MLA_SCALE = 192 ** -0.5
SWA_SCALE = 64 ** -0.5
LOG2E = math.log2(math.e)
HEADS_PER_STEP = 2
SLOPES = tuple(2.0 ** (-8.0 * (h + 1) / SWA_HEADS) for h in range(SWA_HEADS))

P_WIDTH = 4096
QA_OFF, GA_OFF, GB_OFF, CQ_OFF, KA_OFF, CKV_OFF, VA_OFF, KR_OFF = 0, 1024, 2048, 3072, 3456, 3584, 3840, 3968

ADAM_LR, ADAM_B1, ADAM_B2, ADAM_EPS, ADAM_WD, ADAM_STEP = 0.001, 0.9, 0.999, 1e-08, 0.01, 10

LANES = 1024
VMEM_LIMIT = 56 * 1024 * 1024
MESH = pl.DeviceIdType.MESH
ANY = pl.BlockSpec(memory_space=pl.ANY)


def _cparams(*sem):
    return pltpu.CompilerParams(dimension_semantics=sem, vmem_limit_bytes=VMEM_LIMIT)


def _dot(a, b, ca, cb):
    return lax.dot_general(a, b, (((ca,), (cb,)), ((), ())), preferred_element_type=F32)


def _matmul(a, b, *, name, ta=False, tb=False, out_dtype=F32, add=None, tm=1024, tn=1024, tk=1024):
    (kdim, m) = a.shape if ta else a.shape[::-1]
    (n, k2) = b.shape if tb else b.shape[::-1]
    assert kdim == k2, (a.shape, b.shape)
    tm, tn, tk = min(tm, m), min(tn, n), min(tk, kdim)
    assert m % tm == 0 and n % tn == 0 and kdim % tk == 0
    nk = kdim // tk

    def body(*refs):
        if add is None:
            a_ref, b_ref, o_ref, acc = refs
        else:
            a_ref, b_ref, add_ref, o_ref, acc = refs
        k = pl.program_id(2)

        @pl.when(k == 0)
        def _():
            acc[...] = jnp.zeros_like(acc)

        acc[...] += _dot(a_ref[...].astype(MXU_DTYPE), b_ref[...].astype(MXU_DTYPE), 0 if ta else 1, 1 if tb else 0)

        @pl.when(k == nk - 1)
        def _():
            r = acc[...]
            if add is not None:
                r = add_ref[...] + r
            o_ref[...] = r.astype(out_dtype)

    a_spec = pl.BlockSpec((tk, tm), lambda i, j, k: (k, i)) if ta else pl.BlockSpec((tm, tk), lambda i, j, k: (i, k))
    b_spec = pl.BlockSpec((tn, tk), lambda i, j, k: (j, k)) if tb else pl.BlockSpec((tk, tn), lambda i, j, k: (k, j))
    in_specs, args = [a_spec, b_spec], [a, b]
    if add is not None:
        in_specs.append(pl.BlockSpec((tm, tn), lambda i, j, k: (i, j)))
        args.append(add)
    return pl.pallas_call(
        body, name=name, grid=(m // tm, n // tn, nk), in_specs=in_specs,
        out_specs=pl.BlockSpec((tm, tn), lambda i, j, k: (i, j)),
        out_shape=jax.ShapeDtypeStruct((m, n), out_dtype),
        scratch_shapes=[pltpu.VMEM((tm, tn), F32)],
        compiler_params=_cparams("parallel", "parallel", "arbitrary"),
    )(*args)


def _matmul_ktiles_nt(a_tiles, b, *, name, tm=512, tn=1024):
    m, kt = a_tiles[0].shape
    n = b.shape[0]
    nt = len(a_tiles)
    assert b.shape[1] == nt * kt
    tm, tn = min(tm, m), min(tn, n)
    assert m % tm == 0 and n % tn == 0

    def body(*refs):
        a_refs, b_refs, o_ref = refs[:nt], refs[nt:2 * nt], refs[2 * nt]
        acc = _dot(a_refs[0][...].astype(MXU_DTYPE), b_refs[0][...].astype(MXU_DTYPE), 1, 1)
        for j in range(1, nt):
            acc += _dot(a_refs[j][...].astype(MXU_DTYPE), b_refs[j][...].astype(MXU_DTYPE), 1, 1)
        o_ref[...] = acc

    in_specs = [pl.BlockSpec((tm, kt), lambda i, jn: (i, 0))] * nt
    in_specs += [pl.BlockSpec((tn, kt), lambda i, jn, j=j: (jn, j)) for j in range(nt)]
    return pl.pallas_call(
        body, name=name, grid=(m // tm, n // tn), in_specs=in_specs,
        out_specs=pl.BlockSpec((tm, tn), lambda i, jn: (i, jn)),
        out_shape=jax.ShapeDtypeStruct((m, n), F32),
        compiler_params=_cparams("parallel", "parallel"),
    )(*a_tiles, *([b] * nt))


def _rmsnorm_fwd(x, g, *, name):
    s, d = x.shape
    tm = min(512, s)

    def body(x_ref, g_ref, h_ref):
        xv = x_ref[...]
        r = lax.rsqrt(jnp.mean(xv * xv, axis=-1, keepdims=True) + EPS)
        h_ref[...] = (xv * r * g_ref[...]).astype(MXU_DTYPE)

    return pl.pallas_call(
        body, name=name, grid=(s // tm,),
        in_specs=[pl.BlockSpec((tm, d), lambda i: (i, 0)), pl.BlockSpec((1, d), lambda i: (0, 0))],
        out_specs=pl.BlockSpec((tm, d), lambda i: (i, 0)),
        out_shape=jax.ShapeDtypeStruct((s, d), MXU_DTYPE),
        compiler_params=_cparams("parallel"),
    )(x, g)


def _rmsnorm_bwd(dh, x, g, dres, *, name):
    s, d = x.shape
    tm = min(512, s)

    def body(dh_ref, x_ref, g_ref, dres_ref, dx_ref, dg_ref):
        @pl.when(pl.program_id(0) == 0)
        def _():
            dg_ref[...] = jnp.zeros_like(dg_ref)

        xv = x_ref[...]
        r = lax.rsqrt(jnp.mean(xv * xv, axis=-1, keepdims=True) + EPS)
        xn = xv * r
        dy = dh_ref[...]
        dg_ref[...] += jnp.sum(dy * xn, axis=0, keepdims=True)
        u = dy * g_ref[...]
        dx_ref[...] = dres_ref[...] + r * (u - xn * jnp.mean(u * xn, axis=-1, keepdims=True))

    row = pl.BlockSpec((tm, d), lambda i: (i, 0))
    vec = pl.BlockSpec((1, d), lambda i: (0, 0))
    return pl.pallas_call(
        body, name=name, grid=(s // tm,), in_specs=[row, row, vec, row], out_specs=[row, vec],
        out_shape=[jax.ShapeDtypeStruct((s, d), F32), jax.ShapeDtypeStruct((1, d), F32)],
        compiler_params=_cparams("arbitrary"),
    )(dh, x, g, dres)


def _final_loss(x, g, tgt, *, name):
    s, d = x.shape
    tm = min(512, s)

    def body(x_ref, g_ref, t_ref, dx_ref, dg_ref, loss_ref):
        @pl.when(pl.program_id(0) == 0)
        def _():
            dg_ref[...] = jnp.zeros_like(dg_ref)
            loss_ref[...] = jnp.zeros_like(loss_ref)

        xv = x_ref[...]
        gv = g_ref[...]
        r = lax.rsqrt(jnp.mean(xv * xv, axis=-1, keepdims=True) + EPS)
        xn = xv * r
        err = xn * gv - t_ref[...]
        sq = jnp.sum(jnp.sum(err * err, axis=-1, keepdims=True), axis=0, keepdims=True)
        loss_ref[...] += (0.5 / d) * sq
        dy = err * (1.0 / d)
        dg_ref[...] += jnp.sum(dy * xn, axis=0, keepdims=True)
        u = dy * gv
        dx_ref[...] = r * (u - xn * jnp.mean(u * xn, axis=-1, keepdims=True))

    row = pl.BlockSpec((tm, d), lambda i: (i, 0))
    vec = pl.BlockSpec((1, d), lambda i: (0, 0))
    return pl.pallas_call(
        body, name=name, grid=(s // tm,), in_specs=[row, vec, row],
        out_specs=[row, vec, pl.BlockSpec((1, 128), lambda i: (0, 0))],
        out_shape=[jax.ShapeDtypeStruct((s, d), F32), jax.ShapeDtypeStruct((1, d), F32),
                   jax.ShapeDtypeStruct((1, 128), F32)],
        compiler_params=_cparams("arbitrary"),
    )(x, g, tgt)


def _swa_keys(kp_ref, kc_ref):
    kk = jnp.concatenate([kp_ref[...], kc_ref[...]], axis=0)
    kr = pltpu.roll(kk, 64, 1)
    lo = lax.broadcasted_iota(jnp.int32, kk.shape, 1) < 64
    return [jnp.where(lo, kk, kr).astype(MXU_DTYPE), jnp.where(lo, kr, kk).astype(MXU_DTYPE)]


def _swa_mask(n):
    qi = lax.broadcasted_iota(jnp.int32, (BLOCK, 2 * BLOCK), 0)
    ki = lax.broadcasted_iota(jnp.int32, (BLOCK, 2 * BLOCK), 1)
    delta = BLOCK + qi - ki
    valid = (delta >= 0) & (delta < BLOCK) & ((ki >= BLOCK) | (n > 0))
    return valid, delta.astype(F32)


def _swa_scores(qm, keys, h, valid, deltaf):
    s = _dot(qm, keys, 1, 1) * SWA_SCALE
    return jnp.where(valid, s - SLOPES[h] * deltaf, NEG)


def _swa_specs(nb):
    kcol, vcol = KA_OFF // BLOCK, VA_OFF // BLOCK
    last = nb - 1
    cur = lambda n: jnp.minimum(n, last)
    prev = lambda n: jnp.maximum(jnp.minimum(n, last) - 1, 0)
    return [
        pl.BlockSpec(memory_space=pltpu.SMEM),
        pl.BlockSpec((BLOCK, 1024), lambda n: (cur(n), QA_OFF // 1024)),
        pl.BlockSpec((BLOCK, BLOCK), lambda n: (cur(n), kcol)),
        pl.BlockSpec((BLOCK, BLOCK), lambda n: (prev(n), kcol)),
        pl.BlockSpec((BLOCK, BLOCK), lambda n: (cur(n), vcol)),
        pl.BlockSpec((BLOCK, BLOCK), lambda n: (prev(n), vcol)),
    ]


def _swa_fwd(proj, sinks, *, name):
    s = proj.shape[0]
    nb = s // BLOCK

    def body(sink_ref, q_ref, kc_ref, kp_ref, vc_ref, vp_ref, o_ref, lse_ref):
        n = pl.program_id(0)
        keys = _swa_keys(kp_ref, kc_ref)
        vals = _swa_keys(vp_ref, vc_ref)
        valid, deltaf = _swa_mask(n)
        lane = lax.broadcasted_iota(jnp.int32, (BLOCK, BLOCK), 1)
        lo = lane < 64
        lse_acc = jnp.zeros((BLOCK, BLOCK), F32)
        for i in range(SWA_HEADS // 2):
            j = i // 4
            qp = q_ref[:, i * 128:(i + 1) * 128]
            outs = []
            for half in range(2):
                h = 2 * i + half
                qm = jnp.where(lo if half == 0 else ~lo, qp, 0.0).astype(MXU_DTYPE)
                sc = _swa_scores(qm, keys[j], h, valid, deltaf)
                sink = sink_ref[h]
                m = jnp.maximum(jnp.max(sc, axis=-1, keepdims=True), sink)
                p = jnp.exp(sc - m)
                l = jnp.sum(p, axis=-1, keepdims=True) + jnp.exp(sink - m)
                outs.append(jnp.dot((p / l).astype(MXU_DTYPE), vals[j], preferred_element_type=F32))
                lse_acc = jnp.where(lane == h, m + jnp.log(l), lse_acc)
            o_ref[:, i * 128:(i + 1) * 128] = jnp.where(lo, outs[0], outs[1])
        lse_ref[...] = lse_acc

    return pl.pallas_call(
        body, name=name, grid=(nb,), in_specs=_swa_specs(nb),
        out_specs=[pl.BlockSpec((BLOCK, 1024), lambda n: (n, 0)), pl.BlockSpec((BLOCK, BLOCK), lambda n: (n, 0))],
        out_shape=[jax.ShapeDtypeStruct((s, 1024), F32), jax.ShapeDtypeStruct((s, BLOCK), F32)],
        compiler_params=_cparams("parallel"),
    )(sinks, proj, proj, proj, proj, proj)


def _swa_bwd(proj, sinks, lse, do, *, name):
    s = proj.shape[0]
    nb = s // BLOCK
    last = nb - 1

    def body(sink_ref, q_ref, kc_ref, kp_ref, vc_ref, vp_ref, lse_ref, do_ref,
             dq_ref, dk_ref, dv_ref, dsink_ref, carry_k, carry_v):
        n = pl.program_id(0)

        @pl.when(n == 0)
        def _():
            carry_k[...] = jnp.zeros_like(carry_k)
            carry_v[...] = jnp.zeros_like(carry_v)
            dsink_ref[...] = jnp.zeros_like(dsink_ref)

        @pl.when(n < nb)
        def _():
            keys = _swa_keys(kp_ref, kc_ref)
            vals = _swa_keys(vp_ref, vc_ref)
            valid, deltaf = _swa_mask(n)
            lane = lax.broadcasted_iota(jnp.int32, (BLOCK, BLOCK), 1)
            lane1 = lax.broadcasted_iota(jnp.int32, (1, BLOCK), 1)
            lo = lane < 64
            lse_blk = lse_ref[...]
            acc_k = [jnp.zeros((2 * BLOCK, BLOCK), F32) for _ in range(2)]
            acc_v = [jnp.zeros((2 * BLOCK, BLOCK), F32) for _ in range(2)]
            dsink = jnp.zeros((1, BLOCK), F32)
            for i in range(SWA_HEADS // 2):
                j = i // 4
                qp = q_ref[:, i * 128:(i + 1) * 128]
                dop = do_ref[:, i * 128:(i + 1) * 128].astype(F32)
                dqs = []
                for half in range(2):
                    h = 2 * i + half
                    msk = lo if half == 0 else ~lo
                    qm = jnp.where(msk, qp, 0.0).astype(MXU_DTYPE)
                    dom = jnp.where(msk, dop, 0.0).astype(MXU_DTYPE)
                    sc = _swa_scores(qm, keys[j], h, valid, deltaf)
                    lse_h = jnp.sum(jnp.where(lane == h, lse_blk, 0.0), axis=-1, keepdims=True)
                    p = jnp.exp(sc - lse_h)
                    dp = _dot(dom, vals[j], 1, 1)
                    dlt = jnp.sum(dp * p, axis=-1, keepdims=True)
                    ds = (p * (dp - dlt) * SWA_SCALE).astype(MXU_DTYPE)
                    psink = jnp.exp(sink_ref[h] - lse_h)
                    dsink = jnp.where(lane1 == h, -jnp.sum(psink * dlt, axis=0, keepdims=True), dsink)
                    dqs.append(jnp.dot(ds, keys[j], preferred_element_type=F32))
                    acc_k[j] += _dot(ds, qm, 0, 0)
                    acc_v[j] += _dot(p.astype(MXU_DTYPE), dom, 0, 0)
                dq_ref[:, i * 128:(i + 1) * 128] = jnp.where(lo, dqs[0], dqs[1]).astype(dq_ref.dtype)
            lo2 = lax.broadcasted_iota(jnp.int32, (2 * BLOCK, BLOCK), 1) < 64
            fold = lambda acc: jnp.where(lo2, acc[0] + pltpu.roll(acc[0], 64, 1), acc[1] + pltpu.roll(acc[1], 64, 1))
            dkk, dvv = fold(acc_k), fold(acc_v)
            dk_ref[...] = (carry_k[...] + dkk[:BLOCK]).astype(dk_ref.dtype)
            dv_ref[...] = (carry_v[...] + dvv[:BLOCK]).astype(dv_ref.dtype)
            carry_k[...] = dkk[BLOCK:]
            carry_v[...] = dvv[BLOCK:]
            dsink_ref[...] += jnp.broadcast_to(dsink, dsink_ref.shape)

        @pl.when(n == nb)
        def _():
            dk_ref[...] = carry_k[...].astype(dk_ref.dtype)
            dv_ref[...] = carry_v[...].astype(dv_ref.dtype)

    cur = lambda n: jnp.minimum(n, last)
    lag = lambda n: jnp.maximum(n - 1, 0)
    return pl.pallas_call(
        body, name=name, grid=(nb + 1,),
        in_specs=_swa_specs(nb) + [pl.BlockSpec((BLOCK, BLOCK), lambda n: (cur(n), 0)),
                                   pl.BlockSpec((BLOCK, 1024), lambda n: (cur(n), 0))],
        out_specs=[pl.BlockSpec((BLOCK, 1024), lambda n: (cur(n), 0)),
                   pl.BlockSpec((BLOCK, BLOCK), lambda n: (lag(n), 0)),
                   pl.BlockSpec((BLOCK, BLOCK), lambda n: (lag(n), 0)),
                   pl.BlockSpec((8, BLOCK), lambda n: (0, 0))],
        out_shape=[jax.ShapeDtypeStruct((s, 1024), MXU_DTYPE), jax.ShapeDtypeStruct((s, BLOCK), MXU_DTYPE),
                   jax.ShapeDtypeStruct((s, BLOCK), MXU_DTYPE), jax.ShapeDtypeStruct((8, BLOCK), F32)],
        scratch_shapes=[pltpu.VMEM((BLOCK, BLOCK), F32), pltpu.VMEM((BLOCK, BLOCK), F32)],
        compiler_params=_cparams("arbitrary"),
    )(sinks, proj, proj, proj, proj, proj, lse, do)


def _rope_partner(v, first, width):
    lane = lax.broadcasted_iota(jnp.int32, v.shape, 1)
    in_a = (lane >= first) & (lane < first + 32)
    in_b = (lane >= first + 32) & (lane < first + 64)
    return jnp.where(in_a, pltpu.roll(v, width - 32, 1), jnp.where(in_b, pltpu.roll(v, 32, 1), 0.0))


def _mla_qkv_fwd(proj, gq, gkv, wq, wkv, tq_c, tq_s, tk_c, tk_s, *, name):
    s = proj.shape[0]
    tm = min(256, s)

    def body(cq_ref, ckv_ref, kr_ref, gq_ref, gkv_ref, wq_ref, wkv_ref, qc_ref, qs_ref, kc_ref, ks_ref,
             qcat_ref, kcat_ref, v_ref, cqn_ref, ckvn_ref):
        cq = cq_ref[...]
        cqn = (cq * lax.rsqrt(jnp.mean(cq * cq, axis=-1, keepdims=True) + EPS) * gq_ref[...]).astype(MXU_DTYPE)
        cqn_ref[...] = cqn
        qpre = jnp.dot(cqn, wq_ref[...], preferred_element_type=F32)
        qc, qs = qc_ref[...], qs_ref[...]
        for hh in range(MLA_HEADS):
            blk = qpre[:, hh * 256:(hh + 1) * 256]
            qcat_ref[:, hh * 256:(hh + 1) * 256] = (blk * qc + _rope_partner(blk, 128, 256) * qs).astype(MXU_DTYPE)
        ckv = ckv_ref[...]
        ckvn = (ckv * lax.rsqrt(jnp.mean(ckv * ckv, axis=-1, keepdims=True) + EPS) * gkv_ref[...]).astype(MXU_DTYPE)
        ckvn_ref[...] = ckvn
        kv = jnp.dot(ckvn, wkv_ref[...], preferred_element_type=F32)
        v_ref[...] = kv[:, 1024:].astype(MXU_DTYPE)
        kr = kr_ref[...]
        krr = (kr * kc_ref[...] + _rope_partner(kr, 0, 128) * ks_ref[...]).astype(MXU_DTYPE)
        for hh in range(MLA_HEADS):
            kcat_ref[:, hh * 256:hh * 256 + 128] = kv[:, hh * 128:(hh + 1) * 128].astype(MXU_DTYPE)
            kcat_ref[:, hh * 256 + 128:(hh + 1) * 256] = krr

    row = lambda w, c: pl.BlockSpec((tm, w), lambda i: (i, c))
    full = lambda a: pl.BlockSpec(a.shape, lambda i: (0, 0))
    return pl.pallas_call(
        body, name=name, grid=(s // tm,),
        in_specs=[row(Q_RANK, CQ_OFF // Q_RANK), row(KV_RANK, CKV_OFF // KV_RANK), row(128, KR_OFF // 128),
                  full(gq), full(gkv), full(wq), full(wkv), row(256, 0), row(256, 0), row(128, 0), row(128, 0)],
        out_specs=[row(2048, 0), row(2048, 0), row(1024, 0), row(Q_RANK, 0), row(KV_RANK, 0)],
        out_shape=[jax.ShapeDtypeStruct((s, 2048), MXU_DTYPE), jax.ShapeDtypeStruct((s, 2048), MXU_DTYPE),
                   jax.ShapeDtypeStruct((s, 1024), MXU_DTYPE), jax.ShapeDtypeStruct((s, Q_RANK), MXU_DTYPE),
                   jax.ShapeDtypeStruct((s, KV_RANK), MXU_DTYPE)],
        compiler_params=_cparams("parallel"),
    )(proj, proj, proj, gq, gkv, wq, wkv, tq_c, tq_s, tk_c, tk_s)


def _norm_bwd(x, g, dy):
    r = lax.rsqrt(jnp.mean(x * x, axis=-1, keepdims=True) + EPS)
    xn = x * r
    u = dy * g
    return r * (u - xn * jnp.mean(u * xn, axis=-1, keepdims=True)), jnp.sum(dy * xn, axis=0, keepdims=True)


def _mla_qkv_bwd(proj, cqn, ckvn, dqcat, dkcat, dv, dka, dva, gq, gkv, wq, wkv, tq_c, tq_s, tk_c, tk_s, *, name):
    s = proj.shape[0]
    tm = min(256, s)
    t_cq, t_ka, t_ckv, t_va, t_kr = (o - CQ_OFF for o in (CQ_OFF, KA_OFF, CKV_OFF, VA_OFF, KR_OFF))

    def body(cq_ref, ckv_ref, cqn_ref, ckvn_ref, dq_ref, dk_ref, dv_ref, dka_ref, dva_ref, gq_ref, gkv_ref, wq_ref,
             wkv_ref, qc_ref, qs_ref, kc_ref, ks_ref,
             tile_ref, dwq_ref, dwkv_ref, dgq_ref, dgkv_ref, dqpre, dkv):
        dcq_ref = tile_ref.at[:, t_cq:t_cq + Q_RANK]
        dckv_ref = tile_ref.at[:, t_ckv:t_ckv + KV_RANK]
        dkr_ref = tile_ref.at[:, t_kr:t_kr + 128]
        tile_ref[:, t_ka:t_ka + 128] = dka_ref[...]
        tile_ref[:, t_va:t_va + 128] = dva_ref[...]

        @pl.when(pl.program_id(0) == 0)
        def _():
            for r in (dwq_ref, dwkv_ref, dgq_ref, dgkv_ref):
                r[...] = jnp.zeros_like(r)

        qc, qs = qc_ref[...], qs_ref[...]
        dkrr = jnp.zeros((tm, 128), F32)
        for hh in range(MLA_HEADS):
            blk = dq_ref[:, hh * 256:(hh + 1) * 256]
            dqpre[:, hh * 256:(hh + 1) * 256] = (blk * qc + _rope_partner(blk * qs, 128, 256)).astype(MXU_DTYPE)
            dkv[:, hh * 128:(hh + 1) * 128] = dk_ref[:, hh * 256:hh * 256 + 128].astype(MXU_DTYPE)
            dkrr = dkrr + dk_ref[:, hh * 256 + 128:(hh + 1) * 256]
        dkv[:, 1024:] = dv_ref[...].astype(MXU_DTYPE)
        dkr_ref[...] = (dkrr * kc_ref[...] + _rope_partner(dkrr * ks_ref[...], 0, 128)).astype(dkr_ref.dtype)

        dq_b = dqpre[...]
        dwq_ref[...] += _dot(cqn_ref[...], dq_b, 0, 0)
        dcq, dgq = _norm_bwd(cq_ref[...], gq_ref[...], _dot(dq_b, wq_ref[...], 1, 1))
        dcq_ref[...] = dcq.astype(dcq_ref.dtype)
        dgq_ref[...] += dgq

        dkv_b = dkv[...]
        dwkv_ref[...] += _dot(ckvn_ref[...], dkv_b, 0, 0)
        dckv, dgkv = _norm_bwd(ckv_ref[...], gkv_ref[...], _dot(dkv_b, wkv_ref[...], 1, 1))
        dckv_ref[...] = dckv.astype(dckv_ref.dtype)
        dgkv_ref[...] += dgkv

    row = lambda w, c: pl.BlockSpec((tm, w), lambda i: (i, c))
    full = lambda shape: pl.BlockSpec(shape, lambda i: (0, 0))
    return pl.pallas_call(
        body, name=name, grid=(s // tm,),
        in_specs=[row(Q_RANK, CQ_OFF // Q_RANK), row(KV_RANK, CKV_OFF // KV_RANK), row(Q_RANK, 0), row(KV_RANK, 0),
                  row(2048, 0), row(2048, 0), row(1024, 0), row(128, 0), row(128, 0), full(gq.shape), full(gkv.shape),
                  full(wq.shape), full(wkv.shape), row(256, 0), row(256, 0), row(128, 0), row(128, 0)],
        out_specs=[row(1024, 0), full(wq.shape), full(wkv.shape), full(gq.shape), full(gkv.shape)],
        out_shape=[jax.ShapeDtypeStruct((s, 1024), MXU_DTYPE), jax.ShapeDtypeStruct(wq.shape, F32),
                   jax.ShapeDtypeStruct(wkv.shape, F32), jax.ShapeDtypeStruct(gq.shape, F32),
                   jax.ShapeDtypeStruct(gkv.shape, F32)],
        scratch_shapes=[pltpu.VMEM((tm, 2048), MXU_DTYPE), pltpu.VMEM((tm, 2048), MXU_DTYPE)],
        compiler_params=_cparams("arbitrary"),
    )(proj, proj, cqn, ckvn, dqcat, dkcat, dv, dka, dva, gq, gkv, wq, wkv, tq_c, tq_s, tk_c, tk_s)


def _causal_mask(t):
    return lax.broadcasted_iota(jnp.int32, (t, t), 1) <= lax.broadcasted_iota(jnp.int32, (t, t), 0)


def _mla_fwd(qcat, kcat, v, *, name):
    s = qcat.shape[0]
    t = min(512, s)
    nq = s // t
    hp = HEADS_PER_STEP
    ng = MLA_HEADS // hp
    c2 = MLA_SCALE * LOG2E

    def body(q_ref, k_ref, v_ref, o_ref, lse_ref, m_s, l_s, acc_s):
        qi = pl.program_id(1)
        for e in range(hp):
            m_s[e] = jnp.full((t, 1), NEG, F32)
            l_s[e] = jnp.zeros((t, 1), F32)
            acc_s[e] = jnp.zeros((t, 128), F32)

        def chunk(j, masked):
            rows = pl.ds(pl.multiple_of(j * t, t), t)
            for e in range(hp):
                sc = _dot(q_ref[:, e * 256:(e + 1) * 256], k_ref[rows, e * 256:(e + 1) * 256], 1, 1) * c2
                if masked:
                    sc = jnp.where(_causal_mask(t), sc, NEG)
                m_prev = m_s[e]
                m_new = jnp.maximum(m_prev, jnp.max(sc, axis=-1, keepdims=True))
                alpha = jnp.exp2(m_prev - m_new)
                p = jnp.exp2(sc - m_new)
                l_s[e] = alpha * l_s[e] + jnp.sum(p, axis=-1, keepdims=True)
                acc_s[e] = alpha * acc_s[e] + jnp.dot(p.astype(MXU_DTYPE), v_ref[rows, e * 128:(e + 1) * 128],
                                                      preferred_element_type=F32)
                m_s[e] = m_new

        def unmasked(j, carry):
            chunk(j, False)
            return carry

        lax.fori_loop(0, qi, unmasked, 0)
        chunk(qi, True)
        lane = lax.broadcasted_iota(jnp.int32, (t, 128), 1)
        stats = jnp.zeros((t, 128), F32)
        for e in range(hp):
            l = l_s[e]
            o_ref[:, e * 128:(e + 1) * 128] = acc_s[e] / l
            stats = jnp.where(lane == e, m_s[e] + jnp.log(l) * LOG2E, stats)
        lse_ref[...] = stats

    return pl.pallas_call(
        body, name=name, grid=(ng, nq),
        in_specs=[pl.BlockSpec((t, 256 * hp), lambda g, qi: (qi, g)), pl.BlockSpec((s, 256 * hp), lambda g, qi: (0, g)),
                  pl.BlockSpec((s, 128 * hp), lambda g, qi: (0, g))],
        out_specs=[pl.BlockSpec((t, 128 * hp), lambda g, qi: (qi, g)), pl.BlockSpec((t, 128), lambda g, qi: (qi, g))],
        out_shape=[jax.ShapeDtypeStruct((s, 1024), F32), jax.ShapeDtypeStruct((s, 128 * ng), F32)],
        scratch_shapes=[pltpu.VMEM((hp, t, 1), F32), pltpu.VMEM((hp, t, 1), F32), pltpu.VMEM((hp, t, 128), F32)],
        compiler_params=_cparams("parallel", "arbitrary"),
    )(qcat, kcat, v)


def _mla_bwd(qcat, kcat, v, do, lse, delta, *, name):
    s = qcat.shape[0]
    t = min(512, s)
    nq = s // t
    hp = HEADS_PER_STEP
    c2 = MLA_SCALE * LOG2E

    def body(q_ref, k_ref, v_ref, do_ref, lse_ref, dl_ref, dq_ref, dk_ref, dv_ref, dk_acc, dv_acc):
        h, ki = pl.program_id(0), pl.program_id(1)

        @pl.when(ki == 0)
        def _():
            dq_ref[...] = jnp.zeros_like(dq_ref)

        dk_acc[...] = jnp.zeros_like(dk_acc)
        dv_acc[...] = jnp.zeros_like(dv_acc)
        k, vv = k_ref[...], v_ref[...]
        mine = lax.broadcasted_iota(jnp.int32, (t, 128), 1) == h % hp

        def chunk(qi, masked):
            rows = pl.ds(pl.multiple_of(qi * t, t), t)
            q, dob = q_ref[rows, :], do_ref[rows, :]
            pick = lambda r: jnp.sum(jnp.where(mine, r[rows, :], 0.0), axis=-1, keepdims=True)
            sc = _dot(q, k, 1, 1) * c2
            if masked:
                sc = jnp.where(_causal_mask(t), sc, NEG)
            p = jnp.exp2(sc - pick(lse_ref))
            dp = _dot(dob, vv, 1, 1)
            ds = (p * (dp - pick(dl_ref)) * MLA_SCALE).astype(MXU_DTYPE)
            dv_acc[...] += _dot(p.astype(MXU_DTYPE), dob, 0, 0)
            dk_acc[...] += _dot(ds, q, 0, 0)
            dq_ref[rows, :] += jnp.dot(ds, k, preferred_element_type=F32)

        def unmasked(qi, carry):
            chunk(qi, False)
            return carry

        chunk(ki, True)
        lax.fori_loop(ki + 1, nq, unmasked, 0)
        dk_ref[...] = dk_acc[...]
        dv_ref[...] = dv_acc[...]

    head = lambda w: pl.BlockSpec((s, w), lambda h, ki: (0, h))
    blk = lambda w: pl.BlockSpec((t, w), lambda h, ki: (ki, h))
    stat = pl.BlockSpec((s, 128), lambda h, ki: (0, h // hp))
    return pl.pallas_call(
        body, name=name, grid=(MLA_HEADS, nq),
        in_specs=[head(256), blk(256), blk(128), head(128), stat, stat],
        out_specs=[head(256), blk(256), blk(128)],
        out_shape=[jax.ShapeDtypeStruct((s, 2048), F32), jax.ShapeDtypeStruct((s, 2048), F32),
                   jax.ShapeDtypeStruct((s, 1024), F32)],
        scratch_shapes=[pltpu.VMEM((t, 256), F32), pltpu.VMEM((t, 128), F32)],
        compiler_params=_cparams("parallel", "arbitrary"),
    )(qcat, kcat, v, do, lse, delta)


def _gate_specs(tm):
    half = lambda c: pl.BlockSpec((tm, 1024), lambda i: (i, c))
    return half(0), half(GA_OFF // 1024), half(GB_OFF // 1024)


def _gate_fwd(oa, ob, proj, *, name):
    s = oa.shape[0]
    tm = min(512, s)

    def body(oa_ref, ob_ref, ga_ref, gb_ref, y_ref):
        ga, gb = ga_ref[...], gb_ref[...]
        y_ref[:, :1024] = (oa_ref[...] * (ga * jax.nn.sigmoid(ga))).astype(MXU_DTYPE)
        y_ref[:, 1024:] = (ob_ref[...] * (gb * jax.nn.sigmoid(gb))).astype(MXU_DTYPE)

    o_spec, ga_spec, gb_spec = _gate_specs(tm)
    return pl.pallas_call(
        body, name=name, grid=(s // tm,), in_specs=[o_spec, o_spec, ga_spec, gb_spec],
        out_specs=pl.BlockSpec((tm, 2048), lambda i: (i, 0)),
        out_shape=jax.ShapeDtypeStruct((s, 2048), MXU_DTYPE),
        compiler_params=_cparams("parallel"),
    )(oa, ob, proj, proj)


def _gate_bwd(dy, oa, ob, proj, *, name):
    s = oa.shape[0]
    tm = min(512, s)

    def body(dy_ref, oa_ref, ob_ref, ga_ref, gb_ref, doa_ref, dob_ref, dga_ref, dgb_ref, dl_ref):
        def branch(dyv, o, g, do_ref, dg_ref):
            sg = jax.nn.sigmoid(g)
            do = dyv * (g * sg)
            do_ref[...] = do.astype(MXU_DTYPE)
            dg_ref[...] = (dyv * o * (sg * (1.0 + g * (1.0 - sg)))).astype(MXU_DTYPE)
            return do

        branch(dy_ref[:, :1024], oa_ref[...], ga_ref[...], doa_ref, dga_ref)
        ob = ob_ref[...]
        prod = branch(dy_ref[:, 1024:], ob, gb_ref[...], dob_ref, dgb_ref) * ob
        lane = lax.broadcasted_iota(jnp.int32, (tm, stat_w), 1)
        acc = jnp.zeros((tm, stat_w), F32)
        for hh in range(MLA_HEADS):
            at = (hh // HEADS_PER_STEP) * 128 + hh % HEADS_PER_STEP
            acc = jnp.where(lane == at, jnp.sum(prod[:, hh * 128:(hh + 1) * 128], axis=-1, keepdims=True), acc)
        dl_ref[...] = acc

    stat_w = 128 * (MLA_HEADS // HEADS_PER_STEP)
    o_spec, ga_spec, gb_spec = _gate_specs(tm)
    return pl.pallas_call(
        body, name=name, grid=(s // tm,),
        in_specs=[pl.BlockSpec((tm, 2048), lambda i: (i, 0)), o_spec, o_spec, ga_spec, gb_spec],
        out_specs=[o_spec, o_spec, o_spec, o_spec, pl.BlockSpec((tm, stat_w), lambda i: (i, 0))],
        out_shape=[jax.ShapeDtypeStruct((s, 1024), MXU_DTYPE)] * 4 + [jax.ShapeDtypeStruct((s, stat_w), F32)],
        compiler_params=_cparams("parallel"),
    )(dy, oa, ob, proj, proj)


def _row_block(rows, cols, itemsize=4, budget=2 << 20):
    fits = [tr for tr in range(16, rows + 1, 16) if rows % tr == 0 and tr * cols * itemsize <= budget]
    return fits[-1] if fits else rows


def _adamw(w, g, m, v, *, name):
    shape = w.shape
    cols = shape[-1]
    w2, g2, m2, v2 = (a.reshape(-1, cols) for a in (w, g, m, v))
    rows = w2.shape[0]
    tr = _row_block(rows, cols)

    def body(w_ref, g_ref, m_ref, v_ref, d_ref, mo_ref, vo_ref):
        gv = g_ref[...]
        mn = ADAM_B1 * m_ref[...] + (1.0 - ADAM_B1) * gv
        vn = ADAM_B2 * v_ref[...] + (1.0 - ADAM_B2) * jnp.square(gv)
        m_hat = mn / (1.0 - ADAM_B1 ** ADAM_STEP)
        v_hat = vn / (1.0 - ADAM_B2 ** ADAM_STEP)
        d_ref[...] = -ADAM_LR * (m_hat / (jnp.sqrt(v_hat) + ADAM_EPS) + ADAM_WD * w_ref[...])
        mo_ref[...] = mn
        vo_ref[...] = vn

    spec = pl.BlockSpec((tr, cols), lambda i: (i, 0))
    outs = pl.pallas_call(
        body, name=name, grid=(rows // tr,), in_specs=[spec] * 4, out_specs=[spec] * 3,
        out_shape=[jax.ShapeDtypeStruct((rows, cols), F32)] * 3,
        compiler_params=_cparams("parallel"),
    )(w2, g2, m2, v2)
    return tuple(o.reshape(shape) for o in outs)


def _pair_sum(where, grads, recv, *, name):
    n, _, rows, cols = grads.shape
    tr = _row_block(rows, cols)

    def body(where_ref, a_ref, b_ref, o_ref):
        o_ref[...] = (a_ref[...] + b_ref[...].astype(F32)).astype(WIRE_DTYPE)

    spec = pl.BlockSpec((None, tr, cols), lambda k, i, w: (k, i, 0))
    return pl.pallas_call(
        body, name=name,
        grid_spec=pltpu.PrefetchScalarGridSpec(
            num_scalar_prefetch=1, grid=(n, rows // tr),
            in_specs=[pl.BlockSpec((None, None, tr, cols), lambda k, i, w: (k, w[4], i, 0)), spec], out_specs=spec),
        out_shape=jax.ShapeDtypeStruct((n, rows, cols), WIRE_DTYPE),
        compiler_params=_cparams("parallel", "parallel"),
    )(where, grads, recv)


def _chip_sum(where, grads, recv, parts, *, name):
    _, _, rows, cols = grads.shape
    tr = _row_block(rows, cols)

    def body(where_ref, a_ref, b_ref, t0_ref, t1_ref, t2_ref, o_ref):
        total = a_ref[...] + b_ref[...].astype(F32)
        for t_ref in (t0_ref, t1_ref, t2_ref):
            total = total + t_ref[...].astype(F32)
        o_ref[...] = total

    slot = lambda j: pl.BlockSpec((None, tr, cols), lambda i, w: (w[j], i, 0))
    return pl.pallas_call(
        body, name=name,
        grid_spec=pltpu.PrefetchScalarGridSpec(
            num_scalar_prefetch=1, grid=(rows // tr,),
            in_specs=[pl.BlockSpec((None, None, tr, cols), lambda i, w: (w[0], w[4], i, 0)), slot(0), slot(1), slot(2),
                      slot(3)],
            out_specs=slot(4)),
        out_shape=jax.ShapeDtypeStruct((2, rows, cols), F32),
        compiler_params=_cparams("parallel"),
    )(where, grads, recv, parts, parts, parts)


def _place():
    x, y, c = lax.axis_index("x"), lax.axis_index("y"), lax.axis_index("c")
    chips = [(1 - x, y), (x, 1 - y), (1 - x, 1 - y)]
    return x, y, c, chips


def _comm_gather_weights(pack, *, name):
    rows, cols = pack.shape
    rh = rows // 2

    def body(src_ref, out_ref, send_sems, recv_sems):
        x, y, c, chips = _place()
        k = 2 * x + y
        mine = pl.ds(c * rh, rh)
        other = pl.ds((1 - c) * rh, rh)

        def copy(sem, src, slot, rws, to):
            return pltpu.make_async_remote_copy(src_ref=src, dst_ref=out_ref.at[slot, rws], send_sem=send_sems.at[sem],
                                                recv_sem=recv_sems.at[sem], device_id=to, device_id_type=MESH)

        first = [copy(j, src_ref.at[mine], k, mine, (*chip, c)) for j, chip in enumerate(chips)]
        for cp in first:
            cp.start()
        passed = []
        for j, (px, py) in enumerate(chips):
            slot = 2 * px + py
            copy(j, src_ref.at[mine], slot, mine, (px, py, c)).wait_recv()
            cp = copy(3 + j, out_ref.at[slot, mine], slot, mine, (x, y, 1 - c))
            cp.start()
            passed.append(cp)
        for j, (px, py) in enumerate(chips):
            slot = 2 * px + py
            copy(3 + j, out_ref.at[slot, other], slot, other, (x, y, 1 - c)).wait_recv()
        for cp in first + passed:
            cp.wait_send()

    return pl.pallas_call(
        body, name=name, in_specs=[ANY], out_specs=ANY,
        out_shape=jax.ShapeDtypeStruct((4, rows, cols), pack.dtype),
        scratch_shapes=[pltpu.SemaphoreType.DMA((6,)), pltpu.SemaphoreType.DMA((6,))],
    )(pack)


def _comm_swap_sibling(buf, *, name):
    def body(src_ref, out_ref, send_sem, recv_sem):
        x, y, c, _ = _place()
        cp = pltpu.make_async_remote_copy(src_ref=src_ref, dst_ref=out_ref, send_sem=send_sem, recv_sem=recv_sem,
                                          device_id=(x, y, 1 - c), device_id_type=MESH)
        cp.start()
        cp.wait()

    return pl.pallas_call(
        body, name=name, in_specs=[ANY], out_specs=ANY, out_shape=jax.ShapeDtypeStruct(buf.shape, buf.dtype),
        scratch_shapes=[pltpu.SemaphoreType.DMA, pltpu.SemaphoreType.DMA],
    )(buf)


def _comm_scatter_chips(parts, *, name):
    def body(src_ref, out_ref, send_sems, recv_sems):
        x, y, c, chips = _place()
        k = 2 * x + y
        sends = []
        for j, (px, py) in enumerate(chips):
            cp = pltpu.make_async_remote_copy(src_ref=src_ref.at[2 * px + py], dst_ref=out_ref.at[k], send_sem=send_sems.at[j],
                                              recv_sem=recv_sems.at[j], device_id=(px, py, c), device_id_type=MESH)
            cp.start()
            sends.append(cp)
        for j, (px, py) in enumerate(chips):
            pltpu.make_async_remote_copy(src_ref=src_ref.at[k], dst_ref=out_ref.at[2 * px + py], send_sem=send_sems.at[j],
                                         recv_sem=recv_sems.at[j], device_id=(px, py, c), device_id_type=MESH).wait_recv()
        for cp in sends:
            cp.wait_send()

    return pl.pallas_call(
        body, name=name, in_specs=[ANY], out_specs=ANY, out_shape=jax.ShapeDtypeStruct(parts.shape, parts.dtype),
        scratch_shapes=[pltpu.SemaphoreType.DMA((3,)), pltpu.SemaphoreType.DMA((3,))],
    )(parts)


def _comm_join_halves(halves, *, name):
    def body(src_ref, out_ref, send_sem, recv_sem):
        x, y, c, _ = _place()
        cp = pltpu.make_async_remote_copy(src_ref=src_ref.at[c], dst_ref=out_ref.at[c], send_sem=send_sem, recv_sem=recv_sem,
                                          device_id=(x, y, 1 - c), device_id_type=MESH)
        cp.start()
        pltpu.make_async_remote_copy(src_ref=src_ref.at[c], dst_ref=out_ref.at[1 - c], send_sem=send_sem, recv_sem=recv_sem,
                                     device_id=(x, y, 1 - c), device_id_type=MESH).wait_recv()
        cp.wait_send()

    return pl.pallas_call(
        body, name=name, in_specs=[ANY], out_specs=ANY, out_shape=jax.ShapeDtypeStruct(halves.shape, halves.dtype),
        input_output_aliases={0: 0},
        scratch_shapes=[pltpu.SemaphoreType.DMA, pltpu.SemaphoreType.DMA],
    )(halves)


def _comm_allreduce_small(part, *, name):
    rows, cols = part.shape

    def body(p_ref, o_ref, buf, send_sems, recv_sems):
        x, y, c, _ = _place()
        me = 4 * x + 2 * y + c
        buf[me] = p_ref[...]
        flip = lambda v, bit: 1 - v if bit else v
        peers = [(flip(x, d & 4), flip(y, d & 2), flip(c, d & 1)) for d in range(1, 8)]
        sends = []
        for j, peer in enumerate(peers):
            cp = pltpu.make_async_remote_copy(src_ref=buf.at[me], dst_ref=buf.at[me], send_sem=send_sems.at[j],
                                              recv_sem=recv_sems.at[j], device_id=peer, device_id_type=MESH)
            cp.start()
            sends.append(cp)
        for j, (px, py, pc) in enumerate(peers):
            pltpu.make_async_remote_copy(src_ref=buf.at[me], dst_ref=buf.at[4 * px + 2 * py + pc], send_sem=send_sems.at[j],
                                         recv_sem=recv_sems.at[j], device_id=(px, py, pc), device_id_type=MESH).wait_recv()
        for cp in sends:
            cp.wait_send()
        total = buf[0]
        for i in range(1, 8):
            total = total + buf[i]
        o_ref[...] = total

    vm = pl.BlockSpec(memory_space=pltpu.VMEM)
    return pl.pallas_call(
        body, name=name, in_specs=[vm], out_specs=vm, out_shape=jax.ShapeDtypeStruct((rows, cols), F32),
        scratch_shapes=[pltpu.VMEM((8, rows, cols), F32), pltpu.SemaphoreType.DMA((7,)), pltpu.SemaphoreType.DMA((7,))],
    )(part)


def _pad_in_cols(w):
    qa, ka, va, ga, cq, ckv, kr, gb = jnp.split(w, (1024, 1152, 1280, 2304, 2688, 2944, 3008), axis=-1)
    return jnp.concatenate([qa, ga, gb, cq, ka, ckv, va, kr, jnp.zeros_like(kr)], axis=-1)


def _unpad_in_tiles(qa, ga, gb, mixed):
    cq, ka, ckv, va, kr = (mixed[..., o - CQ_OFF:o - CQ_OFF + n] for o, n in (
        (CQ_OFF, Q_RANK), (KA_OFF, 128), (CKV_OFF, KV_RANK), (VA_OFF, 128), (KR_OFF, 64)))
    return jnp.concatenate([qa, ka, va, ga, cq, ckv, kr, gb], axis=-1)


def _pad_q_cols(w):
    w = w.reshape(w.shape[:-1] + (MLA_HEADS, 192))
    return jnp.pad(w, [(0, 0)] * (w.ndim - 1) + [(0, 64)]).reshape(w.shape[:-2] + (MLA_HEADS * 256,))


def _unpad_q_cols(w):
    return w.reshape(w.shape[:-1] + (MLA_HEADS, 256))[..., :192].reshape(w.shape[:-1] + (MLA_HEADS * 192,))


def _perm_kv_cols(w):
    w = w.reshape(w.shape[:-1] + (MLA_HEADS, 2, 128))
    return jnp.swapaxes(w, -3, -2).reshape(w.shape[:-3] + (2048,))


def _unperm_kv_cols(w):
    w = w.reshape(w.shape[:-1] + (2, MLA_HEADS, 128))
    return jnp.swapaxes(w, -3, -2).reshape(w.shape[:-3] + (2048,))


def _pack(w_in, w_q_b, w_kv_b, w_out):
    lead = w_in.shape[:-2]
    parts = [a.reshape(lead + (-1, LANES)) for a in (w_in, w_q_b, w_kv_b, w_out)]
    out = jnp.concatenate(parts, axis=-2)
    return out.reshape(lead[:-1] + (-1, LANES))


def _unpack(p, depth, shapes):
    lead = p.shape[:-2]
    p = p.reshape(lead + (depth, -1, LANES))
    outs, at = [], 0
    for shp in shapes:
        r = shp[0] * shp[1] // LANES
        outs.append(p[..., at:at + r, :].reshape(lead + (depth,) + tuple(shp)))
        at += r
    return outs


def _rope_tables(s):
    pos = jnp.arange(s, dtype=F32)
    inv_freq = 10000.0 ** (-jnp.arange(0, 64, 2, dtype=F32) / 64)
    ang = pos[:, None] * inv_freq[None, :]
    cos, sin = jnp.cos(ang), jnp.sin(ang)
    z64 = jnp.zeros((s, 64), F32)
    tk_c = jnp.concatenate([cos, cos, z64], axis=-1)
    tk_s = jnp.concatenate([-sin, sin, z64], axis=-1)
    tq_c = jnp.concatenate([jnp.ones((s, 128), F32), tk_c], axis=-1)
    tq_s = jnp.concatenate([jnp.zeros((s, 128), F32), tk_s], axis=-1)
    return tq_c, tq_s, tk_c, tk_s


def _device_step(xs, tgt, attn_g, sinks, gq, gkv, final_g, w_in_p, w_q_p, w_kv_p, w_o):
    depth = len(w_in_p)
    s = xs.shape[0]
    tabs = _rope_tables(s)
    saved = []
    x = xs
    for l in range(depth):
        h = _rmsnorm_fwd(x, attn_g[l:l + 1], name=f"norm_fwd{l}")
        proj = _matmul(h, w_in_p[l], name=f"in_proj{l}")
        oa, lse_a = _swa_fwd(proj, sinks[l], name=f"swa_fwd{l}")
        qcat, kcat, v, cqn, ckvn = _mla_qkv_fwd(proj, gq[l:l + 1], gkv[l:l + 1], w_q_p[l], w_kv_p[l], *tabs,
                                                name=f"mla_qkv_fwd{l}")
        ob, lse_b = _mla_fwd(qcat, kcat, v, name=f"mla_fwd{l}")
        y = _gate_fwd(oa, ob, proj, name=f"gate_fwd{l}")
        x_next = _matmul(y, w_o[l], add=x, name=f"out_proj{l}")
        saved.append((x, h, proj, oa, lse_a, qcat, kcat, v, cqn, ckvn, ob, lse_b, y))
        x = x_next

    dx, d_final_g, loss = _final_loss(x, final_g, tgt, name="final_loss")

    d_attn_g, d_sinks, d_gq, d_gkv = [None] * depth, [None] * depth, [None] * depth, [None] * depth
    d_w_in, d_w_q, d_w_kv, d_w_o = [None] * depth, [None] * depth, [None] * depth, [None] * depth
    for l in reversed(range(depth)):
        x, h, proj, oa, lse_a, qcat, kcat, v, cqn, ckvn, ob, lse_b, y = saved[l]
        dy = _matmul(dx, w_o[l], tb=True, name=f"out_proj_dx{l}")
        d_w_o[l] = _matmul(y, dx, ta=True, name=f"out_proj_dw{l}")
        doa, dob, dga, dgb, delta_b = _gate_bwd(dy, oa, ob, proj, name=f"gate_bwd{l}")
        dqa, dka, dva, dsink = _swa_bwd(proj, sinks[l], lse_a, doa, name=f"swa_bwd{l}")
        dqc, dkc, dv = _mla_bwd(qcat, kcat, v, dob, lse_b, delta_b, name=f"mla_bwd{l}")
        mixed, d_w_q[l], d_w_kv[l], dgq_l, dgkv_l = _mla_qkv_bwd(
            proj, cqn, ckvn, dqc, dkc, dv, dka, dva, gq[l:l + 1], gkv[l:l + 1], w_q_p[l], w_kv_p[l], *tabs,
            name=f"mla_qkv_bwd{l}")
        dproj = [dqa, dga, dgb, mixed]
        dh = _matmul_ktiles_nt(dproj, w_in_p[l], name=f"in_proj_dx{l}")
        d_w_in[l] = [_matmul(h, tile, ta=True, name=f"in_proj_dw{l}_{j}") for j, tile in enumerate(dproj)]
        dx, dg_l = _rmsnorm_bwd(dh, x, attn_g[l:l + 1], dx, name=f"norm_bwd{l}")
        d_attn_g[l], d_sinks[l], d_gq[l], d_gkv[l] = dg_l, dsink[0:1, :SWA_HEADS], dgq_l, dgkv_l

    cat = lambda parts: jnp.concatenate(parts, axis=0)
    return (loss, dx, cat(d_attn_g), cat(d_sinks), cat(d_gq), cat(d_gkv), d_final_g, d_w_in, d_w_q, d_w_kv, d_w_o)


def kernel(x, attn_norm_g, w_in, swa_sinks, q_a_norm_g, kv_a_norm_g, w_q_b, w_kv_b, w_out, final_norm_g, loss_target, m_attn_norm_g, m_w_in, m_swa_sinks, m_q_a_norm_g, m_kv_a_norm_g, m_w_q_b, m_w_kv_b, m_w_out, m_final_norm_g, v_attn_norm_g, v_w_in, v_swa_sinks, v_q_a_norm_g, v_kv_a_norm_g, v_w_q_b, v_w_kv_b, v_w_out, v_final_norm_g):
    depth = w_in.shape[0]
    shard_shapes = [w_in.shape[1:], w_q_b.shape[1:], w_kv_b.shape[1:], w_out.shape[1:]]
    c = lax.axis_index("c")
    chip = 2 * lax.axis_index("x") + lax.axis_index("y")

    pack = _pack(w_in, w_q_b, w_kv_b, w_out).astype(WIRE_DTYPE)
    gathered = lax.dynamic_update_index_in_dim(_comm_gather_weights(pack, name="comm_gather_weights"), pack, chip, 0)
    g_in, g_q, g_kv, g_o = _unpack(gathered, depth, shard_shapes)
    cols = lambda a: jnp.moveaxis(a, 0, 2).reshape(a.shape[1:3] + (-1,))
    full_in, full_q, full_kv = _pad_in_cols(cols(g_in)), _pad_q_cols(cols(g_q)), _perm_kv_cols(cols(g_kv))
    full_o = jnp.moveaxis(g_o, 0, 1).reshape(depth, -1, g_o.shape[-1])
    per_layer = lambda a: [a[l] for l in range(depth)]

    (loss, dx, d_attn_g, d_sinks, d_gq, d_gkv, d_final_g, d_w_in, d_w_q, d_w_kv, d_w_o) = _device_step(
        x[0], loss_target[0], attn_norm_g, swa_sinks, q_a_norm_g, kv_a_norm_g, final_norm_g.reshape(1, -1),
        per_layer(full_in), per_layer(full_q), per_layer(full_kv), per_layer(full_o))

    split_cols = lambda a, n: jnp.moveaxis(a.reshape(a.shape[:2] + (4, n)), 2, 0)
    in_tiles = [jnp.stack([d_w_in[l][j] for l in range(depth)]) for j in range(4)]
    gs_in = split_cols(_unpad_in_tiles(*in_tiles), shard_shapes[0][1])
    gs_q = split_cols(_unpad_q_cols(jnp.stack(d_w_q)), shard_shapes[1][1])
    gs_kv = split_cols(_unperm_kv_cols(jnp.stack(d_w_kv)), shard_shapes[2][1])
    gs_o = jnp.moveaxis(jnp.stack(d_w_o).reshape(depth, 4, -1, D_MODEL), 1, 0)
    gpack = _pack(gs_in, gs_q, gs_kv, gs_o)
    rh = gpack.shape[1] // 2
    halves = gpack.reshape(4, 2, rh, LANES)
    give = lax.dynamic_index_in_dim(halves, 1 - c, axis=1, keepdims=False).astype(WIRE_DTYPE)
    recv = _comm_swap_sibling(give, name="comm_swap_sibling")
    x_, y_ = lax.axis_index("x"), lax.axis_index("y")
    where = jnp.stack([chip, 2 * (1 - x_) + y_, 2 * x_ + 1 - y_, 2 * (1 - x_) + 1 - y_, c]).astype(jnp.int32)
    pair = _pair_sum(where, halves, recv, name="pair_sum")
    parts = _comm_scatter_chips(pair, name="comm_scatter_chips")
    mine = _chip_sum(where, halves, recv, parts, name="chip_sum")
    reduced = _comm_join_halves(mine, name="comm_join_halves").reshape(2 * rh, LANES)
    g_w_in, g_w_q_b, g_w_kv_b, g_w_out = _unpack(reduced, depth, shard_shapes)

    small = [d_attn_g, d_sinks, d_gq, d_gkv, d_final_g, loss[:, :1]]
    flat = jnp.concatenate([a.reshape(-1) for a in small])
    n_small = flat.shape[0]
    rows = -(-n_small // 1024) * 8
    total = _comm_allreduce_small(jnp.pad(flat, (0, rows * 128 - n_small)).reshape(rows, 128),
                                  name="comm_allreduce_small").reshape(-1)
    outs, at = [], 0
    for a in small:
        outs.append(total[at:at + a.size].reshape(a.shape))
        at += a.size
    g_attn_g, g_sinks, g_gq, g_gkv, g_final_g, loss_total = outs
    g_final_g = g_final_g.reshape(final_norm_g.shape)

    weights = [attn_norm_g, w_in, swa_sinks, q_a_norm_g, kv_a_norm_g, w_q_b, w_kv_b, w_out, final_norm_g]
    grads = [g_attn_g, g_w_in, g_sinks, g_gq, g_gkv, g_w_q_b, g_w_kv_b, g_w_out, g_final_g]
    ms = [m_attn_norm_g, m_w_in, m_swa_sinks, m_q_a_norm_g, m_kv_a_norm_g, m_w_q_b, m_w_kv_b, m_w_out, m_final_norm_g]
    vs = [v_attn_norm_g, v_w_in, v_swa_sinks, v_q_a_norm_g, v_kv_a_norm_g, v_w_q_b, v_w_kv_b, v_w_out, v_final_norm_g]
    as2d = lambda a: a.reshape(1, -1) if a.ndim == 1 else a
    deltas, new_m, new_v = [], [], []
    for i, (w, g, m, v) in enumerate(zip(weights, grads, ms, vs)):
        d, mn, vn = _adamw(as2d(w), as2d(g), as2d(m), as2d(v), name=f"adamw{i}")
        deltas.append(d.reshape(w.shape))
        new_m.append(mn.reshape(w.shape))
        new_v.append(vn.reshape(w.shape))

    return (loss_total.reshape(()), dx[None], *grads, *deltas, *new_m, *new_v)
```

```python
import functools
import math

import jax
import jax.numpy as jnp
from jax import lax
from jax.experimental import pallas as pl
from jax.experimental.pallas import tpu as pltpu

F32 = jnp.float32
MXU_DTYPE = jnp.bfloat16
WIRE_DTYPE = jnp.bfloat16

EPS = 1e-6
NEG = -1e30
BLOCK = 128
D_MODEL = 2048
SWA_HEADS = 16
MLA_HEADS = 8
Q_RANK = 384
KV_RANK = 256
IN_WIDTH = 4032
MLA_SCALE = 192 ** -0.5
SWA_SCALE = 64 ** -0.5
LOG2E = math.log2(math.e)
HEADS_PER_STEP = 2
SLOPES = tuple(2.0 ** (-8.0 * (h + 1) / SWA_HEADS) for h in range(SWA_HEADS))

P_WIDTH = 4096
QA_OFF, GA_OFF, GB_OFF, CQ_OFF, KA_OFF, CKV_OFF, VA_OFF, KR_OFF = 0, 1024, 2048, 3072, 3456, 3584, 3840, 3968

ADAM_LR, ADAM_B1, ADAM_B2, ADAM_EPS, ADAM_WD, ADAM_STEP = 0.001, 0.9, 0.999, 1e-08, 0.01, 10

VMEM_LIMIT = 56 * 1024 * 1024
MESH = pl.DeviceIdType.MESH
ANY = pl.BlockSpec(memory_space=pl.ANY)


def _cparams(*sem):
    return pltpu.CompilerParams(dimension_semantics=sem, vmem_limit_bytes=VMEM_LIMIT)


def _dot(a, b, ca, cb):
    return lax.dot_general(a, b, (((ca,), (cb,)), ((), ())), preferred_element_type=F32)


def _matmul(a, b, *, name, ta=False, tb=False, out_dtype=F32, add=None, tm=1024, tn=1024, tk=1024):
    (kdim, m) = a.shape if ta else a.shape[::-1]
    (n, k2) = b.shape if tb else b.shape[::-1]
    assert kdim == k2, (a.shape, b.shape)
    tm, tn, tk = min(tm, m), min(tn, n), min(tk, kdim)
    assert m % tm == 0 and n % tn == 0 and kdim % tk == 0
    nk = kdim // tk

    def body(*refs):
        if add is None:
            a_ref, b_ref, o_ref, acc = refs
        else:
            a_ref, b_ref, add_ref, o_ref, acc = refs
        k = pl.program_id(2)

        @pl.when(k == 0)
        def _():
            acc[...] = jnp.zeros_like(acc)

        acc[...] += _dot(a_ref[...].astype(MXU_DTYPE), b_ref[...].astype(MXU_DTYPE), 0 if ta else 1, 1 if tb else 0)

        @pl.when(k == nk - 1)
        def _():
            r = acc[...]
            if add is not None:
                r = add_ref[...] + r
            o_ref[...] = r.astype(out_dtype)

    a_spec = pl.BlockSpec((tk, tm), lambda i, j, k: (k, i)) if ta else pl.BlockSpec((tm, tk), lambda i, j, k: (i, k))
    b_spec = pl.BlockSpec((tn, tk), lambda i, j, k: (j, k)) if tb else pl.BlockSpec((tk, tn), lambda i, j, k: (k, j))
    in_specs, args = [a_spec, b_spec], [a, b]
    if add is not None:
        in_specs.append(pl.BlockSpec((tm, tn), lambda i, j, k: (i, j)))
        args.append(add)
    return pl.pallas_call(
        body, name=name, grid=(m // tm, n // tn, nk), in_specs=in_specs,
        out_specs=pl.BlockSpec((tm, tn), lambda i, j, k: (i, j)),
        out_shape=jax.ShapeDtypeStruct((m, n), out_dtype),
        scratch_shapes=[pltpu.VMEM((tm, tn), F32)],
        compiler_params=_cparams("parallel", "parallel", "arbitrary"),
    )(*args)


def _matmul_ktiles(a_tiles, b, *, name, tm=512, tn=1024):
    m, kt = a_tiles[0].shape
    n = b.shape[1]
    nt = len(a_tiles)
    assert b.shape[0] == nt * kt
    tm, tn = min(tm, m), min(tn, n)
    assert m % tm == 0 and n % tn == 0

    def body(*refs):
        a_refs, b_refs, o_ref = refs[:nt], refs[nt:2 * nt], refs[2 * nt]
        acc = _dot(a_refs[0][...].astype(MXU_DTYPE), b_refs[0][...].astype(MXU_DTYPE), 1, 0)
        for j in range(1, nt):
            acc += _dot(a_refs[j][...].astype(MXU_DTYPE), b_refs[j][...].astype(MXU_DTYPE), 1, 0)
        o_ref[...] = acc

    in_specs = [pl.BlockSpec((tm, kt), lambda i, jn: (i, 0))] * nt
    in_specs += [pl.BlockSpec((kt, tn), lambda i, jn, j=j: (j, jn)) for j in range(nt)]
    return pl.pallas_call(
        body, name=name, grid=(m // tm, n // tn), in_specs=in_specs,
        out_specs=pl.BlockSpec((tm, tn), lambda i, jn: (i, jn)),
        out_shape=jax.ShapeDtypeStruct((m, n), F32),
        compiler_params=_cparams("parallel", "parallel"),
    )(*a_tiles, *([b] * nt))


def _rmsnorm_fwd(x, g, *, name):
    s, d = x.shape
    tm = min(512, s)

    def body(x_ref, g_ref, h_ref):
        xv = x_ref[...]
        r = lax.rsqrt(jnp.mean(xv * xv, axis=-1, keepdims=True) + EPS)
        h_ref[...] = (xv * r * g_ref[...]).astype(MXU_DTYPE)

    return pl.pallas_call(
        body, name=name, grid=(s // tm,),
        in_specs=[pl.BlockSpec((tm, d), lambda i: (i, 0)), pl.BlockSpec((1, d), lambda i: (0, 0))],
        out_specs=pl.BlockSpec((tm, d), lambda i: (i, 0)),
        out_shape=jax.ShapeDtypeStruct((s, d), MXU_DTYPE),
        compiler_params=_cparams("parallel"),
    )(x, g)


def _rmsnorm_bwd(dh, x, g, dres, *, name):
    s, d = x.shape
    tm = min(512, s)

    def body(dh_ref, x_ref, g_ref, dres_ref, dx_ref, dg_ref):
        @pl.when(pl.program_id(0) == 0)
        def _():
            dg_ref[...] = jnp.zeros_like(dg_ref)

        xv = x_ref[...]
        r = lax.rsqrt(jnp.mean(xv * xv, axis=-1, keepdims=True) + EPS)
        xn = xv * r
        dy = dh_ref[...]
        dg_ref[...] += jnp.sum(dy * xn, axis=0, keepdims=True)
        u = dy * g_ref[...]
        dx_ref[...] = dres_ref[...] + r * (u - xn * jnp.mean(u * xn, axis=-1, keepdims=True))

    row = pl.BlockSpec((tm, d), lambda i: (i, 0))
    vec = pl.BlockSpec((1, d), lambda i: (0, 0))
    return pl.pallas_call(
        body, name=name, grid=(s // tm,), in_specs=[row, row, vec, row], out_specs=[row, vec],
        out_shape=[jax.ShapeDtypeStruct((s, d), F32), jax.ShapeDtypeStruct((1, d), F32)],
        compiler_params=_cparams("arbitrary"),
    )(dh, x, g, dres)


def _final_loss(x, g, tgt, *, name):
    s, d = x.shape
    tm = min(512, s)

    def body(x_ref, g_ref, t_ref, dx_ref, dg_ref, loss_ref):
        @pl.when(pl.program_id(0) == 0)
        def _():
            dg_ref[...] = jnp.zeros_like(dg_ref)
            loss_ref[...] = jnp.zeros_like(loss_ref)

        xv = x_ref[...]
        gv = g_ref[...]
        r = lax.rsqrt(jnp.mean(xv * xv, axis=-1, keepdims=True) + EPS)
        xn = xv * r
        err = xn * gv - t_ref[...]
        sq = jnp.sum(jnp.sum(err * err, axis=-1, keepdims=True), axis=0, keepdims=True)
        loss_ref[...] += (0.5 / d) * sq
        dy = err * (1.0 / d)
        dg_ref[...] += jnp.sum(dy * xn, axis=0, keepdims=True)
        u = dy * gv
        dx_ref[...] = r * (u - xn * jnp.mean(u * xn, axis=-1, keepdims=True))

    row = pl.BlockSpec((tm, d), lambda i: (i, 0))
    vec = pl.BlockSpec((1, d), lambda i: (0, 0))
    return pl.pallas_call(
        body, name=name, grid=(s // tm,), in_specs=[row, vec, row],
        out_specs=[row, vec, pl.BlockSpec((1, 128), lambda i: (0, 0))],
        out_shape=[jax.ShapeDtypeStruct((s, d), F32), jax.ShapeDtypeStruct((1, d), F32),
                   jax.ShapeDtypeStruct((1, 128), F32)],
        compiler_params=_cparams("arbitrary"),
    )(x, g, tgt)


def _swa_keys(kp_ref, kc_ref):
    kk = jnp.concatenate([kp_ref[...], kc_ref[...]], axis=0)
    kr = pltpu.roll(kk, 64, 1)
    lo = lax.broadcasted_iota(jnp.int32, kk.shape, 1) < 64
    return [jnp.where(lo, kk, kr).astype(MXU_DTYPE), jnp.where(lo, kr, kk).astype(MXU_DTYPE)]


def _swa_mask(n):
    qi = lax.broadcasted_iota(jnp.int32, (BLOCK, 2 * BLOCK), 0)
    ki = lax.broadcasted_iota(jnp.int32, (BLOCK, 2 * BLOCK), 1)
    delta = BLOCK + qi - ki
    valid = (delta >= 0) & (delta < BLOCK) & ((ki >= BLOCK) | (n > 0))
    return valid, delta.astype(F32)


def _swa_scores(qm, keys, h, valid, deltaf):
    s = _dot(qm, keys, 1, 1) * SWA_SCALE
    return jnp.where(valid, s - SLOPES[h] * deltaf, NEG)


def _swa_specs(nb):
    kcol, vcol = KA_OFF // BLOCK, VA_OFF // BLOCK
    last = nb - 1
    cur = lambda n: jnp.minimum(n, last)
    prev = lambda n: jnp.maximum(jnp.minimum(n, last) - 1, 0)
    return [
        pl.BlockSpec(memory_space=pltpu.SMEM),
        pl.BlockSpec((BLOCK, 1024), lambda n: (cur(n), QA_OFF // 1024)),
        pl.BlockSpec((BLOCK, BLOCK), lambda n: (cur(n), kcol)),
        pl.BlockSpec((BLOCK, BLOCK), lambda n: (prev(n), kcol)),
        pl.BlockSpec((BLOCK, BLOCK), lambda n: (cur(n), vcol)),
        pl.BlockSpec((BLOCK, BLOCK), lambda n: (prev(n), vcol)),
    ]


def _swa_fwd(proj, sinks, *, name):
    s = proj.shape[0]
    nb = s // BLOCK

    def body(sink_ref, q_ref, kc_ref, kp_ref, vc_ref, vp_ref, o_ref, lse_ref):
        n = pl.program_id(0)
        keys = _swa_keys(kp_ref, kc_ref)
        vals = _swa_keys(vp_ref, vc_ref)
        valid, deltaf = _swa_mask(n)
        lane = lax.broadcasted_iota(jnp.int32, (BLOCK, BLOCK), 1)
        lo = lane < 64
        lse_acc = jnp.zeros((BLOCK, BLOCK), F32)
        for i in range(SWA_HEADS // 2):
            j = i // 4
            qp = q_ref[:, i * 128:(i + 1) * 128]
            outs = []
            for half in range(2):
                h = 2 * i + half
                qm = jnp.where(lo if half == 0 else ~lo, qp, 0.0).astype(MXU_DTYPE)
                sc = _swa_scores(qm, keys[j], h, valid, deltaf)
                sink = sink_ref[h]
                m = jnp.maximum(jnp.max(sc, axis=-1, keepdims=True), sink)
                p = jnp.exp(sc - m)
                l = jnp.sum(p, axis=-1, keepdims=True) + jnp.exp(sink - m)
                outs.append(jnp.dot((p / l).astype(MXU_DTYPE), vals[j], preferred_element_type=F32))
                lse_acc = jnp.where(lane == h, m + jnp.log(l), lse_acc)
            o_ref[:, i * 128:(i + 1) * 128] = jnp.where(lo, outs[0], outs[1])
        lse_ref[...] = lse_acc

    return pl.pallas_call(
        body, name=name, grid=(nb,), in_specs=_swa_specs(nb),
        out_specs=[pl.BlockSpec((BLOCK, 1024), lambda n: (n, 0)), pl.BlockSpec((BLOCK, BLOCK), lambda n: (n, 0))],
        out_shape=[jax.ShapeDtypeStruct((s, 1024), F32), jax.ShapeDtypeStruct((s, BLOCK), F32)],
        compiler_params=_cparams("parallel"),
    )(sinks, proj, proj, proj, proj, proj)


def _swa_bwd(proj, sinks, lse, do, *, name):
    s = proj.shape[0]
    nb = s // BLOCK
    last = nb - 1

    def body(sink_ref, q_ref, kc_ref, kp_ref, vc_ref, vp_ref, lse_ref, do_ref,
             dq_ref, dk_ref, dv_ref, dsink_ref, carry_k, carry_v):
        n = pl.program_id(0)

        @pl.when(n == 0)
        def _():
            carry_k[...] = jnp.zeros_like(carry_k)
            carry_v[...] = jnp.zeros_like(carry_v)
            dsink_ref[...] = jnp.zeros_like(dsink_ref)

        @pl.when(n < nb)
        def _():
            keys = _swa_keys(kp_ref, kc_ref)
            vals = _swa_keys(vp_ref, vc_ref)
            valid, deltaf = _swa_mask(n)
            lane = lax.broadcasted_iota(jnp.int32, (BLOCK, BLOCK), 1)
            lane1 = lax.broadcasted_iota(jnp.int32, (1, BLOCK), 1)
            lo = lane < 64
            lse_blk = lse_ref[...]
            acc_k = [jnp.zeros((2 * BLOCK, BLOCK), F32) for _ in range(2)]
            acc_v = [jnp.zeros((2 * BLOCK, BLOCK), F32) for _ in range(2)]
            dsink = jnp.zeros((1, BLOCK), F32)
            for i in range(SWA_HEADS // 2):
                j = i // 4
                qp = q_ref[:, i * 128:(i + 1) * 128]
                dop = do_ref[:, i * 128:(i + 1) * 128].astype(F32)
                dqs = []
                for half in range(2):
                    h = 2 * i + half
                    msk = lo if half == 0 else ~lo
                    qm = jnp.where(msk, qp, 0.0).astype(MXU_DTYPE)
                    dom = jnp.where(msk, dop, 0.0).astype(MXU_DTYPE)
                    sc = _swa_scores(qm, keys[j], h, valid, deltaf)
                    lse_h = jnp.sum(jnp.where(lane == h, lse_blk, 0.0), axis=-1, keepdims=True)
                    p = jnp.exp(sc - lse_h)
                    dp = _dot(dom, vals[j], 1, 1)
                    dlt = jnp.sum(dp * p, axis=-1, keepdims=True)
                    ds = (p * (dp - dlt) * SWA_SCALE).astype(MXU_DTYPE)
                    psink = jnp.exp(sink_ref[h] - lse_h)
                    dsink = jnp.where(lane1 == h, -jnp.sum(psink * dlt, axis=0, keepdims=True), dsink)
                    dqs.append(jnp.dot(ds, keys[j], preferred_element_type=F32))
                    acc_k[j] += _dot(ds, qm, 0, 0)
                    acc_v[j] += _dot(p.astype(MXU_DTYPE), dom, 0, 0)
                dq_ref[:, i * 128:(i + 1) * 128] = jnp.where(lo, dqs[0], dqs[1]).astype(dq_ref.dtype)
            lo2 = lax.broadcasted_iota(jnp.int32, (2 * BLOCK, BLOCK), 1) < 64
            fold = lambda acc: jnp.where(lo2, acc[0] + pltpu.roll(acc[0], 64, 1), acc[1] + pltpu.roll(acc[1], 64, 1))
            dkk, dvv = fold(acc_k), fold(acc_v)
            dk_ref[...] = (carry_k[...] + dkk[:BLOCK]).astype(dk_ref.dtype)
            dv_ref[...] = (carry_v[...] + dvv[:BLOCK]).astype(dv_ref.dtype)
            carry_k[...] = dkk[BLOCK:]
            carry_v[...] = dvv[BLOCK:]
            dsink_ref[...] += jnp.broadcast_to(dsink, dsink_ref.shape)

        @pl.when(n == nb)
        def _():
            dk_ref[...] = carry_k[...].astype(dk_ref.dtype)
            dv_ref[...] = carry_v[...].astype(dv_ref.dtype)

    cur = lambda n: jnp.minimum(n, last)
    lag = lambda n: jnp.maximum(n - 1, 0)
    return pl.pallas_call(
        body, name=name, grid=(nb + 1,),
        in_specs=_swa_specs(nb) + [pl.BlockSpec((BLOCK, BLOCK), lambda n: (cur(n), 0)),
                                   pl.BlockSpec((BLOCK, 1024), lambda n: (cur(n), 0))],
        out_specs=[pl.BlockSpec((BLOCK, 1024), lambda n: (cur(n), 0)),
                   pl.BlockSpec((BLOCK, BLOCK), lambda n: (lag(n), 0)),
                   pl.BlockSpec((BLOCK, BLOCK), lambda n: (lag(n), 0)),
                   pl.BlockSpec((8, BLOCK), lambda n: (0, 0))],
        out_shape=[jax.ShapeDtypeStruct((s, 1024), MXU_DTYPE), jax.ShapeDtypeStruct((s, BLOCK), MXU_DTYPE),
                   jax.ShapeDtypeStruct((s, BLOCK), MXU_DTYPE), jax.ShapeDtypeStruct((8, BLOCK), F32)],
        scratch_shapes=[pltpu.VMEM((BLOCK, BLOCK), F32), pltpu.VMEM((BLOCK, BLOCK), F32)],
        compiler_params=_cparams("arbitrary"),
    )(sinks, proj, proj, proj, proj, proj, lse, do)


def _rope_partner(v, first, width):
    lane = lax.broadcasted_iota(jnp.int32, v.shape, 1)
    in_a = (lane >= first) & (lane < first + 32)
    in_b = (lane >= first + 32) & (lane < first + 64)
    return jnp.where(in_a, pltpu.roll(v, width - 32, 1), jnp.where(in_b, pltpu.roll(v, 32, 1), 0.0))


def _mla_qkv_fwd(proj, gq, gkv, wq, wkv, tq_c, tq_s, tk_c, tk_s, *, name):
    s = proj.shape[0]
    tm = min(256, s)

    def body(cq_ref, ckv_ref, kr_ref, gq_ref, gkv_ref, wq_ref, wkv_ref, qc_ref, qs_ref, kc_ref, ks_ref,
             qcat_ref, kcat_ref, v_ref, cqn_ref, ckvn_ref):
        cq = cq_ref[...]
        cqn = (cq * lax.rsqrt(jnp.mean(cq * cq, axis=-1, keepdims=True) + EPS) * gq_ref[...]).astype(MXU_DTYPE)
        cqn_ref[...] = cqn
        qpre = _dot(cqn, wq_ref[...], 1, 1)
        qc, qs = qc_ref[...], qs_ref[...]
        for hh in range(MLA_HEADS):
            blk = qpre[:, hh * 256:(hh + 1) * 256]
            qcat_ref[:, hh * 256:(hh + 1) * 256] = (blk * qc + _rope_partner(blk, 128, 256) * qs).astype(MXU_DTYPE)
        ckv = ckv_ref[...]
        ckvn = (ckv * lax.rsqrt(jnp.mean(ckv * ckv, axis=-1, keepdims=True) + EPS) * gkv_ref[...]).astype(MXU_DTYPE)
        ckvn_ref[...] = ckvn
        kv = _dot(ckvn, wkv_ref[...], 1, 1)
        kr = kr_ref[...]
        krr = (kr * kc_ref[...] + _rope_partner(kr, 0, 128) * ks_ref[...]).astype(MXU_DTYPE)
        for hh in range(MLA_HEADS):
            kcat_ref[:, hh * 256:hh * 256 + 128] = kv[:, hh * 128:(hh + 1) * 128].astype(MXU_DTYPE)
            kcat_ref[:, hh * 256 + 128:(hh + 1) * 256] = krr
            v_ref[:, hh * 256:hh * 256 + 128] = kv[:, 1024 + hh * 128:1024 + (hh + 1) * 128].astype(MXU_DTYPE)
            v_ref[:, hh * 256 + 128:(hh + 1) * 256] = jnp.ones((tm, 128), MXU_DTYPE)

    row = lambda w, c: pl.BlockSpec((tm, w), lambda i: (i, c))
    full = lambda a: pl.BlockSpec(a.shape, lambda i: (0, 0))
    return pl.pallas_call(
        body, name=name, grid=(s // tm,),
        in_specs=[row(Q_RANK, CQ_OFF // Q_RANK), row(KV_RANK, CKV_OFF // KV_RANK), row(128, KR_OFF // 128),
                  full(gq), full(gkv), full(wq), full(wkv), row(256, 0), row(256, 0), row(128, 0), row(128, 0)],
        out_specs=[row(2048, 0), row(2048, 0), row(2048, 0), row(Q_RANK, 0), row(KV_RANK, 0)],
        out_shape=[jax.ShapeDtypeStruct((s, 2048), MXU_DTYPE), jax.ShapeDtypeStruct((s, 2048), MXU_DTYPE),
                   jax.ShapeDtypeStruct((s, 2048), MXU_DTYPE), jax.ShapeDtypeStruct((s, Q_RANK), MXU_DTYPE),
                   jax.ShapeDtypeStruct((s, KV_RANK), MXU_DTYPE)],
        compiler_params=_cparams("parallel"),
    )(proj, proj, proj, gq, gkv, wq, wkv, tq_c, tq_s, tk_c, tk_s)


def _norm_bwd(x, g, dy):
    r = lax.rsqrt(jnp.mean(x * x, axis=-1, keepdims=True) + EPS)
    xn = x * r
    u = dy * g
    return r * (u - xn * jnp.mean(u * xn, axis=-1, keepdims=True)), jnp.sum(dy * xn, axis=0, keepdims=True)


def _mla_qkv_bwd(proj, cqn, ckvn, dqcat, dkcat, dv, dka, dva, gq, gkv, wq, wkv, tq_c, tq_s, tk_c, tk_s, *, name):
    s = proj.shape[0]
    tm = min(256, s)
    t_cq, t_ka, t_ckv, t_va, t_kr = (o - CQ_OFF for o in (CQ_OFF, KA_OFF, CKV_OFF, VA_OFF, KR_OFF))

    def body(cq_ref, ckv_ref, cqn_ref, ckvn_ref, dq_ref, dk_ref, dv_ref, dka_ref, dva_ref, gq_ref, gkv_ref, wq_ref,
             wkv_ref, qc_ref, qs_ref, kc_ref, ks_ref,
             tile_ref, dwq_ref, dwkv_ref, dgq_ref, dgkv_ref, dqpre, dkv):
        dcq_ref = tile_ref.at[:, t_cq:t_cq + Q_RANK]
        dckv_ref = tile_ref.at[:, t_ckv:t_ckv + KV_RANK]
        dkr_ref = tile_ref.at[:, t_kr:t_kr + 128]
        tile_ref[:, t_ka:t_ka + 128] = dka_ref[...]
        tile_ref[:, t_va:t_va + 128] = dva_ref[...]

        @pl.when(pl.program_id(0) == 0)
        def _():
            for r in (dwq_ref, dwkv_ref, dgq_ref, dgkv_ref):
                r[...] = jnp.zeros_like(r)

        qc, qs = qc_ref[...], qs_ref[...]
        dkrr = jnp.zeros((tm, 128), F32)
        for hh in range(MLA_HEADS):
            blk = dq_ref[:, hh * 256:(hh + 1) * 256]
            dqpre[:, hh * 256:(hh + 1) * 256] = (blk * qc + _rope_partner(blk * qs, 128, 256)).astype(MXU_DTYPE)
            dkv[:, hh * 128:(hh + 1) * 128] = dk_ref[:, hh * 256:hh * 256 + 128].astype(MXU_DTYPE)
            dkrr = dkrr + dk_ref[:, hh * 256 + 128:(hh + 1) * 256]
        dkv[:, 1024:] = dv_ref[...].astype(MXU_DTYPE)
        dkr_ref[...] = (dkrr * kc_ref[...] + _rope_partner(dkrr * ks_ref[...], 0, 128)).astype(dkr_ref.dtype)

        dq_b = dqpre[...]
        dwq_ref[...] += _dot(dq_b, cqn_ref[...], 0, 0)
        dcq, dgq = _norm_bwd(cq_ref[...], gq_ref[...], _dot(dq_b, wq_ref[...], 1, 0))
        dcq_ref[...] = dcq.astype(dcq_ref.dtype)
        dgq_ref[...] += dgq

        dkv_b = dkv[...]
        dwkv_ref[...] += _dot(dkv_b, ckvn_ref[...], 0, 0)
        dckv, dgkv = _norm_bwd(ckv_ref[...], gkv_ref[...], _dot(dkv_b, wkv_ref[...], 1, 0))
        dckv_ref[...] = dckv.astype(dckv_ref.dtype)
        dgkv_ref[...] += dgkv

    row = lambda w, c: pl.BlockSpec((tm, w), lambda i: (i, c))
    full = lambda shape: pl.BlockSpec(shape, lambda i: (0, 0))
    return pl.pallas_call(
        body, name=name, grid=(s // tm,),
        in_specs=[row(Q_RANK, CQ_OFF // Q_RANK), row(KV_RANK, CKV_OFF // KV_RANK), row(Q_RANK, 0), row(KV_RANK, 0),
                  row(2048, 0), row(2048, 0), row(1024, 0), row(128, 0), row(128, 0), full(gq.shape), full(gkv.shape),
                  full(wq.shape), full(wkv.shape), row(256, 0), row(256, 0), row(128, 0), row(128, 0)],
        out_specs=[row(1024, 0), full(wq.shape), full(wkv.shape), full(gq.shape), full(gkv.shape)],
        out_shape=[jax.ShapeDtypeStruct((s, 1024), MXU_DTYPE), jax.ShapeDtypeStruct(wq.shape, F32),
                   jax.ShapeDtypeStruct(wkv.shape, F32), jax.ShapeDtypeStruct(gq.shape, F32),
                   jax.ShapeDtypeStruct(gkv.shape, F32)],
        scratch_shapes=[pltpu.VMEM((tm, 2048), MXU_DTYPE), pltpu.VMEM((tm, 2048), MXU_DTYPE)],
        compiler_params=_cparams("arbitrary"),
    )(proj, proj, cqn, ckvn, dqcat, dkcat, dv, dka, dva, gq, gkv, wq, wkv, tq_c, tq_s, tk_c, tk_s)


def _causal_mask(t):
    return lax.broadcasted_iota(jnp.int32, (t, t), 1) <= lax.broadcasted_iota(jnp.int32, (t, t), 0)


def _mla_fwd(qcat, kcat, v, *, name):
    s = qcat.shape[0]
    t = min(512, s)
    nq = s // t
    hp = HEADS_PER_STEP
    ng = MLA_HEADS // hp
    c2 = MLA_SCALE * LOG2E

    def body(q_ref, k_ref, v_ref, o_ref, lse_ref, top_s, acc_s):
        qi = pl.program_id(1)
        rows = lambda j: pl.ds(pl.multiple_of(j * t, t), t)
        head = lambda e: slice(e * 256, (e + 1) * 256)
        raw = lambda e, j: _dot(q_ref[:, head(e)], k_ref[rows(j), head(e)], 1, 1)

        for e in range(hp):
            top_s[e] = jnp.where(_causal_mask(t), raw(e, qi), NEG)

        def pass1(j, carry):
            for e in range(hp):
                top_s[e] = jnp.maximum(top_s[e], raw(e, j))
            return carry

        lax.fori_loop(0, qi, pass1, 0)
        m = [jnp.max(top_s[e], axis=-1, keepdims=True) * c2 for e in range(hp)]

        def weighted(e, j, masked):
            sc = raw(e, j) * c2 - m[e]
            if masked:
                sc = jnp.where(_causal_mask(t), sc, NEG)
            return jnp.dot(jnp.exp2(sc).astype(MXU_DTYPE), v_ref[rows(j), head(e)], preferred_element_type=F32)

        for e in range(hp):
            acc_s[e] = weighted(e, qi, True)

        def pass2(j, carry):
            for e in range(hp):
                acc_s[e] += weighted(e, j, False)
            return carry

        lax.fori_loop(0, qi, pass2, 0)
        lane = lax.broadcasted_iota(jnp.int32, (t, 128), 1)
        stats = jnp.zeros((t, 128), F32)
        for e in range(hp):
            l = acc_s[e, :, 128:]
            o_ref[:, e * 128:(e + 1) * 128] = acc_s[e, :, :128] / l
            stats = jnp.where(lane == e, m[e] + jnp.log(l) * LOG2E, stats)
        lse_ref[...] = stats

    return pl.pallas_call(
        body, name=name, grid=(ng, nq),
        in_specs=[pl.BlockSpec((t, 256 * hp), lambda g, qi: (qi, g)), pl.BlockSpec((s, 256 * hp), lambda g, qi: (0, g)),
                  pl.BlockSpec((s, 256 * hp), lambda g, qi: (0, g))],
        out_specs=[pl.BlockSpec((t, 128 * hp), lambda g, qi: (qi, g)), pl.BlockSpec((t, 128), lambda g, qi: (qi, g))],
        out_shape=[jax.ShapeDtypeStruct((s, 1024), F32), jax.ShapeDtypeStruct((s, 128 * ng), F32)],
        scratch_shapes=[pltpu.VMEM((hp, t, t), F32), pltpu.VMEM((hp, t, 256), F32)],
        compiler_params=_cparams("parallel", "arbitrary"),
    )(qcat, kcat, v)


def _mla_bwd(qcat, kcat, v, do, lse, delta, *, name):
    s = qcat.shape[0]
    t = min(512, s)
    nq = s // t
    hp = HEADS_PER_STEP
    c2 = MLA_SCALE * LOG2E

    def body(q_ref, k_ref, v_ref, do_ref, lse_ref, dl_ref, dq_ref, dk_ref, dv_ref, dk_acc, dv_acc):
        h, ki = pl.program_id(0), pl.program_id(1)

        @pl.when(ki == 0)
        def _():
            dq_ref[...] = jnp.zeros_like(dq_ref)

        dk_acc[...] = jnp.zeros_like(dk_acc)
        dv_acc[...] = jnp.zeros_like(dv_acc)
        k, vv = k_ref[...], v_ref[...]
        mine = lax.broadcasted_iota(jnp.int32, (t, 128), 1) == h % hp

        def chunk(qi, masked):
            rows = pl.ds(pl.multiple_of(qi * t, t), t)
            q, dob = q_ref[rows, :], do_ref[rows, :]
            pick = lambda r: jnp.sum(jnp.where(mine, r[rows, :], 0.0), axis=-1, keepdims=True)
            sc = _dot(q, k, 1, 1) * c2
            if masked:
                sc = jnp.where(_causal_mask(t), sc, NEG)
            p = jnp.exp2(sc - pick(lse_ref))
            dp = _dot(dob, vv, 1, 1)
            ds = (p * (dp - pick(dl_ref)) * MLA_SCALE).astype(MXU_DTYPE)
            dv_acc[...] += _dot(p.astype(MXU_DTYPE), dob, 0, 0)
            dk_acc[...] += _dot(ds, q, 0, 0)
            dq_ref[rows, :] += jnp.dot(ds, k, preferred_element_type=F32)

        def unmasked(qi, carry):
            chunk(qi, False)
            return carry

        chunk(ki, True)
        lax.fori_loop(ki + 1, nq, unmasked, 0)
        dk_ref[...] = dk_acc[...]
        dv_ref[...] = dv_acc[...]

    head = lambda w: pl.BlockSpec((s, w), lambda h, ki: (0, h))
    blk = lambda w: pl.BlockSpec((t, w), lambda h, ki: (ki, h))
    stat = pl.BlockSpec((s, 128), lambda h, ki: (0, h // hp))
    return pl.pallas_call(
        body, name=name, grid=(MLA_HEADS, nq),
        in_specs=[head(256), blk(256), pl.BlockSpec((t, 128), lambda h, ki: (ki, 2 * h)), head(128), stat, stat],
        out_specs=[head(256), blk(256), blk(128)],
        out_shape=[jax.ShapeDtypeStruct((s, 2048), F32), jax.ShapeDtypeStruct((s, 2048), F32),
                   jax.ShapeDtypeStruct((s, 1024), F32)],
        scratch_shapes=[pltpu.VMEM((t, 256), F32), pltpu.VMEM((t, 128), F32)],
        compiler_params=_cparams("parallel", "arbitrary"),
    )(qcat, kcat, v, do, lse, delta)


def _gate_specs(tm):
    half = lambda c: pl.BlockSpec((tm, 1024), lambda i: (i, c))
    return half(0), half(GA_OFF // 1024), half(GB_OFF // 1024)


def _gate_fwd(oa, ob, proj, *, name):
    s = oa.shape[0]
    tm = min(512, s)

    def body(oa_ref, ob_ref, ga_ref, gb_ref, y_ref):
        ga, gb = ga_ref[...], gb_ref[...]
        y_ref[:, :1024] = (oa_ref[...] * (ga * jax.nn.sigmoid(ga))).astype(MXU_DTYPE)
        y_ref[:, 1024:] = (ob_ref[...] * (gb * jax.nn.sigmoid(gb))).astype(MXU_DTYPE)

    o_spec, ga_spec, gb_spec = _gate_specs(tm)
    return pl.pallas_call(
        body, name=name, grid=(s // tm,), in_specs=[o_spec, o_spec, ga_spec, gb_spec],
        out_specs=pl.BlockSpec((tm, 2048), lambda i: (i, 0)),
        out_shape=jax.ShapeDtypeStruct((s, 2048), MXU_DTYPE),
        compiler_params=_cparams("parallel"),
    )(oa, ob, proj, proj)


def _gate_bwd(dy, oa, ob, proj, *, name):
    s = oa.shape[0]
    tm = min(512, s)

    def body(dy_ref, oa_ref, ob_ref, ga_ref, gb_ref, doa_ref, dob_ref, dga_ref, dgb_ref, dl_ref):
        def branch(dyv, o, g, do_ref, dg_ref):
            sg = jax.nn.sigmoid(g)
            do = dyv * (g * sg)
            do_ref[...] = do.astype(MXU_DTYPE)
            dg_ref[...] = (dyv * o * (sg * (1.0 + g * (1.0 - sg)))).astype(MXU_DTYPE)
            return do

        branch(dy_ref[:, :1024], oa_ref[...], ga_ref[...], doa_ref, dga_ref)
        ob = ob_ref[...]
        prod = branch(dy_ref[:, 1024:], ob, gb_ref[...], dob_ref, dgb_ref) * ob
        lane = lax.broadcasted_iota(jnp.int32, (tm, stat_w), 1)
        acc = jnp.zeros((tm, stat_w), F32)
        for hh in range(MLA_HEADS):
            at = (hh // HEADS_PER_STEP) * 128 + hh % HEADS_PER_STEP
            acc = jnp.where(lane == at, jnp.sum(prod[:, hh * 128:(hh + 1) * 128], axis=-1, keepdims=True), acc)
        dl_ref[...] = acc

    stat_w = 128 * (MLA_HEADS // HEADS_PER_STEP)
    o_spec, ga_spec, gb_spec = _gate_specs(tm)
    return pl.pallas_call(
        body, name=name, grid=(s // tm,),
        in_specs=[pl.BlockSpec((tm, 2048), lambda i: (i, 0)), o_spec, o_spec, ga_spec, gb_spec],
        out_specs=[o_spec, o_spec, o_spec, o_spec, pl.BlockSpec((tm, stat_w), lambda i: (i, 0))],
        out_shape=[jax.ShapeDtypeStruct((s, 1024), MXU_DTYPE)] * 4 + [jax.ShapeDtypeStruct((s, stat_w), F32)],
        compiler_params=_cparams("parallel"),
    )(dy, oa, ob, proj, proj)


def _row_block(rows, cols, itemsize=4, budget=2 << 20):
    fits = [tr for tr in range(16, rows + 1, 16) if rows % tr == 0 and tr * cols * itemsize <= budget]
    return fits[-1] if fits else rows


def _adamw(w, g, m, v, *, name):
    shape = w.shape
    rows, cols = shape[-2:]
    w3, g3, m3, v3 = (a.reshape((-1, rows, cols)) for a in (w, g, m, v))
    lead = w3.shape[0]
    tr = _row_block(rows, cols)

    def body(w_ref, g_ref, m_ref, v_ref, d_ref, mo_ref, vo_ref):
        gv = g_ref[...]
        mn = ADAM_B1 * m_ref[...] + (1.0 - ADAM_B1) * gv
        vn = ADAM_B2 * v_ref[...] + (1.0 - ADAM_B2) * jnp.square(gv)
        m_hat = mn / (1.0 - ADAM_B1 ** ADAM_STEP)
        v_hat = vn / (1.0 - ADAM_B2 ** ADAM_STEP)
        d_ref[...] = -ADAM_LR * (m_hat / (jnp.sqrt(v_hat) + ADAM_EPS) + ADAM_WD * w_ref[...])
        mo_ref[...] = mn
        vo_ref[...] = vn

    spec = pl.BlockSpec((None, tr, cols), lambda a, i: (a, i, 0))
    outs = pl.pallas_call(
        body, name=name, grid=(lead, rows // tr), in_specs=[spec] * 4, out_specs=[spec] * 3,
        out_shape=[jax.ShapeDtypeStruct((lead, rows, cols), F32)] * 3,
        compiler_params=_cparams("parallel", "parallel"),
    )(w3, g3, m3, v3)
    return tuple(o.reshape(shape) for o in outs)


def _pair_sum(where, grads, recv, *, name):
    depth, chips, rows, cols = grads.shape
    hl = depth // 2
    tr = _row_block(rows, cols)

    def body(where_ref, a_ref, b_ref, o_ref):
        o_ref[...] = (a_ref[...] + b_ref[...].astype(F32)).astype(WIRE_DTYPE)

    spec = pl.BlockSpec((None, None, tr, cols), lambda a, k, i, w: (a, k, i, 0))
    return pl.pallas_call(
        body, name=name,
        grid_spec=pltpu.PrefetchScalarGridSpec(
            num_scalar_prefetch=1, grid=(hl, chips, rows // tr),
            in_specs=[pl.BlockSpec((None, None, tr, cols), lambda a, k, i, w: (hl * w[4] + a, k, i, 0)), spec],
            out_specs=spec),
        out_shape=jax.ShapeDtypeStruct((hl, chips, rows, cols), WIRE_DTYPE),
        compiler_params=_cparams("parallel", "parallel", "parallel"),
    )(where, grads, recv)


def _chip_sum(where, grads, recv, parts, *, name):
    depth, _, rows, cols = grads.shape
    hl = depth // 2
    tr = _row_block(rows, cols)

    def body(where_ref, a_ref, b_ref, t0_ref, t1_ref, t2_ref, o_ref):
        total = a_ref[...] + b_ref[...].astype(F32)
        for t_ref in (t0_ref, t1_ref, t2_ref):
            total = total + t_ref[...].astype(F32)
        o_ref[...] = total

    slot = lambda j: pl.BlockSpec((None, None, tr, cols), lambda a, i, w: (a, w[j], i, 0))
    return pl.pallas_call(
        body, name=name,
        grid_spec=pltpu.PrefetchScalarGridSpec(
            num_scalar_prefetch=1, grid=(hl, rows // tr),
            in_specs=[pl.BlockSpec((None, None, tr, cols), lambda a, i, w: (hl * w[4] + a, w[0], i, 0)), slot(0), slot(1),
                      slot(2), slot(3)],
            out_specs=pl.BlockSpec((None, tr, cols), lambda a, i, w: (hl * w[4] + a, i, 0))),
        out_shape=jax.ShapeDtypeStruct((depth, rows, cols), F32),
        compiler_params=_cparams("parallel", "parallel"),
    )(where, grads, recv, parts, parts, parts)


def _place():
    x, y, c = lax.axis_index("x"), lax.axis_index("y"), lax.axis_index("c")
    chips = [(1 - x, y), (x, 1 - y), (1 - x, 1 - y)]
    return x, y, c, chips


def _sems(*shape):
    return [pltpu.SemaphoreType.DMA(shape), pltpu.SemaphoreType.DMA(shape)]


def _comm_gather_weights(shards, *, name):
    nt = len(shards)
    hl = shards[0].shape[0] // 2

    def body(*refs):
        src, out, (send_sems, recv_sems) = refs[:nt], refs[nt:2 * nt], refs[2 * nt:]
        x, y, c, chips = _place()
        k = 2 * x + y
        mine = pl.ds(c * hl, hl)
        other = pl.ds((1 - c) * hl, hl)

        def rows(t, slot):
            n = src[t].shape[1]
            return pl.ds(pl.multiple_of(slot * n, 16), n)

        def copy(sem, t, src_ref, layers, slot, to):
            return pltpu.make_async_remote_copy(src_ref=src_ref, dst_ref=out[t].at[layers, rows(t, slot)],
                                                send_sem=send_sems.at[sem, t], recv_sem=recv_sems.at[sem, t],
                                                device_id=to, device_id_type=MESH)

        first = [copy(j, t, src[t].at[mine], mine, k, (*chip, c)) for j, chip in enumerate(chips) for t in range(nt)]
        for cp in first:
            cp.start()
        passed = []
        for j, (px, py) in enumerate(chips):
            slot = 2 * px + py
            for t in range(nt):
                copy(j, t, src[t].at[mine], mine, slot, (px, py, c)).wait_recv()
                cp = copy(3 + j, t, out[t].at[mine, rows(t, slot)], mine, slot, (x, y, 1 - c))
                cp.start()
                passed.append(cp)
        for j, (px, py) in enumerate(chips):
            for t in range(nt):
                copy(3 + j, t, out[t].at[other, rows(t, 2 * px + py)], other, 2 * px + py, (x, y, 1 - c)).wait_recv()
        for cp in first + passed:
            cp.wait_send()

    return pl.pallas_call(
        body, name=name, in_specs=[ANY] * nt, out_specs=[ANY] * nt,
        out_shape=[jax.ShapeDtypeStruct((a.shape[0], 4 * a.shape[1], a.shape[2]), a.dtype) for a in shards],
        scratch_shapes=_sems(6, nt),
    )(*shards)


def _comm_swap_sibling(bufs, *, name):
    nt = len(bufs)

    def body(*refs):
        src, out, (send_sems, recv_sems) = refs[:nt], refs[nt:2 * nt], refs[2 * nt:]
        x, y, c, _ = _place()
        cps = [pltpu.make_async_remote_copy(src_ref=src[t], dst_ref=out[t], send_sem=send_sems.at[t], recv_sem=recv_sems.at[t],
                                            device_id=(x, y, 1 - c), device_id_type=MESH) for t in range(nt)]
        for cp in cps:
            cp.start()
        for cp in cps:
            cp.wait()

    return pl.pallas_call(
        body, name=name, in_specs=[ANY] * nt, out_specs=[ANY] * nt,
        out_shape=[jax.ShapeDtypeStruct(a.shape, a.dtype) for a in bufs], scratch_shapes=_sems(nt),
    )(*bufs)


def _comm_scatter_chips(parts, *, name):
    nt = len(parts)

    def body(*refs):
        src, out, (send_sems, recv_sems) = refs[:nt], refs[nt:2 * nt], refs[2 * nt:]
        x, y, c, chips = _place()
        k = 2 * x + y
        sends = []
        for j, (px, py) in enumerate(chips):
            for t in range(nt):
                cp = pltpu.make_async_remote_copy(src_ref=src[t].at[:, 2 * px + py], dst_ref=out[t].at[:, k],
                                                  send_sem=send_sems.at[j, t], recv_sem=recv_sems.at[j, t],
                                                  device_id=(px, py, c), device_id_type=MESH)
                cp.start()
                sends.append(cp)
        for j, (px, py) in enumerate(chips):
            for t in range(nt):
                pltpu.make_async_remote_copy(src_ref=src[t].at[:, k], dst_ref=out[t].at[:, 2 * px + py],
                                             send_sem=send_sems.at[j, t], recv_sem=recv_sems.at[j, t],
                                             device_id=(px, py, c), device_id_type=MESH).wait_recv()
        for cp in sends:
            cp.wait_send()

    return pl.pallas_call(
        body, name=name, in_specs=[ANY] * nt, out_specs=[ANY] * nt,
        out_shape=[jax.ShapeDtypeStruct(a.shape, a.dtype) for a in parts], scratch_shapes=_sems(3, nt),
    )(*parts)


def _comm_join_halves(bufs, *, name):
    nt = len(bufs)
    hl = bufs[0].shape[0] // 2

    def body(*refs):
        src, out, (send_sems, recv_sems) = refs[:nt], refs[nt:2 * nt], refs[2 * nt:]
        x, y, c, _ = _place()
        mine = pl.ds(c * hl, hl)
        other = pl.ds((1 - c) * hl, hl)
        copy = lambda t, layers: pltpu.make_async_remote_copy(
            src_ref=src[t].at[mine], dst_ref=out[t].at[layers], send_sem=send_sems.at[t], recv_sem=recv_sems.at[t],
            device_id=(x, y, 1 - c), device_id_type=MESH)
        sends = [copy(t, mine) for t in range(nt)]
        for cp in sends:
            cp.start()
        for t in range(nt):
            copy(t, other).wait_recv()
        for cp in sends:
            cp.wait_send()

    return pl.pallas_call(
        body, name=name, in_specs=[ANY] * nt, out_specs=[ANY] * nt,
        out_shape=[jax.ShapeDtypeStruct(a.shape, a.dtype) for a in bufs],
        input_output_aliases={t: t for t in range(nt)}, scratch_shapes=_sems(nt),
    )(*bufs)


def _comm_allreduce_small(part, *, name):
    rows, cols = part.shape

    def body(p_ref, o_ref, buf, send_sems, recv_sems):
        x, y, c, _ = _place()
        me = 4 * x + 2 * y + c
        buf[me] = p_ref[...]
        flip = lambda v, bit: 1 - v if bit else v
        peers = [(flip(x, d & 4), flip(y, d & 2), flip(c, d & 1)) for d in range(1, 8)]
        sends = []
        for j, peer in enumerate(peers):
            cp = pltpu.make_async_remote_copy(src_ref=buf.at[me], dst_ref=buf.at[me], send_sem=send_sems.at[j],
                                              recv_sem=recv_sems.at[j], device_id=peer, device_id_type=MESH)
            cp.start()
            sends.append(cp)
        for j, (px, py, pc) in enumerate(peers):
            pltpu.make_async_remote_copy(src_ref=buf.at[me], dst_ref=buf.at[4 * px + 2 * py + pc], send_sem=send_sems.at[j],
                                         recv_sem=recv_sems.at[j], device_id=(px, py, pc), device_id_type=MESH).wait_recv()
        for cp in sends:
            cp.wait_send()
        total = buf[0]
        for i in range(1, 8):
            total = total + buf[i]
        o_ref[...] = total

    vm = pl.BlockSpec(memory_space=pltpu.VMEM)
    return pl.pallas_call(
        body, name=name, in_specs=[vm], out_specs=vm, out_shape=jax.ShapeDtypeStruct((rows, cols), F32),
        scratch_shapes=[pltpu.VMEM((8, rows, cols), F32), pltpu.SemaphoreType.DMA((7,)), pltpu.SemaphoreType.DMA((7,))],
    )(part)


def _pad_in_rows(wt):
    r = lambda o, n: wt[..., o:o + n, :]
    kr = r(2944, 64)
    return jnp.concatenate([r(0, 1024), r(1280, 1024), r(3008, 1024), r(2304, Q_RANK), r(1024, 128), r(2688, KV_RANK),
                            r(1152, 128), kr, jnp.zeros_like(kr)], axis=-2)


def _unpad_in_rows(qa, ga, gb, mixed):
    cq, ka, ckv, va, kr = (mixed[..., o - CQ_OFF:o - CQ_OFF + n, :] for o, n in (
        (CQ_OFF, Q_RANK), (KA_OFF, 128), (CKV_OFF, KV_RANK), (VA_OFF, 128), (KR_OFF, 64)))
    return jnp.concatenate([qa, ka, va, ga, cq, ckv, kr, gb], axis=-2)


def _pad_q_rows(wt):
    lead, cols = wt.shape[:-2], wt.shape[-1]
    wt = jnp.pad(wt.reshape(lead + (MLA_HEADS, 192, cols)), [(0, 0)] * (len(lead) + 1) + [(0, 64), (0, 0)])
    return wt.reshape(lead + (MLA_HEADS * 256, cols))


def _unpad_q_rows(wt):
    lead, cols = wt.shape[:-2], wt.shape[-1]
    return wt.reshape(lead + (MLA_HEADS, 256, cols))[..., :192, :].reshape(lead + (MLA_HEADS * 192, cols))


def _perm_kv_rows(wt):
    lead, cols = wt.shape[:-2], wt.shape[-1]
    return jnp.swapaxes(wt.reshape(lead + (MLA_HEADS, 2, 128, cols)), -4, -3).reshape(lead + (2048, cols))


def _unperm_kv_rows(wt):
    lead, cols = wt.shape[:-2], wt.shape[-1]
    return jnp.swapaxes(wt.reshape(lead + (2, MLA_HEADS, 128, cols)), -4, -3).reshape(lead + (2048, cols))


def _t(a):
    return jnp.swapaxes(a, -1, -2)


def _rope_tables(s):
    pos = jnp.arange(s, dtype=F32)
    inv_freq = 10000.0 ** (-jnp.arange(0, 64, 2, dtype=F32) / 64)
    ang = pos[:, None] * inv_freq[None, :]
    cos, sin = jnp.cos(ang), jnp.sin(ang)
    z64 = jnp.zeros((s, 64), F32)
    tk_c = jnp.concatenate([cos, cos, z64], axis=-1)
    tk_s = jnp.concatenate([-sin, sin, z64], axis=-1)
    tq_c = jnp.concatenate([jnp.ones((s, 128), F32), tk_c], axis=-1)
    tq_s = jnp.concatenate([jnp.zeros((s, 128), F32), tk_s], axis=-1)
    return tq_c, tq_s, tk_c, tk_s


def _device_step(xs, tgt, attn_g, sinks, gq, gkv, final_g, w_in_p, w_q_p, w_kv_p, w_o):
    depth = len(w_in_p)
    s = xs.shape[0]
    tabs = _rope_tables(s)
    saved = []
    x = xs
    for l in range(depth):
        h = _rmsnorm_fwd(x, attn_g[l:l + 1], name=f"norm_fwd{l}")
        proj = _matmul(h, w_in_p[l], tb=True, name=f"in_proj{l}")
        oa, lse_a = _swa_fwd(proj, sinks[l], name=f"swa_fwd{l}")
        qcat, kcat, v, cqn, ckvn = _mla_qkv_fwd(proj, gq[l:l + 1], gkv[l:l + 1], w_q_p[l], w_kv_p[l], *tabs,
                                                name=f"mla_qkv_fwd{l}")
        ob, lse_b = _mla_fwd(qcat, kcat, v, name=f"mla_fwd{l}")
        y = _gate_fwd(oa, ob, proj, name=f"gate_fwd{l}")
        x_next = _matmul(y, w_o[l], add=x, name=f"out_proj{l}")
        saved.append((x, h, proj, oa, lse_a, qcat, kcat, v, cqn, ckvn, ob, lse_b, y))
        x = x_next

    dx, d_final_g, loss = _final_loss(x, final_g, tgt, name="final_loss")

    d_attn_g, d_sinks, d_gq, d_gkv = [None] * depth, [None] * depth, [None] * depth, [None] * depth
    d_w_in, d_w_q, d_w_kv, d_w_o = [None] * depth, [None] * depth, [None] * depth, [None] * depth
    for l in reversed(range(depth)):
        x, h, proj, oa, lse_a, qcat, kcat, v, cqn, ckvn, ob, lse_b, y = saved[l]
        dy = _matmul(dx, w_o[l], tb=True, name=f"out_proj_dx{l}")
        d_w_o[l] = _matmul(y, dx, ta=True, name=f"out_proj_dw{l}")
        doa, dob, dga, dgb, delta_b = _gate_bwd(dy, oa, ob, proj, name=f"gate_bwd{l}")
        dqa, dka, dva, dsink = _swa_bwd(proj, sinks[l], lse_a, doa, name=f"swa_bwd{l}")
        dqc, dkc, dv = _mla_bwd(qcat, kcat, v, dob, lse_b, delta_b, name=f"mla_bwd{l}")
        mixed, d_w_q[l], d_w_kv[l], dgq_l, dgkv_l = _mla_qkv_bwd(
            proj, cqn, ckvn, dqc, dkc, dv, dka, dva, gq[l:l + 1], gkv[l:l + 1], w_q_p[l], w_kv_p[l], *tabs,
            name=f"mla_qkv_bwd{l}")
        dproj = [dqa, dga, dgb, mixed]
        dh = _matmul_ktiles(dproj, w_in_p[l], name=f"in_proj_dx{l}")
        d_w_in[l] = [_matmul(tile, h, ta=True, name=f"in_proj_dw{l}_{j}") for j, tile in enumerate(dproj)]
        dx, dg_l = _rmsnorm_bwd(dh, x, attn_g[l:l + 1], dx, name=f"norm_bwd{l}")
        d_attn_g[l], d_sinks[l], d_gq[l], d_gkv[l] = dg_l, dsink[0:1, :SWA_HEADS], dgq_l, dgkv_l

    cat = lambda parts: jnp.concatenate(parts, axis=0)
    return (loss, dx, cat(d_attn_g), cat(d_sinks), cat(d_gq), cat(d_gkv), d_final_g, d_w_in, d_w_q, d_w_kv, d_w_o)


def kernel(x, attn_norm_g, w_in, swa_sinks, q_a_norm_g, kv_a_norm_g, w_q_b, w_kv_b, w_out, final_norm_g, loss_target, m_attn_norm_g, m_w_in, m_swa_sinks, m_q_a_norm_g, m_kv_a_norm_g, m_w_q_b, m_w_kv_b, m_w_out, m_final_norm_g, v_attn_norm_g, v_w_in, v_swa_sinks, v_q_a_norm_g, v_kv_a_norm_g, v_w_q_b, v_w_kv_b, v_w_out, v_final_norm_g):
    depth = w_in.shape[0]
    hl = depth // 2
    x_, y_, c = lax.axis_index("x"), lax.axis_index("y"), lax.axis_index("c")
    chip = 2 * x_ + y_
    where = jnp.stack([chip, 2 * (1 - x_) + y_, 2 * x_ + 1 - y_, 2 * (1 - x_) + 1 - y_, c]).astype(jnp.int32)

    sent = [a.astype(WIRE_DTYPE) for a in (_t(w_in), _t(w_q_b), _t(w_kv_b), w_out)]
    gathered = _comm_gather_weights(sent, name="comm_gather_weights")
    own_rows = lambda g, own: lax.dynamic_update_slice_in_dim(g, own, chip * own.shape[1], axis=1)
    full_in, full_q, full_kv, full_o = (own_rows(g, own) for g, own in zip(gathered, sent))
    per_layer = lambda a: [a[l] for l in range(depth)]

    (loss, dx, d_attn_g, d_sinks, d_gq, d_gkv, d_final_g, d_w_in, d_w_q, d_w_kv, d_w_o) = _device_step(
        x[0], loss_target[0], attn_norm_g, swa_sinks, q_a_norm_g, kv_a_norm_g, final_norm_g.reshape(1, -1),
        per_layer(_pad_in_rows(full_in)), per_layer(_pad_q_rows(full_q)), per_layer(_perm_kv_rows(full_kv)),
        per_layer(full_o))

    in_tiles = [jnp.stack([d_w_in[l][j] for l in range(depth)]) for j in range(4)]
    full_grads = [_unpad_in_rows(*in_tiles), _unpad_q_rows(jnp.stack(d_w_q)), _unperm_kv_rows(jnp.stack(d_w_kv)),
                  jnp.stack(d_w_o)]
    grads4 = [g.reshape(depth, 4, g.shape[1] // 4, g.shape[2]) for g in full_grads]
    give = [lax.dynamic_slice_in_dim(g, (1 - c) * hl, hl, axis=0).astype(WIRE_DTYPE) for g in grads4]
    recv = _comm_swap_sibling(give, name="comm_swap_sibling")
    pair = [_pair_sum(where, g, r, name=f"pair_sum{t}") for t, (g, r) in enumerate(zip(grads4, recv))]
    parts = _comm_scatter_chips(pair, name="comm_scatter_chips")
    mine = [_chip_sum(where, g, r, p, name=f"chip_sum{t}") for t, (g, r, p) in enumerate(zip(grads4, recv, parts))]
    reduced = _comm_join_halves(mine, name="comm_join_halves")
    g_w_in, g_w_q_b, g_w_kv_b, g_w_out = _t(reduced[0]), _t(reduced[1]), _t(reduced[2]), reduced[3]

    small = [d_attn_g, d_sinks, d_gq, d_gkv, d_final_g, loss[:, :1]]
    flat = jnp.concatenate([a.reshape(-1) for a in small])
    n_small = flat.shape[0]
    rows = -(-n_small // 1024) * 8
    total = _comm_allreduce_small(jnp.pad(flat, (0, rows * 128 - n_small)).reshape(rows, 128),
                                  name="comm_allreduce_small").reshape(-1)
    outs, at = [], 0
    for a in small:
        outs.append(total[at:at + a.size].reshape(a.shape))
        at += a.size
    g_attn_g, g_sinks, g_gq, g_gkv, g_final_g, loss_total = outs
    g_final_g = g_final_g.reshape(final_norm_g.shape)

    weights = [attn_norm_g, w_in, swa_sinks, q_a_norm_g, kv_a_norm_g, w_q_b, w_kv_b, w_out, final_norm_g]
    grads = [g_attn_g, g_w_in, g_sinks, g_gq, g_gkv, g_w_q_b, g_w_kv_b, g_w_out, g_final_g]
    ms = [m_attn_norm_g, m_w_in, m_swa_sinks, m_q_a_norm_g, m_kv_a_norm_g, m_w_q_b, m_w_kv_b, m_w_out, m_final_norm_g]
    vs = [v_attn_norm_g, v_w_in, v_swa_sinks, v_q_a_norm_g, v_kv_a_norm_g, v_w_q_b, v_w_kv_b, v_w_out, v_final_norm_g]
    as2d = lambda a: a.reshape(1, -1) if a.ndim == 1 else a
    deltas, new_m, new_v = [], [], []
    for i, (w, g, m, v) in enumerate(zip(weights, grads, ms, vs)):
        d, mn, vn = _adamw(as2d(w), as2d(g), as2d(m), as2d(v), name=f"adamw{i}")
        deltas.append(d.reshape(w.shape))
        new_m.append(mn.reshape(w.shape))
        new_v.append(vn.reshape(w.shape))

    return (loss_total.reshape(()), dx[None], *grads, *deltas, *new_m, *new_v)
```

```python
import functools
import math

import jax
import jax.numpy as jnp
from jax import lax
from jax.experimental import pallas as pl
from jax.experimental.pallas import tpu as pltpu

F32 = jnp.float32
MXU_DTYPE = jnp.bfloat16
WIRE_DTYPE = jnp.bfloat16

EPS = 1e-6
NEG = -1e30
BLOCK = 128
D_MODEL = 2048
SWA_HEADS = 16
MLA_HEADS = 8
Q_RANK = 384
KV_RANK = 256
IN_WIDTH = 4032
MLA_SCALE = 192 ** -0.5
SWA_SCALE = 64 ** -0.5
LOG2E = math.log2(math.e)
HEADS_PER_STEP = 2
SLOPES = tuple(2.0 ** (-8.0 * (h + 1) / SWA_HEADS) for h in range(SWA_HEADS))

P_WIDTH = 4096
QA_OFF, GA_OFF, GB_OFF, CQ_OFF, KA_OFF, CKV_OFF, VA_OFF, KR_OFF = 0, 1024, 2048, 3072, 3456, 3584, 3840, 3968

ADAM_LR, ADAM_B1, ADAM_B2, ADAM_EPS, ADAM_WD, ADAM_STEP = 0.001, 0.9, 0.999, 1e-08, 0.01, 10

VMEM_LIMIT = 56 * 1024 * 1024
MESH = pl.DeviceIdType.MESH
ANY = pl.BlockSpec(memory_space=pl.ANY)


def _cparams(*sem):
    return pltpu.CompilerParams(dimension_semantics=sem, vmem_limit_bytes=VMEM_LIMIT)


def _dot(a, b, ca, cb):
    return lax.dot_general(a, b, (((ca,), (cb,)), ((), ())), preferred_element_type=F32)


def _layer_spec(block, index_map, layer):
    if layer is None:
        return pl.BlockSpec(block, index_map)
    return pl.BlockSpec((None,) + tuple(block), lambda *g: (layer,) + tuple(index_map(*g)))


def _matmul(a, b, *, name, ta=False, tb=False, out_dtype=F32, add=None, b_layer=None, tm=1024, tn=1024, tk=2048):
    (kdim, m) = a.shape if ta else a.shape[::-1]
    (n, k2) = b.shape[-2:] if tb else b.shape[-2:][::-1]
    assert kdim == k2, (a.shape, b.shape)
    tm, tn, tk = min(tm, m), min(tn, n), min(tk, kdim)
    assert m % tm == 0 and n % tn == 0 and kdim % tk == 0
    nk = kdim // tk

    def body(*refs):
        a_ref, b_ref = refs[:2]
        add_ref = None if add is None else refs[2]
        o_ref = refs[2 + (add is not None)]
        part = _dot(a_ref[...].astype(MXU_DTYPE), b_ref[...].astype(MXU_DTYPE), 0 if ta else 1, 1 if tb else 0)

        def finish(r):
            o_ref[...] = (r if add is None else add_ref[...] + r).astype(out_dtype)

        if nk == 1:
            finish(part)
            return
        acc = refs[-1]
        k = pl.program_id(2)

        @pl.when(k == 0)
        def _():
            acc[...] = part

        @pl.when((k > 0) & (k < nk - 1))
        def _():
            acc[...] += part

        @pl.when(k == nk - 1)
        def _():
            finish(acc[...] + part)

    a_spec = pl.BlockSpec((tk, tm), lambda i, j, k: (k, i)) if ta else pl.BlockSpec((tm, tk), lambda i, j, k: (i, k))
    b_spec = (_layer_spec((tn, tk), lambda i, j, k: (j, k), b_layer) if tb else
              _layer_spec((tk, tn), lambda i, j, k: (k, j), b_layer))
    in_specs, args = [a_spec, b_spec], [a, b]
    if add is not None:
        in_specs.append(pl.BlockSpec((tm, tn), lambda i, j, k: (i, j)))
        args.append(add)
    return pl.pallas_call(
        body, name=name, grid=(m // tm, n // tn, nk), in_specs=in_specs,
        out_specs=pl.BlockSpec((tm, tn), lambda i, j, k: (i, j)),
        out_shape=jax.ShapeDtypeStruct((m, n), out_dtype),
        scratch_shapes=[pltpu.VMEM((tm, tn), F32)] if nk > 1 else [],
        compiler_params=_cparams("parallel", "parallel", "arbitrary"),
    )(*args)


def _matmul_ktiles(a_tiles, b, *, name, b_layer=None, tm=512, tn=1024):
    m, kt = a_tiles[0].shape
    n = b.shape[-1]
    nt = len(a_tiles)
    assert b.shape[-2] == nt * kt
    tm, tn = min(tm, m), min(tn, n)
    assert m % tm == 0 and n % tn == 0

    def body(*refs):
        a_refs, b_refs, o_ref = refs[:nt], refs[nt:2 * nt], refs[2 * nt]
        acc = _dot(a_refs[0][...].astype(MXU_DTYPE), b_refs[0][...].astype(MXU_DTYPE), 1, 0)
        for j in range(1, nt):
            acc += _dot(a_refs[j][...].astype(MXU_DTYPE), b_refs[j][...].astype(MXU_DTYPE), 1, 0)
        o_ref[...] = acc

    in_specs = [pl.BlockSpec((tm, kt), lambda i, jn: (i, 0))] * nt
    in_specs += [_layer_spec((kt, tn), lambda i, jn, j=j: (j, jn), b_layer) for j in range(nt)]
    return pl.pallas_call(
        body, name=name, grid=(m // tm, n // tn), in_specs=in_specs,
        out_specs=pl.BlockSpec((tm, tn), lambda i, jn: (i, jn)),
        out_shape=jax.ShapeDtypeStruct((m, n), F32),
        compiler_params=_cparams("parallel", "parallel"),
    )(*a_tiles, *([b] * nt))


def _rmsnorm_fwd(x, g, *, name):
    s, d = x.shape
    tm = min(512, s)

    def body(x_ref, g_ref, h_ref):
        xv = x_ref[...]
        r = lax.rsqrt(jnp.mean(xv * xv, axis=-1, keepdims=True) + EPS)
        h_ref[...] = (xv * r * g_ref[...]).astype(MXU_DTYPE)

    return pl.pallas_call(
        body, name=name, grid=(s // tm,),
        in_specs=[pl.BlockSpec((tm, d), lambda i: (i, 0)), pl.BlockSpec((1, d), lambda i: (0, 0))],
        out_specs=pl.BlockSpec((tm, d), lambda i: (i, 0)),
        out_shape=jax.ShapeDtypeStruct((s, d), MXU_DTYPE),
        compiler_params=_cparams("parallel"),
    )(x, g)


def _rmsnorm_bwd(dh, x, g, dres, *, name):
    s, d = x.shape
    tm = min(512, s)

    def body(dh_ref, x_ref, g_ref, dres_ref, dx_ref, dg_ref):
        @pl.when(pl.program_id(0) == 0)
        def _():
            dg_ref[...] = jnp.zeros_like(dg_ref)

        xv = x_ref[...]
        r = lax.rsqrt(jnp.mean(xv * xv, axis=-1, keepdims=True) + EPS)
        xn = xv * r
        dy = dh_ref[...]
        dg_ref[...] += jnp.sum(dy * xn, axis=0, keepdims=True)
        u = dy * g_ref[...]
        dx_ref[...] = dres_ref[...] + r * (u - xn * jnp.mean(u * xn, axis=-1, keepdims=True))

    row = pl.BlockSpec((tm, d), lambda i: (i, 0))
    vec = pl.BlockSpec((1, d), lambda i: (0, 0))
    return pl.pallas_call(
        body, name=name, grid=(s // tm,), in_specs=[row, row, vec, row], out_specs=[row, vec],
        out_shape=[jax.ShapeDtypeStruct((s, d), F32), jax.ShapeDtypeStruct((1, d), F32)],
        compiler_params=_cparams("arbitrary"),
    )(dh, x, g, dres)


def _final_loss(x, g, tgt, *, name):
    s, d = x.shape
    tm = min(512, s)

    def body(x_ref, g_ref, t_ref, dx_ref, dg_ref, loss_ref):
        @pl.when(pl.program_id(0) == 0)
        def _():
            dg_ref[...] = jnp.zeros_like(dg_ref)
            loss_ref[...] = jnp.zeros_like(loss_ref)

        xv = x_ref[...]
        gv = g_ref[...]
        r = lax.rsqrt(jnp.mean(xv * xv, axis=-1, keepdims=True) + EPS)
        xn = xv * r
        err = xn * gv - t_ref[...]
        sq = jnp.sum(jnp.sum(err * err, axis=-1, keepdims=True), axis=0, keepdims=True)
        loss_ref[...] += (0.5 / d) * sq
        dy = err * (1.0 / d)
        dg_ref[...] += jnp.sum(dy * xn, axis=0, keepdims=True)
        u = dy * gv
        dx_ref[...] = r * (u - xn * jnp.mean(u * xn, axis=-1, keepdims=True))

    row = pl.BlockSpec((tm, d), lambda i: (i, 0))
    vec = pl.BlockSpec((1, d), lambda i: (0, 0))
    return pl.pallas_call(
        body, name=name, grid=(s // tm,), in_specs=[row, vec, row],
        out_specs=[row, vec, pl.BlockSpec((1, 128), lambda i: (0, 0))],
        out_shape=[jax.ShapeDtypeStruct((s, d), F32), jax.ShapeDtypeStruct((1, d), F32),
                   jax.ShapeDtypeStruct((1, 128), F32)],
        compiler_params=_cparams("arbitrary"),
    )(x, g, tgt)


def _swa_keys(kp_ref, kc_ref):
    kk = jnp.concatenate([kp_ref[...], kc_ref[...]], axis=0)
    kr = pltpu.roll(kk, 64, 1)
    lo = lax.broadcasted_iota(jnp.int32, kk.shape, 1) < 64
    return [jnp.where(lo, kk, kr).astype(MXU_DTYPE), jnp.where(lo, kr, kk).astype(MXU_DTYPE)]


def _swa_mask(n):
    qi = lax.broadcasted_iota(jnp.int32, (BLOCK, 2 * BLOCK), 0)
    ki = lax.broadcasted_iota(jnp.int32, (BLOCK, 2 * BLOCK), 1)
    delta = BLOCK + qi - ki
    valid = (delta >= 0) & (delta < BLOCK) & ((ki >= BLOCK) | (n > 0))
    return valid, delta.astype(F32)


def _swa_scores(qm, keys, h, valid, deltaf):
    s = _dot(qm, keys, 1, 1) * SWA_SCALE
    return jnp.where(valid, s - SLOPES[h] * deltaf, NEG)


def _swa_specs(nb):
    kcol, vcol = KA_OFF // BLOCK, VA_OFF // BLOCK
    last = nb - 1
    cur = lambda n: jnp.minimum(n, last)
    prev = lambda n: jnp.maximum(jnp.minimum(n, last) - 1, 0)
    return [
        pl.BlockSpec(memory_space=pltpu.SMEM),
        pl.BlockSpec((BLOCK, 1024), lambda n: (cur(n), QA_OFF // 1024)),
        pl.BlockSpec((BLOCK, BLOCK), lambda n: (cur(n), kcol)),
        pl.BlockSpec((BLOCK, BLOCK), lambda n: (prev(n), kcol)),
        pl.BlockSpec((BLOCK, BLOCK), lambda n: (cur(n), vcol)),
        pl.BlockSpec((BLOCK, BLOCK), lambda n: (prev(n), vcol)),
    ]


def _swa_fwd(proj, sinks, *, name):
    s = proj.shape[0]
    nb = s // BLOCK

    def body(sink_ref, q_ref, kc_ref, kp_ref, vc_ref, vp_ref, o_ref, lse_ref):
        n = pl.program_id(0)
        keys = _swa_keys(kp_ref, kc_ref)
        vals = _swa_keys(vp_ref, vc_ref)
        valid, deltaf = _swa_mask(n)
        lane = lax.broadcasted_iota(jnp.int32, (BLOCK, BLOCK), 1)
        lo = lane < 64
        lse_acc = jnp.zeros((BLOCK, BLOCK), F32)
        for i in range(SWA_HEADS // 2):
            j = i // 4
            qp = q_ref[:, i * 128:(i + 1) * 128]
            outs = []
            for half in range(2):
                h = 2 * i + half
                qm = jnp.where(lo if half == 0 else ~lo, qp, 0.0).astype(MXU_DTYPE)
                sc = _swa_scores(qm, keys[j], h, valid, deltaf)
                sink = sink_ref[h]
                m = jnp.maximum(jnp.max(sc, axis=-1, keepdims=True), sink)
                p = jnp.exp(sc - m)
                l = jnp.sum(p, axis=-1, keepdims=True) + jnp.exp(sink - m)
                outs.append(jnp.dot((p / l).astype(MXU_DTYPE), vals[j], preferred_element_type=F32))
                lse_acc = jnp.where(lane == h, m + jnp.log(l), lse_acc)
            o_ref[:, i * 128:(i + 1) * 128] = jnp.where(lo, outs[0], outs[1])
        lse_ref[...] = lse_acc

    return pl.pallas_call(
        body, name=name, grid=(nb,), in_specs=_swa_specs(nb),
        out_specs=[pl.BlockSpec((BLOCK, 1024), lambda n: (n, 0)), pl.BlockSpec((BLOCK, BLOCK), lambda n: (n, 0))],
        out_shape=[jax.ShapeDtypeStruct((s, 1024), F32), jax.ShapeDtypeStruct((s, BLOCK), F32)],
        compiler_params=_cparams("parallel"),
    )(sinks, proj, proj, proj, proj, proj)


def _swa_bwd(proj, sinks, lse, do, *, name):
    s = proj.shape[0]
    nb = s // BLOCK
    last = nb - 1

    def body(sink_ref, q_ref, kc_ref, kp_ref, vc_ref, vp_ref, lse_ref, do_ref,
             dq_ref, dk_ref, dv_ref, dsink_ref, carry_k, carry_v):
        n = pl.program_id(0)

        @pl.when(n == 0)
        def _():
            carry_k[...] = jnp.zeros_like(carry_k)
            carry_v[...] = jnp.zeros_like(carry_v)
            dsink_ref[...] = jnp.zeros_like(dsink_ref)

        @pl.when(n < nb)
        def _():
            keys = _swa_keys(kp_ref, kc_ref)
            vals = _swa_keys(vp_ref, vc_ref)
            valid, deltaf = _swa_mask(n)
            lane = lax.broadcasted_iota(jnp.int32, (BLOCK, BLOCK), 1)
            lane1 = lax.broadcasted_iota(jnp.int32, (1, BLOCK), 1)
            lo = lane < 64
            lse_blk = lse_ref[...]
            acc_k = [jnp.zeros((2 * BLOCK, BLOCK), F32) for _ in range(2)]
            acc_v = [jnp.zeros((2 * BLOCK, BLOCK), F32) for _ in range(2)]
            dsink = jnp.zeros((1, BLOCK), F32)
            for i in range(SWA_HEADS // 2):
                j = i // 4
                qp = q_ref[:, i * 128:(i + 1) * 128]
                dop = do_ref[:, i * 128:(i + 1) * 128].astype(F32)
                dqs = []
                for half in range(2):
                    h = 2 * i + half
                    msk = lo if half == 0 else ~lo
                    qm = jnp.where(msk, qp, 0.0).astype(MXU_DTYPE)
                    dom = jnp.where(msk, dop, 0.0).astype(MXU_DTYPE)
                    sc = _swa_scores(qm, keys[j], h, valid, deltaf)
                    lse_h = jnp.sum(jnp.where(lane == h, lse_blk, 0.0), axis=-1, keepdims=True)
                    p = jnp.exp(sc - lse_h)
                    dp = _dot(dom, vals[j], 1, 1)
                    dlt = jnp.sum(dp * p, axis=-1, keepdims=True)
                    ds = (p * (dp - dlt) * SWA_SCALE).astype(MXU_DTYPE)
                    psink = jnp.exp(sink_ref[h] - lse_h)
                    dsink = jnp.where(lane1 == h, -jnp.sum(psink * dlt, axis=0, keepdims=True), dsink)
                    dqs.append(jnp.dot(ds, keys[j], preferred_element_type=F32))
                    acc_k[j] += _dot(ds, qm, 0, 0)
                    acc_v[j] += _dot(p.astype(MXU_DTYPE), dom, 0, 0)
                dq_ref[:, i * 128:(i + 1) * 128] = jnp.where(lo, dqs[0], dqs[1]).astype(dq_ref.dtype)
            lo2 = lax.broadcasted_iota(jnp.int32, (2 * BLOCK, BLOCK), 1) < 64
            fold = lambda acc: jnp.where(lo2, acc[0] + pltpu.roll(acc[0], 64, 1), acc[1] + pltpu.roll(acc[1], 64, 1))
            dkk, dvv = fold(acc_k), fold(acc_v)
            dk_ref[...] = (carry_k[...] + dkk[:BLOCK]).astype(dk_ref.dtype)
            dv_ref[...] = (carry_v[...] + dvv[:BLOCK]).astype(dv_ref.dtype)
            carry_k[...] = dkk[BLOCK:]
            carry_v[...] = dvv[BLOCK:]
            dsink_ref[...] += jnp.broadcast_to(dsink, dsink_ref.shape)

        @pl.when(n == nb)
        def _():
            dk_ref[...] = carry_k[...].astype(dk_ref.dtype)
            dv_ref[...] = carry_v[...].astype(dv_ref.dtype)

    cur = lambda n: jnp.minimum(n, last)
    lag = lambda n: jnp.maximum(n - 1, 0)
    return pl.pallas_call(
        body, name=name, grid=(nb + 1,),
        in_specs=_swa_specs(nb) + [pl.BlockSpec((BLOCK, BLOCK), lambda n: (cur(n), 0)),
                                   pl.BlockSpec((BLOCK, 1024), lambda n: (cur(n), 0))],
        out_specs=[pl.BlockSpec((BLOCK, 1024), lambda n: (cur(n), 0)),
                   pl.BlockSpec((BLOCK, BLOCK), lambda n: (lag(n), 0)),
                   pl.BlockSpec((BLOCK, BLOCK), lambda n: (lag(n), 0)),
                   pl.BlockSpec((8, BLOCK), lambda n: (0, 0))],
        out_shape=[jax.ShapeDtypeStruct((s, 1024), MXU_DTYPE), jax.ShapeDtypeStruct((s, BLOCK), MXU_DTYPE),
                   jax.ShapeDtypeStruct((s, BLOCK), MXU_DTYPE), jax.ShapeDtypeStruct((8, BLOCK), F32)],
        scratch_shapes=[pltpu.VMEM((BLOCK, BLOCK), F32), pltpu.VMEM((BLOCK, BLOCK), F32)],
        compiler_params=_cparams("arbitrary"),
    )(sinks, proj, proj, proj, proj, proj, lse, do)


def _rope_partner(v, first, width):
    lane = lax.broadcasted_iota(jnp.int32, v.shape, 1)
    in_a = (lane >= first) & (lane < first + 32)
    in_b = (lane >= first + 32) & (lane < first + 64)
    return jnp.where(in_a, pltpu.roll(v, width - 32, 1), jnp.where(in_b, pltpu.roll(v, 32, 1), 0.0))


def _mla_qkv_fwd(proj, gq, gkv, wq, wkv, tq_c, tq_s, tk_c, tk_s, *, layer, name):
    s = proj.shape[0]
    tm = min(256, s)

    def body(cq_ref, ckv_ref, kr_ref, gq_ref, gkv_ref, wq_ref, wkv_ref, qc_ref, qs_ref, kc_ref, ks_ref,
             qcat_ref, kcat_ref, v_ref, cqn_ref, ckvn_ref):
        cq = cq_ref[...]
        cqn = (cq * lax.rsqrt(jnp.mean(cq * cq, axis=-1, keepdims=True) + EPS) * gq_ref[...]).astype(MXU_DTYPE)
        cqn_ref[...] = cqn
        qpre = _dot(cqn, wq_ref[...], 1, 1)
        qc, qs = qc_ref[...], qs_ref[...]
        for hh in range(MLA_HEADS):
            blk = qpre[:, hh * 256:(hh + 1) * 256]
            qcat_ref[:, hh * 256:(hh + 1) * 256] = (blk * qc + _rope_partner(blk, 128, 256) * qs).astype(MXU_DTYPE)
        ckv = ckv_ref[...]
        ckvn = (ckv * lax.rsqrt(jnp.mean(ckv * ckv, axis=-1, keepdims=True) + EPS) * gkv_ref[...]).astype(MXU_DTYPE)
        ckvn_ref[...] = ckvn
        kv = _dot(ckvn, wkv_ref[...], 1, 1)
        kr = kr_ref[...]
        krr = (kr * kc_ref[...] + _rope_partner(kr, 0, 128) * ks_ref[...]).astype(MXU_DTYPE)
        for hh in range(MLA_HEADS):
            kcat_ref[:, hh * 256:hh * 256 + 128] = kv[:, hh * 128:(hh + 1) * 128].astype(MXU_DTYPE)
            kcat_ref[:, hh * 256 + 128:(hh + 1) * 256] = krr
            v_ref[:, hh * 256:hh * 256 + 128] = kv[:, 1024 + hh * 128:1024 + (hh + 1) * 128].astype(MXU_DTYPE)
            v_ref[:, hh * 256 + 128:(hh + 1) * 256] = jnp.ones((tm, 128), MXU_DTYPE)

    row = lambda w, c: pl.BlockSpec((tm, w), lambda i: (i, c))
    full = lambda a: pl.BlockSpec(a.shape, lambda i: (0, 0))
    of_layer = lambda a: _layer_spec(a.shape[1:], lambda i: (0, 0), layer)
    return pl.pallas_call(
        body, name=name, grid=(s // tm,),
        in_specs=[row(Q_RANK, CQ_OFF // Q_RANK), row(KV_RANK, CKV_OFF // KV_RANK), row(128, KR_OFF // 128),
                  full(gq), full(gkv), of_layer(wq), of_layer(wkv), row(256, 0), row(256, 0), row(128, 0), row(128, 0)],
        out_specs=[row(2048, 0), row(2048, 0), row(2048, 0), row(Q_RANK, 0), row(KV_RANK, 0)],
        out_shape=[jax.ShapeDtypeStruct((s, 2048), MXU_DTYPE), jax.ShapeDtypeStruct((s, 2048), MXU_DTYPE),
                   jax.ShapeDtypeStruct((s, 2048), MXU_DTYPE), jax.ShapeDtypeStruct((s, Q_RANK), MXU_DTYPE),
                   jax.ShapeDtypeStruct((s, KV_RANK), MXU_DTYPE)],
        compiler_params=_cparams("parallel"),
    )(proj, proj, proj, gq, gkv, wq, wkv, tq_c, tq_s, tk_c, tk_s)


def _norm_bwd(x, g, dy):
    r = lax.rsqrt(jnp.mean(x * x, axis=-1, keepdims=True) + EPS)
    xn = x * r
    u = dy * g
    return r * (u - xn * jnp.mean(u * xn, axis=-1, keepdims=True)), jnp.sum(dy * xn, axis=0, keepdims=True)


def _mla_qkv_bwd(proj, cqn, ckvn, dqcat, dkcat, dv, dka, dva, gq, gkv, wq, wkv, tq_c, tq_s, tk_c, tk_s, *, layer,
                 name):
    s = proj.shape[0]
    tm = min(256, s)
    t_cq, t_ka, t_ckv, t_va, t_kr = (o - CQ_OFF for o in (CQ_OFF, KA_OFF, CKV_OFF, VA_OFF, KR_OFF))

    def body(cq_ref, ckv_ref, cqn_ref, ckvn_ref, dq_ref, dk_ref, dv_ref, dka_ref, dva_ref, gq_ref, gkv_ref, wq_ref,
             wkv_ref, qc_ref, qs_ref, kc_ref, ks_ref,
             tile_ref, dwq_ref, dwkv_ref, dgq_ref, dgkv_ref, dqpre, dkv):
        dcq_ref = tile_ref.at[:, t_cq:t_cq + Q_RANK]
        dckv_ref = tile_ref.at[:, t_ckv:t_ckv + KV_RANK]
        dkr_ref = tile_ref.at[:, t_kr:t_kr + 128]
        tile_ref[:, t_ka:t_ka + 128] = dka_ref[...]
        tile_ref[:, t_va:t_va + 128] = dva_ref[...]

        @pl.when(pl.program_id(0) == 0)
        def _():
            for r in (dwq_ref, dwkv_ref, dgq_ref, dgkv_ref):
                r[...] = jnp.zeros_like(r)

        qc, qs = qc_ref[...], qs_ref[...]
        dkrr = jnp.zeros((tm, 128), F32)
        for hh in range(MLA_HEADS):
            blk = dq_ref[:, hh * 256:(hh + 1) * 256]
            dqpre[:, hh * 256:(hh + 1) * 256] = (blk * qc + _rope_partner(blk * qs, 128, 256)).astype(MXU_DTYPE)
            dkv[:, hh * 128:(hh + 1) * 128] = dk_ref[:, hh * 256:hh * 256 + 128].astype(MXU_DTYPE)
            dkrr = dkrr + dk_ref[:, hh * 256 + 128:(hh + 1) * 256]
        dkv[:, 1024:] = dv_ref[...].astype(MXU_DTYPE)
        dkr_ref[...] = (dkrr * kc_ref[...] + _rope_partner(dkrr * ks_ref[...], 0, 128)).astype(dkr_ref.dtype)

        dq_b = dqpre[...]
        dwq_ref[...] += _dot(dq_b, cqn_ref[...], 0, 0)
        dcq, dgq = _norm_bwd(cq_ref[...], gq_ref[...], _dot(dq_b, wq_ref[...], 1, 0))
        dcq_ref[...] = dcq.astype(dcq_ref.dtype)
        dgq_ref[...] += dgq

        dkv_b = dkv[...]
        dwkv_ref[...] += _dot(dkv_b, ckvn_ref[...], 0, 0)
        dckv, dgkv = _norm_bwd(ckv_ref[...], gkv_ref[...], _dot(dkv_b, wkv_ref[...], 1, 0))
        dckv_ref[...] = dckv.astype(dckv_ref.dtype)
        dgkv_ref[...] += dgkv

    row = lambda w, c: pl.BlockSpec((tm, w), lambda i: (i, c))
    full = lambda shape: pl.BlockSpec(shape, lambda i: (0, 0))
    of_layer = lambda a: _layer_spec(a.shape[1:], lambda i: (0, 0), layer)
    return pl.pallas_call(
        body, name=name, grid=(s // tm,),
        in_specs=[row(Q_RANK, CQ_OFF // Q_RANK), row(KV_RANK, CKV_OFF // KV_RANK), row(Q_RANK, 0), row(KV_RANK, 0),
                  row(2048, 0), row(2048, 0), row(1024, 0), row(128, 0), row(128, 0), full(gq.shape), full(gkv.shape),
                  of_layer(wq), of_layer(wkv), row(256, 0), row(256, 0), row(128, 0), row(128, 0)],
        out_specs=[row(1024, 0), full(wq.shape[1:]), full(wkv.shape[1:]), full(gq.shape), full(gkv.shape)],
        out_shape=[jax.ShapeDtypeStruct((s, 1024), MXU_DTYPE), jax.ShapeDtypeStruct(wq.shape[1:], F32),
                   jax.ShapeDtypeStruct(wkv.shape[1:], F32), jax.ShapeDtypeStruct(gq.shape, F32),
                   jax.ShapeDtypeStruct(gkv.shape, F32)],
        scratch_shapes=[pltpu.VMEM((tm, 2048), MXU_DTYPE), pltpu.VMEM((tm, 2048), MXU_DTYPE)],
        compiler_params=_cparams("arbitrary"),
    )(proj, proj, cqn, ckvn, dqcat, dkcat, dv, dka, dva, gq, gkv, wq, wkv, tq_c, tq_s, tk_c, tk_s)


def _causal_mask(t):
    return lax.broadcasted_iota(jnp.int32, (t, t), 1) <= lax.broadcasted_iota(jnp.int32, (t, t), 0)


def _mla_fwd(qcat, kcat, v, *, name):
    s = qcat.shape[0]
    t = min(512, s)
    nq = s // t
    hp = HEADS_PER_STEP
    ng = MLA_HEADS // hp
    c2 = MLA_SCALE * LOG2E

    def body(q_ref, k_ref, v_ref, o_ref, lse_ref, top_s, acc_s):
        qi = pl.program_id(1)
        rows = lambda j: pl.ds(pl.multiple_of(j * t, t), t)
        head = lambda e: slice(e * 256, (e + 1) * 256)
        raw = lambda e, j: _dot(q_ref[:, head(e)], k_ref[rows(j), head(e)], 1, 1)

        for e in range(hp):
            top_s[e] = jnp.where(_causal_mask(t), raw(e, qi), NEG)

        def pass1(j, carry):
            for e in range(hp):
                top_s[e] = jnp.maximum(top_s[e], raw(e, j))
            return carry

        lax.fori_loop(0, qi, pass1, 0)
        m = [jnp.max(top_s[e], axis=-1, keepdims=True) * c2 for e in range(hp)]

        def weighted(e, j, masked):
            sc = raw(e, j) * c2 - m[e]
            if masked:
                sc = jnp.where(_causal_mask(t), sc, NEG)
            return jnp.dot(jnp.exp2(sc).astype(MXU_DTYPE), v_ref[rows(j), head(e)], preferred_element_type=F32)

        for e in range(hp):
            acc_s[e] = weighted(e, qi, True)

        def pass2(j, carry):
            for e in range(hp):
                acc_s[e] += weighted(e, j, False)
            return carry

        lax.fori_loop(0, qi, pass2, 0)
        lane = lax.broadcasted_iota(jnp.int32, (t, 128), 1)
        stats = jnp.zeros((t, 128), F32)
        for e in range(hp):
            l = acc_s[e, :, 128:]
            o_ref[:, e * 128:(e + 1) * 128] = acc_s[e, :, :128] / l
            stats = jnp.where(lane == e, m[e] + jnp.log(l) * LOG2E, stats)
        lse_ref[...] = stats

    return pl.pallas_call(
        body, name=name, grid=(ng, nq),
        in_specs=[pl.BlockSpec((t, 256 * hp), lambda g, qi: (qi, g)), pl.BlockSpec((s, 256 * hp), lambda g, qi: (0, g)),
                  pl.BlockSpec((s, 256 * hp), lambda g, qi: (0, g))],
        out_specs=[pl.BlockSpec((t, 128 * hp), lambda g, qi: (qi, g)), pl.BlockSpec((t, 128), lambda g, qi: (qi, g))],
        out_shape=[jax.ShapeDtypeStruct((s, 1024), F32), jax.ShapeDtypeStruct((s, 128 * ng), F32)],
        scratch_shapes=[pltpu.VMEM((hp, t, t), F32), pltpu.VMEM((hp, t, 256), F32)],
        compiler_params=_cparams("parallel", "arbitrary"),
    )(qcat, kcat, v)


def _mla_bwd(qcat, kcat, v, do, lse, delta, *, name):
    s = qcat.shape[0]
    t = min(512, s)
    nq = s // t
    hp = HEADS_PER_STEP
    c2 = MLA_SCALE * LOG2E

    def body(q_ref, k_ref, v_ref, do_ref, lse_ref, dl_ref, dq_ref, dk_ref, dv_ref, dk_acc, dv_acc):
        h, ki = pl.program_id(0), pl.program_id(1)

        @pl.when(ki == 0)
        def _():
            dq_ref[...] = jnp.zeros_like(dq_ref)

        dk_acc[...] = jnp.zeros_like(dk_acc)
        dv_acc[...] = jnp.zeros_like(dv_acc)
        k, vv = k_ref[...], v_ref[...]
        mine = lax.broadcasted_iota(jnp.int32, (t, 128), 1) == h % hp

        def chunk(qi, masked):
            rows = pl.ds(pl.multiple_of(qi * t, t), t)
            q, dob = q_ref[rows, :], do_ref[rows, :]
            pick = lambda r: jnp.sum(jnp.where(mine, r[rows, :], 0.0), axis=-1, keepdims=True)
            sc = _dot(q, k, 1, 1) * c2
            if masked:
                sc = jnp.where(_causal_mask(t), sc, NEG)
            p = jnp.exp2(sc - pick(lse_ref))
            dp = _dot(dob, vv, 1, 1)
            ds = (p * (dp - pick(dl_ref)) * MLA_SCALE).astype(MXU_DTYPE)
            dv_acc[...] += _dot(p.astype(MXU_DTYPE), dob, 0, 0)
            dk_acc[...] += _dot(ds, q, 0, 0)
            dq_ref[rows, :] += jnp.dot(ds, k, preferred_element_type=F32)

        def unmasked(qi, carry):
            chunk(qi, False)
            return carry

        chunk(ki, True)
        lax.fori_loop(ki + 1, nq, unmasked, 0)
        dk_ref[...] = dk_acc[...]
        dv_ref[...] = dv_acc[...]

    head = lambda w: pl.BlockSpec((s, w), lambda h, ki: (0, h))
    blk = lambda w: pl.BlockSpec((t, w), lambda h, ki: (ki, h))
    stat = pl.BlockSpec((s, 128), lambda h, ki: (0, h // hp))
    return pl.pallas_call(
        body, name=name, grid=(MLA_HEADS, nq),
        in_specs=[head(256), blk(256), pl.BlockSpec((t, 128), lambda h, ki: (ki, 2 * h)), head(128), stat, stat],
        out_specs=[head(256), blk(256), blk(128)],
        out_shape=[jax.ShapeDtypeStruct((s, 2048), F32), jax.ShapeDtypeStruct((s, 2048), F32),
                   jax.ShapeDtypeStruct((s, 1024), F32)],
        scratch_shapes=[pltpu.VMEM((t, 256), F32), pltpu.VMEM((t, 128), F32)],
        compiler_params=_cparams("parallel", "arbitrary"),
    )(qcat, kcat, v, do, lse, delta)


def _gate_specs(tm):
    half = lambda c: pl.BlockSpec((tm, 1024), lambda i: (i, c))
    return half(0), half(GA_OFF // 1024), half(GB_OFF // 1024)


def _gate_fwd(oa, ob, proj, *, name):
    s = oa.shape[0]
    tm = min(512, s)

    def body(oa_ref, ob_ref, ga_ref, gb_ref, y_ref):
        ga, gb = ga_ref[...], gb_ref[...]
        y_ref[:, :1024] = (oa_ref[...] * (ga * jax.nn.sigmoid(ga))).astype(MXU_DTYPE)
        y_ref[:, 1024:] = (ob_ref[...] * (gb * jax.nn.sigmoid(gb))).astype(MXU_DTYPE)

    o_spec, ga_spec, gb_spec = _gate_specs(tm)
    return pl.pallas_call(
        body, name=name, grid=(s // tm,), in_specs=[o_spec, o_spec, ga_spec, gb_spec],
        out_specs=pl.BlockSpec((tm, 2048), lambda i: (i, 0)),
        out_shape=jax.ShapeDtypeStruct((s, 2048), MXU_DTYPE),
        compiler_params=_cparams("parallel"),
    )(oa, ob, proj, proj)


def _gate_bwd(dy, oa, ob, proj, *, name):
    s = oa.shape[0]
    tm = min(512, s)

    def body(dy_ref, oa_ref, ob_ref, ga_ref, gb_ref, doa_ref, dob_ref, dga_ref, dgb_ref, dl_ref):
        def branch(dyv, o, g, do_ref, dg_ref):
            sg = jax.nn.sigmoid(g)
            do = dyv * (g * sg)
            do_ref[...] = do.astype(MXU_DTYPE)
            dg_ref[...] = (dyv * o * (sg * (1.0 + g * (1.0 - sg)))).astype(MXU_DTYPE)
            return do

        branch(dy_ref[:, :1024], oa_ref[...], ga_ref[...], doa_ref, dga_ref)
        ob = ob_ref[...]
        prod = branch(dy_ref[:, 1024:], ob, gb_ref[...], dob_ref, dgb_ref) * ob
        lane = lax.broadcasted_iota(jnp.int32, (tm, stat_w), 1)
        acc = jnp.zeros((tm, stat_w), F32)
        for hh in range(MLA_HEADS):
            at = (hh // HEADS_PER_STEP) * 128 + hh % HEADS_PER_STEP
            acc = jnp.where(lane == at, jnp.sum(prod[:, hh * 128:(hh + 1) * 128], axis=-1, keepdims=True), acc)
        dl_ref[...] = acc

    stat_w = 128 * (MLA_HEADS // HEADS_PER_STEP)
    o_spec, ga_spec, gb_spec = _gate_specs(tm)
    return pl.pallas_call(
        body, name=name, grid=(s // tm,),
        in_specs=[pl.BlockSpec((tm, 2048), lambda i: (i, 0)), o_spec, o_spec, ga_spec, gb_spec],
        out_specs=[o_spec, o_spec, o_spec, o_spec, pl.BlockSpec((tm, stat_w), lambda i: (i, 0))],
        out_shape=[jax.ShapeDtypeStruct((s, 1024), MXU_DTYPE)] * 4 + [jax.ShapeDtypeStruct((s, stat_w), F32)],
        compiler_params=_cparams("parallel"),
    )(dy, oa, ob, proj, proj)


def _row_block(rows, cols, itemsize=4, budget=2 << 20):
    fits = [tr for tr in range(16, rows + 1, 16) if rows % tr == 0 and tr * cols * itemsize <= budget]
    return fits[-1] if fits else rows


def _adamw(w, g, m, v, *, name):
    shape = w.shape
    rows, cols = shape[-2:]
    w3, g3, m3, v3 = (a.reshape((-1, rows, cols)) for a in (w, g, m, v))
    lead = w3.shape[0]
    tr = _row_block(rows, cols)

    def body(w_ref, g_ref, m_ref, v_ref, d_ref, mo_ref, vo_ref):
        gv = g_ref[...]
        mn = ADAM_B1 * m_ref[...] + (1.0 - ADAM_B1) * gv
        vn = ADAM_B2 * v_ref[...] + (1.0 - ADAM_B2) * jnp.square(gv)
        m_hat = mn / (1.0 - ADAM_B1 ** ADAM_STEP)
        v_hat = vn / (1.0 - ADAM_B2 ** ADAM_STEP)
        d_ref[...] = -ADAM_LR * (m_hat / (jnp.sqrt(v_hat) + ADAM_EPS) + ADAM_WD * w_ref[...])
        mo_ref[...] = mn
        vo_ref[...] = vn

    spec = pl.BlockSpec((None, tr, cols), lambda a, i: (a, i, 0))
    outs = pl.pallas_call(
        body, name=name, grid=(lead, rows // tr), in_specs=[spec] * 4, out_specs=[spec] * 3,
        out_shape=[jax.ShapeDtypeStruct((lead, rows, cols), F32)] * 3,
        compiler_params=_cparams("parallel", "parallel"),
    )(w3, g3, m3, v3)
    return tuple(o.reshape(shape) for o in outs)


def _pair_sum(where, grads, recv, *, name):
    depth, chips, rows, cols = grads.shape
    hl = depth // 2
    tr = _row_block(rows, cols)

    def body(where_ref, a_ref, b_ref, o_ref):
        o_ref[...] = (a_ref[...] + b_ref[...].astype(F32)).astype(WIRE_DTYPE)

    spec = pl.BlockSpec((None, None, tr, cols), lambda a, k, i, w: (a, k, i, 0))
    return pl.pallas_call(
        body, name=name,
        grid_spec=pltpu.PrefetchScalarGridSpec(
            num_scalar_prefetch=1, grid=(hl, chips, rows // tr),
            in_specs=[pl.BlockSpec((None, None, tr, cols), lambda a, k, i, w: (hl * w[4] + a, k, i, 0)), spec],
            out_specs=spec),
        out_shape=jax.ShapeDtypeStruct((hl, chips, rows, cols), WIRE_DTYPE),
        compiler_params=_cparams("parallel", "parallel", "parallel"),
    )(where, grads, recv)


def _chip_sum(where, grads, recv, parts, *, name):
    depth, _, rows, cols = grads.shape
    hl = depth // 2
    tr = _row_block(rows, cols)

    def body(where_ref, a_ref, b_ref, t0_ref, t1_ref, t2_ref, o_ref):
        total = a_ref[...] + b_ref[...].astype(F32)
        for t_ref in (t0_ref, t1_ref, t2_ref):
            total = total + t_ref[...].astype(F32)
        o_ref[...] = total

    slot = lambda j: pl.BlockSpec((None, None, tr, cols), lambda a, i, w: (a, w[j], i, 0))
    return pl.pallas_call(
        body, name=name,
        grid_spec=pltpu.PrefetchScalarGridSpec(
            num_scalar_prefetch=1, grid=(hl, rows // tr),
            in_specs=[pl.BlockSpec((None, None, tr, cols), lambda a, i, w: (hl * w[4] + a, w[0], i, 0)), slot(0), slot(1),
                      slot(2), slot(3)],
            out_specs=pl.BlockSpec((None, tr, cols), lambda a, i, w: (hl * w[4] + a, i, 0))),
        out_shape=jax.ShapeDtypeStruct((depth, rows, cols), F32),
        compiler_params=_cparams("parallel", "parallel"),
    )(where, grads, recv, parts, parts, parts)


def _place():
    x, y, c = lax.axis_index("x"), lax.axis_index("y"), lax.axis_index("c")
    chips = [(1 - x, y), (x, 1 - y), (1 - x, 1 - y)]
    return x, y, c, chips


def _sems(*shape):
    return [pltpu.SemaphoreType.DMA(shape), pltpu.SemaphoreType.DMA(shape)]


def _comm_gather_weights(shards, *, name):
    nt = len(shards)
    depth = shards[0].shape[0]
    hl = depth // 2

    def body(*refs):
        src, out, (send_sems, recv_sems, local_sems) = refs[:nt], refs[nt:2 * nt], refs[2 * nt:]
        x, y, c, chips = _place()
        k = 2 * x + y
        mine = pl.ds(c * hl, hl)
        other = pl.ds((1 - c) * hl, hl)

        def rows(t, slot):
            n = src[t].shape[1]
            return pl.ds(pl.multiple_of(slot * n, 16), n)

        def copy(sem, t, src_ref, layers, slot, to):
            return pltpu.make_async_remote_copy(src_ref=src_ref, dst_ref=out[t].at[layers, rows(t, slot)],
                                                send_sem=send_sems.at[sem, t], recv_sem=recv_sems.at[sem, t],
                                                device_id=to, device_id_type=MESH)

        first = [copy(j, t, src[t].at[mine], mine, k, (*chip, c)) for j, chip in enumerate(chips) for t in range(nt)]
        for cp in first:
            cp.start()
        local = [pltpu.make_async_copy(src[t].at[l], out[t].at[l, rows(t, k)], local_sems.at[t, l])
                 for t in range(nt) for l in range(depth)]
        for cp in local:
            cp.start()
        passed = []
        for j, (px, py) in enumerate(chips):
            slot = 2 * px + py
            for t in range(nt):
                copy(j, t, src[t].at[mine], mine, slot, (px, py, c)).wait_recv()
                cp = copy(3 + j, t, out[t].at[mine, rows(t, slot)], mine, slot, (x, y, 1 - c))
                cp.start()
                passed.append(cp)
        for j, (px, py) in enumerate(chips):
            for t in range(nt):
                copy(3 + j, t, out[t].at[other, rows(t, 2 * px + py)], other, 2 * px + py, (x, y, 1 - c)).wait_recv()
        for cp in first + passed:
            cp.wait_send()
        for cp in local:
            cp.wait()

    return pl.pallas_call(
        body, name=name, in_specs=[ANY] * nt, out_specs=[ANY] * nt,
        out_shape=[jax.ShapeDtypeStruct((a.shape[0], 4 * a.shape[1], a.shape[2]), a.dtype) for a in shards],
        scratch_shapes=_sems(6, nt) + [pltpu.SemaphoreType.DMA((nt, depth))],
    )(*shards)


def _comm_swap_sibling(bufs, *, name):
    nt = len(bufs)

    def body(*refs):
        src, out, (send_sems, recv_sems) = refs[:nt], refs[nt:2 * nt], refs[2 * nt:]
        x, y, c, _ = _place()
        cps = [pltpu.make_async_remote_copy(src_ref=src[t], dst_ref=out[t], send_sem=send_sems.at[t], recv_sem=recv_sems.at[t],
                                            device_id=(x, y, 1 - c), device_id_type=MESH) for t in range(nt)]
        for cp in cps:
            cp.start()
        for cp in cps:
            cp.wait()

    return pl.pallas_call(
        body, name=name, in_specs=[ANY] * nt, out_specs=[ANY] * nt,
        out_shape=[jax.ShapeDtypeStruct(a.shape, a.dtype) for a in bufs], scratch_shapes=_sems(nt),
    )(*bufs)


def _comm_scatter_chips(parts, *, name):
    nt = len(parts)

    def body(*refs):
        src, out, (send_sems, recv_sems) = refs[:nt], refs[nt:2 * nt], refs[2 * nt:]
        x, y, c, chips = _place()
        k = 2 * x + y
        sends = []
        for j, (px, py) in enumerate(chips):
            for t in range(nt):
                cp = pltpu.make_async_remote_copy(src_ref=src[t].at[:, 2 * px + py], dst_ref=out[t].at[:, k],
                                                  send_sem=send_sems.at[j, t], recv_sem=recv_sems.at[j, t],
                                                  device_id=(px, py, c), device_id_type=MESH)
                cp.start()
                sends.append(cp)
        for j, (px, py) in enumerate(chips):
            for t in range(nt):
                pltpu.make_async_remote_copy(src_ref=src[t].at[:, k], dst_ref=out[t].at[:, 2 * px + py],
                                             send_sem=send_sems.at[j, t], recv_sem=recv_sems.at[j, t],
                                             device_id=(px, py, c), device_id_type=MESH).wait_recv()
        for cp in sends:
            cp.wait_send()

    return pl.pallas_call(
        body, name=name, in_specs=[ANY] * nt, out_specs=[ANY] * nt,
        out_shape=[jax.ShapeDtypeStruct(a.shape, a.dtype) for a in parts], scratch_shapes=_sems(3, nt),
    )(*parts)


def _comm_join_halves(bufs, *, name):
    nt = len(bufs)
    hl = bufs[0].shape[0] // 2

    def body(*refs):
        src, out, (send_sems, recv_sems) = refs[:nt], refs[nt:2 * nt], refs[2 * nt:]
        x, y, c, _ = _place()
        mine = pl.ds(c * hl, hl)
        other = pl.ds((1 - c) * hl, hl)
        copy = lambda t, layers: pltpu.make_async_remote_copy(
            src_ref=src[t].at[mine], dst_ref=out[t].at[layers], send_sem=send_sems.at[t], recv_sem=recv_sems.at[t],
            device_id=(x, y, 1 - c), device_id_type=MESH)
        sends = [copy(t, mine) for t in range(nt)]
        for cp in sends:
            cp.start()
        for t in range(nt):
            copy(t, other).wait_recv()
        for cp in sends:
            cp.wait_send()

    return pl.pallas_call(
        body, name=name, in_specs=[ANY] * nt, out_specs=[ANY] * nt,
        out_shape=[jax.ShapeDtypeStruct(a.shape, a.dtype) for a in bufs],
        input_output_aliases={t: t for t in range(nt)}, scratch_shapes=_sems(nt),
    )(*bufs)


def _comm_allreduce_small(part, *, name):
    rows, cols = part.shape

    def body(p_ref, o_ref, buf, send_sems, recv_sems):
        x, y, c, _ = _place()
        me = 4 * x + 2 * y + c
        buf[me] = p_ref[...]
        flip = lambda v, bit: 1 - v if bit else v
        peers = [(flip(x, d & 4), flip(y, d & 2), flip(c, d & 1)) for d in range(1, 8)]
        sends = []
        for j, peer in enumerate(peers):
            cp = pltpu.make_async_remote_copy(src_ref=buf.at[me], dst_ref=buf.at[me], send_sem=send_sems.at[j],
                                              recv_sem=recv_sems.at[j], device_id=peer, device_id_type=MESH)
            cp.start()
            sends.append(cp)
        for j, (px, py, pc) in enumerate(peers):
            pltpu.make_async_remote_copy(src_ref=buf.at[me], dst_ref=buf.at[4 * px + 2 * py + pc], send_sem=send_sems.at[j],
                                         recv_sem=recv_sems.at[j], device_id=(px, py, pc), device_id_type=MESH).wait_recv()
        for cp in sends:
            cp.wait_send()
        total = buf[0]
        for i in range(1, 8):
            total = total + buf[i]
        o_ref[...] = total

    vm = pl.BlockSpec(memory_space=pltpu.VMEM)
    return pl.pallas_call(
        body, name=name, in_specs=[vm], out_specs=vm, out_shape=jax.ShapeDtypeStruct((rows, cols), F32),
        scratch_shapes=[pltpu.VMEM((8, rows, cols), F32), pltpu.SemaphoreType.DMA((7,)), pltpu.SemaphoreType.DMA((7,))],
    )(part)


def _pad_in_rows(wt):
    r = lambda o, n: wt[..., o:o + n, :]
    kr = r(2944, 64)
    return jnp.concatenate([r(0, 1024), r(1280, 1024), r(3008, 1024), r(2304, Q_RANK), r(1024, 128), r(2688, KV_RANK),
                            r(1152, 128), kr, jnp.zeros_like(kr)], axis=-2)


def _unpad_in_rows(qa, ga, gb, mixed):
    cq, ka, ckv, va, kr = (mixed[..., o - CQ_OFF:o - CQ_OFF + n, :] for o, n in (
        (CQ_OFF, Q_RANK), (KA_OFF, 128), (CKV_OFF, KV_RANK), (VA_OFF, 128), (KR_OFF, 64)))
    return jnp.concatenate([qa, ka, va, ga, cq, ckv, kr, gb], axis=-2)


def _pad_q_rows(wt):
    lead, cols = wt.shape[:-2], wt.shape[-1]
    wt = jnp.pad(wt.reshape(lead + (MLA_HEADS, 192, cols)), [(0, 0)] * (len(lead) + 1) + [(0, 64), (0, 0)])
    return wt.reshape(lead + (MLA_HEADS * 256, cols))


def _unpad_q_rows(wt):
    lead, cols = wt.shape[:-2], wt.shape[-1]
    return wt.reshape(lead + (MLA_HEADS, 256, cols))[..., :192, :].reshape(lead + (MLA_HEADS * 192, cols))


def _perm_kv_rows(wt):
    lead, cols = wt.shape[:-2], wt.shape[-1]
    return jnp.swapaxes(wt.reshape(lead + (MLA_HEADS, 2, 128, cols)), -4, -3).reshape(lead + (2048, cols))


def _unperm_kv_rows(wt):
    lead, cols = wt.shape[:-2], wt.shape[-1]
    return jnp.swapaxes(wt.reshape(lead + (2, MLA_HEADS, 128, cols)), -4, -3).reshape(lead + (2048, cols))


def _t(a):
    return jnp.swapaxes(a, -1, -2)


def _rope_tables(s):
    pos = jnp.arange(s, dtype=F32)
    inv_freq = 10000.0 ** (-jnp.arange(0, 64, 2, dtype=F32) / 64)
    ang = pos[:, None] * inv_freq[None, :]
    cos, sin = jnp.cos(ang), jnp.sin(ang)
    z64 = jnp.zeros((s, 64), F32)
    tk_c = jnp.concatenate([cos, cos, z64], axis=-1)
    tk_s = jnp.concatenate([-sin, sin, z64], axis=-1)
    tq_c = jnp.concatenate([jnp.ones((s, 128), F32), tk_c], axis=-1)
    tq_s = jnp.concatenate([jnp.zeros((s, 128), F32), tk_s], axis=-1)
    return tq_c, tq_s, tk_c, tk_s


def _device_step(xs, tgt, attn_g, sinks, gq, gkv, final_g, w_in_p, w_q_p, w_kv_p, w_o):
    depth = w_in_p.shape[0]
    s = xs.shape[0]
    tabs = _rope_tables(s)
    saved = []
    x = xs
    for l in range(depth):
        h = _rmsnorm_fwd(x, attn_g[l:l + 1], name=f"norm_fwd{l}")
        proj = _matmul(h, w_in_p, tb=True, b_layer=l, name=f"in_proj{l}")
        oa, lse_a = _swa_fwd(proj, sinks[l], name=f"swa_fwd{l}")
        qcat, kcat, v, cqn, ckvn = _mla_qkv_fwd(proj, gq[l:l + 1], gkv[l:l + 1], w_q_p, w_kv_p, *tabs, layer=l,
                                                name=f"mla_qkv_fwd{l}")
        ob, lse_b = _mla_fwd(qcat, kcat, v, name=f"mla_fwd{l}")
        y = _gate_fwd(oa, ob, proj, name=f"gate_fwd{l}")
        x_next = _matmul(y, w_o, add=x, b_layer=l, name=f"out_proj{l}")
        saved.append((x, h, proj, oa, lse_a, qcat, kcat, v, cqn, ckvn, ob, lse_b, y))
        x = x_next

    dx, d_final_g, loss = _final_loss(x, final_g, tgt, name="final_loss")

    d_attn_g, d_sinks, d_gq, d_gkv = [None] * depth, [None] * depth, [None] * depth, [None] * depth
    d_w_in, d_w_q, d_w_kv, d_w_o = [None] * depth, [None] * depth, [None] * depth, [None] * depth
    for l in reversed(range(depth)):
        x, h, proj, oa, lse_a, qcat, kcat, v, cqn, ckvn, ob, lse_b, y = saved[l]
        dy = _matmul(dx, w_o, tb=True, b_layer=l, name=f"out_proj_dx{l}")
        d_w_o[l] = _matmul(y, dx, ta=True, name=f"out_proj_dw{l}")
        doa, dob, dga, dgb, delta_b = _gate_bwd(dy, oa, ob, proj, name=f"gate_bwd{l}")
        dqa, dka, dva, dsink = _swa_bwd(proj, sinks[l], lse_a, doa, name=f"swa_bwd{l}")
        dqc, dkc, dv = _mla_bwd(qcat, kcat, v, dob, lse_b, delta_b, name=f"mla_bwd{l}")
        mixed, d_w_q[l], d_w_kv[l], dgq_l, dgkv_l = _mla_qkv_bwd(
            proj, cqn, ckvn, dqc, dkc, dv, dka, dva, gq[l:l + 1], gkv[l:l + 1], w_q_p, w_kv_p, *tabs, layer=l,
            name=f"mla_qkv_bwd{l}")
        dproj = [dqa, dga, dgb, mixed]
        dh = _matmul_ktiles(dproj, w_in_p, b_layer=l, name=f"in_proj_dx{l}")
        d_w_in[l] = [_matmul(tile, h, ta=True, name=f"in_proj_dw{l}_{j}") for j, tile in enumerate(dproj)]
        dx, dg_l = _rmsnorm_bwd(dh, x, attn_g[l:l + 1], dx, name=f"norm_bwd{l}")
        d_attn_g[l], d_sinks[l], d_gq[l], d_gkv[l] = dg_l, dsink[0:1, :SWA_HEADS], dgq_l, dgkv_l

    cat = lambda parts: jnp.concatenate(parts, axis=0)
    return (loss, dx, cat(d_attn_g), cat(d_sinks), cat(d_gq), cat(d_gkv), d_final_g, d_w_in, d_w_q, d_w_kv, d_w_o)


def kernel(x, attn_norm_g, w_in, swa_sinks, q_a_norm_g, kv_a_norm_g, w_q_b, w_kv_b, w_out, final_norm_g, loss_target, m_attn_norm_g, m_w_in, m_swa_sinks, m_q_a_norm_g, m_kv_a_norm_g, m_w_q_b, m_w_kv_b, m_w_out, m_final_norm_g, v_attn_norm_g, v_w_in, v_swa_sinks, v_q_a_norm_g, v_kv_a_norm_g, v_w_q_b, v_w_kv_b, v_w_out, v_final_norm_g):
    depth = w_in.shape[0]
    hl = depth // 2
    x_, y_, c = lax.axis_index("x"), lax.axis_index("y"), lax.axis_index("c")
    chip = 2 * x_ + y_
    where = jnp.stack([chip, 2 * (1 - x_) + y_, 2 * x_ + 1 - y_, 2 * (1 - x_) + 1 - y_, c]).astype(jnp.int32)

    sent = [a.astype(WIRE_DTYPE) for a in (_t(w_in), _t(w_q_b), _t(w_kv_b), w_out)]
    full_in, full_q, full_kv, full_o = _comm_gather_weights(sent, name="comm_gather_weights")

    (loss, dx, d_attn_g, d_sinks, d_gq, d_gkv, d_final_g, d_w_in, d_w_q, d_w_kv, d_w_o) = _device_step(
        x[0], loss_target[0], attn_norm_g, swa_sinks, q_a_norm_g, kv_a_norm_g, final_norm_g.reshape(1, -1),
        _pad_in_rows(full_in), _pad_q_rows(full_q), _perm_kv_rows(full_kv), full_o)

    in_tiles = [jnp.stack([d_w_in[l][j] for l in range(depth)]) for j in range(4)]
    full_grads = [_unpad_in_rows(*in_tiles), _unpad_q_rows(jnp.stack(d_w_q)), _unperm_kv_rows(jnp.stack(d_w_kv)),
                  jnp.stack(d_w_o)]
    grads4 = [g.reshape(depth, 4, g.shape[1] // 4, g.shape[2]) for g in full_grads]
    give = [lax.dynamic_slice_in_dim(g, (1 - c) * hl, hl, axis=0).astype(WIRE_DTYPE) for g in grads4]
    recv = _comm_swap_sibling(give, name="comm_swap_sibling")
    pair = [_pair_sum(where, g, r, name=f"pair_sum{t}") for t, (g, r) in enumerate(zip(grads4, recv))]
    parts = _comm_scatter_chips(pair, name="comm_scatter_chips")
    mine = [_chip_sum(where, g, r, p, name=f"chip_sum{t}") for t, (g, r, p) in enumerate(zip(grads4, recv, parts))]
    reduced = _comm_join_halves(mine, name="comm_join_halves")
    g_w_in, g_w_q_b, g_w_kv_b, g_w_out = _t(reduced[0]), _t(reduced[1]), _t(reduced[2]), reduced[3]

    small = [d_attn_g, d_sinks, d_gq, d_gkv, d_final_g, loss[:, :1]]
    flat = jnp.concatenate([a.reshape(-1) for a in small])
    n_small = flat.shape[0]
    rows = -(-n_small // 1024) * 8
    total = _comm_allreduce_small(jnp.pad(flat, (0, rows * 128 - n_small)).reshape(rows, 128),
                                  name="comm_allreduce_small").reshape(-1)
    outs, at = [], 0
    for a in small:
        outs.append(total[at:at + a.size].reshape(a.shape))
        at += a.size
    g_attn_g, g_sinks, g_gq, g_gkv, g_final_g, loss_total = outs
    g_final_g = g_final_g.reshape(final_norm_g.shape)

    weights = [attn_norm_g, w_in, swa_sinks, q_a_norm_g, kv_a_norm_g, w_q_b, w_kv_b, w_out, final_norm_g]
    grads = [g_attn_g, g_w_in, g_sinks, g_gq, g_gkv, g_w_q_b, g_w_kv_b, g_w_out, g_final_g]
    ms = [m_attn_norm_g, m_w_in, m_swa_sinks, m_q_a_norm_g, m_kv_a_norm_g, m_w_q_b, m_w_kv_b, m_w_out, m_final_norm_g]
    vs = [v_attn_norm_g, v_w_in, v_swa_sinks, v_q_a_norm_g, v_kv_a_norm_g, v_w_q_b, v_w_kv_b, v_w_out, v_final_norm_g]
    as2d = lambda a: a.reshape(1, -1) if a.ndim == 1 else a
    deltas, new_m, new_v = [], [], []
    for i, (w, g, m, v) in enumerate(zip(weights, grads, ms, vs)):
        view = _t if w is w_in else as2d
        d, mn, vn = _adamw(view(w), reduced[0] if w is w_in else view(g), view(m), view(v), name=f"adamw{i}")
        back = _t if w is w_in else (lambda a: a.reshape(w.shape))
        deltas.append(back(d))
        new_m.append(back(mn))
        new_v.append(back(vn))

    return (loss_total.reshape(()), dx[None], *grads, *deltas, *new_m, *new_v)
```

```python
import functools
import math

import jax
import jax.numpy as jnp
from jax import lax
from jax.experimental import pallas as pl
from jax.experimental.pallas import tpu as pltpu

F32 = jnp.float32
MXU_DTYPE = jnp.bfloat16
WIRE_DTYPE = jnp.bfloat16

EPS = 1e-6
NEG = -1e30
BLOCK = 128
D_MODEL = 2048
SWA_HEADS = 16
MLA_HEADS = 8
Q_RANK = 384
KV_RANK = 256
IN_WIDTH = 4032
MLA_SCALE = 192 ** -0.5
SWA_SCALE = 64 ** -0.5
LOG2E = math.log2(math.e)
HEADS_PER_STEP = 2
SLOPES = tuple(2.0 ** (-8.0 * (h + 1) / SWA_HEADS) for h in range(SWA_HEADS))

P_WIDTH = 4096
QA_OFF, GA_OFF, GB_OFF, CQ_OFF, KA_OFF, CKV_OFF, VA_OFF, KR_OFF = 0, 1024, 2048, 3072, 3456, 3584, 3840, 3968

ADAM_LR, ADAM_B1, ADAM_B2, ADAM_EPS, ADAM_WD, ADAM_STEP = 0.001, 0.9, 0.999, 1e-08, 0.01, 10

VMEM_LIMIT = 56 * 1024 * 1024
MESH = pl.DeviceIdType.MESH
ANY = pl.BlockSpec(memory_space=pl.ANY)


def _cparams(*sem):
    return pltpu.CompilerParams(dimension_semantics=sem, vmem_limit_bytes=VMEM_LIMIT)


def _dot(a, b, ca, cb):
    return lax.dot_general(a, b, (((ca,), (cb,)), ((), ())), preferred_element_type=F32)


def _layer_spec(block, index_map, layer):
    if layer is None:
        return pl.BlockSpec(block, index_map)
    return pl.BlockSpec((None,) + tuple(block), lambda *g: (layer,) + tuple(index_map(*g)))


def _matmul(a, b, *, name, ta=False, tb=False, out_dtype=F32, add=None, b_layer=None, tm=1024, tn=1024, tk=2048):
    (kdim, m) = a.shape if ta else a.shape[::-1]
    (n, k2) = b.shape[-2:] if tb else b.shape[-2:][::-1]
    assert kdim == k2, (a.shape, b.shape)
    tm, tn, tk = min(tm, m), min(tn, n), min(tk, kdim)
    assert m % tm == 0 and n % tn == 0 and kdim % tk == 0
    nk = kdim // tk

    def body(*refs):
        a_ref, b_ref = refs[:2]
        add_ref = None if add is None else refs[2]
        o_ref = refs[2 + (add is not None)]
        part = _dot(a_ref[...].astype(MXU_DTYPE), b_ref[...].astype(MXU_DTYPE), 0 if ta else 1, 1 if tb else 0)

        def finish(r):
            o_ref[...] = (r if add is None else add_ref[...] + r).astype(out_dtype)

        if nk == 1:
            finish(part)
            return
        acc = refs[-1]
        k = pl.program_id(2)

        @pl.when(k == 0)
        def _():
            acc[...] = part

        @pl.when((k > 0) & (k < nk - 1))
        def _():
            acc[...] += part

        @pl.when(k == nk - 1)
        def _():
            finish(acc[...] + part)

    a_spec = pl.BlockSpec((tk, tm), lambda i, j, k: (k, i)) if ta else pl.BlockSpec((tm, tk), lambda i, j, k: (i, k))
    b_spec = (_layer_spec((tn, tk), lambda i, j, k: (j, k), b_layer) if tb else
              _layer_spec((tk, tn), lambda i, j, k: (k, j), b_layer))
    in_specs, args = [a_spec, b_spec], [a, b]
    if add is not None:
        in_specs.append(pl.BlockSpec((tm, tn), lambda i, j, k: (i, j)))
        args.append(add)
    return pl.pallas_call(
        body, name=name, grid=(m // tm, n // tn, nk), in_specs=in_specs,
        out_specs=pl.BlockSpec((tm, tn), lambda i, j, k: (i, j)),
        out_shape=jax.ShapeDtypeStruct((m, n), out_dtype),
        scratch_shapes=[pltpu.VMEM((tm, tn), F32)] if nk > 1 else [],
        compiler_params=_cparams("parallel", "parallel", "arbitrary"),
    )(*args)


def _matmul_ktiles(a_tiles, b, *, name, b_layer=None, tm=512, tn=1024):
    m, kt = a_tiles[0].shape
    n = b.shape[-1]
    nt = len(a_tiles)
    assert b.shape[-2] == nt * kt
    tm, tn = min(tm, m), min(tn, n)
    assert m % tm == 0 and n % tn == 0

    def body(*refs):
        a_refs, b_refs, o_ref = refs[:nt], refs[nt:2 * nt], refs[2 * nt]
        acc = _dot(a_refs[0][...].astype(MXU_DTYPE), b_refs[0][...].astype(MXU_DTYPE), 1, 0)
        for j in range(1, nt):
            acc += _dot(a_refs[j][...].astype(MXU_DTYPE), b_refs[j][...].astype(MXU_DTYPE), 1, 0)
        o_ref[...] = acc

    in_specs = [pl.BlockSpec((tm, kt), lambda i, jn: (i, 0))] * nt
    in_specs += [_layer_spec((kt, tn), lambda i, jn, j=j: (j, jn), b_layer) for j in range(nt)]
    return pl.pallas_call(
        body, name=name, grid=(m // tm, n // tn), in_specs=in_specs,
        out_specs=pl.BlockSpec((tm, tn), lambda i, jn: (i, jn)),
        out_shape=jax.ShapeDtypeStruct((m, n), F32),
        compiler_params=_cparams("parallel", "parallel"),
    )(*a_tiles, *([b] * nt))


def _rmsnorm_fwd(x, g, *, name):
    s, d = x.shape
    tm = min(512, s)

    def body(x_ref, g_ref, h_ref):
        xv = x_ref[...]
        r = lax.rsqrt(jnp.mean(xv * xv, axis=-1, keepdims=True) + EPS)
        h_ref[...] = (xv * r * g_ref[...]).astype(MXU_DTYPE)

    return pl.pallas_call(
        body, name=name, grid=(s // tm,),
        in_specs=[pl.BlockSpec((tm, d), lambda i: (i, 0)), pl.BlockSpec((1, d), lambda i: (0, 0))],
        out_specs=pl.BlockSpec((tm, d), lambda i: (i, 0)),
        out_shape=jax.ShapeDtypeStruct((s, d), MXU_DTYPE),
        compiler_params=_cparams("parallel"),
    )(x, g)


def _rmsnorm_bwd(dh, x, g, dres, *, name):
    s, d = x.shape
    tm = min(512, s)

    def body(dh_ref, x_ref, g_ref, dres_ref, dx_ref, dg_ref):
        @pl.when(pl.program_id(0) == 0)
        def _():
            dg_ref[...] = jnp.zeros_like(dg_ref)

        xv = x_ref[...]
        r = lax.rsqrt(jnp.mean(xv * xv, axis=-1, keepdims=True) + EPS)
        xn = xv * r
        dy = dh_ref[...]
        dg_ref[...] += jnp.sum(dy * xn, axis=0, keepdims=True)
        u = dy * g_ref[...]
        dx_ref[...] = dres_ref[...] + r * (u - xn * jnp.mean(u * xn, axis=-1, keepdims=True))

    row = pl.BlockSpec((tm, d), lambda i: (i, 0))
    vec = pl.BlockSpec((1, d), lambda i: (0, 0))
    return pl.pallas_call(
        body, name=name, grid=(s // tm,), in_specs=[row, row, vec, row], out_specs=[row, vec],
        out_shape=[jax.ShapeDtypeStruct((s, d), F32), jax.ShapeDtypeStruct((1, d), F32)],
        compiler_params=_cparams("arbitrary"),
    )(dh, x, g, dres)


def _final_loss(x, g, tgt, *, name):
    s, d = x.shape
    tm = min(512, s)

    def body(x_ref, g_ref, t_ref, dx_ref, dg_ref, loss_ref):
        @pl.when(pl.program_id(0) == 0)
        def _():
            dg_ref[...] = jnp.zeros_like(dg_ref)
            loss_ref[...] = jnp.zeros_like(loss_ref)

        xv = x_ref[...]
        gv = g_ref[...]
        r = lax.rsqrt(jnp.mean(xv * xv, axis=-1, keepdims=True) + EPS)
        xn = xv * r
        err = xn * gv - t_ref[...]
        sq = jnp.sum(jnp.sum(err * err, axis=-1, keepdims=True), axis=0, keepdims=True)
        loss_ref[...] += (0.5 / d) * sq
        dy = err * (1.0 / d)
        dg_ref[...] += jnp.sum(dy * xn, axis=0, keepdims=True)
        u = dy * gv
        dx_ref[...] = r * (u - xn * jnp.mean(u * xn, axis=-1, keepdims=True))

    row = pl.BlockSpec((tm, d), lambda i: (i, 0))
    vec = pl.BlockSpec((1, d), lambda i: (0, 0))
    return pl.pallas_call(
        body, name=name, grid=(s // tm,), in_specs=[row, vec, row],
        out_specs=[row, vec, pl.BlockSpec((1, 128), lambda i: (0, 0))],
        out_shape=[jax.ShapeDtypeStruct((s, d), F32), jax.ShapeDtypeStruct((1, d), F32),
                   jax.ShapeDtypeStruct((1, 128), F32)],
        compiler_params=_cparams("arbitrary"),
    )(x, g, tgt)


def _swa_keys(kp_ref, kc_ref):
    kk = jnp.concatenate([kp_ref[...], kc_ref[...]], axis=0)
    kr = pltpu.roll(kk, 64, 1)
    lo = lax.broadcasted_iota(jnp.int32, kk.shape, 1) < 64
    return [jnp.where(lo, kk, kr).astype(MXU_DTYPE), jnp.where(lo, kr, kk).astype(MXU_DTYPE)]


GROUP = SWA_HEADS // 2


def _swa_mask(n):
    qi = lax.broadcasted_iota(jnp.int32, (BLOCK, 2 * BLOCK), 0)
    ki = lax.broadcasted_iota(jnp.int32, (BLOCK, 2 * BLOCK), 1)
    delta = BLOCK + qi - ki
    valid = (delta >= 0) & (delta < BLOCK) & ((ki >= BLOCK) | (n > 0))
    return valid, delta.astype(F32)


def _stack_heads(ref, j):
    lo = lax.broadcasted_iota(jnp.int32, (BLOCK, BLOCK), 1) < 64
    parts = []
    for r in range(GROUP):
        pair = (GROUP * j + r) // 2
        blk = ref[:, pair * 128:(pair + 1) * 128].astype(F32)
        parts.append(jnp.where(lo if r % 2 == 0 else ~lo, blk, 0.0).astype(MXU_DTYPE))
    return jnp.concatenate(parts, axis=0)


def _unstack_heads(stacked, ref, j):
    lo = lax.broadcasted_iota(jnp.int32, (BLOCK, BLOCK), 1) < 64
    for i in range(GROUP // 2):
        pair = (GROUP * j) // 2 + i
        even, odd = stacked[2 * i * BLOCK:(2 * i + 1) * BLOCK], stacked[(2 * i + 1) * BLOCK:(2 * i + 2) * BLOCK]
        ref[:, pair * 128:(pair + 1) * 128] = jnp.where(lo, even, odd).astype(ref.dtype)


def _head_rows(stacked, r):
    return stacked[r * BLOCK:(r + 1) * BLOCK]


def _swa_scores(raw, h, valid, deltaf):
    return jnp.where(valid, raw * (SWA_SCALE * LOG2E) - (SLOPES[h] * LOG2E) * deltaf, NEG)


def _swa_specs(nb):
    kcol, vcol = KA_OFF // BLOCK, VA_OFF // BLOCK
    last = nb - 1
    cur = lambda n: jnp.minimum(n, last)
    prev = lambda n: jnp.maximum(jnp.minimum(n, last) - 1, 0)
    return [
        pl.BlockSpec(memory_space=pltpu.SMEM),
        pl.BlockSpec((BLOCK, 1024), lambda n: (cur(n), QA_OFF // 1024)),
        pl.BlockSpec((BLOCK, BLOCK), lambda n: (cur(n), kcol)),
        pl.BlockSpec((BLOCK, BLOCK), lambda n: (prev(n), kcol)),
        pl.BlockSpec((BLOCK, BLOCK), lambda n: (cur(n), vcol)),
        pl.BlockSpec((BLOCK, BLOCK), lambda n: (prev(n), vcol)),
    ]


def _swa_fwd(proj, sinks, *, name):
    s = proj.shape[0]
    nb = s // BLOCK

    def body(sink_ref, q_ref, kc_ref, kp_ref, vc_ref, vp_ref, o_ref, lse_ref):
        n = pl.program_id(0)
        keys = _swa_keys(kp_ref, kc_ref)
        vals = _swa_keys(vp_ref, vc_ref)
        valid, deltaf = _swa_mask(n)
        lane = lax.broadcasted_iota(jnp.int32, (BLOCK, BLOCK), 1)
        lse_acc = jnp.zeros((BLOCK, BLOCK), F32)
        for j in range(2):
            raw = _dot(_stack_heads(q_ref, j), keys[j], 1, 1)
            probs = []
            for r in range(GROUP):
                h = GROUP * j + r
                sc = _swa_scores(_head_rows(raw, r), h, valid, deltaf)
                sink = sink_ref[h] * LOG2E
                m = jnp.maximum(jnp.max(sc, axis=-1, keepdims=True), sink)
                p = jnp.exp2(sc - m)
                l = jnp.sum(p, axis=-1, keepdims=True) + jnp.exp2(sink - m)
                probs.append((p * (1.0 / l)).astype(MXU_DTYPE))
                lse_acc = jnp.where(lane == h, m + jnp.log(l) * LOG2E, lse_acc)
            _unstack_heads(jnp.dot(jnp.concatenate(probs, axis=0), vals[j], preferred_element_type=F32), o_ref, j)
        lse_ref[...] = lse_acc

    return pl.pallas_call(
        body, name=name, grid=(nb,), in_specs=_swa_specs(nb),
        out_specs=[pl.BlockSpec((BLOCK, 1024), lambda n: (n, 0)), pl.BlockSpec((BLOCK, BLOCK), lambda n: (n, 0))],
        out_shape=[jax.ShapeDtypeStruct((s, 1024), F32), jax.ShapeDtypeStruct((s, BLOCK), F32)],
        compiler_params=_cparams("parallel"),
    )(sinks, proj, proj, proj, proj, proj)


def _swa_bwd(proj, sinks, lse, do, *, name):
    s = proj.shape[0]
    nb = s // BLOCK
    last = nb - 1

    def body(sink_ref, q_ref, kc_ref, kp_ref, vc_ref, vp_ref, lse_ref, do_ref,
             dq_ref, dk_ref, dv_ref, dsink_ref, carry_k, carry_v):
        n = pl.program_id(0)

        @pl.when(n == 0)
        def _():
            carry_k[...] = jnp.zeros_like(carry_k)
            carry_v[...] = jnp.zeros_like(carry_v)
            dsink_ref[...] = jnp.zeros_like(dsink_ref)

        @pl.when(n < nb)
        def _():
            keys = _swa_keys(kp_ref, kc_ref)
            vals = _swa_keys(vp_ref, vc_ref)
            valid, deltaf = _swa_mask(n)
            lane = lax.broadcasted_iota(jnp.int32, (BLOCK, BLOCK), 1)
            lane1 = lax.broadcasted_iota(jnp.int32, (1, BLOCK), 1)
            lse_blk = lse_ref[...]
            acc_k, acc_v = [], []
            dsink = jnp.zeros((1, BLOCK), F32)
            for j in range(2):
                q_all, do_all = _stack_heads(q_ref, j), _stack_heads(do_ref, j)
                raw = _dot(q_all, keys[j], 1, 1)
                dp_all = _dot(do_all, vals[j], 1, 1)
                probs, dscores = [], []
                for r in range(GROUP):
                    h = GROUP * j + r
                    lse_h = jnp.sum(jnp.where(lane == h, lse_blk, 0.0), axis=-1, keepdims=True)
                    p = jnp.exp2(_swa_scores(_head_rows(raw, r), h, valid, deltaf) - lse_h)
                    dp = _head_rows(dp_all, r)
                    dlt = jnp.sum(dp * p, axis=-1, keepdims=True)
                    dscores.append((p * (dp - dlt) * SWA_SCALE).astype(MXU_DTYPE))
                    probs.append(p.astype(MXU_DTYPE))
                    sunk = jnp.exp2(sink_ref[h] * LOG2E - lse_h) * dlt
                    dsink = jnp.where(lane1 == h, -jnp.sum(sunk, axis=0, keepdims=True), dsink)
                ds_all = jnp.concatenate(dscores, axis=0)
                _unstack_heads(jnp.dot(ds_all, keys[j], preferred_element_type=F32), dq_ref, j)
                acc_k.append(_dot(ds_all, q_all, 0, 0))
                acc_v.append(_dot(jnp.concatenate(probs, axis=0), do_all, 0, 0))
            lo2 = lax.broadcasted_iota(jnp.int32, (2 * BLOCK, BLOCK), 1) < 64
            fold = lambda acc: jnp.where(lo2, acc[0] + pltpu.roll(acc[0], 64, 1), acc[1] + pltpu.roll(acc[1], 64, 1))
            dkk, dvv = fold(acc_k), fold(acc_v)
            dk_ref[...] = (carry_k[...] + dkk[:BLOCK]).astype(dk_ref.dtype)
            dv_ref[...] = (carry_v[...] + dvv[:BLOCK]).astype(dv_ref.dtype)
            carry_k[...] = dkk[BLOCK:]
            carry_v[...] = dvv[BLOCK:]
            dsink_ref[...] += jnp.broadcast_to(dsink, dsink_ref.shape)

        @pl.when(n == nb)
        def _():
            dk_ref[...] = carry_k[...].astype(dk_ref.dtype)
            dv_ref[...] = carry_v[...].astype(dv_ref.dtype)

    cur = lambda n: jnp.minimum(n, last)
    lag = lambda n: jnp.maximum(n - 1, 0)
    return pl.pallas_call(
        body, name=name, grid=(nb + 1,),
        in_specs=_swa_specs(nb) + [pl.BlockSpec((BLOCK, BLOCK), lambda n: (cur(n), 0)),
                                   pl.BlockSpec((BLOCK, 1024), lambda n: (cur(n), 0))],
        out_specs=[pl.BlockSpec((BLOCK, 1024), lambda n: (cur(n), 0)),
                   pl.BlockSpec((BLOCK, BLOCK), lambda n: (lag(n), 0)),
                   pl.BlockSpec((BLOCK, BLOCK), lambda n: (lag(n), 0)),
                   pl.BlockSpec((8, BLOCK), lambda n: (0, 0))],
        out_shape=[jax.ShapeDtypeStruct((s, 1024), MXU_DTYPE), jax.ShapeDtypeStruct((s, BLOCK), MXU_DTYPE),
                   jax.ShapeDtypeStruct((s, BLOCK), MXU_DTYPE), jax.ShapeDtypeStruct((8, BLOCK), F32)],
        scratch_shapes=[pltpu.VMEM((BLOCK, BLOCK), F32), pltpu.VMEM((BLOCK, BLOCK), F32)],
        compiler_params=_cparams("arbitrary"),
    )(sinks, proj, proj, proj, proj, proj, lse, do)


def _rope_partner(v, first, width):
    lane = lax.broadcasted_iota(jnp.int32, v.shape, 1)
    in_a = (lane >= first) & (lane < first + 32)
    in_b = (lane >= first + 32) & (lane < first + 64)
    return jnp.where(in_a, pltpu.roll(v, width - 32, 1), jnp.where(in_b, pltpu.roll(v, 32, 1), 0.0))


def _mla_qkv_fwd(proj, gq, gkv, wq, wkv, tq_c, tq_s, tk_c, tk_s, *, layer, name):
    s = proj.shape[0]
    tm = min(256, s)

    def body(cq_ref, ckv_ref, kr_ref, gq_ref, gkv_ref, wq_ref, wkv_ref, qc_ref, qs_ref, kc_ref, ks_ref,
             qcat_ref, kcat_ref, v_ref, cqn_ref, ckvn_ref):
        cq = cq_ref[...]
        cqn = (cq * lax.rsqrt(jnp.mean(cq * cq, axis=-1, keepdims=True) + EPS) * gq_ref[...]).astype(MXU_DTYPE)
        cqn_ref[...] = cqn
        qpre = _dot(cqn, wq_ref[...], 1, 1)
        qc, qs = qc_ref[...], qs_ref[...]
        for hh in range(MLA_HEADS):
            blk = qpre[:, hh * 256:(hh + 1) * 256]
            qcat_ref[:, hh * 256:(hh + 1) * 256] = (blk * qc + _rope_partner(blk, 128, 256) * qs).astype(MXU_DTYPE)
        ckv = ckv_ref[...]
        ckvn = (ckv * lax.rsqrt(jnp.mean(ckv * ckv, axis=-1, keepdims=True) + EPS) * gkv_ref[...]).astype(MXU_DTYPE)
        ckvn_ref[...] = ckvn
        kv = _dot(ckvn, wkv_ref[...], 1, 1)
        kr = kr_ref[...]
        krr = (kr * kc_ref[...] + _rope_partner(kr, 0, 128) * ks_ref[...]).astype(MXU_DTYPE)
        for hh in range(MLA_HEADS):
            kcat_ref[:, hh * 256:hh * 256 + 128] = kv[:, hh * 128:(hh + 1) * 128].astype(MXU_DTYPE)
            kcat_ref[:, hh * 256 + 128:(hh + 1) * 256] = krr
            v_ref[:, hh * 256:hh * 256 + 128] = kv[:, 1024 + hh * 128:1024 + (hh + 1) * 128].astype(MXU_DTYPE)
            v_ref[:, hh * 256 + 128:(hh + 1) * 256] = jnp.ones((tm, 128), MXU_DTYPE)

    row = lambda w, c: pl.BlockSpec((tm, w), lambda i: (i, c))
    full = lambda a: pl.BlockSpec(a.shape, lambda i: (0, 0))
    of_layer = lambda a: _layer_spec(a.shape[1:], lambda i: (0, 0), layer)
    return pl.pallas_call(
        body, name=name, grid=(s // tm,),
        in_specs=[row(Q_RANK, CQ_OFF // Q_RANK), row(KV_RANK, CKV_OFF // KV_RANK), row(128, KR_OFF // 128),
                  full(gq), full(gkv), of_layer(wq), of_layer(wkv), row(256, 0), row(256, 0), row(128, 0), row(128, 0)],
        out_specs=[row(2048, 0), row(2048, 0), row(2048, 0), row(Q_RANK, 0), row(KV_RANK, 0)],
        out_shape=[jax.ShapeDtypeStruct((s, 2048), MXU_DTYPE), jax.ShapeDtypeStruct((s, 2048), MXU_DTYPE),
                   jax.ShapeDtypeStruct((s, 2048), MXU_DTYPE), jax.ShapeDtypeStruct((s, Q_RANK), MXU_DTYPE),
                   jax.ShapeDtypeStruct((s, KV_RANK), MXU_DTYPE)],
        compiler_params=_cparams("parallel"),
    )(proj, proj, proj, gq, gkv, wq, wkv, tq_c, tq_s, tk_c, tk_s)


def _norm_bwd(x, g, dy):
    r = lax.rsqrt(jnp.mean(x * x, axis=-1, keepdims=True) + EPS)
    xn = x * r
    u = dy * g
    return r * (u - xn * jnp.mean(u * xn, axis=-1, keepdims=True)), jnp.sum(dy * xn, axis=0, keepdims=True)


def _mla_qkv_bwd(proj, cqn, ckvn, dqcat, dkcat, dv, dka, dva, gq, gkv, wq, wkv, tq_c, tq_s, tk_c, tk_s, *, layer,
                 name):
    s = proj.shape[0]
    tm = min(256, s)
    t_cq, t_ka, t_ckv, t_va, t_kr = (o - CQ_OFF for o in (CQ_OFF, KA_OFF, CKV_OFF, VA_OFF, KR_OFF))

    def body(cq_ref, ckv_ref, cqn_ref, ckvn_ref, dq_ref, dk_ref, dv_ref, dka_ref, dva_ref, gq_ref, gkv_ref, wq_ref,
             wkv_ref, qc_ref, qs_ref, kc_ref, ks_ref,
             tile_ref, dwq_ref, dwkv_ref, dgq_ref, dgkv_ref, dqpre, dkv):
        dcq_ref = tile_ref.at[:, t_cq:t_cq + Q_RANK]
        dckv_ref = tile_ref.at[:, t_ckv:t_ckv + KV_RANK]
        dkr_ref = tile_ref.at[:, t_kr:t_kr + 128]
        tile_ref[:, t_ka:t_ka + 128] = dka_ref[...]
        tile_ref[:, t_va:t_va + 128] = dva_ref[...]

        @pl.when(pl.program_id(0) == 0)
        def _():
            for r in (dwq_ref, dwkv_ref, dgq_ref, dgkv_ref):
                r[...] = jnp.zeros_like(r)

        qc, qs = qc_ref[...], qs_ref[...]
        dkrr = jnp.zeros((tm, 128), F32)
        for hh in range(MLA_HEADS):
            blk = dq_ref[:, hh * 256:(hh + 1) * 256]
            dqpre[:, hh * 256:(hh + 1) * 256] = (blk * qc + _rope_partner(blk * qs, 128, 256)).astype(MXU_DTYPE)
            dkv[:, hh * 128:(hh + 1) * 128] = dk_ref[:, hh * 256:hh * 256 + 128].astype(MXU_DTYPE)
            dkrr = dkrr + dk_ref[:, hh * 256 + 128:(hh + 1) * 256]
        dkv[:, 1024:] = dv_ref[...].astype(MXU_DTYPE)
        dkr_ref[...] = (dkrr * kc_ref[...] + _rope_partner(dkrr * ks_ref[...], 0, 128)).astype(dkr_ref.dtype)

        dq_b = dqpre[...]
        dwq_ref[...] += _dot(dq_b, cqn_ref[...], 0, 0)
        dcq, dgq = _norm_bwd(cq_ref[...], gq_ref[...], _dot(dq_b, wq_ref[...], 1, 0))
        dcq_ref[...] = dcq.astype(dcq_ref.dtype)
        dgq_ref[...] += dgq

        dkv_b = dkv[...]
        dwkv_ref[...] += _dot(dkv_b, ckvn_ref[...], 0, 0)
        dckv, dgkv = _norm_bwd(ckv_ref[...], gkv_ref[...], _dot(dkv_b, wkv_ref[...], 1, 0))
        dckv_ref[...] = dckv.astype(dckv_ref.dtype)
        dgkv_ref[...] += dgkv

    row = lambda w, c: pl.BlockSpec((tm, w), lambda i: (i, c))
    full = lambda shape: pl.BlockSpec(shape, lambda i: (0, 0))
    of_layer = lambda a: _layer_spec(a.shape[1:], lambda i: (0, 0), layer)
    return pl.pallas_call(
        body, name=name, grid=(s // tm,),
        in_specs=[row(Q_RANK, CQ_OFF // Q_RANK), row(KV_RANK, CKV_OFF // KV_RANK), row(Q_RANK, 0), row(KV_RANK, 0),
                  row(2048, 0), row(2048, 0), row(1024, 0), row(128, 0), row(128, 0), full(gq.shape), full(gkv.shape),
                  of_layer(wq), of_layer(wkv), row(256, 0), row(256, 0), row(128, 0), row(128, 0)],
        out_specs=[row(1024, 0), full(wq.shape[1:]), full(wkv.shape[1:]), full(gq.shape), full(gkv.shape)],
        out_shape=[jax.ShapeDtypeStruct((s, 1024), MXU_DTYPE), jax.ShapeDtypeStruct(wq.shape[1:], F32),
                   jax.ShapeDtypeStruct(wkv.shape[1:], F32), jax.ShapeDtypeStruct(gq.shape, F32),
                   jax.ShapeDtypeStruct(gkv.shape, F32)],
        scratch_shapes=[pltpu.VMEM((tm, 2048), MXU_DTYPE), pltpu.VMEM((tm, 2048), MXU_DTYPE)],
        compiler_params=_cparams("arbitrary"),
    )(proj, proj, cqn, ckvn, dqcat, dkcat, dv, dka, dva, gq, gkv, wq, wkv, tq_c, tq_s, tk_c, tk_s)


def _causal_mask(t):
    return lax.broadcasted_iota(jnp.int32, (t, t), 1) <= lax.broadcasted_iota(jnp.int32, (t, t), 0)


def _mla_fwd(qcat, kcat, v, *, name):
    s = qcat.shape[0]
    t = min(512, s)
    nq = s // t
    hp = HEADS_PER_STEP
    ng = MLA_HEADS // hp
    c2 = MLA_SCALE * LOG2E

    def body(q_ref, k_ref, v_ref, o_ref, lse_ref, top_s, acc_s):
        qi = pl.program_id(1)
        rows = lambda j: pl.ds(pl.multiple_of(j * t, t), t)
        head = lambda e: slice(e * 256, (e + 1) * 256)
        raw = lambda e, j: _dot(q_ref[:, head(e)], k_ref[rows(j), head(e)], 1, 1)

        for e in range(hp):
            top_s[e] = jnp.where(_causal_mask(t), raw(e, qi), NEG)

        def pass1(j, carry):
            for e in range(hp):
                top_s[e] = jnp.maximum(top_s[e], raw(e, j))
            return carry

        lax.fori_loop(0, qi, pass1, 0)
        m = [jnp.max(top_s[e], axis=-1, keepdims=True) * c2 for e in range(hp)]

        def weighted(e, j, masked):
            sc = raw(e, j) * c2 - m[e]
            if masked:
                sc = jnp.where(_causal_mask(t), sc, NEG)
            return jnp.dot(jnp.exp2(sc).astype(MXU_DTYPE), v_ref[rows(j), head(e)], preferred_element_type=F32)

        for e in range(hp):
            acc_s[e] = weighted(e, qi, True)

        def pass2(j, carry):
            for e in range(hp):
                acc_s[e] += weighted(e, j, False)
            return carry

        lax.fori_loop(0, qi, pass2, 0)
        lane = lax.broadcasted_iota(jnp.int32, (t, 128), 1)
        stats = jnp.zeros((t, 128), F32)
        for e in range(hp):
            l = acc_s[e, :, 128:]
            o_ref[:, e * 128:(e + 1) * 128] = acc_s[e, :, :128] / l
            stats = jnp.where(lane == e, m[e] + jnp.log(l) * LOG2E, stats)
        lse_ref[...] = stats

    return pl.pallas_call(
        body, name=name, grid=(ng, nq),
        in_specs=[pl.BlockSpec((t, 256 * hp), lambda g, qi: (qi, g)), pl.BlockSpec((s, 256 * hp), lambda g, qi: (0, g)),
                  pl.BlockSpec((s, 256 * hp), lambda g, qi: (0, g))],
        out_specs=[pl.BlockSpec((t, 128 * hp), lambda g, qi: (qi, g)), pl.BlockSpec((t, 128), lambda g, qi: (qi, g))],
        out_shape=[jax.ShapeDtypeStruct((s, 1024), F32), jax.ShapeDtypeStruct((s, 128 * ng), F32)],
        scratch_shapes=[pltpu.VMEM((hp, t, t), F32), pltpu.VMEM((hp, t, 256), F32)],
        compiler_params=_cparams("parallel", "arbitrary"),
    )(qcat, kcat, v)


def _mla_bwd(qcat, kcat, v, do, lse, delta, *, name):
    s = qcat.shape[0]
    t = min(512, s)
    nq = s // t
    hp = HEADS_PER_STEP
    c2 = MLA_SCALE * LOG2E

    def body(q_ref, k_ref, v_ref, do_ref, lse_ref, dl_ref, dq_ref, dk_ref, dv_ref, dk_acc, dv_acc):
        h, ki = pl.program_id(0), pl.program_id(1)

        @pl.when(ki == 0)
        def _():
            dq_ref[...] = jnp.zeros_like(dq_ref)

        dk_acc[...] = jnp.zeros_like(dk_acc)
        dv_acc[...] = jnp.zeros_like(dv_acc)
        k, vv = k_ref[...], v_ref[...]
        mine = lax.broadcasted_iota(jnp.int32, (t, 128), 1) == h % hp

        def chunk(qi, masked):
            rows = pl.ds(pl.multiple_of(qi * t, t), t)
            q, dob = q_ref[rows, :], do_ref[rows, :]
            pick = lambda r: jnp.sum(jnp.where(mine, r[rows, :], 0.0), axis=-1, keepdims=True)
            sc = _dot(q, k, 1, 1) * c2
            if masked:
                sc = jnp.where(_causal_mask(t), sc, NEG)
            p = jnp.exp2(sc - pick(lse_ref))
            dp = _dot(dob, vv, 1, 1)
            ds = (p * (dp - pick(dl_ref)) * MLA_SCALE).astype(MXU_DTYPE)
            dv_acc[...] += _dot(p.astype(MXU_DTYPE), dob, 0, 0)
            dk_acc[...] += _dot(ds, q, 0, 0)
            dq_ref[rows, :] += jnp.dot(ds, k, preferred_element_type=F32)

        def unmasked(qi, carry):
            chunk(qi, False)
            return carry

        chunk(ki, True)
        lax.fori_loop(ki + 1, nq, unmasked, 0)
        dk_ref[...] = dk_acc[...]
        dv_ref[...] = dv_acc[...]

    head = lambda w: pl.BlockSpec((s, w), lambda h, ki: (0, h))
    blk = lambda w: pl.BlockSpec((t, w), lambda h, ki: (ki, h))
    stat = pl.BlockSpec((s, 128), lambda h, ki: (0, h // hp))
    return pl.pallas_call(
        body, name=name, grid=(MLA_HEADS, nq),
        in_specs=[head(256), blk(256), pl.BlockSpec((t, 128), lambda h, ki: (ki, 2 * h)), head(128), stat, stat],
        out_specs=[head(256), blk(256), blk(128)],
        out_shape=[jax.ShapeDtypeStruct((s, 2048), F32), jax.ShapeDtypeStruct((s, 2048), F32),
                   jax.ShapeDtypeStruct((s, 1024), F32)],
        scratch_shapes=[pltpu.VMEM((t, 256), F32), pltpu.VMEM((t, 128), F32)],
        compiler_params=_cparams("parallel", "arbitrary"),
    )(qcat, kcat, v, do, lse, delta)


def _gate_specs(tm):
    half = lambda c: pl.BlockSpec((tm, 1024), lambda i: (i, c))
    return half(0), half(GA_OFF // 1024), half(GB_OFF // 1024)


def _gate_fwd(oa, ob, proj, *, name):
    s = oa.shape[0]
    tm = min(512, s)

    def body(oa_ref, ob_ref, ga_ref, gb_ref, y_ref):
        ga, gb = ga_ref[...], gb_ref[...]
        y_ref[:, :1024] = (oa_ref[...] * (ga * jax.nn.sigmoid(ga))).astype(MXU_DTYPE)
        y_ref[:, 1024:] = (ob_ref[...] * (gb * jax.nn.sigmoid(gb))).astype(MXU_DTYPE)

    o_spec, ga_spec, gb_spec = _gate_specs(tm)
    return pl.pallas_call(
        body, name=name, grid=(s // tm,), in_specs=[o_spec, o_spec, ga_spec, gb_spec],
        out_specs=pl.BlockSpec((tm, 2048), lambda i: (i, 0)),
        out_shape=jax.ShapeDtypeStruct((s, 2048), MXU_DTYPE),
        compiler_params=_cparams("parallel"),
    )(oa, ob, proj, proj)


def _gate_bwd(dy, oa, ob, proj, *, name):
    s = oa.shape[0]
    tm = min(512, s)

    def body(dy_ref, oa_ref, ob_ref, ga_ref, gb_ref, doa_ref, dob_ref, dga_ref, dgb_ref, dl_ref):
        def branch(dyv, o, g, do_ref, dg_ref):
            sg = jax.nn.sigmoid(g)
            do = dyv * (g * sg)
            do_ref[...] = do.astype(MXU_DTYPE)
            dg_ref[...] = (dyv * o * (sg * (1.0 + g * (1.0 - sg)))).astype(MXU_DTYPE)
            return do

        branch(dy_ref[:, :1024], oa_ref[...], ga_ref[...], doa_ref, dga_ref)
        ob = ob_ref[...]
        prod = branch(dy_ref[:, 1024:], ob, gb_ref[...], dob_ref, dgb_ref) * ob
        lane = lax.broadcasted_iota(jnp.int32, (tm, stat_w), 1)
        acc = jnp.zeros((tm, stat_w), F32)
        for hh in range(MLA_HEADS):
            at = (hh // HEADS_PER_STEP) * 128 + hh % HEADS_PER_STEP
            acc = jnp.where(lane == at, jnp.sum(prod[:, hh * 128:(hh + 1) * 128], axis=-1, keepdims=True), acc)
        dl_ref[...] = acc

    stat_w = 128 * (MLA_HEADS // HEADS_PER_STEP)
    o_spec, ga_spec, gb_spec = _gate_specs(tm)
    return pl.pallas_call(
        body, name=name, grid=(s // tm,),
        in_specs=[pl.BlockSpec((tm, 2048), lambda i: (i, 0)), o_spec, o_spec, ga_spec, gb_spec],
        out_specs=[o_spec, o_spec, o_spec, o_spec, pl.BlockSpec((tm, stat_w), lambda i: (i, 0))],
        out_shape=[jax.ShapeDtypeStruct((s, 1024), MXU_DTYPE)] * 4 + [jax.ShapeDtypeStruct((s, stat_w), F32)],
        compiler_params=_cparams("parallel"),
    )(dy, oa, ob, proj, proj)


def _row_block(rows, cols, itemsize=4, budget=2 << 20):
    fits = [tr for tr in range(16, rows + 1, 16) if rows % tr == 0 and tr * cols * itemsize <= budget]
    return fits[-1] if fits else rows


def _adamw(w, g, m, v, *, name):
    shape = w.shape
    rows, cols = shape[-2:]
    w3, g3, m3, v3 = (a.reshape((-1, rows, cols)) for a in (w, g, m, v))
    lead = w3.shape[0]
    tr = _row_block(rows, cols)

    def body(w_ref, g_ref, m_ref, v_ref, d_ref, mo_ref, vo_ref):
        gv = g_ref[...]
        mn = ADAM_B1 * m_ref[...] + (1.0 - ADAM_B1) * gv
        vn = ADAM_B2 * v_ref[...] + (1.0 - ADAM_B2) * jnp.square(gv)
        m_hat = mn / (1.0 - ADAM_B1 ** ADAM_STEP)
        v_hat = vn / (1.0 - ADAM_B2 ** ADAM_STEP)
        d_ref[...] = -ADAM_LR * (m_hat / (jnp.sqrt(v_hat) + ADAM_EPS) + ADAM_WD * w_ref[...])
        mo_ref[...] = mn
        vo_ref[...] = vn

    spec = pl.BlockSpec((None, tr, cols), lambda a, i: (a, i, 0))
    outs = pl.pallas_call(
        body, name=name, grid=(lead, rows // tr), in_specs=[spec] * 4, out_specs=[spec] * 3,
        out_shape=[jax.ShapeDtypeStruct((lead, rows, cols), F32)] * 3,
        compiler_params=_cparams("parallel", "parallel"),
    )(w3, g3, m3, v3)
    return tuple(o.reshape(shape) for o in outs)


def _pair_sum(where, grads, recv, *, name):
    depth, chips, rows, cols = grads.shape
    hl = depth // 2
    tr = _row_block(rows, cols)

    def body(where_ref, a_ref, b_ref, o_ref):
        o_ref[...] = (a_ref[...] + b_ref[...].astype(F32)).astype(WIRE_DTYPE)

    spec = pl.BlockSpec((None, None, tr, cols), lambda a, k, i, w: (a, k, i, 0))
    return pl.pallas_call(
        body, name=name,
        grid_spec=pltpu.PrefetchScalarGridSpec(
            num_scalar_prefetch=1, grid=(hl, chips, rows // tr),
            in_specs=[pl.BlockSpec((None, None, tr, cols), lambda a, k, i, w: (hl * w[4] + a, k, i, 0)), spec],
            out_specs=spec),
        out_shape=jax.ShapeDtypeStruct((hl, chips, rows, cols), WIRE_DTYPE),
        compiler_params=_cparams("parallel", "parallel", "parallel"),
    )(where, grads, recv)


def _chip_sum(where, grads, recv, parts, *, name):
    depth, _, rows, cols = grads.shape
    hl = depth // 2
    tr = _row_block(rows, cols)

    def body(where_ref, a_ref, b_ref, t0_ref, t1_ref, t2_ref, o_ref):
        total = a_ref[...] + b_ref[...].astype(F32)
        for t_ref in (t0_ref, t1_ref, t2_ref):
            total = total + t_ref[...].astype(F32)
        o_ref[...] = total

    slot = lambda j: pl.BlockSpec((None, None, tr, cols), lambda a, i, w: (a, w[j], i, 0))
    return pl.pallas_call(
        body, name=name,
        grid_spec=pltpu.PrefetchScalarGridSpec(
            num_scalar_prefetch=1, grid=(hl, rows // tr),
            in_specs=[pl.BlockSpec((None, None, tr, cols), lambda a, i, w: (hl * w[4] + a, w[0], i, 0)), slot(0), slot(1),
                      slot(2), slot(3)],
            out_specs=pl.BlockSpec((None, tr, cols), lambda a, i, w: (hl * w[4] + a, i, 0))),
        out_shape=jax.ShapeDtypeStruct((depth, rows, cols), F32),
        compiler_params=_cparams("parallel", "parallel"),
    )(where, grads, recv, parts, parts, parts)


def _place():
    x, y, c = lax.axis_index("x"), lax.axis_index("y"), lax.axis_index("c")
    chips = [(1 - x, y), (x, 1 - y), (1 - x, 1 - y)]
    return x, y, c, chips


def _sems(*shape):
    return [pltpu.SemaphoreType.DMA(shape), pltpu.SemaphoreType.DMA(shape)]


def _comm_gather_weights(shards, *, name):
    nt = len(shards)
    hl = shards[0].shape[0] // 2

    def body(*refs):
        src, out, (send_sems, recv_sems) = refs[:nt], refs[nt:2 * nt], refs[2 * nt:]
        x, y, c, chips = _place()
        k = 2 * x + y
        mine = pl.ds(c * hl, hl)
        other = pl.ds((1 - c) * hl, hl)

        def rows(t, slot):
            n = src[t].shape[1]
            return pl.ds(pl.multiple_of(slot * n, 16), n)

        def copy(sem, t, src_ref, layers, slot, to):
            return pltpu.make_async_remote_copy(src_ref=src_ref, dst_ref=out[t].at[layers, rows(t, slot)],
                                                send_sem=send_sems.at[sem, t], recv_sem=recv_sems.at[sem, t],
                                                device_id=to, device_id_type=MESH)

        first = [copy(j, t, src[t].at[mine], mine, k, (*chip, c)) for j, chip in enumerate(chips) for t in range(nt)]
        for cp in first:
            cp.start()
        passed = []
        for j, (px, py) in enumerate(chips):
            slot = 2 * px + py
            for t in range(nt):
                copy(j, t, src[t].at[mine], mine, slot, (px, py, c)).wait_recv()
                cp = copy(3 + j, t, out[t].at[mine, rows(t, slot)], mine, slot, (x, y, 1 - c))
                cp.start()
                passed.append(cp)
        for j, (px, py) in enumerate(chips):
            for t in range(nt):
                copy(3 + j, t, out[t].at[other, rows(t, 2 * px + py)], other, 2 * px + py, (x, y, 1 - c)).wait_recv()
        for cp in first + passed:
            cp.wait_send()

    return pl.pallas_call(
        body, name=name, in_specs=[ANY] * nt, out_specs=[ANY] * nt,
        out_shape=[jax.ShapeDtypeStruct((a.shape[0], 4 * a.shape[1], a.shape[2]), a.dtype) for a in shards],
        scratch_shapes=_sems(6, nt),
    )(*shards)


def _comm_swap_sibling(bufs, *, name):
    nt = len(bufs)

    def body(*refs):
        src, out, (send_sems, recv_sems) = refs[:nt], refs[nt:2 * nt], refs[2 * nt:]
        x, y, c, _ = _place()
        cps = [pltpu.make_async_remote_copy(src_ref=src[t], dst_ref=out[t], send_sem=send_sems.at[t], recv_sem=recv_sems.at[t],
                                            device_id=(x, y, 1 - c), device_id_type=MESH) for t in range(nt)]
        for cp in cps:
            cp.start()
        for cp in cps:
            cp.wait()

    return pl.pallas_call(
        body, name=name, in_specs=[ANY] * nt, out_specs=[ANY] * nt,
        out_shape=[jax.ShapeDtypeStruct(a.shape, a.dtype) for a in bufs], scratch_shapes=_sems(nt),
    )(*bufs)


def _comm_scatter_chips(parts, *, name):
    nt = len(parts)

    def body(*refs):
        src, out, (send_sems, recv_sems) = refs[:nt], refs[nt:2 * nt], refs[2 * nt:]
        x, y, c, chips = _place()
        k = 2 * x + y
        sends = []
        for j, (px, py) in enumerate(chips):
            for t in range(nt):
                cp = pltpu.make_async_remote_copy(src_ref=src[t].at[:, 2 * px + py], dst_ref=out[t].at[:, k],
                                                  send_sem=send_sems.at[j, t], recv_sem=recv_sems.at[j, t],
                                                  device_id=(px, py, c), device_id_type=MESH)
                cp.start()
                sends.append(cp)
        for j, (px, py) in enumerate(chips):
            for t in range(nt):
                pltpu.make_async_remote_copy(src_ref=src[t].at[:, k], dst_ref=out[t].at[:, 2 * px + py],
                                             send_sem=send_sems.at[j, t], recv_sem=recv_sems.at[j, t],
                                             device_id=(px, py, c), device_id_type=MESH).wait_recv()
        for cp in sends:
            cp.wait_send()

    return pl.pallas_call(
        body, name=name, in_specs=[ANY] * nt, out_specs=[ANY] * nt,
        out_shape=[jax.ShapeDtypeStruct(a.shape, a.dtype) for a in parts], scratch_shapes=_sems(3, nt),
    )(*parts)


def _comm_join_halves(bufs, *, name):
    nt = len(bufs)
    hl = bufs[0].shape[0] // 2

    def body(*refs):
        src, out, (send_sems, recv_sems) = refs[:nt], refs[nt:2 * nt], refs[2 * nt:]
        x, y, c, _ = _place()
        mine = pl.ds(c * hl, hl)
        other = pl.ds((1 - c) * hl, hl)
        copy = lambda t, layers: pltpu.make_async_remote_copy(
            src_ref=src[t].at[mine], dst_ref=out[t].at[layers], send_sem=send_sems.at[t], recv_sem=recv_sems.at[t],
            device_id=(x, y, 1 - c), device_id_type=MESH)
        sends = [copy(t, mine) for t in range(nt)]
        for cp in sends:
            cp.start()
        for t in range(nt):
            copy(t, other).wait_recv()
        for cp in sends:
            cp.wait_send()

    return pl.pallas_call(
        body, name=name, in_specs=[ANY] * nt, out_specs=[ANY] * nt,
        out_shape=[jax.ShapeDtypeStruct(a.shape, a.dtype) for a in bufs],
        input_output_aliases={t: t for t in range(nt)}, scratch_shapes=_sems(nt),
    )(*bufs)


def _comm_allreduce_small(part, *, name):
    rows, cols = part.shape

    def body(p_ref, o_ref, buf, send_sems, recv_sems):
        x, y, c, _ = _place()
        me = 4 * x + 2 * y + c
        buf[me] = p_ref[...]
        flip = lambda v, bit: 1 - v if bit else v
        peers = [(flip(x, d & 4), flip(y, d & 2), flip(c, d & 1)) for d in range(1, 8)]
        sends = []
        for j, peer in enumerate(peers):
            cp = pltpu.make_async_remote_copy(src_ref=buf.at[me], dst_ref=buf.at[me], send_sem=send_sems.at[j],
                                              recv_sem=recv_sems.at[j], device_id=peer, device_id_type=MESH)
            cp.start()
            sends.append(cp)
        for j, (px, py, pc) in enumerate(peers):
            pltpu.make_async_remote_copy(src_ref=buf.at[me], dst_ref=buf.at[4 * px + 2 * py + pc], send_sem=send_sems.at[j],
                                         recv_sem=recv_sems.at[j], device_id=(px, py, pc), device_id_type=MESH).wait_recv()
        for cp in sends:
            cp.wait_send()
        total = buf[0]
        for i in range(1, 8):
            total = total + buf[i]
        o_ref[...] = total

    vm = pl.BlockSpec(memory_space=pltpu.VMEM)
    return pl.pallas_call(
        body, name=name, in_specs=[vm], out_specs=vm, out_shape=jax.ShapeDtypeStruct((rows, cols), F32),
        scratch_shapes=[pltpu.VMEM((8, rows, cols), F32), pltpu.SemaphoreType.DMA((7,)), pltpu.SemaphoreType.DMA((7,))],
    )(part)


def _pad_in_rows(wt):
    r = lambda o, n: wt[..., o:o + n, :]
    kr = r(2944, 64)
    return jnp.concatenate([r(0, 1024), r(1280, 1024), r(3008, 1024), r(2304, Q_RANK), r(1024, 128), r(2688, KV_RANK),
                            r(1152, 128), kr, jnp.zeros_like(kr)], axis=-2)


def _unpad_in_rows(qa, ga, gb, mixed):
    cq, ka, ckv, va, kr = (mixed[..., o - CQ_OFF:o - CQ_OFF + n, :] for o, n in (
        (CQ_OFF, Q_RANK), (KA_OFF, 128), (CKV_OFF, KV_RANK), (VA_OFF, 128), (KR_OFF, 64)))
    return jnp.concatenate([qa, ka, va, ga, cq, ckv, kr, gb], axis=-2)


def _pad_q_rows(wt):
    lead, cols = wt.shape[:-2], wt.shape[-1]
    wt = jnp.pad(wt.reshape(lead + (MLA_HEADS, 192, cols)), [(0, 0)] * (len(lead) + 1) + [(0, 64), (0, 0)])
    return wt.reshape(lead + (MLA_HEADS * 256, cols))


def _unpad_q_rows(wt):
    lead, cols = wt.shape[:-2], wt.shape[-1]
    return wt.reshape(lead + (MLA_HEADS, 256, cols))[..., :192, :].reshape(lead + (MLA_HEADS * 192, cols))


def _perm_kv_rows(wt):
    lead, cols = wt.shape[:-2], wt.shape[-1]
    return jnp.swapaxes(wt.reshape(lead + (MLA_HEADS, 2, 128, cols)), -4, -3).reshape(lead + (2048, cols))


def _unperm_kv_rows(wt):
    lead, cols = wt.shape[:-2], wt.shape[-1]
    return jnp.swapaxes(wt.reshape(lead + (2, MLA_HEADS, 128, cols)), -4, -3).reshape(lead + (2048, cols))


def _t(a):
    return jnp.swapaxes(a, -1, -2)


def _rope_tables(s):
    pos = jnp.arange(s, dtype=F32)
    inv_freq = 10000.0 ** (-jnp.arange(0, 64, 2, dtype=F32) / 64)
    ang = pos[:, None] * inv_freq[None, :]
    cos, sin = jnp.cos(ang), jnp.sin(ang)
    z64 = jnp.zeros((s, 64), F32)
    tk_c = jnp.concatenate([cos, cos, z64], axis=-1)
    tk_s = jnp.concatenate([-sin, sin, z64], axis=-1)
    tq_c = jnp.concatenate([jnp.ones((s, 128), F32), tk_c], axis=-1)
    tq_s = jnp.concatenate([jnp.zeros((s, 128), F32), tk_s], axis=-1)
    return tq_c, tq_s, tk_c, tk_s


def _device_step(xs, tgt, attn_g, sinks, gq, gkv, final_g, w_in_p, w_q_p, w_kv_p, w_o):
    depth = w_in_p.shape[0]
    s = xs.shape[0]
    tabs = _rope_tables(s)
    saved = []
    x = xs
    for l in range(depth):
        h = _rmsnorm_fwd(x, attn_g[l:l + 1], name=f"norm_fwd{l}")
        proj = _matmul(h, w_in_p, tb=True, b_layer=l, name=f"in_proj{l}")
        oa, lse_a = _swa_fwd(proj, sinks[l], name=f"swa_fwd{l}")
        qcat, kcat, v, cqn, ckvn = _mla_qkv_fwd(proj, gq[l:l + 1], gkv[l:l + 1], w_q_p, w_kv_p, *tabs, layer=l,
                                                name=f"mla_qkv_fwd{l}")
        ob, lse_b = _mla_fwd(qcat, kcat, v, name=f"mla_fwd{l}")
        y = _gate_fwd(oa, ob, proj, name=f"gate_fwd{l}")
        x_next = _matmul(y, w_o, add=x, b_layer=l, name=f"out_proj{l}")
        saved.append((x, h, proj, oa, lse_a, qcat, kcat, v, cqn, ckvn, ob, lse_b, y))
        x = x_next

    dx, d_final_g, loss = _final_loss(x, final_g, tgt, name="final_loss")

    d_attn_g, d_sinks, d_gq, d_gkv = [None] * depth, [None] * depth, [None] * depth, [None] * depth
    d_w_in, d_w_q, d_w_kv, d_w_o = [None] * depth, [None] * depth, [None] * depth, [None] * depth
    for l in reversed(range(depth)):
        x, h, proj, oa, lse_a, qcat, kcat, v, cqn, ckvn, ob, lse_b, y = saved[l]
        dy = _matmul(dx, w_o, tb=True, b_layer=l, name=f"out_proj_dx{l}")
        d_w_o[l] = _matmul(y, dx, ta=True, name=f"out_proj_dw{l}")
        doa, dob, dga, dgb, delta_b = _gate_bwd(dy, oa, ob, proj, name=f"gate_bwd{l}")
        dqa, dka, dva, dsink = _swa_bwd(proj, sinks[l], lse_a, doa, name=f"swa_bwd{l}")
        dqc, dkc, dv = _mla_bwd(qcat, kcat, v, dob, lse_b, delta_b, name=f"mla_bwd{l}")
        mixed, d_w_q[l], d_w_kv[l], dgq_l, dgkv_l = _mla_qkv_bwd(
            proj, cqn, ckvn, dqc, dkc, dv, dka, dva, gq[l:l + 1], gkv[l:l + 1], w_q_p, w_kv_p, *tabs, layer=l,
            name=f"mla_qkv_bwd{l}")
        dproj = [dqa, dga, dgb, mixed]
        dh = _matmul_ktiles(dproj, w_in_p, b_layer=l, name=f"in_proj_dx{l}")
        d_w_in[l] = [_matmul(tile, h, ta=True, name=f"in_proj_dw{l}_{j}") for j, tile in enumerate(dproj)]
        dx, dg_l = _rmsnorm_bwd(dh, x, attn_g[l:l + 1], dx, name=f"norm_bwd{l}")
        d_attn_g[l], d_sinks[l], d_gq[l], d_gkv[l] = dg_l, dsink[0:1, :SWA_HEADS], dgq_l, dgkv_l

    cat = lambda parts: jnp.concatenate(parts, axis=0)
    return (loss, dx, cat(d_attn_g), cat(d_sinks), cat(d_gq), cat(d_gkv), d_final_g, d_w_in, d_w_q, d_w_kv, d_w_o)


def kernel(x, attn_norm_g, w_in, swa_sinks, q_a_norm_g, kv_a_norm_g, w_q_b, w_kv_b, w_out, final_norm_g, loss_target, m_attn_norm_g, m_w_in, m_swa_sinks, m_q_a_norm_g, m_kv_a_norm_g, m_w_q_b, m_w_kv_b, m_w_out, m_final_norm_g, v_attn_norm_g, v_w_in, v_swa_sinks, v_q_a_norm_g, v_kv_a_norm_g, v_w_q_b, v_w_kv_b, v_w_out, v_final_norm_g):
    depth = w_in.shape[0]
    hl = depth // 2
    x_, y_, c = lax.axis_index("x"), lax.axis_index("y"), lax.axis_index("c")
    chip = 2 * x_ + y_
    where = jnp.stack([chip, 2 * (1 - x_) + y_, 2 * x_ + 1 - y_, 2 * (1 - x_) + 1 - y_, c]).astype(jnp.int32)

    sent = [a.astype(WIRE_DTYPE) for a in (_t(w_in), _t(w_q_b), _t(w_kv_b), w_out)]
    gathered = _comm_gather_weights(sent, name="comm_gather_weights")

    def with_own_rows(g, own):
        n = own.shape[1]
        return jnp.concatenate([lax.select(chip == j, own, g[:, j * n:(j + 1) * n]) for j in range(4)], axis=1)

    full_in, full_q, full_kv, full_o = (with_own_rows(g, own) for g, own in zip(gathered, sent))

    (loss, dx, d_attn_g, d_sinks, d_gq, d_gkv, d_final_g, d_w_in, d_w_q, d_w_kv, d_w_o) = _device_step(
        x[0], loss_target[0], attn_norm_g, swa_sinks, q_a_norm_g, kv_a_norm_g, final_norm_g.reshape(1, -1),
        _pad_in_rows(full_in), _pad_q_rows(full_q), _perm_kv_rows(full_kv), full_o)

    in_tiles = [jnp.stack([d_w_in[l][j] for l in range(depth)]) for j in range(4)]
    full_grads = [_unpad_in_rows(*in_tiles), _unpad_q_rows(jnp.stack(d_w_q)), _unperm_kv_rows(jnp.stack(d_w_kv)),
                  jnp.stack(d_w_o)]
    grads4 = [g.reshape(depth, 4, g.shape[1] // 4, g.shape[2]) for g in full_grads]
    give = [lax.dynamic_slice_in_dim(g, (1 - c) * hl, hl, axis=0).astype(WIRE_DTYPE) for g in grads4]
    recv = _comm_swap_sibling(give, name="comm_swap_sibling")
    pair = [_pair_sum(where, g, r, name=f"pair_sum{t}") for t, (g, r) in enumerate(zip(grads4, recv))]
    parts = _comm_scatter_chips(pair, name="comm_scatter_chips")
    mine = [_chip_sum(where, g, r, p, name=f"chip_sum{t}") for t, (g, r, p) in enumerate(zip(grads4, recv, parts))]
    reduced = _comm_join_halves(mine, name="comm_join_halves")
    g_w_in, g_w_q_b, g_w_kv_b, g_w_out = _t(reduced[0]), _t(reduced[1]), _t(reduced[2]), reduced[3]

    small = [d_attn_g, d_sinks, d_gq, d_gkv, d_final_g, loss[:, :1]]
    flat = jnp.concatenate([a.reshape(-1) for a in small])
    n_small = flat.shape[0]
    rows = -(-n_small // 1024) * 8
    total = _comm_allreduce_small(jnp.pad(flat, (0, rows * 128 - n_small)).reshape(rows, 128),
                                  name="comm_allreduce_small").reshape(-1)
    outs, at = [], 0
    for a in small:
        outs.append(total[at:at + a.size].reshape(a.shape))
        at += a.size
    g_attn_g, g_sinks, g_gq, g_gkv, g_final_g, loss_total = outs
    g_final_g = g_final_g.reshape(final_norm_g.shape)

    weights = [attn_norm_g, w_in, swa_sinks, q_a_norm_g, kv_a_norm_g, w_q_b, w_kv_b, w_out, final_norm_g]
    grads = [g_attn_g, g_w_in, g_sinks, g_gq, g_gkv, g_w_q_b, g_w_kv_b, g_w_out, g_final_g]
    ms = [m_attn_norm_g, m_w_in, m_swa_sinks, m_q_a_norm_g, m_kv_a_norm_g, m_w_q_b, m_w_kv_b, m_w_out, m_final_norm_g]
    vs = [v_attn_norm_g, v_w_in, v_swa_sinks, v_q_a_norm_g, v_kv_a_norm_g, v_w_q_b, v_w_kv_b, v_w_out, v_final_norm_g]
    as2d = lambda a: a.reshape(1, -1) if a.ndim == 1 else a
    deltas, new_m, new_v = [], [], []
    for i, (w, g, m, v) in enumerate(zip(weights, grads, ms, vs)):
        view = _t if w is w_in else as2d
        d, mn, vn = _adamw(view(w), reduced[0] if w is w_in else view(g), view(m), view(v), name=f"adamw{i}")
        back = _t if w is w_in else (lambda a: a.reshape(w.shape))
        deltas.append(back(d))
        new_m.append(back(mn))
        new_v.append(back(vn))

    return (loss_total.reshape(()), dx[None], *grads, *deltas, *new_m, *new_v)
```

```python
import functools
import math

import jax
import jax.numpy as jnp
from jax import lax
from jax.experimental import pallas as pl
from jax.experimental.pallas import tpu as pltpu

F32 = jnp.float32
MXU_DTYPE = jnp.bfloat16
WIRE_DTYPE = jnp.bfloat16

EPS = 1e-6
NEG = -1e30
BLOCK = 128
D_MODEL = 2048
SWA_HEADS = 16
MLA_HEADS = 8
Q_RANK = 384
KV_RANK = 256
IN_WIDTH = 4032
MLA_SCALE = 192 ** -0.5
SWA_SCALE = 64 ** -0.5
LOG2E = math.log2(math.e)
HEADS_PER_STEP = 2
SLOPES = tuple(2.0 ** (-8.0 * (h + 1) / SWA_HEADS) for h in range(SWA_HEADS))

P_WIDTH = 4096
QA_OFF, GA_OFF, GB_OFF, CQ_OFF, KA_OFF, CKV_OFF, VA_OFF, KR_OFF = 0, 1024, 2048, 3072, 3456, 3584, 3840, 3968

ADAM_LR, ADAM_B1, ADAM_B2, ADAM_EPS, ADAM_WD, ADAM_STEP = 0.001, 0.9, 0.999, 1e-08, 0.01, 10

VMEM_LIMIT = 56 * 1024 * 1024
MESH = pl.DeviceIdType.MESH
ANY = pl.BlockSpec(memory_space=pl.ANY)


def _cparams(*sem):
    return pltpu.CompilerParams(dimension_semantics=sem, vmem_limit_bytes=VMEM_LIMIT)


def _dot(a, b, ca, cb):
    return lax.dot_general(a, b, (((ca,), (cb,)), ((), ())), preferred_element_type=F32)


def _layer_spec(block, index_map, layer):
    if layer is None:
        return pl.BlockSpec(block, index_map)
    return pl.BlockSpec((None,) + tuple(block), lambda *g: (layer,) + tuple(index_map(*g)))


def _matmul(a, b, *, name, ta=False, tb=False, out_dtype=F32, add=None, b_layer=None, tm=1024, tn=1024, tk=2048):
    (kdim, m) = a.shape if ta else a.shape[::-1]
    (n, k2) = b.shape[-2:] if tb else b.shape[-2:][::-1]
    assert kdim == k2, (a.shape, b.shape)
    tm, tn, tk = min(tm, m), min(tn, n), min(tk, kdim)
    assert m % tm == 0 and n % tn == 0 and kdim % tk == 0
    nk = kdim // tk

    def body(*refs):
        a_ref, b_ref = refs[:2]
        add_ref = None if add is None else refs[2]
        o_ref = refs[2 + (add is not None)]
        part = _dot(a_ref[...].astype(MXU_DTYPE), b_ref[...].astype(MXU_DTYPE), 0 if ta else 1, 1 if tb else 0)

        def finish(r):
            o_ref[...] = (r if add is None else add_ref[...] + r).astype(out_dtype)

        if nk == 1:
            finish(part)
            return
        acc = refs[-1]
        k = pl.program_id(2)

        @pl.when(k == 0)
        def _():
            acc[...] = part

        @pl.when((k > 0) & (k < nk - 1))
        def _():
            acc[...] += part

        @pl.when(k == nk - 1)
        def _():
            finish(acc[...] + part)

    a_spec = pl.BlockSpec((tk, tm), lambda i, j, k: (k, i)) if ta else pl.BlockSpec((tm, tk), lambda i, j, k: (i, k))
    b_spec = (_layer_spec((tn, tk), lambda i, j, k: (j, k), b_layer) if tb else
              _layer_spec((tk, tn), lambda i, j, k: (k, j), b_layer))
    in_specs, args = [a_spec, b_spec], [a, b]
    if add is not None:
        in_specs.append(pl.BlockSpec((tm, tn), lambda i, j, k: (i, j)))
        args.append(add)
    return pl.pallas_call(
        body, name=name, grid=(m // tm, n // tn, nk), in_specs=in_specs,
        out_specs=pl.BlockSpec((tm, tn), lambda i, j, k: (i, j)),
        out_shape=jax.ShapeDtypeStruct((m, n), out_dtype),
        scratch_shapes=[pltpu.VMEM((tm, tn), F32)] if nk > 1 else [],
        compiler_params=_cparams("parallel", "parallel", "arbitrary"),
    )(*args)


def _matmul_ktiles(a_tiles, b, *, name, b_layer=None, tm=512, tn=1024):
    m, kt = a_tiles[0].shape
    n = b.shape[-1]
    nt = len(a_tiles)
    assert b.shape[-2] == nt * kt
    tm, tn = min(tm, m), min(tn, n)
    assert m % tm == 0 and n % tn == 0

    def body(*refs):
        a_refs, b_refs, o_ref = refs[:nt], refs[nt:2 * nt], refs[2 * nt]
        acc = _dot(a_refs[0][...].astype(MXU_DTYPE), b_refs[0][...].astype(MXU_DTYPE), 1, 0)
        for j in range(1, nt):
            acc += _dot(a_refs[j][...].astype(MXU_DTYPE), b_refs[j][...].astype(MXU_DTYPE), 1, 0)
        o_ref[...] = acc

    in_specs = [pl.BlockSpec((tm, kt), lambda i, jn: (i, 0))] * nt
    in_specs += [_layer_spec((kt, tn), lambda i, jn, j=j: (j, jn), b_layer) for j in range(nt)]
    return pl.pallas_call(
        body, name=name, grid=(m // tm, n // tn), in_specs=in_specs,
        out_specs=pl.BlockSpec((tm, tn), lambda i, jn: (i, jn)),
        out_shape=jax.ShapeDtypeStruct((m, n), F32),
        compiler_params=_cparams("parallel", "parallel"),
    )(*a_tiles, *([b] * nt))


def _rmsnorm_fwd(x, g, *, name):
    s, d = x.shape
    tm = min(512, s)

    def body(x_ref, g_ref, h_ref):
        xv = x_ref[...]
        r = lax.rsqrt(jnp.mean(xv * xv, axis=-1, keepdims=True) + EPS)
        h_ref[...] = (xv * r * g_ref[...]).astype(MXU_DTYPE)

    return pl.pallas_call(
        body, name=name, grid=(s // tm,),
        in_specs=[pl.BlockSpec((tm, d), lambda i: (i, 0)), pl.BlockSpec((1, d), lambda i: (0, 0))],
        out_specs=pl.BlockSpec((tm, d), lambda i: (i, 0)),
        out_shape=jax.ShapeDtypeStruct((s, d), MXU_DTYPE),
        compiler_params=_cparams("parallel"),
    )(x, g)


def _rmsnorm_bwd(dh, x, g, dres, *, name):
    s, d = x.shape
    tm = min(512, s)

    def body(dh_ref, x_ref, g_ref, dres_ref, dx_ref, dg_ref):
        @pl.when(pl.program_id(0) == 0)
        def _():
            dg_ref[...] = jnp.zeros_like(dg_ref)

        xv = x_ref[...]
        r = lax.rsqrt(jnp.mean(xv * xv, axis=-1, keepdims=True) + EPS)
        xn = xv * r
        dy = dh_ref[...]
        dg_ref[...] += jnp.sum(dy * xn, axis=0, keepdims=True)
        u = dy * g_ref[...]
        dx_ref[...] = dres_ref[...] + r * (u - xn * jnp.mean(u * xn, axis=-1, keepdims=True))

    row = pl.BlockSpec((tm, d), lambda i: (i, 0))
    vec = pl.BlockSpec((1, d), lambda i: (0, 0))
    return pl.pallas_call(
        body, name=name, grid=(s // tm,), in_specs=[row, row, vec, row], out_specs=[row, vec],
        out_shape=[jax.ShapeDtypeStruct((s, d), F32), jax.ShapeDtypeStruct((1, d), F32)],
        compiler_params=_cparams("arbitrary"),
    )(dh, x, g, dres)


def _final_loss(x, g, tgt, *, name):
    s, d = x.shape
    tm = min(512, s)

    def body(x_ref, g_ref, t_ref, dx_ref, dg_ref, loss_ref):
        @pl.when(pl.program_id(0) == 0)
        def _():
            dg_ref[...] = jnp.zeros_like(dg_ref)
            loss_ref[...] = jnp.zeros_like(loss_ref)

        xv = x_ref[...]
        gv = g_ref[...]
        r = lax.rsqrt(jnp.mean(xv * xv, axis=-1, keepdims=True) + EPS)
        xn = xv * r
        err = xn * gv - t_ref[...]
        sq = jnp.sum(jnp.sum(err * err, axis=-1, keepdims=True), axis=0, keepdims=True)
        loss_ref[...] += (0.5 / d) * sq
        dy = err * (1.0 / d)
        dg_ref[...] += jnp.sum(dy * xn, axis=0, keepdims=True)
        u = dy * gv
        dx_ref[...] = r * (u - xn * jnp.mean(u * xn, axis=-1, keepdims=True))

    row = pl.BlockSpec((tm, d), lambda i: (i, 0))
    vec = pl.BlockSpec((1, d), lambda i: (0, 0))
    return pl.pallas_call(
        body, name=name, grid=(s // tm,), in_specs=[row, vec, row],
        out_specs=[row, vec, pl.BlockSpec((1, 128), lambda i: (0, 0))],
        out_shape=[jax.ShapeDtypeStruct((s, d), F32), jax.ShapeDtypeStruct((1, d), F32),
                   jax.ShapeDtypeStruct((1, 128), F32)],
        compiler_params=_cparams("arbitrary"),
    )(x, g, tgt)


def _swa_keys(kp_ref, kc_ref):
    kk = jnp.concatenate([kp_ref[...], kc_ref[...]], axis=0)
    kr = pltpu.roll(kk, 64, 1)
    lo = lax.broadcasted_iota(jnp.int32, kk.shape, 1) < 64
    return [jnp.where(lo, kk, kr).astype(MXU_DTYPE), jnp.where(lo, kr, kk).astype(MXU_DTYPE)]


GROUP = SWA_HEADS // 2


def _swa_mask(n):
    qi = lax.broadcasted_iota(jnp.int32, (BLOCK, 2 * BLOCK), 0)
    ki = lax.broadcasted_iota(jnp.int32, (BLOCK, 2 * BLOCK), 1)
    delta = BLOCK + qi - ki
    valid = (delta >= 0) & (delta < BLOCK) & ((ki >= BLOCK) | (n > 0))
    return valid, delta.astype(F32)


def _stack_heads(ref, j):
    lo = lax.broadcasted_iota(jnp.int32, (BLOCK, BLOCK), 1) < 64
    parts = []
    for r in range(GROUP):
        pair = (GROUP * j + r) // 2
        blk = ref[:, pair * 128:(pair + 1) * 128].astype(F32)
        parts.append(jnp.where(lo if r % 2 == 0 else ~lo, blk, 0.0).astype(MXU_DTYPE))
    return jnp.concatenate(parts, axis=0)


def _unstack_heads(stacked, ref, j):
    lo = lax.broadcasted_iota(jnp.int32, (BLOCK, BLOCK), 1) < 64
    for i in range(GROUP // 2):
        pair = (GROUP * j) // 2 + i
        even, odd = stacked[2 * i * BLOCK:(2 * i + 1) * BLOCK], stacked[(2 * i + 1) * BLOCK:(2 * i + 2) * BLOCK]
        ref[:, pair * 128:(pair + 1) * 128] = jnp.where(lo, even, odd).astype(ref.dtype)


def _head_rows(stacked, r):
    return stacked[r * BLOCK:(r + 1) * BLOCK]


def _swa_scores(raw, h, valid, deltaf):
    return jnp.where(valid, raw * (SWA_SCALE * LOG2E) - (SLOPES[h] * LOG2E) * deltaf, NEG)


def _swa_specs(nb):
    kcol, vcol = KA_OFF // BLOCK, VA_OFF // BLOCK
    last = nb - 1
    cur = lambda n: jnp.minimum(n, last)
    prev = lambda n: jnp.maximum(jnp.minimum(n, last) - 1, 0)
    return [
        pl.BlockSpec(memory_space=pltpu.SMEM),
        pl.BlockSpec((BLOCK, 1024), lambda n: (cur(n), QA_OFF // 1024)),
        pl.BlockSpec((BLOCK, BLOCK), lambda n: (cur(n), kcol)),
        pl.BlockSpec((BLOCK, BLOCK), lambda n: (prev(n), kcol)),
        pl.BlockSpec((BLOCK, BLOCK), lambda n: (cur(n), vcol)),
        pl.BlockSpec((BLOCK, BLOCK), lambda n: (prev(n), vcol)),
    ]


def _swa_fwd(proj, sinks, *, name):
    s = proj.shape[0]
    nb = s // BLOCK

    def body(sink_ref, q_ref, kc_ref, kp_ref, vc_ref, vp_ref, o_ref, lse_ref):
        n = pl.program_id(0)
        keys = _swa_keys(kp_ref, kc_ref)
        vals = _swa_keys(vp_ref, vc_ref)
        valid, deltaf = _swa_mask(n)
        lane = lax.broadcasted_iota(jnp.int32, (BLOCK, BLOCK), 1)
        lse_acc = jnp.zeros((BLOCK, BLOCK), F32)
        for j in range(2):
            raw = _dot(_stack_heads(q_ref, j), keys[j], 1, 1)
            probs = []
            for r in range(GROUP):
                h = GROUP * j + r
                sc = _swa_scores(_head_rows(raw, r), h, valid, deltaf)
                sink = sink_ref[h] * LOG2E
                m = jnp.maximum(jnp.max(sc, axis=-1, keepdims=True), sink)
                p = jnp.exp2(sc - m)
                l = jnp.sum(p, axis=-1, keepdims=True) + jnp.exp2(sink - m)
                probs.append((p * (1.0 / l)).astype(MXU_DTYPE))
                lse_acc = jnp.where(lane == h, m + jnp.log(l) * LOG2E, lse_acc)
            _unstack_heads(jnp.dot(jnp.concatenate(probs, axis=0), vals[j], preferred_element_type=F32), o_ref, j)
        lse_ref[...] = lse_acc

    return pl.pallas_call(
        body, name=name, grid=(nb,), in_specs=_swa_specs(nb),
        out_specs=[pl.BlockSpec((BLOCK, 1024), lambda n: (n, 0)), pl.BlockSpec((BLOCK, BLOCK), lambda n: (n, 0))],
        out_shape=[jax.ShapeDtypeStruct((s, 1024), F32), jax.ShapeDtypeStruct((s, BLOCK), F32)],
        compiler_params=_cparams("parallel"),
    )(sinks, proj, proj, proj, proj, proj)


def _swa_bwd(proj, sinks, lse, do, *, name):
    s = proj.shape[0]
    nb = s // BLOCK
    last = nb - 1

    def body(sink_ref, q_ref, kc_ref, kp_ref, vc_ref, vp_ref, lse_ref, do_ref,
             dq_ref, dk_ref, dv_ref, dsink_ref, carry_k, carry_v):
        n = pl.program_id(0)

        @pl.when(n == 0)
        def _():
            carry_k[...] = jnp.zeros_like(carry_k)
            carry_v[...] = jnp.zeros_like(carry_v)
            dsink_ref[...] = jnp.zeros_like(dsink_ref)

        @pl.when(n < nb)
        def _():
            keys = _swa_keys(kp_ref, kc_ref)
            vals = _swa_keys(vp_ref, vc_ref)
            valid, deltaf = _swa_mask(n)
            lane = lax.broadcasted_iota(jnp.int32, (BLOCK, BLOCK), 1)
            lane1 = lax.broadcasted_iota(jnp.int32, (1, BLOCK), 1)
            lse_blk = lse_ref[...]
            acc_k, acc_v = [], []
            dsink = jnp.zeros((1, BLOCK), F32)
            for j in range(2):
                q_all, do_all = _stack_heads(q_ref, j), _stack_heads(do_ref, j)
                raw = _dot(q_all, keys[j], 1, 1)
                dp_all = _dot(do_all, vals[j], 1, 1)
                probs, dscores = [], []
                for r in range(GROUP):
                    h = GROUP * j + r
                    lse_h = jnp.sum(jnp.where(lane == h, lse_blk, 0.0), axis=-1, keepdims=True)
                    p = jnp.exp2(_swa_scores(_head_rows(raw, r), h, valid, deltaf) - lse_h)
                    dp = _head_rows(dp_all, r)
                    dlt = jnp.sum(dp * p, axis=-1, keepdims=True)
                    dscores.append((p * (dp - dlt) * SWA_SCALE).astype(MXU_DTYPE))
                    probs.append(p.astype(MXU_DTYPE))
                    sunk = jnp.exp2(sink_ref[h] * LOG2E - lse_h) * dlt
                    dsink = jnp.where(lane1 == h, -jnp.sum(sunk, axis=0, keepdims=True), dsink)
                ds_all = jnp.concatenate(dscores, axis=0)
                _unstack_heads(jnp.dot(ds_all, keys[j], preferred_element_type=F32), dq_ref, j)
                acc_k.append(_dot(ds_all, q_all, 0, 0))
                acc_v.append(_dot(jnp.concatenate(probs, axis=0), do_all, 0, 0))
            lo2 = lax.broadcasted_iota(jnp.int32, (2 * BLOCK, BLOCK), 1) < 64
            fold = lambda acc: jnp.where(lo2, acc[0] + pltpu.roll(acc[0], 64, 1), acc[1] + pltpu.roll(acc[1], 64, 1))
            dkk, dvv = fold(acc_k), fold(acc_v)
            dk_ref[...] = (carry_k[...] + dkk[:BLOCK]).astype(dk_ref.dtype)
            dv_ref[...] = (carry_v[...] + dvv[:BLOCK]).astype(dv_ref.dtype)
            carry_k[...] = dkk[BLOCK:]
            carry_v[...] = dvv[BLOCK:]
            dsink_ref[...] += jnp.broadcast_to(dsink, dsink_ref.shape)

        @pl.when(n == nb)
        def _():
            dk_ref[...] = carry_k[...].astype(dk_ref.dtype)
            dv_ref[...] = carry_v[...].astype(dv_ref.dtype)

    cur = lambda n: jnp.minimum(n, last)
    lag = lambda n: jnp.maximum(n - 1, 0)
    return pl.pallas_call(
        body, name=name, grid=(nb + 1,),
        in_specs=_swa_specs(nb) + [pl.BlockSpec((BLOCK, BLOCK), lambda n: (cur(n), 0)),
                                   pl.BlockSpec((BLOCK, 1024), lambda n: (cur(n), 0))],
        out_specs=[pl.BlockSpec((BLOCK, 1024), lambda n: (cur(n), 0)),
                   pl.BlockSpec((BLOCK, BLOCK), lambda n: (lag(n), 0)),
                   pl.BlockSpec((BLOCK, BLOCK), lambda n: (lag(n), 0)),
                   pl.BlockSpec((8, BLOCK), lambda n: (0, 0))],
        out_shape=[jax.ShapeDtypeStruct((s, 1024), MXU_DTYPE), jax.ShapeDtypeStruct((s, BLOCK), MXU_DTYPE),
                   jax.ShapeDtypeStruct((s, BLOCK), MXU_DTYPE), jax.ShapeDtypeStruct((8, BLOCK), F32)],
        scratch_shapes=[pltpu.VMEM((BLOCK, BLOCK), F32), pltpu.VMEM((BLOCK, BLOCK), F32)],
        compiler_params=_cparams("arbitrary"),
    )(sinks, proj, proj, proj, proj, proj, lse, do)


def _rope_partner(v, first, width):
    lane = lax.broadcasted_iota(jnp.int32, v.shape, 1)
    in_a = (lane >= first) & (lane < first + 32)
    in_b = (lane >= first + 32) & (lane < first + 64)
    return jnp.where(in_a, pltpu.roll(v, width - 32, 1), jnp.where(in_b, pltpu.roll(v, 32, 1), 0.0))


def _mla_qkv_fwd(proj, gq, gkv, wq, wkv, tq_c, tq_s, tk_c, tk_s, *, layer, name):
    s = proj.shape[0]
    tm = min(256, s)

    def body(cq_ref, ckv_ref, kr_ref, gq_ref, gkv_ref, wq_ref, wkv_ref, qc_ref, qs_ref, kc_ref, ks_ref,
             qcat_ref, kcat_ref, v_ref, cqn_ref, ckvn_ref):
        cq = cq_ref[...]
        cqn = (cq * lax.rsqrt(jnp.mean(cq * cq, axis=-1, keepdims=True) + EPS) * gq_ref[...]).astype(MXU_DTYPE)
        cqn_ref[...] = cqn
        qpre = _dot(cqn, wq_ref[...], 1, 1)
        qc, qs = qc_ref[...], qs_ref[...]
        for hh in range(MLA_HEADS):
            blk = qpre[:, hh * 256:(hh + 1) * 256]
            qcat_ref[:, hh * 256:(hh + 1) * 256] = (blk * qc + _rope_partner(blk, 128, 256) * qs).astype(MXU_DTYPE)
        ckv = ckv_ref[...]
        ckvn = (ckv * lax.rsqrt(jnp.mean(ckv * ckv, axis=-1, keepdims=True) + EPS) * gkv_ref[...]).astype(MXU_DTYPE)
        ckvn_ref[...] = ckvn
        kv = _dot(ckvn, wkv_ref[...], 1, 1)
        kr = kr_ref[...]
        krr = (kr * kc_ref[...] + _rope_partner(kr, 0, 128) * ks_ref[...]).astype(MXU_DTYPE)
        for hh in range(MLA_HEADS):
            kcat_ref[:, hh * 256:hh * 256 + 128] = kv[:, hh * 128:(hh + 1) * 128].astype(MXU_DTYPE)
            kcat_ref[:, hh * 256 + 128:(hh + 1) * 256] = krr
            v_ref[:, hh * 256:hh * 256 + 128] = kv[:, 1024 + hh * 128:1024 + (hh + 1) * 128].astype(MXU_DTYPE)
            v_ref[:, hh * 256 + 128:(hh + 1) * 256] = jnp.ones((tm, 128), MXU_DTYPE)

    row = lambda w, c: pl.BlockSpec((tm, w), lambda i: (i, c))
    full = lambda a: pl.BlockSpec(a.shape, lambda i: (0, 0))
    of_layer = lambda a: _layer_spec(a.shape[-2:], lambda i: (0, 0), layer)
    return pl.pallas_call(
        body, name=name, grid=(s // tm,),
        in_specs=[row(Q_RANK, CQ_OFF // Q_RANK), row(KV_RANK, CKV_OFF // KV_RANK), row(128, KR_OFF // 128),
                  full(gq), full(gkv), of_layer(wq), of_layer(wkv), row(256, 0), row(256, 0), row(128, 0), row(128, 0)],
        out_specs=[row(2048, 0), row(2048, 0), row(2048, 0), row(Q_RANK, 0), row(KV_RANK, 0)],
        out_shape=[jax.ShapeDtypeStruct((s, 2048), MXU_DTYPE), jax.ShapeDtypeStruct((s, 2048), MXU_DTYPE),
                   jax.ShapeDtypeStruct((s, 2048), MXU_DTYPE), jax.ShapeDtypeStruct((s, Q_RANK), MXU_DTYPE),
                   jax.ShapeDtypeStruct((s, KV_RANK), MXU_DTYPE)],
        compiler_params=_cparams("parallel"),
    )(proj, proj, proj, gq, gkv, wq, wkv, tq_c, tq_s, tk_c, tk_s)


def _norm_bwd(x, g, dy):
    r = lax.rsqrt(jnp.mean(x * x, axis=-1, keepdims=True) + EPS)
    xn = x * r
    u = dy * g
    return r * (u - xn * jnp.mean(u * xn, axis=-1, keepdims=True)), jnp.sum(dy * xn, axis=0, keepdims=True)


def _mla_qkv_bwd(proj, cqn, ckvn, dqcat, dkcat, dv, dka, dva, gq, gkv, wq, wkv, tq_c, tq_s, tk_c, tk_s, *, layer,
                 name):
    s = proj.shape[0]
    tm = min(256, s)
    t_cq, t_ka, t_ckv, t_va, t_kr = (o - CQ_OFF for o in (CQ_OFF, KA_OFF, CKV_OFF, VA_OFF, KR_OFF))

    def body(cq_ref, ckv_ref, cqn_ref, ckvn_ref, dq_ref, dk_ref, dv_ref, dka_ref, dva_ref, gq_ref, gkv_ref, wq_ref,
             wkv_ref, qc_ref, qs_ref, kc_ref, ks_ref,
             tile_ref, dwq_ref, dwkv_ref, dgq_ref, dgkv_ref, dqpre, dkv):
        dcq_ref = tile_ref.at[:, t_cq:t_cq + Q_RANK]
        dckv_ref = tile_ref.at[:, t_ckv:t_ckv + KV_RANK]
        dkr_ref = tile_ref.at[:, t_kr:t_kr + 128]
        tile_ref[:, t_ka:t_ka + 128] = dka_ref[...]
        tile_ref[:, t_va:t_va + 128] = dva_ref[...]

        @pl.when(pl.program_id(0) == 0)
        def _():
            for r in (dwq_ref, dwkv_ref, dgq_ref, dgkv_ref):
                r[...] = jnp.zeros_like(r)

        qc, qs = qc_ref[...], qs_ref[...]
        dkrr = jnp.zeros((tm, 128), F32)
        for hh in range(MLA_HEADS):
            blk = dq_ref[:, hh * 256:(hh + 1) * 256]
            dqpre[:, hh * 256:(hh + 1) * 256] = (blk * qc + _rope_partner(blk * qs, 128, 256)).astype(MXU_DTYPE)
            dkv[:, hh * 128:(hh + 1) * 128] = dk_ref[:, hh * 256:hh * 256 + 128].astype(MXU_DTYPE)
            dkrr = dkrr + dk_ref[:, hh * 256 + 128:(hh + 1) * 256]
        dkv[:, 1024:] = dv_ref[...].astype(MXU_DTYPE)
        dkr_ref[...] = (dkrr * kc_ref[...] + _rope_partner(dkrr * ks_ref[...], 0, 128)).astype(dkr_ref.dtype)

        dq_b = dqpre[...]
        dwq_ref[...] += _dot(dq_b, cqn_ref[...], 0, 0)
        dcq, dgq = _norm_bwd(cq_ref[...], gq_ref[...], _dot(dq_b, wq_ref[...], 1, 0))
        dcq_ref[...] = dcq.astype(dcq_ref.dtype)
        dgq_ref[...] += dgq

        dkv_b = dkv[...]
        dwkv_ref[...] += _dot(dkv_b, ckvn_ref[...], 0, 0)
        dckv, dgkv = _norm_bwd(ckv_ref[...], gkv_ref[...], _dot(dkv_b, wkv_ref[...], 1, 0))
        dckv_ref[...] = dckv.astype(dckv_ref.dtype)
        dgkv_ref[...] += dgkv

    row = lambda w, c: pl.BlockSpec((tm, w), lambda i: (i, c))
    full = lambda shape: pl.BlockSpec(shape, lambda i: (0, 0))
    of_layer = lambda a: _layer_spec(a.shape[-2:], lambda i: (0, 0), layer)
    return pl.pallas_call(
        body, name=name, grid=(s // tm,),
        in_specs=[row(Q_RANK, CQ_OFF // Q_RANK), row(KV_RANK, CKV_OFF // KV_RANK), row(Q_RANK, 0), row(KV_RANK, 0),
                  row(2048, 0), row(2048, 0), row(1024, 0), row(128, 0), row(128, 0), full(gq.shape), full(gkv.shape),
                  of_layer(wq), of_layer(wkv), row(256, 0), row(256, 0), row(128, 0), row(128, 0)],
        out_specs=[row(1024, 0), full(wq.shape[-2:]), full(wkv.shape[-2:]), full(gq.shape), full(gkv.shape)],
        out_shape=[jax.ShapeDtypeStruct((s, 1024), MXU_DTYPE), jax.ShapeDtypeStruct(wq.shape[-2:], F32),
                   jax.ShapeDtypeStruct(wkv.shape[-2:], F32), jax.ShapeDtypeStruct(gq.shape, F32),
                   jax.ShapeDtypeStruct(gkv.shape, F32)],
        scratch_shapes=[pltpu.VMEM((tm, 2048), MXU_DTYPE), pltpu.VMEM((tm, 2048), MXU_DTYPE)],
        compiler_params=_cparams("arbitrary"),
    )(proj, proj, cqn, ckvn, dqcat, dkcat, dv, dka, dva, gq, gkv, wq, wkv, tq_c, tq_s, tk_c, tk_s)


def _causal_mask(t):
    return lax.broadcasted_iota(jnp.int32, (t, t), 1) <= lax.broadcasted_iota(jnp.int32, (t, t), 0)


def _mla_fwd(qcat, kcat, v, *, name, gather=None):
    s = qcat.shape[0]
    t = min(512, s)
    nq = s // t
    hp = HEADS_PER_STEP
    ng = MLA_HEADS // hp
    c2 = MLA_SCALE * LOG2E

    nw = 0 if gather is None else len(gather)

    def body(q_ref, k_ref, v_ref, *rest):
        src, (o_ref, lse_ref), out = rest[:nw], rest[nw:nw + 2], rest[nw + 2:2 * nw + 2]
        top_s, acc_s, *sems = rest[2 * nw + 2:]
        g, qi = pl.program_id(0), pl.program_id(1)
        if nw:
            @pl.when((g == 0) & (qi == 0))
            def _():
                _gather_start(src, out, *sems)

        rows = lambda j: pl.ds(pl.multiple_of(j * t, t), t)
        head = lambda e: slice(e * 256, (e + 1) * 256)
        raw = lambda e, j: _dot(q_ref[:, head(e)], k_ref[rows(j), head(e)], 1, 1)

        for e in range(hp):
            top_s[e] = jnp.where(_causal_mask(t), raw(e, qi), NEG)

        def pass1(j, carry):
            for e in range(hp):
                top_s[e] = jnp.maximum(top_s[e], raw(e, j))
            return carry

        lax.fori_loop(0, qi, pass1, 0)
        m = [jnp.max(top_s[e], axis=-1, keepdims=True) * c2 for e in range(hp)]

        def weighted(e, j, masked):
            sc = raw(e, j) * c2 - m[e]
            if masked:
                sc = jnp.where(_causal_mask(t), sc, NEG)
            return jnp.dot(jnp.exp2(sc).astype(MXU_DTYPE), v_ref[rows(j), head(e)], preferred_element_type=F32)

        for e in range(hp):
            acc_s[e] = weighted(e, qi, True)

        def pass2(j, carry):
            for e in range(hp):
                acc_s[e] += weighted(e, j, False)
            return carry

        lax.fori_loop(0, qi, pass2, 0)
        lane = lax.broadcasted_iota(jnp.int32, (t, 128), 1)
        stats = jnp.zeros((t, 128), F32)
        for e in range(hp):
            l = acc_s[e, :, 128:]
            o_ref[:, e * 128:(e + 1) * 128] = acc_s[e, :, :128] / l
            stats = jnp.where(lane == e, m[e] + jnp.log(l) * LOG2E, stats)
        lse_ref[...] = stats
        if nw:
            @pl.when((g == ng - 1) & (qi == nq - 1))
            def _():
                _gather_finish(src, out, *sems)

    outs = pl.pallas_call(
        body, name=name, grid=(ng, nq),
        in_specs=[pl.BlockSpec((t, 256 * hp), lambda g, qi: (qi, g)), pl.BlockSpec((s, 256 * hp), lambda g, qi: (0, g)),
                  pl.BlockSpec((s, 256 * hp), lambda g, qi: (0, g))] + [ANY] * nw,
        out_specs=[pl.BlockSpec((t, 128 * hp), lambda g, qi: (qi, g)), pl.BlockSpec((t, 128), lambda g, qi: (qi, g))]
        + [ANY] * nw,
        out_shape=[jax.ShapeDtypeStruct((s, 1024), F32), jax.ShapeDtypeStruct((s, 128 * ng), F32)]
        + (_gathered_shapes(gather) if nw else []),
        scratch_shapes=[pltpu.VMEM((hp, t, t), F32), pltpu.VMEM((hp, t, 256), F32)] + (_sems(6, nw) if nw else []),
        compiler_params=_cparams("arbitrary", "arbitrary"),
    )(qcat, kcat, v, *(gather or []))
    return outs[0], outs[1], list(outs[2:])


def _mla_bwd(qcat, kcat, v, do, lse, delta, *, name, scatter=None):
    s = qcat.shape[0]
    t = min(512, s)
    nq = s // t
    hp = HEADS_PER_STEP
    c2 = MLA_SCALE * LOG2E
    nw = 0 if scatter is None else len(scatter)

    def body(q_ref, k_ref, v_ref, do_ref, lse_ref, dl_ref, *rest):
        src, (dq_ref, dk_ref, dv_ref), out = rest[:nw], rest[nw:nw + 3], rest[nw + 3:2 * nw + 3]
        dk_acc, dv_acc, *sems = rest[2 * nw + 3:]
        h, ki = pl.program_id(0), pl.program_id(1)
        if nw:
            @pl.when((h == 0) & (ki == 0))
            def _():
                _scatter_start(src, out, *sems)

        @pl.when(ki == 0)
        def _():
            dq_ref[...] = jnp.zeros_like(dq_ref)

        dk_acc[...] = jnp.zeros_like(dk_acc)
        dv_acc[...] = jnp.zeros_like(dv_acc)
        k, vv = k_ref[...], v_ref[...]
        mine = lax.broadcasted_iota(jnp.int32, (t, 128), 1) == h % hp

        def chunk(qi, masked):
            rows = pl.ds(pl.multiple_of(qi * t, t), t)
            q, dob = q_ref[rows, :], do_ref[rows, :]
            pick = lambda r: jnp.sum(jnp.where(mine, r[rows, :], 0.0), axis=-1, keepdims=True)
            sc = _dot(q, k, 1, 1) * c2
            if masked:
                sc = jnp.where(_causal_mask(t), sc, NEG)
            p = jnp.exp2(sc - pick(lse_ref))
            dp = _dot(dob, vv, 1, 1)
            ds = (p * (dp - pick(dl_ref)) * MLA_SCALE).astype(MXU_DTYPE)
            dv_acc[...] += _dot(dob, p.astype(MXU_DTYPE), 0, 0)
            dk_acc[...] += _dot(q, ds, 0, 0)
            dq_ref[rows, :] += jnp.dot(ds, k, preferred_element_type=F32)

        def unmasked(qi, carry):
            chunk(qi, False)
            return carry

        chunk(ki, True)
        lax.fori_loop(ki + 1, nq, unmasked, 0)
        dk_ref[...] = dk_acc[...].T
        dv_ref[...] = dv_acc[...].T
        if nw:
            @pl.when((h == MLA_HEADS - 1) & (ki == nq - 1))
            def _():
                _scatter_finish(src, out, *sems)

    head = lambda w: pl.BlockSpec((s, w), lambda h, ki: (0, h))
    blk = lambda w: pl.BlockSpec((t, w), lambda h, ki: (ki, h))
    stat = pl.BlockSpec((s, 128), lambda h, ki: (0, h // hp))
    outs = pl.pallas_call(
        body, name=name, grid=(MLA_HEADS, nq),
        in_specs=[head(256), blk(256), pl.BlockSpec((t, 128), lambda h, ki: (ki, 2 * h)), head(128), stat, stat]
        + [ANY] * nw,
        out_specs=[head(256), blk(256), blk(128)] + [ANY] * nw,
        out_shape=[jax.ShapeDtypeStruct((s, 2048), F32), jax.ShapeDtypeStruct((s, 2048), F32),
                   jax.ShapeDtypeStruct((s, 1024), F32)] + [jax.ShapeDtypeStruct(a.shape, a.dtype) for a in scatter or []],
        scratch_shapes=[pltpu.VMEM((256, t), F32), pltpu.VMEM((128, t), F32)] + (_sems(3, nw) if nw else []),
        compiler_params=_cparams("arbitrary", "arbitrary"),
    )(qcat, kcat, v, do, lse, delta, *(scatter or []))
    return outs[0], outs[1], outs[2], list(outs[3:])


def _gate_specs(tm):
    half = lambda c: pl.BlockSpec((tm, 1024), lambda i: (i, c))
    return half(0), half(GA_OFF // 1024), half(GB_OFF // 1024)


def _gate_fwd(oa, ob, proj, *, name):
    s = oa.shape[0]
    tm = min(512, s)

    def body(oa_ref, ob_ref, ga_ref, gb_ref, y_ref):
        ga, gb = ga_ref[...], gb_ref[...]
        y_ref[:, :1024] = (oa_ref[...] * (ga * jax.nn.sigmoid(ga))).astype(MXU_DTYPE)
        y_ref[:, 1024:] = (ob_ref[...] * (gb * jax.nn.sigmoid(gb))).astype(MXU_DTYPE)

    o_spec, ga_spec, gb_spec = _gate_specs(tm)
    return pl.pallas_call(
        body, name=name, grid=(s // tm,), in_specs=[o_spec, o_spec, ga_spec, gb_spec],
        out_specs=pl.BlockSpec((tm, 2048), lambda i: (i, 0)),
        out_shape=jax.ShapeDtypeStruct((s, 2048), MXU_DTYPE),
        compiler_params=_cparams("parallel"),
    )(oa, ob, proj, proj)


def _gate_bwd(dy, oa, ob, proj, *, name):
    s = oa.shape[0]
    tm = min(512, s)

    def body(dy_ref, oa_ref, ob_ref, ga_ref, gb_ref, doa_ref, dob_ref, dga_ref, dgb_ref, dl_ref):
        def branch(dyv, o, g, do_ref, dg_ref):
            sg = jax.nn.sigmoid(g)
            do = dyv * (g * sg)
            do_ref[...] = do.astype(MXU_DTYPE)
            dg_ref[...] = (dyv * o * (sg * (1.0 + g * (1.0 - sg)))).astype(MXU_DTYPE)
            return do

        branch(dy_ref[:, :1024], oa_ref[...], ga_ref[...], doa_ref, dga_ref)
        ob = ob_ref[...]
        prod = branch(dy_ref[:, 1024:], ob, gb_ref[...], dob_ref, dgb_ref) * ob
        lane = lax.broadcasted_iota(jnp.int32, (tm, stat_w), 1)
        acc = jnp.zeros((tm, stat_w), F32)
        for hh in range(MLA_HEADS):
            at = (hh // HEADS_PER_STEP) * 128 + hh % HEADS_PER_STEP
            acc = jnp.where(lane == at, jnp.sum(prod[:, hh * 128:(hh + 1) * 128], axis=-1, keepdims=True), acc)
        dl_ref[...] = acc

    stat_w = 128 * (MLA_HEADS // HEADS_PER_STEP)
    o_spec, ga_spec, gb_spec = _gate_specs(tm)
    return pl.pallas_call(
        body, name=name, grid=(s // tm,),
        in_specs=[pl.BlockSpec((tm, 2048), lambda i: (i, 0)), o_spec, o_spec, ga_spec, gb_spec],
        out_specs=[o_spec, o_spec, o_spec, o_spec, pl.BlockSpec((tm, stat_w), lambda i: (i, 0))],
        out_shape=[jax.ShapeDtypeStruct((s, 1024), MXU_DTYPE)] * 4 + [jax.ShapeDtypeStruct((s, stat_w), F32)],
        compiler_params=_cparams("parallel"),
    )(dy, oa, ob, proj, proj)


def _row_block(rows, cols, itemsize=4, budget=2 << 20):
    fits = [tr for tr in range(16, rows + 1, 16) if rows % tr == 0 and tr * cols * itemsize <= budget]
    return fits[-1] if fits else rows


def _adamw(w, g, m, v, *, name):
    shape = w.shape
    rows, cols = shape[-2:]
    w3, g3, m3, v3 = (a.reshape((-1, rows, cols)) for a in (w, g, m, v))
    lead = w3.shape[0]
    tr = _row_block(rows, cols)

    def body(w_ref, g_ref, m_ref, v_ref, d_ref, mo_ref, vo_ref):
        gv = g_ref[...]
        mn = ADAM_B1 * m_ref[...] + (1.0 - ADAM_B1) * gv
        vn = ADAM_B2 * v_ref[...] + (1.0 - ADAM_B2) * jnp.square(gv)
        m_hat = mn / (1.0 - ADAM_B1 ** ADAM_STEP)
        v_hat = vn / (1.0 - ADAM_B2 ** ADAM_STEP)
        d_ref[...] = -ADAM_LR * (m_hat / (jnp.sqrt(v_hat) + ADAM_EPS) + ADAM_WD * w_ref[...])
        mo_ref[...] = mn
        vo_ref[...] = vn

    spec = pl.BlockSpec((None, tr, cols), lambda a, i: (a, i, 0))
    outs = pl.pallas_call(
        body, name=name, grid=(lead, rows // tr), in_specs=[spec] * 4, out_specs=[spec] * 3,
        out_shape=[jax.ShapeDtypeStruct((lead, rows, cols), F32)] * 3,
        compiler_params=_cparams("parallel", "parallel"),
    )(w3, g3, m3, v3)
    return tuple(o.reshape(shape) for o in outs)


def _pair_sum(where, grads, recv, *, name):
    depth, chips, rows, cols = grads.shape
    hl = depth // 2
    tr = _row_block(rows, cols)

    def body(where_ref, a_ref, b_ref, o_ref):
        o_ref[...] = (a_ref[...] + b_ref[...].astype(F32)).astype(WIRE_DTYPE)

    spec = pl.BlockSpec((None, None, tr, cols), lambda a, k, i, w: (a, k, i, 0))
    return pl.pallas_call(
        body, name=name,
        grid_spec=pltpu.PrefetchScalarGridSpec(
            num_scalar_prefetch=1, grid=(hl, chips, rows // tr),
            in_specs=[pl.BlockSpec((None, None, tr, cols), lambda a, k, i, w: (hl * w[4] + a, k, i, 0)), spec],
            out_specs=spec),
        out_shape=jax.ShapeDtypeStruct((hl, chips, rows, cols), WIRE_DTYPE),
        compiler_params=_cparams("parallel", "parallel", "parallel"),
    )(where, grads, recv)


def _chip_sum(where, grads, recv, parts, *, name):
    depth, _, rows, cols = grads.shape
    hl = depth // 2
    tr = _row_block(rows, cols)

    def body(where_ref, a_ref, b_ref, t0_ref, t1_ref, t2_ref, o_ref):
        total = a_ref[...] + b_ref[...].astype(F32)
        for t_ref in (t0_ref, t1_ref, t2_ref):
            total = total + t_ref[...].astype(F32)
        o_ref[...] = total

    slot = lambda j: pl.BlockSpec((None, None, tr, cols), lambda a, i, w: (a, w[j], i, 0))
    return pl.pallas_call(
        body, name=name,
        grid_spec=pltpu.PrefetchScalarGridSpec(
            num_scalar_prefetch=1, grid=(hl, rows // tr),
            in_specs=[pl.BlockSpec((None, None, tr, cols), lambda a, i, w: (hl * w[4] + a, w[0], i, 0)), slot(0), slot(1),
                      slot(2), slot(3)],
            out_specs=pl.BlockSpec((None, tr, cols), lambda a, i, w: (hl * w[4] + a, i, 0))),
        out_shape=jax.ShapeDtypeStruct((depth, rows, cols), F32),
        compiler_params=_cparams("parallel", "parallel"),
    )(where, grads, recv, parts, parts, parts)


def _place():
    x, y, c = lax.axis_index("x"), lax.axis_index("y"), lax.axis_index("c")
    chips = [(1 - x, y), (x, 1 - y), (1 - x, 1 - y)]
    return x, y, c, chips


def _sems(*shape):
    return [pltpu.SemaphoreType.DMA(shape), pltpu.SemaphoreType.DMA(shape)]


OWNER_CORE = (0, 1, 1, 1)


def _gather_copy(src, out, send_sems, recv_sems, t, sem, slot, to, forward=False):
    n = src[t].shape[0]
    rows = out[t].at[pl.ds(pl.multiple_of(slot * n, 16), n)]
    return pltpu.make_async_remote_copy(src_ref=rows if forward else src[t], dst_ref=rows, send_sem=send_sems.at[sem, t],
                                        recv_sem=recv_sems.at[sem, t], device_id=to, device_id_type=MESH)


def _gather_start(src, out, send_sems, recv_sems):
    x, y, c, chips = _place()
    for t, owner in enumerate(OWNER_CORE):
        @pl.when(c == owner)
        def _():
            for j, chip in enumerate(chips):
                _gather_copy(src, out, send_sems, recv_sems, t, j, 2 * x + y, (*chip, c)).start()


def _gather_finish(src, out, send_sems, recv_sems):
    x, y, c, chips = _place()
    slots = [2 * px + py for px, py in chips]
    for t, owner in enumerate(OWNER_CORE):
        copy = functools.partial(_gather_copy, src, out, send_sems, recv_sems, t)

        @pl.when(c == owner)
        def _():
            for j, (px, py) in enumerate(chips):
                copy(j, slots[j], (px, py, c)).wait_recv()
                copy(3 + j, slots[j], (x, y, 1 - c), forward=True).start()
            for j, (px, py) in enumerate(chips):
                copy(j, 2 * x + y, (px, py, c)).wait_send()
                copy(3 + j, slots[j], (x, y, 1 - c), forward=True).wait_send()

        @pl.when(c != owner)
        def _():
            for j in range(3):
                copy(3 + j, slots[j], (x, y, 1 - c), forward=True).wait_recv()


def _gathered_shapes(shards):
    return [jax.ShapeDtypeStruct((4 * a.shape[0], a.shape[1]), a.dtype) for a in shards]


def _comm_gather_layer(shards, *, name):
    nt = len(shards)

    def body(*refs):
        src, out, sems = refs[:nt], refs[nt:2 * nt], refs[2 * nt:]
        _gather_start(src, out, *sems)
        _gather_finish(src, out, *sems)

    return pl.pallas_call(
        body, name=name, in_specs=[ANY] * nt, out_specs=[ANY] * nt, out_shape=_gathered_shapes(shards),
        scratch_shapes=_sems(6, nt),
    )(*shards)


def _comm_swap_sibling(bufs, *, name):
    nt = len(bufs)

    def body(*refs):
        src, out, (send_sems, recv_sems) = refs[:nt], refs[nt:2 * nt], refs[2 * nt:]
        x, y, c, _ = _place()
        cps = [pltpu.make_async_remote_copy(src_ref=src[t], dst_ref=out[t], send_sem=send_sems.at[t], recv_sem=recv_sems.at[t],
                                            device_id=(x, y, 1 - c), device_id_type=MESH) for t in range(nt)]
        for cp in cps:
            cp.start()
        for cp in cps:
            cp.wait()

    return pl.pallas_call(
        body, name=name, in_specs=[ANY] * nt, out_specs=[ANY] * nt,
        out_shape=[jax.ShapeDtypeStruct(a.shape, a.dtype) for a in bufs], scratch_shapes=_sems(nt),
    )(*bufs)


def _scatter_copy(src, out, send_sems, recv_sems, j, t, from_slot, to_slot, to):
    return pltpu.make_async_remote_copy(src_ref=src[t].at[:, from_slot], dst_ref=out[t].at[:, to_slot],
                                        send_sem=send_sems.at[j, t], recv_sem=recv_sems.at[j, t], device_id=to,
                                        device_id_type=MESH)


def _scatter_start(src, out, send_sems, recv_sems):
    x, y, c, chips = _place()
    for j, (px, py) in enumerate(chips):
        for t in range(len(src)):
            _scatter_copy(src, out, send_sems, recv_sems, j, t, 2 * px + py, 2 * x + y, (px, py, c)).start()


def _scatter_finish(src, out, send_sems, recv_sems):
    x, y, c, chips = _place()
    for j, (px, py) in enumerate(chips):
        for t in range(len(src)):
            _scatter_copy(src, out, send_sems, recv_sems, j, t, 2 * x + y, 2 * px + py, (px, py, c)).wait_recv()
    for j, (px, py) in enumerate(chips):
        for t in range(len(src)):
            _scatter_copy(src, out, send_sems, recv_sems, j, t, 2 * px + py, 2 * x + y, (px, py, c)).wait_send()


def _comm_scatter_chips(parts, *, name):
    nt = len(parts)

    def body(*refs):
        src, out, sems = refs[:nt], refs[nt:2 * nt], refs[2 * nt:]
        _scatter_start(src, out, *sems)
        _scatter_finish(src, out, *sems)

    return pl.pallas_call(
        body, name=name, in_specs=[ANY] * nt, out_specs=[ANY] * nt,
        out_shape=[jax.ShapeDtypeStruct(a.shape, a.dtype) for a in parts], scratch_shapes=_sems(3, nt),
    )(*parts)


def _comm_join_halves(bufs, *, name):
    nt = len(bufs)
    hl = bufs[0].shape[0] // 2

    def body(*refs):
        src, out, (send_sems, recv_sems) = refs[:nt], refs[nt:2 * nt], refs[2 * nt:]
        x, y, c, _ = _place()
        mine = pl.ds(c * hl, hl)
        other = pl.ds((1 - c) * hl, hl)
        copy = lambda t, layers: pltpu.make_async_remote_copy(
            src_ref=src[t].at[mine], dst_ref=out[t].at[layers], send_sem=send_sems.at[t], recv_sem=recv_sems.at[t],
            device_id=(x, y, 1 - c), device_id_type=MESH)
        sends = [copy(t, mine) for t in range(nt)]
        for cp in sends:
            cp.start()
        for t in range(nt):
            copy(t, other).wait_recv()
        for cp in sends:
            cp.wait_send()

    return pl.pallas_call(
        body, name=name, in_specs=[ANY] * nt, out_specs=[ANY] * nt,
        out_shape=[jax.ShapeDtypeStruct(a.shape, a.dtype) for a in bufs],
        input_output_aliases={t: t for t in range(nt)}, scratch_shapes=_sems(nt),
    )(*bufs)


def _comm_allreduce_small(part, *, name):
    rows, cols = part.shape

    def body(p_ref, o_ref, buf, send_sems, recv_sems):
        x, y, c, _ = _place()
        me = 4 * x + 2 * y + c
        buf[me] = p_ref[...]
        flip = lambda v, bit: 1 - v if bit else v
        peers = [(flip(x, d & 4), flip(y, d & 2), flip(c, d & 1)) for d in range(1, 8)]
        sends = []
        for j, peer in enumerate(peers):
            cp = pltpu.make_async_remote_copy(src_ref=buf.at[me], dst_ref=buf.at[me], send_sem=send_sems.at[j],
                                              recv_sem=recv_sems.at[j], device_id=peer, device_id_type=MESH)
            cp.start()
            sends.append(cp)
        for j, (px, py, pc) in enumerate(peers):
            pltpu.make_async_remote_copy(src_ref=buf.at[me], dst_ref=buf.at[4 * px + 2 * py + pc], send_sem=send_sems.at[j],
                                         recv_sem=recv_sems.at[j], device_id=(px, py, pc), device_id_type=MESH).wait_recv()
        for cp in sends:
            cp.wait_send()
        total = buf[0]
        for i in range(1, 8):
            total = total + buf[i]
        o_ref[...] = total

    vm = pl.BlockSpec(memory_space=pltpu.VMEM)
    return pl.pallas_call(
        body, name=name, in_specs=[vm], out_specs=vm, out_shape=jax.ShapeDtypeStruct((rows, cols), F32),
        scratch_shapes=[pltpu.VMEM((8, rows, cols), F32), pltpu.SemaphoreType.DMA((7,)), pltpu.SemaphoreType.DMA((7,))],
    )(part)


def _pad_in_rows(wt):
    r = lambda o, n: wt[..., o:o + n, :]
    kr = r(2944, 64)
    return jnp.concatenate([r(0, 1024), r(1280, 1024), r(3008, 1024), r(2304, Q_RANK), r(1024, 128), r(2688, KV_RANK),
                            r(1152, 128), kr, jnp.zeros_like(kr)], axis=-2)


def _unpad_in_rows(qa, ga, gb, mixed):
    cq, ka, ckv, va, kr = (mixed[..., o - CQ_OFF:o - CQ_OFF + n, :] for o, n in (
        (CQ_OFF, Q_RANK), (KA_OFF, 128), (CKV_OFF, KV_RANK), (VA_OFF, 128), (KR_OFF, 64)))
    return jnp.concatenate([qa, ka, va, ga, cq, ckv, kr, gb], axis=-2)


def _pad_q_rows(wt):
    lead, cols = wt.shape[:-2], wt.shape[-1]
    wt = jnp.pad(wt.reshape(lead + (MLA_HEADS, 192, cols)), [(0, 0)] * (len(lead) + 1) + [(0, 64), (0, 0)])
    return wt.reshape(lead + (MLA_HEADS * 256, cols))


def _unpad_q_rows(wt):
    lead, cols = wt.shape[:-2], wt.shape[-1]
    return wt.reshape(lead + (MLA_HEADS, 256, cols))[..., :192, :].reshape(lead + (MLA_HEADS * 192, cols))


def _perm_kv_rows(wt):
    lead, cols = wt.shape[:-2], wt.shape[-1]
    return jnp.swapaxes(wt.reshape(lead + (MLA_HEADS, 2, 128, cols)), -4, -3).reshape(lead + (2048, cols))


def _unperm_kv_rows(wt):
    lead, cols = wt.shape[:-2], wt.shape[-1]
    return jnp.swapaxes(wt.reshape(lead + (2, MLA_HEADS, 128, cols)), -4, -3).reshape(lead + (2048, cols))


def _t(a):
    return jnp.swapaxes(a, -1, -2)


def _rope_tables(s):
    pos = jnp.arange(s, dtype=F32)
    inv_freq = 10000.0 ** (-jnp.arange(0, 64, 2, dtype=F32) / 64)
    ang = pos[:, None] * inv_freq[None, :]
    cos, sin = jnp.cos(ang), jnp.sin(ang)
    z64 = jnp.zeros((s, 64), F32)
    tk_c = jnp.concatenate([cos, cos, z64], axis=-1)
    tk_s = jnp.concatenate([-sin, sin, z64], axis=-1)
    tq_c = jnp.concatenate([jnp.ones((s, 128), F32), tk_c], axis=-1)
    tq_s = jnp.concatenate([jnp.zeros((s, 128), F32), tk_s], axis=-1)
    return tq_c, tq_s, tk_c, tk_s


def _layer_weights(gathered, own, chip):
    def with_own_rows(g, o):
        n = o.shape[0]
        return jnp.concatenate([lax.select(chip == j, o, g[j * n:(j + 1) * n]) for j in range(4)], axis=0)

    full_in, full_q, full_kv, full_o = (with_own_rows(g, o) for g, o in zip(gathered, own))
    return _pad_in_rows(full_in), _pad_q_rows(full_q), _perm_kv_rows(full_kv), full_o


def _device_step(xs, tgt, attn_g, sinks, gq, gkv, final_g, shards, where):
    chip = where[0]
    depth = shards[0].shape[0]
    s = xs.shape[0]
    tabs = _rope_tables(s)
    saved = []
    x = xs
    of_layer = lambda l: [a[l] for a in shards]
    gathered = _comm_gather_layer(of_layer(0), name="comm_gather_layer0")
    weights = []
    for l in range(depth):
        w_in_p, w_q_p, w_kv_p, w_o = _layer_weights(gathered, of_layer(l), chip)
        weights.append((w_in_p, w_q_p, w_kv_p, w_o))
        h = _rmsnorm_fwd(x, attn_g[l:l + 1], name=f"norm_fwd{l}")
        proj = _matmul(h, w_in_p, tb=True, name=f"in_proj{l}")
        oa, lse_a = _swa_fwd(proj, sinks[l], name=f"swa_fwd{l}")
        qcat, kcat, v, cqn, ckvn = _mla_qkv_fwd(proj, gq[l:l + 1], gkv[l:l + 1], w_q_p, w_kv_p, *tabs, layer=None,
                                                name=f"mla_qkv_fwd{l}")
        ob, lse_b, gathered = _mla_fwd(qcat, kcat, v, gather=of_layer(l + 1) if l + 1 < depth else None,
                                       name=f"mla_fwd{l}")
        y = _gate_fwd(oa, ob, proj, name=f"gate_fwd{l}")
        x_next = _matmul(y, w_o, add=x, name=f"out_proj{l}")
        saved.append((x, h, proj, oa, lse_a, qcat, kcat, v, cqn, ckvn, ob, lse_b, y))
        x = x_next

    dx, d_final_g, loss = _final_loss(x, final_g, tgt, name="final_loss")

    d_attn_g, d_sinks, d_gq, d_gkv = [None] * depth, [None] * depth, [None] * depth, [None] * depth
    d_w_in, d_w_q, d_w_kv, d_w_o = [None] * depth, [None] * depth, [None] * depth, [None] * depth
    half = depth // 2

    def grads_by_chip(layers):
        stack = lambda per_layer: jnp.stack([per_layer[l] for l in layers])
        in_tiles = [stack([tiles[j] if tiles else None for tiles in d_w_in]) for j in range(4)]
        full = [_unpad_in_rows(*in_tiles), _unpad_q_rows(stack(d_w_q)), _unperm_kv_rows(stack(d_w_kv)), stack(d_w_o)]
        return [g.reshape(g.shape[0], 4, g.shape[1] // 4, g.shape[2]) for g in full]

    def reduce_begin(grads4, tag):
        hl = grads4[0].shape[0] // 2
        give = [lax.dynamic_slice_in_dim(g, (1 - where[4]) * hl, hl, axis=0).astype(WIRE_DTYPE) for g in grads4]
        recv = _comm_swap_sibling(give, name=f"comm_swap_sibling_{tag}")
        return recv, [_pair_sum(where, g, r, name=f"pair_sum_{tag}{t}") for t, (g, r) in enumerate(zip(grads4, recv))]

    def reduce_end(grads4, recv, parts, tag):
        mine = [_chip_sum(where, g, r, p, name=f"chip_sum_{tag}{t}") for t, (g, r, p) in enumerate(zip(grads4, recv, parts))]
        return _comm_join_halves(mine, name=f"comm_join_halves_{tag}")

    for l in reversed(range(depth)):
        x, h, proj, oa, lse_a, qcat, kcat, v, cqn, ckvn, ob, lse_b, y = saved[l]
        w_in_p, w_q_p, w_kv_p, w_o = weights[l]
        dy = _matmul(dx, w_o, tb=True, name=f"out_proj_dx{l}")
        d_w_o[l] = _matmul(y, dx, ta=True, name=f"out_proj_dw{l}")
        doa, dob, dga, dgb, delta_b = _gate_bwd(dy, oa, ob, proj, name=f"gate_bwd{l}")
        dqa, dka, dva, dsink = _swa_bwd(proj, sinks[l], lse_a, doa, name=f"swa_bwd{l}")
        pair_hi = None
        if l == half - 1:
            grads_hi = grads_by_chip(range(half, depth))
            recv_hi, pair_hi = reduce_begin(grads_hi, "hi")
        dqc, dkc, dv, parts = _mla_bwd(qcat, kcat, v, dob, lse_b, delta_b, scatter=pair_hi, name=f"mla_bwd{l}")
        if l == half - 1:
            reduced_hi = reduce_end(grads_hi, recv_hi, parts, "hi")
        mixed, d_w_q[l], d_w_kv[l], dgq_l, dgkv_l = _mla_qkv_bwd(
            proj, cqn, ckvn, dqc, dkc, dv, dka, dva, gq[l:l + 1], gkv[l:l + 1], w_q_p, w_kv_p, *tabs, layer=None,
            name=f"mla_qkv_bwd{l}")
        dproj = [dqa, dga, dgb, mixed]
        dh = _matmul_ktiles(dproj, w_in_p, name=f"in_proj_dx{l}")
        d_w_in[l] = [_matmul(tile, h, ta=True, name=f"in_proj_dw{l}_{j}") for j, tile in enumerate(dproj)]
        dx, dg_l = _rmsnorm_bwd(dh, x, attn_g[l:l + 1], dx, name=f"norm_bwd{l}")
        d_attn_g[l], d_sinks[l], d_gq[l], d_gkv[l] = dg_l, dsink[0:1, :SWA_HEADS], dgq_l, dgkv_l

    grads_lo = grads_by_chip(range(half))
    recv_lo, pair_lo = reduce_begin(grads_lo, "lo")
    reduced_lo = reduce_end(grads_lo, recv_lo, _comm_scatter_chips(pair_lo, name="comm_scatter_chips_lo"), "lo")
    cat = lambda parts: jnp.concatenate(parts, axis=0)
    reduced = [cat([lo, hi]) for lo, hi in zip(reduced_lo, reduced_hi)]
    return loss, dx, cat(d_attn_g), cat(d_sinks), cat(d_gq), cat(d_gkv), d_final_g, reduced


def kernel(x, attn_norm_g, w_in, swa_sinks, q_a_norm_g, kv_a_norm_g, w_q_b, w_kv_b, w_out, final_norm_g, loss_target, m_attn_norm_g, m_w_in, m_swa_sinks, m_q_a_norm_g, m_kv_a_norm_g, m_w_q_b, m_w_kv_b, m_w_out, m_final_norm_g, v_attn_norm_g, v_w_in, v_swa_sinks, v_q_a_norm_g, v_kv_a_norm_g, v_w_q_b, v_w_kv_b, v_w_out, v_final_norm_g):
    x_, y_, c = lax.axis_index("x"), lax.axis_index("y"), lax.axis_index("c")
    chip = 2 * x_ + y_
    where = jnp.stack([chip, 2 * (1 - x_) + y_, 2 * x_ + 1 - y_, 2 * (1 - x_) + 1 - y_, c]).astype(jnp.int32)

    sent = [a.astype(WIRE_DTYPE) for a in (_t(w_in), _t(w_q_b), _t(w_kv_b), w_out)]

    loss, dx, d_attn_g, d_sinks, d_gq, d_gkv, d_final_g, reduced = _device_step(
        x[0], loss_target[0], attn_norm_g, swa_sinks, q_a_norm_g, kv_a_norm_g, final_norm_g.reshape(1, -1), sent, where)
    g_w_in, g_w_q_b, g_w_kv_b, g_w_out = _t(reduced[0]), _t(reduced[1]), _t(reduced[2]), reduced[3]

    small = [d_attn_g, d_sinks, d_gq, d_gkv, d_final_g, loss[:, :1]]
    flat = jnp.concatenate([a.reshape(-1) for a in small])
    n_small = flat.shape[0]
    rows = -(-n_small // 1024) * 8
    total = _comm_allreduce_small(jnp.pad(flat, (0, rows * 128 - n_small)).reshape(rows, 128),
                                  name="comm_allreduce_small").reshape(-1)
    outs, at = [], 0
    for a in small:
        outs.append(total[at:at + a.size].reshape(a.shape))
        at += a.size
    g_attn_g, g_sinks, g_gq, g_gkv, g_final_g, loss_total = outs
    g_final_g = g_final_g.reshape(final_norm_g.shape)

    weights = [attn_norm_g, w_in, swa_sinks, q_a_norm_g, kv_a_norm_g, w_q_b, w_kv_b, w_out, final_norm_g]
    grads = [g_attn_g, g_w_in, g_sinks, g_gq, g_gkv, g_w_q_b, g_w_kv_b, g_w_out, g_final_g]
    ms = [m_attn_norm_g, m_w_in, m_swa_sinks, m_q_a_norm_g, m_kv_a_norm_g, m_w_q_b, m_w_kv_b, m_w_out, m_final_norm_g]
    vs = [v_attn_norm_g, v_w_in, v_swa_sinks, v_q_a_norm_g, v_kv_a_norm_g, v_w_q_b, v_w_kv_b, v_w_out, v_final_norm_g]
    as2d = lambda a: a.reshape(1, -1) if a.ndim == 1 else a
    deltas, new_m, new_v = [], [], []
    for i, (w, g, m, v) in enumerate(zip(weights, grads, ms, vs)):
        view = _t if w is w_in else as2d
        d, mn, vn = _adamw(view(w), reduced[0] if w is w_in else view(g), view(m), view(v), name=f"adamw{i}")
        back = _t if w is w_in else (lambda a: a.reshape(w.shape))
        deltas.append(back(d))
        new_m.append(back(mn))
        new_v.append(back(vn))

    return (loss_total.reshape(()), dx[None], *grads, *deltas, *new_m, *new_v)
```

```python
import functools
import math

import jax
import jax.numpy as jnp
from jax import lax
from jax.experimental import pallas as pl
from jax.experimental.pallas import tpu as pltpu

F32 = jnp.float32
MXU_DTYPE = jnp.bfloat16
WIRE_DTYPE = jnp.bfloat16

EPS = 1e-6
NEG = -1e30
BLOCK = 128
D_MODEL = 2048
SWA_HEADS = 16
MLA_HEADS = 8
Q_RANK = 384
KV_RANK = 256
IN_WIDTH = 4032
MLA_SCALE = 192 ** -0.5
SWA_SCALE = 64 ** -0.5
LOG2E = math.log2(math.e)
HEADS_PER_STEP = 2
SLOPES = tuple(2.0 ** (-8.0 * (h + 1) / SWA_HEADS) for h in range(SWA_HEADS))

P_WIDTH = 4096
QA_OFF, GA_OFF, GB_OFF, CQ_OFF, KA_OFF, CKV_OFF, VA_OFF, KR_OFF = 0, 1024, 2048, 3072, 3456, 3584, 3840, 3968

ADAM_LR, ADAM_B1, ADAM_B2, ADAM_EPS, ADAM_WD, ADAM_STEP = 0.001, 0.9, 0.999, 1e-08, 0.01, 10

VMEM_LIMIT = 56 * 1024 * 1024
MESH = pl.DeviceIdType.MESH
ANY = pl.BlockSpec(memory_space=pl.ANY)


def _cparams(*sem):
    return pltpu.CompilerParams(dimension_semantics=sem, vmem_limit_bytes=VMEM_LIMIT)


def _dot(a, b, ca, cb):
    return lax.dot_general(a, b, (((ca,), (cb,)), ((), ())), preferred_element_type=F32)


def _layer_spec(block, index_map, layer):
    if layer is None:
        return pl.BlockSpec(block, index_map)
    return pl.BlockSpec((None,) + tuple(block), lambda *g: (layer,) + tuple(index_map(*g)))


def _matmul(a, b, *, name, ta=False, tb=False, out_dtype=F32, add=None, b_layer=None, tm=1024, tn=1024, tk=2048):
    (kdim, m) = a.shape if ta else a.shape[::-1]
    (n, k2) = b.shape[-2:] if tb else b.shape[-2:][::-1]
    assert kdim == k2, (a.shape, b.shape)
    tm, tn, tk = min(tm, m), min(tn, n), min(tk, kdim)
    assert m % tm == 0 and n % tn == 0 and kdim % tk == 0
    nk = kdim // tk

    def body(*refs):
        a_ref, b_ref = refs[:2]
        add_ref = None if add is None else refs[2]
        o_ref = refs[2 + (add is not None)]
        part = _dot(a_ref[...].astype(MXU_DTYPE), b_ref[...].astype(MXU_DTYPE), 0 if ta else 1, 1 if tb else 0)

        def finish(r):
            o_ref[...] = (r if add is None else add_ref[...] + r).astype(out_dtype)

        if nk == 1:
            finish(part)
            return
        acc = refs[-1]
        k = pl.program_id(2)

        @pl.when(k == 0)
        def _():
            acc[...] = part

        @pl.when((k > 0) & (k < nk - 1))
        def _():
            acc[...] += part

        @pl.when(k == nk - 1)
        def _():
            finish(acc[...] + part)

    a_spec = pl.BlockSpec((tk, tm), lambda i, j, k: (k, i)) if ta else pl.BlockSpec((tm, tk), lambda i, j, k: (i, k))
    b_spec = (_layer_spec((tn, tk), lambda i, j, k: (j, k), b_layer) if tb else
              _layer_spec((tk, tn), lambda i, j, k: (k, j), b_layer))
    in_specs, args = [a_spec, b_spec], [a, b]
    if add is not None:
        in_specs.append(pl.BlockSpec((tm, tn), lambda i, j, k: (i, j)))
        args.append(add)
    return pl.pallas_call(
        body, name=name, grid=(m // tm, n // tn, nk), in_specs=in_specs,
        out_specs=pl.BlockSpec((tm, tn), lambda i, j, k: (i, j)),
        out_shape=jax.ShapeDtypeStruct((m, n), out_dtype),
        scratch_shapes=[pltpu.VMEM((tm, tn), F32)] if nk > 1 else [],
        compiler_params=_cparams("parallel", "parallel", "arbitrary"),
    )(*args)


def _matmul_ktiles(a_tiles, b, *, name, b_layer=None, tm=512, tn=1024):
    m, kt = a_tiles[0].shape
    n = b.shape[-1]
    nt = len(a_tiles)
    assert b.shape[-2] == nt * kt
    tm, tn = min(tm, m), min(tn, n)
    assert m % tm == 0 and n % tn == 0

    def body(*refs):
        a_refs, b_refs, o_ref = refs[:nt], refs[nt:2 * nt], refs[2 * nt]
        acc = _dot(a_refs[0][...].astype(MXU_DTYPE), b_refs[0][...].astype(MXU_DTYPE), 1, 0)
        for j in range(1, nt):
            acc += _dot(a_refs[j][...].astype(MXU_DTYPE), b_refs[j][...].astype(MXU_DTYPE), 1, 0)
        o_ref[...] = acc

    in_specs = [pl.BlockSpec((tm, kt), lambda i, jn: (i, 0))] * nt
    in_specs += [_layer_spec((kt, tn), lambda i, jn, j=j: (j, jn), b_layer) for j in range(nt)]
    return pl.pallas_call(
        body, name=name, grid=(m // tm, n // tn), in_specs=in_specs,
        out_specs=pl.BlockSpec((tm, tn), lambda i, jn: (i, jn)),
        out_shape=jax.ShapeDtypeStruct((m, n), F32),
        compiler_params=_cparams("parallel", "parallel"),
    )(*a_tiles, *([b] * nt))


def _rmsnorm_fwd(x, g, *, name):
    s, d = x.shape
    tm = min(512, s)

    def body(x_ref, g_ref, h_ref):
        xv = x_ref[...]
        r = lax.rsqrt(jnp.mean(xv * xv, axis=-1, keepdims=True) + EPS)
        h_ref[...] = (xv * r * g_ref[...]).astype(MXU_DTYPE)

    return pl.pallas_call(
        body, name=name, grid=(s // tm,),
        in_specs=[pl.BlockSpec((tm, d), lambda i: (i, 0)), pl.BlockSpec((1, d), lambda i: (0, 0))],
        out_specs=pl.BlockSpec((tm, d), lambda i: (i, 0)),
        out_shape=jax.ShapeDtypeStruct((s, d), MXU_DTYPE),
        compiler_params=_cparams("parallel"),
    )(x, g)


def _rmsnorm_bwd(dh, x, g, dres, *, name):
    s, d = x.shape
    tm = min(512, s)

    def body(dh_ref, x_ref, g_ref, dres_ref, dx_ref, dg_ref):
        @pl.when(pl.program_id(0) == 0)
        def _():
            dg_ref[...] = jnp.zeros_like(dg_ref)

        xv = x_ref[...]
        r = lax.rsqrt(jnp.mean(xv * xv, axis=-1, keepdims=True) + EPS)
        xn = xv * r
        dy = dh_ref[...]
        dg_ref[...] += jnp.sum(dy * xn, axis=0, keepdims=True)
        u = dy * g_ref[...]
        dx_ref[...] = dres_ref[...] + r * (u - xn * jnp.mean(u * xn, axis=-1, keepdims=True))

    row = pl.BlockSpec((tm, d), lambda i: (i, 0))
    vec = pl.BlockSpec((1, d), lambda i: (0, 0))
    return pl.pallas_call(
        body, name=name, grid=(s // tm,), in_specs=[row, row, vec, row], out_specs=[row, vec],
        out_shape=[jax.ShapeDtypeStruct((s, d), F32), jax.ShapeDtypeStruct((1, d), F32)],
        compiler_params=_cparams("arbitrary"),
    )(dh, x, g, dres)


def _final_loss(x, g, tgt, *, name):
    s, d = x.shape
    tm = min(512, s)

    def body(x_ref, g_ref, t_ref, dx_ref, dg_ref, loss_ref):
        @pl.when(pl.program_id(0) == 0)
        def _():
            dg_ref[...] = jnp.zeros_like(dg_ref)
            loss_ref[...] = jnp.zeros_like(loss_ref)

        xv = x_ref[...]
        gv = g_ref[...]
        r = lax.rsqrt(jnp.mean(xv * xv, axis=-1, keepdims=True) + EPS)
        xn = xv * r
        err = xn * gv - t_ref[...]
        sq = jnp.sum(jnp.sum(err * err, axis=-1, keepdims=True), axis=0, keepdims=True)
        loss_ref[...] += (0.5 / d) * sq
        dy = err * (1.0 / d)
        dg_ref[...] += jnp.sum(dy * xn, axis=0, keepdims=True)
        u = dy * gv
        dx_ref[...] = r * (u - xn * jnp.mean(u * xn, axis=-1, keepdims=True))

    row = pl.BlockSpec((tm, d), lambda i: (i, 0))
    vec = pl.BlockSpec((1, d), lambda i: (0, 0))
    return pl.pallas_call(
        body, name=name, grid=(s // tm,), in_specs=[row, vec, row],
        out_specs=[row, vec, pl.BlockSpec((1, 128), lambda i: (0, 0))],
        out_shape=[jax.ShapeDtypeStruct((s, d), F32), jax.ShapeDtypeStruct((1, d), F32),
                   jax.ShapeDtypeStruct((1, 128), F32)],
        compiler_params=_cparams("arbitrary"),
    )(x, g, tgt)


def _swa_keys(kp_ref, kc_ref):
    kk = jnp.concatenate([kp_ref[...], kc_ref[...]], axis=0)
    kr = pltpu.roll(kk, 64, 1)
    lo = lax.broadcasted_iota(jnp.int32, kk.shape, 1) < 64
    return [jnp.where(lo, kk, kr).astype(MXU_DTYPE), jnp.where(lo, kr, kk).astype(MXU_DTYPE)]


GROUP = SWA_HEADS // 2


def _swa_mask(n):
    qi = lax.broadcasted_iota(jnp.int32, (BLOCK, 2 * BLOCK), 0)
    ki = lax.broadcasted_iota(jnp.int32, (BLOCK, 2 * BLOCK), 1)
    delta = BLOCK + qi - ki
    valid = (delta >= 0) & (delta < BLOCK) & ((ki >= BLOCK) | (n > 0))
    return valid, delta.astype(F32)


def _stack_heads(ref, j):
    lo = lax.broadcasted_iota(jnp.int32, (BLOCK, BLOCK), 1) < 64
    parts = []
    for r in range(GROUP):
        pair = (GROUP * j + r) // 2
        blk = ref[:, pair * 128:(pair + 1) * 128].astype(F32)
        parts.append(jnp.where(lo if r % 2 == 0 else ~lo, blk, 0.0).astype(MXU_DTYPE))
    return jnp.concatenate(parts, axis=0)


def _unstack_heads(stacked, ref, j):
    lo = lax.broadcasted_iota(jnp.int32, (BLOCK, BLOCK), 1) < 64
    for i in range(GROUP // 2):
        pair = (GROUP * j) // 2 + i
        even, odd = stacked[2 * i * BLOCK:(2 * i + 1) * BLOCK], stacked[(2 * i + 1) * BLOCK:(2 * i + 2) * BLOCK]
        ref[:, pair * 128:(pair + 1) * 128] = jnp.where(lo, even, odd).astype(ref.dtype)


def _head_rows(stacked, r):
    return stacked[r * BLOCK:(r + 1) * BLOCK]


def _swa_scores(raw, h, valid, deltaf):
    return jnp.where(valid, raw * (SWA_SCALE * LOG2E) - (SLOPES[h] * LOG2E) * deltaf, NEG)


def _swa_specs(nb):
    kcol, vcol = KA_OFF // BLOCK, VA_OFF // BLOCK
    last = nb - 1
    cur = lambda n: jnp.minimum(n, last)
    prev = lambda n: jnp.maximum(jnp.minimum(n, last) - 1, 0)
    return [
        pl.BlockSpec(memory_space=pltpu.SMEM),
        pl.BlockSpec((BLOCK, 1024), lambda n: (cur(n), QA_OFF // 1024)),
        pl.BlockSpec((BLOCK, BLOCK), lambda n: (cur(n), kcol)),
        pl.BlockSpec((BLOCK, BLOCK), lambda n: (prev(n), kcol)),
        pl.BlockSpec((BLOCK, BLOCK), lambda n: (cur(n), vcol)),
        pl.BlockSpec((BLOCK, BLOCK), lambda n: (prev(n), vcol)),
    ]


def _swa_fwd(proj, sinks, *, name):
    s = proj.shape[0]
    nb = s // BLOCK

    def body(sink_ref, q_ref, kc_ref, kp_ref, vc_ref, vp_ref, o_ref, lse_ref):
        n = pl.program_id(0)
        keys = _swa_keys(kp_ref, kc_ref)
        vals = _swa_keys(vp_ref, vc_ref)
        valid, deltaf = _swa_mask(n)
        lane = lax.broadcasted_iota(jnp.int32, (BLOCK, BLOCK), 1)
        lse_acc = jnp.zeros((BLOCK, BLOCK), F32)
        for j in range(2):
            raw = _dot(_stack_heads(q_ref, j), keys[j], 1, 1)
            probs = []
            for r in range(GROUP):
                h = GROUP * j + r
                sc = _swa_scores(_head_rows(raw, r), h, valid, deltaf)
                sink = sink_ref[h] * LOG2E
                m = jnp.maximum(jnp.max(sc, axis=-1, keepdims=True), sink)
                p = jnp.exp2(sc - m)
                l = jnp.sum(p, axis=-1, keepdims=True) + jnp.exp2(sink - m)
                probs.append((p * (1.0 / l)).astype(MXU_DTYPE))
                lse_acc = jnp.where(lane == h, m + jnp.log(l) * LOG2E, lse_acc)
            _unstack_heads(jnp.dot(jnp.concatenate(probs, axis=0), vals[j], preferred_element_type=F32), o_ref, j)
        lse_ref[...] = lse_acc

    return pl.pallas_call(
        body, name=name, grid=(nb,), in_specs=_swa_specs(nb),
        out_specs=[pl.BlockSpec((BLOCK, 1024), lambda n: (n, 0)), pl.BlockSpec((BLOCK, BLOCK), lambda n: (n, 0))],
        out_shape=[jax.ShapeDtypeStruct((s, 1024), F32), jax.ShapeDtypeStruct((s, BLOCK), F32)],
        compiler_params=_cparams("parallel"),
    )(sinks, proj, proj, proj, proj, proj)


def _swa_bwd(proj, sinks, lse, do, *, name):
    s = proj.shape[0]
    nb = s // BLOCK
    last = nb - 1

    def body(sink_ref, q_ref, kc_ref, kp_ref, vc_ref, vp_ref, lse_ref, do_ref,
             dq_ref, dk_ref, dv_ref, dsink_ref, carry_k, carry_v):
        n = pl.program_id(0)

        @pl.when(n == 0)
        def _():
            carry_k[...] = jnp.zeros_like(carry_k)
            carry_v[...] = jnp.zeros_like(carry_v)
            dsink_ref[...] = jnp.zeros_like(dsink_ref)

        @pl.when(n < nb)
        def _():
            keys = _swa_keys(kp_ref, kc_ref)
            vals = _swa_keys(vp_ref, vc_ref)
            valid, deltaf = _swa_mask(n)
            lane = lax.broadcasted_iota(jnp.int32, (BLOCK, BLOCK), 1)
            lane1 = lax.broadcasted_iota(jnp.int32, (1, BLOCK), 1)
            lse_blk = lse_ref[...]
            acc_k, acc_v = [], []
            dsink = jnp.zeros((1, BLOCK), F32)
            for j in range(2):
                q_all, do_all = _stack_heads(q_ref, j), _stack_heads(do_ref, j)
                raw = _dot(q_all, keys[j], 1, 1)
                dp_all = _dot(do_all, vals[j], 1, 1)
                probs, dscores = [], []
                for r in range(GROUP):
                    h = GROUP * j + r
                    lse_h = jnp.sum(jnp.where(lane == h, lse_blk, 0.0), axis=-1, keepdims=True)
                    p = jnp.exp2(_swa_scores(_head_rows(raw, r), h, valid, deltaf) - lse_h)
                    dp = _head_rows(dp_all, r)
                    dlt = jnp.sum(dp * p, axis=-1, keepdims=True)
                    dscores.append((p * (dp - dlt) * SWA_SCALE).astype(MXU_DTYPE))
                    probs.append(p.astype(MXU_DTYPE))
                    sunk = jnp.exp2(sink_ref[h] * LOG2E - lse_h) * dlt
                    dsink = jnp.where(lane1 == h, -jnp.sum(sunk, axis=0, keepdims=True), dsink)
                ds_all = jnp.concatenate(dscores, axis=0)
                _unstack_heads(jnp.dot(ds_all, keys[j], preferred_element_type=F32), dq_ref, j)
                acc_k.append(_dot(ds_all, q_all, 0, 0))
                acc_v.append(_dot(jnp.concatenate(probs, axis=0), do_all, 0, 0))
            lo2 = lax.broadcasted_iota(jnp.int32, (2 * BLOCK, BLOCK), 1) < 64
            fold = lambda acc: jnp.where(lo2, acc[0] + pltpu.roll(acc[0], 64, 1), acc[1] + pltpu.roll(acc[1], 64, 1))
            dkk, dvv = fold(acc_k), fold(acc_v)
            dk_ref[...] = (carry_k[...] + dkk[:BLOCK]).astype(dk_ref.dtype)
            dv_ref[...] = (carry_v[...] + dvv[:BLOCK]).astype(dv_ref.dtype)
            carry_k[...] = dkk[BLOCK:]
            carry_v[...] = dvv[BLOCK:]
            dsink_ref[...] += jnp.broadcast_to(dsink, dsink_ref.shape)

        @pl.when(n == nb)
        def _():
            dk_ref[...] = carry_k[...].astype(dk_ref.dtype)
            dv_ref[...] = carry_v[...].astype(dv_ref.dtype)

    cur = lambda n: jnp.minimum(n, last)
    lag = lambda n: jnp.maximum(n - 1, 0)
    return pl.pallas_call(
        body, name=name, grid=(nb + 1,),
        in_specs=_swa_specs(nb) + [pl.BlockSpec((BLOCK, BLOCK), lambda n: (cur(n), 0)),
                                   pl.BlockSpec((BLOCK, 1024), lambda n: (cur(n), 0))],
        out_specs=[pl.BlockSpec((BLOCK, 1024), lambda n: (cur(n), 0)),
                   pl.BlockSpec((BLOCK, BLOCK), lambda n: (lag(n), 0)),
                   pl.BlockSpec((BLOCK, BLOCK), lambda n: (lag(n), 0)),
                   pl.BlockSpec((8, BLOCK), lambda n: (0, 0))],
        out_shape=[jax.ShapeDtypeStruct((s, 1024), MXU_DTYPE), jax.ShapeDtypeStruct((s, BLOCK), MXU_DTYPE),
                   jax.ShapeDtypeStruct((s, BLOCK), MXU_DTYPE), jax.ShapeDtypeStruct((8, BLOCK), F32)],
        scratch_shapes=[pltpu.VMEM((BLOCK, BLOCK), F32), pltpu.VMEM((BLOCK, BLOCK), F32)],
        compiler_params=_cparams("arbitrary"),
    )(sinks, proj, proj, proj, proj, proj, lse, do)


def _rope_partner(v, first, width):
    lane = lax.broadcasted_iota(jnp.int32, v.shape, 1)
    in_a = (lane >= first) & (lane < first + 32)
    in_b = (lane >= first + 32) & (lane < first + 64)
    return jnp.where(in_a, pltpu.roll(v, width - 32, 1), jnp.where(in_b, pltpu.roll(v, 32, 1), 0.0))


def _mla_qkv_fwd(proj, gq, gkv, wq, wkv, tk_c, tk_s, *, layer, name):
    s = proj.shape[0]
    tm = min(256, s)

    def body(cq_ref, ckv_ref, kr_ref, gq_ref, gkv_ref, wq_ref, wkv_ref, kc_ref, ks_ref,
             qcat_ref, kcat_ref, v_ref, cqn_ref, ckvn_ref):
        cq = cq_ref[...]
        cqn = (cq * lax.rsqrt(jnp.mean(cq * cq, axis=-1, keepdims=True) + EPS) * gq_ref[...]).astype(MXU_DTYPE)
        cqn_ref[...] = cqn
        qpre = _dot(cqn, wq_ref[...], 1, 1)
        kc, ks = kc_ref[...], ks_ref[...]
        for hh in range(MLA_HEADS):
            qcat_ref[:, hh * 256:hh * 256 + 128] = qpre[:, hh * 256:hh * 256 + 128].astype(MXU_DTYPE)
            blk = qpre[:, hh * 256 + 128:(hh + 1) * 256]
            qcat_ref[:, hh * 256 + 128:(hh + 1) * 256] = (blk * kc + _rope_partner(blk, 0, 128) * ks).astype(MXU_DTYPE)
        ckv = ckv_ref[...]
        ckvn = (ckv * lax.rsqrt(jnp.mean(ckv * ckv, axis=-1, keepdims=True) + EPS) * gkv_ref[...]).astype(MXU_DTYPE)
        ckvn_ref[...] = ckvn
        kv = _dot(ckvn, wkv_ref[...], 1, 1)
        kr = kr_ref[...]
        krr = (kr * kc + _rope_partner(kr, 0, 128) * ks).astype(MXU_DTYPE)
        for hh in range(MLA_HEADS):
            kcat_ref[:, hh * 256:hh * 256 + 128] = kv[:, hh * 128:(hh + 1) * 128].astype(MXU_DTYPE)
            kcat_ref[:, hh * 256 + 128:(hh + 1) * 256] = krr
            v_ref[:, hh * 256:hh * 256 + 128] = kv[:, 1024 + hh * 128:1024 + (hh + 1) * 128].astype(MXU_DTYPE)
            v_ref[:, hh * 256 + 128:(hh + 1) * 256] = jnp.ones((tm, 128), MXU_DTYPE)

    row = lambda w, c: pl.BlockSpec((tm, w), lambda i: (i, c))
    full = lambda a: pl.BlockSpec(a.shape, lambda i: (0, 0))
    of_layer = lambda a: _layer_spec(a.shape[-2:], lambda i: (0, 0), layer)
    return pl.pallas_call(
        body, name=name, grid=(s // tm,),
        in_specs=[row(Q_RANK, CQ_OFF // Q_RANK), row(KV_RANK, CKV_OFF // KV_RANK), row(128, KR_OFF // 128),
                  full(gq), full(gkv), of_layer(wq), of_layer(wkv), row(128, 0), row(128, 0)],
        out_specs=[row(2048, 0), row(2048, 0), row(2048, 0), row(Q_RANK, 0), row(KV_RANK, 0)],
        out_shape=[jax.ShapeDtypeStruct((s, 2048), MXU_DTYPE), jax.ShapeDtypeStruct((s, 2048), MXU_DTYPE),
                   jax.ShapeDtypeStruct((s, 2048), MXU_DTYPE), jax.ShapeDtypeStruct((s, Q_RANK), MXU_DTYPE),
                   jax.ShapeDtypeStruct((s, KV_RANK), MXU_DTYPE)],
        compiler_params=_cparams("parallel"),
    )(proj, proj, proj, gq, gkv, wq, wkv, tk_c, tk_s)


def _norm_bwd(x, g, dy):
    r = lax.rsqrt(jnp.mean(x * x, axis=-1, keepdims=True) + EPS)
    xn = x * r
    u = dy * g
    return r * (u - xn * jnp.mean(u * xn, axis=-1, keepdims=True)), jnp.sum(dy * xn, axis=0, keepdims=True)


def _mla_qkv_bwd(proj, cqn, ckvn, dqcat, dkcat, dv, dka, dva, gq, gkv, wq, wkv, tk_c, tk_s, *, layer, name):
    s = proj.shape[0]
    tm = min(256, s)
    t_cq, t_ka, t_ckv, t_va, t_kr = (o - CQ_OFF for o in (CQ_OFF, KA_OFF, CKV_OFF, VA_OFF, KR_OFF))

    def body(cq_ref, ckv_ref, cqn_ref, ckvn_ref, dq_ref, dk_ref, dv_ref, dka_ref, dva_ref, gq_ref, gkv_ref, wq_ref,
             wkv_ref, kc_ref, ks_ref,
             tile_ref, dwq_ref, dwkv_ref, dgq_ref, dgkv_ref, dqpre, dkv, dwq_acc, dwkv_acc):
        dcq_ref = tile_ref.at[:, t_cq:t_cq + Q_RANK]
        dckv_ref = tile_ref.at[:, t_ckv:t_ckv + KV_RANK]
        dkr_ref = tile_ref.at[:, t_kr:t_kr + 128]
        tile_ref[:, t_ka:t_ka + 128] = dka_ref[...]
        tile_ref[:, t_va:t_va + 128] = dva_ref[...]

        @pl.when(pl.program_id(0) == 0)
        def _():
            for r in (dwq_acc, dwkv_acc, dgq_ref, dgkv_ref):
                r[...] = jnp.zeros_like(r)

        kc, ks = kc_ref[...], ks_ref[...]
        dkrr = jnp.zeros((tm, 128), F32)
        for hh in range(MLA_HEADS):
            dqpre[:, hh * 256:hh * 256 + 128] = dq_ref[:, hh * 256:hh * 256 + 128].astype(MXU_DTYPE)
            blk = dq_ref[:, hh * 256 + 128:(hh + 1) * 256]
            dqpre[:, hh * 256 + 128:(hh + 1) * 256] = (blk * kc + _rope_partner(blk * ks, 0, 128)).astype(MXU_DTYPE)
            dkv[:, hh * 128:(hh + 1) * 128] = dk_ref[:, hh * 256:hh * 256 + 128].astype(MXU_DTYPE)
            dkrr = dkrr + dk_ref[:, hh * 256 + 128:(hh + 1) * 256]
        dkv[:, 1024:] = dv_ref[...].astype(MXU_DTYPE)
        dkr_ref[...] = (dkrr * kc + _rope_partner(dkrr * ks, 0, 128)).astype(dkr_ref.dtype)

        dq_b = dqpre[...]
        dwq_acc[...] += _dot(cqn_ref[...], dq_b, 0, 0)
        dcq, dgq = _norm_bwd(cq_ref[...], gq_ref[...], _dot(dq_b, wq_ref[...], 1, 0))
        dcq_ref[...] = dcq.astype(dcq_ref.dtype)
        dgq_ref[...] += dgq

        dkv_b = dkv[...]
        dwkv_acc[...] += _dot(ckvn_ref[...], dkv_b, 0, 0)
        dckv, dgkv = _norm_bwd(ckv_ref[...], gkv_ref[...], _dot(dkv_b, wkv_ref[...], 1, 0))
        dckv_ref[...] = dckv.astype(dckv_ref.dtype)
        dgkv_ref[...] += dgkv

        @pl.when(pl.program_id(0) == s // tm - 1)
        def _():
            dwq_ref[...] = dwq_acc[...].T
            dwkv_ref[...] = dwkv_acc[...].T

    row = lambda w, c: pl.BlockSpec((tm, w), lambda i: (i, c))
    full = lambda shape: pl.BlockSpec(shape, lambda i: (0, 0))
    of_layer = lambda a: _layer_spec(a.shape[-2:], lambda i: (0, 0), layer)
    return pl.pallas_call(
        body, name=name, grid=(s // tm,),
        in_specs=[row(Q_RANK, CQ_OFF // Q_RANK), row(KV_RANK, CKV_OFF // KV_RANK), row(Q_RANK, 0), row(KV_RANK, 0),
                  row(2048, 0), row(2048, 0), row(1024, 0), row(128, 0), row(128, 0), full(gq.shape), full(gkv.shape),
                  of_layer(wq), of_layer(wkv), row(128, 0), row(128, 0)],
        out_specs=[row(1024, 0), full(wq.shape[-2:]), full(wkv.shape[-2:]), full(gq.shape), full(gkv.shape)],
        out_shape=[jax.ShapeDtypeStruct((s, 1024), MXU_DTYPE), jax.ShapeDtypeStruct(wq.shape[-2:], F32),
                   jax.ShapeDtypeStruct(wkv.shape[-2:], F32), jax.ShapeDtypeStruct(gq.shape, F32),
                   jax.ShapeDtypeStruct(gkv.shape, F32)],
        scratch_shapes=[pltpu.VMEM((tm, 2048), MXU_DTYPE), pltpu.VMEM((tm, 2048), MXU_DTYPE),
                        pltpu.VMEM((Q_RANK, 2048), F32), pltpu.VMEM((KV_RANK, 2048), F32)],
        compiler_params=_cparams("arbitrary"),
    )(proj, proj, cqn, ckvn, dqcat, dkcat, dv, dka, dva, gq, gkv, wq, wkv, tk_c, tk_s)


def _causal_mask(t):
    return lax.broadcasted_iota(jnp.int32, (t, t), 1) <= lax.broadcasted_iota(jnp.int32, (t, t), 0)


def _mla_fwd(qcat, kcat, v, *, name, gather=None):
    s = qcat.shape[0]
    t = min(512, s)
    nq = s // t
    hp = HEADS_PER_STEP
    ng = MLA_HEADS // hp
    c2 = MLA_SCALE * LOG2E

    nw = 0 if gather is None else len(gather)

    def body(q_ref, k_ref, v_ref, *rest):
        src, (o_ref, lse_ref), out = rest[:nw], rest[nw:nw + 2], rest[nw + 2:2 * nw + 2]
        top_s, acc_s, *sems = rest[2 * nw + 2:]
        g, qi = pl.program_id(0), pl.program_id(1)
        if nw:
            @pl.when((g == 0) & (qi == 0))
            def _():
                _gather_start(src, out, *sems)

            @pl.when((g == ng - 1) & (qi == 0))
            def _():
                _gather_forward(src, out, *sems)

        rows = lambda j: pl.ds(pl.multiple_of(j * t, t), t)
        head = lambda e: slice(e * 256, (e + 1) * 256)
        raw = lambda e, j: _dot(q_ref[:, head(e)], k_ref[rows(j), head(e)], 1, 1)

        for e in range(hp):
            top_s[e] = jnp.where(_causal_mask(t), raw(e, qi), NEG)

        def pass1(j, carry):
            for e in range(hp):
                top_s[e] = jnp.maximum(top_s[e], raw(e, j))
            return carry

        lax.fori_loop(0, qi, pass1, 0)
        m = [jnp.max(top_s[e], axis=-1, keepdims=True) * c2 for e in range(hp)]

        def weighted(e, j, masked):
            sc = raw(e, j) * c2 - m[e]
            if masked:
                sc = jnp.where(_causal_mask(t), sc, NEG)
            return jnp.dot(jnp.exp2(sc).astype(MXU_DTYPE), v_ref[rows(j), head(e)], preferred_element_type=F32)

        for e in range(hp):
            acc_s[e] = weighted(e, qi, True)

        def pass2(j, carry):
            for e in range(hp):
                acc_s[e] += weighted(e, j, False)
            return carry

        lax.fori_loop(0, qi, pass2, 0)
        lane = lax.broadcasted_iota(jnp.int32, (t, 128), 1)
        stats = jnp.zeros((t, 128), F32)
        for e in range(hp):
            l = acc_s[e, :, 128:]
            o_ref[:, e * 128:(e + 1) * 128] = acc_s[e, :, :128] / l
            stats = jnp.where(lane == e, m[e] + jnp.log(l) * LOG2E, stats)
        lse_ref[...] = stats
        if nw:
            @pl.when((g == ng - 1) & (qi == nq - 1))
            def _():
                _gather_finish(src, out, *sems)

    outs = pl.pallas_call(
        body, name=name, grid=(ng, nq),
        in_specs=[pl.BlockSpec((t, 256 * hp), lambda g, qi: (qi, g)), pl.BlockSpec((s, 256 * hp), lambda g, qi: (0, g)),
                  pl.BlockSpec((s, 256 * hp), lambda g, qi: (0, g))] + [ANY] * nw,
        out_specs=[pl.BlockSpec((t, 128 * hp), lambda g, qi: (qi, g)), pl.BlockSpec((t, 128), lambda g, qi: (qi, g))]
        + [ANY] * nw,
        out_shape=[jax.ShapeDtypeStruct((s, 1024), F32), jax.ShapeDtypeStruct((s, 128 * ng), F32)]
        + (_gathered_shapes(gather) if nw else []),
        scratch_shapes=[pltpu.VMEM((hp, t, t), F32), pltpu.VMEM((hp, t, 256), F32)] + (_sems(6, nw) if nw else []),
        compiler_params=_cparams("arbitrary", "arbitrary"),
    )(qcat, kcat, v, *(gather or []))
    return outs[0], outs[1], list(outs[2:])


def _mla_bwd(qcat, kcat, v, do, lse, delta, *, name, scatter=None):
    s = qcat.shape[0]
    t = min(512, s)
    nq = s // t
    hp = HEADS_PER_STEP
    c2 = MLA_SCALE * LOG2E
    nw = 0 if scatter is None else len(scatter)

    def body(q_ref, k_ref, v_ref, do_ref, lse_ref, dl_ref, *rest):
        src, (dq_ref, dk_ref, dv_ref), out = rest[:nw], rest[nw:nw + 3], rest[nw + 3:2 * nw + 3]
        dk_acc, dv_acc, *sems = rest[2 * nw + 3:]
        h, ki = pl.program_id(0), pl.program_id(1)
        if nw:
            @pl.when((h == 0) & (ki == 0))
            def _():
                _scatter_start(src, out, *sems)

        @pl.when(ki == 0)
        def _():
            dq_ref[...] = jnp.zeros_like(dq_ref)

        dk_acc[...] = jnp.zeros_like(dk_acc)
        dv_acc[...] = jnp.zeros_like(dv_acc)
        k, vv = k_ref[...], v_ref[...]
        mine = lax.broadcasted_iota(jnp.int32, (t, 128), 1) == h % hp

        def chunk(qi, masked):
            rows = pl.ds(pl.multiple_of(qi * t, t), t)
            q, dob = q_ref[rows, :], do_ref[rows, :]
            pick = lambda r: jnp.sum(jnp.where(mine, r[rows, :], 0.0), axis=-1, keepdims=True)
            sc = _dot(q, k, 1, 1) * c2
            if masked:
                sc = jnp.where(_causal_mask(t), sc, NEG)
            p = jnp.exp2(sc - pick(lse_ref))
            dp = _dot(dob, vv, 1, 1)
            ds = (p * (dp - pick(dl_ref)) * MLA_SCALE).astype(MXU_DTYPE)
            dv_acc[...] += _dot(dob, p.astype(MXU_DTYPE), 0, 0)
            dk_acc[...] += _dot(q, ds, 0, 0)
            dq_ref[rows, :] += jnp.dot(ds, k, preferred_element_type=F32)

        def unmasked(qi, carry):
            chunk(qi, False)
            return carry

        chunk(ki, True)
        lax.fori_loop(ki + 1, nq, unmasked, 0)
        dk_ref[...] = dk_acc[...].T
        dv_ref[...] = dv_acc[...].T
        if nw:
            @pl.when((h == MLA_HEADS - 1) & (ki == nq - 1))
            def _():
                _scatter_finish(src, out, *sems)

    head = lambda w: pl.BlockSpec((s, w), lambda h, ki: (0, h))
    blk = lambda w: pl.BlockSpec((t, w), lambda h, ki: (ki, h))
    stat = pl.BlockSpec((s, 128), lambda h, ki: (0, h // hp))
    outs = pl.pallas_call(
        body, name=name, grid=(MLA_HEADS, nq),
        in_specs=[head(256), blk(256), pl.BlockSpec((t, 128), lambda h, ki: (ki, 2 * h)), head(128), stat, stat]
        + [ANY] * nw,
        out_specs=[head(256), blk(256), blk(128)] + [ANY] * nw,
        out_shape=[jax.ShapeDtypeStruct((s, 2048), F32), jax.ShapeDtypeStruct((s, 2048), F32),
                   jax.ShapeDtypeStruct((s, 1024), F32)] + [jax.ShapeDtypeStruct(a.shape, a.dtype) for a in scatter or []],
        scratch_shapes=[pltpu.VMEM((256, t), F32), pltpu.VMEM((128, t), F32)] + (_sems(3, nw) if nw else []),
        compiler_params=_cparams("arbitrary", "arbitrary"),
    )(qcat, kcat, v, do, lse, delta, *(scatter or []))
    return outs[0], outs[1], outs[2], list(outs[3:])


def _gate_specs(tm):
    half = lambda c: pl.BlockSpec((tm, 1024), lambda i: (i, c))
    return half(0), half(GA_OFF // 1024), half(GB_OFF // 1024)


def _gate_fwd(oa, ob, proj, *, name):
    s = oa.shape[0]
    tm = min(512, s)

    def body(oa_ref, ob_ref, ga_ref, gb_ref, y_ref):
        ga, gb = ga_ref[...], gb_ref[...]
        y_ref[:, :1024] = (oa_ref[...] * (ga * jax.nn.sigmoid(ga))).astype(MXU_DTYPE)
        y_ref[:, 1024:] = (ob_ref[...] * (gb * jax.nn.sigmoid(gb))).astype(MXU_DTYPE)

    o_spec, ga_spec, gb_spec = _gate_specs(tm)
    return pl.pallas_call(
        body, name=name, grid=(s // tm,), in_specs=[o_spec, o_spec, ga_spec, gb_spec],
        out_specs=pl.BlockSpec((tm, 2048), lambda i: (i, 0)),
        out_shape=jax.ShapeDtypeStruct((s, 2048), MXU_DTYPE),
        compiler_params=_cparams("parallel"),
    )(oa, ob, proj, proj)


def _gate_bwd(dy, oa, ob, proj, *, name):
    s = oa.shape[0]
    tm = min(512, s)

    def body(dy_ref, oa_ref, ob_ref, ga_ref, gb_ref, doa_ref, dob_ref, dga_ref, dgb_ref, dl_ref):
        def branch(dyv, o, g, do_ref, dg_ref):
            sg = jax.nn.sigmoid(g)
            do = dyv * (g * sg)
            do_ref[...] = do.astype(MXU_DTYPE)
            dg_ref[...] = (dyv * o * (sg * (1.0 + g * (1.0 - sg)))).astype(MXU_DTYPE)
            return do

        branch(dy_ref[:, :1024], oa_ref[...], ga_ref[...], doa_ref, dga_ref)
        ob = ob_ref[...]
        prod = branch(dy_ref[:, 1024:], ob, gb_ref[...], dob_ref, dgb_ref) * ob
        lane = lax.broadcasted_iota(jnp.int32, (tm, stat_w), 1)
        acc = jnp.zeros((tm, stat_w), F32)
        for hh in range(MLA_HEADS):
            at = (hh // HEADS_PER_STEP) * 128 + hh % HEADS_PER_STEP
            acc = jnp.where(lane == at, jnp.sum(prod[:, hh * 128:(hh + 1) * 128], axis=-1, keepdims=True), acc)
        dl_ref[...] = acc

    stat_w = 128 * (MLA_HEADS // HEADS_PER_STEP)
    o_spec, ga_spec, gb_spec = _gate_specs(tm)
    return pl.pallas_call(
        body, name=name, grid=(s // tm,),
        in_specs=[pl.BlockSpec((tm, 2048), lambda i: (i, 0)), o_spec, o_spec, ga_spec, gb_spec],
        out_specs=[o_spec, o_spec, o_spec, o_spec, pl.BlockSpec((tm, stat_w), lambda i: (i, 0))],
        out_shape=[jax.ShapeDtypeStruct((s, 1024), MXU_DTYPE)] * 4 + [jax.ShapeDtypeStruct((s, stat_w), F32)],
        compiler_params=_cparams("parallel"),
    )(dy, oa, ob, proj, proj)


def _row_block(rows, cols, itemsize=4, budget=2 << 20):
    fits = [tr for tr in range(16, rows + 1, 16) if rows % tr == 0 and tr * cols * itemsize <= budget]
    return fits[-1] if fits else rows


def _adamw(w, g, m, v, *, name):
    shape = w.shape
    rows, cols = shape[-2:]
    w3, g3, m3, v3 = (a.reshape((-1, rows, cols)) for a in (w, g, m, v))
    lead = w3.shape[0]
    tr = _row_block(rows, cols)

    def body(w_ref, g_ref, m_ref, v_ref, d_ref, mo_ref, vo_ref):
        gv = g_ref[...]
        mn = ADAM_B1 * m_ref[...] + (1.0 - ADAM_B1) * gv
        vn = ADAM_B2 * v_ref[...] + (1.0 - ADAM_B2) * jnp.square(gv)
        m_hat = mn / (1.0 - ADAM_B1 ** ADAM_STEP)
        v_hat = vn / (1.0 - ADAM_B2 ** ADAM_STEP)
        d_ref[...] = -ADAM_LR * (m_hat / (jnp.sqrt(v_hat) + ADAM_EPS) + ADAM_WD * w_ref[...])
        mo_ref[...] = mn
        vo_ref[...] = vn

    spec = pl.BlockSpec((None, tr, cols), lambda a, i: (a, i, 0))
    outs = pl.pallas_call(
        body, name=name, grid=(lead, rows // tr), in_specs=[spec] * 4, out_specs=[spec] * 3,
        out_shape=[jax.ShapeDtypeStruct((lead, rows, cols), F32)] * 3,
        compiler_params=_cparams("parallel", "parallel"),
    )(w3, g3, m3, v3)
    return tuple(o.reshape(shape) for o in outs)


def _pair_sum(where, grads, recv, *, name):
    depth, chips, rows, cols = grads.shape
    hl = depth // 2
    tr = _row_block(rows, cols)

    def body(where_ref, a_ref, b_ref, o_ref):
        o_ref[...] = (a_ref[...] + b_ref[...].astype(F32)).astype(WIRE_DTYPE)

    spec = pl.BlockSpec((None, None, tr, cols), lambda a, k, i, w: (a, k, i, 0))
    return pl.pallas_call(
        body, name=name,
        grid_spec=pltpu.PrefetchScalarGridSpec(
            num_scalar_prefetch=1, grid=(hl, chips, rows // tr),
            in_specs=[pl.BlockSpec((None, None, tr, cols), lambda a, k, i, w: (hl * w[4] + a, k, i, 0)), spec],
            out_specs=spec),
        out_shape=jax.ShapeDtypeStruct((hl, chips, rows, cols), WIRE_DTYPE),
        compiler_params=_cparams("parallel", "parallel", "parallel"),
    )(where, grads, recv)


def _chip_sum(where, grads, recv, parts, *, name):
    depth, _, rows, cols = grads.shape
    hl = depth // 2
    tr = _row_block(rows, cols)

    def body(where_ref, a_ref, b_ref, t0_ref, t1_ref, t2_ref, o_ref):
        total = a_ref[...] + b_ref[...].astype(F32)
        for t_ref in (t0_ref, t1_ref, t2_ref):
            total = total + t_ref[...].astype(F32)
        o_ref[...] = total

    slot = lambda j: pl.BlockSpec((None, None, tr, cols), lambda a, i, w: (a, w[j], i, 0))
    return pl.pallas_call(
        body, name=name,
        grid_spec=pltpu.PrefetchScalarGridSpec(
            num_scalar_prefetch=1, grid=(hl, rows // tr),
            in_specs=[pl.BlockSpec((None, None, tr, cols), lambda a, i, w: (hl * w[4] + a, w[0], i, 0)), slot(0), slot(1),
                      slot(2), slot(3)],
            out_specs=pl.BlockSpec((None, tr, cols), lambda a, i, w: (hl * w[4] + a, i, 0))),
        out_shape=jax.ShapeDtypeStruct((depth, rows, cols), F32),
        compiler_params=_cparams("parallel", "parallel"),
    )(where, grads, recv, parts, parts, parts)


def _place():
    x, y, c = lax.axis_index("x"), lax.axis_index("y"), lax.axis_index("c")
    chips = [(1 - x, y), (x, 1 - y), (1 - x, 1 - y)]
    return x, y, c, chips


def _sems(*shape):
    return [pltpu.SemaphoreType.DMA(shape), pltpu.SemaphoreType.DMA(shape)]


OWNER_CORE = (0, 1, 1, 1)


def _gather_copy(src, out, send_sems, recv_sems, t, sem, slot, to, forward=False):
    n = src[t].shape[0]
    rows = out[t].at[pl.ds(pl.multiple_of(slot * n, 16), n)]
    return pltpu.make_async_remote_copy(src_ref=rows if forward else src[t], dst_ref=rows, send_sem=send_sems.at[sem, t],
                                        recv_sem=recv_sems.at[sem, t], device_id=to, device_id_type=MESH)


def _gather_start(src, out, send_sems, recv_sems):
    x, y, c, chips = _place()
    for t, owner in enumerate(OWNER_CORE):
        @pl.when(c == owner)
        def _():
            for j, chip in enumerate(chips):
                _gather_copy(src, out, send_sems, recv_sems, t, j, 2 * x + y, (*chip, c)).start()


def _gather_forward(src, out, send_sems, recv_sems):
    x, y, c, chips = _place()
    for t, owner in enumerate(OWNER_CORE):
        @pl.when(c == owner)
        def _():
            for j, (px, py) in enumerate(chips):
                _gather_copy(src, out, send_sems, recv_sems, t, j, 2 * px + py, (px, py, c)).wait_recv()
                _gather_copy(src, out, send_sems, recv_sems, t, 3 + j, 2 * px + py, (x, y, 1 - c), forward=True).start()


def _gather_finish(src, out, send_sems, recv_sems):
    x, y, c, chips = _place()
    slots = [2 * px + py for px, py in chips]
    for t, owner in enumerate(OWNER_CORE):
        copy = functools.partial(_gather_copy, src, out, send_sems, recv_sems, t)

        @pl.when(c == owner)
        def _():
            for j, (px, py) in enumerate(chips):
                copy(j, 2 * x + y, (px, py, c)).wait_send()
                copy(3 + j, slots[j], (x, y, 1 - c), forward=True).wait_send()

        @pl.when(c != owner)
        def _():
            for j in range(3):
                copy(3 + j, slots[j], (x, y, 1 - c), forward=True).wait_recv()


def _gathered_shapes(shards):
    return [jax.ShapeDtypeStruct((4 * a.shape[0], a.shape[1]), a.dtype) for a in shards]


def _comm_gather_layer(shards, *, name):
    nt = len(shards)

    def body(*refs):
        src, out, sems = refs[:nt], refs[nt:2 * nt], refs[2 * nt:]
        _gather_start(src, out, *sems)
        _gather_forward(src, out, *sems)
        _gather_finish(src, out, *sems)

    return pl.pallas_call(
        body, name=name, in_specs=[ANY] * nt, out_specs=[ANY] * nt, out_shape=_gathered_shapes(shards),
        scratch_shapes=_sems(6, nt),
    )(*shards)


def _comm_swap_sibling(bufs, *, name):
    nt = len(bufs)

    def body(*refs):
        src, out, (send_sems, recv_sems) = refs[:nt], refs[nt:2 * nt], refs[2 * nt:]
        x, y, c, _ = _place()
        cps = [pltpu.make_async_remote_copy(src_ref=src[t], dst_ref=out[t], send_sem=send_sems.at[t], recv_sem=recv_sems.at[t],
                                            device_id=(x, y, 1 - c), device_id_type=MESH) for t in range(nt)]
        for cp in cps:
            cp.start()
        for cp in cps:
            cp.wait()

    return pl.pallas_call(
        body, name=name, in_specs=[ANY] * nt, out_specs=[ANY] * nt,
        out_shape=[jax.ShapeDtypeStruct(a.shape, a.dtype) for a in bufs], scratch_shapes=_sems(nt),
    )(*bufs)


def _scatter_copy(src, out, send_sems, recv_sems, j, t, from_slot, to_slot, to):
    return pltpu.make_async_remote_copy(src_ref=src[t].at[:, from_slot], dst_ref=out[t].at[:, to_slot],
                                        send_sem=send_sems.at[j, t], recv_sem=recv_sems.at[j, t], device_id=to,
                                        device_id_type=MESH)


def _scatter_start(src, out, send_sems, recv_sems):
    x, y, c, chips = _place()
    for j, (px, py) in enumerate(chips):
        for t in range(len(src)):
            _scatter_copy(src, out, send_sems, recv_sems, j, t, 2 * px + py, 2 * x + y, (px, py, c)).start()


def _scatter_finish(src, out, send_sems, recv_sems):
    x, y, c, chips = _place()
    for j, (px, py) in enumerate(chips):
        for t in range(len(src)):
            _scatter_copy(src, out, send_sems, recv_sems, j, t, 2 * x + y, 2 * px + py, (px, py, c)).wait_recv()
    for j, (px, py) in enumerate(chips):
        for t in range(len(src)):
            _scatter_copy(src, out, send_sems, recv_sems, j, t, 2 * px + py, 2 * x + y, (px, py, c)).wait_send()


def _comm_scatter_chips(parts, *, name):
    nt = len(parts)

    def body(*refs):
        src, out, sems = refs[:nt], refs[nt:2 * nt], refs[2 * nt:]
        _scatter_start(src, out, *sems)
        _scatter_finish(src, out, *sems)

    return pl.pallas_call(
        body, name=name, in_specs=[ANY] * nt, out_specs=[ANY] * nt,
        out_shape=[jax.ShapeDtypeStruct(a.shape, a.dtype) for a in parts], scratch_shapes=_sems(3, nt),
    )(*parts)


def _comm_join_halves(bufs, *, name):
    nt = len(bufs)
    hl = bufs[0].shape[0] // 2

    def body(*refs):
        src, out, (send_sems, recv_sems) = refs[:nt], refs[nt:2 * nt], refs[2 * nt:]
        x, y, c, _ = _place()
        mine = pl.ds(c * hl, hl)
        other = pl.ds((1 - c) * hl, hl)
        copy = lambda t, layers: pltpu.make_async_remote_copy(
            src_ref=src[t].at[mine], dst_ref=out[t].at[layers], send_sem=send_sems.at[t], recv_sem=recv_sems.at[t],
            device_id=(x, y, 1 - c), device_id_type=MESH)
        sends = [copy(t, mine) for t in range(nt)]
        for cp in sends:
            cp.start()
        for t in range(nt):
            copy(t, other).wait_recv()
        for cp in sends:
            cp.wait_send()

    return pl.pallas_call(
        body, name=name, in_specs=[ANY] * nt, out_specs=[ANY] * nt,
        out_shape=[jax.ShapeDtypeStruct(a.shape, a.dtype) for a in bufs],
        input_output_aliases={t: t for t in range(nt)}, scratch_shapes=_sems(nt),
    )(*bufs)


def _comm_allreduce_small(part, *, name):
    rows, cols = part.shape

    def body(p_ref, o_ref, buf, send_sems, recv_sems):
        x, y, c, _ = _place()
        me = 4 * x + 2 * y + c
        buf[me] = p_ref[...]
        flip = lambda v, bit: 1 - v if bit else v
        peers = [(flip(x, d & 4), flip(y, d & 2), flip(c, d & 1)) for d in range(1, 8)]
        sends = []
        for j, peer in enumerate(peers):
            cp = pltpu.make_async_remote_copy(src_ref=buf.at[me], dst_ref=buf.at[me], send_sem=send_sems.at[j],
                                              recv_sem=recv_sems.at[j], device_id=peer, device_id_type=MESH)
            cp.start()
            sends.append(cp)
        for j, (px, py, pc) in enumerate(peers):
            pltpu.make_async_remote_copy(src_ref=buf.at[me], dst_ref=buf.at[4 * px + 2 * py + pc], send_sem=send_sems.at[j],
                                         recv_sem=recv_sems.at[j], device_id=(px, py, pc), device_id_type=MESH).wait_recv()
        for cp in sends:
            cp.wait_send()
        total = buf[0]
        for i in range(1, 8):
            total = total + buf[i]
        o_ref[...] = total

    vm = pl.BlockSpec(memory_space=pltpu.VMEM)
    return pl.pallas_call(
        body, name=name, in_specs=[vm], out_specs=vm, out_shape=jax.ShapeDtypeStruct((rows, cols), F32),
        scratch_shapes=[pltpu.VMEM((8, rows, cols), F32), pltpu.SemaphoreType.DMA((7,)), pltpu.SemaphoreType.DMA((7,))],
    )(part)


def _pad_in_rows(wt):
    r = lambda o, n: wt[..., o:o + n, :]
    kr = r(2944, 64)
    return jnp.concatenate([r(0, 1024), r(1280, 1024), r(3008, 1024), r(2304, Q_RANK), r(1024, 128), r(2688, KV_RANK),
                            r(1152, 128), kr, jnp.zeros_like(kr)], axis=-2)


def _unpad_in_rows(qa, ga, gb, mixed):
    cq, ka, ckv, va, kr = (mixed[..., o - CQ_OFF:o - CQ_OFF + n, :] for o, n in (
        (CQ_OFF, Q_RANK), (KA_OFF, 128), (CKV_OFF, KV_RANK), (VA_OFF, 128), (KR_OFF, 64)))
    return jnp.concatenate([qa, ka, va, ga, cq, ckv, kr, gb], axis=-2)


def _pad_q_rows(wt):
    lead, cols = wt.shape[:-2], wt.shape[-1]
    wt = jnp.pad(wt.reshape(lead + (MLA_HEADS, 192, cols)), [(0, 0)] * (len(lead) + 1) + [(0, 64), (0, 0)])
    return wt.reshape(lead + (MLA_HEADS * 256, cols))


def _unpad_q_rows(wt):
    lead, cols = wt.shape[:-2], wt.shape[-1]
    return wt.reshape(lead + (MLA_HEADS, 256, cols))[..., :192, :].reshape(lead + (MLA_HEADS * 192, cols))


def _perm_kv_rows(wt):
    lead, cols = wt.shape[:-2], wt.shape[-1]
    return jnp.swapaxes(wt.reshape(lead + (MLA_HEADS, 2, 128, cols)), -4, -3).reshape(lead + (2048, cols))


def _unperm_kv_rows(wt):
    lead, cols = wt.shape[:-2], wt.shape[-1]
    return jnp.swapaxes(wt.reshape(lead + (2, MLA_HEADS, 128, cols)), -4, -3).reshape(lead + (2048, cols))


def _t(a):
    return jnp.swapaxes(a, -1, -2)


def _rope_tables(s):
    pos = jnp.arange(s, dtype=F32)
    inv_freq = 10000.0 ** (-jnp.arange(0, 64, 2, dtype=F32) / 64)
    ang = pos[:, None] * inv_freq[None, :]
    cos, sin = jnp.cos(ang), jnp.sin(ang)
    z64 = jnp.zeros((s, 64), F32)
    tk_c = jnp.concatenate([cos, cos, z64], axis=-1)
    tk_s = jnp.concatenate([-sin, sin, z64], axis=-1)
    return tk_c, tk_s


def _layer_weights(gathered, own, chip):
    def with_own_rows(g, o):
        n = o.shape[0]
        return jnp.concatenate([lax.select(chip == j, o, g[j * n:(j + 1) * n]) for j in range(4)], axis=0)

    full_in, full_q, full_kv, full_o = (with_own_rows(g, o) for g, o in zip(gathered, own))
    return _pad_in_rows(full_in), _pad_q_rows(full_q), _perm_kv_rows(full_kv), full_o


def _device_step(xs, tgt, attn_g, sinks, gq, gkv, final_g, shards, where):
    chip = where[0]
    depth = shards[0].shape[0]
    s = xs.shape[0]
    tabs = _rope_tables(s)
    saved = []
    x = xs
    of_layer = lambda l: [a[l] for a in shards]
    gathered = _comm_gather_layer(of_layer(0), name="comm_gather_layer0")
    weights = []
    for l in range(depth):
        w_in_p, w_q_p, w_kv_p, w_o = _layer_weights(gathered, of_layer(l), chip)
        weights.append((w_in_p, w_q_p, w_kv_p, w_o))
        h = _rmsnorm_fwd(x, attn_g[l:l + 1], name=f"norm_fwd{l}")
        proj = _matmul(h, w_in_p, tb=True, name=f"in_proj{l}")
        oa, lse_a = _swa_fwd(proj, sinks[l], name=f"swa_fwd{l}")
        qcat, kcat, v, cqn, ckvn = _mla_qkv_fwd(proj, gq[l:l + 1], gkv[l:l + 1], w_q_p, w_kv_p, *tabs, layer=None,
                                                name=f"mla_qkv_fwd{l}")
        ob, lse_b, gathered = _mla_fwd(qcat, kcat, v, gather=of_layer(l + 1) if l + 1 < depth else None,
                                       name=f"mla_fwd{l}")
        y = _gate_fwd(oa, ob, proj, name=f"gate_fwd{l}")
        x_next = _matmul(y, w_o, add=x, name=f"out_proj{l}")
        saved.append((x, h, proj, oa, lse_a, qcat, kcat, v, cqn, ckvn, ob, lse_b, y))
        x = x_next

    dx, d_final_g, loss = _final_loss(x, final_g, tgt, name="final_loss")

    d_attn_g, d_sinks, d_gq, d_gkv = [None] * depth, [None] * depth, [None] * depth, [None] * depth
    d_w_in, d_w_q, d_w_kv, d_w_o = [None] * depth, [None] * depth, [None] * depth, [None] * depth
    half = depth // 2

    def grads_by_chip(layers):
        stack = lambda per_layer: jnp.stack([per_layer[l] for l in layers])
        in_tiles = [stack([tiles[j] if tiles else None for tiles in d_w_in]) for j in range(4)]
        full = [_unpad_in_rows(*in_tiles), _unpad_q_rows(stack(d_w_q)), _unperm_kv_rows(stack(d_w_kv)), stack(d_w_o)]
        return [g.reshape(g.shape[0], 4, g.shape[1] // 4, g.shape[2]) for g in full]

    def reduce_begin(grads4, tag):
        hl = grads4[0].shape[0] // 2
        give = [lax.dynamic_slice_in_dim(g, (1 - where[4]) * hl, hl, axis=0).astype(WIRE_DTYPE) for g in grads4]
        recv = _comm_swap_sibling(give, name=f"comm_swap_sibling_{tag}")
        return recv, [_pair_sum(where, g, r, name=f"pair_sum_{tag}{t}") for t, (g, r) in enumerate(zip(grads4, recv))]

    def reduce_end(grads4, recv, parts, tag):
        mine = [_chip_sum(where, g, r, p, name=f"chip_sum_{tag}{t}") for t, (g, r, p) in enumerate(zip(grads4, recv, parts))]
        return _comm_join_halves(mine, name=f"comm_join_halves_{tag}")

    for l in reversed(range(depth)):
        x, h, proj, oa, lse_a, qcat, kcat, v, cqn, ckvn, ob, lse_b, y = saved[l]
        w_in_p, w_q_p, w_kv_p, w_o = weights[l]
        dy = _matmul(dx, w_o, tb=True, name=f"out_proj_dx{l}")
        d_w_o[l] = _matmul(y, dx, ta=True, name=f"out_proj_dw{l}")
        doa, dob, dga, dgb, delta_b = _gate_bwd(dy, oa, ob, proj, name=f"gate_bwd{l}")
        dqa, dka, dva, dsink = _swa_bwd(proj, sinks[l], lse_a, doa, name=f"swa_bwd{l}")
        pair_hi = None
        if l == half - 1:
            grads_hi = grads_by_chip(range(half, depth))
            recv_hi, pair_hi = reduce_begin(grads_hi, "hi")
        dqc, dkc, dv, parts = _mla_bwd(qcat, kcat, v, dob, lse_b, delta_b, scatter=pair_hi, name=f"mla_bwd{l}")
        if l == half - 1:
            reduced_hi = reduce_end(grads_hi, recv_hi, parts, "hi")
        mixed, d_w_q[l], d_w_kv[l], dgq_l, dgkv_l = _mla_qkv_bwd(
            proj, cqn, ckvn, dqc, dkc, dv, dka, dva, gq[l:l + 1], gkv[l:l + 1], w_q_p, w_kv_p, *tabs, layer=None,
            name=f"mla_qkv_bwd{l}")
        dproj = [dqa, dga, dgb, mixed]
        dh = _matmul_ktiles(dproj, w_in_p, name=f"in_proj_dx{l}")
        d_w_in[l] = [_matmul(tile, h, ta=True, name=f"in_proj_dw{l}_{j}") for j, tile in enumerate(dproj)]
        dx, dg_l = _rmsnorm_bwd(dh, x, attn_g[l:l + 1], dx, name=f"norm_bwd{l}")
        d_attn_g[l], d_sinks[l], d_gq[l], d_gkv[l] = dg_l, dsink[0:1, :SWA_HEADS], dgq_l, dgkv_l

    grads_lo = grads_by_chip(range(half))
    recv_lo, pair_lo = reduce_begin(grads_lo, "lo")
    reduced_lo = reduce_end(grads_lo, recv_lo, _comm_scatter_chips(pair_lo, name="comm_scatter_chips_lo"), "lo")
    cat = lambda parts: jnp.concatenate(parts, axis=0)
    reduced = [cat([lo, hi]) for lo, hi in zip(reduced_lo, reduced_hi)]
    return loss, dx, cat(d_attn_g), cat(d_sinks), cat(d_gq), cat(d_gkv), d_final_g, reduced


def kernel(x, attn_norm_g, w_in, swa_sinks, q_a_norm_g, kv_a_norm_g, w_q_b, w_kv_b, w_out, final_norm_g, loss_target, m_attn_norm_g, m_w_in, m_swa_sinks, m_q_a_norm_g, m_kv_a_norm_g, m_w_q_b, m_w_kv_b, m_w_out, m_final_norm_g, v_attn_norm_g, v_w_in, v_swa_sinks, v_q_a_norm_g, v_kv_a_norm_g, v_w_q_b, v_w_kv_b, v_w_out, v_final_norm_g):
    x_, y_, c = lax.axis_index("x"), lax.axis_index("y"), lax.axis_index("c")
    chip = 2 * x_ + y_
    where = jnp.stack([chip, 2 * (1 - x_) + y_, 2 * x_ + 1 - y_, 2 * (1 - x_) + 1 - y_, c]).astype(jnp.int32)

    sent = [a.astype(WIRE_DTYPE) for a in (_t(w_in), _t(w_q_b), _t(w_kv_b), w_out)]

    loss, dx, d_attn_g, d_sinks, d_gq, d_gkv, d_final_g, reduced = _device_step(
        x[0], loss_target[0], attn_norm_g, swa_sinks, q_a_norm_g, kv_a_norm_g, final_norm_g.reshape(1, -1), sent, where)
    g_w_in, g_w_q_b, g_w_kv_b, g_w_out = _t(reduced[0]), _t(reduced[1]), _t(reduced[2]), reduced[3]

    small = [d_attn_g, d_sinks, d_gq, d_gkv, d_final_g, loss[:, :1]]
    flat = jnp.concatenate([a.reshape(-1) for a in small])
    n_small = flat.shape[0]
    rows = -(-n_small // 1024) * 8
    total = _comm_allreduce_small(jnp.pad(flat, (0, rows * 128 - n_small)).reshape(rows, 128),
                                  name="comm_allreduce_small").reshape(-1)
    outs, at = [], 0
    for a in small:
        outs.append(total[at:at + a.size].reshape(a.shape))
        at += a.size
    g_attn_g, g_sinks, g_gq, g_gkv, g_final_g, loss_total = outs
    g_final_g = g_final_g.reshape(final_norm_g.shape)

    weights = [attn_norm_g, w_in, swa_sinks, q_a_norm_g, kv_a_norm_g, w_q_b, w_kv_b, w_out, final_norm_g]
    grads = [g_attn_g, g_w_in, g_sinks, g_gq, g_gkv, g_w_q_b, g_w_kv_b, g_w_out, g_final_g]
    ms = [m_attn_norm_g, m_w_in, m_swa_sinks, m_q_a_norm_g, m_kv_a_norm_g, m_w_q_b, m_w_kv_b, m_w_out, m_final_norm_g]
    vs = [v_attn_norm_g, v_w_in, v_swa_sinks, v_q_a_norm_g, v_kv_a_norm_g, v_w_q_b, v_w_kv_b, v_w_out, v_final_norm_g]
    as2d = lambda a: a.reshape(1, -1) if a.ndim == 1 else a
    deltas, new_m, new_v = [], [], []
    for i, (w, g, m, v) in enumerate(zip(weights, grads, ms, vs)):
        view = _t if w is w_in else as2d
        d, mn, vn = _adamw(view(w), reduced[0] if w is w_in else view(g), view(m), view(v), name=f"adamw{i}")
        back = _t if w is w_in else (lambda a: a.reshape(w.shape))
        deltas.append(back(d))
        new_m.append(back(mn))
        new_v.append(back(vn))

    return (loss_total.reshape(()), dx[None], *grads, *deltas, *new_m, *new_v)
```

```python
import functools
import math

import jax
import jax.numpy as jnp
from jax import lax
from jax.experimental import pallas as pl
from jax.experimental.pallas import tpu as pltpu

F32 = jnp.float32
MXU_DTYPE = jnp.bfloat16
WIRE_DTYPE = jnp.bfloat16

EPS = 1e-6
NEG = -1e30
BLOCK = 128
D_MODEL = 2048
SWA_HEADS = 16
MLA_HEADS = 8
Q_RANK = 384
KV_RANK = 256
IN_WIDTH = 4032
MLA_SCALE = 192 ** -0.5
SWA_SCALE = 64 ** -0.5
LOG2E = math.log2(math.e)
HEADS_PER_STEP = 2
SLOPES = tuple(2.0 ** (-8.0 * (h + 1) / SWA_HEADS) for h in range(SWA_HEADS))

P_WIDTH = 4096
QA_OFF, GA_OFF, GB_OFF, CQ_OFF, KA_OFF, CKV_OFF, VA_OFF, KR_OFF = 0, 1024, 2048, 3072, 3456, 3584, 3840, 3968

ADAM_LR, ADAM_B1, ADAM_B2, ADAM_EPS, ADAM_WD, ADAM_STEP = 0.001, 0.9, 0.999, 1e-08, 0.01, 10

VMEM_LIMIT = 56 * 1024 * 1024
MESH = pl.DeviceIdType.MESH
ANY = pl.BlockSpec(memory_space=pl.ANY)


def _cparams(*sem):
    return pltpu.CompilerParams(dimension_semantics=sem, vmem_limit_bytes=VMEM_LIMIT)


def _dot(a, b, ca, cb):
    return lax.dot_general(a, b, (((ca,), (cb,)), ((), ())), preferred_element_type=F32)


def _layer_spec(block, index_map, layer):
    if layer is None:
        return pl.BlockSpec(block, index_map)
    return pl.BlockSpec((None,) + tuple(block), lambda *g: (layer,) + tuple(index_map(*g)))


def _matmul(a, b, *, name, ta=False, tb=False, out_dtype=F32, add=None, b_layer=None, tm=1024, tn=1024, tk=2048):
    (kdim, m) = a.shape if ta else a.shape[::-1]
    (n, k2) = b.shape[-2:] if tb else b.shape[-2:][::-1]
    assert kdim == k2, (a.shape, b.shape)
    tm, tn, tk = min(tm, m), min(tn, n), min(tk, kdim)
    assert m % tm == 0 and n % tn == 0 and kdim % tk == 0
    nk = kdim // tk

    def body(*refs):
        a_ref, b_ref = refs[:2]
        add_ref = None if add is None else refs[2]
        o_ref = refs[2 + (add is not None)]
        part = _dot(a_ref[...].astype(MXU_DTYPE), b_ref[...].astype(MXU_DTYPE), 0 if ta else 1, 1 if tb else 0)

        def finish(r):
            o_ref[...] = (r if add is None else add_ref[...] + r).astype(out_dtype)

        if nk == 1:
            finish(part)
            return
        acc = refs[-1]
        k = pl.program_id(2)

        @pl.when(k == 0)
        def _():
            acc[...] = part

        @pl.when((k > 0) & (k < nk - 1))
        def _():
            acc[...] += part

        @pl.when(k == nk - 1)
        def _():
            finish(acc[...] + part)

    a_spec = pl.BlockSpec((tk, tm), lambda i, j, k: (k, i)) if ta else pl.BlockSpec((tm, tk), lambda i, j, k: (i, k))
    b_spec = (_layer_spec((tn, tk), lambda i, j, k: (j, k), b_layer) if tb else
              _layer_spec((tk, tn), lambda i, j, k: (k, j), b_layer))
    in_specs, args = [a_spec, b_spec], [a, b]
    if add is not None:
        in_specs.append(pl.BlockSpec((tm, tn), lambda i, j, k: (i, j)))
        args.append(add)
    return pl.pallas_call(
        body, name=name, grid=(m // tm, n // tn, nk), in_specs=in_specs,
        out_specs=pl.BlockSpec((tm, tn), lambda i, j, k: (i, j)),
        out_shape=jax.ShapeDtypeStruct((m, n), out_dtype),
        scratch_shapes=[pltpu.VMEM((tm, tn), F32)] if nk > 1 else [],
        compiler_params=_cparams("parallel", "parallel", "arbitrary"),
    )(*args)


def _matmul_ktiles(a_tiles, b, *, name, b_layer=None, tm=512, tn=1024):
    m, kt = a_tiles[0].shape
    n = b.shape[-1]
    nt = len(a_tiles)
    assert b.shape[-2] == nt * kt
    tm, tn = min(tm, m), min(tn, n)
    assert m % tm == 0 and n % tn == 0

    def body(*refs):
        a_refs, b_refs, o_ref = refs[:nt], refs[nt:2 * nt], refs[2 * nt]
        acc = _dot(a_refs[0][...].astype(MXU_DTYPE), b_refs[0][...].astype(MXU_DTYPE), 1, 0)
        for j in range(1, nt):
            acc += _dot(a_refs[j][...].astype(MXU_DTYPE), b_refs[j][...].astype(MXU_DTYPE), 1, 0)
        o_ref[...] = acc

    in_specs = [pl.BlockSpec((tm, kt), lambda i, jn: (i, 0))] * nt
    in_specs += [_layer_spec((kt, tn), lambda i, jn, j=j: (j, jn), b_layer) for j in range(nt)]
    return pl.pallas_call(
        body, name=name, grid=(m // tm, n // tn), in_specs=in_specs,
        out_specs=pl.BlockSpec((tm, tn), lambda i, jn: (i, jn)),
        out_shape=jax.ShapeDtypeStruct((m, n), F32),
        compiler_params=_cparams("parallel", "parallel"),
    )(*a_tiles, *([b] * nt))


def _rmsnorm_fwd(x, g, *, name):
    s, d = x.shape
    tm = min(512, s)

    def body(x_ref, g_ref, h_ref):
        xv = x_ref[...]
        r = lax.rsqrt(jnp.mean(xv * xv, axis=-1, keepdims=True) + EPS)
        h_ref[...] = (xv * r * g_ref[...]).astype(MXU_DTYPE)

    return pl.pallas_call(
        body, name=name, grid=(s // tm,),
        in_specs=[pl.BlockSpec((tm, d), lambda i: (i, 0)), pl.BlockSpec((1, d), lambda i: (0, 0))],
        out_specs=pl.BlockSpec((tm, d), lambda i: (i, 0)),
        out_shape=jax.ShapeDtypeStruct((s, d), MXU_DTYPE),
        compiler_params=_cparams("parallel"),
    )(x, g)


def _rmsnorm_bwd(dh, x, g, dres, *, name):
    s, d = x.shape
    tm = min(512, s)

    def body(dh_ref, x_ref, g_ref, dres_ref, dx_ref, dg_ref):
        @pl.when(pl.program_id(0) == 0)
        def _():
            dg_ref[...] = jnp.zeros_like(dg_ref)

        xv = x_ref[...]
        r = lax.rsqrt(jnp.mean(xv * xv, axis=-1, keepdims=True) + EPS)
        xn = xv * r
        dy = dh_ref[...]
        dg_ref[...] += jnp.sum(dy * xn, axis=0, keepdims=True)
        u = dy * g_ref[...]
        dx_ref[...] = dres_ref[...] + r * (u - xn * jnp.mean(u * xn, axis=-1, keepdims=True))

    row = pl.BlockSpec((tm, d), lambda i: (i, 0))
    vec = pl.BlockSpec((1, d), lambda i: (0, 0))
    return pl.pallas_call(
        body, name=name, grid=(s // tm,), in_specs=[row, row, vec, row], out_specs=[row, vec],
        out_shape=[jax.ShapeDtypeStruct((s, d), F32), jax.ShapeDtypeStruct((1, d), F32)],
        compiler_params=_cparams("arbitrary"),
    )(dh, x, g, dres)


def _final_loss(x, g, tgt, *, name):
    s, d = x.shape
    tm = min(512, s)

    def body(x_ref, g_ref, t_ref, dx_ref, dg_ref, loss_ref):
        @pl.when(pl.program_id(0) == 0)
        def _():
            dg_ref[...] = jnp.zeros_like(dg_ref)
            loss_ref[...] = jnp.zeros_like(loss_ref)

        xv = x_ref[...]
        gv = g_ref[...]
        r = lax.rsqrt(jnp.mean(xv * xv, axis=-1, keepdims=True) + EPS)
        xn = xv * r
        err = xn * gv - t_ref[...]
        sq = jnp.sum(jnp.sum(err * err, axis=-1, keepdims=True), axis=0, keepdims=True)
        loss_ref[...] += (0.5 / d) * sq
        dy = err * (1.0 / d)
        dg_ref[...] += jnp.sum(dy * xn, axis=0, keepdims=True)
        u = dy * gv
        dx_ref[...] = r * (u - xn * jnp.mean(u * xn, axis=-1, keepdims=True))

    row = pl.BlockSpec((tm, d), lambda i: (i, 0))
    vec = pl.BlockSpec((1, d), lambda i: (0, 0))
    return pl.pallas_call(
        body, name=name, grid=(s // tm,), in_specs=[row, vec, row],
        out_specs=[row, vec, pl.BlockSpec((1, 128), lambda i: (0, 0))],
        out_shape=[jax.ShapeDtypeStruct((s, d), F32), jax.ShapeDtypeStruct((1, d), F32),
                   jax.ShapeDtypeStruct((1, 128), F32)],
        compiler_params=_cparams("arbitrary"),
    )(x, g, tgt)


def _swa_keys(kp_ref, kc_ref):
    kk = jnp.concatenate([kp_ref[...], kc_ref[...]], axis=0)
    kr = pltpu.roll(kk, 64, 1)
    lo = lax.broadcasted_iota(jnp.int32, kk.shape, 1) < 64
    return [jnp.where(lo, kk, kr).astype(MXU_DTYPE), jnp.where(lo, kr, kk).astype(MXU_DTYPE)]


GROUP = SWA_HEADS // 2


def _swa_mask(n):
    qi = lax.broadcasted_iota(jnp.int32, (BLOCK, 2 * BLOCK), 0)
    ki = lax.broadcasted_iota(jnp.int32, (BLOCK, 2 * BLOCK), 1)
    delta = BLOCK + qi - ki
    valid = (delta >= 0) & (delta < BLOCK) & ((ki >= BLOCK) | (n > 0))
    return valid, delta.astype(F32)


def _stack_heads(ref, j):
    lo = lax.broadcasted_iota(jnp.int32, (BLOCK, BLOCK), 1) < 64
    parts = []
    for r in range(GROUP):
        pair = (GROUP * j + r) // 2
        blk = ref[:, pair * 128:(pair + 1) * 128].astype(F32)
        parts.append(jnp.where(lo if r % 2 == 0 else ~lo, blk, 0.0).astype(MXU_DTYPE))
    return jnp.concatenate(parts, axis=0)


def _unstack_heads(stacked, ref, j):
    lo = lax.broadcasted_iota(jnp.int32, (BLOCK, BLOCK), 1) < 64
    for i in range(GROUP // 2):
        pair = (GROUP * j) // 2 + i
        even, odd = stacked[2 * i * BLOCK:(2 * i + 1) * BLOCK], stacked[(2 * i + 1) * BLOCK:(2 * i + 2) * BLOCK]
        ref[:, pair * 128:(pair + 1) * 128] = jnp.where(lo, even, odd).astype(ref.dtype)


def _head_rows(stacked, r):
    return stacked[r * BLOCK:(r + 1) * BLOCK]


def _swa_scores(raw, h, valid, deltaf):
    return jnp.where(valid, raw * (SWA_SCALE * LOG2E) - (SLOPES[h] * LOG2E) * deltaf, NEG)


def _swa_specs(nb):
    kcol, vcol = KA_OFF // BLOCK, VA_OFF // BLOCK
    last = nb - 1
    cur = lambda n: jnp.minimum(n, last)
    prev = lambda n: jnp.maximum(jnp.minimum(n, last) - 1, 0)
    return [
        pl.BlockSpec(memory_space=pltpu.SMEM),
        pl.BlockSpec((BLOCK, 1024), lambda n: (cur(n), QA_OFF // 1024)),
        pl.BlockSpec((BLOCK, BLOCK), lambda n: (cur(n), kcol)),
        pl.BlockSpec((BLOCK, BLOCK), lambda n: (prev(n), kcol)),
        pl.BlockSpec((BLOCK, BLOCK), lambda n: (cur(n), vcol)),
        pl.BlockSpec((BLOCK, BLOCK), lambda n: (prev(n), vcol)),
    ]


def _swa_fwd(proj, sinks, *, name):
    s = proj.shape[0]
    nb = s // BLOCK

    def body(sink_ref, q_ref, kc_ref, kp_ref, vc_ref, vp_ref, o_ref, lse_ref):
        n = pl.program_id(0)
        keys = _swa_keys(kp_ref, kc_ref)
        vals = _swa_keys(vp_ref, vc_ref)
        valid, deltaf = _swa_mask(n)
        lane = lax.broadcasted_iota(jnp.int32, (BLOCK, BLOCK), 1)
        lse_acc = jnp.zeros((BLOCK, BLOCK), F32)
        for j in range(2):
            raw = _dot(_stack_heads(q_ref, j), keys[j], 1, 1)
            probs = []
            for r in range(GROUP):
                h = GROUP * j + r
                sc = _swa_scores(_head_rows(raw, r), h, valid, deltaf)
                sink = sink_ref[h] * LOG2E
                m = jnp.maximum(jnp.max(sc, axis=-1, keepdims=True), sink)
                p = jnp.exp2(sc - m)
                l = jnp.sum(p, axis=-1, keepdims=True) + jnp.exp2(sink - m)
                probs.append((p * (1.0 / l)).astype(MXU_DTYPE))
                lse_acc = jnp.where(lane == h, m + jnp.log(l) * LOG2E, lse_acc)
            _unstack_heads(jnp.dot(jnp.concatenate(probs, axis=0), vals[j], preferred_element_type=F32), o_ref, j)
        lse_ref[...] = lse_acc

    return pl.pallas_call(
        body, name=name, grid=(nb,), in_specs=_swa_specs(nb),
        out_specs=[pl.BlockSpec((BLOCK, 1024), lambda n: (n, 0)), pl.BlockSpec((BLOCK, BLOCK), lambda n: (n, 0))],
        out_shape=[jax.ShapeDtypeStruct((s, 1024), F32), jax.ShapeDtypeStruct((s, BLOCK), F32)],
        compiler_params=_cparams("parallel"),
    )(sinks, proj, proj, proj, proj, proj)


def _swa_bwd(proj, sinks, lse, do, *, name):
    s = proj.shape[0]
    nb = s // BLOCK
    last = nb - 1

    def body(sink_ref, q_ref, kc_ref, kp_ref, vc_ref, vp_ref, lse_ref, do_ref,
             dq_ref, dk_ref, dv_ref, dsink_ref, carry_k, carry_v):
        n = pl.program_id(0)

        @pl.when(n == 0)
        def _():
            carry_k[...] = jnp.zeros_like(carry_k)
            carry_v[...] = jnp.zeros_like(carry_v)
            dsink_ref[...] = jnp.zeros_like(dsink_ref)

        @pl.when(n < nb)
        def _():
            keys = _swa_keys(kp_ref, kc_ref)
            vals = _swa_keys(vp_ref, vc_ref)
            valid, deltaf = _swa_mask(n)
            lane = lax.broadcasted_iota(jnp.int32, (BLOCK, BLOCK), 1)
            lane1 = lax.broadcasted_iota(jnp.int32, (1, BLOCK), 1)
            lse_blk = lse_ref[...]
            acc_k, acc_v = [], []
            dsink = jnp.zeros((1, BLOCK), F32)
            for j in range(2):
                q_all, do_all = _stack_heads(q_ref, j), _stack_heads(do_ref, j)
                raw = _dot(q_all, keys[j], 1, 1)
                dp_all = _dot(do_all, vals[j], 1, 1)
                probs, dscores = [], []
                for r in range(GROUP):
                    h = GROUP * j + r
                    lse_h = jnp.sum(jnp.where(lane == h, lse_blk, 0.0), axis=-1, keepdims=True)
                    p = jnp.exp2(_swa_scores(_head_rows(raw, r), h, valid, deltaf) - lse_h)
                    dp = _head_rows(dp_all, r)
                    dlt = jnp.sum(dp * p, axis=-1, keepdims=True)
                    dscores.append((p * (dp - dlt) * SWA_SCALE).astype(MXU_DTYPE))
                    probs.append(p.astype(MXU_DTYPE))
                    sunk = jnp.exp2(sink_ref[h] * LOG2E - lse_h) * dlt
                    dsink = jnp.where(lane1 == h, -jnp.sum(sunk, axis=0, keepdims=True), dsink)
                ds_all = jnp.concatenate(dscores, axis=0)
                _unstack_heads(jnp.dot(ds_all, keys[j], preferred_element_type=F32), dq_ref, j)
                acc_k.append(_dot(ds_all, q_all, 0, 0))
                acc_v.append(_dot(jnp.concatenate(probs, axis=0), do_all, 0, 0))
            lo2 = lax.broadcasted_iota(jnp.int32, (2 * BLOCK, BLOCK), 1) < 64
            fold = lambda acc: jnp.where(lo2, acc[0] + pltpu.roll(acc[0], 64, 1), acc[1] + pltpu.roll(acc[1], 64, 1))
            dkk, dvv = fold(acc_k), fold(acc_v)
            dk_ref[...] = (carry_k[...] + dkk[:BLOCK]).astype(dk_ref.dtype)
            dv_ref[...] = (carry_v[...] + dvv[:BLOCK]).astype(dv_ref.dtype)
            carry_k[...] = dkk[BLOCK:]
            carry_v[...] = dvv[BLOCK:]
            dsink_ref[...] += jnp.broadcast_to(dsink, dsink_ref.shape)

        @pl.when(n == nb)
        def _():
            dk_ref[...] = carry_k[...].astype(dk_ref.dtype)
            dv_ref[...] = carry_v[...].astype(dv_ref.dtype)

    cur = lambda n: jnp.minimum(n, last)
    lag = lambda n: jnp.maximum(n - 1, 0)
    return pl.pallas_call(
        body, name=name, grid=(nb + 1,),
        in_specs=_swa_specs(nb) + [pl.BlockSpec((BLOCK, BLOCK), lambda n: (cur(n), 0)),
                                   pl.BlockSpec((BLOCK, 1024), lambda n: (cur(n), 0))],
        out_specs=[pl.BlockSpec((BLOCK, 1024), lambda n: (cur(n), 0)),
                   pl.BlockSpec((BLOCK, BLOCK), lambda n: (lag(n), 0)),
                   pl.BlockSpec((BLOCK, BLOCK), lambda n: (lag(n), 0)),
                   pl.BlockSpec((8, BLOCK), lambda n: (0, 0))],
        out_shape=[jax.ShapeDtypeStruct((s, 1024), MXU_DTYPE), jax.ShapeDtypeStruct((s, BLOCK), MXU_DTYPE),
                   jax.ShapeDtypeStruct((s, BLOCK), MXU_DTYPE), jax.ShapeDtypeStruct((8, BLOCK), F32)],
        scratch_shapes=[pltpu.VMEM((BLOCK, BLOCK), F32), pltpu.VMEM((BLOCK, BLOCK), F32)],
        compiler_params=_cparams("arbitrary"),
    )(sinks, proj, proj, proj, proj, proj, lse, do)


def _rope_partner(v, first, width):
    lane = lax.broadcasted_iota(jnp.int32, v.shape, 1)
    in_a = (lane >= first) & (lane < first + 32)
    in_b = (lane >= first + 32) & (lane < first + 64)
    return jnp.where(in_a, pltpu.roll(v, width - 32, 1), jnp.where(in_b, pltpu.roll(v, 32, 1), 0.0))


def _mla_qkv_fwd(proj, gq, gkv, wq, wkv, tk_c, tk_s, *, layer, name):
    s = proj.shape[0]
    tm = min(256, s)

    def body(cq_ref, ckv_ref, kr_ref, gq_ref, gkv_ref, wq_ref, wkv_ref, kc_ref, ks_ref,
             qcat_ref, kcat_ref, v_ref, cqn_ref, ckvn_ref):
        cq = cq_ref[...]
        cqn = (cq * lax.rsqrt(jnp.mean(cq * cq, axis=-1, keepdims=True) + EPS) * gq_ref[...]).astype(MXU_DTYPE)
        cqn_ref[...] = cqn
        qpre = _dot(cqn, wq_ref[...], 1, 1)
        kc, ks = kc_ref[...], ks_ref[...]
        for hh in range(MLA_HEADS):
            qcat_ref[:, hh * 256:hh * 256 + 128] = qpre[:, hh * 256:hh * 256 + 128].astype(MXU_DTYPE)
            blk = qpre[:, hh * 256 + 128:(hh + 1) * 256]
            qcat_ref[:, hh * 256 + 128:(hh + 1) * 256] = (blk * kc + _rope_partner(blk, 0, 128) * ks).astype(MXU_DTYPE)
        ckv = ckv_ref[...]
        ckvn = (ckv * lax.rsqrt(jnp.mean(ckv * ckv, axis=-1, keepdims=True) + EPS) * gkv_ref[...]).astype(MXU_DTYPE)
        ckvn_ref[...] = ckvn
        kv = _dot(ckvn, wkv_ref[...], 1, 1)
        kr = kr_ref[...]
        krr = (kr * kc + _rope_partner(kr, 0, 128) * ks).astype(MXU_DTYPE)
        for hh in range(MLA_HEADS):
            kcat_ref[:, hh * 256:hh * 256 + 128] = kv[:, hh * 128:(hh + 1) * 128].astype(MXU_DTYPE)
            kcat_ref[:, hh * 256 + 128:(hh + 1) * 256] = krr
            v_ref[:, hh * 256:hh * 256 + 128] = kv[:, 1024 + hh * 128:1024 + (hh + 1) * 128].astype(MXU_DTYPE)
            v_ref[:, hh * 256 + 128:(hh + 1) * 256] = jnp.ones((tm, 128), MXU_DTYPE)

    row = lambda w, c: pl.BlockSpec((tm, w), lambda i: (i, c))
    full = lambda a: pl.BlockSpec(a.shape, lambda i: (0, 0))
    of_layer = lambda a: _layer_spec(a.shape[-2:], lambda i: (0, 0), layer)
    return pl.pallas_call(
        body, name=name, grid=(s // tm,),
        in_specs=[row(Q_RANK, CQ_OFF // Q_RANK), row(KV_RANK, CKV_OFF // KV_RANK), row(128, KR_OFF // 128),
                  full(gq), full(gkv), of_layer(wq), of_layer(wkv), row(128, 0), row(128, 0)],
        out_specs=[row(2048, 0), row(2048, 0), row(2048, 0), row(Q_RANK, 0), row(KV_RANK, 0)],
        out_shape=[jax.ShapeDtypeStruct((s, 2048), MXU_DTYPE), jax.ShapeDtypeStruct((s, 2048), MXU_DTYPE),
                   jax.ShapeDtypeStruct((s, 2048), MXU_DTYPE), jax.ShapeDtypeStruct((s, Q_RANK), MXU_DTYPE),
                   jax.ShapeDtypeStruct((s, KV_RANK), MXU_DTYPE)],
        compiler_params=_cparams("parallel"),
    )(proj, proj, proj, gq, gkv, wq, wkv, tk_c, tk_s)


def _norm_bwd(x, g, dy):
    r = lax.rsqrt(jnp.mean(x * x, axis=-1, keepdims=True) + EPS)
    xn = x * r
    u = dy * g
    return r * (u - xn * jnp.mean(u * xn, axis=-1, keepdims=True)), jnp.sum(dy * xn, axis=0, keepdims=True)


def _mla_qkv_bwd(proj, cqn, ckvn, dqcat, dkcat, dv, dka, dva, gq, gkv, wq, wkv, tk_c, tk_s, *, layer, name):
    s = proj.shape[0]
    tm = min(256, s)
    t_cq, t_ka, t_ckv, t_va, t_kr = (o - CQ_OFF for o in (CQ_OFF, KA_OFF, CKV_OFF, VA_OFF, KR_OFF))

    def body(cq_ref, ckv_ref, cqn_ref, ckvn_ref, dq_ref, dk_ref, dv_ref, dka_ref, dva_ref, gq_ref, gkv_ref, wq_ref,
             wkv_ref, kc_ref, ks_ref,
             tile_ref, dwq_ref, dwkv_ref, dgq_ref, dgkv_ref, dqpre, dkv, dwq_acc, dwkv_acc):
        dcq_ref = tile_ref.at[:, t_cq:t_cq + Q_RANK]
        dckv_ref = tile_ref.at[:, t_ckv:t_ckv + KV_RANK]
        dkr_ref = tile_ref.at[:, t_kr:t_kr + 128]
        tile_ref[:, t_ka:t_ka + 128] = dka_ref[...]
        tile_ref[:, t_va:t_va + 128] = dva_ref[...]

        @pl.when(pl.program_id(0) == 0)
        def _():
            for r in (dwq_acc, dwkv_acc, dgq_ref, dgkv_ref):
                r[...] = jnp.zeros_like(r)

        kc, ks = kc_ref[...], ks_ref[...]
        dkrr = jnp.zeros((tm, 128), F32)
        for hh in range(MLA_HEADS):
            dqpre[:, hh * 256:hh * 256 + 128] = dq_ref[:, hh * 256:hh * 256 + 128].astype(MXU_DTYPE)
            blk = dq_ref[:, hh * 256 + 128:(hh + 1) * 256]
            dqpre[:, hh * 256 + 128:(hh + 1) * 256] = (blk * kc + _rope_partner(blk * ks, 0, 128)).astype(MXU_DTYPE)
            dkv[:, hh * 128:(hh + 1) * 128] = dk_ref[:, hh * 256:hh * 256 + 128].astype(MXU_DTYPE)
            dkrr = dkrr + dk_ref[:, hh * 256 + 128:(hh + 1) * 256]
        dkv[:, 1024:] = dv_ref[...].astype(MXU_DTYPE)
        dkr_ref[...] = (dkrr * kc + _rope_partner(dkrr * ks, 0, 128)).astype(dkr_ref.dtype)

        dq_b = dqpre[...]
        dwq_acc[...] += _dot(cqn_ref[...], dq_b, 0, 0)
        dcq, dgq = _norm_bwd(cq_ref[...], gq_ref[...], _dot(dq_b, wq_ref[...], 1, 0))
        dcq_ref[...] = dcq.astype(dcq_ref.dtype)
        dgq_ref[...] += dgq

        dkv_b = dkv[...]
        dwkv_acc[...] += _dot(ckvn_ref[...], dkv_b, 0, 0)
        dckv, dgkv = _norm_bwd(ckv_ref[...], gkv_ref[...], _dot(dkv_b, wkv_ref[...], 1, 0))
        dckv_ref[...] = dckv.astype(dckv_ref.dtype)
        dgkv_ref[...] += dgkv

        @pl.when(pl.program_id(0) == s // tm - 1)
        def _():
            dwq_ref[...] = dwq_acc[...].T
            dwkv_ref[...] = dwkv_acc[...].T

    row = lambda w, c: pl.BlockSpec((tm, w), lambda i: (i, c))
    full = lambda shape: pl.BlockSpec(shape, lambda i: (0, 0))
    of_layer = lambda a: _layer_spec(a.shape[-2:], lambda i: (0, 0), layer)
    return pl.pallas_call(
        body, name=name, grid=(s // tm,),
        in_specs=[row(Q_RANK, CQ_OFF // Q_RANK), row(KV_RANK, CKV_OFF // KV_RANK), row(Q_RANK, 0), row(KV_RANK, 0),
                  row(2048, 0), row(2048, 0), row(1024, 0), row(128, 0), row(128, 0), full(gq.shape), full(gkv.shape),
                  of_layer(wq), of_layer(wkv), row(128, 0), row(128, 0)],
        out_specs=[row(1024, 0), full(wq.shape[-2:]), full(wkv.shape[-2:]), full(gq.shape), full(gkv.shape)],
        out_shape=[jax.ShapeDtypeStruct((s, 1024), MXU_DTYPE), jax.ShapeDtypeStruct(wq.shape[-2:], F32),
                   jax.ShapeDtypeStruct(wkv.shape[-2:], F32), jax.ShapeDtypeStruct(gq.shape, F32),
                   jax.ShapeDtypeStruct(gkv.shape, F32)],
        scratch_shapes=[pltpu.VMEM((tm, 2048), MXU_DTYPE), pltpu.VMEM((tm, 2048), MXU_DTYPE),
                        pltpu.VMEM((Q_RANK, 2048), F32), pltpu.VMEM((KV_RANK, 2048), F32)],
        compiler_params=_cparams("arbitrary"),
    )(proj, proj, cqn, ckvn, dqcat, dkcat, dv, dka, dva, gq, gkv, wq, wkv, tk_c, tk_s)


def _loop_by_two(lo, hi, step):
    def pair(i, carry):
        step(lo + 2 * i)
        step(lo + 2 * i + 1)
        return carry

    lax.fori_loop(0, (hi - lo) // 2, pair, 0)

    @pl.when((hi - lo) % 2 == 1)
    def _():
        step(hi - 1)


def _causal_mask(t):
    return lax.broadcasted_iota(jnp.int32, (t, t), 1) <= lax.broadcasted_iota(jnp.int32, (t, t), 0)


def _mla_fwd(qcat, kcat, v, *, name, gather=None):
    s = qcat.shape[0]
    t = min(512, s)
    nq = s // t
    hp = HEADS_PER_STEP
    ng = MLA_HEADS // hp
    c2 = MLA_SCALE * LOG2E

    nw = 0 if gather is None else len(gather)

    def body(q_ref, k_ref, v_ref, *rest):
        src, (o_ref, lse_ref), out = rest[:nw], rest[nw:nw + 2], rest[nw + 2:2 * nw + 2]
        top_s, acc_s, *sems = rest[2 * nw + 2:]
        g, qi = pl.program_id(0), pl.program_id(1)
        if nw:
            @pl.when((g == 0) & (qi == 0))
            def _():
                _gather_start(src, out, *sems)

            @pl.when((g == ng - 1) & (qi == 0))
            def _():
                _gather_forward(src, out, *sems)

        rows = lambda j: pl.ds(pl.multiple_of(j * t, t), t)
        head = lambda e: slice(e * 256, (e + 1) * 256)
        raw = lambda e, j: _dot(q_ref[:, head(e)], k_ref[rows(j), head(e)], 1, 1)

        for e in range(hp):
            top_s[e] = jnp.where(_causal_mask(t), raw(e, qi), NEG)

        def pass1(j):
            for e in range(hp):
                top_s[e] = jnp.maximum(top_s[e], raw(e, j))

        _loop_by_two(0, qi, pass1)
        m = [jnp.max(top_s[e], axis=-1, keepdims=True) * c2 for e in range(hp)]

        def weighted(e, j, masked):
            sc = raw(e, j) * c2 - m[e]
            if masked:
                sc = jnp.where(_causal_mask(t), sc, NEG)
            return jnp.dot(jnp.exp2(sc).astype(MXU_DTYPE), v_ref[rows(j), head(e)], preferred_element_type=F32)

        for e in range(hp):
            acc_s[e] = weighted(e, qi, True)

        def pass2(j):
            for e in range(hp):
                acc_s[e] += weighted(e, j, False)

        _loop_by_two(0, qi, pass2)
        lane = lax.broadcasted_iota(jnp.int32, (t, 128), 1)
        stats = jnp.zeros((t, 128), F32)
        for e in range(hp):
            l = acc_s[e, :, 128:]
            o_ref[:, e * 128:(e + 1) * 128] = acc_s[e, :, :128] / l
            stats = jnp.where(lane == e, m[e] + jnp.log(l) * LOG2E, stats)
        lse_ref[...] = stats
        if nw:
            @pl.when((g == ng - 1) & (qi == nq - 1))
            def _():
                _gather_finish(src, out, *sems)

    outs = pl.pallas_call(
        body, name=name, grid=(ng, nq),
        in_specs=[pl.BlockSpec((t, 256 * hp), lambda g, qi: (qi, g)), pl.BlockSpec((s, 256 * hp), lambda g, qi: (0, g)),
                  pl.BlockSpec((s, 256 * hp), lambda g, qi: (0, g))] + [ANY] * nw,
        out_specs=[pl.BlockSpec((t, 128 * hp), lambda g, qi: (qi, g)), pl.BlockSpec((t, 128), lambda g, qi: (qi, g))]
        + [ANY] * nw,
        out_shape=[jax.ShapeDtypeStruct((s, 1024), F32), jax.ShapeDtypeStruct((s, 128 * ng), F32)]
        + (_gathered_shapes(gather) if nw else []),
        scratch_shapes=[pltpu.VMEM((hp, t, t), F32), pltpu.VMEM((hp, t, 256), F32)] + (_sems(6, nw) if nw else []),
        compiler_params=_cparams("arbitrary", "arbitrary"),
    )(qcat, kcat, v, *(gather or []))
    return outs[0], outs[1], list(outs[2:])


def _mla_bwd(qcat, kcat, v, do, lse, delta, *, name, scatter=None):
    s = qcat.shape[0]
    t = min(512, s)
    nq = s // t
    hp = HEADS_PER_STEP
    c2 = MLA_SCALE * LOG2E
    nw = 0 if scatter is None else len(scatter)

    def body(q_ref, k_ref, v_ref, do_ref, lse_ref, dl_ref, *rest):
        src, (dq_ref, dk_ref, dv_ref), out = rest[:nw], rest[nw:nw + 3], rest[nw + 3:2 * nw + 3]
        dk_acc, dv_acc, *sems = rest[2 * nw + 3:]
        h, ki = pl.program_id(0), pl.program_id(1)
        if nw:
            @pl.when((h == 0) & (ki == 0))
            def _():
                _scatter_start(src, out, *sems)

        @pl.when(ki == 0)
        def _():
            dq_ref[...] = jnp.zeros_like(dq_ref)

        dk_acc[...] = jnp.zeros_like(dk_acc)
        dv_acc[...] = jnp.zeros_like(dv_acc)
        k, vv = k_ref[...], v_ref[...]
        mine = lax.broadcasted_iota(jnp.int32, (t, 128), 1) == h % hp

        def chunk(qi, masked):
            rows = pl.ds(pl.multiple_of(qi * t, t), t)
            q, dob = q_ref[rows, :], do_ref[rows, :]
            pick = lambda r: jnp.sum(jnp.where(mine, r[rows, :], 0.0), axis=-1, keepdims=True)
            sc = _dot(q, k, 1, 1) * c2
            if masked:
                sc = jnp.where(_causal_mask(t), sc, NEG)
            p = jnp.exp2(sc - pick(lse_ref))
            dp = _dot(dob, vv, 1, 1)
            ds = (p * (dp - pick(dl_ref)) * MLA_SCALE).astype(MXU_DTYPE)
            dv_acc[...] += _dot(dob, p.astype(MXU_DTYPE), 0, 0)
            dk_acc[...] += _dot(q, ds, 0, 0)
            dq_ref[rows, :] += jnp.dot(ds, k, preferred_element_type=F32)

        chunk(ki, True)
        _loop_by_two(ki + 1, nq, lambda qi: chunk(qi, False))
        dk_ref[...] = dk_acc[...].T
        dv_ref[...] = dv_acc[...].T
        if nw:
            @pl.when((h == MLA_HEADS - 1) & (ki == nq - 1))
            def _():
                _scatter_finish(src, out, *sems)

    head = lambda w: pl.BlockSpec((s, w), lambda h, ki: (0, h))
    blk = lambda w: pl.BlockSpec((t, w), lambda h, ki: (ki, h))
    stat = pl.BlockSpec((s, 128), lambda h, ki: (0, h // hp))
    outs = pl.pallas_call(
        body, name=name, grid=(MLA_HEADS, nq),
        in_specs=[head(256), blk(256), pl.BlockSpec((t, 128), lambda h, ki: (ki, 2 * h)), head(128), stat, stat]
        + [ANY] * nw,
        out_specs=[head(256), blk(256), blk(128)] + [ANY] * nw,
        out_shape=[jax.ShapeDtypeStruct((s, 2048), F32), jax.ShapeDtypeStruct((s, 2048), F32),
                   jax.ShapeDtypeStruct((s, 1024), F32)] + [jax.ShapeDtypeStruct(a.shape, a.dtype) for a in scatter or []],
        scratch_shapes=[pltpu.VMEM((256, t), F32), pltpu.VMEM((128, t), F32)] + (_sems(3, nw) if nw else []),
        compiler_params=_cparams("arbitrary", "arbitrary"),
    )(qcat, kcat, v, do, lse, delta, *(scatter or []))
    return outs[0], outs[1], outs[2], list(outs[3:])


def _gate_specs(tm):
    half = lambda c: pl.BlockSpec((tm, 1024), lambda i: (i, c))
    return half(0), half(GA_OFF // 1024), half(GB_OFF // 1024)


def _gate_fwd(oa, ob, proj, *, name):
    s = oa.shape[0]
    tm = min(512, s)

    def body(oa_ref, ob_ref, ga_ref, gb_ref, y_ref):
        ga, gb = ga_ref[...], gb_ref[...]
        y_ref[:, :1024] = (oa_ref[...] * (ga * jax.nn.sigmoid(ga))).astype(MXU_DTYPE)
        y_ref[:, 1024:] = (ob_ref[...] * (gb * jax.nn.sigmoid(gb))).astype(MXU_DTYPE)

    o_spec, ga_spec, gb_spec = _gate_specs(tm)
    return pl.pallas_call(
        body, name=name, grid=(s // tm,), in_specs=[o_spec, o_spec, ga_spec, gb_spec],
        out_specs=pl.BlockSpec((tm, 2048), lambda i: (i, 0)),
        out_shape=jax.ShapeDtypeStruct((s, 2048), MXU_DTYPE),
        compiler_params=_cparams("parallel"),
    )(oa, ob, proj, proj)


def _gate_bwd(dy, oa, ob, proj, *, name):
    s = oa.shape[0]
    tm = min(512, s)

    def body(dy_ref, oa_ref, ob_ref, ga_ref, gb_ref, doa_ref, dob_ref, dga_ref, dgb_ref, dl_ref):
        def branch(dyv, o, g, do_ref, dg_ref):
            sg = jax.nn.sigmoid(g)
            do = dyv * (g * sg)
            do_ref[...] = do.astype(MXU_DTYPE)
            dg_ref[...] = (dyv * o * (sg * (1.0 + g * (1.0 - sg)))).astype(MXU_DTYPE)
            return do

        branch(dy_ref[:, :1024], oa_ref[...], ga_ref[...], doa_ref, dga_ref)
        ob = ob_ref[...]
        prod = branch(dy_ref[:, 1024:], ob, gb_ref[...], dob_ref, dgb_ref) * ob
        lane = lax.broadcasted_iota(jnp.int32, (tm, stat_w), 1)
        acc = jnp.zeros((tm, stat_w), F32)
        for hh in range(MLA_HEADS):
            at = (hh // HEADS_PER_STEP) * 128 + hh % HEADS_PER_STEP
            acc = jnp.where(lane == at, jnp.sum(prod[:, hh * 128:(hh + 1) * 128], axis=-1, keepdims=True), acc)
        dl_ref[...] = acc

    stat_w = 128 * (MLA_HEADS // HEADS_PER_STEP)
    o_spec, ga_spec, gb_spec = _gate_specs(tm)
    return pl.pallas_call(
        body, name=name, grid=(s // tm,),
        in_specs=[pl.BlockSpec((tm, 2048), lambda i: (i, 0)), o_spec, o_spec, ga_spec, gb_spec],
        out_specs=[o_spec, o_spec, o_spec, o_spec, pl.BlockSpec((tm, stat_w), lambda i: (i, 0))],
        out_shape=[jax.ShapeDtypeStruct((s, 1024), MXU_DTYPE)] * 4 + [jax.ShapeDtypeStruct((s, stat_w), F32)],
        compiler_params=_cparams("parallel"),
    )(dy, oa, ob, proj, proj)


def _row_block(rows, cols, itemsize=4, budget=2 << 20):
    fits = [tr for tr in range(16, rows + 1, 16) if rows % tr == 0 and tr * cols * itemsize <= budget]
    return fits[-1] if fits else rows


def _adamw(w, g, m, v, *, name):
    shape = w.shape
    rows, cols = shape[-2:]
    w3, g3, m3, v3 = (a.reshape((-1, rows, cols)) for a in (w, g, m, v))
    lead = w3.shape[0]
    tr = _row_block(rows, cols)

    def body(w_ref, g_ref, m_ref, v_ref, d_ref, mo_ref, vo_ref):
        gv = g_ref[...]
        mn = ADAM_B1 * m_ref[...] + (1.0 - ADAM_B1) * gv
        vn = ADAM_B2 * v_ref[...] + (1.0 - ADAM_B2) * jnp.square(gv)
        m_hat = mn / (1.0 - ADAM_B1 ** ADAM_STEP)
        v_hat = vn / (1.0 - ADAM_B2 ** ADAM_STEP)
        d_ref[...] = -ADAM_LR * (m_hat / (jnp.sqrt(v_hat) + ADAM_EPS) + ADAM_WD * w_ref[...])
        mo_ref[...] = mn
        vo_ref[...] = vn

    spec = pl.BlockSpec((None, tr, cols), lambda a, i: (a, i, 0))
    outs = pl.pallas_call(
        body, name=name, grid=(lead, rows // tr), in_specs=[spec] * 4, out_specs=[spec] * 3,
        out_shape=[jax.ShapeDtypeStruct((lead, rows, cols), F32)] * 3,
        compiler_params=_cparams("parallel", "parallel"),
    )(w3, g3, m3, v3)
    return tuple(o.reshape(shape) for o in outs)


def _pair_sum(where, grads, recv, *, name):
    depth, chips, rows, cols = grads.shape
    hl = depth // 2
    tr = _row_block(rows, cols)

    def body(where_ref, a_ref, b_ref, o_ref):
        o_ref[...] = (a_ref[...] + b_ref[...].astype(F32)).astype(WIRE_DTYPE)

    spec = pl.BlockSpec((None, None, tr, cols), lambda a, k, i, w: (a, k, i, 0))
    return pl.pallas_call(
        body, name=name,
        grid_spec=pltpu.PrefetchScalarGridSpec(
            num_scalar_prefetch=1, grid=(hl, chips, rows // tr),
            in_specs=[pl.BlockSpec((None, None, tr, cols), lambda a, k, i, w: (hl * w[4] + a, k, i, 0)), spec],
            out_specs=spec),
        out_shape=jax.ShapeDtypeStruct((hl, chips, rows, cols), WIRE_DTYPE),
        compiler_params=_cparams("parallel", "parallel", "parallel"),
    )(where, grads, recv)


def _chip_sum(where, grads, recv, parts, *, name):
    depth, _, rows, cols = grads.shape
    hl = depth // 2
    tr = _row_block(rows, cols)

    def body(where_ref, a_ref, b_ref, t0_ref, t1_ref, t2_ref, o_ref):
        total = a_ref[...] + b_ref[...].astype(F32)
        for t_ref in (t0_ref, t1_ref, t2_ref):
            total = total + t_ref[...].astype(F32)
        o_ref[...] = total

    slot = lambda j: pl.BlockSpec((None, None, tr, cols), lambda a, i, w: (a, w[j], i, 0))
    return pl.pallas_call(
        body, name=name,
        grid_spec=pltpu.PrefetchScalarGridSpec(
            num_scalar_prefetch=1, grid=(hl, rows // tr),
            in_specs=[pl.BlockSpec((None, None, tr, cols), lambda a, i, w: (hl * w[4] + a, w[0], i, 0)), slot(0), slot(1),
                      slot(2), slot(3)],
            out_specs=pl.BlockSpec((None, tr, cols), lambda a, i, w: (hl * w[4] + a, i, 0))),
        out_shape=jax.ShapeDtypeStruct((depth, rows, cols), F32),
        compiler_params=_cparams("parallel", "parallel"),
    )(where, grads, recv, parts, parts, parts)


def _place():
    x, y, c = lax.axis_index("x"), lax.axis_index("y"), lax.axis_index("c")
    chips = [(1 - x, y), (x, 1 - y), (1 - x, 1 - y)]
    return x, y, c, chips


def _sems(*shape):
    return [pltpu.SemaphoreType.DMA(shape), pltpu.SemaphoreType.DMA(shape)]


OWNER_CORE = (0, 1, 1, 1)


def _gather_copy(src, out, send_sems, recv_sems, t, sem, slot, to, forward=False):
    n = src[t].shape[0]
    rows = out[t].at[pl.ds(pl.multiple_of(slot * n, 16), n)]
    return pltpu.make_async_remote_copy(src_ref=rows if forward else src[t], dst_ref=rows, send_sem=send_sems.at[sem, t],
                                        recv_sem=recv_sems.at[sem, t], device_id=to, device_id_type=MESH)


def _gather_start(src, out, send_sems, recv_sems):
    x, y, c, chips = _place()
    for t, owner in enumerate(OWNER_CORE):
        @pl.when(c == owner)
        def _():
            for j, chip in enumerate(chips):
                _gather_copy(src, out, send_sems, recv_sems, t, j, 2 * x + y, (*chip, c)).start()


def _gather_forward(src, out, send_sems, recv_sems):
    x, y, c, chips = _place()
    for t, owner in enumerate(OWNER_CORE):
        @pl.when(c == owner)
        def _():
            for j, (px, py) in enumerate(chips):
                _gather_copy(src, out, send_sems, recv_sems, t, j, 2 * px + py, (px, py, c)).wait_recv()
                _gather_copy(src, out, send_sems, recv_sems, t, 3 + j, 2 * px + py, (x, y, 1 - c), forward=True).start()


def _gather_finish(src, out, send_sems, recv_sems):
    x, y, c, chips = _place()
    slots = [2 * px + py for px, py in chips]
    for t, owner in enumerate(OWNER_CORE):
        copy = functools.partial(_gather_copy, src, out, send_sems, recv_sems, t)

        @pl.when(c == owner)
        def _():
            for j, (px, py) in enumerate(chips):
                copy(j, 2 * x + y, (px, py, c)).wait_send()
                copy(3 + j, slots[j], (x, y, 1 - c), forward=True).wait_send()

        @pl.when(c != owner)
        def _():
            for j in range(3):
                copy(3 + j, slots[j], (x, y, 1 - c), forward=True).wait_recv()


def _gathered_shapes(shards):
    return [jax.ShapeDtypeStruct((4 * a.shape[0], a.shape[1]), a.dtype) for a in shards]


def _comm_gather_layer(shards, *, name):
    nt = len(shards)

    def body(*refs):
        src, out, sems = refs[:nt], refs[nt:2 * nt], refs[2 * nt:]
        _gather_start(src, out, *sems)
        _gather_forward(src, out, *sems)
        _gather_finish(src, out, *sems)

    return pl.pallas_call(
        body, name=name, in_specs=[ANY] * nt, out_specs=[ANY] * nt, out_shape=_gathered_shapes(shards),
        scratch_shapes=_sems(6, nt),
    )(*shards)


def _comm_swap_sibling(bufs, *, name):
    nt = len(bufs)

    def body(*refs):
        src, out, (send_sems, recv_sems) = refs[:nt], refs[nt:2 * nt], refs[2 * nt:]
        x, y, c, _ = _place()
        cps = [pltpu.make_async_remote_copy(src_ref=src[t], dst_ref=out[t], send_sem=send_sems.at[t], recv_sem=recv_sems.at[t],
                                            device_id=(x, y, 1 - c), device_id_type=MESH) for t in range(nt)]
        for cp in cps:
            cp.start()
        for cp in cps:
            cp.wait()

    return pl.pallas_call(
        body, name=name, in_specs=[ANY] * nt, out_specs=[ANY] * nt,
        out_shape=[jax.ShapeDtypeStruct(a.shape, a.dtype) for a in bufs], scratch_shapes=_sems(nt),
    )(*bufs)


def _scatter_copy(src, out, send_sems, recv_sems, j, t, from_slot, to_slot, to):
    return pltpu.make_async_remote_copy(src_ref=src[t].at[:, from_slot], dst_ref=out[t].at[:, to_slot],
                                        send_sem=send_sems.at[j, t], recv_sem=recv_sems.at[j, t], device_id=to,
                                        device_id_type=MESH)


def _scatter_start(src, out, send_sems, recv_sems):
    x, y, c, chips = _place()
    for j, (px, py) in enumerate(chips):
        for t in range(len(src)):
            _scatter_copy(src, out, send_sems, recv_sems, j, t, 2 * px + py, 2 * x + y, (px, py, c)).start()


def _scatter_finish(src, out, send_sems, recv_sems):
    x, y, c, chips = _place()
    for j, (px, py) in enumerate(chips):
        for t in range(len(src)):
            _scatter_copy(src, out, send_sems, recv_sems, j, t, 2 * x + y, 2 * px + py, (px, py, c)).wait_recv()
    for j, (px, py) in enumerate(chips):
        for t in range(len(src)):
            _scatter_copy(src, out, send_sems, recv_sems, j, t, 2 * px + py, 2 * x + y, (px, py, c)).wait_send()


def _comm_scatter_chips(parts, *, name):
    nt = len(parts)

    def body(*refs):
        src, out, sems = refs[:nt], refs[nt:2 * nt], refs[2 * nt:]
        _scatter_start(src, out, *sems)
        _scatter_finish(src, out, *sems)

    return pl.pallas_call(
        body, name=name, in_specs=[ANY] * nt, out_specs=[ANY] * nt,
        out_shape=[jax.ShapeDtypeStruct(a.shape, a.dtype) for a in parts], scratch_shapes=_sems(3, nt),
    )(*parts)


def _comm_join_halves(bufs, *, name):
    nt = len(bufs)
    hl = bufs[0].shape[0] // 2

    def body(*refs):
        src, out, (send_sems, recv_sems) = refs[:nt], refs[nt:2 * nt], refs[2 * nt:]
        x, y, c, _ = _place()
        mine = pl.ds(c * hl, hl)
        other = pl.ds((1 - c) * hl, hl)
        copy = lambda t, layers: pltpu.make_async_remote_copy(
            src_ref=src[t].at[mine], dst_ref=out[t].at[layers], send_sem=send_sems.at[t], recv_sem=recv_sems.at[t],
            device_id=(x, y, 1 - c), device_id_type=MESH)
        sends = [copy(t, mine) for t in range(nt)]
        for cp in sends:
            cp.start()
        for t in range(nt):
            copy(t, other).wait_recv()
        for cp in sends:
            cp.wait_send()

    return pl.pallas_call(
        body, name=name, in_specs=[ANY] * nt, out_specs=[ANY] * nt,
        out_shape=[jax.ShapeDtypeStruct(a.shape, a.dtype) for a in bufs],
        input_output_aliases={t: t for t in range(nt)}, scratch_shapes=_sems(nt),
    )(*bufs)


def _comm_allreduce_small(part, *, name):
    rows, cols = part.shape

    def body(p_ref, o_ref, buf, send_sems, recv_sems):
        x, y, c, _ = _place()
        me = 4 * x + 2 * y + c
        buf[me] = p_ref[...]
        flip = lambda v, bit: 1 - v if bit else v
        peers = [(flip(x, d & 4), flip(y, d & 2), flip(c, d & 1)) for d in range(1, 8)]
        sends = []
        for j, peer in enumerate(peers):
            cp = pltpu.make_async_remote_copy(src_ref=buf.at[me], dst_ref=buf.at[me], send_sem=send_sems.at[j],
                                              recv_sem=recv_sems.at[j], device_id=peer, device_id_type=MESH)
            cp.start()
            sends.append(cp)
        for j, (px, py, pc) in enumerate(peers):
            pltpu.make_async_remote_copy(src_ref=buf.at[me], dst_ref=buf.at[4 * px + 2 * py + pc], send_sem=send_sems.at[j],
                                         recv_sem=recv_sems.at[j], device_id=(px, py, pc), device_id_type=MESH).wait_recv()
        for cp in sends:
            cp.wait_send()
        total = buf[0]
        for i in range(1, 8):
            total = total + buf[i]
        o_ref[...] = total

    vm = pl.BlockSpec(memory_space=pltpu.VMEM)
    return pl.pallas_call(
        body, name=name, in_specs=[vm], out_specs=vm, out_shape=jax.ShapeDtypeStruct((rows, cols), F32),
        scratch_shapes=[pltpu.VMEM((8, rows, cols), F32), pltpu.SemaphoreType.DMA((7,)), pltpu.SemaphoreType.DMA((7,))],
    )(part)


def _pad_in_rows(wt):
    r = lambda o, n: wt[..., o:o + n, :]
    kr = r(2944, 64)
    return jnp.concatenate([r(0, 1024), r(1280, 1024), r(3008, 1024), r(2304, Q_RANK), r(1024, 128), r(2688, KV_RANK),
                            r(1152, 128), kr, jnp.zeros_like(kr)], axis=-2)


def _unpad_in_rows(qa, ga, gb, mixed):
    cq, ka, ckv, va, kr = (mixed[..., o - CQ_OFF:o - CQ_OFF + n, :] for o, n in (
        (CQ_OFF, Q_RANK), (KA_OFF, 128), (CKV_OFF, KV_RANK), (VA_OFF, 128), (KR_OFF, 64)))
    return jnp.concatenate([qa, ka, va, ga, cq, ckv, kr, gb], axis=-2)


def _pad_q_rows(wt):
    lead, cols = wt.shape[:-2], wt.shape[-1]
    wt = jnp.pad(wt.reshape(lead + (MLA_HEADS, 192, cols)), [(0, 0)] * (len(lead) + 1) + [(0, 64), (0, 0)])
    return wt.reshape(lead + (MLA_HEADS * 256, cols))


def _unpad_q_rows(wt):
    lead, cols = wt.shape[:-2], wt.shape[-1]
    return wt.reshape(lead + (MLA_HEADS, 256, cols))[..., :192, :].reshape(lead + (MLA_HEADS * 192, cols))


def _perm_kv_rows(wt):
    lead, cols = wt.shape[:-2], wt.shape[-1]
    return jnp.swapaxes(wt.reshape(lead + (MLA_HEADS, 2, 128, cols)), -4, -3).reshape(lead + (2048, cols))


def _unperm_kv_rows(wt):
    lead, cols = wt.shape[:-2], wt.shape[-1]
    return jnp.swapaxes(wt.reshape(lead + (2, MLA_HEADS, 128, cols)), -4, -3).reshape(lead + (2048, cols))


def _t(a):
    return jnp.swapaxes(a, -1, -2)


def _rope_tables(s):
    pos = jnp.arange(s, dtype=F32)
    inv_freq = 10000.0 ** (-jnp.arange(0, 64, 2, dtype=F32) / 64)
    ang = pos[:, None] * inv_freq[None, :]
    cos, sin = jnp.cos(ang), jnp.sin(ang)
    z64 = jnp.zeros((s, 64), F32)
    tk_c = jnp.concatenate([cos, cos, z64], axis=-1)
    tk_s = jnp.concatenate([-sin, sin, z64], axis=-1)
    return tk_c, tk_s


def _layer_weights(gathered, own, chip):
    def with_own_rows(g, o):
        n = o.shape[0]
        return jnp.concatenate([lax.select(chip == j, o, g[j * n:(j + 1) * n]) for j in range(4)], axis=0)

    full_in, full_q, full_kv, full_o = (with_own_rows(g, o) for g, o in zip(gathered, own))
    return _pad_in_rows(full_in), _pad_q_rows(full_q), _perm_kv_rows(full_kv), full_o


def _device_step(xs, tgt, attn_g, sinks, gq, gkv, final_g, shards, where):
    chip = where[0]
    depth = shards[0].shape[0]
    s = xs.shape[0]
    tabs = _rope_tables(s)
    saved = []
    x = xs
    of_layer = lambda l: [a[l] for a in shards]
    gathered = _comm_gather_layer(of_layer(0), name="comm_gather_layer0")
    weights = []
    for l in range(depth):
        w_in_p, w_q_p, w_kv_p, w_o = _layer_weights(gathered, of_layer(l), chip)
        weights.append((w_in_p, w_q_p, w_kv_p, w_o))
        h = _rmsnorm_fwd(x, attn_g[l:l + 1], name=f"norm_fwd{l}")
        proj = _matmul(h, w_in_p, tb=True, name=f"in_proj{l}")
        oa, lse_a = _swa_fwd(proj, sinks[l], name=f"swa_fwd{l}")
        qcat, kcat, v, cqn, ckvn = _mla_qkv_fwd(proj, gq[l:l + 1], gkv[l:l + 1], w_q_p, w_kv_p, *tabs, layer=None,
                                                name=f"mla_qkv_fwd{l}")
        ob, lse_b, gathered = _mla_fwd(qcat, kcat, v, gather=of_layer(l + 1) if l + 1 < depth else None,
                                       name=f"mla_fwd{l}")
        y = _gate_fwd(oa, ob, proj, name=f"gate_fwd{l}")
        x_next = _matmul(y, w_o, add=x, name=f"out_proj{l}")
        saved.append((x, h, proj, oa, lse_a, qcat, kcat, v, cqn, ckvn, ob, lse_b, y))
        x = x_next

    dx, d_final_g, loss = _final_loss(x, final_g, tgt, name="final_loss")

    d_attn_g, d_sinks, d_gq, d_gkv = [None] * depth, [None] * depth, [None] * depth, [None] * depth
    d_w_in, d_w_q, d_w_kv, d_w_o = [None] * depth, [None] * depth, [None] * depth, [None] * depth
    half = depth // 2

    def grads_by_chip(layers):
        stack = lambda per_layer: jnp.stack([per_layer[l] for l in layers])
        in_tiles = [stack([tiles[j] if tiles else None for tiles in d_w_in]) for j in range(4)]
        full = [_unpad_in_rows(*in_tiles), _unpad_q_rows(stack(d_w_q)), _unperm_kv_rows(stack(d_w_kv)), stack(d_w_o)]
        return [g.reshape(g.shape[0], 4, g.shape[1] // 4, g.shape[2]) for g in full]

    def reduce_begin(grads4, tag):
        hl = grads4[0].shape[0] // 2
        give = [lax.dynamic_slice_in_dim(g, (1 - where[4]) * hl, hl, axis=0).astype(WIRE_DTYPE) for g in grads4]
        recv = _comm_swap_sibling(give, name=f"comm_swap_sibling_{tag}")
        return recv, [_pair_sum(where, g, r, name=f"pair_sum_{tag}{t}") for t, (g, r) in enumerate(zip(grads4, recv))]

    def reduce_end(grads4, recv, parts, tag):
        mine = [_chip_sum(where, g, r, p, name=f"chip_sum_{tag}{t}") for t, (g, r, p) in enumerate(zip(grads4, recv, parts))]
        return _comm_join_halves(mine, name=f"comm_join_halves_{tag}")

    for l in reversed(range(depth)):
        x, h, proj, oa, lse_a, qcat, kcat, v, cqn, ckvn, ob, lse_b, y = saved[l]
        w_in_p, w_q_p, w_kv_p, w_o = weights[l]
        dy = _matmul(dx, w_o, tb=True, name=f"out_proj_dx{l}")
        d_w_o[l] = _matmul(y, dx, ta=True, name=f"out_proj_dw{l}")
        doa, dob, dga, dgb, delta_b = _gate_bwd(dy, oa, ob, proj, name=f"gate_bwd{l}")
        dqa, dka, dva, dsink = _swa_bwd(proj, sinks[l], lse_a, doa, name=f"swa_bwd{l}")
        pair_hi = None
        if l == half - 1:
            grads_hi = grads_by_chip(range(half, depth))
            recv_hi, pair_hi = reduce_begin(grads_hi, "hi")
        dqc, dkc, dv, parts = _mla_bwd(qcat, kcat, v, dob, lse_b, delta_b, scatter=pair_hi, name=f"mla_bwd{l}")
        if l == half - 1:
            reduced_hi = reduce_end(grads_hi, recv_hi, parts, "hi")
        mixed, d_w_q[l], d_w_kv[l], dgq_l, dgkv_l = _mla_qkv_bwd(
            proj, cqn, ckvn, dqc, dkc, dv, dka, dva, gq[l:l + 1], gkv[l:l + 1], w_q_p, w_kv_p, *tabs, layer=None,
            name=f"mla_qkv_bwd{l}")
        dproj = [dqa, dga, dgb, mixed]
        dh = _matmul_ktiles(dproj, w_in_p, name=f"in_proj_dx{l}")
        d_w_in[l] = [_matmul(tile, h, ta=True, name=f"in_proj_dw{l}_{j}") for j, tile in enumerate(dproj)]
        dx, dg_l = _rmsnorm_bwd(dh, x, attn_g[l:l + 1], dx, name=f"norm_bwd{l}")
        d_attn_g[l], d_sinks[l], d_gq[l], d_gkv[l] = dg_l, dsink[0:1, :SWA_HEADS], dgq_l, dgkv_l

    grads_lo = grads_by_chip(range(half))
    recv_lo, pair_lo = reduce_begin(grads_lo, "lo")
    reduced_lo = reduce_end(grads_lo, recv_lo, _comm_scatter_chips(pair_lo, name="comm_scatter_chips_lo"), "lo")
    cat = lambda parts: jnp.concatenate(parts, axis=0)
    reduced = [cat([lo, hi]) for lo, hi in zip(reduced_lo, reduced_hi)]
    return loss, dx, cat(d_attn_g), cat(d_sinks), cat(d_gq), cat(d_gkv), d_final_g, reduced


def kernel(x, attn_norm_g, w_in, swa_sinks, q_a_norm_g, kv_a_norm_g, w_q_b, w_kv_b, w_out, final_norm_g, loss_target, m_attn_norm_g, m_w_in, m_swa_sinks, m_q_a_norm_g, m_kv_a_norm_g, m_w_q_b, m_w_kv_b, m_w_out, m_final_norm_g, v_attn_norm_g, v_w_in, v_swa_sinks, v_q_a_norm_g, v_kv_a_norm_g, v_w_q_b, v_w_kv_b, v_w_out, v_final_norm_g):
    x_, y_, c = lax.axis_index("x"), lax.axis_index("y"), lax.axis_index("c")
    chip = 2 * x_ + y_
    where = jnp.stack([chip, 2 * (1 - x_) + y_, 2 * x_ + 1 - y_, 2 * (1 - x_) + 1 - y_, c]).astype(jnp.int32)

    sent = [a.astype(WIRE_DTYPE) for a in (_t(w_in), _t(w_q_b), _t(w_kv_b), w_out)]

    loss, dx, d_attn_g, d_sinks, d_gq, d_gkv, d_final_g, reduced = _device_step(
        x[0], loss_target[0], attn_norm_g, swa_sinks, q_a_norm_g, kv_a_norm_g, final_norm_g.reshape(1, -1), sent, where)
    g_w_in, g_w_q_b, g_w_kv_b, g_w_out = _t(reduced[0]), _t(reduced[1]), _t(reduced[2]), reduced[3]

    small = [d_attn_g, d_sinks, d_gq, d_gkv, d_final_g, loss[:, :1]]
    flat = jnp.concatenate([a.reshape(-1) for a in small])
    n_small = flat.shape[0]
    rows = -(-n_small // 1024) * 8
    total = _comm_allreduce_small(jnp.pad(flat, (0, rows * 128 - n_small)).reshape(rows, 128),
                                  name="comm_allreduce_small").reshape(-1)
    outs, at = [], 0
    for a in small:
        outs.append(total[at:at + a.size].reshape(a.shape))
        at += a.size
    g_attn_g, g_sinks, g_gq, g_gkv, g_final_g, loss_total = outs
    g_final_g = g_final_g.reshape(final_norm_g.shape)

    weights = [attn_norm_g, w_in, swa_sinks, q_a_norm_g, kv_a_norm_g, w_q_b, w_kv_b, w_out, final_norm_g]
    grads = [g_attn_g, g_w_in, g_sinks, g_gq, g_gkv, g_w_q_b, g_w_kv_b, g_w_out, g_final_g]
    ms = [m_attn_norm_g, m_w_in, m_swa_sinks, m_q_a_norm_g, m_kv_a_norm_g, m_w_q_b, m_w_kv_b, m_w_out, m_final_norm_g]
    vs = [v_attn_norm_g, v_w_in, v_swa_sinks, v_q_a_norm_g, v_kv_a_norm_g, v_w_q_b, v_w_kv_b, v_w_out, v_final_norm_g]
    as2d = lambda a: a.reshape(1, -1) if a.ndim == 1 else a
    deltas, new_m, new_v = [], [], []
    for i, (w, g, m, v) in enumerate(zip(weights, grads, ms, vs)):
        view = _t if w is w_in else as2d
        d, mn, vn = _adamw(view(w), reduced[0] if w is w_in else view(g), view(m), view(v), name=f"adamw{i}")
        back = _t if w is w_in else (lambda a: a.reshape(w.shape))
        deltas.append(back(d))
        new_m.append(back(mn))
        new_v.append(back(vn))

    return (loss_total.reshape(()), dx[None], *grads, *deltas, *new_m, *new_v)
```

```python
import functools
import math

import jax
import jax.numpy as jnp
from jax import lax
from jax.experimental import pallas as pl
from jax.experimental.pallas import tpu as pltpu

F32 = jnp.float32
MXU_DTYPE = jnp.bfloat16
WIRE_DTYPE = jnp.bfloat16

EPS = 1e-6
NEG = -1e30
BLOCK = 128
D_MODEL = 2048
SWA_HEADS = 16
MLA_HEADS = 8
Q_RANK = 384
KV_RANK = 256
IN_WIDTH = 4032
MLA_SCALE = 192 ** -0.5
SWA_SCALE = 64 ** -0.5
LOG2E = math.log2(math.e)
HEADS_PER_STEP = 2
SLOPES = tuple(2.0 ** (-8.0 * (h + 1) / SWA_HEADS) for h in range(SWA_HEADS))

P_WIDTH = 4096
QA_OFF, GA_OFF, GB_OFF, CQ_OFF, KA_OFF, CKV_OFF, VA_OFF, KR_OFF = 0, 1024, 2048, 3072, 3456, 3584, 3840, 3968

ADAM_LR, ADAM_B1, ADAM_B2, ADAM_EPS, ADAM_WD, ADAM_STEP = 0.001, 0.9, 0.999, 1e-08, 0.01, 10

VMEM_LIMIT = 56 * 1024 * 1024
MESH = pl.DeviceIdType.MESH
ANY = pl.BlockSpec(memory_space=pl.ANY)


def _cparams(*sem):
    return pltpu.CompilerParams(dimension_semantics=sem, vmem_limit_bytes=VMEM_LIMIT)


def _dot(a, b, ca, cb):
    return lax.dot_general(a, b, (((ca,), (cb,)), ((), ())), preferred_element_type=F32)


def _layer_spec(block, index_map, layer):
    if layer is None:
        return pl.BlockSpec(block, index_map)
    return pl.BlockSpec((None,) + tuple(block), lambda *g: (layer,) + tuple(index_map(*g)))


def _matmul(a, b, *, name, ta=False, tb=False, out_dtype=F32, add=None, b_layer=None, tm=1024, tn=1024, tk=2048):
    (kdim, m) = a.shape if ta else a.shape[::-1]
    (n, k2) = b.shape[-2:] if tb else b.shape[-2:][::-1]
    assert kdim == k2, (a.shape, b.shape)
    tm, tn, tk = min(tm, m), min(tn, n), min(tk, kdim)
    assert m % tm == 0 and n % tn == 0 and kdim % tk == 0
    nk = kdim // tk

    def body(*refs):
        a_ref, b_ref = refs[:2]
        add_ref = None if add is None else refs[2]
        o_ref = refs[2 + (add is not None)]
        part = _dot(a_ref[...].astype(MXU_DTYPE), b_ref[...].astype(MXU_DTYPE), 0 if ta else 1, 1 if tb else 0)

        def finish(r):
            o_ref[...] = (r if add is None else add_ref[...] + r).astype(out_dtype)

        if nk == 1:
            finish(part)
            return
        acc = refs[-1]
        k = pl.program_id(2)

        @pl.when(k == 0)
        def _():
            acc[...] = part

        @pl.when((k > 0) & (k < nk - 1))
        def _():
            acc[...] += part

        @pl.when(k == nk - 1)
        def _():
            finish(acc[...] + part)

    a_spec = pl.BlockSpec((tk, tm), lambda i, j, k: (k, i)) if ta else pl.BlockSpec((tm, tk), lambda i, j, k: (i, k))
    b_spec = (_layer_spec((tn, tk), lambda i, j, k: (j, k), b_layer) if tb else
              _layer_spec((tk, tn), lambda i, j, k: (k, j), b_layer))
    in_specs, args = [a_spec, b_spec], [a, b]
    if add is not None:
        in_specs.append(pl.BlockSpec((tm, tn), lambda i, j, k: (i, j)))
        args.append(add)
    return pl.pallas_call(
        body, name=name, grid=(m // tm, n // tn, nk), in_specs=in_specs,
        out_specs=pl.BlockSpec((tm, tn), lambda i, j, k: (i, j)),
        out_shape=jax.ShapeDtypeStruct((m, n), out_dtype),
        scratch_shapes=[pltpu.VMEM((tm, tn), F32)] if nk > 1 else [],
        compiler_params=_cparams("parallel", "parallel", "arbitrary"),
    )(*args)


def _matmul_ktiles(a_tiles, b, *, name, b_layer=None, tm=512, tn=1024):
    m, kt = a_tiles[0].shape
    n = b.shape[-1]
    nt = len(a_tiles)
    assert b.shape[-2] == nt * kt
    tm, tn = min(tm, m), min(tn, n)
    assert m % tm == 0 and n % tn == 0

    def body(*refs):
        a_refs, b_refs, o_ref = refs[:nt], refs[nt:2 * nt], refs[2 * nt]
        acc = _dot(a_refs[0][...].astype(MXU_DTYPE), b_refs[0][...].astype(MXU_DTYPE), 1, 0)
        for j in range(1, nt):
            acc += _dot(a_refs[j][...].astype(MXU_DTYPE), b_refs[j][...].astype(MXU_DTYPE), 1, 0)
        o_ref[...] = acc

    in_specs = [pl.BlockSpec((tm, kt), lambda jn, i: (i, 0))] * nt
    in_specs += [_layer_spec((kt, tn), lambda jn, i, j=j: (j, jn), b_layer) for j in range(nt)]
    return pl.pallas_call(
        body, name=name, grid=(n // tn, m // tm), in_specs=in_specs,
        out_specs=pl.BlockSpec((tm, tn), lambda jn, i: (i, jn)),
        out_shape=jax.ShapeDtypeStruct((m, n), F32),
        compiler_params=_cparams("parallel", "parallel"),
    )(*a_tiles, *([b] * nt))


def _rmsnorm_fwd(x, g, *, name):
    s, d = x.shape
    tm = min(512, s)

    def body(x_ref, g_ref, h_ref):
        xv = x_ref[...]
        r = lax.rsqrt(jnp.mean(xv * xv, axis=-1, keepdims=True) + EPS)
        h_ref[...] = (xv * r * g_ref[...]).astype(MXU_DTYPE)

    return pl.pallas_call(
        body, name=name, grid=(s // tm,),
        in_specs=[pl.BlockSpec((tm, d), lambda i: (i, 0)), pl.BlockSpec((1, d), lambda i: (0, 0))],
        out_specs=pl.BlockSpec((tm, d), lambda i: (i, 0)),
        out_shape=jax.ShapeDtypeStruct((s, d), MXU_DTYPE),
        compiler_params=_cparams("parallel"),
    )(x, g)


def _rmsnorm_bwd(dh, x, g, dres, *, name):
    s, d = x.shape
    tm = min(512, s)

    def body(dh_ref, x_ref, g_ref, dres_ref, dx_ref, dg_ref):
        @pl.when(pl.program_id(0) == 0)
        def _():
            dg_ref[...] = jnp.zeros_like(dg_ref)

        xv = x_ref[...]
        r = lax.rsqrt(jnp.mean(xv * xv, axis=-1, keepdims=True) + EPS)
        xn = xv * r
        dy = dh_ref[...]
        dg_ref[...] += jnp.sum(dy * xn, axis=0, keepdims=True)
        u = dy * g_ref[...]
        dx_ref[...] = dres_ref[...] + r * (u - xn * jnp.mean(u * xn, axis=-1, keepdims=True))

    row = pl.BlockSpec((tm, d), lambda i: (i, 0))
    vec = pl.BlockSpec((1, d), lambda i: (0, 0))
    return pl.pallas_call(
        body, name=name, grid=(s // tm,), in_specs=[row, row, vec, row], out_specs=[row, vec],
        out_shape=[jax.ShapeDtypeStruct((s, d), F32), jax.ShapeDtypeStruct((1, d), F32)],
        compiler_params=_cparams("arbitrary"),
    )(dh, x, g, dres)


def _final_loss(x, g, tgt, *, name):
    s, d = x.shape
    tm = min(512, s)

    def body(x_ref, g_ref, t_ref, dx_ref, dg_ref, loss_ref):
        @pl.when(pl.program_id(0) == 0)
        def _():
            dg_ref[...] = jnp.zeros_like(dg_ref)
            loss_ref[...] = jnp.zeros_like(loss_ref)

        xv = x_ref[...]
        gv = g_ref[...]
        r = lax.rsqrt(jnp.mean(xv * xv, axis=-1, keepdims=True) + EPS)
        xn = xv * r
        err = xn * gv - t_ref[...]
        sq = jnp.sum(jnp.sum(err * err, axis=-1, keepdims=True), axis=0, keepdims=True)
        loss_ref[...] += (0.5 / d) * sq
        dy = err * (1.0 / d)
        dg_ref[...] += jnp.sum(dy * xn, axis=0, keepdims=True)
        u = dy * gv
        dx_ref[...] = r * (u - xn * jnp.mean(u * xn, axis=-1, keepdims=True))

    row = pl.BlockSpec((tm, d), lambda i: (i, 0))
    vec = pl.BlockSpec((1, d), lambda i: (0, 0))
    return pl.pallas_call(
        body, name=name, grid=(s // tm,), in_specs=[row, vec, row],
        out_specs=[row, vec, pl.BlockSpec((1, 128), lambda i: (0, 0))],
        out_shape=[jax.ShapeDtypeStruct((s, d), F32), jax.ShapeDtypeStruct((1, d), F32),
                   jax.ShapeDtypeStruct((1, 128), F32)],
        compiler_params=_cparams("arbitrary"),
    )(x, g, tgt)


def _swa_keys(kp_ref, kc_ref):
    kk = jnp.concatenate([kp_ref[...], kc_ref[...]], axis=0)
    kr = pltpu.roll(kk, 64, 1)
    lo = lax.broadcasted_iota(jnp.int32, kk.shape, 1) < 64
    return [jnp.where(lo, kk, kr).astype(MXU_DTYPE), jnp.where(lo, kr, kk).astype(MXU_DTYPE)]


GROUP = SWA_HEADS // 2


def _swa_mask(n):
    qi = lax.broadcasted_iota(jnp.int32, (BLOCK, 2 * BLOCK), 0)
    ki = lax.broadcasted_iota(jnp.int32, (BLOCK, 2 * BLOCK), 1)
    delta = BLOCK + qi - ki
    valid = (delta >= 0) & (delta < BLOCK) & ((ki >= BLOCK) | (n > 0))
    return valid, delta.astype(F32)


def _stack_heads(ref, j):
    lo = lax.broadcasted_iota(jnp.int32, (BLOCK, BLOCK), 1) < 64
    parts = []
    for r in range(GROUP):
        pair = (GROUP * j + r) // 2
        blk = ref[:, pair * 128:(pair + 1) * 128].astype(F32)
        parts.append(jnp.where(lo if r % 2 == 0 else ~lo, blk, 0.0).astype(MXU_DTYPE))
    return jnp.concatenate(parts, axis=0)


def _unstack_heads(stacked, ref, j):
    lo = lax.broadcasted_iota(jnp.int32, (BLOCK, BLOCK), 1) < 64
    for i in range(GROUP // 2):
        pair = (GROUP * j) // 2 + i
        even, odd = stacked[2 * i * BLOCK:(2 * i + 1) * BLOCK], stacked[(2 * i + 1) * BLOCK:(2 * i + 2) * BLOCK]
        ref[:, pair * 128:(pair + 1) * 128] = jnp.where(lo, even, odd).astype(ref.dtype)


def _head_rows(stacked, r):
    return stacked[r * BLOCK:(r + 1) * BLOCK]


def _swa_scores(raw, h, valid, deltaf):
    return jnp.where(valid, raw * (SWA_SCALE * LOG2E) - (SLOPES[h] * LOG2E) * deltaf, NEG)


def _swa_specs(nb):
    kcol, vcol = KA_OFF // BLOCK, VA_OFF // BLOCK
    last = nb - 1
    cur = lambda n: jnp.minimum(n, last)
    prev = lambda n: jnp.maximum(jnp.minimum(n, last) - 1, 0)
    return [
        pl.BlockSpec(memory_space=pltpu.SMEM),
        pl.BlockSpec((BLOCK, 1024), lambda n: (cur(n), QA_OFF // 1024)),
        pl.BlockSpec((BLOCK, BLOCK), lambda n: (cur(n), kcol)),
        pl.BlockSpec((BLOCK, BLOCK), lambda n: (prev(n), kcol)),
        pl.BlockSpec((BLOCK, BLOCK), lambda n: (cur(n), vcol)),
        pl.BlockSpec((BLOCK, BLOCK), lambda n: (prev(n), vcol)),
    ]


def _swa_fwd(proj, sinks, *, name):
    s = proj.shape[0]
    nb = s // BLOCK

    def body(sink_ref, q_ref, kc_ref, kp_ref, vc_ref, vp_ref, o_ref, lse_ref):
        n = pl.program_id(0)
        keys = _swa_keys(kp_ref, kc_ref)
        vals = _swa_keys(vp_ref, vc_ref)
        valid, deltaf = _swa_mask(n)
        lane = lax.broadcasted_iota(jnp.int32, (BLOCK, BLOCK), 1)
        lse_acc = jnp.zeros((BLOCK, BLOCK), F32)
        for j in range(2):
            raw = _dot(_stack_heads(q_ref, j), keys[j], 1, 1)
            probs = []
            for r in range(GROUP):
                h = GROUP * j + r
                sc = _swa_scores(_head_rows(raw, r), h, valid, deltaf)
                sink = sink_ref[h] * LOG2E
                m = jnp.maximum(jnp.max(sc, axis=-1, keepdims=True), sink)
                p = jnp.exp2(sc - m)
                l = jnp.sum(p, axis=-1, keepdims=True) + jnp.exp2(sink - m)
                probs.append((p * (1.0 / l)).astype(MXU_DTYPE))
                lse_acc = jnp.where(lane == h, m + jnp.log(l) * LOG2E, lse_acc)
            _unstack_heads(jnp.dot(jnp.concatenate(probs, axis=0), vals[j], preferred_element_type=F32), o_ref, j)
        lse_ref[...] = lse_acc

    return pl.pallas_call(
        body, name=name, grid=(nb,), in_specs=_swa_specs(nb),
        out_specs=[pl.BlockSpec((BLOCK, 1024), lambda n: (n, 0)), pl.BlockSpec((BLOCK, BLOCK), lambda n: (n, 0))],
        out_shape=[jax.ShapeDtypeStruct((s, 1024), F32), jax.ShapeDtypeStruct((s, BLOCK), F32)],
        compiler_params=_cparams("parallel"),
    )(sinks, proj, proj, proj, proj, proj)


def _swa_bwd(proj, sinks, lse, do, *, name):
    s = proj.shape[0]
    nb = s // BLOCK
    last = nb - 1

    def body(sink_ref, q_ref, kc_ref, kp_ref, vc_ref, vp_ref, lse_ref, do_ref,
             dq_ref, dk_ref, dv_ref, dsink_ref, carry_k, carry_v):
        n = pl.program_id(0)

        @pl.when(n == 0)
        def _():
            carry_k[...] = jnp.zeros_like(carry_k)
            carry_v[...] = jnp.zeros_like(carry_v)
            dsink_ref[...] = jnp.zeros_like(dsink_ref)

        @pl.when(n < nb)
        def _():
            keys = _swa_keys(kp_ref, kc_ref)
            vals = _swa_keys(vp_ref, vc_ref)
            valid, deltaf = _swa_mask(n)
            lane = lax.broadcasted_iota(jnp.int32, (BLOCK, BLOCK), 1)
            lane1 = lax.broadcasted_iota(jnp.int32, (1, BLOCK), 1)
            lse_blk = lse_ref[...]
            acc_k, acc_v = [], []
            dsink = jnp.zeros((1, BLOCK), F32)
            for j in range(2):
                q_all, do_all = _stack_heads(q_ref, j), _stack_heads(do_ref, j)
                raw = _dot(q_all, keys[j], 1, 1)
                dp_all = _dot(do_all, vals[j], 1, 1)
                probs, dscores = [], []
                for r in range(GROUP):
                    h = GROUP * j + r
                    lse_h = jnp.sum(jnp.where(lane == h, lse_blk, 0.0), axis=-1, keepdims=True)
                    p = jnp.exp2(_swa_scores(_head_rows(raw, r), h, valid, deltaf) - lse_h)
                    dp = _head_rows(dp_all, r)
                    dlt = jnp.sum(dp * p, axis=-1, keepdims=True)
                    dscores.append((p * (dp - dlt) * SWA_SCALE).astype(MXU_DTYPE))
                    probs.append(p.astype(MXU_DTYPE))
                    sunk = jnp.exp2(sink_ref[h] * LOG2E - lse_h) * dlt
                    dsink = jnp.where(lane1 == h, -jnp.sum(sunk, axis=0, keepdims=True), dsink)
                ds_all = jnp.concatenate(dscores, axis=0)
                _unstack_heads(jnp.dot(ds_all, keys[j], preferred_element_type=F32), dq_ref, j)
                acc_k.append(_dot(ds_all, q_all, 0, 0))
                acc_v.append(_dot(jnp.concatenate(probs, axis=0), do_all, 0, 0))
            lo2 = lax.broadcasted_iota(jnp.int32, (2 * BLOCK, BLOCK), 1) < 64
            fold = lambda acc: jnp.where(lo2, acc[0] + pltpu.roll(acc[0], 64, 1), acc[1] + pltpu.roll(acc[1], 64, 1))
            dkk, dvv = fold(acc_k), fold(acc_v)
            dk_ref[...] = (carry_k[...] + dkk[:BLOCK]).astype(dk_ref.dtype)
            dv_ref[...] = (carry_v[...] + dvv[:BLOCK]).astype(dv_ref.dtype)
            carry_k[...] = dkk[BLOCK:]
            carry_v[...] = dvv[BLOCK:]
            dsink_ref[...] += jnp.broadcast_to(dsink, dsink_ref.shape)

        @pl.when(n == nb)
        def _():
            dk_ref[...] = carry_k[...].astype(dk_ref.dtype)
            dv_ref[...] = carry_v[...].astype(dv_ref.dtype)

    cur = lambda n: jnp.minimum(n, last)
    lag = lambda n: jnp.maximum(n - 1, 0)
    return pl.pallas_call(
        body, name=name, grid=(nb + 1,),
        in_specs=_swa_specs(nb) + [pl.BlockSpec((BLOCK, BLOCK), lambda n: (cur(n), 0)),
                                   pl.BlockSpec((BLOCK, 1024), lambda n: (cur(n), 0))],
        out_specs=[pl.BlockSpec((BLOCK, 1024), lambda n: (cur(n), 0)),
                   pl.BlockSpec((BLOCK, BLOCK), lambda n: (lag(n), 0)),
                   pl.BlockSpec((BLOCK, BLOCK), lambda n: (lag(n), 0)),
                   pl.BlockSpec((8, BLOCK), lambda n: (0, 0))],
        out_shape=[jax.ShapeDtypeStruct((s, 1024), MXU_DTYPE), jax.ShapeDtypeStruct((s, BLOCK), MXU_DTYPE),
                   jax.ShapeDtypeStruct((s, BLOCK), MXU_DTYPE), jax.ShapeDtypeStruct((8, BLOCK), F32)],
        scratch_shapes=[pltpu.VMEM((BLOCK, BLOCK), F32), pltpu.VMEM((BLOCK, BLOCK), F32)],
        compiler_params=_cparams("arbitrary"),
    )(sinks, proj, proj, proj, proj, proj, lse, do)


def _rope_partner(v, first, width):
    lane = lax.broadcasted_iota(jnp.int32, v.shape, 1)
    in_a = (lane >= first) & (lane < first + 32)
    in_b = (lane >= first + 32) & (lane < first + 64)
    return jnp.where(in_a, pltpu.roll(v, width - 32, 1), jnp.where(in_b, pltpu.roll(v, 32, 1), 0.0))


def _mla_qkv_fwd(proj, gq, gkv, wq, wkv, tk_c, tk_s, *, layer, name):
    s = proj.shape[0]
    tm = min(256, s)

    def body(cq_ref, ckv_ref, kr_ref, gq_ref, gkv_ref, wq_ref, wkv_ref, kc_ref, ks_ref,
             qcat_ref, kcat_ref, v_ref, cqn_ref, ckvn_ref):
        cq = cq_ref[...]
        cqn = (cq * lax.rsqrt(jnp.mean(cq * cq, axis=-1, keepdims=True) + EPS) * gq_ref[...]).astype(MXU_DTYPE)
        cqn_ref[...] = cqn
        qpre = _dot(cqn, wq_ref[...], 1, 1)
        kc, ks = kc_ref[...], ks_ref[...]
        for hh in range(MLA_HEADS):
            qcat_ref[:, hh * 256:hh * 256 + 128] = qpre[:, hh * 256:hh * 256 + 128].astype(MXU_DTYPE)
            blk = qpre[:, hh * 256 + 128:(hh + 1) * 256]
            qcat_ref[:, hh * 256 + 128:(hh + 1) * 256] = (blk * kc + _rope_partner(blk, 0, 128) * ks).astype(MXU_DTYPE)
        ckv = ckv_ref[...]
        ckvn = (ckv * lax.rsqrt(jnp.mean(ckv * ckv, axis=-1, keepdims=True) + EPS) * gkv_ref[...]).astype(MXU_DTYPE)
        ckvn_ref[...] = ckvn
        kv = _dot(ckvn, wkv_ref[...], 1, 1)
        kr = kr_ref[...]
        krr = (kr * kc + _rope_partner(kr, 0, 128) * ks).astype(MXU_DTYPE)
        for hh in range(MLA_HEADS):
            kcat_ref[:, hh * 256:hh * 256 + 128] = kv[:, hh * 128:(hh + 1) * 128].astype(MXU_DTYPE)
            kcat_ref[:, hh * 256 + 128:(hh + 1) * 256] = krr
            v_ref[:, hh * 256:hh * 256 + 128] = kv[:, 1024 + hh * 128:1024 + (hh + 1) * 128].astype(MXU_DTYPE)
            v_ref[:, hh * 256 + 128:(hh + 1) * 256] = jnp.ones((tm, 128), MXU_DTYPE)

    row = lambda w, c: pl.BlockSpec((tm, w), lambda i: (i, c))
    full = lambda a: pl.BlockSpec(a.shape, lambda i: (0, 0))
    of_layer = lambda a: _layer_spec(a.shape[-2:], lambda i: (0, 0), layer)
    return pl.pallas_call(
        body, name=name, grid=(s // tm,),
        in_specs=[row(Q_RANK, CQ_OFF // Q_RANK), row(KV_RANK, CKV_OFF // KV_RANK), row(128, KR_OFF // 128),
                  full(gq), full(gkv), of_layer(wq), of_layer(wkv), row(128, 0), row(128, 0)],
        out_specs=[row(2048, 0), row(2048, 0), row(2048, 0), row(Q_RANK, 0), row(KV_RANK, 0)],
        out_shape=[jax.ShapeDtypeStruct((s, 2048), MXU_DTYPE), jax.ShapeDtypeStruct((s, 2048), MXU_DTYPE),
                   jax.ShapeDtypeStruct((s, 2048), MXU_DTYPE), jax.ShapeDtypeStruct((s, Q_RANK), MXU_DTYPE),
                   jax.ShapeDtypeStruct((s, KV_RANK), MXU_DTYPE)],
        compiler_params=_cparams("parallel"),
    )(proj, proj, proj, gq, gkv, wq, wkv, tk_c, tk_s)


def _norm_bwd(x, g, dy):
    r = lax.rsqrt(jnp.mean(x * x, axis=-1, keepdims=True) + EPS)
    xn = x * r
    u = dy * g
    return r * (u - xn * jnp.mean(u * xn, axis=-1, keepdims=True)), jnp.sum(dy * xn, axis=0, keepdims=True)


def _mla_qkv_bwd(proj, cqn, ckvn, dqcat, dkcat, dv, dka, dva, gq, gkv, wq, wkv, tk_c, tk_s, *, layer, name):
    s = proj.shape[0]
    tm = min(256, s)
    t_cq, t_ka, t_ckv, t_va, t_kr = (o - CQ_OFF for o in (CQ_OFF, KA_OFF, CKV_OFF, VA_OFF, KR_OFF))

    def body(cq_ref, ckv_ref, cqn_ref, ckvn_ref, dq_ref, dk_ref, dv_ref, dka_ref, dva_ref, gq_ref, gkv_ref, wq_ref,
             wkv_ref, kc_ref, ks_ref,
             tile_ref, dwq_ref, dwkv_ref, dgq_ref, dgkv_ref, dqpre, dkv, dwq_acc, dwkv_acc):
        dcq_ref = tile_ref.at[:, t_cq:t_cq + Q_RANK]
        dckv_ref = tile_ref.at[:, t_ckv:t_ckv + KV_RANK]
        dkr_ref = tile_ref.at[:, t_kr:t_kr + 128]
        tile_ref[:, t_ka:t_ka + 128] = dka_ref[...]
        tile_ref[:, t_va:t_va + 128] = dva_ref[...]

        @pl.when(pl.program_id(0) == 0)
        def _():
            for r in (dwq_acc, dwkv_acc, dgq_ref, dgkv_ref):
                r[...] = jnp.zeros_like(r)

        kc, ks = kc_ref[...], ks_ref[...]
        dkrr = jnp.zeros((tm, 128), F32)
        for hh in range(MLA_HEADS):
            dqpre[:, hh * 256:hh * 256 + 128] = dq_ref[:, hh * 256:hh * 256 + 128].astype(MXU_DTYPE)
            blk = dq_ref[:, hh * 256 + 128:(hh + 1) * 256]
            dqpre[:, hh * 256 + 128:(hh + 1) * 256] = (blk * kc + _rope_partner(blk * ks, 0, 128)).astype(MXU_DTYPE)
            dkv[:, hh * 128:(hh + 1) * 128] = dk_ref[:, hh * 256:hh * 256 + 128].astype(MXU_DTYPE)
            dkrr = dkrr + dk_ref[:, hh * 256 + 128:(hh + 1) * 256]
        dkv[:, 1024:] = dv_ref[...].astype(MXU_DTYPE)
        dkr_ref[...] = (dkrr * kc + _rope_partner(dkrr * ks, 0, 128)).astype(dkr_ref.dtype)

        dq_b = dqpre[...]
        dwq_acc[...] += _dot(cqn_ref[...], dq_b, 0, 0)
        dcq, dgq = _norm_bwd(cq_ref[...], gq_ref[...], _dot(dq_b, wq_ref[...], 1, 0))
        dcq_ref[...] = dcq.astype(dcq_ref.dtype)
        dgq_ref[...] += dgq

        dkv_b = dkv[...]
        dwkv_acc[...] += _dot(ckvn_ref[...], dkv_b, 0, 0)
        dckv, dgkv = _norm_bwd(ckv_ref[...], gkv_ref[...], _dot(dkv_b, wkv_ref[...], 1, 0))
        dckv_ref[...] = dckv.astype(dckv_ref.dtype)
        dgkv_ref[...] += dgkv

        @pl.when(pl.program_id(0) == s // tm - 1)
        def _():
            dwq_ref[...] = dwq_acc[...].T
            dwkv_ref[...] = dwkv_acc[...].T

    row = lambda w, c: pl.BlockSpec((tm, w), lambda i: (i, c))
    full = lambda shape: pl.BlockSpec(shape, lambda i: (0, 0))
    of_layer = lambda a: _layer_spec(a.shape[-2:], lambda i: (0, 0), layer)
    return pl.pallas_call(
        body, name=name, grid=(s // tm,),
        in_specs=[row(Q_RANK, CQ_OFF // Q_RANK), row(KV_RANK, CKV_OFF // KV_RANK), row(Q_RANK, 0), row(KV_RANK, 0),
                  row(2048, 0), row(2048, 0), row(1024, 0), row(128, 0), row(128, 0), full(gq.shape), full(gkv.shape),
                  of_layer(wq), of_layer(wkv), row(128, 0), row(128, 0)],
        out_specs=[row(1024, 0), full(wq.shape[-2:]), full(wkv.shape[-2:]), full(gq.shape), full(gkv.shape)],
        out_shape=[jax.ShapeDtypeStruct((s, 1024), MXU_DTYPE), jax.ShapeDtypeStruct(wq.shape[-2:], F32),
                   jax.ShapeDtypeStruct(wkv.shape[-2:], F32), jax.ShapeDtypeStruct(gq.shape, F32),
                   jax.ShapeDtypeStruct(gkv.shape, F32)],
        scratch_shapes=[pltpu.VMEM((tm, 2048), MXU_DTYPE), pltpu.VMEM((tm, 2048), MXU_DTYPE),
                        pltpu.VMEM((Q_RANK, 2048), F32), pltpu.VMEM((KV_RANK, 2048), F32)],
        compiler_params=_cparams("arbitrary"),
    )(proj, proj, cqn, ckvn, dqcat, dkcat, dv, dka, dva, gq, gkv, wq, wkv, tk_c, tk_s)


def _loop_by_two(lo, hi, step):
    def pair(i, carry):
        step(lo + 2 * i)
        step(lo + 2 * i + 1)
        return carry

    lax.fori_loop(0, (hi - lo) // 2, pair, 0)

    @pl.when((hi - lo) % 2 == 1)
    def _():
        step(hi - 1)


def _causal_mask(t):
    return lax.broadcasted_iota(jnp.int32, (t, t), 1) <= lax.broadcasted_iota(jnp.int32, (t, t), 0)


def _mla_fwd(qcat, kcat, v, *, name, gather=None):
    s = qcat.shape[0]
    t = min(512, s)
    nq = s // t
    hp = HEADS_PER_STEP
    ng = MLA_HEADS // hp
    c2 = MLA_SCALE * LOG2E

    nw = 0 if gather is None else len(gather)

    def body(q_ref, k_ref, v_ref, *rest):
        src, (o_ref, lse_ref), out = rest[:nw], rest[nw:nw + 2], rest[nw + 2:2 * nw + 2]
        top_s, acc_s, *sems = rest[2 * nw + 2:]
        g, qi = pl.program_id(0), pl.program_id(1)
        if nw:
            @pl.when((g == 0) & (qi == 0))
            def _():
                _gather_start(src, out, *sems)

            @pl.when((g == ng - 1) & (qi == 0))
            def _():
                _gather_forward(src, out, *sems)

        rows = lambda j: pl.ds(pl.multiple_of(j * t, t), t)
        head = lambda e: slice(e * 256, (e + 1) * 256)
        raw = lambda e, j: _dot(q_ref[:, head(e)], k_ref[rows(j), head(e)], 1, 1)

        for e in range(hp):
            top_s[e] = jnp.where(_causal_mask(t), raw(e, qi), NEG)

        def pass1(j):
            for e in range(hp):
                top_s[e] = jnp.maximum(top_s[e], raw(e, j))

        _loop_by_two(0, qi, pass1)
        m = [jnp.max(top_s[e], axis=-1, keepdims=True) * c2 for e in range(hp)]

        def weighted(e, j, masked):
            sc = raw(e, j) * c2 - m[e]
            if masked:
                sc = jnp.where(_causal_mask(t), sc, NEG)
            return jnp.dot(jnp.exp2(sc).astype(MXU_DTYPE), v_ref[rows(j), head(e)], preferred_element_type=F32)

        for e in range(hp):
            acc_s[e] = weighted(e, qi, True)

        def pass2(j):
            for e in range(hp):
                acc_s[e] += weighted(e, j, False)

        _loop_by_two(0, qi, pass2)
        lane = lax.broadcasted_iota(jnp.int32, (t, 128), 1)
        stats = jnp.zeros((t, 128), F32)
        for e in range(hp):
            l = acc_s[e, :, 128:]
            o_ref[:, e * 128:(e + 1) * 128] = acc_s[e, :, :128] / l
            stats = jnp.where(lane == e, m[e] + jnp.log(l) * LOG2E, stats)
        lse_ref[...] = stats
        if nw:
            @pl.when((g == ng - 1) & (qi == nq - 1))
            def _():
                _gather_finish(src, out, *sems)

    outs = pl.pallas_call(
        body, name=name, grid=(ng, nq),
        in_specs=[pl.BlockSpec((t, 256 * hp), lambda g, qi: (qi, g)), pl.BlockSpec((s, 256 * hp), lambda g, qi: (0, g)),
                  pl.BlockSpec((s, 256 * hp), lambda g, qi: (0, g))] + [ANY] * nw,
        out_specs=[pl.BlockSpec((t, 128 * hp), lambda g, qi: (qi, g)), pl.BlockSpec((t, 128), lambda g, qi: (qi, g))]
        + [ANY] * nw,
        out_shape=[jax.ShapeDtypeStruct((s, 1024), F32), jax.ShapeDtypeStruct((s, 128 * ng), F32)]
        + (_gathered_shapes(gather) if nw else []),
        scratch_shapes=[pltpu.VMEM((hp, t, t), F32), pltpu.VMEM((hp, t, 256), F32)] + (_sems(6, nw) if nw else []),
        compiler_params=_cparams("arbitrary", "arbitrary"),
    )(qcat, kcat, v, *(gather or []))
    return outs[0], outs[1], list(outs[2:])


def _mla_bwd(qcat, kcat, v, do, lse, delta, *, name, scatter=None):
    s = qcat.shape[0]
    t = min(512, s)
    nq = s // t
    hp = HEADS_PER_STEP
    c2 = MLA_SCALE * LOG2E
    nw = 0 if scatter is None else len(scatter)

    def body(q_ref, k_ref, v_ref, do_ref, lse_ref, dl_ref, *rest):
        src, (dq_ref, dk_ref, dv_ref), out = rest[:nw], rest[nw:nw + 3], rest[nw + 3:2 * nw + 3]
        dk_acc, dv_acc, *sems = rest[2 * nw + 3:]
        h, ki = pl.program_id(0), pl.program_id(1)
        if nw:
            @pl.when((h == 0) & (ki == 0))
            def _():
                _scatter_start(src, out, *sems)

        @pl.when(ki == 0)
        def _():
            dq_ref[...] = jnp.zeros_like(dq_ref)

        dk_acc[...] = jnp.zeros_like(dk_acc)
        dv_acc[...] = jnp.zeros_like(dv_acc)
        k, vv = k_ref[...], v_ref[...]
        mine = lax.broadcasted_iota(jnp.int32, (t, 128), 1) == h % hp

        def chunk(qi, masked):
            rows = pl.ds(pl.multiple_of(qi * t, t), t)
            q, dob = q_ref[rows, :], do_ref[rows, :]
            pick = lambda r: jnp.sum(jnp.where(mine, r[rows, :], 0.0), axis=-1, keepdims=True)
            sc = _dot(q, k, 1, 1) * c2
            if masked:
                sc = jnp.where(_causal_mask(t), sc, NEG)
            p = jnp.exp2(sc - pick(lse_ref))
            dp = _dot(dob, vv, 1, 1)
            ds = (p * (dp - pick(dl_ref)) * MLA_SCALE).astype(MXU_DTYPE)
            dv_acc[...] += _dot(dob, p.astype(MXU_DTYPE), 0, 0)
            dk_acc[...] += _dot(q, ds, 0, 0)
            dq_ref[rows, :] += jnp.dot(ds, k, preferred_element_type=F32)

        chunk(ki, True)
        _loop_by_two(ki + 1, nq, lambda qi: chunk(qi, False))
        dk_ref[...] = dk_acc[...].T
        dv_ref[...] = dv_acc[...].T
        if nw:
            @pl.when((h == MLA_HEADS - 1) & (ki == nq - 1))
            def _():
                _scatter_finish(src, out, *sems)

    head = lambda w: pl.BlockSpec((s, w), lambda h, ki: (0, h))
    blk = lambda w: pl.BlockSpec((t, w), lambda h, ki: (ki, h))
    stat = pl.BlockSpec((s, 128), lambda h, ki: (0, h // hp))
    outs = pl.pallas_call(
        body, name=name, grid=(MLA_HEADS, nq),
        in_specs=[head(256), blk(256), pl.BlockSpec((t, 128), lambda h, ki: (ki, 2 * h)), head(128), stat, stat]
        + [ANY] * nw,
        out_specs=[head(256), blk(256), blk(128)] + [ANY] * nw,
        out_shape=[jax.ShapeDtypeStruct((s, 2048), F32), jax.ShapeDtypeStruct((s, 2048), F32),
                   jax.ShapeDtypeStruct((s, 1024), F32)] + [jax.ShapeDtypeStruct(a.shape, a.dtype) for a in scatter or []],
        scratch_shapes=[pltpu.VMEM((256, t), F32), pltpu.VMEM((128, t), F32)] + (_sems(3, nw) if nw else []),
        compiler_params=_cparams("arbitrary", "arbitrary"),
    )(qcat, kcat, v, do, lse, delta, *(scatter or []))
    return outs[0], outs[1], outs[2], list(outs[3:])


def _gate_specs(tm):
    half = lambda c: pl.BlockSpec((tm, 1024), lambda i: (i, c))
    return half(0), half(GA_OFF // 1024), half(GB_OFF // 1024)


def _gate_fwd(oa, ob, proj, *, name):
    s = oa.shape[0]
    tm = min(512, s)

    def body(oa_ref, ob_ref, ga_ref, gb_ref, y_ref):
        ga, gb = ga_ref[...], gb_ref[...]
        y_ref[:, :1024] = (oa_ref[...] * (ga * jax.nn.sigmoid(ga))).astype(MXU_DTYPE)
        y_ref[:, 1024:] = (ob_ref[...] * (gb * jax.nn.sigmoid(gb))).astype(MXU_DTYPE)

    o_spec, ga_spec, gb_spec = _gate_specs(tm)
    return pl.pallas_call(
        body, name=name, grid=(s // tm,), in_specs=[o_spec, o_spec, ga_spec, gb_spec],
        out_specs=pl.BlockSpec((tm, 2048), lambda i: (i, 0)),
        out_shape=jax.ShapeDtypeStruct((s, 2048), MXU_DTYPE),
        compiler_params=_cparams("parallel"),
    )(oa, ob, proj, proj)


def _gate_bwd(dy, oa, ob, proj, *, name):
    s = oa.shape[0]
    tm = min(512, s)

    def body(dy_ref, oa_ref, ob_ref, ga_ref, gb_ref, doa_ref, dob_ref, dga_ref, dgb_ref, dl_ref):
        def branch(dyv, o, g, do_ref, dg_ref):
            sg = jax.nn.sigmoid(g)
            do = dyv * (g * sg)
            do_ref[...] = do.astype(MXU_DTYPE)
            dg_ref[...] = (dyv * o * (sg * (1.0 + g * (1.0 - sg)))).astype(MXU_DTYPE)
            return do

        branch(dy_ref[:, :1024], oa_ref[...], ga_ref[...], doa_ref, dga_ref)
        ob = ob_ref[...]
        prod = branch(dy_ref[:, 1024:], ob, gb_ref[...], dob_ref, dgb_ref) * ob
        lane = lax.broadcasted_iota(jnp.int32, (tm, stat_w), 1)
        acc = jnp.zeros((tm, stat_w), F32)
        for hh in range(MLA_HEADS):
            at = (hh // HEADS_PER_STEP) * 128 + hh % HEADS_PER_STEP
            acc = jnp.where(lane == at, jnp.sum(prod[:, hh * 128:(hh + 1) * 128], axis=-1, keepdims=True), acc)
        dl_ref[...] = acc

    stat_w = 128 * (MLA_HEADS // HEADS_PER_STEP)
    o_spec, ga_spec, gb_spec = _gate_specs(tm)
    return pl.pallas_call(
        body, name=name, grid=(s // tm,),
        in_specs=[pl.BlockSpec((tm, 2048), lambda i: (i, 0)), o_spec, o_spec, ga_spec, gb_spec],
        out_specs=[o_spec, o_spec, o_spec, o_spec, pl.BlockSpec((tm, stat_w), lambda i: (i, 0))],
        out_shape=[jax.ShapeDtypeStruct((s, 1024), MXU_DTYPE)] * 4 + [jax.ShapeDtypeStruct((s, stat_w), F32)],
        compiler_params=_cparams("parallel"),
    )(dy, oa, ob, proj, proj)


def _row_block(rows, cols, itemsize=4, budget=2 << 20):
    fits = [tr for tr in range(16, rows + 1, 16) if rows % tr == 0 and tr * cols * itemsize <= budget]
    return fits[-1] if fits else rows


def _adamw(w, g, m, v, *, name):
    shape = w.shape
    rows, cols = shape[-2:]
    w3, g3, m3, v3 = (a.reshape((-1, rows, cols)) for a in (w, g, m, v))
    lead = w3.shape[0]
    tr = _row_block(rows, cols)

    def body(w_ref, g_ref, m_ref, v_ref, d_ref, mo_ref, vo_ref):
        gv = g_ref[...]
        mn = ADAM_B1 * m_ref[...] + (1.0 - ADAM_B1) * gv
        vn = ADAM_B2 * v_ref[...] + (1.0 - ADAM_B2) * jnp.square(gv)
        m_hat = mn / (1.0 - ADAM_B1 ** ADAM_STEP)
        v_hat = vn / (1.0 - ADAM_B2 ** ADAM_STEP)
        d_ref[...] = -ADAM_LR * (m_hat / (jnp.sqrt(v_hat) + ADAM_EPS) + ADAM_WD * w_ref[...])
        mo_ref[...] = mn
        vo_ref[...] = vn

    spec = pl.BlockSpec((None, tr, cols), lambda a, i: (a, i, 0))
    outs = pl.pallas_call(
        body, name=name, grid=(lead, rows // tr), in_specs=[spec] * 4, out_specs=[spec] * 3,
        out_shape=[jax.ShapeDtypeStruct((lead, rows, cols), F32)] * 3,
        compiler_params=_cparams("parallel", "parallel"),
    )(w3, g3, m3, v3)
    return tuple(o.reshape(shape) for o in outs)


def _pair_sum(where, grads, recv, *, name):
    layers, chips, _, rows, cols = grads.shape
    tr = _row_block(rows, cols)

    def body(where_ref, a_ref, b_ref, o_ref):
        o_ref[...] = (a_ref[...] + b_ref[...].astype(F32)).astype(WIRE_DTYPE)

    spec = pl.BlockSpec((None, None, tr, cols), lambda a, k, i, w: (a, k, i, 0))
    return pl.pallas_call(
        body, name=name,
        grid_spec=pltpu.PrefetchScalarGridSpec(
            num_scalar_prefetch=1, grid=(layers, chips, rows // tr),
            in_specs=[pl.BlockSpec((None, None, None, tr, cols), lambda a, k, i, w: (a, k, w[4], i, 0)), spec],
            out_specs=spec),
        out_shape=jax.ShapeDtypeStruct((layers, chips, rows, cols), WIRE_DTYPE),
        compiler_params=_cparams("parallel", "parallel", "parallel"),
    )(where, grads, recv)


def _chip_sum(where, grads, recv, parts, *, name):
    layers, _, _, rows, cols = grads.shape
    tr = _row_block(rows, cols)

    def body(where_ref, a_ref, b_ref, t0_ref, t1_ref, t2_ref, o_ref):
        total = a_ref[...] + b_ref[...].astype(F32)
        for t_ref in (t0_ref, t1_ref, t2_ref):
            total = total + t_ref[...].astype(F32)
        o_ref[...] = total

    slot = lambda j: pl.BlockSpec((None, None, tr, cols), lambda a, i, w: (a, w[j], i, 0))
    return pl.pallas_call(
        body, name=name,
        grid_spec=pltpu.PrefetchScalarGridSpec(
            num_scalar_prefetch=1, grid=(layers, rows // tr),
            in_specs=[pl.BlockSpec((None, None, None, tr, cols), lambda a, i, w: (a, w[0], w[4], i, 0)), slot(0), slot(1),
                      slot(2), slot(3)],
            out_specs=slot(4)),
        out_shape=jax.ShapeDtypeStruct((layers, 2, rows, cols), F32),
        compiler_params=_cparams("parallel", "parallel"),
    )(where, grads, recv, parts, parts, parts)


def _place():
    x, y, c = lax.axis_index("x"), lax.axis_index("y"), lax.axis_index("c")
    chips = [(1 - x, y), (x, 1 - y), (1 - x, 1 - y)]
    return x, y, c, chips


def _sems(*shape):
    return [pltpu.SemaphoreType.DMA(shape), pltpu.SemaphoreType.DMA(shape)]


OWNER_CORE = (0, 1, 1, 1)


def _gather_copy(src, out, send_sems, recv_sems, t, sem, slot, to, forward=False):
    n = src[t].shape[0]
    rows = out[t].at[pl.ds(pl.multiple_of(slot * n, 16), n)]
    return pltpu.make_async_remote_copy(src_ref=rows if forward else src[t], dst_ref=rows, send_sem=send_sems.at[sem, t],
                                        recv_sem=recv_sems.at[sem, t], device_id=to, device_id_type=MESH)


def _gather_start(src, out, send_sems, recv_sems):
    x, y, c, chips = _place()
    for t, owner in enumerate(OWNER_CORE):
        @pl.when(c == owner)
        def _():
            for j, chip in enumerate(chips):
                _gather_copy(src, out, send_sems, recv_sems, t, j, 2 * x + y, (*chip, c)).start()


def _gather_forward(src, out, send_sems, recv_sems):
    x, y, c, chips = _place()
    for t, owner in enumerate(OWNER_CORE):
        @pl.when(c == owner)
        def _():
            for j, (px, py) in enumerate(chips):
                _gather_copy(src, out, send_sems, recv_sems, t, j, 2 * px + py, (px, py, c)).wait_recv()
                _gather_copy(src, out, send_sems, recv_sems, t, 3 + j, 2 * px + py, (x, y, 1 - c), forward=True).start()


def _gather_finish(src, out, send_sems, recv_sems):
    x, y, c, chips = _place()
    slots = [2 * px + py for px, py in chips]
    for t, owner in enumerate(OWNER_CORE):
        copy = functools.partial(_gather_copy, src, out, send_sems, recv_sems, t)

        @pl.when(c == owner)
        def _():
            for j, (px, py) in enumerate(chips):
                copy(j, 2 * x + y, (px, py, c)).wait_send()
                copy(3 + j, slots[j], (x, y, 1 - c), forward=True).wait_send()

        @pl.when(c != owner)
        def _():
            for j in range(3):
                copy(3 + j, slots[j], (x, y, 1 - c), forward=True).wait_recv()


def _gathered_shapes(shards):
    return [jax.ShapeDtypeStruct((4 * a.shape[0], a.shape[1]), a.dtype) for a in shards]


def _comm_gather_layer(shards, *, name):
    nt = len(shards)

    def body(*refs):
        src, out, sems = refs[:nt], refs[nt:2 * nt], refs[2 * nt:]
        _gather_start(src, out, *sems)
        _gather_forward(src, out, *sems)
        _gather_finish(src, out, *sems)

    return pl.pallas_call(
        body, name=name, in_specs=[ANY] * nt, out_specs=[ANY] * nt, out_shape=_gathered_shapes(shards),
        scratch_shapes=_sems(6, nt),
    )(*shards)


def _comm_swap_sibling(bufs, *, name):
    nt = len(bufs)

    def body(*refs):
        src, out, (send_sems, recv_sems) = refs[:nt], refs[nt:2 * nt], refs[2 * nt:]
        x, y, c, _ = _place()
        cps = [pltpu.make_async_remote_copy(src_ref=src[t], dst_ref=out[t], send_sem=send_sems.at[t], recv_sem=recv_sems.at[t],
                                            device_id=(x, y, 1 - c), device_id_type=MESH) for t in range(nt)]
        for cp in cps:
            cp.start()
        for cp in cps:
            cp.wait()

    return pl.pallas_call(
        body, name=name, in_specs=[ANY] * nt, out_specs=[ANY] * nt,
        out_shape=[jax.ShapeDtypeStruct(a.shape, a.dtype) for a in bufs], scratch_shapes=_sems(nt),
    )(*bufs)


def _scatter_copy(src, out, send_sems, recv_sems, j, t, from_slot, to_slot, to):
    return pltpu.make_async_remote_copy(src_ref=src[t].at[:, from_slot], dst_ref=out[t].at[:, to_slot],
                                        send_sem=send_sems.at[j, t], recv_sem=recv_sems.at[j, t], device_id=to,
                                        device_id_type=MESH)


def _scatter_start(src, out, send_sems, recv_sems):
    x, y, c, chips = _place()
    for j, (px, py) in enumerate(chips):
        for t in range(len(src)):
            _scatter_copy(src, out, send_sems, recv_sems, j, t, 2 * px + py, 2 * x + y, (px, py, c)).start()


def _scatter_finish(src, out, send_sems, recv_sems):
    x, y, c, chips = _place()
    for j, (px, py) in enumerate(chips):
        for t in range(len(src)):
            _scatter_copy(src, out, send_sems, recv_sems, j, t, 2 * x + y, 2 * px + py, (px, py, c)).wait_recv()
    for j, (px, py) in enumerate(chips):
        for t in range(len(src)):
            _scatter_copy(src, out, send_sems, recv_sems, j, t, 2 * px + py, 2 * x + y, (px, py, c)).wait_send()


def _comm_scatter_chips(parts, *, name):
    nt = len(parts)

    def body(*refs):
        src, out, sems = refs[:nt], refs[nt:2 * nt], refs[2 * nt:]
        _scatter_start(src, out, *sems)
        _scatter_finish(src, out, *sems)

    return pl.pallas_call(
        body, name=name, in_specs=[ANY] * nt, out_specs=[ANY] * nt,
        out_shape=[jax.ShapeDtypeStruct(a.shape, a.dtype) for a in parts], scratch_shapes=_sems(3, nt),
    )(*parts)


def _comm_join_halves(bufs, *, name):
    nt = len(bufs)

    def body(*refs):
        src, out, (send_sems, recv_sems) = refs[:nt], refs[nt:2 * nt], refs[2 * nt:]
        x, y, c, _ = _place()
        copy = lambda t, half: pltpu.make_async_remote_copy(
            src_ref=src[t].at[:, c], dst_ref=out[t].at[:, half], send_sem=send_sems.at[t], recv_sem=recv_sems.at[t],
            device_id=(x, y, 1 - c), device_id_type=MESH)
        sends = [copy(t, c) for t in range(nt)]
        for cp in sends:
            cp.start()
        for t in range(nt):
            copy(t, 1 - c).wait_recv()
        for cp in sends:
            cp.wait_send()

    return pl.pallas_call(
        body, name=name, in_specs=[ANY] * nt, out_specs=[ANY] * nt,
        out_shape=[jax.ShapeDtypeStruct(a.shape, a.dtype) for a in bufs],
        input_output_aliases={t: t for t in range(nt)}, scratch_shapes=_sems(nt),
    )(*bufs)


def _comm_allreduce_small(part, *, name):
    rows, cols = part.shape

    def body(p_ref, o_ref, buf, send_sems, recv_sems):
        x, y, c, _ = _place()
        me = 4 * x + 2 * y + c
        buf[me] = p_ref[...]
        flip = lambda v, bit: 1 - v if bit else v
        peers = [(flip(x, d & 4), flip(y, d & 2), flip(c, d & 1)) for d in range(1, 8)]
        sends = []
        for j, peer in enumerate(peers):
            cp = pltpu.make_async_remote_copy(src_ref=buf.at[me], dst_ref=buf.at[me], send_sem=send_sems.at[j],
                                              recv_sem=recv_sems.at[j], device_id=peer, device_id_type=MESH)
            cp.start()
            sends.append(cp)
        for j, (px, py, pc) in enumerate(peers):
            pltpu.make_async_remote_copy(src_ref=buf.at[me], dst_ref=buf.at[4 * px + 2 * py + pc], send_sem=send_sems.at[j],
                                         recv_sem=recv_sems.at[j], device_id=(px, py, pc), device_id_type=MESH).wait_recv()
        for cp in sends:
            cp.wait_send()
        total = buf[0]
        for i in range(1, 8):
            total = total + buf[i]
        o_ref[...] = total

    vm = pl.BlockSpec(memory_space=pltpu.VMEM)
    return pl.pallas_call(
        body, name=name, in_specs=[vm], out_specs=vm, out_shape=jax.ShapeDtypeStruct((rows, cols), F32),
        scratch_shapes=[pltpu.VMEM((8, rows, cols), F32), pltpu.SemaphoreType.DMA((7,)), pltpu.SemaphoreType.DMA((7,))],
    )(part)


def _pad_in_rows(wt):
    r = lambda o, n: wt[..., o:o + n, :]
    kr = r(2944, 64)
    return jnp.concatenate([r(0, 1024), r(1280, 1024), r(3008, 1024), r(2304, Q_RANK), r(1024, 128), r(2688, KV_RANK),
                            r(1152, 128), kr, jnp.zeros_like(kr)], axis=-2)


def _unpad_in_rows(qa, ga, gb, mixed):
    cq, ka, ckv, va, kr = (mixed[..., o - CQ_OFF:o - CQ_OFF + n, :] for o, n in (
        (CQ_OFF, Q_RANK), (KA_OFF, 128), (CKV_OFF, KV_RANK), (VA_OFF, 128), (KR_OFF, 64)))
    return jnp.concatenate([qa, ka, va, ga, cq, ckv, kr, gb], axis=-2)


def _pad_q_rows(wt):
    lead, cols = wt.shape[:-2], wt.shape[-1]
    wt = jnp.pad(wt.reshape(lead + (MLA_HEADS, 192, cols)), [(0, 0)] * (len(lead) + 1) + [(0, 64), (0, 0)])
    return wt.reshape(lead + (MLA_HEADS * 256, cols))


def _unpad_q_rows(wt):
    lead, cols = wt.shape[:-2], wt.shape[-1]
    return wt.reshape(lead + (MLA_HEADS, 256, cols))[..., :192, :].reshape(lead + (MLA_HEADS * 192, cols))


def _perm_kv_rows(wt):
    lead, cols = wt.shape[:-2], wt.shape[-1]
    return jnp.swapaxes(wt.reshape(lead + (MLA_HEADS, 2, 128, cols)), -4, -3).reshape(lead + (2048, cols))


def _unperm_kv_rows(wt):
    lead, cols = wt.shape[:-2], wt.shape[-1]
    return jnp.swapaxes(wt.reshape(lead + (2, MLA_HEADS, 128, cols)), -4, -3).reshape(lead + (2048, cols))


def _t(a):
    return jnp.swapaxes(a, -1, -2)


def _rope_tables(s):
    pos = jnp.arange(s, dtype=F32)
    inv_freq = 10000.0 ** (-jnp.arange(0, 64, 2, dtype=F32) / 64)
    ang = pos[:, None] * inv_freq[None, :]
    cos, sin = jnp.cos(ang), jnp.sin(ang)
    z64 = jnp.zeros((s, 64), F32)
    tk_c = jnp.concatenate([cos, cos, z64], axis=-1)
    tk_s = jnp.concatenate([-sin, sin, z64], axis=-1)
    return tk_c, tk_s


def _layer_weights(gathered, own, chip):
    def with_own_rows(g, o):
        n = o.shape[0]
        return jnp.concatenate([lax.select(chip == j, o, g[j * n:(j + 1) * n]) for j in range(4)], axis=0)

    full_in, full_q, full_kv, full_o = (with_own_rows(g, o) for g, o in zip(gathered, own))
    return _pad_in_rows(full_in), _pad_q_rows(full_q), _perm_kv_rows(full_kv), full_o


def _device_step(xs, tgt, attn_g, sinks, gq, gkv, final_g, shards, where):
    chip = where[0]
    depth = shards[0].shape[0]
    s = xs.shape[0]
    tabs = _rope_tables(s)
    saved = []
    x = xs
    of_layer = lambda l: [a[l] for a in shards]
    gathered = _comm_gather_layer(of_layer(0), name="comm_gather_layer0")
    weights = []
    for l in range(depth):
        w_in_p, w_q_p, w_kv_p, w_o = _layer_weights(gathered, of_layer(l), chip)
        weights.append((w_in_p, w_q_p, w_kv_p, w_o))
        h = _rmsnorm_fwd(x, attn_g[l:l + 1], name=f"norm_fwd{l}")
        proj = _matmul(h, w_in_p, tb=True, name=f"in_proj{l}")
        oa, lse_a = _swa_fwd(proj, sinks[l], name=f"swa_fwd{l}")
        qcat, kcat, v, cqn, ckvn = _mla_qkv_fwd(proj, gq[l:l + 1], gkv[l:l + 1], w_q_p, w_kv_p, *tabs, layer=None,
                                                name=f"mla_qkv_fwd{l}")
        ob, lse_b, gathered = _mla_fwd(qcat, kcat, v, gather=of_layer(l + 1) if l + 1 < depth else None,
                                       name=f"mla_fwd{l}")
        y = _gate_fwd(oa, ob, proj, name=f"gate_fwd{l}")
        x_next = _matmul(y, w_o, add=x, name=f"out_proj{l}")
        saved.append((x, h, proj, oa, lse_a, qcat, kcat, v, cqn, ckvn, ob, lse_b, y))
        x = x_next

    dx, d_final_g, loss = _final_loss(x, final_g, tgt, name="final_loss")

    d_attn_g, d_sinks, d_gq, d_gkv = [None] * depth, [None] * depth, [None] * depth, [None] * depth
    d_w_in, d_w_q, d_w_kv, d_w_o = [None] * depth, [None] * depth, [None] * depth, [None] * depth
    half = depth // 2
    carried = {half - 1: list(range(half, depth)), **{l - 1: [l] for l in range(1, half)}}
    done = {}

    def grads_by_chip(layers):
        stack = lambda per_layer: jnp.stack([per_layer[l] for l in layers])
        in_tiles = [stack([tiles[j] if tiles else None for tiles in d_w_in]) for j in range(4)]
        full = [_unpad_in_rows(*in_tiles), _unpad_q_rows(stack(d_w_q)), _unperm_kv_rows(stack(d_w_kv)), stack(d_w_o)]
        out = []
        for g in full:
            g = g.reshape(g.shape[0], 4, g.shape[1] // 4, g.shape[2])
            g = jnp.pad(g, ((0, 0), (0, 0), (0, -g.shape[2] % 32), (0, 0)))
            out.append(g.reshape(g.shape[0], 4, 2, g.shape[2] // 2, g.shape[3]))
        return out

    def reduce_begin(grads5, tag):
        give = [lax.dynamic_index_in_dim(g, 1 - where[4], axis=2, keepdims=False).astype(WIRE_DTYPE) for g in grads5]
        recv = _comm_swap_sibling(give, name=f"comm_swap_sibling_{tag}")
        return recv, [_pair_sum(where, g, r, name=f"pair_sum_{tag}_{t}") for t, (g, r) in enumerate(zip(grads5, recv))]

    def reduce_end(grads5, recv, parts, tag):
        mine = [_chip_sum(where, g, r, p, name=f"chip_sum_{tag}_{t}") for t, (g, r, p) in enumerate(zip(grads5, recv, parts))]
        joined = _comm_join_halves(mine, name=f"comm_join_halves_{tag}")
        return [a.reshape(a.shape[0], -1, a.shape[-1]) for a in joined]

    for l in reversed(range(depth)):
        x, h, proj, oa, lse_a, qcat, kcat, v, cqn, ckvn, ob, lse_b, y = saved[l]
        w_in_p, w_q_p, w_kv_p, w_o = weights[l]
        dy = _matmul(dx, w_o, tb=True, name=f"out_proj_dx{l}")
        d_w_o[l] = _matmul(y, dx, ta=True, name=f"out_proj_dw{l}")
        doa, dob, dga, dgb, delta_b = _gate_bwd(dy, oa, ob, proj, name=f"gate_bwd{l}")
        dqa, dka, dva, dsink = _swa_bwd(proj, sinks[l], lse_a, doa, name=f"swa_bwd{l}")
        phase, pair = carried.get(l), None
        if phase:
            grads5 = grads_by_chip(phase)
            recv, pair = reduce_begin(grads5, f"l{phase[0]}")
        dqc, dkc, dv, parts = _mla_bwd(qcat, kcat, v, dob, lse_b, delta_b, scatter=pair, name=f"mla_bwd{l}")
        if phase:
            done[phase[0]] = reduce_end(grads5, recv, parts, f"l{phase[0]}")
        mixed, d_w_q[l], d_w_kv[l], dgq_l, dgkv_l = _mla_qkv_bwd(
            proj, cqn, ckvn, dqc, dkc, dv, dka, dva, gq[l:l + 1], gkv[l:l + 1], w_q_p, w_kv_p, *tabs, layer=None,
            name=f"mla_qkv_bwd{l}")
        dproj = [dqa, dga, dgb, mixed]
        dh = _matmul_ktiles(dproj, w_in_p, name=f"in_proj_dx{l}")
        d_w_in[l] = [_matmul(tile, h, ta=True, name=f"in_proj_dw{l}_{j}") for j, tile in enumerate(dproj)]
        dx, dg_l = _rmsnorm_bwd(dh, x, attn_g[l:l + 1], dx, name=f"norm_bwd{l}")
        d_attn_g[l], d_sinks[l], d_gq[l], d_gkv[l] = dg_l, dsink[0:1, :SWA_HEADS], dgq_l, dgkv_l

    grads5 = grads_by_chip([0])
    recv, pair = reduce_begin(grads5, "l0")
    done[0] = reduce_end(grads5, recv, _comm_scatter_chips(pair, name="comm_scatter_chips_l0"), "l0")
    cat = lambda parts: jnp.concatenate(parts, axis=0)
    reduced = [cat([done[first][t] for first in sorted(done)])[:, :shards[t].shape[1]] for t in range(len(shards))]
    return loss, dx, cat(d_attn_g), cat(d_sinks), cat(d_gq), cat(d_gkv), d_final_g, reduced


def kernel(x, attn_norm_g, w_in, swa_sinks, q_a_norm_g, kv_a_norm_g, w_q_b, w_kv_b, w_out, final_norm_g, loss_target, m_attn_norm_g, m_w_in, m_swa_sinks, m_q_a_norm_g, m_kv_a_norm_g, m_w_q_b, m_w_kv_b, m_w_out, m_final_norm_g, v_attn_norm_g, v_w_in, v_swa_sinks, v_q_a_norm_g, v_kv_a_norm_g, v_w_q_b, v_w_kv_b, v_w_out, v_final_norm_g):
    x_, y_, c = lax.axis_index("x"), lax.axis_index("y"), lax.axis_index("c")
    chip = 2 * x_ + y_
    where = jnp.stack([chip, 2 * (1 - x_) + y_, 2 * x_ + 1 - y_, 2 * (1 - x_) + 1 - y_, c]).astype(jnp.int32)

    sent = [a.astype(WIRE_DTYPE) for a in (_t(w_in), _t(w_q_b), _t(w_kv_b), w_out)]

    loss, dx, d_attn_g, d_sinks, d_gq, d_gkv, d_final_g, reduced = _device_step(
        x[0], loss_target[0], attn_norm_g, swa_sinks, q_a_norm_g, kv_a_norm_g, final_norm_g.reshape(1, -1), sent, where)
    g_w_in, g_w_q_b, g_w_kv_b, g_w_out = _t(reduced[0]), _t(reduced[1]), _t(reduced[2]), reduced[3]

    small = [d_attn_g, d_sinks, d_gq, d_gkv, d_final_g, loss[:, :1]]
    flat = jnp.concatenate([a.reshape(-1) for a in small])
    n_small = flat.shape[0]
    rows = -(-n_small // 1024) * 8
    total = _comm_allreduce_small(jnp.pad(flat, (0, rows * 128 - n_small)).reshape(rows, 128),
                                  name="comm_allreduce_small").reshape(-1)
    outs, at = [], 0
    for a in small:
        outs.append(total[at:at + a.size].reshape(a.shape))
        at += a.size
    g_attn_g, g_sinks, g_gq, g_gkv, g_final_g, loss_total = outs
    g_final_g = g_final_g.reshape(final_norm_g.shape)

    weights = [attn_norm_g, w_in, swa_sinks, q_a_norm_g, kv_a_norm_g, w_q_b, w_kv_b, w_out, final_norm_g]
    grads = [g_attn_g, g_w_in, g_sinks, g_gq, g_gkv, g_w_q_b, g_w_kv_b, g_w_out, g_final_g]
    ms = [m_attn_norm_g, m_w_in, m_swa_sinks, m_q_a_norm_g, m_kv_a_norm_g, m_w_q_b, m_w_kv_b, m_w_out, m_final_norm_g]
    vs = [v_attn_norm_g, v_w_in, v_swa_sinks, v_q_a_norm_g, v_kv_a_norm_g, v_w_q_b, v_w_kv_b, v_w_out, v_final_norm_g]
    as2d = lambda a: a.reshape(1, -1) if a.ndim == 1 else a
    deltas, new_m, new_v = [], [], []
    for i, (w, g, m, v) in enumerate(zip(weights, grads, ms, vs)):
        view = _t if w is w_in else as2d
        d, mn, vn = _adamw(view(w), reduced[0] if w is w_in else view(g), view(m), view(v), name=f"adamw{i}")
        back = _t if w is w_in else (lambda a: a.reshape(w.shape))
        deltas.append(back(d))
        new_m.append(back(mn))
        new_v.append(back(vn))

    return (loss_total.reshape(()), dx[None], *grads, *deltas, *new_m, *new_v)
```

```python
import functools
import math

import jax
import jax.numpy as jnp
from jax import lax
from jax.experimental import pallas as pl
from jax.experimental.pallas import tpu as pltpu

F32 = jnp.float32
MXU_DTYPE = jnp.bfloat16
WIRE_DTYPE = jnp.bfloat16

EPS = 1e-6
NEG = -1e30
BLOCK = 128
D_MODEL = 2048
SWA_HEADS = 16
MLA_HEADS = 8
Q_RANK = 384
KV_RANK = 256
IN_WIDTH = 4032
MLA_SCALE = 192 ** -0.5
SWA_SCALE = 64 ** -0.5
LOG2E = math.log2(math.e)
HEADS_PER_STEP = 2
SLOPES = tuple(2.0 ** (-8.0 * (h + 1) / SWA_HEADS) for h in range(SWA_HEADS))

P_WIDTH = 4096
QA_OFF, GA_OFF, GB_OFF, CQ_OFF, KA_OFF, CKV_OFF, VA_OFF, KR_OFF = 0, 1024, 2048, 3072, 3456, 3584, 3840, 3968

ADAM_LR, ADAM_B1, ADAM_B2, ADAM_EPS, ADAM_WD, ADAM_STEP = 0.001, 0.9, 0.999, 1e-08, 0.01, 10

VMEM_LIMIT = 56 * 1024 * 1024
MESH = pl.DeviceIdType.MESH
ANY = pl.BlockSpec(memory_space=pl.ANY)


def _cparams(*sem):
    return pltpu.CompilerParams(dimension_semantics=sem, vmem_limit_bytes=VMEM_LIMIT)


def _dot(a, b, ca, cb):
    return lax.dot_general(a, b, (((ca,), (cb,)), ((), ())), preferred_element_type=F32)


def _layer_spec(block, index_map, layer):
    if layer is None:
        return pl.BlockSpec(block, index_map)
    return pl.BlockSpec((None,) + tuple(block), lambda *g: (layer,) + tuple(index_map(*g)))


def _matmul(a, b, *, name, ta=False, tb=False, out_dtype=F32, add=None, b_layer=None, tm=1024, tn=1024, tk=2048):
    (kdim, m) = a.shape if ta else a.shape[::-1]
    (n, k2) = b.shape[-2:] if tb else b.shape[-2:][::-1]
    assert kdim == k2, (a.shape, b.shape)
    tm, tn, tk = min(tm, m), min(tn, n), min(tk, kdim)
    assert m % tm == 0 and n % tn == 0 and kdim % tk == 0
    nk = kdim // tk

    def body(*refs):
        a_ref, b_ref = refs[:2]
        add_ref = None if add is None else refs[2]
        o_ref = refs[2 + (add is not None)]
        part = _dot(a_ref[...].astype(MXU_DTYPE), b_ref[...].astype(MXU_DTYPE), 0 if ta else 1, 1 if tb else 0)

        def finish(r):
            o_ref[...] = (r if add is None else add_ref[...] + r).astype(out_dtype)

        if nk == 1:
            finish(part)
            return
        acc = refs[-1]
        k = pl.program_id(2)

        @pl.when(k == 0)
        def _():
            acc[...] = part

        @pl.when((k > 0) & (k < nk - 1))
        def _():
            acc[...] += part

        @pl.when(k == nk - 1)
        def _():
            finish(acc[...] + part)

    a_spec = pl.BlockSpec((tk, tm), lambda i, j, k: (k, i)) if ta else pl.BlockSpec((tm, tk), lambda i, j, k: (i, k))
    b_spec = (_layer_spec((tn, tk), lambda i, j, k: (j, k), b_layer) if tb else
              _layer_spec((tk, tn), lambda i, j, k: (k, j), b_layer))
    in_specs, args = [a_spec, b_spec], [a, b]
    if add is not None:
        in_specs.append(pl.BlockSpec((tm, tn), lambda i, j, k: (i, j)))
        args.append(add)
    return pl.pallas_call(
        body, name=name, grid=(m // tm, n // tn, nk), in_specs=in_specs,
        out_specs=pl.BlockSpec((tm, tn), lambda i, j, k: (i, j)),
        out_shape=jax.ShapeDtypeStruct((m, n), out_dtype),
        scratch_shapes=[pltpu.VMEM((tm, tn), F32)] if nk > 1 else [],
        compiler_params=_cparams("parallel", "parallel", "arbitrary"),
    )(*args)


def _matmul_ktiles(a_tiles, b, *, name, b_layer=None, tm=512, tn=1024):
    m, kt = a_tiles[0].shape
    n = b.shape[-1]
    nt = len(a_tiles)
    assert b.shape[-2] == nt * kt
    tm, tn = min(tm, m), min(tn, n)
    assert m % tm == 0 and n % tn == 0

    def body(*refs):
        a_refs, b_refs, o_ref = refs[:nt], refs[nt:2 * nt], refs[2 * nt]
        acc = _dot(a_refs[0][...].astype(MXU_DTYPE), b_refs[0][...].astype(MXU_DTYPE), 1, 0)
        for j in range(1, nt):
            acc += _dot(a_refs[j][...].astype(MXU_DTYPE), b_refs[j][...].astype(MXU_DTYPE), 1, 0)
        o_ref[...] = acc

    in_specs = [pl.BlockSpec((tm, kt), lambda jn, i: (i, 0))] * nt
    in_specs += [_layer_spec((kt, tn), lambda jn, i, j=j: (j, jn), b_layer) for j in range(nt)]
    return pl.pallas_call(
        body, name=name, grid=(n // tn, m // tm), in_specs=in_specs,
        out_specs=pl.BlockSpec((tm, tn), lambda jn, i: (i, jn)),
        out_shape=jax.ShapeDtypeStruct((m, n), F32),
        compiler_params=_cparams("parallel", "parallel"),
    )(*a_tiles, *([b] * nt))


def _rmsnorm_fwd(x, g, *, name):
    s, d = x.shape
    tm = min(512, s)

    def body(x_ref, g_ref, h_ref):
        xv = x_ref[...]
        r = lax.rsqrt(jnp.mean(xv * xv, axis=-1, keepdims=True) + EPS)
        h_ref[...] = (xv * r * g_ref[...]).astype(MXU_DTYPE)

    return pl.pallas_call(
        body, name=name, grid=(s // tm,),
        in_specs=[pl.BlockSpec((tm, d), lambda i: (i, 0)), pl.BlockSpec((1, d), lambda i: (0, 0))],
        out_specs=pl.BlockSpec((tm, d), lambda i: (i, 0)),
        out_shape=jax.ShapeDtypeStruct((s, d), MXU_DTYPE),
        compiler_params=_cparams("parallel"),
    )(x, g)


def _rmsnorm_bwd(dh, x, g, dres, *, name):
    s, d = x.shape
    tm = min(512, s)

    def body(dh_ref, x_ref, g_ref, dres_ref, dx_ref, dg_ref):
        @pl.when(pl.program_id(0) == 0)
        def _():
            dg_ref[...] = jnp.zeros_like(dg_ref)

        xv = x_ref[...]
        r = lax.rsqrt(jnp.mean(xv * xv, axis=-1, keepdims=True) + EPS)
        xn = xv * r
        dy = dh_ref[...]
        dg_ref[...] += jnp.sum(dy * xn, axis=0, keepdims=True)
        u = dy * g_ref[...]
        dx_ref[...] = dres_ref[...] + r * (u - xn * jnp.mean(u * xn, axis=-1, keepdims=True))

    row = pl.BlockSpec((tm, d), lambda i: (i, 0))
    vec = pl.BlockSpec((1, d), lambda i: (0, 0))
    return pl.pallas_call(
        body, name=name, grid=(s // tm,), in_specs=[row, row, vec, row], out_specs=[row, vec],
        out_shape=[jax.ShapeDtypeStruct((s, d), F32), jax.ShapeDtypeStruct((1, d), F32)],
        compiler_params=_cparams("arbitrary"),
    )(dh, x, g, dres)


def _final_loss(x, g, tgt, *, name):
    s, d = x.shape
    tm = min(512, s)

    def body(x_ref, g_ref, t_ref, dx_ref, dg_ref, loss_ref):
        @pl.when(pl.program_id(0) == 0)
        def _():
            dg_ref[...] = jnp.zeros_like(dg_ref)
            loss_ref[...] = jnp.zeros_like(loss_ref)

        xv = x_ref[...]
        gv = g_ref[...]
        r = lax.rsqrt(jnp.mean(xv * xv, axis=-1, keepdims=True) + EPS)
        xn = xv * r
        err = xn * gv - t_ref[...]
        sq = jnp.sum(jnp.sum(err * err, axis=-1, keepdims=True), axis=0, keepdims=True)
        loss_ref[...] += (0.5 / d) * sq
        dy = err * (1.0 / d)
        dg_ref[...] += jnp.sum(dy * xn, axis=0, keepdims=True)
        u = dy * gv
        dx_ref[...] = r * (u - xn * jnp.mean(u * xn, axis=-1, keepdims=True))

    row = pl.BlockSpec((tm, d), lambda i: (i, 0))
    vec = pl.BlockSpec((1, d), lambda i: (0, 0))
    return pl.pallas_call(
        body, name=name, grid=(s // tm,), in_specs=[row, vec, row],
        out_specs=[row, vec, pl.BlockSpec((1, 128), lambda i: (0, 0))],
        out_shape=[jax.ShapeDtypeStruct((s, d), F32), jax.ShapeDtypeStruct((1, d), F32),
                   jax.ShapeDtypeStruct((1, 128), F32)],
        compiler_params=_cparams("arbitrary"),
    )(x, g, tgt)


def _swa_keys(kp_ref, kc_ref):
    kk = jnp.concatenate([kp_ref[...], kc_ref[...]], axis=0)
    kr = pltpu.roll(kk, 64, 1)
    lo = lax.broadcasted_iota(jnp.int32, kk.shape, 1) < 64
    return [jnp.where(lo, kk, kr).astype(MXU_DTYPE), jnp.where(lo, kr, kk).astype(MXU_DTYPE)]


GROUP = SWA_HEADS // 2


def _swa_mask(n):
    qi = lax.broadcasted_iota(jnp.int32, (BLOCK, 2 * BLOCK), 0)
    ki = lax.broadcasted_iota(jnp.int32, (BLOCK, 2 * BLOCK), 1)
    delta = BLOCK + qi - ki
    valid = (delta >= 0) & (delta < BLOCK) & ((ki >= BLOCK) | (n > 0))
    return valid, delta.astype(F32)


def _stack_heads(ref, j):
    lo = lax.broadcasted_iota(jnp.int32, (BLOCK, BLOCK), 1) < 64
    parts = []
    for r in range(GROUP):
        pair = (GROUP * j + r) // 2
        blk = ref[:, pair * 128:(pair + 1) * 128].astype(F32)
        parts.append(jnp.where(lo if r % 2 == 0 else ~lo, blk, 0.0).astype(MXU_DTYPE))
    return jnp.concatenate(parts, axis=0)


def _unstack_heads(stacked, ref, j):
    lo = lax.broadcasted_iota(jnp.int32, (BLOCK, BLOCK), 1) < 64
    for i in range(GROUP // 2):
        pair = (GROUP * j) // 2 + i
        even, odd = stacked[2 * i * BLOCK:(2 * i + 1) * BLOCK], stacked[(2 * i + 1) * BLOCK:(2 * i + 2) * BLOCK]
        ref[:, pair * 128:(pair + 1) * 128] = jnp.where(lo, even, odd).astype(ref.dtype)


def _head_rows(stacked, r):
    return stacked[r * BLOCK:(r + 1) * BLOCK]


def _swa_scores(raw, h, valid, deltaf):
    return jnp.where(valid, raw * (SWA_SCALE * LOG2E) - (SLOPES[h] * LOG2E) * deltaf, NEG)


def _swa_specs(nb):
    kcol, vcol = KA_OFF // BLOCK, VA_OFF // BLOCK
    last = nb - 1
    cur = lambda n: jnp.minimum(n, last)
    prev = lambda n: jnp.maximum(jnp.minimum(n, last) - 1, 0)
    return [
        pl.BlockSpec(memory_space=pltpu.SMEM),
        pl.BlockSpec((BLOCK, 1024), lambda n: (cur(n), QA_OFF // 1024)),
        pl.BlockSpec((BLOCK, BLOCK), lambda n: (cur(n), kcol)),
        pl.BlockSpec((BLOCK, BLOCK), lambda n: (prev(n), kcol)),
        pl.BlockSpec((BLOCK, BLOCK), lambda n: (cur(n), vcol)),
        pl.BlockSpec((BLOCK, BLOCK), lambda n: (prev(n), vcol)),
    ]


def _swa_fwd(proj, sinks, *, name):
    s = proj.shape[0]
    nb = s // BLOCK

    def body(sink_ref, q_ref, kc_ref, kp_ref, vc_ref, vp_ref, o_ref, lse_ref):
        n = pl.program_id(0)
        keys = _swa_keys(kp_ref, kc_ref)
        vals = _swa_keys(vp_ref, vc_ref)
        valid, deltaf = _swa_mask(n)
        lane = lax.broadcasted_iota(jnp.int32, (BLOCK, BLOCK), 1)
        lse_acc = jnp.zeros((BLOCK, BLOCK), F32)
        for j in range(2):
            raw = _dot(_stack_heads(q_ref, j), keys[j], 1, 1)
            probs = []
            for r in range(GROUP):
                h = GROUP * j + r
                sc = _swa_scores(_head_rows(raw, r), h, valid, deltaf)
                sink = sink_ref[h] * LOG2E
                m = jnp.maximum(jnp.max(sc, axis=-1, keepdims=True), sink)
                p = jnp.exp2(sc - m)
                l = jnp.sum(p, axis=-1, keepdims=True) + jnp.exp2(sink - m)
                probs.append((p * (1.0 / l)).astype(MXU_DTYPE))
                lse_acc = jnp.where(lane == h, m + jnp.log(l) * LOG2E, lse_acc)
            _unstack_heads(jnp.dot(jnp.concatenate(probs, axis=0), vals[j], preferred_element_type=F32), o_ref, j)
        lse_ref[...] = lse_acc

    return pl.pallas_call(
        body, name=name, grid=(nb,), in_specs=_swa_specs(nb),
        out_specs=[pl.BlockSpec((BLOCK, 1024), lambda n: (n, 0)), pl.BlockSpec((BLOCK, BLOCK), lambda n: (n, 0))],
        out_shape=[jax.ShapeDtypeStruct((s, 1024), F32), jax.ShapeDtypeStruct((s, BLOCK), F32)],
        compiler_params=_cparams("parallel"),
    )(sinks, proj, proj, proj, proj, proj)


def _swa_bwd(proj, sinks, lse, do, *, name):
    s = proj.shape[0]
    nb = s // BLOCK
    last = nb - 1

    def body(sink_ref, q_ref, kc_ref, kp_ref, vc_ref, vp_ref, lse_ref, do_ref,
             dq_ref, dk_ref, dv_ref, dsink_ref, carry_k, carry_v):
        n = pl.program_id(0)

        @pl.when(n == 0)
        def _():
            carry_k[...] = jnp.zeros_like(carry_k)
            carry_v[...] = jnp.zeros_like(carry_v)
            dsink_ref[...] = jnp.zeros_like(dsink_ref)

        @pl.when(n < nb)
        def _():
            keys = _swa_keys(kp_ref, kc_ref)
            vals = _swa_keys(vp_ref, vc_ref)
            valid, deltaf = _swa_mask(n)
            lane = lax.broadcasted_iota(jnp.int32, (BLOCK, BLOCK), 1)
            lane1 = lax.broadcasted_iota(jnp.int32, (1, BLOCK), 1)
            lse_blk = lse_ref[...]
            acc_k, acc_v = [], []
            dsink = jnp.zeros((1, BLOCK), F32)
            for j in range(2):
                q_all, do_all = _stack_heads(q_ref, j), _stack_heads(do_ref, j)
                raw = _dot(q_all, keys[j], 1, 1)
                dp_all = _dot(do_all, vals[j], 1, 1)
                probs, dscores = [], []
                for r in range(GROUP):
                    h = GROUP * j + r
                    lse_h = jnp.sum(jnp.where(lane == h, lse_blk, 0.0), axis=-1, keepdims=True)
                    p = jnp.exp2(_swa_scores(_head_rows(raw, r), h, valid, deltaf) - lse_h)
                    dp = _head_rows(dp_all, r)
                    dlt = jnp.sum(dp * p, axis=-1, keepdims=True)
                    dscores.append((p * (dp - dlt) * SWA_SCALE).astype(MXU_DTYPE))
                    probs.append(p.astype(MXU_DTYPE))
                    sunk = jnp.exp2(sink_ref[h] * LOG2E - lse_h) * dlt
                    dsink = jnp.where(lane1 == h, -jnp.sum(sunk, axis=0, keepdims=True), dsink)
                ds_all = jnp.concatenate(dscores, axis=0)
                _unstack_heads(jnp.dot(ds_all, keys[j], preferred_element_type=F32), dq_ref, j)
                acc_k.append(_dot(ds_all, q_all, 0, 0))
                acc_v.append(_dot(jnp.concatenate(probs, axis=0), do_all, 0, 0))
            lo2 = lax.broadcasted_iota(jnp.int32, (2 * BLOCK, BLOCK), 1) < 64
            fold = lambda acc: jnp.where(lo2, acc[0] + pltpu.roll(acc[0], 64, 1), acc[1] + pltpu.roll(acc[1], 64, 1))
            dkk, dvv = fold(acc_k), fold(acc_v)
            dk_ref[...] = (carry_k[...] + dkk[:BLOCK]).astype(dk_ref.dtype)
            dv_ref[...] = (carry_v[...] + dvv[:BLOCK]).astype(dv_ref.dtype)
            carry_k[...] = dkk[BLOCK:]
            carry_v[...] = dvv[BLOCK:]
            dsink_ref[...] += jnp.broadcast_to(dsink, dsink_ref.shape)

        @pl.when(n == nb)
        def _():
            dk_ref[...] = carry_k[...].astype(dk_ref.dtype)
            dv_ref[...] = carry_v[...].astype(dv_ref.dtype)

    cur = lambda n: jnp.minimum(n, last)
    lag = lambda n: jnp.maximum(n - 1, 0)
    return pl.pallas_call(
        body, name=name, grid=(nb + 1,),
        in_specs=_swa_specs(nb) + [pl.BlockSpec((BLOCK, BLOCK), lambda n: (cur(n), 0)),
                                   pl.BlockSpec((BLOCK, 1024), lambda n: (cur(n), 0))],
        out_specs=[pl.BlockSpec((BLOCK, 1024), lambda n: (cur(n), 0)),
                   pl.BlockSpec((BLOCK, BLOCK), lambda n: (lag(n), 0)),
                   pl.BlockSpec((BLOCK, BLOCK), lambda n: (lag(n), 0)),
                   pl.BlockSpec((8, BLOCK), lambda n: (0, 0))],
        out_shape=[jax.ShapeDtypeStruct((s, 1024), MXU_DTYPE), jax.ShapeDtypeStruct((s, BLOCK), MXU_DTYPE),
                   jax.ShapeDtypeStruct((s, BLOCK), MXU_DTYPE), jax.ShapeDtypeStruct((8, BLOCK), F32)],
        scratch_shapes=[pltpu.VMEM((BLOCK, BLOCK), F32), pltpu.VMEM((BLOCK, BLOCK), F32)],
        compiler_params=_cparams("arbitrary"),
    )(sinks, proj, proj, proj, proj, proj, lse, do)


def _rope_partner(v, first, width):
    lane = lax.broadcasted_iota(jnp.int32, v.shape, 1)
    in_a = (lane >= first) & (lane < first + 32)
    in_b = (lane >= first + 32) & (lane < first + 64)
    return jnp.where(in_a, pltpu.roll(v, width - 32, 1), jnp.where(in_b, pltpu.roll(v, 32, 1), 0.0))


def _mla_qkv_fwd(proj, gq, gkv, wq, wkv, tk_c, tk_s, *, layer, name):
    s = proj.shape[0]
    tm = min(256, s)

    def body(cq_ref, ckv_ref, kr_ref, gq_ref, gkv_ref, wq_ref, wkv_ref, kc_ref, ks_ref,
             qcat_ref, kcat_ref, v_ref, cqn_ref, ckvn_ref):
        cq = cq_ref[...]
        cqn = (cq * lax.rsqrt(jnp.mean(cq * cq, axis=-1, keepdims=True) + EPS) * gq_ref[...]).astype(MXU_DTYPE)
        cqn_ref[...] = cqn
        qpre = _dot(cqn, wq_ref[...], 1, 1)
        kc, ks = kc_ref[...], ks_ref[...]
        for hh in range(MLA_HEADS):
            qcat_ref[:, hh * 256:hh * 256 + 128] = qpre[:, hh * 256:hh * 256 + 128].astype(MXU_DTYPE)
            blk = qpre[:, hh * 256 + 128:(hh + 1) * 256]
            qcat_ref[:, hh * 256 + 128:(hh + 1) * 256] = (blk * kc + _rope_partner(blk, 0, 128) * ks).astype(MXU_DTYPE)
        ckv = ckv_ref[...]
        ckvn = (ckv * lax.rsqrt(jnp.mean(ckv * ckv, axis=-1, keepdims=True) + EPS) * gkv_ref[...]).astype(MXU_DTYPE)
        ckvn_ref[...] = ckvn
        kv = _dot(ckvn, wkv_ref[...], 1, 1)
        kr = kr_ref[...]
        krr = (kr * kc + _rope_partner(kr, 0, 128) * ks).astype(MXU_DTYPE)
        for hh in range(MLA_HEADS):
            kcat_ref[:, hh * 256:hh * 256 + 128] = kv[:, hh * 128:(hh + 1) * 128].astype(MXU_DTYPE)
            kcat_ref[:, hh * 256 + 128:(hh + 1) * 256] = krr
            v_ref[:, hh * 256:hh * 256 + 128] = kv[:, 1024 + hh * 128:1024 + (hh + 1) * 128].astype(MXU_DTYPE)
            v_ref[:, hh * 256 + 128:(hh + 1) * 256] = jnp.ones((tm, 128), MXU_DTYPE)

    row = lambda w, c: pl.BlockSpec((tm, w), lambda i: (i, c))
    full = lambda a: pl.BlockSpec(a.shape, lambda i: (0, 0))
    of_layer = lambda a: _layer_spec(a.shape[-2:], lambda i: (0, 0), layer)
    return pl.pallas_call(
        body, name=name, grid=(s // tm,),
        in_specs=[row(Q_RANK, CQ_OFF // Q_RANK), row(KV_RANK, CKV_OFF // KV_RANK), row(128, KR_OFF // 128),
                  full(gq), full(gkv), of_layer(wq), of_layer(wkv), row(128, 0), row(128, 0)],
        out_specs=[row(2048, 0), row(2048, 0), row(2048, 0), row(Q_RANK, 0), row(KV_RANK, 0)],
        out_shape=[jax.ShapeDtypeStruct((s, 2048), MXU_DTYPE), jax.ShapeDtypeStruct((s, 2048), MXU_DTYPE),
                   jax.ShapeDtypeStruct((s, 2048), MXU_DTYPE), jax.ShapeDtypeStruct((s, Q_RANK), MXU_DTYPE),
                   jax.ShapeDtypeStruct((s, KV_RANK), MXU_DTYPE)],
        compiler_params=_cparams("parallel"),
    )(proj, proj, proj, gq, gkv, wq, wkv, tk_c, tk_s)


def _norm_bwd(x, g, dy):
    r = lax.rsqrt(jnp.mean(x * x, axis=-1, keepdims=True) + EPS)
    xn = x * r
    u = dy * g
    return r * (u - xn * jnp.mean(u * xn, axis=-1, keepdims=True)), jnp.sum(dy * xn, axis=0, keepdims=True)


def _mla_qkv_bwd(proj, cqn, ckvn, dqcat, dkcat, dv, dka, dva, gq, gkv, wq, wkv, tk_c, tk_s, *, layer, name):
    s = proj.shape[0]
    tm = min(256, s)
    t_cq, t_ka, t_ckv, t_va, t_kr = (o - CQ_OFF for o in (CQ_OFF, KA_OFF, CKV_OFF, VA_OFF, KR_OFF))

    def body(cq_ref, ckv_ref, cqn_ref, ckvn_ref, dq_ref, dk_ref, dv_ref, dka_ref, dva_ref, gq_ref, gkv_ref, wq_ref,
             wkv_ref, kc_ref, ks_ref,
             tile_ref, dwq_ref, dwkv_ref, dgq_ref, dgkv_ref, dqpre, dkv, dwq_acc, dwkv_acc):
        dcq_ref = tile_ref.at[:, t_cq:t_cq + Q_RANK]
        dckv_ref = tile_ref.at[:, t_ckv:t_ckv + KV_RANK]
        dkr_ref = tile_ref.at[:, t_kr:t_kr + 128]
        tile_ref[:, t_ka:t_ka + 128] = dka_ref[...]
        tile_ref[:, t_va:t_va + 128] = dva_ref[...]

        @pl.when(pl.program_id(0) == 0)
        def _():
            for r in (dwq_acc, dwkv_acc, dgq_ref, dgkv_ref):
                r[...] = jnp.zeros_like(r)

        kc, ks = kc_ref[...], ks_ref[...]
        dkrr = jnp.zeros((tm, 128), F32)
        for hh in range(MLA_HEADS):
            dqpre[:, hh * 256:hh * 256 + 128] = dq_ref[:, hh * 256:hh * 256 + 128].astype(MXU_DTYPE)
            blk = dq_ref[:, hh * 256 + 128:(hh + 1) * 256]
            dqpre[:, hh * 256 + 128:(hh + 1) * 256] = (blk * kc + _rope_partner(blk * ks, 0, 128)).astype(MXU_DTYPE)
            dkv[:, hh * 128:(hh + 1) * 128] = dk_ref[:, hh * 256:hh * 256 + 128].astype(MXU_DTYPE)
            dkrr = dkrr + dk_ref[:, hh * 256 + 128:(hh + 1) * 256]
        dkv[:, 1024:] = dv_ref[...].astype(MXU_DTYPE)
        dkr_ref[...] = (dkrr * kc + _rope_partner(dkrr * ks, 0, 128)).astype(dkr_ref.dtype)

        dq_b = dqpre[...]
        dwq_acc[...] += _dot(cqn_ref[...], dq_b, 0, 0)
        dcq, dgq = _norm_bwd(cq_ref[...], gq_ref[...], _dot(dq_b, wq_ref[...], 1, 0))
        dcq_ref[...] = dcq.astype(dcq_ref.dtype)
        dgq_ref[...] += dgq

        dkv_b = dkv[...]
        dwkv_acc[...] += _dot(ckvn_ref[...], dkv_b, 0, 0)
        dckv, dgkv = _norm_bwd(ckv_ref[...], gkv_ref[...], _dot(dkv_b, wkv_ref[...], 1, 0))
        dckv_ref[...] = dckv.astype(dckv_ref.dtype)
        dgkv_ref[...] += dgkv

        @pl.when(pl.program_id(0) == s // tm - 1)
        def _():
            dwq_ref[...] = dwq_acc[...].T
            dwkv_ref[...] = dwkv_acc[...].T

    row = lambda w, c: pl.BlockSpec((tm, w), lambda i: (i, c))
    full = lambda shape: pl.BlockSpec(shape, lambda i: (0, 0))
    of_layer = lambda a: _layer_spec(a.shape[-2:], lambda i: (0, 0), layer)
    return pl.pallas_call(
        body, name=name, grid=(s // tm,),
        in_specs=[row(Q_RANK, CQ_OFF // Q_RANK), row(KV_RANK, CKV_OFF // KV_RANK), row(Q_RANK, 0), row(KV_RANK, 0),
                  row(2048, 0), row(2048, 0), row(1024, 0), row(128, 0), row(128, 0), full(gq.shape), full(gkv.shape),
                  of_layer(wq), of_layer(wkv), row(128, 0), row(128, 0)],
        out_specs=[row(1024, 0), full(wq.shape[-2:]), full(wkv.shape[-2:]), full(gq.shape), full(gkv.shape)],
        out_shape=[jax.ShapeDtypeStruct((s, 1024), MXU_DTYPE), jax.ShapeDtypeStruct(wq.shape[-2:], F32),
                   jax.ShapeDtypeStruct(wkv.shape[-2:], F32), jax.ShapeDtypeStruct(gq.shape, F32),
                   jax.ShapeDtypeStruct(gkv.shape, F32)],
        scratch_shapes=[pltpu.VMEM((tm, 2048), MXU_DTYPE), pltpu.VMEM((tm, 2048), MXU_DTYPE),
                        pltpu.VMEM((Q_RANK, 2048), F32), pltpu.VMEM((KV_RANK, 2048), F32)],
        compiler_params=_cparams("arbitrary"),
    )(proj, proj, cqn, ckvn, dqcat, dkcat, dv, dka, dva, gq, gkv, wq, wkv, tk_c, tk_s)


def _loop_by_two(lo, hi, step):
    def pair(i, carry):
        step(lo + 2 * i)
        step(lo + 2 * i + 1)
        return carry

    lax.fori_loop(0, (hi - lo) // 2, pair, 0)

    @pl.when((hi - lo) % 2 == 1)
    def _():
        step(hi - 1)


def _causal_mask(t):
    return lax.broadcasted_iota(jnp.int32, (t, t), 1) <= lax.broadcasted_iota(jnp.int32, (t, t), 0)


def _mla_fwd(qcat, kcat, v, *, name, gather=None):
    s = qcat.shape[0]
    t = min(512, s)
    nq = s // t
    hp = HEADS_PER_STEP
    ng = MLA_HEADS // hp
    c2 = MLA_SCALE * LOG2E

    nw = 0 if gather is None else len(gather)

    def body(q_ref, k_ref, v_ref, *rest):
        src, (o_ref, lse_ref), out = rest[:nw], rest[nw:nw + 2], rest[nw + 2:2 * nw + 2]
        top_s, acc_s, *sems = rest[2 * nw + 2:]
        g, qi = pl.program_id(0), pl.program_id(1)
        if nw:
            @pl.when((g == 0) & (qi == 0))
            def _():
                _gather_start(src, out, *sems)

            @pl.when((g == ng - 1) & (qi == 0))
            def _():
                _gather_forward(src, out, *sems)

        rows = lambda j: pl.ds(pl.multiple_of(j * t, t), t)
        head = lambda e: slice(e * 256, (e + 1) * 256)
        raw = lambda e, j: _dot(q_ref[:, head(e)], k_ref[rows(j), head(e)], 1, 1)

        for e in range(hp):
            top_s[e] = jnp.where(_causal_mask(t), raw(e, qi), NEG)

        def pass1(j):
            for e in range(hp):
                top_s[e] = jnp.maximum(top_s[e], raw(e, j))

        _loop_by_two(0, qi, pass1)
        m = [jnp.max(top_s[e], axis=-1, keepdims=True) * c2 for e in range(hp)]

        def weighted(e, j, masked):
            sc = raw(e, j) * c2 - m[e]
            if masked:
                sc = jnp.where(_causal_mask(t), sc, NEG)
            return jnp.dot(jnp.exp2(sc).astype(MXU_DTYPE), v_ref[rows(j), head(e)], preferred_element_type=F32)

        for e in range(hp):
            acc_s[e] = weighted(e, qi, True)

        def pass2(j):
            for e in range(hp):
                acc_s[e] += weighted(e, j, False)

        _loop_by_two(0, qi, pass2)
        lane = lax.broadcasted_iota(jnp.int32, (t, 128), 1)
        stats = jnp.zeros((t, 128), F32)
        for e in range(hp):
            l = acc_s[e, :, 128:]
            o_ref[:, e * 128:(e + 1) * 128] = acc_s[e, :, :128] / l
            stats = jnp.where(lane == e, m[e] + jnp.log(l) * LOG2E, stats)
        lse_ref[...] = stats
        if nw:
            @pl.when((g == ng - 1) & (qi == nq - 1))
            def _():
                _gather_finish(src, out, *sems)

    outs = pl.pallas_call(
        body, name=name, grid=(ng, nq),
        in_specs=[pl.BlockSpec((t, 256 * hp), lambda g, qi: (qi, g)), pl.BlockSpec((s, 256 * hp), lambda g, qi: (0, g)),
                  pl.BlockSpec((s, 256 * hp), lambda g, qi: (0, g))] + [ANY] * nw,
        out_specs=[pl.BlockSpec((t, 128 * hp), lambda g, qi: (qi, g)), pl.BlockSpec((t, 128), lambda g, qi: (qi, g))]
        + [ANY] * nw,
        out_shape=[jax.ShapeDtypeStruct((s, 1024), F32), jax.ShapeDtypeStruct((s, 128 * ng), F32)]
        + (_gathered_shapes(gather) if nw else []),
        scratch_shapes=[pltpu.VMEM((hp, t, t), F32), pltpu.VMEM((hp, t, 256), F32)] + (_sems(6, nw) if nw else []),
        compiler_params=_cparams("arbitrary", "arbitrary"),
    )(qcat, kcat, v, *(gather or []))
    return outs[0], outs[1], list(outs[2:])


def _mla_bwd(qcat, kcat, v, do, lse, delta, *, name, scatter=None):
    s = qcat.shape[0]
    t = min(512, s)
    nq = s // t
    hp = HEADS_PER_STEP
    c2 = MLA_SCALE * LOG2E
    nw = 0 if scatter is None else len(scatter)

    def body(q_ref, k_ref, v_ref, do_ref, lse_ref, dl_ref, *rest):
        src, (dq_ref, dk_ref, dv_ref), out = rest[:nw], rest[nw:nw + 3], rest[nw + 3:2 * nw + 3]
        dk_acc, dv_acc, *sems = rest[2 * nw + 3:]
        h, ki = pl.program_id(0), pl.program_id(1)
        if nw:
            @pl.when((h == 0) & (ki == 0))
            def _():
                _scatter_start(src, out, *sems)

        @pl.when(ki == 0)
        def _():
            dq_ref[...] = jnp.zeros_like(dq_ref)

        dk_acc[...] = jnp.zeros_like(dk_acc)
        dv_acc[...] = jnp.zeros_like(dv_acc)
        k, vv = k_ref[...], v_ref[...]
        mine = lax.broadcasted_iota(jnp.int32, (t, 128), 1) == h % hp

        def chunk(qi, masked):
            rows = pl.ds(pl.multiple_of(qi * t, t), t)
            q, dob = q_ref[rows, :], do_ref[rows, :]
            pick = lambda r: jnp.sum(jnp.where(mine, r[rows, :], 0.0), axis=-1, keepdims=True)
            sc = _dot(q, k, 1, 1) * c2
            if masked:
                sc = jnp.where(_causal_mask(t), sc, NEG)
            p = jnp.exp2(sc - pick(lse_ref))
            dp = _dot(dob, vv, 1, 1)
            ds = (p * (dp - pick(dl_ref)) * MLA_SCALE).astype(MXU_DTYPE)
            dv_acc[...] += _dot(dob, p.astype(MXU_DTYPE), 0, 0)
            dk_acc[...] += _dot(q, ds, 0, 0)
            dq_ref[rows, :] += jnp.dot(ds, k, preferred_element_type=F32)

        chunk(ki, True)
        _loop_by_two(ki + 1, nq, lambda qi: chunk(qi, False))
        dk_ref[...] = dk_acc[...].T
        dv_ref[...] = dv_acc[...].T
        if nw:
            @pl.when((h == MLA_HEADS - 1) & (ki == nq - 1))
            def _():
                _scatter_finish(src, out, *sems)

    head = lambda w: pl.BlockSpec((s, w), lambda h, ki: (0, h))
    blk = lambda w: pl.BlockSpec((t, w), lambda h, ki: (ki, h))
    stat = pl.BlockSpec((s, 128), lambda h, ki: (0, h // hp))
    outs = pl.pallas_call(
        body, name=name, grid=(MLA_HEADS, nq),
        in_specs=[head(256), blk(256), pl.BlockSpec((t, 128), lambda h, ki: (ki, 2 * h)), head(128), stat, stat]
        + [ANY] * nw,
        out_specs=[head(256), blk(256), blk(128)] + [ANY] * nw,
        out_shape=[jax.ShapeDtypeStruct((s, 2048), F32), jax.ShapeDtypeStruct((s, 2048), F32),
                   jax.ShapeDtypeStruct((s, 1024), F32)] + [jax.ShapeDtypeStruct(a.shape, a.dtype) for a in scatter or []],
        scratch_shapes=[pltpu.VMEM((256, t), F32), pltpu.VMEM((128, t), F32)] + (_sems(3, nw) if nw else []),
        compiler_params=_cparams("arbitrary", "arbitrary"),
    )(qcat, kcat, v, do, lse, delta, *(scatter or []))
    return outs[0], outs[1], outs[2], list(outs[3:])


def _gate_specs(tm):
    half = lambda c: pl.BlockSpec((tm, 1024), lambda i: (i, c))
    return half(0), half(GA_OFF // 1024), half(GB_OFF // 1024)


def _gate_fwd(oa, ob, proj, *, name):
    s = oa.shape[0]
    tm = min(512, s)

    def body(oa_ref, ob_ref, ga_ref, gb_ref, y_ref):
        ga, gb = ga_ref[...], gb_ref[...]
        y_ref[:, :1024] = (oa_ref[...] * (ga * jax.nn.sigmoid(ga))).astype(MXU_DTYPE)
        y_ref[:, 1024:] = (ob_ref[...] * (gb * jax.nn.sigmoid(gb))).astype(MXU_DTYPE)

    o_spec, ga_spec, gb_spec = _gate_specs(tm)
    return pl.pallas_call(
        body, name=name, grid=(s // tm,), in_specs=[o_spec, o_spec, ga_spec, gb_spec],
        out_specs=pl.BlockSpec((tm, 2048), lambda i: (i, 0)),
        out_shape=jax.ShapeDtypeStruct((s, 2048), MXU_DTYPE),
        compiler_params=_cparams("parallel"),
    )(oa, ob, proj, proj)


def _gate_bwd(dy, oa, ob, proj, *, name):
    s = oa.shape[0]
    tm = min(512, s)

    def body(dy_ref, oa_ref, ob_ref, ga_ref, gb_ref, doa_ref, dob_ref, dga_ref, dgb_ref, dl_ref):
        def branch(dyv, o, g, do_ref, dg_ref):
            sg = jax.nn.sigmoid(g)
            do = dyv * (g * sg)
            do_ref[...] = do.astype(MXU_DTYPE)
            dg_ref[...] = (dyv * o * (sg * (1.0 + g * (1.0 - sg)))).astype(MXU_DTYPE)
            return do

        branch(dy_ref[:, :1024], oa_ref[...], ga_ref[...], doa_ref, dga_ref)
        ob = ob_ref[...]
        prod = branch(dy_ref[:, 1024:], ob, gb_ref[...], dob_ref, dgb_ref) * ob
        lane = lax.broadcasted_iota(jnp.int32, (tm, stat_w), 1)
        acc = jnp.zeros((tm, stat_w), F32)
        for hh in range(MLA_HEADS):
            at = (hh // HEADS_PER_STEP) * 128 + hh % HEADS_PER_STEP
            acc = jnp.where(lane == at, jnp.sum(prod[:, hh * 128:(hh + 1) * 128], axis=-1, keepdims=True), acc)
        dl_ref[...] = acc

    stat_w = 128 * (MLA_HEADS // HEADS_PER_STEP)
    o_spec, ga_spec, gb_spec = _gate_specs(tm)
    return pl.pallas_call(
        body, name=name, grid=(s // tm,),
        in_specs=[pl.BlockSpec((tm, 2048), lambda i: (i, 0)), o_spec, o_spec, ga_spec, gb_spec],
        out_specs=[o_spec, o_spec, o_spec, o_spec, pl.BlockSpec((tm, stat_w), lambda i: (i, 0))],
        out_shape=[jax.ShapeDtypeStruct((s, 1024), MXU_DTYPE)] * 4 + [jax.ShapeDtypeStruct((s, stat_w), F32)],
        compiler_params=_cparams("parallel"),
    )(dy, oa, ob, proj, proj)


def _row_block(rows, cols, itemsize=4, budget=2 << 20):
    fits = [tr for tr in range(16, rows + 1, 16) if rows % tr == 0 and tr * cols * itemsize <= budget]
    return fits[-1] if fits else rows


def _adamw(w, g, m, v, *, name):
    shape = w.shape
    rows, cols = shape[-2:]
    w3, g3, m3, v3 = (a.reshape((-1, rows, cols)) for a in (w, g, m, v))
    lead = w3.shape[0]
    tr = _row_block(rows, cols)

    def body(w_ref, g_ref, m_ref, v_ref, d_ref, mo_ref, vo_ref):
        gv = g_ref[...]
        mn = ADAM_B1 * m_ref[...] + (1.0 - ADAM_B1) * gv
        vn = ADAM_B2 * v_ref[...] + (1.0 - ADAM_B2) * jnp.square(gv)
        m_hat = mn / (1.0 - ADAM_B1 ** ADAM_STEP)
        v_hat = vn / (1.0 - ADAM_B2 ** ADAM_STEP)
        d_ref[...] = -ADAM_LR * (m_hat / (jnp.sqrt(v_hat) + ADAM_EPS) + ADAM_WD * w_ref[...])
        mo_ref[...] = mn
        vo_ref[...] = vn

    spec = pl.BlockSpec((None, tr, cols), lambda a, i: (a, i, 0))
    outs = pl.pallas_call(
        body, name=name, grid=(lead, rows // tr), in_specs=[spec] * 4, out_specs=[spec] * 3,
        out_shape=[jax.ShapeDtypeStruct((lead, rows, cols), F32)] * 3,
        compiler_params=_cparams("parallel", "parallel"),
    )(w3, g3, m3, v3)
    return tuple(o.reshape(shape) for o in outs)


def _pair_sum(where, grads, recv, *, name):
    layers, chips, _, rows, cols = grads.shape
    tr = _row_block(rows, cols)

    def body(where_ref, a_ref, b_ref, o_ref):
        o_ref[...] = (a_ref[...] + b_ref[...].astype(F32)).astype(WIRE_DTYPE)

    spec = pl.BlockSpec((None, None, tr, cols), lambda a, k, i, w: (a, k, i, 0))
    return pl.pallas_call(
        body, name=name,
        grid_spec=pltpu.PrefetchScalarGridSpec(
            num_scalar_prefetch=1, grid=(layers, chips, rows // tr),
            in_specs=[pl.BlockSpec((None, None, None, tr, cols), lambda a, k, i, w: (a, k, w[4], i, 0)), spec],
            out_specs=spec),
        out_shape=jax.ShapeDtypeStruct((layers, chips, rows, cols), WIRE_DTYPE),
        compiler_params=_cparams("parallel", "parallel", "parallel"),
    )(where, grads, recv)


def _chip_sum(where, grads, recv, parts, *, name):
    layers, _, _, rows, cols = grads.shape
    tr = _row_block(rows, cols)

    def body(where_ref, a_ref, b_ref, t0_ref, t1_ref, t2_ref, o_ref):
        total = a_ref[...] + b_ref[...].astype(F32)
        for t_ref in (t0_ref, t1_ref, t2_ref):
            total = total + t_ref[...].astype(F32)
        o_ref[...] = total

    slot = lambda j: pl.BlockSpec((None, None, tr, cols), lambda a, i, w: (a, w[j], i, 0))
    return pl.pallas_call(
        body, name=name,
        grid_spec=pltpu.PrefetchScalarGridSpec(
            num_scalar_prefetch=1, grid=(layers, rows // tr),
            in_specs=[pl.BlockSpec((None, None, None, tr, cols), lambda a, i, w: (a, w[0], w[4], i, 0)), slot(0), slot(1),
                      slot(2), slot(3)],
            out_specs=slot(4)),
        out_shape=jax.ShapeDtypeStruct((layers, 2, rows, cols), F32),
        compiler_params=_cparams("parallel", "parallel"),
    )(where, grads, recv, parts, parts, parts)


def _place():
    x, y, c = lax.axis_index("x"), lax.axis_index("y"), lax.axis_index("c")
    chips = [(1 - x, y), (x, 1 - y), (1 - x, 1 - y)]
    return x, y, c, chips


def _sems(*shape):
    return [pltpu.SemaphoreType.DMA(shape), pltpu.SemaphoreType.DMA(shape)]


OWNER_CORE = (0, 1, 1, 1)


def _gather_copy(src, out, send_sems, recv_sems, t, sem, slot, to, forward=False):
    n = src[t].shape[0]
    rows = out[t].at[pl.ds(pl.multiple_of(slot * n, 16), n)]
    return pltpu.make_async_remote_copy(src_ref=rows if forward else src[t], dst_ref=rows, send_sem=send_sems.at[sem, t],
                                        recv_sem=recv_sems.at[sem, t], device_id=to, device_id_type=MESH)


def _gather_start(src, out, send_sems, recv_sems):
    x, y, c, chips = _place()
    for t, owner in enumerate(OWNER_CORE):
        @pl.when(c == owner)
        def _():
            for j, chip in enumerate(chips):
                _gather_copy(src, out, send_sems, recv_sems, t, j, 2 * x + y, (*chip, c)).start()


def _gather_forward(src, out, send_sems, recv_sems):
    x, y, c, chips = _place()
    for t, owner in enumerate(OWNER_CORE):
        @pl.when(c == owner)
        def _():
            for j, (px, py) in enumerate(chips):
                _gather_copy(src, out, send_sems, recv_sems, t, j, 2 * px + py, (px, py, c)).wait_recv()
                _gather_copy(src, out, send_sems, recv_sems, t, 3 + j, 2 * px + py, (x, y, 1 - c), forward=True).start()


def _gather_finish(src, out, send_sems, recv_sems):
    x, y, c, chips = _place()
    slots = [2 * px + py for px, py in chips]
    for t, owner in enumerate(OWNER_CORE):
        copy = functools.partial(_gather_copy, src, out, send_sems, recv_sems, t)

        @pl.when(c == owner)
        def _():
            for j, (px, py) in enumerate(chips):
                copy(j, 2 * x + y, (px, py, c)).wait_send()
                copy(3 + j, slots[j], (x, y, 1 - c), forward=True).wait_send()

        @pl.when(c != owner)
        def _():
            for j in range(3):
                copy(3 + j, slots[j], (x, y, 1 - c), forward=True).wait_recv()


def _gathered_shapes(shards):
    return [jax.ShapeDtypeStruct((4 * a.shape[0], a.shape[1]), a.dtype) for a in shards]


def _comm_gather_layer(shards, *, name):
    nt = len(shards)

    def body(*refs):
        src, out, sems = refs[:nt], refs[nt:2 * nt], refs[2 * nt:]
        _gather_start(src, out, *sems)
        _gather_forward(src, out, *sems)
        _gather_finish(src, out, *sems)

    return pl.pallas_call(
        body, name=name, in_specs=[ANY] * nt, out_specs=[ANY] * nt, out_shape=_gathered_shapes(shards),
        scratch_shapes=_sems(6, nt),
    )(*shards)


def _comm_swap_sibling(bufs, *, name):
    nt = len(bufs)

    def body(*refs):
        src, out, (send_sems, recv_sems) = refs[:nt], refs[nt:2 * nt], refs[2 * nt:]
        x, y, c, _ = _place()
        cps = [pltpu.make_async_remote_copy(src_ref=src[t], dst_ref=out[t], send_sem=send_sems.at[t], recv_sem=recv_sems.at[t],
                                            device_id=(x, y, 1 - c), device_id_type=MESH) for t in range(nt)]
        for cp in cps:
            cp.start()
        for cp in cps:
            cp.wait()

    return pl.pallas_call(
        body, name=name, in_specs=[ANY] * nt, out_specs=[ANY] * nt,
        out_shape=[jax.ShapeDtypeStruct(a.shape, a.dtype) for a in bufs], scratch_shapes=_sems(nt),
    )(*bufs)


def _scatter_copy(src, out, send_sems, recv_sems, j, t, from_slot, to_slot, to):
    return pltpu.make_async_remote_copy(src_ref=src[t].at[:, from_slot], dst_ref=out[t].at[:, to_slot],
                                        send_sem=send_sems.at[j, t], recv_sem=recv_sems.at[j, t], device_id=to,
                                        device_id_type=MESH)


def _scatter_start(src, out, send_sems, recv_sems):
    x, y, c, chips = _place()
    for j, (px, py) in enumerate(chips):
        for t in range(len(src)):
            _scatter_copy(src, out, send_sems, recv_sems, j, t, 2 * px + py, 2 * x + y, (px, py, c)).start()


def _scatter_finish(src, out, send_sems, recv_sems):
    x, y, c, chips = _place()
    for j, (px, py) in enumerate(chips):
        for t in range(len(src)):
            _scatter_copy(src, out, send_sems, recv_sems, j, t, 2 * x + y, 2 * px + py, (px, py, c)).wait_recv()
    for j, (px, py) in enumerate(chips):
        for t in range(len(src)):
            _scatter_copy(src, out, send_sems, recv_sems, j, t, 2 * px + py, 2 * x + y, (px, py, c)).wait_send()


def _comm_scatter_chips(parts, *, name):
    nt = len(parts)

    def body(*refs):
        src, out, sems = refs[:nt], refs[nt:2 * nt], refs[2 * nt:]
        _scatter_start(src, out, *sems)
        _scatter_finish(src, out, *sems)

    return pl.pallas_call(
        body, name=name, in_specs=[ANY] * nt, out_specs=[ANY] * nt,
        out_shape=[jax.ShapeDtypeStruct(a.shape, a.dtype) for a in parts], scratch_shapes=_sems(3, nt),
    )(*parts)


def _comm_join_halves(bufs, *, name):
    nt = len(bufs)

    def body(*refs):
        src, out, (send_sems, recv_sems) = refs[:nt], refs[nt:2 * nt], refs[2 * nt:]
        x, y, c, _ = _place()
        copy = lambda t, half: pltpu.make_async_remote_copy(
            src_ref=src[t].at[:, c], dst_ref=out[t].at[:, half], send_sem=send_sems.at[t], recv_sem=recv_sems.at[t],
            device_id=(x, y, 1 - c), device_id_type=MESH)
        sends = [copy(t, c) for t in range(nt)]
        for cp in sends:
            cp.start()
        for t in range(nt):
            copy(t, 1 - c).wait_recv()
        for cp in sends:
            cp.wait_send()

    return pl.pallas_call(
        body, name=name, in_specs=[ANY] * nt, out_specs=[ANY] * nt,
        out_shape=[jax.ShapeDtypeStruct(a.shape, a.dtype) for a in bufs],
        input_output_aliases={t: t for t in range(nt)}, scratch_shapes=_sems(nt),
    )(*bufs)


def _comm_allreduce_small(part, *, name):
    rows, cols = part.shape

    def body(p_ref, o_ref, buf, send_sems, recv_sems):
        x, y, c, _ = _place()
        me = 4 * x + 2 * y + c
        buf[me] = p_ref[...]
        flip = lambda v, bit: 1 - v if bit else v
        peers = [(flip(x, d & 4), flip(y, d & 2), flip(c, d & 1)) for d in range(1, 8)]
        sends = []
        for j, peer in enumerate(peers):
            cp = pltpu.make_async_remote_copy(src_ref=buf.at[me], dst_ref=buf.at[me], send_sem=send_sems.at[j],
                                              recv_sem=recv_sems.at[j], device_id=peer, device_id_type=MESH)
            cp.start()
            sends.append(cp)
        for j, (px, py, pc) in enumerate(peers):
            pltpu.make_async_remote_copy(src_ref=buf.at[me], dst_ref=buf.at[4 * px + 2 * py + pc], send_sem=send_sems.at[j],
                                         recv_sem=recv_sems.at[j], device_id=(px, py, pc), device_id_type=MESH).wait_recv()
        for cp in sends:
            cp.wait_send()
        total = buf[0]
        for i in range(1, 8):
            total = total + buf[i]
        o_ref[...] = total

    vm = pl.BlockSpec(memory_space=pltpu.VMEM)
    return pl.pallas_call(
        body, name=name, in_specs=[vm], out_specs=vm, out_shape=jax.ShapeDtypeStruct((rows, cols), F32),
        scratch_shapes=[pltpu.VMEM((8, rows, cols), F32), pltpu.SemaphoreType.DMA((7,)), pltpu.SemaphoreType.DMA((7,))],
    )(part)


def _pad_in_rows(wt):
    r = lambda o, n: wt[..., o:o + n, :]
    kr = r(2944, 64)
    return jnp.concatenate([r(0, 1024), r(1280, 1024), r(3008, 1024), r(2304, Q_RANK), r(1024, 128), r(2688, KV_RANK),
                            r(1152, 128), kr, jnp.zeros_like(kr)], axis=-2)


def _unpad_in_rows(qa, ga, gb, mixed):
    cq, ka, ckv, va, kr = (mixed[..., o - CQ_OFF:o - CQ_OFF + n, :] for o, n in (
        (CQ_OFF, Q_RANK), (KA_OFF, 128), (CKV_OFF, KV_RANK), (VA_OFF, 128), (KR_OFF, 64)))
    return jnp.concatenate([qa, ka, va, ga, cq, ckv, kr, gb], axis=-2)


def _pad_q_rows(wt):
    lead, cols = wt.shape[:-2], wt.shape[-1]
    wt = jnp.pad(wt.reshape(lead + (MLA_HEADS, 192, cols)), [(0, 0)] * (len(lead) + 1) + [(0, 64), (0, 0)])
    return wt.reshape(lead + (MLA_HEADS * 256, cols))


def _unpad_q_rows(wt):
    lead, cols = wt.shape[:-2], wt.shape[-1]
    return wt.reshape(lead + (MLA_HEADS, 256, cols))[..., :192, :].reshape(lead + (MLA_HEADS * 192, cols))


def _perm_kv_rows(wt):
    lead, cols = wt.shape[:-2], wt.shape[-1]
    return jnp.swapaxes(wt.reshape(lead + (MLA_HEADS, 2, 128, cols)), -4, -3).reshape(lead + (2048, cols))


def _unperm_kv_rows(wt):
    lead, cols = wt.shape[:-2], wt.shape[-1]
    return jnp.swapaxes(wt.reshape(lead + (2, MLA_HEADS, 128, cols)), -4, -3).reshape(lead + (2048, cols))


def _t(a):
    return jnp.swapaxes(a, -1, -2)


def _rope_tables(s):
    pos = jnp.arange(s, dtype=F32)
    inv_freq = 10000.0 ** (-jnp.arange(0, 64, 2, dtype=F32) / 64)
    ang = pos[:, None] * inv_freq[None, :]
    cos, sin = jnp.cos(ang), jnp.sin(ang)
    z64 = jnp.zeros((s, 64), F32)
    tk_c = jnp.concatenate([cos, cos, z64], axis=-1)
    tk_s = jnp.concatenate([-sin, sin, z64], axis=-1)
    return tk_c, tk_s


def _layer_weights(gathered, own, chip):
    def with_own_rows(g, o):
        n = o.shape[0]
        return jnp.concatenate([lax.select(chip == j, o, g[j * n:(j + 1) * n]) for j in range(4)], axis=0)

    full_in, full_q, full_kv, full_o = (with_own_rows(g, o) for g, o in zip(gathered, own))
    return _pad_in_rows(full_in), _pad_q_rows(full_q), _perm_kv_rows(full_kv), full_o


def _device_step(xs, tgt, attn_g, sinks, gq, gkv, final_g, shards, where):
    chip = where[0]
    depth = shards[0].shape[0]
    s = xs.shape[0]
    tabs = _rope_tables(s)
    saved = []
    x = xs
    of_layer = lambda l: [a[l] for a in shards]
    gathered = _comm_gather_layer(of_layer(0), name="comm_gather_layer0")
    weights = []
    for l in range(depth):
        w_in_p, w_q_p, w_kv_p, w_o = _layer_weights(gathered, of_layer(l), chip)
        weights.append((w_in_p, w_q_p, w_kv_p, w_o))
        h = _rmsnorm_fwd(x, attn_g[l:l + 1], name=f"norm_fwd{l}")
        proj = _matmul(h, w_in_p, tb=True, name=f"in_proj{l}")
        oa, lse_a = _swa_fwd(proj, sinks[l], name=f"swa_fwd{l}")
        qcat, kcat, v, cqn, ckvn = _mla_qkv_fwd(proj, gq[l:l + 1], gkv[l:l + 1], w_q_p, w_kv_p, *tabs, layer=None,
                                                name=f"mla_qkv_fwd{l}")
        ob, lse_b, gathered = _mla_fwd(qcat, kcat, v, gather=of_layer(l + 1) if l + 1 < depth else None,
                                       name=f"mla_fwd{l}")
        y = _gate_fwd(oa, ob, proj, name=f"gate_fwd{l}")
        x_next = _matmul(y, w_o, add=x, name=f"out_proj{l}")
        saved.append((x, h, proj, oa, lse_a, qcat, kcat, v, cqn, ckvn, ob, lse_b, y))
        x = x_next

    dx, d_final_g, loss = _final_loss(x, final_g, tgt, name="final_loss")

    d_attn_g, d_sinks, d_gq, d_gkv = [None] * depth, [None] * depth, [None] * depth, [None] * depth
    d_w_in, d_w_q, d_w_kv, d_w_o = [None] * depth, [None] * depth, [None] * depth, [None] * depth
    half = depth // 2
    carried = {half - 1: list(range(half, depth)), **{l - 1: [l] for l in range(1, half)}}
    done = {}

    def grads_by_chip(layers):
        stack = lambda per_layer: jnp.stack([per_layer[l] for l in layers])
        in_tiles = [stack([tiles[j] if tiles else None for tiles in d_w_in]) for j in range(4)]
        full = [_unpad_in_rows(*in_tiles), _unpad_q_rows(stack(d_w_q)), _unperm_kv_rows(stack(d_w_kv)), stack(d_w_o)]
        return [g.reshape(g.shape[0], 4, 2, g.shape[1] // 8, g.shape[2]) for g in full]

    def reduce_begin(grads5, tag):
        give = [lax.dynamic_index_in_dim(g, 1 - where[4], axis=2, keepdims=False).astype(WIRE_DTYPE) for g in grads5]
        recv = _comm_swap_sibling(give, name=f"comm_swap_sibling_{tag}")
        return recv, [_pair_sum(where, g, r, name=f"pair_sum_{tag}_{t}") for t, (g, r) in enumerate(zip(grads5, recv))]

    def reduce_end(grads5, recv, parts, tag):
        mine = [_chip_sum(where, g, r, p, name=f"chip_sum_{tag}_{t}") for t, (g, r, p) in enumerate(zip(grads5, recv, parts))]
        joined = _comm_join_halves(mine, name=f"comm_join_halves_{tag}")
        return [a.reshape(a.shape[0], -1, a.shape[-1]) for a in joined]

    for l in reversed(range(depth)):
        x, h, proj, oa, lse_a, qcat, kcat, v, cqn, ckvn, ob, lse_b, y = saved[l]
        w_in_p, w_q_p, w_kv_p, w_o = weights[l]
        dy = _matmul(dx, w_o, tb=True, name=f"out_proj_dx{l}")
        d_w_o[l] = _matmul(y, dx, ta=True, name=f"out_proj_dw{l}")
        doa, dob, dga, dgb, delta_b = _gate_bwd(dy, oa, ob, proj, name=f"gate_bwd{l}")
        dqa, dka, dva, dsink = _swa_bwd(proj, sinks[l], lse_a, doa, name=f"swa_bwd{l}")
        phase, pair = carried.get(l), None
        if phase:
            grads5 = grads_by_chip(phase)
            recv, pair = reduce_begin(grads5, f"l{phase[0]}")
        dqc, dkc, dv, parts = _mla_bwd(qcat, kcat, v, dob, lse_b, delta_b, scatter=pair, name=f"mla_bwd{l}")
        if phase:
            done[phase[0]] = reduce_end(grads5, recv, parts, f"l{phase[0]}")
        mixed, d_w_q[l], d_w_kv[l], dgq_l, dgkv_l = _mla_qkv_bwd(
            proj, cqn, ckvn, dqc, dkc, dv, dka, dva, gq[l:l + 1], gkv[l:l + 1], w_q_p, w_kv_p, *tabs, layer=None,
            name=f"mla_qkv_bwd{l}")
        dproj = [dqa, dga, dgb, mixed]
        dh = _matmul_ktiles(dproj, w_in_p, name=f"in_proj_dx{l}")
        d_w_in[l] = [_matmul(tile, h, ta=True, name=f"in_proj_dw{l}_{j}") for j, tile in enumerate(dproj)]
        dx, dg_l = _rmsnorm_bwd(dh, x, attn_g[l:l + 1], dx, name=f"norm_bwd{l}")
        d_attn_g[l], d_sinks[l], d_gq[l], d_gkv[l] = dg_l, dsink[0:1, :SWA_HEADS], dgq_l, dgkv_l

    grads5 = grads_by_chip([0])
    recv, pair = reduce_begin(grads5, "l0")
    done[0] = reduce_end(grads5, recv, _comm_scatter_chips(pair, name="comm_scatter_chips_l0"), "l0")
    cat = lambda parts: jnp.concatenate(parts, axis=0)
    reduced = [cat([done[first][t] for first in sorted(done)]) for t in range(len(shards))]
    return loss, dx, cat(d_attn_g), cat(d_sinks), cat(d_gq), cat(d_gkv), d_final_g, reduced


def kernel(x, attn_norm_g, w_in, swa_sinks, q_a_norm_g, kv_a_norm_g, w_q_b, w_kv_b, w_out, final_norm_g, loss_target, m_attn_norm_g, m_w_in, m_swa_sinks, m_q_a_norm_g, m_kv_a_norm_g, m_w_q_b, m_w_kv_b, m_w_out, m_final_norm_g, v_attn_norm_g, v_w_in, v_swa_sinks, v_q_a_norm_g, v_kv_a_norm_g, v_w_q_b, v_w_kv_b, v_w_out, v_final_norm_g):
    x_, y_, c = lax.axis_index("x"), lax.axis_index("y"), lax.axis_index("c")
    chip = 2 * x_ + y_
    where = jnp.stack([chip, 2 * (1 - x_) + y_, 2 * x_ + 1 - y_, 2 * (1 - x_) + 1 - y_, c]).astype(jnp.int32)

    sent = [a.astype(WIRE_DTYPE) for a in (_t(w_in), _t(w_q_b), _t(w_kv_b), w_out)]

    loss, dx, d_attn_g, d_sinks, d_gq, d_gkv, d_final_g, reduced = _device_step(
        x[0], loss_target[0], attn_norm_g, swa_sinks, q_a_norm_g, kv_a_norm_g, final_norm_g.reshape(1, -1), sent, where)
    g_w_in, g_w_q_b, g_w_kv_b, g_w_out = _t(reduced[0]), _t(reduced[1]), _t(reduced[2]), reduced[3]

    small = [d_attn_g, d_sinks, d_gq, d_gkv, d_final_g, loss[:, :1]]
    flat = jnp.concatenate([a.reshape(-1) for a in small])
    n_small = flat.shape[0]
    rows = -(-n_small // 1024) * 8
    total = _comm_allreduce_small(jnp.pad(flat, (0, rows * 128 - n_small)).reshape(rows, 128),
                                  name="comm_allreduce_small").reshape(-1)
    outs, at = [], 0
    for a in small:
        outs.append(total[at:at + a.size].reshape(a.shape))
        at += a.size
    g_attn_g, g_sinks, g_gq, g_gkv, g_final_g, loss_total = outs
    g_final_g = g_final_g.reshape(final_norm_g.shape)

    weights = [attn_norm_g, w_in, swa_sinks, q_a_norm_g, kv_a_norm_g, w_q_b, w_kv_b, w_out, final_norm_g]
    grads = [g_attn_g, g_w_in, g_sinks, g_gq, g_gkv, g_w_q_b, g_w_kv_b, g_w_out, g_final_g]
    ms = [m_attn_norm_g, m_w_in, m_swa_sinks, m_q_a_norm_g, m_kv_a_norm_g, m_w_q_b, m_w_kv_b, m_w_out, m_final_norm_g]
    vs = [v_attn_norm_g, v_w_in, v_swa_sinks, v_q_a_norm_g, v_kv_a_norm_g, v_w_q_b, v_w_kv_b, v_w_out, v_final_norm_g]
    as2d = lambda a: a.reshape(1, -1) if a.ndim == 1 else a
    deltas, new_m, new_v = [], [], []
    for i, (w, g, m, v) in enumerate(zip(weights, grads, ms, vs)):
        view = _t if w is w_in else as2d
        d, mn, vn = _adamw(view(w), reduced[0] if w is w_in else view(g), view(m), view(v), name=f"adamw{i}")
        back = _t if w is w_in else (lambda a: a.reshape(w.shape))
        deltas.append(back(d))
        new_m.append(back(mn))
        new_v.append(back(vn))

    return (loss_total.reshape(()), dx[None], *grads, *deltas, *new_m, *new_v)
```

```python
import functools
import math

import jax
import jax.numpy as jnp
from jax import lax
from jax.experimental import pallas as pl
from jax.experimental.pallas import tpu as pltpu

F32 = jnp.float32
MXU_DTYPE = jnp.bfloat16
WIRE_DTYPE = jnp.bfloat16

EPS = 1e-6
NEG = -1e30
BLOCK = 128
D_MODEL = 2048
SWA_HEADS = 16
MLA_HEADS = 8
Q_RANK = 384
KV_RANK = 256
IN_WIDTH = 4032
MLA_SCALE = 192 ** -0.5
SWA_SCALE = 64 ** -0.5
LOG2E = math.log2(math.e)
HEADS_PER_STEP = 2
SLOPES = tuple(2.0 ** (-8.0 * (h + 1) / SWA_HEADS) for h in range(SWA_HEADS))

P_WIDTH = 4096
QA_OFF, GA_OFF, GB_OFF, CQ_OFF, KA_OFF, CKV_OFF, VA_OFF, KR_OFF = 0, 1024, 2048, 3072, 3456, 3584, 3840, 3968

ADAM_LR, ADAM_B1, ADAM_B2, ADAM_EPS, ADAM_WD, ADAM_STEP = 0.001, 0.9, 0.999, 1e-08, 0.01, 10

VMEM_LIMIT = 56 * 1024 * 1024
MESH = pl.DeviceIdType.MESH
ANY = pl.BlockSpec(memory_space=pl.ANY)


def _cparams(*sem):
    return pltpu.CompilerParams(dimension_semantics=sem, vmem_limit_bytes=VMEM_LIMIT)


def _dot(a, b, ca, cb):
    return lax.dot_general(a, b, (((ca,), (cb,)), ((), ())), preferred_element_type=F32)


def _layer_spec(block, index_map, layer):
    if layer is None:
        return pl.BlockSpec(block, index_map)
    return pl.BlockSpec((None,) + tuple(block), lambda *g: (layer,) + tuple(index_map(*g)))


def _matmul(a, b, *, name, ta=False, tb=False, out_t=False, out_dtype=F32, add=None, b_layer=None, tm=1024, tn=1024,
            tk=2048):
    (kdim, m) = a.shape if ta else a.shape[::-1]
    (n, k2) = b.shape[-2:] if tb else b.shape[-2:][::-1]
    assert kdim == k2, (a.shape, b.shape)
    tm, tn, tk = min(tm, m), min(tn, n), min(tk, kdim)
    assert m % tm == 0 and n % tn == 0 and kdim % tk == 0
    nk = kdim // tk

    def body(*refs):
        a_ref, b_ref = refs[:2]
        add_ref = None if add is None else refs[2]
        o_ref = refs[2 + (add is not None)]
        part = _dot(a_ref[...].astype(MXU_DTYPE), b_ref[...].astype(MXU_DTYPE), 0 if ta else 1, 1 if tb else 0)

        def finish(r):
            r = r if add is None else add_ref[...] + r
            o_ref[...] = (r.T if out_t else r).astype(out_dtype)

        if nk == 1:
            finish(part)
            return
        acc = refs[-1]
        k = pl.program_id(2)

        @pl.when(k == 0)
        def _():
            acc[...] = part

        @pl.when((k > 0) & (k < nk - 1))
        def _():
            acc[...] += part

        @pl.when(k == nk - 1)
        def _():
            finish(acc[...] + part)

    a_spec = pl.BlockSpec((tk, tm), lambda i, j, k: (k, i)) if ta else pl.BlockSpec((tm, tk), lambda i, j, k: (i, k))
    b_spec = (_layer_spec((tn, tk), lambda i, j, k: (j, k), b_layer) if tb else
              _layer_spec((tk, tn), lambda i, j, k: (k, j), b_layer))
    in_specs, args = [a_spec, b_spec], [a, b]
    if add is not None:
        in_specs.append(pl.BlockSpec((tm, tn), lambda i, j, k: (i, j)))
        args.append(add)
    out_spec = pl.BlockSpec((tn, tm), lambda i, j, k: (j, i)) if out_t else pl.BlockSpec((tm, tn), lambda i, j, k: (i, j))
    return pl.pallas_call(
        body, name=name, grid=(m // tm, n // tn, nk), in_specs=in_specs, out_specs=out_spec,
        out_shape=jax.ShapeDtypeStruct((n, m) if out_t else (m, n), out_dtype),
        scratch_shapes=[pltpu.VMEM((tm, tn), F32)] if nk > 1 else [],
        compiler_params=_cparams("parallel", "parallel", "arbitrary"),
    )(*args)


def _matmul_ktiles(a_tiles, b, *, name, b_layer=None, tm=512, tn=1024):
    m, kt = a_tiles[0].shape
    n = b.shape[-1]
    nt = len(a_tiles)
    assert b.shape[-2] == nt * kt
    tm, tn = min(tm, m), min(tn, n)
    assert m % tm == 0 and n % tn == 0

    def body(*refs):
        a_refs, b_refs, o_ref = refs[:nt], refs[nt:2 * nt], refs[2 * nt]
        acc = _dot(a_refs[0][...].astype(MXU_DTYPE), b_refs[0][...].astype(MXU_DTYPE), 1, 0)
        for j in range(1, nt):
            acc += _dot(a_refs[j][...].astype(MXU_DTYPE), b_refs[j][...].astype(MXU_DTYPE), 1, 0)
        o_ref[...] = acc

    in_specs = [pl.BlockSpec((tm, kt), lambda jn, i: (i, 0))] * nt
    in_specs += [_layer_spec((kt, tn), lambda jn, i, j=j: (j, jn), b_layer) for j in range(nt)]
    return pl.pallas_call(
        body, name=name, grid=(n // tn, m // tm), in_specs=in_specs,
        out_specs=pl.BlockSpec((tm, tn), lambda jn, i: (i, jn)),
        out_shape=jax.ShapeDtypeStruct((m, n), F32),
        compiler_params=_cparams("parallel", "parallel"),
    )(*a_tiles, *([b] * nt))


def _rmsnorm_fwd(x, g, *, name):
    s, d = x.shape
    tm = min(512, s)

    def body(x_ref, g_ref, h_ref, ht_ref):
        xv = x_ref[...]
        r = lax.rsqrt(jnp.mean(xv * xv, axis=-1, keepdims=True) + EPS)
        h = xv * r * g_ref[...]
        h_ref[...] = h.astype(MXU_DTYPE)
        ht_ref[...] = h.T.astype(MXU_DTYPE)

    return pl.pallas_call(
        body, name=name, grid=(s // tm,),
        in_specs=[pl.BlockSpec((tm, d), lambda i: (i, 0)), pl.BlockSpec((1, d), lambda i: (0, 0))],
        out_specs=[pl.BlockSpec((tm, d), lambda i: (i, 0)), pl.BlockSpec((d, tm), lambda i: (0, i))],
        out_shape=[jax.ShapeDtypeStruct((s, d), MXU_DTYPE), jax.ShapeDtypeStruct((d, s), MXU_DTYPE)],
        compiler_params=_cparams("parallel"),
    )(x, g)


def _rmsnorm_bwd(dh, x, g, dres, *, name):
    s, d = x.shape
    tm = min(512, s)

    def body(dh_ref, x_ref, g_ref, dres_ref, dx_ref, dg_ref):
        @pl.when(pl.program_id(0) == 0)
        def _():
            dg_ref[...] = jnp.zeros_like(dg_ref)

        xv = x_ref[...]
        r = lax.rsqrt(jnp.mean(xv * xv, axis=-1, keepdims=True) + EPS)
        xn = xv * r
        dy = dh_ref[...]
        dg_ref[...] += jnp.sum(dy * xn, axis=0, keepdims=True)
        u = dy * g_ref[...]
        dx_ref[...] = dres_ref[...] + r * (u - xn * jnp.mean(u * xn, axis=-1, keepdims=True))

    row = pl.BlockSpec((tm, d), lambda i: (i, 0))
    vec = pl.BlockSpec((1, d), lambda i: (0, 0))
    return pl.pallas_call(
        body, name=name, grid=(s // tm,), in_specs=[row, row, vec, row], out_specs=[row, vec],
        out_shape=[jax.ShapeDtypeStruct((s, d), F32), jax.ShapeDtypeStruct((1, d), F32)],
        compiler_params=_cparams("arbitrary"),
    )(dh, x, g, dres)


def _final_loss(x, g, tgt, *, name):
    s, d = x.shape
    tm = min(512, s)

    def body(x_ref, g_ref, t_ref, dx_ref, dg_ref, loss_ref):
        @pl.when(pl.program_id(0) == 0)
        def _():
            dg_ref[...] = jnp.zeros_like(dg_ref)
            loss_ref[...] = jnp.zeros_like(loss_ref)

        xv = x_ref[...]
        gv = g_ref[...]
        r = lax.rsqrt(jnp.mean(xv * xv, axis=-1, keepdims=True) + EPS)
        xn = xv * r
        err = xn * gv - t_ref[...]
        sq = jnp.sum(jnp.sum(err * err, axis=-1, keepdims=True), axis=0, keepdims=True)
        loss_ref[...] += (0.5 / d) * sq
        dy = err * (1.0 / d)
        dg_ref[...] += jnp.sum(dy * xn, axis=0, keepdims=True)
        u = dy * gv
        dx_ref[...] = r * (u - xn * jnp.mean(u * xn, axis=-1, keepdims=True))

    row = pl.BlockSpec((tm, d), lambda i: (i, 0))
    vec = pl.BlockSpec((1, d), lambda i: (0, 0))
    return pl.pallas_call(
        body, name=name, grid=(s // tm,), in_specs=[row, vec, row],
        out_specs=[row, vec, pl.BlockSpec((1, 128), lambda i: (0, 0))],
        out_shape=[jax.ShapeDtypeStruct((s, d), F32), jax.ShapeDtypeStruct((1, d), F32),
                   jax.ShapeDtypeStruct((1, 128), F32)],
        compiler_params=_cparams("arbitrary"),
    )(x, g, tgt)


def _swa_keys(kp_ref, kc_ref):
    kk = jnp.concatenate([kp_ref[...], kc_ref[...]], axis=0)
    kr = pltpu.roll(kk, 64, 1)
    lo = lax.broadcasted_iota(jnp.int32, kk.shape, 1) < 64
    return [jnp.where(lo, kk, kr).astype(MXU_DTYPE), jnp.where(lo, kr, kk).astype(MXU_DTYPE)]


GROUP = SWA_HEADS // 2


def _swa_mask(n):
    qi = lax.broadcasted_iota(jnp.int32, (BLOCK, 2 * BLOCK), 0)
    ki = lax.broadcasted_iota(jnp.int32, (BLOCK, 2 * BLOCK), 1)
    delta = BLOCK + qi - ki
    valid = (delta >= 0) & (delta < BLOCK) & ((ki >= BLOCK) | (n > 0))
    return valid, delta.astype(F32)


def _stack_heads(ref, j):
    lo = lax.broadcasted_iota(jnp.int32, (BLOCK, BLOCK), 1) < 64
    parts = []
    for r in range(GROUP):
        pair = (GROUP * j + r) // 2
        blk = ref[:, pair * 128:(pair + 1) * 128].astype(F32)
        parts.append(jnp.where(lo if r % 2 == 0 else ~lo, blk, 0.0).astype(MXU_DTYPE))
    return jnp.concatenate(parts, axis=0)


def _unstack_heads(stacked, ref, j):
    lo = lax.broadcasted_iota(jnp.int32, (BLOCK, BLOCK), 1) < 64
    for i in range(GROUP // 2):
        pair = (GROUP * j) // 2 + i
        even, odd = stacked[2 * i * BLOCK:(2 * i + 1) * BLOCK], stacked[(2 * i + 1) * BLOCK:(2 * i + 2) * BLOCK]
        ref[:, pair * 128:(pair + 1) * 128] = jnp.where(lo, even, odd).astype(ref.dtype)


def _head_rows(stacked, r):
    return stacked[r * BLOCK:(r + 1) * BLOCK]


def _swa_scores(raw, h, valid, deltaf):
    return jnp.where(valid, raw * (SWA_SCALE * LOG2E) - (SLOPES[h] * LOG2E) * deltaf, NEG)


def _swa_specs(nb):
    kcol, vcol = KA_OFF // BLOCK, VA_OFF // BLOCK
    last = nb - 1
    cur = lambda n: jnp.minimum(n, last)
    prev = lambda n: jnp.maximum(jnp.minimum(n, last) - 1, 0)
    return [
        pl.BlockSpec(memory_space=pltpu.SMEM),
        pl.BlockSpec((BLOCK, 1024), lambda n: (cur(n), QA_OFF // 1024)),
        pl.BlockSpec((BLOCK, BLOCK), lambda n: (cur(n), kcol)),
        pl.BlockSpec((BLOCK, BLOCK), lambda n: (prev(n), kcol)),
        pl.BlockSpec((BLOCK, BLOCK), lambda n: (cur(n), vcol)),
        pl.BlockSpec((BLOCK, BLOCK), lambda n: (prev(n), vcol)),
    ]


def _swa_fwd(proj, sinks, *, name):
    s = proj.shape[0]
    nb = s // BLOCK

    def body(sink_ref, q_ref, kc_ref, kp_ref, vc_ref, vp_ref, o_ref, lse_ref):
        n = pl.program_id(0)
        keys = _swa_keys(kp_ref, kc_ref)
        vals = _swa_keys(vp_ref, vc_ref)
        valid, deltaf = _swa_mask(n)
        lane = lax.broadcasted_iota(jnp.int32, (BLOCK, BLOCK), 1)
        lse_acc = jnp.zeros((BLOCK, BLOCK), F32)
        for j in range(2):
            raw = _dot(_stack_heads(q_ref, j), keys[j], 1, 1)
            probs = []
            for r in range(GROUP):
                h = GROUP * j + r
                sc = _swa_scores(_head_rows(raw, r), h, valid, deltaf)
                sink = sink_ref[h] * LOG2E
                m = jnp.maximum(jnp.max(sc, axis=-1, keepdims=True), sink)
                p = jnp.exp2(sc - m)
                l = jnp.sum(p, axis=-1, keepdims=True) + jnp.exp2(sink - m)
                probs.append((p * (1.0 / l)).astype(MXU_DTYPE))
                lse_acc = jnp.where(lane == h, m + jnp.log(l) * LOG2E, lse_acc)
            _unstack_heads(jnp.dot(jnp.concatenate(probs, axis=0), vals[j], preferred_element_type=F32), o_ref, j)
        lse_ref[...] = lse_acc

    return pl.pallas_call(
        body, name=name, grid=(nb,), in_specs=_swa_specs(nb),
        out_specs=[pl.BlockSpec((BLOCK, 1024), lambda n: (n, 0)), pl.BlockSpec((BLOCK, BLOCK), lambda n: (n, 0))],
        out_shape=[jax.ShapeDtypeStruct((s, 1024), F32), jax.ShapeDtypeStruct((s, BLOCK), F32)],
        compiler_params=_cparams("parallel"),
    )(sinks, proj, proj, proj, proj, proj)


def _swa_bwd(proj, sinks, lse, do, *, name):
    s = proj.shape[0]
    nb = s // BLOCK
    last = nb - 1

    def body(sink_ref, q_ref, kc_ref, kp_ref, vc_ref, vp_ref, lse_ref, do_ref,
             dq_ref, dk_ref, dv_ref, dsink_ref, carry_k, carry_v):
        n = pl.program_id(0)

        @pl.when(n == 0)
        def _():
            carry_k[...] = jnp.zeros_like(carry_k)
            carry_v[...] = jnp.zeros_like(carry_v)
            dsink_ref[...] = jnp.zeros_like(dsink_ref)

        @pl.when(n < nb)
        def _():
            keys = _swa_keys(kp_ref, kc_ref)
            vals = _swa_keys(vp_ref, vc_ref)
            valid, deltaf = _swa_mask(n)
            lane = lax.broadcasted_iota(jnp.int32, (BLOCK, BLOCK), 1)
            lane1 = lax.broadcasted_iota(jnp.int32, (1, BLOCK), 1)
            lse_blk = lse_ref[...]
            acc_k, acc_v = [], []
            dsink = jnp.zeros((1, BLOCK), F32)
            for j in range(2):
                q_all, do_all = _stack_heads(q_ref, j), _stack_heads(do_ref, j)
                raw = _dot(q_all, keys[j], 1, 1)
                dp_all = _dot(do_all, vals[j], 1, 1)
                probs, dscores = [], []
                for r in range(GROUP):
                    h = GROUP * j + r
                    lse_h = jnp.sum(jnp.where(lane == h, lse_blk, 0.0), axis=-1, keepdims=True)
                    p = jnp.exp2(_swa_scores(_head_rows(raw, r), h, valid, deltaf) - lse_h)
                    dp = _head_rows(dp_all, r)
                    dlt = jnp.sum(dp * p, axis=-1, keepdims=True)
                    dscores.append((p * (dp - dlt) * SWA_SCALE).astype(MXU_DTYPE))
                    probs.append(p.astype(MXU_DTYPE))
                    sunk = jnp.exp2(sink_ref[h] * LOG2E - lse_h) * dlt
                    dsink = jnp.where(lane1 == h, -jnp.sum(sunk, axis=0, keepdims=True), dsink)
                ds_all = jnp.concatenate(dscores, axis=0)
                _unstack_heads(jnp.dot(ds_all, keys[j], preferred_element_type=F32), dq_ref, j)
                acc_k.append(_dot(ds_all, q_all, 0, 0))
                acc_v.append(_dot(jnp.concatenate(probs, axis=0), do_all, 0, 0))
            lo2 = lax.broadcasted_iota(jnp.int32, (2 * BLOCK, BLOCK), 1) < 64
            fold = lambda acc: jnp.where(lo2, acc[0] + pltpu.roll(acc[0], 64, 1), acc[1] + pltpu.roll(acc[1], 64, 1))
            dkk, dvv = fold(acc_k), fold(acc_v)
            dk_ref[...] = (carry_k[...] + dkk[:BLOCK]).astype(dk_ref.dtype)
            dv_ref[...] = (carry_v[...] + dvv[:BLOCK]).astype(dv_ref.dtype)
            carry_k[...] = dkk[BLOCK:]
            carry_v[...] = dvv[BLOCK:]
            dsink_ref[...] += jnp.broadcast_to(dsink, dsink_ref.shape)

        @pl.when(n == nb)
        def _():
            dk_ref[...] = carry_k[...].astype(dk_ref.dtype)
            dv_ref[...] = carry_v[...].astype(dv_ref.dtype)

    cur = lambda n: jnp.minimum(n, last)
    lag = lambda n: jnp.maximum(n - 1, 0)
    return pl.pallas_call(
        body, name=name, grid=(nb + 1,),
        in_specs=_swa_specs(nb) + [pl.BlockSpec((BLOCK, BLOCK), lambda n: (cur(n), 0)),
                                   pl.BlockSpec((BLOCK, 1024), lambda n: (cur(n), 0))],
        out_specs=[pl.BlockSpec((BLOCK, 1024), lambda n: (cur(n), 0)),
                   pl.BlockSpec((BLOCK, BLOCK), lambda n: (lag(n), 0)),
                   pl.BlockSpec((BLOCK, BLOCK), lambda n: (lag(n), 0)),
                   pl.BlockSpec((8, BLOCK), lambda n: (0, 0))],
        out_shape=[jax.ShapeDtypeStruct((s, 1024), MXU_DTYPE), jax.ShapeDtypeStruct((s, BLOCK), MXU_DTYPE),
                   jax.ShapeDtypeStruct((s, BLOCK), MXU_DTYPE), jax.ShapeDtypeStruct((8, BLOCK), F32)],
        scratch_shapes=[pltpu.VMEM((BLOCK, BLOCK), F32), pltpu.VMEM((BLOCK, BLOCK), F32)],
        compiler_params=_cparams("arbitrary"),
    )(sinks, proj, proj, proj, proj, proj, lse, do)


def _rope_partner(v, first, width):
    lane = lax.broadcasted_iota(jnp.int32, v.shape, 1)
    in_a = (lane >= first) & (lane < first + 32)
    in_b = (lane >= first + 32) & (lane < first + 64)
    return jnp.where(in_a, pltpu.roll(v, width - 32, 1), jnp.where(in_b, pltpu.roll(v, 32, 1), 0.0))


def _mla_qkv_fwd(proj, gq, gkv, wq, wkv, tk_c, tk_s, *, layer, name):
    s = proj.shape[0]
    tm = min(256, s)

    def body(cq_ref, ckv_ref, kr_ref, gq_ref, gkv_ref, wq_ref, wkv_ref, kc_ref, ks_ref,
             qcat_ref, kcat_ref, v_ref, cqn_ref, ckvn_ref):
        cq = cq_ref[...]
        cqn = (cq * lax.rsqrt(jnp.mean(cq * cq, axis=-1, keepdims=True) + EPS) * gq_ref[...]).astype(MXU_DTYPE)
        cqn_ref[...] = cqn
        qpre = _dot(cqn, wq_ref[...], 1, 1)
        kc, ks = kc_ref[...], ks_ref[...]
        for hh in range(MLA_HEADS):
            qcat_ref[:, hh * 256:hh * 256 + 128] = qpre[:, hh * 256:hh * 256 + 128].astype(MXU_DTYPE)
            blk = qpre[:, hh * 256 + 128:(hh + 1) * 256]
            qcat_ref[:, hh * 256 + 128:(hh + 1) * 256] = (blk * kc + _rope_partner(blk, 0, 128) * ks).astype(MXU_DTYPE)
        ckv = ckv_ref[...]
        ckvn = (ckv * lax.rsqrt(jnp.mean(ckv * ckv, axis=-1, keepdims=True) + EPS) * gkv_ref[...]).astype(MXU_DTYPE)
        ckvn_ref[...] = ckvn
        kv = _dot(ckvn, wkv_ref[...], 1, 1)
        kr = kr_ref[...]
        krr = (kr * kc + _rope_partner(kr, 0, 128) * ks).astype(MXU_DTYPE)
        for hh in range(MLA_HEADS):
            kcat_ref[:, hh * 256:hh * 256 + 128] = kv[:, hh * 128:(hh + 1) * 128].astype(MXU_DTYPE)
            kcat_ref[:, hh * 256 + 128:(hh + 1) * 256] = krr
            v_ref[:, hh * 256:hh * 256 + 128] = kv[:, 1024 + hh * 128:1024 + (hh + 1) * 128].astype(MXU_DTYPE)
            v_ref[:, hh * 256 + 128:(hh + 1) * 256] = jnp.ones((tm, 128), MXU_DTYPE)

    row = lambda w, c: pl.BlockSpec((tm, w), lambda i: (i, c))
    full = lambda a: pl.BlockSpec(a.shape, lambda i: (0, 0))
    of_layer = lambda a: _layer_spec(a.shape[-2:], lambda i: (0, 0), layer)
    return pl.pallas_call(
        body, name=name, grid=(s // tm,),
        in_specs=[row(Q_RANK, CQ_OFF // Q_RANK), row(KV_RANK, CKV_OFF // KV_RANK), row(128, KR_OFF // 128),
                  full(gq), full(gkv), of_layer(wq), of_layer(wkv), row(128, 0), row(128, 0)],
        out_specs=[row(2048, 0), row(2048, 0), row(2048, 0), row(Q_RANK, 0), row(KV_RANK, 0)],
        out_shape=[jax.ShapeDtypeStruct((s, 2048), MXU_DTYPE), jax.ShapeDtypeStruct((s, 2048), MXU_DTYPE),
                   jax.ShapeDtypeStruct((s, 2048), MXU_DTYPE), jax.ShapeDtypeStruct((s, Q_RANK), MXU_DTYPE),
                   jax.ShapeDtypeStruct((s, KV_RANK), MXU_DTYPE)],
        compiler_params=_cparams("parallel"),
    )(proj, proj, proj, gq, gkv, wq, wkv, tk_c, tk_s)


def _norm_bwd(x, g, dy):
    r = lax.rsqrt(jnp.mean(x * x, axis=-1, keepdims=True) + EPS)
    xn = x * r
    u = dy * g
    return r * (u - xn * jnp.mean(u * xn, axis=-1, keepdims=True)), jnp.sum(dy * xn, axis=0, keepdims=True)


def _mla_qkv_bwd(proj, cqn, ckvn, dqcat, dkcat, dv, dka, dva, gq, gkv, wq, wkv, tk_c, tk_s, *, layer, name):
    s = proj.shape[0]
    tm = min(256, s)
    t_cq, t_ka, t_ckv, t_va, t_kr = (o - CQ_OFF for o in (CQ_OFF, KA_OFF, CKV_OFF, VA_OFF, KR_OFF))

    def body(cq_ref, ckv_ref, cqn_ref, ckvn_ref, dq_ref, dk_ref, dv_ref, dka_ref, dva_ref, gq_ref, gkv_ref, wq_ref,
             wkv_ref, kc_ref, ks_ref,
             tile_ref, dwq_ref, dwkv_ref, dgq_ref, dgkv_ref, dqpre, dkv, dwq_acc, dwkv_acc):
        dcq_ref = tile_ref.at[:, t_cq:t_cq + Q_RANK]
        dckv_ref = tile_ref.at[:, t_ckv:t_ckv + KV_RANK]
        dkr_ref = tile_ref.at[:, t_kr:t_kr + 128]
        tile_ref[:, t_ka:t_ka + 128] = dka_ref[...]
        tile_ref[:, t_va:t_va + 128] = dva_ref[...]

        @pl.when(pl.program_id(0) == 0)
        def _():
            for r in (dwq_acc, dwkv_acc, dgq_ref, dgkv_ref):
                r[...] = jnp.zeros_like(r)

        kc, ks = kc_ref[...], ks_ref[...]
        dkrr = jnp.zeros((tm, 128), F32)
        for hh in range(MLA_HEADS):
            dqpre[:, hh * 256:hh * 256 + 128] = dq_ref[:, hh * 256:hh * 256 + 128].astype(MXU_DTYPE)
            blk = dq_ref[:, hh * 256 + 128:(hh + 1) * 256]
            dqpre[:, hh * 256 + 128:(hh + 1) * 256] = (blk * kc + _rope_partner(blk * ks, 0, 128)).astype(MXU_DTYPE)
            dkv[:, hh * 128:(hh + 1) * 128] = dk_ref[:, hh * 256:hh * 256 + 128].astype(MXU_DTYPE)
            dkrr = dkrr + dk_ref[:, hh * 256 + 128:(hh + 1) * 256]
        dkv[:, 1024:] = dv_ref[...].astype(MXU_DTYPE)
        dkr_ref[...] = (dkrr * kc + _rope_partner(dkrr * ks, 0, 128)).astype(dkr_ref.dtype)

        dq_b = dqpre[...]
        dwq_acc[...] += _dot(cqn_ref[...], dq_b, 0, 0)
        dcq, dgq = _norm_bwd(cq_ref[...], gq_ref[...], _dot(dq_b, wq_ref[...], 1, 0))
        dcq_ref[...] = dcq.astype(dcq_ref.dtype)
        dgq_ref[...] += dgq

        dkv_b = dkv[...]
        dwkv_acc[...] += _dot(ckvn_ref[...], dkv_b, 0, 0)
        dckv, dgkv = _norm_bwd(ckv_ref[...], gkv_ref[...], _dot(dkv_b, wkv_ref[...], 1, 0))
        dckv_ref[...] = dckv.astype(dckv_ref.dtype)
        dgkv_ref[...] += dgkv

        @pl.when(pl.program_id(0) == s // tm - 1)
        def _():
            dwq_ref[...] = dwq_acc[...].T
            dwkv_ref[...] = dwkv_acc[...].T

    row = lambda w, c: pl.BlockSpec((tm, w), lambda i: (i, c))
    full = lambda shape: pl.BlockSpec(shape, lambda i: (0, 0))
    of_layer = lambda a: _layer_spec(a.shape[-2:], lambda i: (0, 0), layer)
    return pl.pallas_call(
        body, name=name, grid=(s // tm,),
        in_specs=[row(Q_RANK, CQ_OFF // Q_RANK), row(KV_RANK, CKV_OFF // KV_RANK), row(Q_RANK, 0), row(KV_RANK, 0),
                  row(2048, 0), row(2048, 0), row(1024, 0), row(128, 0), row(128, 0), full(gq.shape), full(gkv.shape),
                  of_layer(wq), of_layer(wkv), row(128, 0), row(128, 0)],
        out_specs=[row(1024, 0), full(wq.shape[-2:]), full(wkv.shape[-2:]), full(gq.shape), full(gkv.shape)],
        out_shape=[jax.ShapeDtypeStruct((s, 1024), MXU_DTYPE), jax.ShapeDtypeStruct(wq.shape[-2:], F32),
                   jax.ShapeDtypeStruct(wkv.shape[-2:], F32), jax.ShapeDtypeStruct(gq.shape, F32),
                   jax.ShapeDtypeStruct(gkv.shape, F32)],
        scratch_shapes=[pltpu.VMEM((tm, 2048), MXU_DTYPE), pltpu.VMEM((tm, 2048), MXU_DTYPE),
                        pltpu.VMEM((Q_RANK, 2048), F32), pltpu.VMEM((KV_RANK, 2048), F32)],
        compiler_params=_cparams("arbitrary"),
    )(proj, proj, cqn, ckvn, dqcat, dkcat, dv, dka, dva, gq, gkv, wq, wkv, tk_c, tk_s)


def _loop_by_two(lo, hi, step):
    def pair(i, carry):
        step(lo + 2 * i)
        step(lo + 2 * i + 1)
        return carry

    lax.fori_loop(0, (hi - lo) // 2, pair, 0)

    @pl.when((hi - lo) % 2 == 1)
    def _():
        step(hi - 1)


def _causal_mask(t):
    return lax.broadcasted_iota(jnp.int32, (t, t), 1) <= lax.broadcasted_iota(jnp.int32, (t, t), 0)


def _mla_fwd(qcat, kcat, v, *, name, gather=None):
    s = qcat.shape[0]
    t = min(512, s)
    nq = s // t
    hp = HEADS_PER_STEP
    ng = MLA_HEADS // hp
    c2 = MLA_SCALE * LOG2E

    nw = 0 if gather is None else len(gather)

    def body(q_ref, k_ref, v_ref, *rest):
        src, (o_ref, lse_ref), out = rest[:nw], rest[nw:nw + 2], rest[nw + 2:2 * nw + 2]
        top_s, acc_s, *sems = rest[2 * nw + 2:]
        g, qi = pl.program_id(0), pl.program_id(1)
        if nw:
            @pl.when((g == 0) & (qi == 0))
            def _():
                _gather_start(src, out, *sems)

            @pl.when((g == ng - 1) & (qi == 0))
            def _():
                _gather_forward(src, out, *sems)

        rows = lambda j: pl.ds(pl.multiple_of(j * t, t), t)
        head = lambda e: slice(e * 256, (e + 1) * 256)
        raw = lambda e, j: _dot(q_ref[:, head(e)], k_ref[rows(j), head(e)], 1, 1)

        for e in range(hp):
            top_s[e] = jnp.where(_causal_mask(t), raw(e, qi), NEG)

        def pass1(j):
            for e in range(hp):
                top_s[e] = jnp.maximum(top_s[e], raw(e, j))

        _loop_by_two(0, qi, pass1)
        m = [jnp.max(top_s[e], axis=-1, keepdims=True) * c2 for e in range(hp)]

        def weighted(e, j, masked):
            sc = raw(e, j) * c2 - m[e]
            if masked:
                sc = jnp.where(_causal_mask(t), sc, NEG)
            return jnp.dot(jnp.exp2(sc).astype(MXU_DTYPE), v_ref[rows(j), head(e)], preferred_element_type=F32)

        for e in range(hp):
            acc_s[e] = weighted(e, qi, True)

        def pass2(j):
            for e in range(hp):
                acc_s[e] += weighted(e, j, False)

        _loop_by_two(0, qi, pass2)
        lane = lax.broadcasted_iota(jnp.int32, (t, 128), 1)
        stats = jnp.zeros((t, 128), F32)
        for e in range(hp):
            l = acc_s[e, :, 128:]
            o_ref[:, e * 128:(e + 1) * 128] = acc_s[e, :, :128] / l
            stats = jnp.where(lane == e, m[e] + jnp.log(l) * LOG2E, stats)
        lse_ref[...] = stats
        if nw:
            @pl.when((g == ng - 1) & (qi == nq - 1))
            def _():
                _gather_finish(src, out, *sems)

    outs = pl.pallas_call(
        body, name=name, grid=(ng, nq),
        in_specs=[pl.BlockSpec((t, 256 * hp), lambda g, qi: (qi, g)), pl.BlockSpec((s, 256 * hp), lambda g, qi: (0, g)),
                  pl.BlockSpec((s, 256 * hp), lambda g, qi: (0, g))] + [ANY] * nw,
        out_specs=[pl.BlockSpec((t, 128 * hp), lambda g, qi: (qi, g)), pl.BlockSpec((t, 128), lambda g, qi: (qi, g))]
        + [ANY] * nw,
        out_shape=[jax.ShapeDtypeStruct((s, 1024), F32), jax.ShapeDtypeStruct((s, 128 * ng), F32)]
        + (_gathered_shapes(gather) if nw else []),
        scratch_shapes=[pltpu.VMEM((hp, t, t), F32), pltpu.VMEM((hp, t, 256), F32)] + (_sems(6, nw) if nw else []),
        compiler_params=_cparams("arbitrary", "arbitrary"),
    )(qcat, kcat, v, *(gather or []))
    return outs[0], outs[1], list(outs[2:])


def _mla_bwd(qcat, kcat, v, do, lse, delta, *, name, scatter=None):
    s = qcat.shape[0]
    t = min(512, s)
    nq = s // t
    hp = HEADS_PER_STEP
    c2 = MLA_SCALE * LOG2E
    nw = 0 if scatter is None else len(scatter)

    def body(q_ref, k_ref, v_ref, do_ref, lse_ref, dl_ref, *rest):
        src, (dq_ref, dk_ref, dv_ref), out = rest[:nw], rest[nw:nw + 3], rest[nw + 3:2 * nw + 3]
        dk_acc, dv_acc, *sems = rest[2 * nw + 3:]
        h, ki = pl.program_id(0), pl.program_id(1)
        if nw:
            @pl.when((h == 0) & (ki == 0))
            def _():
                _scatter_start(src, out, *sems)

        @pl.when(ki == 0)
        def _():
            dq_ref[...] = jnp.zeros_like(dq_ref)

        dk_acc[...] = jnp.zeros_like(dk_acc)
        dv_acc[...] = jnp.zeros_like(dv_acc)
        k, vv = k_ref[...], v_ref[...]
        mine = lax.broadcasted_iota(jnp.int32, (t, 128), 1) == h % hp

        def chunk(qi, masked):
            rows = pl.ds(pl.multiple_of(qi * t, t), t)
            q, dob = q_ref[rows, :], do_ref[rows, :]
            pick = lambda r: jnp.sum(jnp.where(mine, r[rows, :], 0.0), axis=-1, keepdims=True)
            sc = _dot(q, k, 1, 1) * c2
            if masked:
                sc = jnp.where(_causal_mask(t), sc, NEG)
            p = jnp.exp2(sc - pick(lse_ref))
            dp = _dot(dob, vv, 1, 1)
            ds = (p * (dp - pick(dl_ref)) * MLA_SCALE).astype(MXU_DTYPE)
            dv_acc[...] += _dot(dob, p.astype(MXU_DTYPE), 0, 0)
            dk_acc[...] += _dot(q, ds, 0, 0)
            dq_ref[rows, :] += jnp.dot(ds, k, preferred_element_type=F32)

        chunk(ki, True)
        _loop_by_two(ki + 1, nq, lambda qi: chunk(qi, False))
        dk_ref[...] = dk_acc[...].T
        dv_ref[...] = dv_acc[...].T
        if nw:
            @pl.when((h == MLA_HEADS - 1) & (ki == nq - 1))
            def _():
                _scatter_finish(src, out, *sems)

    head = lambda w: pl.BlockSpec((s, w), lambda h, ki: (0, h))
    blk = lambda w: pl.BlockSpec((t, w), lambda h, ki: (ki, h))
    stat = pl.BlockSpec((s, 128), lambda h, ki: (0, h // hp))
    outs = pl.pallas_call(
        body, name=name, grid=(MLA_HEADS, nq),
        in_specs=[head(256), blk(256), pl.BlockSpec((t, 128), lambda h, ki: (ki, 2 * h)), head(128), stat, stat]
        + [ANY] * nw,
        out_specs=[head(256), blk(256), blk(128)] + [ANY] * nw,
        out_shape=[jax.ShapeDtypeStruct((s, 2048), F32), jax.ShapeDtypeStruct((s, 2048), F32),
                   jax.ShapeDtypeStruct((s, 1024), F32)] + [jax.ShapeDtypeStruct(a.shape, a.dtype) for a in scatter or []],
        scratch_shapes=[pltpu.VMEM((256, t), F32), pltpu.VMEM((128, t), F32)] + (_sems(3, nw) if nw else []),
        compiler_params=_cparams("arbitrary", "arbitrary"),
    )(qcat, kcat, v, do, lse, delta, *(scatter or []))
    return outs[0], outs[1], outs[2], list(outs[3:])


def _gate_specs(tm):
    half = lambda c: pl.BlockSpec((tm, 1024), lambda i: (i, c))
    return half(0), half(GA_OFF // 1024), half(GB_OFF // 1024)


def _gate_fwd(oa, ob, proj, *, name):
    s = oa.shape[0]
    tm = min(512, s)

    def body(oa_ref, ob_ref, ga_ref, gb_ref, y_ref, yt_ref):
        ga, gb = ga_ref[...], gb_ref[...]
        ya = oa_ref[...] * (ga * jax.nn.sigmoid(ga))
        yb = ob_ref[...] * (gb * jax.nn.sigmoid(gb))
        y_ref[:, :1024] = ya.astype(MXU_DTYPE)
        y_ref[:, 1024:] = yb.astype(MXU_DTYPE)
        yt_ref[:1024, :] = ya.T.astype(MXU_DTYPE)
        yt_ref[1024:, :] = yb.T.astype(MXU_DTYPE)

    o_spec, ga_spec, gb_spec = _gate_specs(tm)
    return pl.pallas_call(
        body, name=name, grid=(s // tm,), in_specs=[o_spec, o_spec, ga_spec, gb_spec],
        out_specs=[pl.BlockSpec((tm, 2048), lambda i: (i, 0)), pl.BlockSpec((2048, tm), lambda i: (0, i))],
        out_shape=[jax.ShapeDtypeStruct((s, 2048), MXU_DTYPE), jax.ShapeDtypeStruct((2048, s), MXU_DTYPE)],
        compiler_params=_cparams("parallel"),
    )(oa, ob, proj, proj)


def _gate_bwd(dy, oa, ob, proj, *, name):
    s = oa.shape[0]
    tm = min(512, s)

    def body(dy_ref, oa_ref, ob_ref, ga_ref, gb_ref, doa_ref, dob_ref, dga_ref, dgb_ref, dl_ref):
        def branch(dyv, o, g, do_ref, dg_ref):
            sg = jax.nn.sigmoid(g)
            do = dyv * (g * sg)
            do_ref[...] = do.astype(MXU_DTYPE)
            dg_ref[...] = (dyv * o * (sg * (1.0 + g * (1.0 - sg)))).astype(MXU_DTYPE)
            return do

        branch(dy_ref[:, :1024], oa_ref[...], ga_ref[...], doa_ref, dga_ref)
        ob = ob_ref[...]
        prod = branch(dy_ref[:, 1024:], ob, gb_ref[...], dob_ref, dgb_ref) * ob
        lane = lax.broadcasted_iota(jnp.int32, (tm, stat_w), 1)
        acc = jnp.zeros((tm, stat_w), F32)
        for hh in range(MLA_HEADS):
            at = (hh // HEADS_PER_STEP) * 128 + hh % HEADS_PER_STEP
            acc = jnp.where(lane == at, jnp.sum(prod[:, hh * 128:(hh + 1) * 128], axis=-1, keepdims=True), acc)
        dl_ref[...] = acc

    stat_w = 128 * (MLA_HEADS // HEADS_PER_STEP)
    o_spec, ga_spec, gb_spec = _gate_specs(tm)
    return pl.pallas_call(
        body, name=name, grid=(s // tm,),
        in_specs=[pl.BlockSpec((tm, 2048), lambda i: (i, 0)), o_spec, o_spec, ga_spec, gb_spec],
        out_specs=[o_spec, o_spec, o_spec, o_spec, pl.BlockSpec((tm, stat_w), lambda i: (i, 0))],
        out_shape=[jax.ShapeDtypeStruct((s, 1024), MXU_DTYPE)] * 4 + [jax.ShapeDtypeStruct((s, stat_w), F32)],
        compiler_params=_cparams("parallel"),
    )(dy, oa, ob, proj, proj)


def _row_block(rows, cols, itemsize=4, budget=2 << 20):
    fits = [tr for tr in range(16, rows + 1, 16) if rows % tr == 0 and tr * cols * itemsize <= budget]
    return fits[-1] if fits else rows


def _adamw(w, g, m, v, *, name):
    shape = w.shape
    rows, cols = shape[-2:]
    w3, g3, m3, v3 = (a.reshape((-1, rows, cols)) for a in (w, g, m, v))
    lead = w3.shape[0]
    tr = _row_block(rows, cols)

    def body(w_ref, g_ref, m_ref, v_ref, d_ref, mo_ref, vo_ref):
        gv = g_ref[...]
        mn = ADAM_B1 * m_ref[...] + (1.0 - ADAM_B1) * gv
        vn = ADAM_B2 * v_ref[...] + (1.0 - ADAM_B2) * jnp.square(gv)
        m_hat = mn / (1.0 - ADAM_B1 ** ADAM_STEP)
        v_hat = vn / (1.0 - ADAM_B2 ** ADAM_STEP)
        d_ref[...] = -ADAM_LR * (m_hat / (jnp.sqrt(v_hat) + ADAM_EPS) + ADAM_WD * w_ref[...])
        mo_ref[...] = mn
        vo_ref[...] = vn

    spec = pl.BlockSpec((None, tr, cols), lambda a, i: (a, i, 0))
    outs = pl.pallas_call(
        body, name=name, grid=(lead, rows // tr), in_specs=[spec] * 4, out_specs=[spec] * 3,
        out_shape=[jax.ShapeDtypeStruct((lead, rows, cols), F32)] * 3,
        compiler_params=_cparams("parallel", "parallel"),
    )(w3, g3, m3, v3)
    return tuple(o.reshape(shape) for o in outs)


def _pair_sum(where, grads, recv, *, name):
    layers, chips, _, rows, cols = grads.shape
    tr = _row_block(rows, cols)

    def body(where_ref, a_ref, b_ref, o_ref):
        o_ref[...] = (a_ref[...] + b_ref[...].astype(F32)).astype(WIRE_DTYPE)

    spec = pl.BlockSpec((None, None, tr, cols), lambda a, k, i, w: (a, k, i, 0))
    return pl.pallas_call(
        body, name=name,
        grid_spec=pltpu.PrefetchScalarGridSpec(
            num_scalar_prefetch=1, grid=(layers, chips, rows // tr),
            in_specs=[pl.BlockSpec((None, None, None, tr, cols), lambda a, k, i, w: (a, k, w[4], i, 0)), spec],
            out_specs=spec),
        out_shape=jax.ShapeDtypeStruct((layers, chips, rows, cols), WIRE_DTYPE),
        compiler_params=_cparams("parallel", "parallel", "parallel"),
    )(where, grads, recv)


def _chip_sum(where, grads, recv, parts, *, name):
    layers, _, _, rows, cols = grads.shape
    tr = _row_block(rows, cols)

    def body(where_ref, a_ref, b_ref, t0_ref, t1_ref, t2_ref, o_ref):
        total = a_ref[...] + b_ref[...].astype(F32)
        for t_ref in (t0_ref, t1_ref, t2_ref):
            total = total + t_ref[...].astype(F32)
        o_ref[...] = total

    slot = lambda j: pl.BlockSpec((None, None, tr, cols), lambda a, i, w: (a, w[j], i, 0))
    return pl.pallas_call(
        body, name=name,
        grid_spec=pltpu.PrefetchScalarGridSpec(
            num_scalar_prefetch=1, grid=(layers, rows // tr),
            in_specs=[pl.BlockSpec((None, None, None, tr, cols), lambda a, i, w: (a, w[0], w[4], i, 0)), slot(0), slot(1),
                      slot(2), slot(3)],
            out_specs=slot(4)),
        out_shape=jax.ShapeDtypeStruct((layers, 2, rows, cols), F32),
        compiler_params=_cparams("parallel", "parallel"),
    )(where, grads, recv, parts, parts, parts)


def _place():
    x, y, c = lax.axis_index("x"), lax.axis_index("y"), lax.axis_index("c")
    chips = [(1 - x, y), (x, 1 - y), (1 - x, 1 - y)]
    return x, y, c, chips


def _sems(*shape):
    return [pltpu.SemaphoreType.DMA(shape), pltpu.SemaphoreType.DMA(shape)]


OWNER_CORE = (0, 1, 1, 1)


def _gather_copy(src, out, send_sems, recv_sems, t, sem, slot, to, forward=False):
    n = src[t].shape[0]
    rows = out[t].at[pl.ds(pl.multiple_of(slot * n, 16), n)]
    return pltpu.make_async_remote_copy(src_ref=rows if forward else src[t], dst_ref=rows, send_sem=send_sems.at[sem, t],
                                        recv_sem=recv_sems.at[sem, t], device_id=to, device_id_type=MESH)


def _gather_start(src, out, send_sems, recv_sems):
    x, y, c, chips = _place()
    for t, owner in enumerate(OWNER_CORE):
        @pl.when(c == owner)
        def _():
            for j, chip in enumerate(chips):
                _gather_copy(src, out, send_sems, recv_sems, t, j, 2 * x + y, (*chip, c)).start()


def _gather_forward(src, out, send_sems, recv_sems):
    x, y, c, chips = _place()
    for t, owner in enumerate(OWNER_CORE):
        @pl.when(c == owner)
        def _():
            for j, (px, py) in enumerate(chips):
                _gather_copy(src, out, send_sems, recv_sems, t, j, 2 * px + py, (px, py, c)).wait_recv()
                _gather_copy(src, out, send_sems, recv_sems, t, 3 + j, 2 * px + py, (x, y, 1 - c), forward=True).start()


def _gather_finish(src, out, send_sems, recv_sems):
    x, y, c, chips = _place()
    slots = [2 * px + py for px, py in chips]
    for t, owner in enumerate(OWNER_CORE):
        copy = functools.partial(_gather_copy, src, out, send_sems, recv_sems, t)

        @pl.when(c == owner)
        def _():
            for j, (px, py) in enumerate(chips):
                copy(j, 2 * x + y, (px, py, c)).wait_send()
                copy(3 + j, slots[j], (x, y, 1 - c), forward=True).wait_send()

        @pl.when(c != owner)
        def _():
            for j in range(3):
                copy(3 + j, slots[j], (x, y, 1 - c), forward=True).wait_recv()


def _gathered_shapes(shards):
    return [jax.ShapeDtypeStruct((4 * a.shape[0], a.shape[1]), a.dtype) for a in shards]


def _comm_gather_layer(shards, *, name):
    nt = len(shards)

    def body(*refs):
        src, out, sems = refs[:nt], refs[nt:2 * nt], refs[2 * nt:]
        _gather_start(src, out, *sems)
        _gather_forward(src, out, *sems)
        _gather_finish(src, out, *sems)

    return pl.pallas_call(
        body, name=name, in_specs=[ANY] * nt, out_specs=[ANY] * nt, out_shape=_gathered_shapes(shards),
        scratch_shapes=_sems(6, nt),
    )(*shards)


def _comm_swap_sibling(bufs, *, name):
    nt = len(bufs)

    def body(*refs):
        src, out, (send_sems, recv_sems) = refs[:nt], refs[nt:2 * nt], refs[2 * nt:]
        x, y, c, _ = _place()
        cps = [pltpu.make_async_remote_copy(src_ref=src[t], dst_ref=out[t], send_sem=send_sems.at[t], recv_sem=recv_sems.at[t],
                                            device_id=(x, y, 1 - c), device_id_type=MESH) for t in range(nt)]
        for cp in cps:
            cp.start()
        for cp in cps:
            cp.wait()

    return pl.pallas_call(
        body, name=name, in_specs=[ANY] * nt, out_specs=[ANY] * nt,
        out_shape=[jax.ShapeDtypeStruct(a.shape, a.dtype) for a in bufs], scratch_shapes=_sems(nt),
    )(*bufs)


def _scatter_copy(src, out, send_sems, recv_sems, j, t, from_slot, to_slot, to):
    return pltpu.make_async_remote_copy(src_ref=src[t].at[:, from_slot], dst_ref=out[t].at[:, to_slot],
                                        send_sem=send_sems.at[j, t], recv_sem=recv_sems.at[j, t], device_id=to,
                                        device_id_type=MESH)


def _scatter_start(src, out, send_sems, recv_sems):
    x, y, c, chips = _place()
    for j, (px, py) in enumerate(chips):
        for t in range(len(src)):
            _scatter_copy(src, out, send_sems, recv_sems, j, t, 2 * px + py, 2 * x + y, (px, py, c)).start()


def _scatter_finish(src, out, send_sems, recv_sems):
    x, y, c, chips = _place()
    for j, (px, py) in enumerate(chips):
        for t in range(len(src)):
            _scatter_copy(src, out, send_sems, recv_sems, j, t, 2 * x + y, 2 * px + py, (px, py, c)).wait_recv()
    for j, (px, py) in enumerate(chips):
        for t in range(len(src)):
            _scatter_copy(src, out, send_sems, recv_sems, j, t, 2 * px + py, 2 * x + y, (px, py, c)).wait_send()


def _comm_scatter_chips(parts, *, name):
    nt = len(parts)

    def body(*refs):
        src, out, sems = refs[:nt], refs[nt:2 * nt], refs[2 * nt:]
        _scatter_start(src, out, *sems)
        _scatter_finish(src, out, *sems)

    return pl.pallas_call(
        body, name=name, in_specs=[ANY] * nt, out_specs=[ANY] * nt,
        out_shape=[jax.ShapeDtypeStruct(a.shape, a.dtype) for a in parts], scratch_shapes=_sems(3, nt),
    )(*parts)


def _comm_join_halves(bufs, *, name):
    nt = len(bufs)

    def body(*refs):
        src, out, (send_sems, recv_sems) = refs[:nt], refs[nt:2 * nt], refs[2 * nt:]
        x, y, c, _ = _place()
        copy = lambda t, half: pltpu.make_async_remote_copy(
            src_ref=src[t].at[:, c], dst_ref=out[t].at[:, half], send_sem=send_sems.at[t], recv_sem=recv_sems.at[t],
            device_id=(x, y, 1 - c), device_id_type=MESH)
        sends = [copy(t, c) for t in range(nt)]
        for cp in sends:
            cp.start()
        for t in range(nt):
            copy(t, 1 - c).wait_recv()
        for cp in sends:
            cp.wait_send()

    return pl.pallas_call(
        body, name=name, in_specs=[ANY] * nt, out_specs=[ANY] * nt,
        out_shape=[jax.ShapeDtypeStruct(a.shape, a.dtype) for a in bufs],
        input_output_aliases={t: t for t in range(nt)}, scratch_shapes=_sems(nt),
    )(*bufs)


def _comm_allreduce_small(part, *, name):
    rows, cols = part.shape

    def body(p_ref, o_ref, buf, send_sems, recv_sems):
        x, y, c, _ = _place()
        me = 4 * x + 2 * y + c
        buf[me] = p_ref[...]
        flip = lambda v, bit: 1 - v if bit else v
        peers = [(flip(x, d & 4), flip(y, d & 2), flip(c, d & 1)) for d in range(1, 8)]
        sends = []
        for j, peer in enumerate(peers):
            cp = pltpu.make_async_remote_copy(src_ref=buf.at[me], dst_ref=buf.at[me], send_sem=send_sems.at[j],
                                              recv_sem=recv_sems.at[j], device_id=peer, device_id_type=MESH)
            cp.start()
            sends.append(cp)
        for j, (px, py, pc) in enumerate(peers):
            pltpu.make_async_remote_copy(src_ref=buf.at[me], dst_ref=buf.at[4 * px + 2 * py + pc], send_sem=send_sems.at[j],
                                         recv_sem=recv_sems.at[j], device_id=(px, py, pc), device_id_type=MESH).wait_recv()
        for cp in sends:
            cp.wait_send()
        total = buf[0]
        for i in range(1, 8):
            total = total + buf[i]
        o_ref[...] = total

    vm = pl.BlockSpec(memory_space=pltpu.VMEM)
    return pl.pallas_call(
        body, name=name, in_specs=[vm], out_specs=vm, out_shape=jax.ShapeDtypeStruct((rows, cols), F32),
        scratch_shapes=[pltpu.VMEM((8, rows, cols), F32), pltpu.SemaphoreType.DMA((7,)), pltpu.SemaphoreType.DMA((7,))],
    )(part)


def _pad_in_rows(wt):
    r = lambda o, n: wt[..., o:o + n, :]
    kr = r(2944, 64)
    return jnp.concatenate([r(0, 1024), r(1280, 1024), r(3008, 1024), r(2304, Q_RANK), r(1024, 128), r(2688, KV_RANK),
                            r(1152, 128), kr, jnp.zeros_like(kr)], axis=-2)


def _unpad_in_rows(qa, ga, gb, mixed):
    cq, ka, ckv, va, kr = (mixed[..., o - CQ_OFF:o - CQ_OFF + n, :] for o, n in (
        (CQ_OFF, Q_RANK), (KA_OFF, 128), (CKV_OFF, KV_RANK), (VA_OFF, 128), (KR_OFF, 64)))
    return jnp.concatenate([qa, ka, va, ga, cq, ckv, kr, gb], axis=-2)


def _pad_q_rows(wt):
    lead, cols = wt.shape[:-2], wt.shape[-1]
    wt = jnp.pad(wt.reshape(lead + (MLA_HEADS, 192, cols)), [(0, 0)] * (len(lead) + 1) + [(0, 64), (0, 0)])
    return wt.reshape(lead + (MLA_HEADS * 256, cols))


def _unpad_q_rows(wt):
    lead, cols = wt.shape[:-2], wt.shape[-1]
    return wt.reshape(lead + (MLA_HEADS, 256, cols))[..., :192, :].reshape(lead + (MLA_HEADS * 192, cols))


def _perm_kv_rows(wt):
    lead, cols = wt.shape[:-2], wt.shape[-1]
    return jnp.swapaxes(wt.reshape(lead + (MLA_HEADS, 2, 128, cols)), -4, -3).reshape(lead + (2048, cols))


def _unperm_kv_rows(wt):
    lead, cols = wt.shape[:-2], wt.shape[-1]
    return jnp.swapaxes(wt.reshape(lead + (2, MLA_HEADS, 128, cols)), -4, -3).reshape(lead + (2048, cols))


def _t(a):
    return jnp.swapaxes(a, -1, -2)


def _rope_tables(s):
    pos = jnp.arange(s, dtype=F32)
    inv_freq = 10000.0 ** (-jnp.arange(0, 64, 2, dtype=F32) / 64)
    ang = pos[:, None] * inv_freq[None, :]
    cos, sin = jnp.cos(ang), jnp.sin(ang)
    z64 = jnp.zeros((s, 64), F32)
    tk_c = jnp.concatenate([cos, cos, z64], axis=-1)
    tk_s = jnp.concatenate([-sin, sin, z64], axis=-1)
    return tk_c, tk_s


def _layer_weights(gathered, own, chip):
    def with_own_rows(g, o):
        n = o.shape[0]
        return jnp.concatenate([lax.select(chip == j, o, g[j * n:(j + 1) * n]) for j in range(4)], axis=0)

    full_in, full_q, full_kv, full_o = (with_own_rows(g, o) for g, o in zip(gathered, own))
    return _pad_in_rows(full_in), _pad_q_rows(full_q), _perm_kv_rows(full_kv), full_o


def _device_step(xs, tgt, attn_g, sinks, gq, gkv, final_g, shards, where):
    chip = where[0]
    depth = shards[0].shape[0]
    s = xs.shape[0]
    tabs = _rope_tables(s)
    saved = []
    x = xs
    of_layer = lambda l: [a[l] for a in shards]
    gathered = _comm_gather_layer(of_layer(0), name="comm_gather_layer0")
    weights = []
    for l in range(depth):
        w_in_p, w_q_p, w_kv_p, w_o = _layer_weights(gathered, of_layer(l), chip)
        weights.append((w_in_p, w_q_p, w_kv_p, w_o))
        h, h_t = _rmsnorm_fwd(x, attn_g[l:l + 1], name=f"norm_fwd{l}")
        proj = _matmul(h, w_in_p, tb=True, name=f"in_proj{l}")
        oa, lse_a = _swa_fwd(proj, sinks[l], name=f"swa_fwd{l}")
        qcat, kcat, v, cqn, ckvn = _mla_qkv_fwd(proj, gq[l:l + 1], gkv[l:l + 1], w_q_p, w_kv_p, *tabs, layer=None,
                                                name=f"mla_qkv_fwd{l}")
        ob, lse_b, gathered = _mla_fwd(qcat, kcat, v, gather=of_layer(l + 1) if l + 1 < depth else None,
                                       name=f"mla_fwd{l}")
        y, y_t = _gate_fwd(oa, ob, proj, name=f"gate_fwd{l}")
        x_next = _matmul(y, w_o, add=x, name=f"out_proj{l}")
        saved.append((x, h_t, proj, oa, lse_a, qcat, kcat, v, cqn, ckvn, ob, lse_b, y_t))
        x = x_next

    dx, d_final_g, loss = _final_loss(x, final_g, tgt, name="final_loss")

    d_attn_g, d_sinks, d_gq, d_gkv = [None] * depth, [None] * depth, [None] * depth, [None] * depth
    d_w_in, d_w_q, d_w_kv, d_w_o = [None] * depth, [None] * depth, [None] * depth, [None] * depth
    half = depth // 2
    carried = {half - 1: list(range(half, depth)), **{l - 1: [l] for l in range(1, half)}}
    done = {}

    def grads_by_chip(layers):
        stack = lambda per_layer: jnp.stack([per_layer[l] for l in layers])
        in_tiles = [stack([tiles[j] if tiles else None for tiles in d_w_in]) for j in range(4)]
        full = [_unpad_in_rows(*in_tiles), _unpad_q_rows(stack(d_w_q)), _unperm_kv_rows(stack(d_w_kv)), stack(d_w_o)]
        return [g.reshape(g.shape[0], 4, 2, g.shape[1] // 8, g.shape[2]) for g in full]

    def reduce_begin(grads5, tag):
        give = [lax.dynamic_index_in_dim(g, 1 - where[4], axis=2, keepdims=False).astype(WIRE_DTYPE) for g in grads5]
        recv = _comm_swap_sibling(give, name=f"comm_swap_sibling_{tag}")
        return recv, [_pair_sum(where, g, r, name=f"pair_sum_{tag}_{t}") for t, (g, r) in enumerate(zip(grads5, recv))]

    def reduce_end(grads5, recv, parts, tag):
        mine = [_chip_sum(where, g, r, p, name=f"chip_sum_{tag}_{t}") for t, (g, r, p) in enumerate(zip(grads5, recv, parts))]
        joined = _comm_join_halves(mine, name=f"comm_join_halves_{tag}")
        return [a.reshape(a.shape[0], -1, a.shape[-1]) for a in joined]

    for l in reversed(range(depth)):
        x, h_t, proj, oa, lse_a, qcat, kcat, v, cqn, ckvn, ob, lse_b, y_t = saved[l]
        w_in_p, w_q_p, w_kv_p, w_o = weights[l]
        dy = _matmul(dx, w_o, tb=True, name=f"out_proj_dx{l}")
        d_w_o[l] = _matmul(y_t, dx, tk=s, tn=512, name=f"out_proj_dw{l}")
        doa, dob, dga, dgb, delta_b = _gate_bwd(dy, oa, ob, proj, name=f"gate_bwd{l}")
        dqa, dka, dva, dsink = _swa_bwd(proj, sinks[l], lse_a, doa, name=f"swa_bwd{l}")
        phase, pair = carried.get(l), None
        if phase:
            grads5 = grads_by_chip(phase)
            recv, pair = reduce_begin(grads5, f"l{phase[0]}")
        dqc, dkc, dv, parts = _mla_bwd(qcat, kcat, v, dob, lse_b, delta_b, scatter=pair, name=f"mla_bwd{l}")
        if phase:
            done[phase[0]] = reduce_end(grads5, recv, parts, f"l{phase[0]}")
        mixed, d_w_q[l], d_w_kv[l], dgq_l, dgkv_l = _mla_qkv_bwd(
            proj, cqn, ckvn, dqc, dkc, dv, dka, dva, gq[l:l + 1], gkv[l:l + 1], w_q_p, w_kv_p, *tabs, layer=None,
            name=f"mla_qkv_bwd{l}")
        dproj = [dqa, dga, dgb, mixed]
        dh = _matmul_ktiles(dproj, w_in_p, name=f"in_proj_dx{l}")
        d_w_in[l] = [_matmul(h_t, tile, out_t=True, tk=s, name=f"in_proj_dw{l}_{j}") for j, tile in enumerate(dproj)]
        dx, dg_l = _rmsnorm_bwd(dh, x, attn_g[l:l + 1], dx, name=f"norm_bwd{l}")
        d_attn_g[l], d_sinks[l], d_gq[l], d_gkv[l] = dg_l, dsink[0:1, :SWA_HEADS], dgq_l, dgkv_l

    grads5 = grads_by_chip([0])
    recv, pair = reduce_begin(grads5, "l0")
    done[0] = reduce_end(grads5, recv, _comm_scatter_chips(pair, name="comm_scatter_chips_l0"), "l0")
    cat = lambda parts: jnp.concatenate(parts, axis=0)
    reduced = [cat([done[first][t] for first in sorted(done)]) for t in range(len(shards))]
    return loss, dx, cat(d_attn_g), cat(d_sinks), cat(d_gq), cat(d_gkv), d_final_g, reduced


def kernel(x, attn_norm_g, w_in, swa_sinks, q_a_norm_g, kv_a_norm_g, w_q_b, w_kv_b, w_out, final_norm_g, loss_target, m_attn_norm_g, m_w_in, m_swa_sinks, m_q_a_norm_g, m_kv_a_norm_g, m_w_q_b, m_w_kv_b, m_w_out, m_final_norm_g, v_attn_norm_g, v_w_in, v_swa_sinks, v_q_a_norm_g, v_kv_a_norm_g, v_w_q_b, v_w_kv_b, v_w_out, v_final_norm_g):
    x_, y_, c = lax.axis_index("x"), lax.axis_index("y"), lax.axis_index("c")
    chip = 2 * x_ + y_
    where = jnp.stack([chip, 2 * (1 - x_) + y_, 2 * x_ + 1 - y_, 2 * (1 - x_) + 1 - y_, c]).astype(jnp.int32)

    sent = [a.astype(WIRE_DTYPE) for a in (_t(w_in), _t(w_q_b), _t(w_kv_b), w_out)]

    loss, dx, d_attn_g, d_sinks, d_gq, d_gkv, d_final_g, reduced = _device_step(
        x[0], loss_target[0], attn_norm_g, swa_sinks, q_a_norm_g, kv_a_norm_g, final_norm_g.reshape(1, -1), sent, where)
    g_w_in, g_w_q_b, g_w_kv_b, g_w_out = _t(reduced[0]), _t(reduced[1]), _t(reduced[2]), reduced[3]

    small = [d_attn_g, d_sinks, d_gq, d_gkv, d_final_g, loss[:, :1]]
    flat = jnp.concatenate([a.reshape(-1) for a in small])
    n_small = flat.shape[0]
    rows = -(-n_small // 1024) * 8
    total = _comm_allreduce_small(jnp.pad(flat, (0, rows * 128 - n_small)).reshape(rows, 128),
                                  name="comm_allreduce_small").reshape(-1)
    outs, at = [], 0
    for a in small:
        outs.append(total[at:at + a.size].reshape(a.shape))
        at += a.size
    g_attn_g, g_sinks, g_gq, g_gkv, g_final_g, loss_total = outs
    g_final_g = g_final_g.reshape(final_norm_g.shape)

    weights = [attn_norm_g, w_in, swa_sinks, q_a_norm_g, kv_a_norm_g, w_q_b, w_kv_b, w_out, final_norm_g]
    grads = [g_attn_g, g_w_in, g_sinks, g_gq, g_gkv, g_w_q_b, g_w_kv_b, g_w_out, g_final_g]
    ms = [m_attn_norm_g, m_w_in, m_swa_sinks, m_q_a_norm_g, m_kv_a_norm_g, m_w_q_b, m_w_kv_b, m_w_out, m_final_norm_g]
    vs = [v_attn_norm_g, v_w_in, v_swa_sinks, v_q_a_norm_g, v_kv_a_norm_g, v_w_q_b, v_w_kv_b, v_w_out, v_final_norm_g]
    as2d = lambda a: a.reshape(1, -1) if a.ndim == 1 else a
    deltas, new_m, new_v = [], [], []
    for i, (w, g, m, v) in enumerate(zip(weights, grads, ms, vs)):
        view = _t if w is w_in else as2d
        d, mn, vn = _adamw(view(w), reduced[0] if w is w_in else view(g), view(m), view(v), name=f"adamw{i}")
        back = _t if w is w_in else (lambda a: a.reshape(w.shape))
        deltas.append(back(d))
        new_m.append(back(mn))
        new_v.append(back(vn))

    return (loss_total.reshape(()), dx[None], *grads, *deltas, *new_m, *new_v)
```

```python
import functools
import math

import jax
import jax.numpy as jnp
from jax import lax
from jax.experimental import pallas as pl
from jax.experimental.pallas import tpu as pltpu

F32 = jnp.float32
MXU_DTYPE = jnp.bfloat16
WIRE_DTYPE = jnp.bfloat16

EPS = 1e-6
NEG = -1e30
BLOCK = 128
D_MODEL = 2048
SWA_HEADS = 16
MLA_HEADS = 8
Q_RANK = 384
KV_RANK = 256
IN_WIDTH = 4032
MLA_SCALE = 192 ** -0.5
SWA_SCALE = 64 ** -0.5
LOG2E = math.log2(math.e)
HEADS_PER_STEP = 2
SLOPES = tuple(2.0 ** (-8.0 * (h + 1) / SWA_HEADS) for h in range(SWA_HEADS))

P_WIDTH = 4096
QA_OFF, GA_OFF, GB_OFF, CQ_OFF, KA_OFF, CKV_OFF, VA_OFF, KR_OFF = 0, 1024, 2048, 3072, 3456, 3584, 3840, 3968

ADAM_LR, ADAM_B1, ADAM_B2, ADAM_EPS, ADAM_WD, ADAM_STEP = 0.001, 0.9, 0.999, 1e-08, 0.01, 10

VMEM_LIMIT = 56 * 1024 * 1024
MESH = pl.DeviceIdType.MESH
ANY = pl.BlockSpec(memory_space=pl.ANY)


def _cparams(*sem):
    return pltpu.CompilerParams(dimension_semantics=sem, vmem_limit_bytes=VMEM_LIMIT)


def _dot(a, b, ca, cb):
    return lax.dot_general(a, b, (((ca,), (cb,)), ((), ())), preferred_element_type=F32)


def _layer_spec(block, index_map, layer):
    if layer is None:
        return pl.BlockSpec(block, index_map)
    return pl.BlockSpec((None,) + tuple(block), lambda *g: (layer,) + tuple(index_map(*g)))


def _matmul(a, b, *, name, ta=False, tb=False, out_dtype=F32, add=None, b_layer=None, tm=1024, tn=1024, tk=2048):
    (kdim, m) = a.shape if ta else a.shape[::-1]
    (n, k2) = b.shape[-2:] if tb else b.shape[-2:][::-1]
    assert kdim == k2, (a.shape, b.shape)
    tm, tn, tk = min(tm, m), min(tn, n), min(tk, kdim)
    assert m % tm == 0 and n % tn == 0 and kdim % tk == 0
    nk = kdim // tk

    def body(*refs):
        a_ref, b_ref = refs[:2]
        add_ref = None if add is None else refs[2]
        o_ref = refs[2 + (add is not None)]
        part = _dot(a_ref[...].astype(MXU_DTYPE), b_ref[...].astype(MXU_DTYPE), 0 if ta else 1, 1 if tb else 0)

        def finish(r):
            o_ref[...] = (r if add is None else add_ref[...] + r).astype(out_dtype)

        if nk == 1:
            finish(part)
            return
        acc = refs[-1]
        k = pl.program_id(2)

        @pl.when(k == 0)
        def _():
            acc[...] = part

        @pl.when((k > 0) & (k < nk - 1))
        def _():
            acc[...] += part

        @pl.when(k == nk - 1)
        def _():
            finish(acc[...] + part)

    a_spec = pl.BlockSpec((tk, tm), lambda i, j, k: (k, i)) if ta else pl.BlockSpec((tm, tk), lambda i, j, k: (i, k))
    b_spec = (_layer_spec((tn, tk), lambda i, j, k: (j, k), b_layer) if tb else
              _layer_spec((tk, tn), lambda i, j, k: (k, j), b_layer))
    in_specs, args = [a_spec, b_spec], [a, b]
    if add is not None:
        in_specs.append(pl.BlockSpec((tm, tn), lambda i, j, k: (i, j)))
        args.append(add)
    return pl.pallas_call(
        body, name=name, grid=(m // tm, n // tn, nk), in_specs=in_specs,
        out_specs=pl.BlockSpec((tm, tn), lambda i, j, k: (i, j)),
        out_shape=jax.ShapeDtypeStruct((m, n), out_dtype),
        scratch_shapes=[pltpu.VMEM((tm, tn), F32)] if nk > 1 else [],
        compiler_params=_cparams("parallel", "parallel", "arbitrary"),
    )(*args)


def _matmul_ktiles(a_tiles, b, *, name, b_layer=None, tm=512, tn=1024):
    m, kt = a_tiles[0].shape
    n = b.shape[-1]
    nt = len(a_tiles)
    assert b.shape[-2] == nt * kt
    tm, tn = min(tm, m), min(tn, n)
    assert m % tm == 0 and n % tn == 0

    def body(*refs):
        a_refs, b_refs, o_ref = refs[:nt], refs[nt:2 * nt], refs[2 * nt]
        acc = _dot(a_refs[0][...].astype(MXU_DTYPE), b_refs[0][...].astype(MXU_DTYPE), 1, 0)
        for j in range(1, nt):
            acc += _dot(a_refs[j][...].astype(MXU_DTYPE), b_refs[j][...].astype(MXU_DTYPE), 1, 0)
        o_ref[...] = acc

    in_specs = [pl.BlockSpec((tm, kt), lambda jn, i: (i, 0))] * nt
    in_specs += [_layer_spec((kt, tn), lambda jn, i, j=j: (j, jn), b_layer) for j in range(nt)]
    return pl.pallas_call(
        body, name=name, grid=(n // tn, m // tm), in_specs=in_specs,
        out_specs=pl.BlockSpec((tm, tn), lambda jn, i: (i, jn)),
        out_shape=jax.ShapeDtypeStruct((m, n), F32),
        compiler_params=_cparams("parallel", "parallel"),
    )(*a_tiles, *([b] * nt))


def _rmsnorm_fwd(x, g, *, name):
    s, d = x.shape
    tm = min(512, s)

    def body(x_ref, g_ref, h_ref):
        xv = x_ref[...]
        r = lax.rsqrt(jnp.mean(xv * xv, axis=-1, keepdims=True) + EPS)
        h_ref[...] = (xv * r * g_ref[...]).astype(MXU_DTYPE)

    return pl.pallas_call(
        body, name=name, grid=(s // tm,),
        in_specs=[pl.BlockSpec((tm, d), lambda i: (i, 0)), pl.BlockSpec((1, d), lambda i: (0, 0))],
        out_specs=pl.BlockSpec((tm, d), lambda i: (i, 0)),
        out_shape=jax.ShapeDtypeStruct((s, d), MXU_DTYPE),
        compiler_params=_cparams("parallel"),
    )(x, g)


def _rmsnorm_bwd(dh, x, g, dres, *, name):
    s, d = x.shape
    tm = min(512, s)

    def body(dh_ref, x_ref, g_ref, dres_ref, dx_ref, dg_ref):
        @pl.when(pl.program_id(0) == 0)
        def _():
            dg_ref[...] = jnp.zeros_like(dg_ref)

        xv = x_ref[...]
        r = lax.rsqrt(jnp.mean(xv * xv, axis=-1, keepdims=True) + EPS)
        xn = xv * r
        dy = dh_ref[...]
        dg_ref[...] += jnp.sum(dy * xn, axis=0, keepdims=True)
        u = dy * g_ref[...]
        dx_ref[...] = dres_ref[...] + r * (u - xn * jnp.mean(u * xn, axis=-1, keepdims=True))

    row = pl.BlockSpec((tm, d), lambda i: (i, 0))
    vec = pl.BlockSpec((1, d), lambda i: (0, 0))
    return pl.pallas_call(
        body, name=name, grid=(s // tm,), in_specs=[row, row, vec, row], out_specs=[row, vec],
        out_shape=[jax.ShapeDtypeStruct((s, d), F32), jax.ShapeDtypeStruct((1, d), F32)],
        compiler_params=_cparams("arbitrary"),
    )(dh, x, g, dres)


def _final_loss(x, g, tgt, *, name):
    s, d = x.shape
    tm = min(512, s)

    def body(x_ref, g_ref, t_ref, dx_ref, dg_ref, loss_ref):
        @pl.when(pl.program_id(0) == 0)
        def _():
            dg_ref[...] = jnp.zeros_like(dg_ref)
            loss_ref[...] = jnp.zeros_like(loss_ref)

        xv = x_ref[...]
        gv = g_ref[...]
        r = lax.rsqrt(jnp.mean(xv * xv, axis=-1, keepdims=True) + EPS)
        xn = xv * r
        err = xn * gv - t_ref[...]
        sq = jnp.sum(jnp.sum(err * err, axis=-1, keepdims=True), axis=0, keepdims=True)
        loss_ref[...] += (0.5 / d) * sq
        dy = err * (1.0 / d)
        dg_ref[...] += jnp.sum(dy * xn, axis=0, keepdims=True)
        u = dy * gv
        dx_ref[...] = r * (u - xn * jnp.mean(u * xn, axis=-1, keepdims=True))

    row = pl.BlockSpec((tm, d), lambda i: (i, 0))
    vec = pl.BlockSpec((1, d), lambda i: (0, 0))
    return pl.pallas_call(
        body, name=name, grid=(s // tm,), in_specs=[row, vec, row],
        out_specs=[row, vec, pl.BlockSpec((1, 128), lambda i: (0, 0))],
        out_shape=[jax.ShapeDtypeStruct((s, d), F32), jax.ShapeDtypeStruct((1, d), F32),
                   jax.ShapeDtypeStruct((1, 128), F32)],
        compiler_params=_cparams("arbitrary"),
    )(x, g, tgt)


def _swa_keys(kp_ref, kc_ref):
    kk = jnp.concatenate([kp_ref[...], kc_ref[...]], axis=0)
    kr = pltpu.roll(kk, 64, 1)
    lo = lax.broadcasted_iota(jnp.int32, kk.shape, 1) < 64
    return [jnp.where(lo, kk, kr).astype(MXU_DTYPE), jnp.where(lo, kr, kk).astype(MXU_DTYPE)]


GROUP = SWA_HEADS // 2


def _swa_mask(n):
    qi = lax.broadcasted_iota(jnp.int32, (BLOCK, 2 * BLOCK), 0)
    ki = lax.broadcasted_iota(jnp.int32, (BLOCK, 2 * BLOCK), 1)
    delta = BLOCK + qi - ki
    valid = (delta >= 0) & (delta < BLOCK) & ((ki >= BLOCK) | (n > 0))
    return valid, delta.astype(F32)


def _stack_heads(ref, j):
    lo = lax.broadcasted_iota(jnp.int32, (BLOCK, BLOCK), 1) < 64
    parts = []
    for r in range(GROUP):
        pair = (GROUP * j + r) // 2
        blk = ref[:, pair * 128:(pair + 1) * 128].astype(F32)
        parts.append(jnp.where(lo if r % 2 == 0 else ~lo, blk, 0.0).astype(MXU_DTYPE))
    return jnp.concatenate(parts, axis=0)


def _unstack_heads(stacked, ref, j):
    lo = lax.broadcasted_iota(jnp.int32, (BLOCK, BLOCK), 1) < 64
    for i in range(GROUP // 2):
        pair = (GROUP * j) // 2 + i
        even, odd = stacked[2 * i * BLOCK:(2 * i + 1) * BLOCK], stacked[(2 * i + 1) * BLOCK:(2 * i + 2) * BLOCK]
        ref[:, pair * 128:(pair + 1) * 128] = jnp.where(lo, even, odd).astype(ref.dtype)


def _head_rows(stacked, r):
    return stacked[r * BLOCK:(r + 1) * BLOCK]


def _swa_scores(raw, h, valid, deltaf):
    return jnp.where(valid, raw * (SWA_SCALE * LOG2E) - (SLOPES[h] * LOG2E) * deltaf, NEG)


def _swa_specs(nb):
    kcol, vcol = KA_OFF // BLOCK, VA_OFF // BLOCK
    last = nb - 1
    cur = lambda n: jnp.minimum(n, last)
    prev = lambda n: jnp.maximum(jnp.minimum(n, last) - 1, 0)
    return [
        pl.BlockSpec(memory_space=pltpu.SMEM),
        pl.BlockSpec((BLOCK, 1024), lambda n: (cur(n), QA_OFF // 1024)),
        pl.BlockSpec((BLOCK, BLOCK), lambda n: (cur(n), kcol)),
        pl.BlockSpec((BLOCK, BLOCK), lambda n: (prev(n), kcol)),
        pl.BlockSpec((BLOCK, BLOCK), lambda n: (cur(n), vcol)),
        pl.BlockSpec((BLOCK, BLOCK), lambda n: (prev(n), vcol)),
    ]


def _swa_fwd(proj, sinks, *, name):
    s = proj.shape[0]
    nb = s // BLOCK

    def body(sink_ref, q_ref, kc_ref, kp_ref, vc_ref, vp_ref, o_ref, lse_ref):
        n = pl.program_id(0)
        keys = _swa_keys(kp_ref, kc_ref)
        vals = _swa_keys(vp_ref, vc_ref)
        valid, deltaf = _swa_mask(n)
        lane = lax.broadcasted_iota(jnp.int32, (BLOCK, BLOCK), 1)
        lse_acc = jnp.zeros((BLOCK, BLOCK), F32)
        for j in range(2):
            raw = _dot(_stack_heads(q_ref, j), keys[j], 1, 1)
            probs = []
            for r in range(GROUP):
                h = GROUP * j + r
                sc = _swa_scores(_head_rows(raw, r), h, valid, deltaf)
                sink = sink_ref[h] * LOG2E
                m = jnp.maximum(jnp.max(sc, axis=-1, keepdims=True), sink)
                p = jnp.exp2(sc - m)
                l = jnp.sum(p, axis=-1, keepdims=True) + jnp.exp2(sink - m)
                probs.append((p * (1.0 / l)).astype(MXU_DTYPE))
                lse_acc = jnp.where(lane == h, m + jnp.log(l) * LOG2E, lse_acc)
            _unstack_heads(jnp.dot(jnp.concatenate(probs, axis=0), vals[j], preferred_element_type=F32), o_ref, j)
        lse_ref[...] = lse_acc

    return pl.pallas_call(
        body, name=name, grid=(nb,), in_specs=_swa_specs(nb),
        out_specs=[pl.BlockSpec((BLOCK, 1024), lambda n: (n, 0)), pl.BlockSpec((BLOCK, BLOCK), lambda n: (n, 0))],
        out_shape=[jax.ShapeDtypeStruct((s, 1024), F32), jax.ShapeDtypeStruct((s, BLOCK), F32)],
        compiler_params=_cparams("parallel"),
    )(sinks, proj, proj, proj, proj, proj)


def _swa_bwd(proj, sinks, lse, do, *, name):
    s = proj.shape[0]
    nb = s // BLOCK
    last = nb - 1

    def body(sink_ref, q_ref, kc_ref, kp_ref, vc_ref, vp_ref, lse_ref, do_ref,
             dq_ref, dk_ref, dv_ref, dsink_ref, carry_k, carry_v):
        n = pl.program_id(0)

        @pl.when(n == 0)
        def _():
            carry_k[...] = jnp.zeros_like(carry_k)
            carry_v[...] = jnp.zeros_like(carry_v)
            dsink_ref[...] = jnp.zeros_like(dsink_ref)

        @pl.when(n < nb)
        def _():
            keys = _swa_keys(kp_ref, kc_ref)
            vals = _swa_keys(vp_ref, vc_ref)
            valid, deltaf = _swa_mask(n)
            lane = lax.broadcasted_iota(jnp.int32, (BLOCK, BLOCK), 1)
            lane1 = lax.broadcasted_iota(jnp.int32, (1, BLOCK), 1)
            lse_blk = lse_ref[...]
            acc_k, acc_v = [], []
            dsink = jnp.zeros((1, BLOCK), F32)
            for j in range(2):
                q_all, do_all = _stack_heads(q_ref, j), _stack_heads(do_ref, j)
                raw = _dot(q_all, keys[j], 1, 1)
                dp_all = _dot(do_all, vals[j], 1, 1)
                probs, dscores = [], []
                for r in range(GROUP):
                    h = GROUP * j + r
                    lse_h = jnp.sum(jnp.where(lane == h, lse_blk, 0.0), axis=-1, keepdims=True)
                    p = jnp.exp2(_swa_scores(_head_rows(raw, r), h, valid, deltaf) - lse_h)
                    dp = _head_rows(dp_all, r)
                    dlt = jnp.sum(dp * p, axis=-1, keepdims=True)
                    dscores.append((p * (dp - dlt) * SWA_SCALE).astype(MXU_DTYPE))
                    probs.append(p.astype(MXU_DTYPE))
                    sunk = jnp.exp2(sink_ref[h] * LOG2E - lse_h) * dlt
                    dsink = jnp.where(lane1 == h, -jnp.sum(sunk, axis=0, keepdims=True), dsink)
                ds_all = jnp.concatenate(dscores, axis=0)
                _unstack_heads(jnp.dot(ds_all, keys[j], preferred_element_type=F32), dq_ref, j)
                acc_k.append(_dot(ds_all, q_all, 0, 0))
                acc_v.append(_dot(jnp.concatenate(probs, axis=0), do_all, 0, 0))
            lo2 = lax.broadcasted_iota(jnp.int32, (2 * BLOCK, BLOCK), 1) < 64
            fold = lambda acc: jnp.where(lo2, acc[0] + pltpu.roll(acc[0], 64, 1), acc[1] + pltpu.roll(acc[1], 64, 1))
            dkk, dvv = fold(acc_k), fold(acc_v)
            dk_ref[...] = (carry_k[...] + dkk[:BLOCK]).astype(dk_ref.dtype)
            dv_ref[...] = (carry_v[...] + dvv[:BLOCK]).astype(dv_ref.dtype)
            carry_k[...] = dkk[BLOCK:]
            carry_v[...] = dvv[BLOCK:]
            dsink_ref[...] += jnp.broadcast_to(dsink, dsink_ref.shape)

        @pl.when(n == nb)
        def _():
            dk_ref[...] = carry_k[...].astype(dk_ref.dtype)
            dv_ref[...] = carry_v[...].astype(dv_ref.dtype)

    cur = lambda n: jnp.minimum(n, last)
    lag = lambda n: jnp.maximum(n - 1, 0)
    return pl.pallas_call(
        body, name=name, grid=(nb + 1,),
        in_specs=_swa_specs(nb) + [pl.BlockSpec((BLOCK, BLOCK), lambda n: (cur(n), 0)),
                                   pl.BlockSpec((BLOCK, 1024), lambda n: (cur(n), 0))],
        out_specs=[pl.BlockSpec((BLOCK, 1024), lambda n: (cur(n), 0)),
                   pl.BlockSpec((BLOCK, BLOCK), lambda n: (lag(n), 0)),
                   pl.BlockSpec((BLOCK, BLOCK), lambda n: (lag(n), 0)),
                   pl.BlockSpec((8, BLOCK), lambda n: (0, 0))],
        out_shape=[jax.ShapeDtypeStruct((s, 1024), MXU_DTYPE), jax.ShapeDtypeStruct((s, BLOCK), MXU_DTYPE),
                   jax.ShapeDtypeStruct((s, BLOCK), MXU_DTYPE), jax.ShapeDtypeStruct((8, BLOCK), F32)],
        scratch_shapes=[pltpu.VMEM((BLOCK, BLOCK), F32), pltpu.VMEM((BLOCK, BLOCK), F32)],
        compiler_params=_cparams("arbitrary"),
    )(sinks, proj, proj, proj, proj, proj, lse, do)


def _rope_partner(v, first, width):
    lane = lax.broadcasted_iota(jnp.int32, v.shape, 1)
    in_a = (lane >= first) & (lane < first + 32)
    in_b = (lane >= first + 32) & (lane < first + 64)
    return jnp.where(in_a, pltpu.roll(v, width - 32, 1), jnp.where(in_b, pltpu.roll(v, 32, 1), 0.0))


def _mla_qkv_fwd(proj, gq, gkv, wq, wkv, tk_c, tk_s, *, layer, name):
    s = proj.shape[0]
    tm = min(256, s)

    def body(cq_ref, ckv_ref, kr_ref, gq_ref, gkv_ref, wq_ref, wkv_ref, kc_ref, ks_ref,
             qcat_ref, kcat_ref, v_ref, cqn_ref, ckvn_ref):
        cq = cq_ref[...]
        cqn = (cq * lax.rsqrt(jnp.mean(cq * cq, axis=-1, keepdims=True) + EPS) * gq_ref[...]).astype(MXU_DTYPE)
        cqn_ref[...] = cqn
        qpre = _dot(cqn, wq_ref[...], 1, 1)
        kc, ks = kc_ref[...], ks_ref[...]
        for hh in range(MLA_HEADS):
            qcat_ref[:, hh * 256:hh * 256 + 128] = qpre[:, hh * 256:hh * 256 + 128].astype(MXU_DTYPE)
            blk = qpre[:, hh * 256 + 128:(hh + 1) * 256]
            qcat_ref[:, hh * 256 + 128:(hh + 1) * 256] = (blk * kc + _rope_partner(blk, 0, 128) * ks).astype(MXU_DTYPE)
        ckv = ckv_ref[...]
        ckvn = (ckv * lax.rsqrt(jnp.mean(ckv * ckv, axis=-1, keepdims=True) + EPS) * gkv_ref[...]).astype(MXU_DTYPE)
        ckvn_ref[...] = ckvn
        kv = _dot(ckvn, wkv_ref[...], 1, 1)
        kr = kr_ref[...]
        krr = (kr * kc + _rope_partner(kr, 0, 128) * ks).astype(MXU_DTYPE)
        for hh in range(MLA_HEADS):
            kcat_ref[:, hh * 256:hh * 256 + 128] = kv[:, hh * 128:(hh + 1) * 128].astype(MXU_DTYPE)
            kcat_ref[:, hh * 256 + 128:(hh + 1) * 256] = krr
            v_ref[:, hh * 256:hh * 256 + 128] = kv[:, 1024 + hh * 128:1024 + (hh + 1) * 128].astype(MXU_DTYPE)
            v_ref[:, hh * 256 + 128:(hh + 1) * 256] = jnp.ones((tm, 128), MXU_DTYPE)

    row = lambda w, c: pl.BlockSpec((tm, w), lambda i: (i, c))
    full = lambda a: pl.BlockSpec(a.shape, lambda i: (0, 0))
    of_layer = lambda a: _layer_spec(a.shape[-2:], lambda i: (0, 0), layer)
    return pl.pallas_call(
        body, name=name, grid=(s // tm,),
        in_specs=[row(Q_RANK, CQ_OFF // Q_RANK), row(KV_RANK, CKV_OFF // KV_RANK), row(128, KR_OFF // 128),
                  full(gq), full(gkv), of_layer(wq), of_layer(wkv), row(128, 0), row(128, 0)],
        out_specs=[row(2048, 0), row(2048, 0), row(2048, 0), row(Q_RANK, 0), row(KV_RANK, 0)],
        out_shape=[jax.ShapeDtypeStruct((s, 2048), MXU_DTYPE), jax.ShapeDtypeStruct((s, 2048), MXU_DTYPE),
                   jax.ShapeDtypeStruct((s, 2048), MXU_DTYPE), jax.ShapeDtypeStruct((s, Q_RANK), MXU_DTYPE),
                   jax.ShapeDtypeStruct((s, KV_RANK), MXU_DTYPE)],
        compiler_params=_cparams("parallel"),
    )(proj, proj, proj, gq, gkv, wq, wkv, tk_c, tk_s)


def _norm_bwd(x, g, dy):
    r = lax.rsqrt(jnp.mean(x * x, axis=-1, keepdims=True) + EPS)
    xn = x * r
    u = dy * g
    return r * (u - xn * jnp.mean(u * xn, axis=-1, keepdims=True)), jnp.sum(dy * xn, axis=0, keepdims=True)


def _mla_qkv_bwd(proj, cqn, ckvn, dqcat, dkcat, dv, dka, dva, gq, gkv, wq, wkv, tk_c, tk_s, *, layer, name):
    s = proj.shape[0]
    tm = min(256, s)
    t_cq, t_ka, t_ckv, t_va, t_kr = (o - CQ_OFF for o in (CQ_OFF, KA_OFF, CKV_OFF, VA_OFF, KR_OFF))

    def body(cq_ref, ckv_ref, cqn_ref, ckvn_ref, dq_ref, dk_ref, dv_ref, dka_ref, dva_ref, gq_ref, gkv_ref, wq_ref,
             wkv_ref, kc_ref, ks_ref,
             tile_ref, dwq_ref, dwkv_ref, dgq_ref, dgkv_ref, dqpre, dkv, dwq_acc, dwkv_acc):
        dcq_ref = tile_ref.at[:, t_cq:t_cq + Q_RANK]
        dckv_ref = tile_ref.at[:, t_ckv:t_ckv + KV_RANK]
        dkr_ref = tile_ref.at[:, t_kr:t_kr + 128]
        tile_ref[:, t_ka:t_ka + 128] = dka_ref[...]
        tile_ref[:, t_va:t_va + 128] = dva_ref[...]

        @pl.when(pl.program_id(0) == 0)
        def _():
            for r in (dwq_acc, dwkv_acc, dgq_ref, dgkv_ref):
                r[...] = jnp.zeros_like(r)

        kc, ks = kc_ref[...], ks_ref[...]
        dkrr = jnp.zeros((tm, 128), F32)
        for hh in range(MLA_HEADS):
            dqpre[:, hh * 256:hh * 256 + 128] = dq_ref[:, hh * 256:hh * 256 + 128].astype(MXU_DTYPE)
            blk = dq_ref[:, hh * 256 + 128:(hh + 1) * 256]
            dqpre[:, hh * 256 + 128:(hh + 1) * 256] = (blk * kc + _rope_partner(blk * ks, 0, 128)).astype(MXU_DTYPE)
            dkv[:, hh * 128:(hh + 1) * 128] = dk_ref[:, hh * 256:hh * 256 + 128].astype(MXU_DTYPE)
            dkrr = dkrr + dk_ref[:, hh * 256 + 128:(hh + 1) * 256]
        dkv[:, 1024:] = dv_ref[...].astype(MXU_DTYPE)
        dkr_ref[...] = (dkrr * kc + _rope_partner(dkrr * ks, 0, 128)).astype(dkr_ref.dtype)

        dq_b = dqpre[...]
        dwq_acc[...] += _dot(cqn_ref[...], dq_b, 0, 0)
        dcq, dgq = _norm_bwd(cq_ref[...], gq_ref[...], _dot(dq_b, wq_ref[...], 1, 0))
        dcq_ref[...] = dcq.astype(dcq_ref.dtype)
        dgq_ref[...] += dgq

        dkv_b = dkv[...]
        dwkv_acc[...] += _dot(ckvn_ref[...], dkv_b, 0, 0)
        dckv, dgkv = _norm_bwd(ckv_ref[...], gkv_ref[...], _dot(dkv_b, wkv_ref[...], 1, 0))
        dckv_ref[...] = dckv.astype(dckv_ref.dtype)
        dgkv_ref[...] += dgkv

        @pl.when(pl.program_id(0) == s // tm - 1)
        def _():
            dwq_ref[...] = dwq_acc[...].T
            dwkv_ref[...] = dwkv_acc[...].T

    row = lambda w, c: pl.BlockSpec((tm, w), lambda i: (i, c))
    full = lambda shape: pl.BlockSpec(shape, lambda i: (0, 0))
    of_layer = lambda a: _layer_spec(a.shape[-2:], lambda i: (0, 0), layer)
    return pl.pallas_call(
        body, name=name, grid=(s // tm,),
        in_specs=[row(Q_RANK, CQ_OFF // Q_RANK), row(KV_RANK, CKV_OFF // KV_RANK), row(Q_RANK, 0), row(KV_RANK, 0),
                  row(2048, 0), row(2048, 0), row(1024, 0), row(128, 0), row(128, 0), full(gq.shape), full(gkv.shape),
                  of_layer(wq), of_layer(wkv), row(128, 0), row(128, 0)],
        out_specs=[row(1024, 0), full(wq.shape[-2:]), full(wkv.shape[-2:]), full(gq.shape), full(gkv.shape)],
        out_shape=[jax.ShapeDtypeStruct((s, 1024), MXU_DTYPE), jax.ShapeDtypeStruct(wq.shape[-2:], F32),
                   jax.ShapeDtypeStruct(wkv.shape[-2:], F32), jax.ShapeDtypeStruct(gq.shape, F32),
                   jax.ShapeDtypeStruct(gkv.shape, F32)],
        scratch_shapes=[pltpu.VMEM((tm, 2048), MXU_DTYPE), pltpu.VMEM((tm, 2048), MXU_DTYPE),
                        pltpu.VMEM((Q_RANK, 2048), F32), pltpu.VMEM((KV_RANK, 2048), F32)],
        compiler_params=_cparams("arbitrary"),
    )(proj, proj, cqn, ckvn, dqcat, dkcat, dv, dka, dva, gq, gkv, wq, wkv, tk_c, tk_s)


def _loop_by_two(lo, hi, step):
    n = hi - lo

    def four(i, carry):
        for u in range(4):
            step(lo + 4 * i + u)
        return carry

    lax.fori_loop(0, n // 4, four, 0)
    rest = lo + (n // 4) * 4

    @pl.when(n % 4 >= 2)
    def _():
        step(rest)
        step(rest + 1)

    @pl.when(n % 2 == 1)
    def _():
        step(hi - 1)


def _causal_mask(t):
    return lax.broadcasted_iota(jnp.int32, (t, t), 1) <= lax.broadcasted_iota(jnp.int32, (t, t), 0)


def _mla_fwd(qcat, kcat, v, *, name, gather=None):
    s = qcat.shape[0]
    t = min(512, s)
    nq = s // t
    hp = HEADS_PER_STEP
    ng = MLA_HEADS // hp
    c2 = MLA_SCALE * LOG2E

    nw = 0 if gather is None else len(gather)

    def body(q_ref, k_ref, v_ref, *rest):
        src, (o_ref, lse_ref), out = rest[:nw], rest[nw:nw + 2], rest[nw + 2:2 * nw + 2]
        top_s, acc_s, *sems = rest[2 * nw + 2:]
        g, qi = pl.program_id(0), pl.program_id(1)
        if nw:
            @pl.when((g == 0) & (qi == 0))
            def _():
                _gather_start(src, out, *sems)

            @pl.when((g == ng - 1) & (qi == 0))
            def _():
                _gather_forward(src, out, *sems)

        rows = lambda j: pl.ds(pl.multiple_of(j * t, t), t)
        head = lambda e: slice(e * 256, (e + 1) * 256)
        raw = lambda e, j: _dot(q_ref[:, head(e)], k_ref[rows(j), head(e)], 1, 1)

        for e in range(hp):
            top_s[e] = jnp.where(_causal_mask(t), raw(e, qi), NEG)

        def pass1(j):
            for e in range(hp):
                top_s[e] = jnp.maximum(top_s[e], raw(e, j))

        _loop_by_two(0, qi, pass1)
        m = [jnp.max(top_s[e], axis=-1, keepdims=True) * c2 for e in range(hp)]

        def weighted(e, j, masked):
            sc = raw(e, j) * c2 - m[e]
            if masked:
                sc = jnp.where(_causal_mask(t), sc, NEG)
            return jnp.dot(jnp.exp2(sc).astype(MXU_DTYPE), v_ref[rows(j), head(e)], preferred_element_type=F32)

        for e in range(hp):
            acc_s[e] = weighted(e, qi, True)

        def pass2(j):
            for e in range(hp):
                acc_s[e] += weighted(e, j, False)

        _loop_by_two(0, qi, pass2)
        lane = lax.broadcasted_iota(jnp.int32, (t, 128), 1)
        stats = jnp.zeros((t, 128), F32)
        for e in range(hp):
            l = acc_s[e, :, 128:]
            o_ref[:, e * 128:(e + 1) * 128] = acc_s[e, :, :128] / l
            stats = jnp.where(lane == e, m[e] + jnp.log(l) * LOG2E, stats)
        lse_ref[...] = stats
        if nw:
            @pl.when((g == ng - 1) & (qi == nq - 1))
            def _():
                _gather_finish(src, out, *sems)

    outs = pl.pallas_call(
        body, name=name, grid=(ng, nq),
        in_specs=[pl.BlockSpec((t, 256 * hp), lambda g, qi: (qi, g)), pl.BlockSpec((s, 256 * hp), lambda g, qi: (0, g)),
                  pl.BlockSpec((s, 256 * hp), lambda g, qi: (0, g))] + [ANY] * nw,
        out_specs=[pl.BlockSpec((t, 128 * hp), lambda g, qi: (qi, g)), pl.BlockSpec((t, 128), lambda g, qi: (qi, g))]
        + [ANY] * nw,
        out_shape=[jax.ShapeDtypeStruct((s, 1024), F32), jax.ShapeDtypeStruct((s, 128 * ng), F32)]
        + (_gathered_shapes(gather) if nw else []),
        scratch_shapes=[pltpu.VMEM((hp, t, t), F32), pltpu.VMEM((hp, t, 256), F32)] + (_sems(6, nw) if nw else []),
        compiler_params=_cparams("arbitrary", "arbitrary"),
    )(qcat, kcat, v, *(gather or []))
    return outs[0], outs[1], list(outs[2:])


def _mla_bwd(qcat, kcat, v, do, lse, delta, *, name, scatter=None):
    s = qcat.shape[0]
    t = min(512, s)
    nq = s // t
    hp = HEADS_PER_STEP
    c2 = MLA_SCALE * LOG2E
    nw = 0 if scatter is None else len(scatter)

    def body(q_ref, k_ref, v_ref, do_ref, lse_ref, dl_ref, *rest):
        src, (dq_ref, dk_ref, dv_ref), out = rest[:nw], rest[nw:nw + 3], rest[nw + 3:2 * nw + 3]
        dk_acc, dv_acc, *sems = rest[2 * nw + 3:]
        h, ki = pl.program_id(0), pl.program_id(1)
        if nw:
            @pl.when((h == 0) & (ki == 0))
            def _():
                _scatter_start(src, out, *sems)

        @pl.when(ki == 0)
        def _():
            dq_ref[...] = jnp.zeros_like(dq_ref)

        dk_acc[...] = jnp.zeros_like(dk_acc)
        dv_acc[...] = jnp.zeros_like(dv_acc)
        k, vv = k_ref[...], v_ref[...]
        mine = lax.broadcasted_iota(jnp.int32, (t, 128), 1) == h % hp

        def chunk(qi, masked):
            rows = pl.ds(pl.multiple_of(qi * t, t), t)
            q, dob = q_ref[rows, :], do_ref[rows, :]
            pick = lambda r: jnp.sum(jnp.where(mine, r[rows, :], 0.0), axis=-1, keepdims=True)
            sc = _dot(q, k, 1, 1) * c2
            if masked:
                sc = jnp.where(_causal_mask(t), sc, NEG)
            p = jnp.exp2(sc - pick(lse_ref))
            dp = _dot(dob, vv, 1, 1)
            ds = (p * (dp - pick(dl_ref)) * MLA_SCALE).astype(MXU_DTYPE)
            dv_acc[...] += _dot(dob, p.astype(MXU_DTYPE), 0, 0)
            dk_acc[...] += _dot(q, ds, 0, 0)
            dq_ref[rows, :] += jnp.dot(ds, k, preferred_element_type=F32)

        chunk(ki, True)
        _loop_by_two(ki + 1, nq, lambda qi: chunk(qi, False))
        dk_ref[...] = dk_acc[...].T
        dv_ref[...] = dv_acc[...].T
        if nw:
            @pl.when((h == MLA_HEADS - 1) & (ki == nq - 1))
            def _():
                _scatter_finish(src, out, *sems)

    head = lambda w: pl.BlockSpec((s, w), lambda h, ki: (0, h))
    blk = lambda w: pl.BlockSpec((t, w), lambda h, ki: (ki, h))
    stat = pl.BlockSpec((s, 128), lambda h, ki: (0, h // hp))
    outs = pl.pallas_call(
        body, name=name, grid=(MLA_HEADS, nq),
        in_specs=[head(256), blk(256), pl.BlockSpec((t, 128), lambda h, ki: (ki, 2 * h)), head(128), stat, stat]
        + [ANY] * nw,
        out_specs=[head(256), blk(256), blk(128)] + [ANY] * nw,
        out_shape=[jax.ShapeDtypeStruct((s, 2048), F32), jax.ShapeDtypeStruct((s, 2048), F32),
                   jax.ShapeDtypeStruct((s, 1024), F32)] + [jax.ShapeDtypeStruct(a.shape, a.dtype) for a in scatter or []],
        scratch_shapes=[pltpu.VMEM((256, t), F32), pltpu.VMEM((128, t), F32)] + (_sems(3, nw) if nw else []),
        compiler_params=_cparams("arbitrary", "arbitrary"),
    )(qcat, kcat, v, do, lse, delta, *(scatter or []))
    return outs[0], outs[1], outs[2], list(outs[3:])


def _gate_specs(tm):
    half = lambda c: pl.BlockSpec((tm, 1024), lambda i: (i, c))
    return half(0), half(GA_OFF // 1024), half(GB_OFF // 1024)


def _gate_fwd(oa, ob, proj, *, name):
    s = oa.shape[0]
    tm = min(512, s)

    def body(oa_ref, ob_ref, ga_ref, gb_ref, y_ref):
        ga, gb = ga_ref[...], gb_ref[...]
        y_ref[:, :1024] = (oa_ref[...] * (ga * jax.nn.sigmoid(ga))).astype(MXU_DTYPE)
        y_ref[:, 1024:] = (ob_ref[...] * (gb * jax.nn.sigmoid(gb))).astype(MXU_DTYPE)

    o_spec, ga_spec, gb_spec = _gate_specs(tm)
    return pl.pallas_call(
        body, name=name, grid=(s // tm,), in_specs=[o_spec, o_spec, ga_spec, gb_spec],
        out_specs=pl.BlockSpec((tm, 2048), lambda i: (i, 0)),
        out_shape=jax.ShapeDtypeStruct((s, 2048), MXU_DTYPE),
        compiler_params=_cparams("parallel"),
    )(oa, ob, proj, proj)


def _gate_bwd(dy, oa, ob, proj, *, name):
    s = oa.shape[0]
    tm = min(512, s)

    def body(dy_ref, oa_ref, ob_ref, ga_ref, gb_ref, doa_ref, dob_ref, dga_ref, dgb_ref, dl_ref):
        def branch(dyv, o, g, do_ref, dg_ref):
            sg = jax.nn.sigmoid(g)
            do = dyv * (g * sg)
            do_ref[...] = do.astype(MXU_DTYPE)
            dg_ref[...] = (dyv * o * (sg * (1.0 + g * (1.0 - sg)))).astype(MXU_DTYPE)
            return do

        branch(dy_ref[:, :1024], oa_ref[...], ga_ref[...], doa_ref, dga_ref)
        ob = ob_ref[...]
        prod = branch(dy_ref[:, 1024:], ob, gb_ref[...], dob_ref, dgb_ref) * ob
        lane = lax.broadcasted_iota(jnp.int32, (tm, stat_w), 1)
        acc = jnp.zeros((tm, stat_w), F32)
        for hh in range(MLA_HEADS):
            at = (hh // HEADS_PER_STEP) * 128 + hh % HEADS_PER_STEP
            acc = jnp.where(lane == at, jnp.sum(prod[:, hh * 128:(hh + 1) * 128], axis=-1, keepdims=True), acc)
        dl_ref[...] = acc

    stat_w = 128 * (MLA_HEADS // HEADS_PER_STEP)
    o_spec, ga_spec, gb_spec = _gate_specs(tm)
    return pl.pallas_call(
        body, name=name, grid=(s // tm,),
        in_specs=[pl.BlockSpec((tm, 2048), lambda i: (i, 0)), o_spec, o_spec, ga_spec, gb_spec],
        out_specs=[o_spec, o_spec, o_spec, o_spec, pl.BlockSpec((tm, stat_w), lambda i: (i, 0))],
        out_shape=[jax.ShapeDtypeStruct((s, 1024), MXU_DTYPE)] * 4 + [jax.ShapeDtypeStruct((s, stat_w), F32)],
        compiler_params=_cparams("parallel"),
    )(dy, oa, ob, proj, proj)


def _row_block(rows, cols, itemsize=4, budget=2 << 20):
    fits = [tr for tr in range(16, rows + 1, 16) if rows % tr == 0 and tr * cols * itemsize <= budget]
    return fits[-1] if fits else rows


def _adamw(w, g, m, v, *, name):
    shape = w.shape
    rows, cols = shape[-2:]
    w3, g3, m3, v3 = (a.reshape((-1, rows, cols)) for a in (w, g, m, v))
    lead = w3.shape[0]
    tr = _row_block(rows, cols)

    def body(w_ref, g_ref, m_ref, v_ref, d_ref, mo_ref, vo_ref):
        gv = g_ref[...]
        mn = ADAM_B1 * m_ref[...] + (1.0 - ADAM_B1) * gv
        vn = ADAM_B2 * v_ref[...] + (1.0 - ADAM_B2) * jnp.square(gv)
        m_hat = mn / (1.0 - ADAM_B1 ** ADAM_STEP)
        v_hat = vn / (1.0 - ADAM_B2 ** ADAM_STEP)
        d_ref[...] = -ADAM_LR * (m_hat / (jnp.sqrt(v_hat) + ADAM_EPS) + ADAM_WD * w_ref[...])
        mo_ref[...] = mn
        vo_ref[...] = vn

    spec = pl.BlockSpec((None, tr, cols), lambda a, i: (a, i, 0))
    outs = pl.pallas_call(
        body, name=name, grid=(lead, rows // tr), in_specs=[spec] * 4, out_specs=[spec] * 3,
        out_shape=[jax.ShapeDtypeStruct((lead, rows, cols), F32)] * 3,
        compiler_params=_cparams("parallel", "parallel"),
    )(w3, g3, m3, v3)
    return tuple(o.reshape(shape) for o in outs)


def _pair_sum(where, grads, recv, *, name):
    layers, chips, _, rows, cols = grads.shape
    tr = _row_block(rows, cols)

    def body(where_ref, a_ref, b_ref, o_ref):
        o_ref[...] = (a_ref[...] + b_ref[...].astype(F32)).astype(WIRE_DTYPE)

    spec = pl.BlockSpec((None, None, tr, cols), lambda a, k, i, w: (a, k, i, 0))
    return pl.pallas_call(
        body, name=name,
        grid_spec=pltpu.PrefetchScalarGridSpec(
            num_scalar_prefetch=1, grid=(layers, chips, rows // tr),
            in_specs=[pl.BlockSpec((None, None, None, tr, cols), lambda a, k, i, w: (a, k, w[4], i, 0)), spec],
            out_specs=spec),
        out_shape=jax.ShapeDtypeStruct((layers, chips, rows, cols), WIRE_DTYPE),
        compiler_params=_cparams("parallel", "parallel", "parallel"),
    )(where, grads, recv)


def _chip_sum(where, grads, recv, parts, *, name):
    layers, _, _, rows, cols = grads.shape
    tr = _row_block(rows, cols)

    def body(where_ref, a_ref, b_ref, t0_ref, t1_ref, t2_ref, o_ref):
        total = a_ref[...] + b_ref[...].astype(F32)
        for t_ref in (t0_ref, t1_ref, t2_ref):
            total = total + t_ref[...].astype(F32)
        o_ref[...] = total

    slot = lambda j: pl.BlockSpec((None, None, tr, cols), lambda a, i, w: (a, w[j], i, 0))
    return pl.pallas_call(
        body, name=name,
        grid_spec=pltpu.PrefetchScalarGridSpec(
            num_scalar_prefetch=1, grid=(layers, rows // tr),
            in_specs=[pl.BlockSpec((None, None, None, tr, cols), lambda a, i, w: (a, w[0], w[4], i, 0)), slot(0), slot(1),
                      slot(2), slot(3)],
            out_specs=slot(4)),
        out_shape=jax.ShapeDtypeStruct((layers, 2, rows, cols), F32),
        compiler_params=_cparams("parallel", "parallel"),
    )(where, grads, recv, parts, parts, parts)


def _place():
    x, y, c = lax.axis_index("x"), lax.axis_index("y"), lax.axis_index("c")
    chips = [(1 - x, y), (x, 1 - y), (1 - x, 1 - y)]
    return x, y, c, chips


def _sems(*shape):
    return [pltpu.SemaphoreType.DMA(shape), pltpu.SemaphoreType.DMA(shape)]


OWNER_CORE = (0, 1, 1, 1)


def _gather_copy(src, out, send_sems, recv_sems, t, sem, slot, to, forward=False):
    n = src[t].shape[0]
    rows = out[t].at[pl.ds(pl.multiple_of(slot * n, 16), n)]
    return pltpu.make_async_remote_copy(src_ref=rows if forward else src[t], dst_ref=rows, send_sem=send_sems.at[sem, t],
                                        recv_sem=recv_sems.at[sem, t], device_id=to, device_id_type=MESH)


def _gather_start(src, out, send_sems, recv_sems):
    x, y, c, chips = _place()
    for t, owner in enumerate(OWNER_CORE):
        @pl.when(c == owner)
        def _():
            for j, chip in enumerate(chips):
                _gather_copy(src, out, send_sems, recv_sems, t, j, 2 * x + y, (*chip, c)).start()


def _gather_forward(src, out, send_sems, recv_sems):
    x, y, c, chips = _place()
    for t, owner in enumerate(OWNER_CORE):
        @pl.when(c == owner)
        def _():
            for j, (px, py) in enumerate(chips):
                _gather_copy(src, out, send_sems, recv_sems, t, j, 2 * px + py, (px, py, c)).wait_recv()
                _gather_copy(src, out, send_sems, recv_sems, t, 3 + j, 2 * px + py, (x, y, 1 - c), forward=True).start()


def _gather_finish(src, out, send_sems, recv_sems):
    x, y, c, chips = _place()
    slots = [2 * px + py for px, py in chips]
    for t, owner in enumerate(OWNER_CORE):
        copy = functools.partial(_gather_copy, src, out, send_sems, recv_sems, t)

        @pl.when(c == owner)
        def _():
            for j, (px, py) in enumerate(chips):
                copy(j, 2 * x + y, (px, py, c)).wait_send()
                copy(3 + j, slots[j], (x, y, 1 - c), forward=True).wait_send()

        @pl.when(c != owner)
        def _():
            for j in range(3):
                copy(3 + j, slots[j], (x, y, 1 - c), forward=True).wait_recv()


def _gathered_shapes(shards):
    return [jax.ShapeDtypeStruct((4 * a.shape[0], a.shape[1]), a.dtype) for a in shards]


def _comm_gather_layer(shards, *, name):
    nt = len(shards)

    def body(*refs):
        src, out, sems = refs[:nt], refs[nt:2 * nt], refs[2 * nt:]
        _gather_start(src, out, *sems)
        _gather_forward(src, out, *sems)
        _gather_finish(src, out, *sems)

    return pl.pallas_call(
        body, name=name, in_specs=[ANY] * nt, out_specs=[ANY] * nt, out_shape=_gathered_shapes(shards),
        scratch_shapes=_sems(6, nt),
    )(*shards)


def _comm_swap_sibling(bufs, *, name):
    nt = len(bufs)

    def body(*refs):
        src, out, (send_sems, recv_sems) = refs[:nt], refs[nt:2 * nt], refs[2 * nt:]
        x, y, c, _ = _place()
        cps = [pltpu.make_async_remote_copy(src_ref=src[t], dst_ref=out[t], send_sem=send_sems.at[t], recv_sem=recv_sems.at[t],
                                            device_id=(x, y, 1 - c), device_id_type=MESH) for t in range(nt)]
        for cp in cps:
            cp.start()
        for cp in cps:
            cp.wait()

    return pl.pallas_call(
        body, name=name, in_specs=[ANY] * nt, out_specs=[ANY] * nt,
        out_shape=[jax.ShapeDtypeStruct(a.shape, a.dtype) for a in bufs], scratch_shapes=_sems(nt),
    )(*bufs)


def _scatter_copy(src, out, send_sems, recv_sems, j, t, from_slot, to_slot, to):
    return pltpu.make_async_remote_copy(src_ref=src[t].at[:, from_slot], dst_ref=out[t].at[:, to_slot],
                                        send_sem=send_sems.at[j, t], recv_sem=recv_sems.at[j, t], device_id=to,
                                        device_id_type=MESH)


def _scatter_start(src, out, send_sems, recv_sems):
    x, y, c, chips = _place()
    for j, (px, py) in enumerate(chips):
        for t in range(len(src)):
            _scatter_copy(src, out, send_sems, recv_sems, j, t, 2 * px + py, 2 * x + y, (px, py, c)).start()


def _scatter_finish(src, out, send_sems, recv_sems):
    x, y, c, chips = _place()
    for j, (px, py) in enumerate(chips):
        for t in range(len(src)):
            _scatter_copy(src, out, send_sems, recv_sems, j, t, 2 * x + y, 2 * px + py, (px, py, c)).wait_recv()
    for j, (px, py) in enumerate(chips):
        for t in range(len(src)):
            _scatter_copy(src, out, send_sems, recv_sems, j, t, 2 * px + py, 2 * x + y, (px, py, c)).wait_send()


def _comm_scatter_chips(parts, *, name):
    nt = len(parts)

    def body(*refs):
        src, out, sems = refs[:nt], refs[nt:2 * nt], refs[2 * nt:]
        _scatter_start(src, out, *sems)
        _scatter_finish(src, out, *sems)

    return pl.pallas_call(
        body, name=name, in_specs=[ANY] * nt, out_specs=[ANY] * nt,
        out_shape=[jax.ShapeDtypeStruct(a.shape, a.dtype) for a in parts], scratch_shapes=_sems(3, nt),
    )(*parts)


def _comm_join_halves(bufs, *, name):
    nt = len(bufs)

    def body(*refs):
        src, out, (send_sems, recv_sems) = refs[:nt], refs[nt:2 * nt], refs[2 * nt:]
        x, y, c, _ = _place()
        copy = lambda t, half: pltpu.make_async_remote_copy(
            src_ref=src[t].at[:, c], dst_ref=out[t].at[:, half], send_sem=send_sems.at[t], recv_sem=recv_sems.at[t],
            device_id=(x, y, 1 - c), device_id_type=MESH)
        sends = [copy(t, c) for t in range(nt)]
        for cp in sends:
            cp.start()
        for t in range(nt):
            copy(t, 1 - c).wait_recv()
        for cp in sends:
            cp.wait_send()

    return pl.pallas_call(
        body, name=name, in_specs=[ANY] * nt, out_specs=[ANY] * nt,
        out_shape=[jax.ShapeDtypeStruct(a.shape, a.dtype) for a in bufs],
        input_output_aliases={t: t for t in range(nt)}, scratch_shapes=_sems(nt),
    )(*bufs)


def _comm_allreduce_small(part, *, name):
    rows, cols = part.shape

    def body(p_ref, o_ref, buf, send_sems, recv_sems):
        x, y, c, _ = _place()
        me = 4 * x + 2 * y + c
        buf[me] = p_ref[...]
        flip = lambda v, bit: 1 - v if bit else v
        peers = [(flip(x, d & 4), flip(y, d & 2), flip(c, d & 1)) for d in range(1, 8)]
        sends = []
        for j, peer in enumerate(peers):
            cp = pltpu.make_async_remote_copy(src_ref=buf.at[me], dst_ref=buf.at[me], send_sem=send_sems.at[j],
                                              recv_sem=recv_sems.at[j], device_id=peer, device_id_type=MESH)
            cp.start()
            sends.append(cp)
        for j, (px, py, pc) in enumerate(peers):
            pltpu.make_async_remote_copy(src_ref=buf.at[me], dst_ref=buf.at[4 * px + 2 * py + pc], send_sem=send_sems.at[j],
                                         recv_sem=recv_sems.at[j], device_id=(px, py, pc), device_id_type=MESH).wait_recv()
        for cp in sends:
            cp.wait_send()
        total = buf[0]
        for i in range(1, 8):
            total = total + buf[i]
        o_ref[...] = total

    vm = pl.BlockSpec(memory_space=pltpu.VMEM)
    return pl.pallas_call(
        body, name=name, in_specs=[vm], out_specs=vm, out_shape=jax.ShapeDtypeStruct((rows, cols), F32),
        scratch_shapes=[pltpu.VMEM((8, rows, cols), F32), pltpu.SemaphoreType.DMA((7,)), pltpu.SemaphoreType.DMA((7,))],
    )(part)


def _pad_in_rows(wt):
    r = lambda o, n: wt[..., o:o + n, :]
    kr = r(2944, 64)
    return jnp.concatenate([r(0, 1024), r(1280, 1024), r(3008, 1024), r(2304, Q_RANK), r(1024, 128), r(2688, KV_RANK),
                            r(1152, 128), kr, jnp.zeros_like(kr)], axis=-2)


def _unpad_in_rows(qa, ga, gb, mixed):
    cq, ka, ckv, va, kr = (mixed[..., o - CQ_OFF:o - CQ_OFF + n, :] for o, n in (
        (CQ_OFF, Q_RANK), (KA_OFF, 128), (CKV_OFF, KV_RANK), (VA_OFF, 128), (KR_OFF, 64)))
    return jnp.concatenate([qa, ka, va, ga, cq, ckv, kr, gb], axis=-2)


def _pad_q_rows(wt):
    lead, cols = wt.shape[:-2], wt.shape[-1]
    wt = jnp.pad(wt.reshape(lead + (MLA_HEADS, 192, cols)), [(0, 0)] * (len(lead) + 1) + [(0, 64), (0, 0)])
    return wt.reshape(lead + (MLA_HEADS * 256, cols))


def _unpad_q_rows(wt):
    lead, cols = wt.shape[:-2], wt.shape[-1]
    return wt.reshape(lead + (MLA_HEADS, 256, cols))[..., :192, :].reshape(lead + (MLA_HEADS * 192, cols))


def _perm_kv_rows(wt):
    lead, cols = wt.shape[:-2], wt.shape[-1]
    return jnp.swapaxes(wt.reshape(lead + (MLA_HEADS, 2, 128, cols)), -4, -3).reshape(lead + (2048, cols))


def _unperm_kv_rows(wt):
    lead, cols = wt.shape[:-2], wt.shape[-1]
    return jnp.swapaxes(wt.reshape(lead + (2, MLA_HEADS, 128, cols)), -4, -3).reshape(lead + (2048, cols))


def _t(a):
    return jnp.swapaxes(a, -1, -2)


def _rope_tables(s):
    pos = jnp.arange(s, dtype=F32)
    inv_freq = 10000.0 ** (-jnp.arange(0, 64, 2, dtype=F32) / 64)
    ang = pos[:, None] * inv_freq[None, :]
    cos, sin = jnp.cos(ang), jnp.sin(ang)
    z64 = jnp.zeros((s, 64), F32)
    tk_c = jnp.concatenate([cos, cos, z64], axis=-1)
    tk_s = jnp.concatenate([-sin, sin, z64], axis=-1)
    return tk_c, tk_s


def _layer_weights(gathered, own, chip):
    def with_own_rows(g, o):
        n = o.shape[0]
        return jnp.concatenate([lax.select(chip == j, o, g[j * n:(j + 1) * n]) for j in range(4)], axis=0)

    full_in, full_q, full_kv, full_o = (with_own_rows(g, o) for g, o in zip(gathered, own))
    return _pad_in_rows(full_in), _pad_q_rows(full_q), _perm_kv_rows(full_kv), full_o


def _device_step(xs, tgt, attn_g, sinks, gq, gkv, final_g, shards, where):
    chip = where[0]
    depth = shards[0].shape[0]
    s = xs.shape[0]
    tabs = _rope_tables(s)
    saved = []
    x = xs
    of_layer = lambda l: [a[l] for a in shards]
    gathered = _comm_gather_layer(of_layer(0), name="comm_gather_layer0")
    weights = []
    for l in range(depth):
        w_in_p, w_q_p, w_kv_p, w_o = _layer_weights(gathered, of_layer(l), chip)
        weights.append((w_in_p, w_q_p, w_kv_p, w_o))
        h = _rmsnorm_fwd(x, attn_g[l:l + 1], name=f"norm_fwd{l}")
        proj = _matmul(h, w_in_p, tb=True, name=f"in_proj{l}")
        oa, lse_a = _swa_fwd(proj, sinks[l], name=f"swa_fwd{l}")
        qcat, kcat, v, cqn, ckvn = _mla_qkv_fwd(proj, gq[l:l + 1], gkv[l:l + 1], w_q_p, w_kv_p, *tabs, layer=None,
                                                name=f"mla_qkv_fwd{l}")
        ob, lse_b, gathered = _mla_fwd(qcat, kcat, v, gather=of_layer(l + 1) if l + 1 < depth else None,
                                       name=f"mla_fwd{l}")
        y = _gate_fwd(oa, ob, proj, name=f"gate_fwd{l}")
        x_next = _matmul(y, w_o, add=x, name=f"out_proj{l}")
        saved.append((x, h, proj, oa, lse_a, qcat, kcat, v, cqn, ckvn, ob, lse_b, y))
        x = x_next

    dx, d_final_g, loss = _final_loss(x, final_g, tgt, name="final_loss")

    d_attn_g, d_sinks, d_gq, d_gkv = [None] * depth, [None] * depth, [None] * depth, [None] * depth
    d_w_in, d_w_q, d_w_kv, d_w_o = [None] * depth, [None] * depth, [None] * depth, [None] * depth
    half = depth // 2
    carried = {half - 1: list(range(half, depth)), **{l - 1: [l] for l in range(1, half)}}
    done = {}

    def grads_by_chip(layers):
        stack = lambda per_layer: jnp.stack([per_layer[l] for l in layers])
        in_tiles = [stack([tiles[j] if tiles else None for tiles in d_w_in]) for j in range(4)]
        full = [_unpad_in_rows(*in_tiles), _unpad_q_rows(stack(d_w_q)), _unperm_kv_rows(stack(d_w_kv)), stack(d_w_o)]
        return [g.reshape(g.shape[0], 4, 2, g.shape[1] // 8, g.shape[2]) for g in full]

    def reduce_begin(grads5, tag):
        give = [lax.dynamic_index_in_dim(g, 1 - where[4], axis=2, keepdims=False).astype(WIRE_DTYPE) for g in grads5]
        recv = _comm_swap_sibling(give, name=f"comm_swap_sibling_{tag}")
        return recv, [_pair_sum(where, g, r, name=f"pair_sum_{tag}_{t}") for t, (g, r) in enumerate(zip(grads5, recv))]

    def reduce_end(grads5, recv, parts, tag):
        mine = [_chip_sum(where, g, r, p, name=f"chip_sum_{tag}_{t}") for t, (g, r, p) in enumerate(zip(grads5, recv, parts))]
        joined = _comm_join_halves(mine, name=f"comm_join_halves_{tag}")
        return [a.reshape(a.shape[0], -1, a.shape[-1]) for a in joined]

    for l in reversed(range(depth)):
        x, h, proj, oa, lse_a, qcat, kcat, v, cqn, ckvn, ob, lse_b, y = saved[l]
        w_in_p, w_q_p, w_kv_p, w_o = weights[l]
        dy = _matmul(dx, w_o, tb=True, name=f"out_proj_dx{l}")
        d_w_o[l] = _matmul(y, dx, ta=True, name=f"out_proj_dw{l}")
        doa, dob, dga, dgb, delta_b = _gate_bwd(dy, oa, ob, proj, name=f"gate_bwd{l}")
        dqa, dka, dva, dsink = _swa_bwd(proj, sinks[l], lse_a, doa, name=f"swa_bwd{l}")
        phase, pair = carried.get(l), None
        if phase:
            grads5 = grads_by_chip(phase)
            recv, pair = reduce_begin(grads5, f"l{phase[0]}")
        dqc, dkc, dv, parts = _mla_bwd(qcat, kcat, v, dob, lse_b, delta_b, scatter=pair, name=f"mla_bwd{l}")
        if phase:
            done[phase[0]] = reduce_end(grads5, recv, parts, f"l{phase[0]}")
        mixed, d_w_q[l], d_w_kv[l], dgq_l, dgkv_l = _mla_qkv_bwd(
            proj, cqn, ckvn, dqc, dkc, dv, dka, dva, gq[l:l + 1], gkv[l:l + 1], w_q_p, w_kv_p, *tabs, layer=None,
            name=f"mla_qkv_bwd{l}")
        dproj = [dqa, dga, dgb, mixed]
        dh = _matmul_ktiles(dproj, w_in_p, name=f"in_proj_dx{l}")
        d_w_in[l] = [_matmul(tile, h, ta=True, name=f"in_proj_dw{l}_{j}") for j, tile in enumerate(dproj)]
        dx, dg_l = _rmsnorm_bwd(dh, x, attn_g[l:l + 1], dx, name=f"norm_bwd{l}")
        d_attn_g[l], d_sinks[l], d_gq[l], d_gkv[l] = dg_l, dsink[0:1, :SWA_HEADS], dgq_l, dgkv_l

    grads5 = grads_by_chip([0])
    recv, pair = reduce_begin(grads5, "l0")
    done[0] = reduce_end(grads5, recv, _comm_scatter_chips(pair, name="comm_scatter_chips_l0"), "l0")
    cat = lambda parts: jnp.concatenate(parts, axis=0)
    reduced = [cat([done[first][t] for first in sorted(done)]) for t in range(len(shards))]
    return loss, dx, cat(d_attn_g), cat(d_sinks), cat(d_gq), cat(d_gkv), d_final_g, reduced


def kernel(x, attn_norm_g, w_in, swa_sinks, q_a_norm_g, kv_a_norm_g, w_q_b, w_kv_b, w_out, final_norm_g, loss_target, m_attn_norm_g, m_w_in, m_swa_sinks, m_q_a_norm_g, m_kv_a_norm_g, m_w_q_b, m_w_kv_b, m_w_out, m_final_norm_g, v_attn_norm_g, v_w_in, v_swa_sinks, v_q_a_norm_g, v_kv_a_norm_g, v_w_q_b, v_w_kv_b, v_w_out, v_final_norm_g):
    x_, y_, c = lax.axis_index("x"), lax.axis_index("y"), lax.axis_index("c")
    chip = 2 * x_ + y_
    where = jnp.stack([chip, 2 * (1 - x_) + y_, 2 * x_ + 1 - y_, 2 * (1 - x_) + 1 - y_, c]).astype(jnp.int32)

    sent = [a.astype(WIRE_DTYPE) for a in (_t(w_in), _t(w_q_b), _t(w_kv_b), w_out)]

    loss, dx, d_attn_g, d_sinks, d_gq, d_gkv, d_final_g, reduced = _device_step(
        x[0], loss_target[0], attn_norm_g, swa_sinks, q_a_norm_g, kv_a_norm_g, final_norm_g.reshape(1, -1), sent, where)
    g_w_in, g_w_q_b, g_w_kv_b, g_w_out = _t(reduced[0]), _t(reduced[1]), _t(reduced[2]), reduced[3]

    small = [d_attn_g, d_sinks, d_gq, d_gkv, d_final_g, loss[:, :1]]
    flat = jnp.concatenate([a.reshape(-1) for a in small])
    n_small = flat.shape[0]
    rows = -(-n_small // 1024) * 8
    total = _comm_allreduce_small(jnp.pad(flat, (0, rows * 128 - n_small)).reshape(rows, 128),
                                  name="comm_allreduce_small").reshape(-1)
    outs, at = [], 0
    for a in small:
        outs.append(total[at:at + a.size].reshape(a.shape))
        at += a.size
    g_attn_g, g_sinks, g_gq, g_gkv, g_final_g, loss_total = outs
    g_final_g = g_final_g.reshape(final_norm_g.shape)

    weights = [attn_norm_g, w_in, swa_sinks, q_a_norm_g, kv_a_norm_g, w_q_b, w_kv_b, w_out, final_norm_g]
    grads = [g_attn_g, g_w_in, g_sinks, g_gq, g_gkv, g_w_q_b, g_w_kv_b, g_w_out, g_final_g]
    ms = [m_attn_norm_g, m_w_in, m_swa_sinks, m_q_a_norm_g, m_kv_a_norm_g, m_w_q_b, m_w_kv_b, m_w_out, m_final_norm_g]
    vs = [v_attn_norm_g, v_w_in, v_swa_sinks, v_q_a_norm_g, v_kv_a_norm_g, v_w_q_b, v_w_kv_b, v_w_out, v_final_norm_g]
    as2d = lambda a: a.reshape(1, -1) if a.ndim == 1 else a
    deltas, new_m, new_v = [], [], []
    for i, (w, g, m, v) in enumerate(zip(weights, grads, ms, vs)):
        view = _t if w is w_in else as2d
        d, mn, vn = _adamw(view(w), reduced[0] if w is w_in else view(g), view(m), view(v), name=f"adamw{i}")
        back = _t if w is w_in else (lambda a: a.reshape(w.shape))
        deltas.append(back(d))
        new_m.append(back(mn))
        new_v.append(back(vn))

    return (loss_total.reshape(()), dx[None], *grads, *deltas, *new_m, *new_v)
```

```python
import functools
import math

import jax
import jax.numpy as jnp
from jax import lax
from jax.experimental import pallas as pl
from jax.experimental.pallas import tpu as pltpu

F32 = jnp.float32
MXU_DTYPE = jnp.bfloat16
WIRE_DTYPE = jnp.bfloat16

EPS = 1e-6
NEG = -1e30
BLOCK = 128
D_MODEL = 2048
SWA_HEADS = 16
MLA_HEADS = 8
Q_RANK = 384
KV_RANK = 256
IN_WIDTH = 4032
MLA_SCALE = 192 ** -0.5
SWA_SCALE = 64 ** -0.5
LOG2E = math.log2(math.e)
HEADS_PER_STEP = 2
SLOPES = tuple(2.0 ** (-8.0 * (h + 1) / SWA_HEADS) for h in range(SWA_HEADS))

P_WIDTH = 4096
QA_OFF, GA_OFF, GB_OFF, CQ_OFF, KA_OFF, CKV_OFF, VA_OFF, KR_OFF = 0, 1024, 2048, 3072, 3456, 3584, 3840, 3968

ADAM_LR, ADAM_B1, ADAM_B2, ADAM_EPS, ADAM_WD, ADAM_STEP = 0.001, 0.9, 0.999, 1e-08, 0.01, 10

VMEM_LIMIT = 56 * 1024 * 1024
MESH = pl.DeviceIdType.MESH
ANY = pl.BlockSpec(memory_space=pl.ANY)


def _cparams(*sem):
    return pltpu.CompilerParams(dimension_semantics=sem, vmem_limit_bytes=VMEM_LIMIT)


def _dot(a, b, ca, cb):
    return lax.dot_general(a, b, (((ca,), (cb,)), ((), ())), preferred_element_type=F32)


def _layer_spec(block, index_map, layer):
    if layer is None:
        return pl.BlockSpec(block, index_map)
    return pl.BlockSpec((None,) + tuple(block), lambda *g: (layer,) + tuple(index_map(*g)))


def _matmul(a, b, *, name, ta=False, tb=False, out_dtype=F32, add=None, b_layer=None, tm=1024, tn=1024, tk=2048):
    (kdim, m) = a.shape if ta else a.shape[::-1]
    (n, k2) = b.shape[-2:] if tb else b.shape[-2:][::-1]
    assert kdim == k2, (a.shape, b.shape)
    tm, tn, tk = min(tm, m), min(tn, n), min(tk, kdim)
    assert m % tm == 0 and n % tn == 0 and kdim % tk == 0
    nk = kdim // tk

    def body(*refs):
        a_ref, b_ref = refs[:2]
        add_ref = None if add is None else refs[2]
        o_ref = refs[2 + (add is not None)]
        part = _dot(a_ref[...].astype(MXU_DTYPE), b_ref[...].astype(MXU_DTYPE), 0 if ta else 1, 1 if tb else 0)

        def finish(r):
            o_ref[...] = (r if add is None else add_ref[...] + r).astype(out_dtype)

        if nk == 1:
            finish(part)
            return
        acc = refs[-1]
        k = pl.program_id(2)

        @pl.when(k == 0)
        def _():
            acc[...] = part

        @pl.when((k > 0) & (k < nk - 1))
        def _():
            acc[...] += part

        @pl.when(k == nk - 1)
        def _():
            finish(acc[...] + part)

    a_spec = pl.BlockSpec((tk, tm), lambda i, j, k: (k, i)) if ta else pl.BlockSpec((tm, tk), lambda i, j, k: (i, k))
    b_spec = (_layer_spec((tn, tk), lambda i, j, k: (j, k), b_layer) if tb else
              _layer_spec((tk, tn), lambda i, j, k: (k, j), b_layer))
    in_specs, args = [a_spec, b_spec], [a, b]
    if add is not None:
        in_specs.append(pl.BlockSpec((tm, tn), lambda i, j, k: (i, j)))
        args.append(add)
    return pl.pallas_call(
        body, name=name, grid=(m // tm, n // tn, nk), in_specs=in_specs,
        out_specs=pl.BlockSpec((tm, tn), lambda i, j, k: (i, j)),
        out_shape=jax.ShapeDtypeStruct((m, n), out_dtype),
        scratch_shapes=[pltpu.VMEM((tm, tn), F32)] if nk > 1 else [],
        compiler_params=_cparams("parallel", "parallel", "arbitrary"),
    )(*args)


def _matmul_ktiles(a_tiles, b, *, name, b_layer=None, tm=1024, tn=1024):
    m, kt = a_tiles[0].shape
    n = b.shape[-1]
    nt = len(a_tiles)
    assert b.shape[-2] == nt * kt
    tm, tn = min(tm, m), min(tn, n)
    assert m % tm == 0 and n % tn == 0

    def body(*refs):
        a_refs, b_refs, o_ref = refs[:nt], refs[nt:2 * nt], refs[2 * nt]
        acc = _dot(a_refs[0][...].astype(MXU_DTYPE), b_refs[0][...].astype(MXU_DTYPE), 1, 0)
        for j in range(1, nt):
            acc += _dot(a_refs[j][...].astype(MXU_DTYPE), b_refs[j][...].astype(MXU_DTYPE), 1, 0)
        o_ref[...] = acc

    in_specs = [pl.BlockSpec((tm, kt), lambda jn, i: (i, 0))] * nt
    in_specs += [_layer_spec((kt, tn), lambda jn, i, j=j: (j, jn), b_layer) for j in range(nt)]
    return pl.pallas_call(
        body, name=name, grid=(n // tn, m // tm), in_specs=in_specs,
        out_specs=pl.BlockSpec((tm, tn), lambda jn, i: (i, jn)),
        out_shape=jax.ShapeDtypeStruct((m, n), F32),
        compiler_params=_cparams("parallel", "parallel"),
    )(*a_tiles, *([b] * nt))


def _rmsnorm_fwd(x, g, *, name):
    s, d = x.shape
    tm = min(512, s)

    def body(x_ref, g_ref, h_ref):
        xv = x_ref[...]
        r = lax.rsqrt(jnp.mean(xv * xv, axis=-1, keepdims=True) + EPS)
        h_ref[...] = (xv * r * g_ref[...]).astype(MXU_DTYPE)

    return pl.pallas_call(
        body, name=name, grid=(s // tm,),
        in_specs=[pl.BlockSpec((tm, d), lambda i: (i, 0)), pl.BlockSpec((1, d), lambda i: (0, 0))],
        out_specs=pl.BlockSpec((tm, d), lambda i: (i, 0)),
        out_shape=jax.ShapeDtypeStruct((s, d), MXU_DTYPE),
        compiler_params=_cparams("parallel"),
    )(x, g)


def _rmsnorm_bwd(dh, x, g, dres, *, name):
    s, d = x.shape
    tm = min(512, s)

    def body(dh_ref, x_ref, g_ref, dres_ref, dx_ref, dg_ref):
        @pl.when(pl.program_id(0) == 0)
        def _():
            dg_ref[...] = jnp.zeros_like(dg_ref)

        xv = x_ref[...]
        r = lax.rsqrt(jnp.mean(xv * xv, axis=-1, keepdims=True) + EPS)
        xn = xv * r
        dy = dh_ref[...]
        dg_ref[...] += jnp.sum(dy * xn, axis=0, keepdims=True)
        u = dy * g_ref[...]
        dx_ref[...] = dres_ref[...] + r * (u - xn * jnp.mean(u * xn, axis=-1, keepdims=True))

    row = pl.BlockSpec((tm, d), lambda i: (i, 0))
    vec = pl.BlockSpec((1, d), lambda i: (0, 0))
    return pl.pallas_call(
        body, name=name, grid=(s // tm,), in_specs=[row, row, vec, row], out_specs=[row, vec],
        out_shape=[jax.ShapeDtypeStruct((s, d), F32), jax.ShapeDtypeStruct((1, d), F32)],
        compiler_params=_cparams("arbitrary"),
    )(dh, x, g, dres)


def _final_loss(x, g, tgt, *, name):
    s, d = x.shape
    tm = min(512, s)

    def body(x_ref, g_ref, t_ref, dx_ref, dg_ref, loss_ref):
        @pl.when(pl.program_id(0) == 0)
        def _():
            dg_ref[...] = jnp.zeros_like(dg_ref)
            loss_ref[...] = jnp.zeros_like(loss_ref)

        xv = x_ref[...]
        gv = g_ref[...]
        r = lax.rsqrt(jnp.mean(xv * xv, axis=-1, keepdims=True) + EPS)
        xn = xv * r
        err = xn * gv - t_ref[...]
        sq = jnp.sum(jnp.sum(err * err, axis=-1, keepdims=True), axis=0, keepdims=True)
        loss_ref[...] += (0.5 / d) * sq
        dy = err * (1.0 / d)
        dg_ref[...] += jnp.sum(dy * xn, axis=0, keepdims=True)
        u = dy * gv
        dx_ref[...] = r * (u - xn * jnp.mean(u * xn, axis=-1, keepdims=True))

    row = pl.BlockSpec((tm, d), lambda i: (i, 0))
    vec = pl.BlockSpec((1, d), lambda i: (0, 0))
    return pl.pallas_call(
        body, name=name, grid=(s // tm,), in_specs=[row, vec, row],
        out_specs=[row, vec, pl.BlockSpec((1, 128), lambda i: (0, 0))],
        out_shape=[jax.ShapeDtypeStruct((s, d), F32), jax.ShapeDtypeStruct((1, d), F32),
                   jax.ShapeDtypeStruct((1, 128), F32)],
        compiler_params=_cparams("arbitrary"),
    )(x, g, tgt)


def _swa_keys(kp_ref, kc_ref):
    kk = jnp.concatenate([kp_ref[...], kc_ref[...]], axis=0)
    kr = pltpu.roll(kk, 64, 1)
    lo = lax.broadcasted_iota(jnp.int32, kk.shape, 1) < 64
    return [jnp.where(lo, kk, kr).astype(MXU_DTYPE), jnp.where(lo, kr, kk).astype(MXU_DTYPE)]


GROUP = SWA_HEADS // 2


def _swa_mask(n):
    qi = lax.broadcasted_iota(jnp.int32, (BLOCK, 2 * BLOCK), 0)
    ki = lax.broadcasted_iota(jnp.int32, (BLOCK, 2 * BLOCK), 1)
    delta = BLOCK + qi - ki
    valid = (delta >= 0) & (delta < BLOCK) & ((ki >= BLOCK) | (n > 0))
    return valid, delta.astype(F32)


def _stack_heads(ref, j):
    lo = lax.broadcasted_iota(jnp.int32, (BLOCK, BLOCK), 1) < 64
    parts = []
    for r in range(GROUP):
        pair = (GROUP * j + r) // 2
        blk = ref[:, pair * 128:(pair + 1) * 128].astype(F32)
        parts.append(jnp.where(lo if r % 2 == 0 else ~lo, blk, 0.0).astype(MXU_DTYPE))
    return jnp.concatenate(parts, axis=0)


def _unstack_heads(stacked, ref, j):
    lo = lax.broadcasted_iota(jnp.int32, (BLOCK, BLOCK), 1) < 64
    for i in range(GROUP // 2):
        pair = (GROUP * j) // 2 + i
        even, odd = stacked[2 * i * BLOCK:(2 * i + 1) * BLOCK], stacked[(2 * i + 1) * BLOCK:(2 * i + 2) * BLOCK]
        ref[:, pair * 128:(pair + 1) * 128] = jnp.where(lo, even, odd).astype(ref.dtype)


def _head_rows(stacked, r):
    return stacked[r * BLOCK:(r + 1) * BLOCK]


def _swa_scores(raw, h, valid, deltaf):
    return jnp.where(valid, raw * (SWA_SCALE * LOG2E) - (SLOPES[h] * LOG2E) * deltaf, NEG)


def _swa_specs(nb):
    kcol, vcol = KA_OFF // BLOCK, VA_OFF // BLOCK
    last = nb - 1
    cur = lambda n: jnp.minimum(n, last)
    prev = lambda n: jnp.maximum(jnp.minimum(n, last) - 1, 0)
    return [
        pl.BlockSpec(memory_space=pltpu.SMEM),
        pl.BlockSpec((BLOCK, 1024), lambda n: (cur(n), QA_OFF // 1024)),
        pl.BlockSpec((BLOCK, BLOCK), lambda n: (cur(n), kcol)),
        pl.BlockSpec((BLOCK, BLOCK), lambda n: (prev(n), kcol)),
        pl.BlockSpec((BLOCK, BLOCK), lambda n: (cur(n), vcol)),
        pl.BlockSpec((BLOCK, BLOCK), lambda n: (prev(n), vcol)),
    ]


def _swa_fwd(proj, sinks, *, name):
    s = proj.shape[0]
    nb = s // BLOCK

    def body(sink_ref, q_ref, kc_ref, kp_ref, vc_ref, vp_ref, o_ref, lse_ref):
        n = pl.program_id(0)
        keys = _swa_keys(kp_ref, kc_ref)
        vals = _swa_keys(vp_ref, vc_ref)
        valid, deltaf = _swa_mask(n)
        lane = lax.broadcasted_iota(jnp.int32, (BLOCK, BLOCK), 1)
        lse_acc = jnp.zeros((BLOCK, BLOCK), F32)
        for j in range(2):
            raw = _dot(_stack_heads(q_ref, j), keys[j], 1, 1)
            probs = []
            for r in range(GROUP):
                h = GROUP * j + r
                sc = _swa_scores(_head_rows(raw, r), h, valid, deltaf)
                sink = sink_ref[h] * LOG2E
                m = jnp.maximum(jnp.max(sc, axis=-1, keepdims=True), sink)
                p = jnp.exp2(sc - m)
                l = jnp.sum(p, axis=-1, keepdims=True) + jnp.exp2(sink - m)
                probs.append((p * (1.0 / l)).astype(MXU_DTYPE))
                lse_acc = jnp.where(lane == h, m + jnp.log(l) * LOG2E, lse_acc)
            _unstack_heads(jnp.dot(jnp.concatenate(probs, axis=0), vals[j], preferred_element_type=F32), o_ref, j)
        lse_ref[...] = lse_acc

    return pl.pallas_call(
        body, name=name, grid=(nb,), in_specs=_swa_specs(nb),
        out_specs=[pl.BlockSpec((BLOCK, 1024), lambda n: (n, 0)), pl.BlockSpec((BLOCK, BLOCK), lambda n: (n, 0))],
        out_shape=[jax.ShapeDtypeStruct((s, 1024), F32), jax.ShapeDtypeStruct((s, BLOCK), F32)],
        compiler_params=_cparams("parallel"),
    )(sinks, proj, proj, proj, proj, proj)


def _swa_bwd(proj, sinks, lse, do, *, name):
    s = proj.shape[0]
    nb = s // BLOCK
    last = nb - 1

    def body(sink_ref, q_ref, kc_ref, kp_ref, vc_ref, vp_ref, lse_ref, do_ref,
             dq_ref, dk_ref, dv_ref, dsink_ref, carry_k, carry_v):
        n = pl.program_id(0)

        @pl.when(n == 0)
        def _():
            carry_k[...] = jnp.zeros_like(carry_k)
            carry_v[...] = jnp.zeros_like(carry_v)
            dsink_ref[...] = jnp.zeros_like(dsink_ref)

        @pl.when(n < nb)
        def _():
            keys = _swa_keys(kp_ref, kc_ref)
            vals = _swa_keys(vp_ref, vc_ref)
            valid, deltaf = _swa_mask(n)
            lane = lax.broadcasted_iota(jnp.int32, (BLOCK, BLOCK), 1)
            lane1 = lax.broadcasted_iota(jnp.int32, (1, BLOCK), 1)
            lse_blk = lse_ref[...]
            acc_k, acc_v = [], []
            dsink = jnp.zeros((1, BLOCK), F32)
            for j in range(2):
                q_all, do_all = _stack_heads(q_ref, j), _stack_heads(do_ref, j)
                raw = _dot(q_all, keys[j], 1, 1)
                dp_all = _dot(do_all, vals[j], 1, 1)
                probs, dscores = [], []
                for r in range(GROUP):
                    h = GROUP * j + r
                    lse_h = jnp.sum(jnp.where(lane == h, lse_blk, 0.0), axis=-1, keepdims=True)
                    p = jnp.exp2(_swa_scores(_head_rows(raw, r), h, valid, deltaf) - lse_h)
                    dp = _head_rows(dp_all, r)
                    dlt = jnp.sum(dp * p, axis=-1, keepdims=True)
                    dscores.append((p * (dp - dlt) * SWA_SCALE).astype(MXU_DTYPE))
                    probs.append(p.astype(MXU_DTYPE))
                    sunk = jnp.exp2(sink_ref[h] * LOG2E - lse_h) * dlt
                    dsink = jnp.where(lane1 == h, -jnp.sum(sunk, axis=0, keepdims=True), dsink)
                ds_all = jnp.concatenate(dscores, axis=0)
                _unstack_heads(jnp.dot(ds_all, keys[j], preferred_element_type=F32), dq_ref, j)
                acc_k.append(_dot(ds_all, q_all, 0, 0))
                acc_v.append(_dot(jnp.concatenate(probs, axis=0), do_all, 0, 0))
            lo2 = lax.broadcasted_iota(jnp.int32, (2 * BLOCK, BLOCK), 1) < 64
            fold = lambda acc: jnp.where(lo2, acc[0] + pltpu.roll(acc[0], 64, 1), acc[1] + pltpu.roll(acc[1], 64, 1))
            dkk, dvv = fold(acc_k), fold(acc_v)
            dk_ref[...] = (carry_k[...] + dkk[:BLOCK]).astype(dk_ref.dtype)
            dv_ref[...] = (carry_v[...] + dvv[:BLOCK]).astype(dv_ref.dtype)
            carry_k[...] = dkk[BLOCK:]
            carry_v[...] = dvv[BLOCK:]
            dsink_ref[...] += jnp.broadcast_to(dsink, dsink_ref.shape)

        @pl.when(n == nb)
        def _():
            dk_ref[...] = carry_k[...].astype(dk_ref.dtype)
            dv_ref[...] = carry_v[...].astype(dv_ref.dtype)

    cur = lambda n: jnp.minimum(n, last)
    lag = lambda n: jnp.maximum(n - 1, 0)
    return pl.pallas_call(
        body, name=name, grid=(nb + 1,),
        in_specs=_swa_specs(nb) + [pl.BlockSpec((BLOCK, BLOCK), lambda n: (cur(n), 0)),
                                   pl.BlockSpec((BLOCK, 1024), lambda n: (cur(n), 0))],
        out_specs=[pl.BlockSpec((BLOCK, 1024), lambda n: (cur(n), 0)),
                   pl.BlockSpec((BLOCK, BLOCK), lambda n: (lag(n), 0)),
                   pl.BlockSpec((BLOCK, BLOCK), lambda n: (lag(n), 0)),
                   pl.BlockSpec((8, BLOCK), lambda n: (0, 0))],
        out_shape=[jax.ShapeDtypeStruct((s, 1024), MXU_DTYPE), jax.ShapeDtypeStruct((s, BLOCK), MXU_DTYPE),
                   jax.ShapeDtypeStruct((s, BLOCK), MXU_DTYPE), jax.ShapeDtypeStruct((8, BLOCK), F32)],
        scratch_shapes=[pltpu.VMEM((BLOCK, BLOCK), F32), pltpu.VMEM((BLOCK, BLOCK), F32)],
        compiler_params=_cparams("arbitrary"),
    )(sinks, proj, proj, proj, proj, proj, lse, do)


def _rope_partner(v, first, width):
    lane = lax.broadcasted_iota(jnp.int32, v.shape, 1)
    in_a = (lane >= first) & (lane < first + 32)
    in_b = (lane >= first + 32) & (lane < first + 64)
    return jnp.where(in_a, pltpu.roll(v, width - 32, 1), jnp.where(in_b, pltpu.roll(v, 32, 1), 0.0))


def _mla_qkv_fwd(proj, gq, gkv, wq, wkv, tk_c, tk_s, *, layer, name):
    s = proj.shape[0]
    tm = min(512, s)

    def body(cq_ref, ckv_ref, kr_ref, gq_ref, gkv_ref, wq_ref, wkv_ref, kc_ref, ks_ref,
             qcat_ref, kcat_ref, v_ref, cqn_ref, ckvn_ref):
        cq = cq_ref[...]
        cqn = (cq * lax.rsqrt(jnp.mean(cq * cq, axis=-1, keepdims=True) + EPS) * gq_ref[...]).astype(MXU_DTYPE)
        cqn_ref[...] = cqn
        qpre = _dot(cqn, wq_ref[...], 1, 1)
        kc, ks = kc_ref[...], ks_ref[...]
        for hh in range(MLA_HEADS):
            qcat_ref[:, hh * 256:hh * 256 + 128] = qpre[:, hh * 256:hh * 256 + 128].astype(MXU_DTYPE)
            blk = qpre[:, hh * 256 + 128:(hh + 1) * 256]
            qcat_ref[:, hh * 256 + 128:(hh + 1) * 256] = (blk * kc + _rope_partner(blk, 0, 128) * ks).astype(MXU_DTYPE)
        ckv = ckv_ref[...]
        ckvn = (ckv * lax.rsqrt(jnp.mean(ckv * ckv, axis=-1, keepdims=True) + EPS) * gkv_ref[...]).astype(MXU_DTYPE)
        ckvn_ref[...] = ckvn
        kv = _dot(ckvn, wkv_ref[...], 1, 1)
        kr = kr_ref[...]
        krr = (kr * kc + _rope_partner(kr, 0, 128) * ks).astype(MXU_DTYPE)
        for hh in range(MLA_HEADS):
            kcat_ref[:, hh * 256:hh * 256 + 128] = kv[:, hh * 128:(hh + 1) * 128].astype(MXU_DTYPE)
            kcat_ref[:, hh * 256 + 128:(hh + 1) * 256] = krr
            v_ref[:, hh * 256:hh * 256 + 128] = kv[:, 1024 + hh * 128:1024 + (hh + 1) * 128].astype(MXU_DTYPE)
            v_ref[:, hh * 256 + 128:(hh + 1) * 256] = jnp.ones((tm, 128), MXU_DTYPE)

    row = lambda w, c: pl.BlockSpec((tm, w), lambda i: (i, c))
    full = lambda a: pl.BlockSpec(a.shape, lambda i: (0, 0))
    of_layer = lambda a: _layer_spec(a.shape[-2:], lambda i: (0, 0), layer)
    return pl.pallas_call(
        body, name=name, grid=(s // tm,),
        in_specs=[row(Q_RANK, CQ_OFF // Q_RANK), row(KV_RANK, CKV_OFF // KV_RANK), row(128, KR_OFF // 128),
                  full(gq), full(gkv), of_layer(wq), of_layer(wkv), row(128, 0), row(128, 0)],
        out_specs=[row(2048, 0), row(2048, 0), row(2048, 0), row(Q_RANK, 0), row(KV_RANK, 0)],
        out_shape=[jax.ShapeDtypeStruct((s, 2048), MXU_DTYPE), jax.ShapeDtypeStruct((s, 2048), MXU_DTYPE),
                   jax.ShapeDtypeStruct((s, 2048), MXU_DTYPE), jax.ShapeDtypeStruct((s, Q_RANK), MXU_DTYPE),
                   jax.ShapeDtypeStruct((s, KV_RANK), MXU_DTYPE)],
        compiler_params=_cparams("parallel"),
    )(proj, proj, proj, gq, gkv, wq, wkv, tk_c, tk_s)


def _norm_bwd(x, g, dy):
    r = lax.rsqrt(jnp.mean(x * x, axis=-1, keepdims=True) + EPS)
    xn = x * r
    u = dy * g
    return r * (u - xn * jnp.mean(u * xn, axis=-1, keepdims=True)), jnp.sum(dy * xn, axis=0, keepdims=True)


def _mla_qkv_bwd(proj, cqn, ckvn, dqcat, dkcat, dv, dka, dva, gq, gkv, wq, wkv, tk_c, tk_s, *, layer, name):
    s = proj.shape[0]
    tm = min(512, s)
    t_cq, t_ka, t_ckv, t_va, t_kr = (o - CQ_OFF for o in (CQ_OFF, KA_OFF, CKV_OFF, VA_OFF, KR_OFF))

    def body(cq_ref, ckv_ref, cqn_ref, ckvn_ref, dq_ref, dk_ref, dv_ref, dka_ref, dva_ref, gq_ref, gkv_ref, wq_ref,
             wkv_ref, kc_ref, ks_ref,
             tile_ref, dwq_ref, dwkv_ref, dgq_ref, dgkv_ref, dqpre, dkv, dwq_acc, dwkv_acc):
        dcq_ref = tile_ref.at[:, t_cq:t_cq + Q_RANK]
        dckv_ref = tile_ref.at[:, t_ckv:t_ckv + KV_RANK]
        dkr_ref = tile_ref.at[:, t_kr:t_kr + 128]
        tile_ref[:, t_ka:t_ka + 128] = dka_ref[...]
        tile_ref[:, t_va:t_va + 128] = dva_ref[...]

        @pl.when(pl.program_id(0) == 0)
        def _():
            for r in (dwq_acc, dwkv_acc, dgq_ref, dgkv_ref):
                r[...] = jnp.zeros_like(r)

        kc, ks = kc_ref[...], ks_ref[...]
        dkrr = jnp.zeros((tm, 128), F32)
        for hh in range(MLA_HEADS):
            dqpre[:, hh * 256:hh * 256 + 128] = dq_ref[:, hh * 256:hh * 256 + 128].astype(MXU_DTYPE)
            blk = dq_ref[:, hh * 256 + 128:(hh + 1) * 256]
            dqpre[:, hh * 256 + 128:(hh + 1) * 256] = (blk * kc + _rope_partner(blk * ks, 0, 128)).astype(MXU_DTYPE)
            dkv[:, hh * 128:(hh + 1) * 128] = dk_ref[:, hh * 256:hh * 256 + 128].astype(MXU_DTYPE)
            dkrr = dkrr + dk_ref[:, hh * 256 + 128:(hh + 1) * 256]
        dkv[:, 1024:] = dv_ref[...].astype(MXU_DTYPE)
        dkr_ref[...] = (dkrr * kc + _rope_partner(dkrr * ks, 0, 128)).astype(dkr_ref.dtype)

        dq_b = dqpre[...]
        dwq_acc[...] += _dot(cqn_ref[...], dq_b, 0, 0)
        dcq, dgq = _norm_bwd(cq_ref[...], gq_ref[...], _dot(dq_b, wq_ref[...], 1, 0))
        dcq_ref[...] = dcq.astype(dcq_ref.dtype)
        dgq_ref[...] += dgq

        dkv_b = dkv[...]
        dwkv_acc[...] += _dot(ckvn_ref[...], dkv_b, 0, 0)
        dckv, dgkv = _norm_bwd(ckv_ref[...], gkv_ref[...], _dot(dkv_b, wkv_ref[...], 1, 0))
        dckv_ref[...] = dckv.astype(dckv_ref.dtype)
        dgkv_ref[...] += dgkv

        @pl.when(pl.program_id(0) == s // tm - 1)
        def _():
            dwq_ref[...] = dwq_acc[...].T
            dwkv_ref[...] = dwkv_acc[...].T

    row = lambda w, c: pl.BlockSpec((tm, w), lambda i: (i, c))
    full = lambda shape: pl.BlockSpec(shape, lambda i: (0, 0))
    of_layer = lambda a: _layer_spec(a.shape[-2:], lambda i: (0, 0), layer)
    return pl.pallas_call(
        body, name=name, grid=(s // tm,),
        in_specs=[row(Q_RANK, CQ_OFF // Q_RANK), row(KV_RANK, CKV_OFF // KV_RANK), row(Q_RANK, 0), row(KV_RANK, 0),
                  row(2048, 0), row(2048, 0), row(1024, 0), row(128, 0), row(128, 0), full(gq.shape), full(gkv.shape),
                  of_layer(wq), of_layer(wkv), row(128, 0), row(128, 0)],
        out_specs=[row(1024, 0), full(wq.shape[-2:]), full(wkv.shape[-2:]), full(gq.shape), full(gkv.shape)],
        out_shape=[jax.ShapeDtypeStruct((s, 1024), MXU_DTYPE), jax.ShapeDtypeStruct(wq.shape[-2:], F32),
                   jax.ShapeDtypeStruct(wkv.shape[-2:], F32), jax.ShapeDtypeStruct(gq.shape, F32),
                   jax.ShapeDtypeStruct(gkv.shape, F32)],
        scratch_shapes=[pltpu.VMEM((tm, 2048), MXU_DTYPE), pltpu.VMEM((tm, 2048), MXU_DTYPE),
                        pltpu.VMEM((Q_RANK, 2048), F32), pltpu.VMEM((KV_RANK, 2048), F32)],
        compiler_params=_cparams("arbitrary"),
    )(proj, proj, cqn, ckvn, dqcat, dkcat, dv, dka, dva, gq, gkv, wq, wkv, tk_c, tk_s)


def _loop_by_two(lo, hi, step):
    n = hi - lo

    def four(i, carry):
        for u in range(4):
            step(lo + 4 * i + u)
        return carry

    lax.fori_loop(0, n // 4, four, 0)
    rest = lo + (n // 4) * 4

    @pl.when(n % 4 >= 2)
    def _():
        step(rest)
        step(rest + 1)

    @pl.when(n % 2 == 1)
    def _():
        step(hi - 1)


def _causal_mask(t):
    return lax.broadcasted_iota(jnp.int32, (t, t), 1) <= lax.broadcasted_iota(jnp.int32, (t, t), 0)


def _mla_fwd(qcat, kcat, v, *, name, gather=None):
    s = qcat.shape[0]
    t = min(512, s)
    nq = s // t
    hp = HEADS_PER_STEP
    ng = MLA_HEADS // hp
    c2 = MLA_SCALE * LOG2E

    nw = 0 if gather is None else len(gather)

    def body(q_ref, k_ref, v_ref, *rest):
        src, (o_ref, lse_ref), out = rest[:nw], rest[nw:nw + 2], rest[nw + 2:2 * nw + 2]
        top_s, acc_s, *sems = rest[2 * nw + 2:]
        g, qi = pl.program_id(0), pl.program_id(1)
        if nw:
            @pl.when((g == 0) & (qi == 0))
            def _():
                _gather_start(src, out, *sems)

            @pl.when((g == ng - 1) & (qi == 0))
            def _():
                _gather_forward(src, out, *sems)

        rows = lambda j: pl.ds(pl.multiple_of(j * t, t), t)
        head = lambda e: slice(e * 256, (e + 1) * 256)
        raw = lambda e, j: _dot(q_ref[:, head(e)], k_ref[rows(j), head(e)], 1, 1)

        for e in range(hp):
            top_s[e] = jnp.where(_causal_mask(t), raw(e, qi), NEG)

        def pass1(j):
            for e in range(hp):
                top_s[e] = jnp.maximum(top_s[e], raw(e, j))

        _loop_by_two(0, qi, pass1)
        m = [jnp.max(top_s[e], axis=-1, keepdims=True) * c2 for e in range(hp)]

        def weighted(e, j, masked):
            sc = raw(e, j) * c2 - m[e]
            if masked:
                sc = jnp.where(_causal_mask(t), sc, NEG)
            return jnp.dot(jnp.exp2(sc).astype(MXU_DTYPE), v_ref[rows(j), head(e)], preferred_element_type=F32)

        for e in range(hp):
            acc_s[e] = weighted(e, qi, True)

        def pass2(j):
            for e in range(hp):
                acc_s[e] += weighted(e, j, False)

        _loop_by_two(0, qi, pass2)
        lane = lax.broadcasted_iota(jnp.int32, (t, 128), 1)
        stats = jnp.zeros((t, 128), F32)
        for e in range(hp):
            l = acc_s[e, :, 128:]
            o_ref[:, e * 128:(e + 1) * 128] = acc_s[e, :, :128] / l
            stats = jnp.where(lane == e, m[e] + jnp.log(l) * LOG2E, stats)
        lse_ref[...] = stats
        if nw:
            @pl.when((g == ng - 1) & (qi == nq - 1))
            def _():
                _gather_finish(src, out, *sems)

    outs = pl.pallas_call(
        body, name=name, grid=(ng, nq),
        in_specs=[pl.BlockSpec((t, 256 * hp), lambda g, qi: (qi, g)), pl.BlockSpec((s, 256 * hp), lambda g, qi: (0, g)),
                  pl.BlockSpec((s, 256 * hp), lambda g, qi: (0, g))] + [ANY] * nw,
        out_specs=[pl.BlockSpec((t, 128 * hp), lambda g, qi: (qi, g)), pl.BlockSpec((t, 128), lambda g, qi: (qi, g))]
        + [ANY] * nw,
        out_shape=[jax.ShapeDtypeStruct((s, 1024), F32), jax.ShapeDtypeStruct((s, 128 * ng), F32)]
        + (_gathered_shapes(gather) if nw else []),
        scratch_shapes=[pltpu.VMEM((hp, t, t), F32), pltpu.VMEM((hp, t, 256), F32)] + (_sems(6, nw) if nw else []),
        compiler_params=_cparams("arbitrary", "arbitrary"),
    )(qcat, kcat, v, *(gather or []))
    return outs[0], outs[1], list(outs[2:])


def _mla_bwd(qcat, kcat, v, do, lse, delta, *, name, scatter=None):
    s = qcat.shape[0]
    t = min(512, s)
    nq = s // t
    hp = HEADS_PER_STEP
    c2 = MLA_SCALE * LOG2E
    nw = 0 if scatter is None else len(scatter)

    def body(q_ref, k_ref, v_ref, do_ref, lse_ref, dl_ref, *rest):
        src, (dq_ref, dk_ref, dv_ref), out = rest[:nw], rest[nw:nw + 3], rest[nw + 3:2 * nw + 3]
        dk_acc, dv_acc, *sems = rest[2 * nw + 3:]
        h, ki = pl.program_id(0), pl.program_id(1)
        if nw:
            @pl.when((h == 0) & (ki == 0))
            def _():
                _scatter_start(src, out, *sems)

        @pl.when(ki == 0)
        def _():
            dq_ref[...] = jnp.zeros_like(dq_ref)

        dk_acc[...] = jnp.zeros_like(dk_acc)
        dv_acc[...] = jnp.zeros_like(dv_acc)
        k, vv = k_ref[...], v_ref[...]
        mine = lax.broadcasted_iota(jnp.int32, (t, 128), 1) == h % hp

        def chunk(qi, masked):
            rows = pl.ds(pl.multiple_of(qi * t, t), t)
            q, dob = q_ref[rows, :], do_ref[rows, :]
            pick = lambda r: jnp.sum(jnp.where(mine, r[rows, :], 0.0), axis=-1, keepdims=True)
            sc = _dot(q, k, 1, 1) * c2
            if masked:
                sc = jnp.where(_causal_mask(t), sc, NEG)
            p = jnp.exp2(sc - pick(lse_ref))
            dp = _dot(dob, vv, 1, 1)
            ds = (p * (dp - pick(dl_ref)) * MLA_SCALE).astype(MXU_DTYPE)
            dv_acc[...] += _dot(dob, p.astype(MXU_DTYPE), 0, 0)
            dk_acc[...] += _dot(q, ds, 0, 0)
            dq_ref[rows, :] += jnp.dot(ds, k, preferred_element_type=F32)

        chunk(ki, True)
        _loop_by_two(ki + 1, nq, lambda qi: chunk(qi, False))
        dk_ref[...] = dk_acc[...].T
        dv_ref[...] = dv_acc[...].T
        if nw:
            @pl.when((h == MLA_HEADS - 1) & (ki == nq - 1))
            def _():
                _scatter_finish(src, out, *sems)

    head = lambda w: pl.BlockSpec((s, w), lambda h, ki: (0, h))
    blk = lambda w: pl.BlockSpec((t, w), lambda h, ki: (ki, h))
    stat = pl.BlockSpec((s, 128), lambda h, ki: (0, h // hp))
    outs = pl.pallas_call(
        body, name=name, grid=(MLA_HEADS, nq),
        in_specs=[head(256), blk(256), pl.BlockSpec((t, 128), lambda h, ki: (ki, 2 * h)), head(128), stat, stat]
        + [ANY] * nw,
        out_specs=[head(256), blk(256), blk(128)] + [ANY] * nw,
        out_shape=[jax.ShapeDtypeStruct((s, 2048), F32), jax.ShapeDtypeStruct((s, 2048), F32),
                   jax.ShapeDtypeStruct((s, 1024), F32)] + [jax.ShapeDtypeStruct(a.shape, a.dtype) for a in scatter or []],
        scratch_shapes=[pltpu.VMEM((256, t), F32), pltpu.VMEM((128, t), F32)] + (_sems(3, nw) if nw else []),
        compiler_params=_cparams("arbitrary", "arbitrary"),
    )(qcat, kcat, v, do, lse, delta, *(scatter or []))
    return outs[0], outs[1], outs[2], list(outs[3:])


def _gate_specs(tm):
    half = lambda c: pl.BlockSpec((tm, 1024), lambda i: (i, c))
    return half(0), half(GA_OFF // 1024), half(GB_OFF // 1024)


def _gate_fwd(oa, ob, proj, *, name):
    s = oa.shape[0]
    tm = min(512, s)

    def body(oa_ref, ob_ref, ga_ref, gb_ref, y_ref):
        ga, gb = ga_ref[...], gb_ref[...]
        y_ref[:, :1024] = (oa_ref[...] * (ga * jax.nn.sigmoid(ga))).astype(MXU_DTYPE)
        y_ref[:, 1024:] = (ob_ref[...] * (gb * jax.nn.sigmoid(gb))).astype(MXU_DTYPE)

    o_spec, ga_spec, gb_spec = _gate_specs(tm)
    return pl.pallas_call(
        body, name=name, grid=(s // tm,), in_specs=[o_spec, o_spec, ga_spec, gb_spec],
        out_specs=pl.BlockSpec((tm, 2048), lambda i: (i, 0)),
        out_shape=jax.ShapeDtypeStruct((s, 2048), MXU_DTYPE),
        compiler_params=_cparams("parallel"),
    )(oa, ob, proj, proj)


def _gate_bwd(dy, oa, ob, proj, *, name):
    s = oa.shape[0]
    tm = min(512, s)

    def body(dy_ref, oa_ref, ob_ref, ga_ref, gb_ref, doa_ref, dob_ref, dga_ref, dgb_ref, dl_ref):
        def branch(dyv, o, g, do_ref, dg_ref):
            sg = jax.nn.sigmoid(g)
            do = dyv * (g * sg)
            do_ref[...] = do.astype(MXU_DTYPE)
            dg_ref[...] = (dyv * o * (sg * (1.0 + g * (1.0 - sg)))).astype(MXU_DTYPE)
            return do

        branch(dy_ref[:, :1024], oa_ref[...], ga_ref[...], doa_ref, dga_ref)
        ob = ob_ref[...]
        prod = branch(dy_ref[:, 1024:], ob, gb_ref[...], dob_ref, dgb_ref) * ob
        lane = lax.broadcasted_iota(jnp.int32, (tm, stat_w), 1)
        acc = jnp.zeros((tm, stat_w), F32)
        for hh in range(MLA_HEADS):
            at = (hh // HEADS_PER_STEP) * 128 + hh % HEADS_PER_STEP
            acc = jnp.where(lane == at, jnp.sum(prod[:, hh * 128:(hh + 1) * 128], axis=-1, keepdims=True), acc)
        dl_ref[...] = acc

    stat_w = 128 * (MLA_HEADS // HEADS_PER_STEP)
    o_spec, ga_spec, gb_spec = _gate_specs(tm)
    return pl.pallas_call(
        body, name=name, grid=(s // tm,),
        in_specs=[pl.BlockSpec((tm, 2048), lambda i: (i, 0)), o_spec, o_spec, ga_spec, gb_spec],
        out_specs=[o_spec, o_spec, o_spec, o_spec, pl.BlockSpec((tm, stat_w), lambda i: (i, 0))],
        out_shape=[jax.ShapeDtypeStruct((s, 1024), MXU_DTYPE)] * 4 + [jax.ShapeDtypeStruct((s, stat_w), F32)],
        compiler_params=_cparams("parallel"),
    )(dy, oa, ob, proj, proj)


def _row_block(rows, cols, itemsize=4, budget=2 << 20):
    fits = [tr for tr in range(16, rows + 1, 16) if rows % tr == 0 and tr * cols * itemsize <= budget]
    return fits[-1] if fits else rows


def _adamw(w, g, m, v, *, name):
    shape = w.shape
    rows, cols = shape[-2:]
    w3, g3, m3, v3 = (a.reshape((-1, rows, cols)) for a in (w, g, m, v))
    lead = w3.shape[0]
    tr = _row_block(rows, cols)

    def body(w_ref, g_ref, m_ref, v_ref, d_ref, mo_ref, vo_ref):
        gv = g_ref[...]
        mn = ADAM_B1 * m_ref[...] + (1.0 - ADAM_B1) * gv
        vn = ADAM_B2 * v_ref[...] + (1.0 - ADAM_B2) * jnp.square(gv)
        m_hat = mn / (1.0 - ADAM_B1 ** ADAM_STEP)
        v_hat = vn / (1.0 - ADAM_B2 ** ADAM_STEP)
        d_ref[...] = -ADAM_LR * (m_hat / (jnp.sqrt(v_hat) + ADAM_EPS) + ADAM_WD * w_ref[...])
        mo_ref[...] = mn
        vo_ref[...] = vn

    spec = pl.BlockSpec((None, tr, cols), lambda a, i: (a, i, 0))
    outs = pl.pallas_call(
        body, name=name, grid=(lead, rows // tr), in_specs=[spec] * 4, out_specs=[spec] * 3,
        out_shape=[jax.ShapeDtypeStruct((lead, rows, cols), F32)] * 3,
        compiler_params=_cparams("parallel", "parallel"),
    )(w3, g3, m3, v3)
    return tuple(o.reshape(shape) for o in outs)


def _pair_sum(where, grads, recv, *, name):
    layers, chips, _, rows, cols = grads.shape
    tr = _row_block(rows, cols)

    def body(where_ref, a_ref, b_ref, o_ref):
        o_ref[...] = (a_ref[...] + b_ref[...].astype(F32)).astype(WIRE_DTYPE)

    spec = pl.BlockSpec((None, None, tr, cols), lambda a, k, i, w: (a, k, i, 0))
    return pl.pallas_call(
        body, name=name,
        grid_spec=pltpu.PrefetchScalarGridSpec(
            num_scalar_prefetch=1, grid=(layers, chips, rows // tr),
            in_specs=[pl.BlockSpec((None, None, None, tr, cols), lambda a, k, i, w: (a, k, w[4], i, 0)), spec],
            out_specs=spec),
        out_shape=jax.ShapeDtypeStruct((layers, chips, rows, cols), WIRE_DTYPE),
        compiler_params=_cparams("parallel", "parallel", "parallel"),
    )(where, grads, recv)


def _chip_sum(where, grads, recv, parts, *, name):
    layers, _, _, rows, cols = grads.shape
    tr = _row_block(rows, cols)

    def body(where_ref, a_ref, b_ref, t0_ref, t1_ref, t2_ref, o_ref):
        total = a_ref[...] + b_ref[...].astype(F32)
        for t_ref in (t0_ref, t1_ref, t2_ref):
            total = total + t_ref[...].astype(F32)
        o_ref[...] = total

    slot = lambda j: pl.BlockSpec((None, None, tr, cols), lambda a, i, w: (a, w[j], i, 0))
    return pl.pallas_call(
        body, name=name,
        grid_spec=pltpu.PrefetchScalarGridSpec(
            num_scalar_prefetch=1, grid=(layers, rows // tr),
            in_specs=[pl.BlockSpec((None, None, None, tr, cols), lambda a, i, w: (a, w[0], w[4], i, 0)), slot(0), slot(1),
                      slot(2), slot(3)],
            out_specs=slot(4)),
        out_shape=jax.ShapeDtypeStruct((layers, 2, rows, cols), F32),
        compiler_params=_cparams("parallel", "parallel"),
    )(where, grads, recv, parts, parts, parts)


def _place():
    x, y, c = lax.axis_index("x"), lax.axis_index("y"), lax.axis_index("c")
    chips = [(1 - x, y), (x, 1 - y), (1 - x, 1 - y)]
    return x, y, c, chips


def _sems(*shape):
    return [pltpu.SemaphoreType.DMA(shape), pltpu.SemaphoreType.DMA(shape)]


OWNER_CORE = (0, 1, 1, 1)


def _gather_copy(src, out, send_sems, recv_sems, t, sem, slot, to, forward=False):
    n = src[t].shape[0]
    rows = out[t].at[pl.ds(pl.multiple_of(slot * n, 16), n)]
    return pltpu.make_async_remote_copy(src_ref=rows if forward else src[t], dst_ref=rows, send_sem=send_sems.at[sem, t],
                                        recv_sem=recv_sems.at[sem, t], device_id=to, device_id_type=MESH)


def _gather_start(src, out, send_sems, recv_sems):
    x, y, c, chips = _place()
    for t, owner in enumerate(OWNER_CORE):
        @pl.when(c == owner)
        def _():
            for j, chip in enumerate(chips):
                _gather_copy(src, out, send_sems, recv_sems, t, j, 2 * x + y, (*chip, c)).start()


def _gather_forward(src, out, send_sems, recv_sems):
    x, y, c, chips = _place()
    for t, owner in enumerate(OWNER_CORE):
        @pl.when(c == owner)
        def _():
            for j, (px, py) in enumerate(chips):
                _gather_copy(src, out, send_sems, recv_sems, t, j, 2 * px + py, (px, py, c)).wait_recv()
                _gather_copy(src, out, send_sems, recv_sems, t, 3 + j, 2 * px + py, (x, y, 1 - c), forward=True).start()


def _gather_finish(src, out, send_sems, recv_sems):
    x, y, c, chips = _place()
    slots = [2 * px + py for px, py in chips]
    for t, owner in enumerate(OWNER_CORE):
        copy = functools.partial(_gather_copy, src, out, send_sems, recv_sems, t)

        @pl.when(c == owner)
        def _():
            for j, (px, py) in enumerate(chips):
                copy(j, 2 * x + y, (px, py, c)).wait_send()
                copy(3 + j, slots[j], (x, y, 1 - c), forward=True).wait_send()

        @pl.when(c != owner)
        def _():
            for j in range(3):
                copy(3 + j, slots[j], (x, y, 1 - c), forward=True).wait_recv()


def _gathered_shapes(shards):
    return [jax.ShapeDtypeStruct((4 * a.shape[0], a.shape[1]), a.dtype) for a in shards]


def _comm_gather_layer(shards, *, name):
    nt = len(shards)

    def body(*refs):
        src, out, sems = refs[:nt], refs[nt:2 * nt], refs[2 * nt:]
        _gather_start(src, out, *sems)
        _gather_forward(src, out, *sems)
        _gather_finish(src, out, *sems)

    return pl.pallas_call(
        body, name=name, in_specs=[ANY] * nt, out_specs=[ANY] * nt, out_shape=_gathered_shapes(shards),
        scratch_shapes=_sems(6, nt),
    )(*shards)


def _comm_swap_sibling(bufs, *, name):
    nt = len(bufs)

    def body(*refs):
        src, out, (send_sems, recv_sems) = refs[:nt], refs[nt:2 * nt], refs[2 * nt:]
        x, y, c, _ = _place()
        cps = [pltpu.make_async_remote_copy(src_ref=src[t], dst_ref=out[t], send_sem=send_sems.at[t], recv_sem=recv_sems.at[t],
                                            device_id=(x, y, 1 - c), device_id_type=MESH) for t in range(nt)]
        for cp in cps:
            cp.start()
        for cp in cps:
            cp.wait()

    return pl.pallas_call(
        body, name=name, in_specs=[ANY] * nt, out_specs=[ANY] * nt,
        out_shape=[jax.ShapeDtypeStruct(a.shape, a.dtype) for a in bufs], scratch_shapes=_sems(nt),
    )(*bufs)


def _scatter_copy(src, out, send_sems, recv_sems, j, t, from_slot, to_slot, to):
    return pltpu.make_async_remote_copy(src_ref=src[t].at[:, from_slot], dst_ref=out[t].at[:, to_slot],
                                        send_sem=send_sems.at[j, t], recv_sem=recv_sems.at[j, t], device_id=to,
                                        device_id_type=MESH)


def _scatter_start(src, out, send_sems, recv_sems):
    x, y, c, chips = _place()
    for j, (px, py) in enumerate(chips):
        for t in range(len(src)):
            _scatter_copy(src, out, send_sems, recv_sems, j, t, 2 * px + py, 2 * x + y, (px, py, c)).start()


def _scatter_finish(src, out, send_sems, recv_sems):
    x, y, c, chips = _place()
    for j, (px, py) in enumerate(chips):
        for t in range(len(src)):
            _scatter_copy(src, out, send_sems, recv_sems, j, t, 2 * x + y, 2 * px + py, (px, py, c)).wait_recv()
    for j, (px, py) in enumerate(chips):
        for t in range(len(src)):
            _scatter_copy(src, out, send_sems, recv_sems, j, t, 2 * px + py, 2 * x + y, (px, py, c)).wait_send()


def _comm_scatter_chips(parts, *, name):
    nt = len(parts)

    def body(*refs):
        src, out, sems = refs[:nt], refs[nt:2 * nt], refs[2 * nt:]
        _scatter_start(src, out, *sems)
        _scatter_finish(src, out, *sems)

    return pl.pallas_call(
        body, name=name, in_specs=[ANY] * nt, out_specs=[ANY] * nt,
        out_shape=[jax.ShapeDtypeStruct(a.shape, a.dtype) for a in parts], scratch_shapes=_sems(3, nt),
    )(*parts)


def _comm_join_halves(bufs, *, name):
    nt = len(bufs)

    def body(*refs):
        src, out, (send_sems, recv_sems) = refs[:nt], refs[nt:2 * nt], refs[2 * nt:]
        x, y, c, _ = _place()
        copy = lambda t, half: pltpu.make_async_remote_copy(
            src_ref=src[t].at[:, c], dst_ref=out[t].at[:, half], send_sem=send_sems.at[t], recv_sem=recv_sems.at[t],
            device_id=(x, y, 1 - c), device_id_type=MESH)
        sends = [copy(t, c) for t in range(nt)]
        for cp in sends:
            cp.start()
        for t in range(nt):
            copy(t, 1 - c).wait_recv()
        for cp in sends:
            cp.wait_send()

    return pl.pallas_call(
        body, name=name, in_specs=[ANY] * nt, out_specs=[ANY] * nt,
        out_shape=[jax.ShapeDtypeStruct(a.shape, a.dtype) for a in bufs],
        input_output_aliases={t: t for t in range(nt)}, scratch_shapes=_sems(nt),
    )(*bufs)


def _comm_allreduce_small(part, *, name):
    rows, cols = part.shape

    def body(p_ref, o_ref, buf, send_sems, recv_sems):
        x, y, c, _ = _place()
        me = 4 * x + 2 * y + c
        buf[me] = p_ref[...]
        flip = lambda v, bit: 1 - v if bit else v
        peers = [(flip(x, d & 4), flip(y, d & 2), flip(c, d & 1)) for d in range(1, 8)]
        sends = []
        for j, peer in enumerate(peers):
            cp = pltpu.make_async_remote_copy(src_ref=buf.at[me], dst_ref=buf.at[me], send_sem=send_sems.at[j],
                                              recv_sem=recv_sems.at[j], device_id=peer, device_id_type=MESH)
            cp.start()
            sends.append(cp)
        for j, (px, py, pc) in enumerate(peers):
            pltpu.make_async_remote_copy(src_ref=buf.at[me], dst_ref=buf.at[4 * px + 2 * py + pc], send_sem=send_sems.at[j],
                                         recv_sem=recv_sems.at[j], device_id=(px, py, pc), device_id_type=MESH).wait_recv()
        for cp in sends:
            cp.wait_send()
        total = buf[0]
        for i in range(1, 8):
            total = total + buf[i]
        o_ref[...] = total

    vm = pl.BlockSpec(memory_space=pltpu.VMEM)
    return pl.pallas_call(
        body, name=name, in_specs=[vm], out_specs=vm, out_shape=jax.ShapeDtypeStruct((rows, cols), F32),
        scratch_shapes=[pltpu.VMEM((8, rows, cols), F32), pltpu.SemaphoreType.DMA((7,)), pltpu.SemaphoreType.DMA((7,))],
    )(part)


def _pad_in_rows(wt):
    r = lambda o, n: wt[..., o:o + n, :]
    kr = r(2944, 64)
    return jnp.concatenate([r(0, 1024), r(1280, 1024), r(3008, 1024), r(2304, Q_RANK), r(1024, 128), r(2688, KV_RANK),
                            r(1152, 128), kr, jnp.zeros_like(kr)], axis=-2)


def _unpad_in_rows(qa, ga, gb, mixed):
    cq, ka, ckv, va, kr = (mixed[..., o - CQ_OFF:o - CQ_OFF + n, :] for o, n in (
        (CQ_OFF, Q_RANK), (KA_OFF, 128), (CKV_OFF, KV_RANK), (VA_OFF, 128), (KR_OFF, 64)))
    return jnp.concatenate([qa, ka, va, ga, cq, ckv, kr, gb], axis=-2)


def _pad_q_rows(wt):
    lead, cols = wt.shape[:-2], wt.shape[-1]
    wt = jnp.pad(wt.reshape(lead + (MLA_HEADS, 192, cols)), [(0, 0)] * (len(lead) + 1) + [(0, 64), (0, 0)])
    return wt.reshape(lead + (MLA_HEADS * 256, cols))


def _unpad_q_rows(wt):
    lead, cols = wt.shape[:-2], wt.shape[-1]
    return wt.reshape(lead + (MLA_HEADS, 256, cols))[..., :192, :].reshape(lead + (MLA_HEADS * 192, cols))


def _perm_kv_rows(wt):
    lead, cols = wt.shape[:-2], wt.shape[-1]
    return jnp.swapaxes(wt.reshape(lead + (MLA_HEADS, 2, 128, cols)), -4, -3).reshape(lead + (2048, cols))


def _unperm_kv_rows(wt):
    lead, cols = wt.shape[:-2], wt.shape[-1]
    return jnp.swapaxes(wt.reshape(lead + (2, MLA_HEADS, 128, cols)), -4, -3).reshape(lead + (2048, cols))


def _t(a):
    return jnp.swapaxes(a, -1, -2)


def _rope_tables(s):
    pos = jnp.arange(s, dtype=F32)
    inv_freq = 10000.0 ** (-jnp.arange(0, 64, 2, dtype=F32) / 64)
    ang = pos[:, None] * inv_freq[None, :]
    cos, sin = jnp.cos(ang), jnp.sin(ang)
    z64 = jnp.zeros((s, 64), F32)
    tk_c = jnp.concatenate([cos, cos, z64], axis=-1)
    tk_s = jnp.concatenate([-sin, sin, z64], axis=-1)
    return tk_c, tk_s


def _layer_weights(gathered, own, chip):
    def with_own_rows(g, o):
        n = o.shape[0]
        return jnp.concatenate([lax.select(chip == j, o, g[j * n:(j + 1) * n]) for j in range(4)], axis=0)

    full_in, full_q, full_kv, full_o = (with_own_rows(g, o) for g, o in zip(gathered, own))
    return _pad_in_rows(full_in), _pad_q_rows(full_q), _perm_kv_rows(full_kv), full_o


def _device_step(xs, tgt, attn_g, sinks, gq, gkv, final_g, shards, where):
    chip = where[0]
    depth = shards[0].shape[0]
    s = xs.shape[0]
    tabs = _rope_tables(s)
    saved = []
    x = xs
    of_layer = lambda l: [a[l] for a in shards]
    gathered = _comm_gather_layer(of_layer(0), name="comm_gather_layer0")
    weights = []
    for l in range(depth):
        w_in_p, w_q_p, w_kv_p, w_o = _layer_weights(gathered, of_layer(l), chip)
        weights.append((w_in_p, w_q_p, w_kv_p, w_o))
        h = _rmsnorm_fwd(x, attn_g[l:l + 1], name=f"norm_fwd{l}")
        proj = _matmul(h, w_in_p, tb=True, name=f"in_proj{l}")
        oa, lse_a = _swa_fwd(proj, sinks[l], name=f"swa_fwd{l}")
        qcat, kcat, v, cqn, ckvn = _mla_qkv_fwd(proj, gq[l:l + 1], gkv[l:l + 1], w_q_p, w_kv_p, *tabs, layer=None,
                                                name=f"mla_qkv_fwd{l}")
        ob, lse_b, gathered = _mla_fwd(qcat, kcat, v, gather=of_layer(l + 1) if l + 1 < depth else None,
                                       name=f"mla_fwd{l}")
        y = _gate_fwd(oa, ob, proj, name=f"gate_fwd{l}")
        x_next = _matmul(y, w_o, add=x, name=f"out_proj{l}")
        saved.append((x, h, proj, oa, lse_a, qcat, kcat, v, cqn, ckvn, ob, lse_b, y))
        x = x_next

    dx, d_final_g, loss = _final_loss(x, final_g, tgt, name="final_loss")

    d_attn_g, d_sinks, d_gq, d_gkv = [None] * depth, [None] * depth, [None] * depth, [None] * depth
    d_w_in, d_w_q, d_w_kv, d_w_o = [None] * depth, [None] * depth, [None] * depth, [None] * depth
    half = depth // 2
    carried = {half - 1: list(range(half, depth)), **{l - 1: [l] for l in range(1, half)}}
    done = {}

    def grads_by_chip(layers):
        stack = lambda per_layer: jnp.stack([per_layer[l] for l in layers])
        in_tiles = [stack([tiles[j] if tiles else None for tiles in d_w_in]) for j in range(4)]
        full = [_unpad_in_rows(*in_tiles), _unpad_q_rows(stack(d_w_q)), _unperm_kv_rows(stack(d_w_kv)), stack(d_w_o)]
        return [g.reshape(g.shape[0], 4, 2, g.shape[1] // 8, g.shape[2]) for g in full]

    def reduce_begin(grads5, tag):
        give = [lax.dynamic_index_in_dim(g, 1 - where[4], axis=2, keepdims=False).astype(WIRE_DTYPE) for g in grads5]
        recv = _comm_swap_sibling(give, name=f"comm_swap_sibling_{tag}")
        return recv, [_pair_sum(where, g, r, name=f"pair_sum_{tag}_{t}") for t, (g, r) in enumerate(zip(grads5, recv))]

    def reduce_end(grads5, recv, parts, tag):
        mine = [_chip_sum(where, g, r, p, name=f"chip_sum_{tag}_{t}") for t, (g, r, p) in enumerate(zip(grads5, recv, parts))]
        joined = _comm_join_halves(mine, name=f"comm_join_halves_{tag}")
        return [a.reshape(a.shape[0], -1, a.shape[-1]) for a in joined]

    for l in reversed(range(depth)):
        x, h, proj, oa, lse_a, qcat, kcat, v, cqn, ckvn, ob, lse_b, y = saved[l]
        w_in_p, w_q_p, w_kv_p, w_o = weights[l]
        dy = _matmul(dx, w_o, tb=True, name=f"out_proj_dx{l}")
        d_w_o[l] = _matmul(y, dx, ta=True, name=f"out_proj_dw{l}")
        doa, dob, dga, dgb, delta_b = _gate_bwd(dy, oa, ob, proj, name=f"gate_bwd{l}")
        dqa, dka, dva, dsink = _swa_bwd(proj, sinks[l], lse_a, doa, name=f"swa_bwd{l}")
        phase, pair = carried.get(l), None
        if phase:
            grads5 = grads_by_chip(phase)
            recv, pair = reduce_begin(grads5, f"l{phase[0]}")
        dqc, dkc, dv, parts = _mla_bwd(qcat, kcat, v, dob, lse_b, delta_b, scatter=pair, name=f"mla_bwd{l}")
        if phase:
            done[phase[0]] = reduce_end(grads5, recv, parts, f"l{phase[0]}")
        mixed, d_w_q[l], d_w_kv[l], dgq_l, dgkv_l = _mla_qkv_bwd(
            proj, cqn, ckvn, dqc, dkc, dv, dka, dva, gq[l:l + 1], gkv[l:l + 1], w_q_p, w_kv_p, *tabs, layer=None,
            name=f"mla_qkv_bwd{l}")
        dproj = [dqa, dga, dgb, mixed]
        dh = _matmul_ktiles(dproj, w_in_p, name=f"in_proj_dx{l}")
        d_w_in[l] = [_matmul(tile, h, ta=True, name=f"in_proj_dw{l}_{j}") for j, tile in enumerate(dproj)]
        dx, dg_l = _rmsnorm_bwd(dh, x, attn_g[l:l + 1], dx, name=f"norm_bwd{l}")
        d_attn_g[l], d_sinks[l], d_gq[l], d_gkv[l] = dg_l, dsink[0:1, :SWA_HEADS], dgq_l, dgkv_l

    grads5 = grads_by_chip([0])
    recv, pair = reduce_begin(grads5, "l0")
    done[0] = reduce_end(grads5, recv, _comm_scatter_chips(pair, name="comm_scatter_chips_l0"), "l0")
    cat = lambda parts: jnp.concatenate(parts, axis=0)
    reduced = [cat([done[first][t] for first in sorted(done)]) for t in range(len(shards))]
    return loss, dx, cat(d_attn_g), cat(d_sinks), cat(d_gq), cat(d_gkv), d_final_g, reduced


def kernel(x, attn_norm_g, w_in, swa_sinks, q_a_norm_g, kv_a_norm_g, w_q_b, w_kv_b, w_out, final_norm_g, loss_target, m_attn_norm_g, m_w_in, m_swa_sinks, m_q_a_norm_g, m_kv_a_norm_g, m_w_q_b, m_w_kv_b, m_w_out, m_final_norm_g, v_attn_norm_g, v_w_in, v_swa_sinks, v_q_a_norm_g, v_kv_a_norm_g, v_w_q_b, v_w_kv_b, v_w_out, v_final_norm_g):
    x_, y_, c = lax.axis_index("x"), lax.axis_index("y"), lax.axis_index("c")
    chip = 2 * x_ + y_
    where = jnp.stack([chip, 2 * (1 - x_) + y_, 2 * x_ + 1 - y_, 2 * (1 - x_) + 1 - y_, c]).astype(jnp.int32)

    sent = [a.astype(WIRE_DTYPE) for a in (_t(w_in), _t(w_q_b), _t(w_kv_b), w_out)]

    loss, dx, d_attn_g, d_sinks, d_gq, d_gkv, d_final_g, reduced = _device_step(
        x[0], loss_target[0], attn_norm_g, swa_sinks, q_a_norm_g, kv_a_norm_g, final_norm_g.reshape(1, -1), sent, where)
    g_w_in, g_w_q_b, g_w_kv_b, g_w_out = _t(reduced[0]), _t(reduced[1]), _t(reduced[2]), reduced[3]

    small = [d_attn_g, d_sinks, d_gq, d_gkv, d_final_g, loss[:, :1]]
    flat = jnp.concatenate([a.reshape(-1) for a in small])
    n_small = flat.shape[0]
    rows = -(-n_small // 1024) * 8
    total = _comm_allreduce_small(jnp.pad(flat, (0, rows * 128 - n_small)).reshape(rows, 128),
                                  name="comm_allreduce_small").reshape(-1)
    outs, at = [], 0
    for a in small:
        outs.append(total[at:at + a.size].reshape(a.shape))
        at += a.size
    g_attn_g, g_sinks, g_gq, g_gkv, g_final_g, loss_total = outs
    g_final_g = g_final_g.reshape(final_norm_g.shape)

    weights = [attn_norm_g, w_in, swa_sinks, q_a_norm_g, kv_a_norm_g, w_q_b, w_kv_b, w_out, final_norm_g]
    grads = [g_attn_g, g_w_in, g_sinks, g_gq, g_gkv, g_w_q_b, g_w_kv_b, g_w_out, g_final_g]
    ms = [m_attn_norm_g, m_w_in, m_swa_sinks, m_q_a_norm_g, m_kv_a_norm_g, m_w_q_b, m_w_kv_b, m_w_out, m_final_norm_g]
    vs = [v_attn_norm_g, v_w_in, v_swa_sinks, v_q_a_norm_g, v_kv_a_norm_g, v_w_q_b, v_w_kv_b, v_w_out, v_final_norm_g]
    as2d = lambda a: a.reshape(1, -1) if a.ndim == 1 else a
    deltas, new_m, new_v = [], [], []
    for i, (w, g, m, v) in enumerate(zip(weights, grads, ms, vs)):
        view = _t if w is w_in else as2d
        d, mn, vn = _adamw(view(w), reduced[0] if w is w_in else view(g), view(m), view(v), name=f"adamw{i}")
        back = _t if w is w_in else (lambda a: a.reshape(w.shape))
        deltas.append(back(d))
        new_m.append(back(mn))
        new_v.append(back(vn))

    return (loss_total.reshape(()), dx[None], *grads, *deltas, *new_m, *new_v)
```

```python
import functools
import math

import jax
import jax.numpy as jnp
from jax import lax
from jax.experimental import pallas as pl
from jax.experimental.pallas import tpu as pltpu

F32 = jnp.float32
MXU_DTYPE = jnp.bfloat16
WIRE_DTYPE = jnp.bfloat16

EPS = 1e-6
NEG = -1e30
BLOCK = 128
D_MODEL = 2048
SWA_HEADS = 16
MLA_HEADS = 8
Q_RANK = 384
KV_RANK = 256
IN_WIDTH = 4032
MLA_SCALE = 192 ** -0.5
SWA_SCALE = 64 ** -0.5
LOG2E = math.log2(math.e)
HEADS_PER_STEP = 2
SLOPES = tuple(2.0 ** (-8.0 * (h + 1) / SWA_HEADS) for h in range(SWA_HEADS))

P_WIDTH = 4096
QA_OFF, GA_OFF, GB_OFF, CQ_OFF, KA_OFF, CKV_OFF, VA_OFF, KR_OFF = 0, 1024, 2048, 3072, 3456, 3584, 3840, 3968

ADAM_LR, ADAM_B1, ADAM_B2, ADAM_EPS, ADAM_WD, ADAM_STEP = 0.001, 0.9, 0.999, 1e-08, 0.01, 10

VMEM_LIMIT = 56 * 1024 * 1024
MESH = pl.DeviceIdType.MESH
ANY = pl.BlockSpec(memory_space=pl.ANY)


def _cparams(*sem):
    return pltpu.CompilerParams(dimension_semantics=sem, vmem_limit_bytes=VMEM_LIMIT)


def _dot(a, b, ca, cb):
    return lax.dot_general(a, b, (((ca,), (cb,)), ((), ())), preferred_element_type=F32)


def _layer_spec(block, index_map, layer):
    if layer is None:
        return pl.BlockSpec(block, index_map)
    return pl.BlockSpec((None,) + tuple(block), lambda *g: (layer,) + tuple(index_map(*g)))


def _matmul(a, b, *, name, ta=False, tb=False, out_dtype=F32, add=None, b_layer=None, tm=1024, tn=1024, tk=2048):
    (kdim, m) = a.shape if ta else a.shape[::-1]
    (n, k2) = b.shape[-2:] if tb else b.shape[-2:][::-1]
    assert kdim == k2, (a.shape, b.shape)
    tm, tn, tk = min(tm, m), min(tn, n), min(tk, kdim)
    assert m % tm == 0 and n % tn == 0 and kdim % tk == 0
    nk = kdim // tk

    def body(*refs):
        a_ref, b_ref = refs[:2]
        add_ref = None if add is None else refs[2]
        o_ref = refs[2 + (add is not None)]
        part = _dot(a_ref[...].astype(MXU_DTYPE), b_ref[...].astype(MXU_DTYPE), 0 if ta else 1, 1 if tb else 0)

        def finish(r):
            o_ref[...] = (r if add is None else add_ref[...] + r).astype(out_dtype)

        if nk == 1:
            finish(part)
            return
        acc = refs[-1]
        k = pl.program_id(2)

        @pl.when(k == 0)
        def _():
            acc[...] = part

        @pl.when((k > 0) & (k < nk - 1))
        def _():
            acc[...] += part

        @pl.when(k == nk - 1)
        def _():
            finish(acc[...] + part)

    a_spec = pl.BlockSpec((tk, tm), lambda i, j, k: (k, i)) if ta else pl.BlockSpec((tm, tk), lambda i, j, k: (i, k))
    b_spec = (_layer_spec((tn, tk), lambda i, j, k: (j, k), b_layer) if tb else
              _layer_spec((tk, tn), lambda i, j, k: (k, j), b_layer))
    in_specs, args = [a_spec, b_spec], [a, b]
    if add is not None:
        in_specs.append(pl.BlockSpec((tm, tn), lambda i, j, k: (i, j)))
        args.append(add)
    return pl.pallas_call(
        body, name=name, grid=(m // tm, n // tn, nk), in_specs=in_specs,
        out_specs=pl.BlockSpec((tm, tn), lambda i, j, k: (i, j)),
        out_shape=jax.ShapeDtypeStruct((m, n), out_dtype),
        scratch_shapes=[pltpu.VMEM((tm, tn), F32)] if nk > 1 else [],
        compiler_params=_cparams("parallel", "parallel", "arbitrary"),
    )(*args)


def _matmul_ktiles(a_tiles, b, *, name, b_layer=None, tm=1024, tn=1024):
    m, kt = a_tiles[0].shape
    n = b.shape[-1]
    nt = len(a_tiles)
    assert b.shape[-2] == nt * kt
    tm, tn = min(tm, m), min(tn, n)
    assert m % tm == 0 and n % tn == 0

    def body(*refs):
        a_refs, b_refs, o_ref = refs[:nt], refs[nt:2 * nt], refs[2 * nt]
        acc = _dot(a_refs[0][...].astype(MXU_DTYPE), b_refs[0][...].astype(MXU_DTYPE), 1, 0)
        for j in range(1, nt):
            acc += _dot(a_refs[j][...].astype(MXU_DTYPE), b_refs[j][...].astype(MXU_DTYPE), 1, 0)
        o_ref[...] = acc

    in_specs = [pl.BlockSpec((tm, kt), lambda jn, i: (i, 0))] * nt
    in_specs += [_layer_spec((kt, tn), lambda jn, i, j=j: (j, jn), b_layer) for j in range(nt)]
    return pl.pallas_call(
        body, name=name, grid=(n // tn, m // tm), in_specs=in_specs,
        out_specs=pl.BlockSpec((tm, tn), lambda jn, i: (i, jn)),
        out_shape=jax.ShapeDtypeStruct((m, n), F32),
        compiler_params=_cparams("parallel", "parallel"),
    )(*a_tiles, *([b] * nt))


def _rmsnorm_fwd(x, g, *, name):
    s, d = x.shape
    tm = min(512, s)

    def body(x_ref, g_ref, h_ref):
        xv = x_ref[...]
        r = lax.rsqrt(jnp.mean(xv * xv, axis=-1, keepdims=True) + EPS)
        h_ref[...] = (xv * r * g_ref[...]).astype(MXU_DTYPE)

    return pl.pallas_call(
        body, name=name, grid=(s // tm,),
        in_specs=[pl.BlockSpec((tm, d), lambda i: (i, 0)), pl.BlockSpec((1, d), lambda i: (0, 0))],
        out_specs=pl.BlockSpec((tm, d), lambda i: (i, 0)),
        out_shape=jax.ShapeDtypeStruct((s, d), MXU_DTYPE),
        compiler_params=_cparams("parallel"),
    )(x, g)


def _rmsnorm_bwd(dh, x, g, dres, *, name):
    s, d = x.shape
    tm = min(512, s)

    def body(dh_ref, x_ref, g_ref, dres_ref, dx_ref, dg_ref):
        @pl.when(pl.program_id(0) == 0)
        def _():
            dg_ref[...] = jnp.zeros_like(dg_ref)

        xv = x_ref[...]
        r = lax.rsqrt(jnp.mean(xv * xv, axis=-1, keepdims=True) + EPS)
        xn = xv * r
        dy = dh_ref[...]
        dg_ref[...] += jnp.sum(dy * xn, axis=0, keepdims=True)
        u = dy * g_ref[...]
        dx_ref[...] = dres_ref[...] + r * (u - xn * jnp.mean(u * xn, axis=-1, keepdims=True))

    row = pl.BlockSpec((tm, d), lambda i: (i, 0))
    vec = pl.BlockSpec((1, d), lambda i: (0, 0))
    return pl.pallas_call(
        body, name=name, grid=(s // tm,), in_specs=[row, row, vec, row], out_specs=[row, vec],
        out_shape=[jax.ShapeDtypeStruct((s, d), F32), jax.ShapeDtypeStruct((1, d), F32)],
        compiler_params=_cparams("arbitrary"),
    )(dh, x, g, dres)


def _final_loss(x, g, tgt, *, name):
    s, d = x.shape
    tm = min(512, s)

    def body(x_ref, g_ref, t_ref, dx_ref, dg_ref, loss_ref):
        @pl.when(pl.program_id(0) == 0)
        def _():
            dg_ref[...] = jnp.zeros_like(dg_ref)
            loss_ref[...] = jnp.zeros_like(loss_ref)

        xv = x_ref[...]
        gv = g_ref[...]
        r = lax.rsqrt(jnp.mean(xv * xv, axis=-1, keepdims=True) + EPS)
        xn = xv * r
        err = xn * gv - t_ref[...]
        sq = jnp.sum(jnp.sum(err * err, axis=-1, keepdims=True), axis=0, keepdims=True)
        loss_ref[...] += (0.5 / d) * sq
        dy = err * (1.0 / d)
        dg_ref[...] += jnp.sum(dy * xn, axis=0, keepdims=True)
        u = dy * gv
        dx_ref[...] = r * (u - xn * jnp.mean(u * xn, axis=-1, keepdims=True))

    row = pl.BlockSpec((tm, d), lambda i: (i, 0))
    vec = pl.BlockSpec((1, d), lambda i: (0, 0))
    return pl.pallas_call(
        body, name=name, grid=(s // tm,), in_specs=[row, vec, row],
        out_specs=[row, vec, pl.BlockSpec((1, 128), lambda i: (0, 0))],
        out_shape=[jax.ShapeDtypeStruct((s, d), F32), jax.ShapeDtypeStruct((1, d), F32),
                   jax.ShapeDtypeStruct((1, 128), F32)],
        compiler_params=_cparams("arbitrary"),
    )(x, g, tgt)


def _swa_keys(kp_ref, kc_ref):
    kk = jnp.concatenate([kp_ref[...], kc_ref[...]], axis=0)
    kr = pltpu.roll(kk, 64, 1)
    lo = lax.broadcasted_iota(jnp.int32, kk.shape, 1) < 64
    return [jnp.where(lo, kk, kr).astype(MXU_DTYPE), jnp.where(lo, kr, kk).astype(MXU_DTYPE)]


GROUP = SWA_HEADS // 2


def _swa_mask(n):
    qi = lax.broadcasted_iota(jnp.int32, (BLOCK, 2 * BLOCK), 0)
    ki = lax.broadcasted_iota(jnp.int32, (BLOCK, 2 * BLOCK), 1)
    delta = BLOCK + qi - ki
    valid = (delta >= 0) & (delta < BLOCK) & ((ki >= BLOCK) | (n > 0))
    return valid, delta.astype(F32)


def _stack_heads(ref, j):
    lo = lax.broadcasted_iota(jnp.int32, (BLOCK, BLOCK), 1) < 64
    parts = []
    for r in range(GROUP):
        pair = (GROUP * j + r) // 2
        blk = ref[:, pair * 128:(pair + 1) * 128].astype(F32)
        parts.append(jnp.where(lo if r % 2 == 0 else ~lo, blk, 0.0).astype(MXU_DTYPE))
    return jnp.concatenate(parts, axis=0)


def _unstack_heads(stacked, ref, j):
    lo = lax.broadcasted_iota(jnp.int32, (BLOCK, BLOCK), 1) < 64
    for i in range(GROUP // 2):
        pair = (GROUP * j) // 2 + i
        even, odd = stacked[2 * i * BLOCK:(2 * i + 1) * BLOCK], stacked[(2 * i + 1) * BLOCK:(2 * i + 2) * BLOCK]
        ref[:, pair * 128:(pair + 1) * 128] = jnp.where(lo, even, odd).astype(ref.dtype)


def _head_rows(stacked, r):
    return stacked[r * BLOCK:(r + 1) * BLOCK]


def _swa_scores(raw, h, valid, deltaf):
    return jnp.where(valid, raw * (SWA_SCALE * LOG2E) - (SLOPES[h] * LOG2E) * deltaf, NEG)


def _swa_specs(nb):
    kcol, vcol = KA_OFF // BLOCK, VA_OFF // BLOCK
    last = nb - 1
    cur = lambda n: jnp.minimum(n, last)
    prev = lambda n: jnp.maximum(jnp.minimum(n, last) - 1, 0)
    return [
        pl.BlockSpec(memory_space=pltpu.SMEM),
        pl.BlockSpec((BLOCK, 1024), lambda n: (cur(n), QA_OFF // 1024)),
        pl.BlockSpec((BLOCK, BLOCK), lambda n: (cur(n), kcol)),
        pl.BlockSpec((BLOCK, BLOCK), lambda n: (prev(n), kcol)),
        pl.BlockSpec((BLOCK, BLOCK), lambda n: (cur(n), vcol)),
        pl.BlockSpec((BLOCK, BLOCK), lambda n: (prev(n), vcol)),
    ]


def _swa_fwd(proj, sinks, *, name):
    s = proj.shape[0]
    nb = s // BLOCK

    def body(sink_ref, q_ref, kc_ref, kp_ref, vc_ref, vp_ref, o_ref, lse_ref):
        n = pl.program_id(0)
        keys = _swa_keys(kp_ref, kc_ref)
        vals = _swa_keys(vp_ref, vc_ref)
        valid, deltaf = _swa_mask(n)
        lane = lax.broadcasted_iota(jnp.int32, (BLOCK, BLOCK), 1)
        lse_acc = jnp.zeros((BLOCK, BLOCK), F32)
        for j in range(2):
            raw = _dot(_stack_heads(q_ref, j), keys[j], 1, 1)
            probs = []
            for r in range(GROUP):
                h = GROUP * j + r
                sc = _swa_scores(_head_rows(raw, r), h, valid, deltaf)
                sink = sink_ref[h] * LOG2E
                m = jnp.maximum(jnp.max(sc, axis=-1, keepdims=True), sink)
                p = jnp.exp2(sc - m)
                l = jnp.sum(p, axis=-1, keepdims=True) + jnp.exp2(sink - m)
                probs.append((p * (1.0 / l)).astype(MXU_DTYPE))
                lse_acc = jnp.where(lane == h, m + jnp.log(l) * LOG2E, lse_acc)
            _unstack_heads(jnp.dot(jnp.concatenate(probs, axis=0), vals[j], preferred_element_type=F32), o_ref, j)
        lse_ref[...] = lse_acc

    return pl.pallas_call(
        body, name=name, grid=(nb,), in_specs=_swa_specs(nb),
        out_specs=[pl.BlockSpec((BLOCK, 1024), lambda n: (n, 0)), pl.BlockSpec((BLOCK, BLOCK), lambda n: (n, 0))],
        out_shape=[jax.ShapeDtypeStruct((s, 1024), F32), jax.ShapeDtypeStruct((s, BLOCK), F32)],
        compiler_params=_cparams("parallel"),
    )(sinks, proj, proj, proj, proj, proj)


def _swa_bwd(proj, sinks, lse, do, *, name):
    s = proj.shape[0]
    nb = s // BLOCK
    last = nb - 1

    def body(sink_ref, q_ref, kc_ref, kp_ref, vc_ref, vp_ref, lse_ref, do_ref,
             dq_ref, dk_ref, dv_ref, dsink_ref, carry_k, carry_v):
        n = pl.program_id(0)

        @pl.when(n == 0)
        def _():
            carry_k[...] = jnp.zeros_like(carry_k)
            carry_v[...] = jnp.zeros_like(carry_v)
            dsink_ref[...] = jnp.zeros_like(dsink_ref)

        @pl.when(n < nb)
        def _():
            keys = _swa_keys(kp_ref, kc_ref)
            vals = _swa_keys(vp_ref, vc_ref)
            valid, deltaf = _swa_mask(n)
            lane = lax.broadcasted_iota(jnp.int32, (BLOCK, BLOCK), 1)
            lane1 = lax.broadcasted_iota(jnp.int32, (1, BLOCK), 1)
            lse_blk = lse_ref[...]
            acc_k, acc_v = [], []
            dsink = jnp.zeros((1, BLOCK), F32)
            for j in range(2):
                q_all, do_all = _stack_heads(q_ref, j), _stack_heads(do_ref, j)
                raw = _dot(q_all, keys[j], 1, 1)
                dp_all = _dot(do_all, vals[j], 1, 1)
                probs, dscores = [], []
                for r in range(GROUP):
                    h = GROUP * j + r
                    lse_h = jnp.sum(jnp.where(lane == h, lse_blk, 0.0), axis=-1, keepdims=True)
                    p = jnp.exp2(_swa_scores(_head_rows(raw, r), h, valid, deltaf) - lse_h)
                    dp = _head_rows(dp_all, r)
                    dlt = jnp.sum(dp * p, axis=-1, keepdims=True)
                    dscores.append((p * (dp - dlt) * SWA_SCALE).astype(MXU_DTYPE))
                    probs.append(p.astype(MXU_DTYPE))
                    sunk = jnp.exp2(sink_ref[h] * LOG2E - lse_h) * dlt
                    dsink = jnp.where(lane1 == h, -jnp.sum(sunk, axis=0, keepdims=True), dsink)
                ds_all = jnp.concatenate(dscores, axis=0)
                _unstack_heads(jnp.dot(ds_all, keys[j], preferred_element_type=F32), dq_ref, j)
                acc_k.append(_dot(ds_all, q_all, 0, 0))
                acc_v.append(_dot(jnp.concatenate(probs, axis=0), do_all, 0, 0))
            lo2 = lax.broadcasted_iota(jnp.int32, (2 * BLOCK, BLOCK), 1) < 64
            fold = lambda acc: jnp.where(lo2, acc[0] + pltpu.roll(acc[0], 64, 1), acc[1] + pltpu.roll(acc[1], 64, 1))
            dkk, dvv = fold(acc_k), fold(acc_v)
            dk_ref[...] = (carry_k[...] + dkk[:BLOCK]).astype(dk_ref.dtype)
            dv_ref[...] = (carry_v[...] + dvv[:BLOCK]).astype(dv_ref.dtype)
            carry_k[...] = dkk[BLOCK:]
            carry_v[...] = dvv[BLOCK:]
            dsink_ref[...] += jnp.broadcast_to(dsink, dsink_ref.shape)

        @pl.when(n == nb)
        def _():
            dk_ref[...] = carry_k[...].astype(dk_ref.dtype)
            dv_ref[...] = carry_v[...].astype(dv_ref.dtype)

    cur = lambda n: jnp.minimum(n, last)
    lag = lambda n: jnp.maximum(n - 1, 0)
    return pl.pallas_call(
        body, name=name, grid=(nb + 1,),
        in_specs=_swa_specs(nb) + [pl.BlockSpec((BLOCK, BLOCK), lambda n: (cur(n), 0)),
                                   pl.BlockSpec((BLOCK, 1024), lambda n: (cur(n), 0))],
        out_specs=[pl.BlockSpec((BLOCK, 1024), lambda n: (cur(n), 0)),
                   pl.BlockSpec((BLOCK, BLOCK), lambda n: (lag(n), 0)),
                   pl.BlockSpec((BLOCK, BLOCK), lambda n: (lag(n), 0)),
                   pl.BlockSpec((8, BLOCK), lambda n: (0, 0))],
        out_shape=[jax.ShapeDtypeStruct((s, 1024), MXU_DTYPE), jax.ShapeDtypeStruct((s, BLOCK), MXU_DTYPE),
                   jax.ShapeDtypeStruct((s, BLOCK), MXU_DTYPE), jax.ShapeDtypeStruct((8, BLOCK), F32)],
        scratch_shapes=[pltpu.VMEM((BLOCK, BLOCK), F32), pltpu.VMEM((BLOCK, BLOCK), F32)],
        compiler_params=_cparams("arbitrary"),
    )(sinks, proj, proj, proj, proj, proj, lse, do)


def _rope_partner(v, first, width):
    lane = lax.broadcasted_iota(jnp.int32, v.shape, 1)
    in_a = (lane >= first) & (lane < first + 32)
    in_b = (lane >= first + 32) & (lane < first + 64)
    return jnp.where(in_a, pltpu.roll(v, width - 32, 1), jnp.where(in_b, pltpu.roll(v, 32, 1), 0.0))


def _mla_qkv_fwd(proj, gq, gkv, wq, wkv, tk_c, tk_s, *, layer, name):
    s = proj.shape[0]
    tm = min(512, s)

    def body(cq_ref, ckv_ref, kr_ref, gq_ref, gkv_ref, wq_ref, wkv_ref, kc_ref, ks_ref,
             qcat_ref, kcat_ref, v_ref, cqn_ref, ckvn_ref):
        cq = cq_ref[...]
        cqn = (cq * lax.rsqrt(jnp.mean(cq * cq, axis=-1, keepdims=True) + EPS) * gq_ref[...]).astype(MXU_DTYPE)
        cqn_ref[...] = cqn
        qpre = _dot(cqn, wq_ref[...], 1, 1)
        kc, ks = kc_ref[...], ks_ref[...]
        for hh in range(MLA_HEADS):
            qcat_ref[:, hh * 256:hh * 256 + 128] = qpre[:, hh * 256:hh * 256 + 128].astype(MXU_DTYPE)
            blk = qpre[:, hh * 256 + 128:(hh + 1) * 256]
            qcat_ref[:, hh * 256 + 128:(hh + 1) * 256] = (blk * kc + _rope_partner(blk, 0, 128) * ks).astype(MXU_DTYPE)
        ckv = ckv_ref[...]
        ckvn = (ckv * lax.rsqrt(jnp.mean(ckv * ckv, axis=-1, keepdims=True) + EPS) * gkv_ref[...]).astype(MXU_DTYPE)
        ckvn_ref[...] = ckvn
        kv = _dot(ckvn, wkv_ref[...], 1, 1)
        kr = kr_ref[...]
        krr = (kr * kc + _rope_partner(kr, 0, 128) * ks).astype(MXU_DTYPE)
        for hh in range(MLA_HEADS):
            kcat_ref[:, hh * 256:hh * 256 + 128] = kv[:, hh * 128:(hh + 1) * 128].astype(MXU_DTYPE)
            kcat_ref[:, hh * 256 + 128:(hh + 1) * 256] = krr
            v_ref[:, hh * 256:hh * 256 + 128] = kv[:, 1024 + hh * 128:1024 + (hh + 1) * 128].astype(MXU_DTYPE)
            v_ref[:, hh * 256 + 128:(hh + 1) * 256] = jnp.ones((tm, 128), MXU_DTYPE)

    row = lambda w, c: pl.BlockSpec((tm, w), lambda i: (i, c))
    full = lambda a: pl.BlockSpec(a.shape, lambda i: (0, 0))
    of_layer = lambda a: _layer_spec(a.shape[-2:], lambda i: (0, 0), layer)
    return pl.pallas_call(
        body, name=name, grid=(s // tm,),
        in_specs=[row(Q_RANK, CQ_OFF // Q_RANK), row(KV_RANK, CKV_OFF // KV_RANK), row(128, KR_OFF // 128),
                  full(gq), full(gkv), of_layer(wq), of_layer(wkv), row(128, 0), row(128, 0)],
        out_specs=[row(2048, 0), row(2048, 0), row(2048, 0), row(Q_RANK, 0), row(KV_RANK, 0)],
        out_shape=[jax.ShapeDtypeStruct((s, 2048), MXU_DTYPE), jax.ShapeDtypeStruct((s, 2048), MXU_DTYPE),
                   jax.ShapeDtypeStruct((s, 2048), MXU_DTYPE), jax.ShapeDtypeStruct((s, Q_RANK), MXU_DTYPE),
                   jax.ShapeDtypeStruct((s, KV_RANK), MXU_DTYPE)],
        compiler_params=_cparams("parallel"),
    )(proj, proj, proj, gq, gkv, wq, wkv, tk_c, tk_s)


def _norm_bwd(x, g, dy):
    r = lax.rsqrt(jnp.mean(x * x, axis=-1, keepdims=True) + EPS)
    xn = x * r
    u = dy * g
    return r * (u - xn * jnp.mean(u * xn, axis=-1, keepdims=True)), jnp.sum(dy * xn, axis=0, keepdims=True)


def _mla_qkv_bwd(proj, cqn, ckvn, dqcat, dkcat, dv, dka, dva, gq, gkv, wq, wkv, tk_c, tk_s, *, layer, name):
    s = proj.shape[0]
    tm = min(512, s)
    t_cq, t_ka, t_ckv, t_va, t_kr = (o - CQ_OFF for o in (CQ_OFF, KA_OFF, CKV_OFF, VA_OFF, KR_OFF))

    def body(cq_ref, ckv_ref, cqn_ref, ckvn_ref, dq_ref, dk_ref, dv_ref, dka_ref, dva_ref, gq_ref, gkv_ref, wq_ref,
             wkv_ref, kc_ref, ks_ref,
             tile_ref, dwq_ref, dwkv_ref, dgq_ref, dgkv_ref, dqpre, dkv, dwq_acc, dwkv_acc):
        dcq_ref = tile_ref.at[:, t_cq:t_cq + Q_RANK]
        dckv_ref = tile_ref.at[:, t_ckv:t_ckv + KV_RANK]
        dkr_ref = tile_ref.at[:, t_kr:t_kr + 128]
        tile_ref[:, t_ka:t_ka + 128] = dka_ref[...]
        tile_ref[:, t_va:t_va + 128] = dva_ref[...]

        @pl.when(pl.program_id(0) == 0)
        def _():
            for r in (dwq_acc, dwkv_acc, dgq_ref, dgkv_ref):
                r[...] = jnp.zeros_like(r)

        kc, ks = kc_ref[...], ks_ref[...]
        dkrr = jnp.zeros((tm, 128), F32)
        for hh in range(MLA_HEADS):
            dqpre[:, hh * 256:hh * 256 + 128] = dq_ref[:, hh * 256:hh * 256 + 128].astype(MXU_DTYPE)
            blk = dq_ref[:, hh * 256 + 128:(hh + 1) * 256]
            dqpre[:, hh * 256 + 128:(hh + 1) * 256] = (blk * kc + _rope_partner(blk * ks, 0, 128)).astype(MXU_DTYPE)
            dkv[:, hh * 128:(hh + 1) * 128] = dk_ref[:, hh * 256:hh * 256 + 128].astype(MXU_DTYPE)
            dkrr = dkrr + dk_ref[:, hh * 256 + 128:(hh + 1) * 256]
        dkv[:, 1024:] = dv_ref[...].astype(MXU_DTYPE)
        dkr_ref[...] = (dkrr * kc + _rope_partner(dkrr * ks, 0, 128)).astype(dkr_ref.dtype)

        dq_b = dqpre[...]
        dwq_acc[...] += _dot(cqn_ref[...], dq_b, 0, 0)
        dcq, dgq = _norm_bwd(cq_ref[...], gq_ref[...], _dot(dq_b, wq_ref[...], 1, 0))
        dcq_ref[...] = dcq.astype(dcq_ref.dtype)
        dgq_ref[...] += dgq

        dkv_b = dkv[...]
        dwkv_acc[...] += _dot(ckvn_ref[...], dkv_b, 0, 0)
        dckv, dgkv = _norm_bwd(ckv_ref[...], gkv_ref[...], _dot(dkv_b, wkv_ref[...], 1, 0))
        dckv_ref[...] = dckv.astype(dckv_ref.dtype)
        dgkv_ref[...] += dgkv

        @pl.when(pl.program_id(0) == s // tm - 1)
        def _():
            dwq_ref[...] = dwq_acc[...].T
            dwkv_ref[...] = dwkv_acc[...].T

    row = lambda w, c: pl.BlockSpec((tm, w), lambda i: (i, c))
    full = lambda shape: pl.BlockSpec(shape, lambda i: (0, 0))
    of_layer = lambda a: _layer_spec(a.shape[-2:], lambda i: (0, 0), layer)
    return pl.pallas_call(
        body, name=name, grid=(s // tm,),
        in_specs=[row(Q_RANK, CQ_OFF // Q_RANK), row(KV_RANK, CKV_OFF // KV_RANK), row(Q_RANK, 0), row(KV_RANK, 0),
                  row(2048, 0), row(2048, 0), row(1024, 0), row(128, 0), row(128, 0), full(gq.shape), full(gkv.shape),
                  of_layer(wq), of_layer(wkv), row(128, 0), row(128, 0)],
        out_specs=[row(1024, 0), full(wq.shape[-2:]), full(wkv.shape[-2:]), full(gq.shape), full(gkv.shape)],
        out_shape=[jax.ShapeDtypeStruct((s, 1024), MXU_DTYPE), jax.ShapeDtypeStruct(wq.shape[-2:], F32),
                   jax.ShapeDtypeStruct(wkv.shape[-2:], F32), jax.ShapeDtypeStruct(gq.shape, F32),
                   jax.ShapeDtypeStruct(gkv.shape, F32)],
        scratch_shapes=[pltpu.VMEM((tm, 2048), MXU_DTYPE), pltpu.VMEM((tm, 2048), MXU_DTYPE),
                        pltpu.VMEM((Q_RANK, 2048), F32), pltpu.VMEM((KV_RANK, 2048), F32)],
        compiler_params=_cparams("arbitrary"),
    )(proj, proj, cqn, ckvn, dqcat, dkcat, dv, dka, dva, gq, gkv, wq, wkv, tk_c, tk_s)


def _loop_by_two(lo, hi, step):
    n = hi - lo

    def four(i, carry):
        for u in range(4):
            step(lo + 4 * i + u)
        return carry

    lax.fori_loop(0, n // 4, four, 0)
    rest = lo + (n // 4) * 4

    @pl.when(n % 4 >= 2)
    def _():
        step(rest)
        step(rest + 1)

    @pl.when(n % 2 == 1)
    def _():
        step(hi - 1)


def _causal_mask(t):
    return lax.broadcasted_iota(jnp.int32, (t, t), 1) <= lax.broadcasted_iota(jnp.int32, (t, t), 0)


def _mla_fwd(qcat, kcat, v, *, name, gather=None):
    s = qcat.shape[0]
    t = min(512, s)
    nq = s // t
    hp = HEADS_PER_STEP
    ng = MLA_HEADS // hp
    c2 = MLA_SCALE * LOG2E

    nw = 0 if gather is None else len(gather)

    def body(q_ref, k_ref, v_ref, *rest):
        src, (o_ref, lse_ref), out = rest[:nw], rest[nw:nw + 2], rest[nw + 2:2 * nw + 2]
        top_s, acc_s, *sems = rest[2 * nw + 2:]
        g, qi = pl.program_id(0), pl.program_id(1)
        if nw:
            @pl.when((g == 0) & (qi == 0))
            def _():
                _gather_start(src, out, *sems)

            @pl.when((g == ng - 1) & (qi == 0))
            def _():
                _gather_forward(src, out, *sems)

        rows = lambda j: pl.ds(pl.multiple_of(j * t, t), t)
        head = lambda e: slice(e * 256, (e + 1) * 256)
        raw = lambda e, j: _dot(q_ref[:, head(e)], k_ref[rows(j), head(e)], 1, 1)

        for e in range(hp):
            top_s[e] = jnp.where(_causal_mask(t), raw(e, qi), NEG)

        def pass1(j):
            for e in range(hp):
                top_s[e] = jnp.maximum(top_s[e], raw(e, j))

        _loop_by_two(0, qi, pass1)
        m = [jnp.max(top_s[e], axis=-1, keepdims=True) * c2 for e in range(hp)]

        def weighted(e, j, masked):
            sc = raw(e, j) * c2 - m[e]
            if masked:
                sc = jnp.where(_causal_mask(t), sc, NEG)
            return jnp.dot(jnp.exp2(sc).astype(MXU_DTYPE), v_ref[rows(j), head(e)], preferred_element_type=F32)

        for e in range(hp):
            acc_s[e] = weighted(e, qi, True)

        def pass2(j):
            for e in range(hp):
                acc_s[e] += weighted(e, j, False)

        _loop_by_two(0, qi, pass2)
        lane = lax.broadcasted_iota(jnp.int32, (t, 128), 1)
        stats = jnp.zeros((t, 128), F32)
        for e in range(hp):
            l = acc_s[e, :, 128:]
            o_ref[:, e * 128:(e + 1) * 128] = acc_s[e, :, :128] / l
            stats = jnp.where(lane == e, m[e] + jnp.log(l) * LOG2E, stats)
        lse_ref[...] = stats
        if nw:
            @pl.when((g == ng - 1) & (qi == nq - 1))
            def _():
                _gather_finish(src, out, *sems)

    outs = pl.pallas_call(
        body, name=name, grid=(ng, nq),
        in_specs=[pl.BlockSpec((t, 256 * hp), lambda g, qi: (qi, g)), pl.BlockSpec((s, 256 * hp), lambda g, qi: (0, g)),
                  pl.BlockSpec((s, 256 * hp), lambda g, qi: (0, g))] + [ANY] * nw,
        out_specs=[pl.BlockSpec((t, 128 * hp), lambda g, qi: (qi, g)), pl.BlockSpec((t, 128), lambda g, qi: (qi, g))]
        + [ANY] * nw,
        out_shape=[jax.ShapeDtypeStruct((s, 1024), F32), jax.ShapeDtypeStruct((s, 128 * ng), F32)]
        + (_gathered_shapes(gather) if nw else []),
        scratch_shapes=[pltpu.VMEM((hp, t, t), F32), pltpu.VMEM((hp, t, 256), F32)] + (_sems(6, nw) if nw else []),
        compiler_params=_cparams("arbitrary", "arbitrary"),
    )(qcat, kcat, v, *(gather or []))
    return outs[0], outs[1], list(outs[2:])


def _mla_bwd(qcat, kcat, v, do, lse, delta, *, name, scatter=None):
    s = qcat.shape[0]
    t = min(512, s)
    nq = s // t
    hp = HEADS_PER_STEP
    c2 = MLA_SCALE * LOG2E
    nw = 0 if scatter is None else len(scatter)
    kpb = 2 if nq % 2 == 0 else 1

    def body(q_ref, k_ref, v_ref, do_ref, lse_ref, dl_ref, *rest):
        src, (dq_ref, dk_ref, dv_ref), out = rest[:nw], rest[nw:nw + 3], rest[nw + 3:2 * nw + 3]
        dk_acc, dv_acc, *sems = rest[2 * nw + 3:]
        h, kb = pl.program_id(0), pl.program_id(1)
        if nw:
            @pl.when((h == 0) & (kb == 0))
            def _():
                _scatter_start(src, out, *sems)

        @pl.when(kb == 0)
        def _():
            dq_ref[...] = jnp.zeros_like(dq_ref)

        mine = lax.broadcasted_iota(jnp.int32, (t, 128), 1) == h % hp
        for sub in range(kpb):
            ki = kb * kpb + sub
            own = slice(sub * t, (sub + 1) * t)
            dk_acc[...] = jnp.zeros_like(dk_acc)
            dv_acc[...] = jnp.zeros_like(dv_acc)
            k, vv = k_ref[own, :], v_ref[own, :]

            def chunk(qi, masked):
                rows = pl.ds(pl.multiple_of(qi * t, t), t)
                q, dob = q_ref[rows, :], do_ref[rows, :]
                pick = lambda r: jnp.sum(jnp.where(mine, r[rows, :], 0.0), axis=-1, keepdims=True)
                sc = _dot(q, k, 1, 1) * c2
                if masked:
                    sc = jnp.where(_causal_mask(t), sc, NEG)
                p = jnp.exp2(sc - pick(lse_ref))
                dp = _dot(dob, vv, 1, 1)
                ds = (p * (dp - pick(dl_ref)) * MLA_SCALE).astype(MXU_DTYPE)
                dv_acc[...] += _dot(dob, p.astype(MXU_DTYPE), 0, 0)
                dk_acc[...] += _dot(q, ds, 0, 0)
                dq_ref[rows, :] += jnp.dot(ds, k, preferred_element_type=F32)

            chunk(ki, True)
            _loop_by_two(ki + 1, nq, lambda qi: chunk(qi, False))
            dk_ref[own, :] = dk_acc[...].T
            dv_ref[own, :] = dv_acc[...].T
        if nw:
            @pl.when((h == MLA_HEADS - 1) & (kb == nq // kpb - 1))
            def _():
                _scatter_finish(src, out, *sems)

    head = lambda w: pl.BlockSpec((s, w), lambda h, kb: (0, h))
    blk = lambda w: pl.BlockSpec((kpb * t, w), lambda h, kb: (kb, h))
    stat = pl.BlockSpec((s, 128), lambda h, kb: (0, h // hp))
    outs = pl.pallas_call(
        body, name=name, grid=(MLA_HEADS, nq // kpb),
        in_specs=[head(256), blk(256), pl.BlockSpec((kpb * t, 128), lambda h, kb: (kb, 2 * h)), head(128), stat, stat]
        + [ANY] * nw,
        out_specs=[head(256), blk(256), blk(128)] + [ANY] * nw,
        out_shape=[jax.ShapeDtypeStruct((s, 2048), F32), jax.ShapeDtypeStruct((s, 2048), F32),
                   jax.ShapeDtypeStruct((s, 1024), F32)] + [jax.ShapeDtypeStruct(a.shape, a.dtype) for a in scatter or []],
        scratch_shapes=[pltpu.VMEM((256, t), F32), pltpu.VMEM((128, t), F32)] + (_sems(3, nw) if nw else []),
        compiler_params=_cparams("arbitrary", "arbitrary"),
    )(qcat, kcat, v, do, lse, delta, *(scatter or []))
    return outs[0], outs[1], outs[2], list(outs[3:])


def _gate_specs(tm):
    half = lambda c: pl.BlockSpec((tm, 1024), lambda i: (i, c))
    return half(0), half(GA_OFF // 1024), half(GB_OFF // 1024)


def _gate_fwd(oa, ob, proj, *, name):
    s = oa.shape[0]
    tm = min(512, s)

    def body(oa_ref, ob_ref, ga_ref, gb_ref, y_ref):
        ga, gb = ga_ref[...], gb_ref[...]
        y_ref[:, :1024] = (oa_ref[...] * (ga * jax.nn.sigmoid(ga))).astype(MXU_DTYPE)
        y_ref[:, 1024:] = (ob_ref[...] * (gb * jax.nn.sigmoid(gb))).astype(MXU_DTYPE)

    o_spec, ga_spec, gb_spec = _gate_specs(tm)
    return pl.pallas_call(
        body, name=name, grid=(s // tm,), in_specs=[o_spec, o_spec, ga_spec, gb_spec],
        out_specs=pl.BlockSpec((tm, 2048), lambda i: (i, 0)),
        out_shape=jax.ShapeDtypeStruct((s, 2048), MXU_DTYPE),
        compiler_params=_cparams("parallel"),
    )(oa, ob, proj, proj)


def _gate_bwd(dy, oa, ob, proj, *, name):
    s = oa.shape[0]
    tm = min(512, s)

    def body(dy_ref, oa_ref, ob_ref, ga_ref, gb_ref, doa_ref, dob_ref, dga_ref, dgb_ref, dl_ref):
        def branch(dyv, o, g, do_ref, dg_ref):
            sg = jax.nn.sigmoid(g)
            do = dyv * (g * sg)
            do_ref[...] = do.astype(MXU_DTYPE)
            dg_ref[...] = (dyv * o * (sg * (1.0 + g * (1.0 - sg)))).astype(MXU_DTYPE)
            return do

        branch(dy_ref[:, :1024], oa_ref[...], ga_ref[...], doa_ref, dga_ref)
        ob = ob_ref[...]
        prod = branch(dy_ref[:, 1024:], ob, gb_ref[...], dob_ref, dgb_ref) * ob
        lane = lax.broadcasted_iota(jnp.int32, (tm, stat_w), 1)
        acc = jnp.zeros((tm, stat_w), F32)
        for hh in range(MLA_HEADS):
            at = (hh // HEADS_PER_STEP) * 128 + hh % HEADS_PER_STEP
            acc = jnp.where(lane == at, jnp.sum(prod[:, hh * 128:(hh + 1) * 128], axis=-1, keepdims=True), acc)
        dl_ref[...] = acc

    stat_w = 128 * (MLA_HEADS // HEADS_PER_STEP)
    o_spec, ga_spec, gb_spec = _gate_specs(tm)
    return pl.pallas_call(
        body, name=name, grid=(s // tm,),
        in_specs=[pl.BlockSpec((tm, 2048), lambda i: (i, 0)), o_spec, o_spec, ga_spec, gb_spec],
        out_specs=[o_spec, o_spec, o_spec, o_spec, pl.BlockSpec((tm, stat_w), lambda i: (i, 0))],
        out_shape=[jax.ShapeDtypeStruct((s, 1024), MXU_DTYPE)] * 4 + [jax.ShapeDtypeStruct((s, stat_w), F32)],
        compiler_params=_cparams("parallel"),
    )(dy, oa, ob, proj, proj)


def _row_block(rows, cols, itemsize=4, budget=2 << 20):
    fits = [tr for tr in range(16, rows + 1, 16) if rows % tr == 0 and tr * cols * itemsize <= budget]
    return fits[-1] if fits else rows


def _adamw(w, g, m, v, *, name):
    shape = w.shape
    rows, cols = shape[-2:]
    w3, g3, m3, v3 = (a.reshape((-1, rows, cols)) for a in (w, g, m, v))
    lead = w3.shape[0]
    tr = _row_block(rows, cols)

    def body(w_ref, g_ref, m_ref, v_ref, d_ref, mo_ref, vo_ref):
        gv = g_ref[...]
        mn = ADAM_B1 * m_ref[...] + (1.0 - ADAM_B1) * gv
        vn = ADAM_B2 * v_ref[...] + (1.0 - ADAM_B2) * jnp.square(gv)
        m_hat = mn / (1.0 - ADAM_B1 ** ADAM_STEP)
        v_hat = vn / (1.0 - ADAM_B2 ** ADAM_STEP)
        d_ref[...] = -ADAM_LR * (m_hat / (jnp.sqrt(v_hat) + ADAM_EPS) + ADAM_WD * w_ref[...])
        mo_ref[...] = mn
        vo_ref[...] = vn

    spec = pl.BlockSpec((None, tr, cols), lambda a, i: (a, i, 0))
    outs = pl.pallas_call(
        body, name=name, grid=(lead, rows // tr), in_specs=[spec] * 4, out_specs=[spec] * 3,
        out_shape=[jax.ShapeDtypeStruct((lead, rows, cols), F32)] * 3,
        compiler_params=_cparams("parallel", "parallel"),
    )(w3, g3, m3, v3)
    return tuple(o.reshape(shape) for o in outs)


def _pair_sum(where, grads, recv, *, name):
    layers, chips, _, rows, cols = grads.shape
    tr = _row_block(rows, cols)

    def body(where_ref, a_ref, b_ref, o_ref):
        o_ref[...] = (a_ref[...] + b_ref[...].astype(F32)).astype(WIRE_DTYPE)

    spec = pl.BlockSpec((None, None, tr, cols), lambda a, k, i, w: (a, k, i, 0))
    return pl.pallas_call(
        body, name=name,
        grid_spec=pltpu.PrefetchScalarGridSpec(
            num_scalar_prefetch=1, grid=(layers, chips, rows // tr),
            in_specs=[pl.BlockSpec((None, None, None, tr, cols), lambda a, k, i, w: (a, k, w[4], i, 0)), spec],
            out_specs=spec),
        out_shape=jax.ShapeDtypeStruct((layers, chips, rows, cols), WIRE_DTYPE),
        compiler_params=_cparams("parallel", "parallel", "parallel"),
    )(where, grads, recv)


def _chip_sum(where, grads, recv, parts, *, name):
    layers, _, _, rows, cols = grads.shape
    tr = _row_block(rows, cols)

    def body(where_ref, a_ref, b_ref, t0_ref, t1_ref, t2_ref, o_ref):
        total = a_ref[...] + b_ref[...].astype(F32)
        for t_ref in (t0_ref, t1_ref, t2_ref):
            total = total + t_ref[...].astype(F32)
        o_ref[...] = total

    slot = lambda j: pl.BlockSpec((None, None, tr, cols), lambda a, i, w: (a, w[j], i, 0))
    return pl.pallas_call(
        body, name=name,
        grid_spec=pltpu.PrefetchScalarGridSpec(
            num_scalar_prefetch=1, grid=(layers, rows // tr),
            in_specs=[pl.BlockSpec((None, None, None, tr, cols), lambda a, i, w: (a, w[0], w[4], i, 0)), slot(0), slot(1),
                      slot(2), slot(3)],
            out_specs=slot(4)),
        out_shape=jax.ShapeDtypeStruct((layers, 2, rows, cols), F32),
        compiler_params=_cparams("parallel", "parallel"),
    )(where, grads, recv, parts, parts, parts)


def _place():
    x, y, c = lax.axis_index("x"), lax.axis_index("y"), lax.axis_index("c")
    chips = [(1 - x, y), (x, 1 - y), (1 - x, 1 - y)]
    return x, y, c, chips


def _sems(*shape):
    return [pltpu.SemaphoreType.DMA(shape), pltpu.SemaphoreType.DMA(shape)]


OWNER_CORE = (0, 1, 1, 1)


def _gather_copy(src, out, send_sems, recv_sems, t, sem, slot, to, forward=False):
    n = src[t].shape[0]
    rows = out[t].at[pl.ds(pl.multiple_of(slot * n, 16), n)]
    return pltpu.make_async_remote_copy(src_ref=rows if forward else src[t], dst_ref=rows, send_sem=send_sems.at[sem, t],
                                        recv_sem=recv_sems.at[sem, t], device_id=to, device_id_type=MESH)


def _gather_start(src, out, send_sems, recv_sems):
    x, y, c, chips = _place()
    for t, owner in enumerate(OWNER_CORE):
        @pl.when(c == owner)
        def _():
            for j, chip in enumerate(chips):
                _gather_copy(src, out, send_sems, recv_sems, t, j, 2 * x + y, (*chip, c)).start()


def _gather_forward(src, out, send_sems, recv_sems):
    x, y, c, chips = _place()
    for t, owner in enumerate(OWNER_CORE):
        @pl.when(c == owner)
        def _():
            for j, (px, py) in enumerate(chips):
                _gather_copy(src, out, send_sems, recv_sems, t, j, 2 * px + py, (px, py, c)).wait_recv()
                _gather_copy(src, out, send_sems, recv_sems, t, 3 + j, 2 * px + py, (x, y, 1 - c), forward=True).start()


def _gather_finish(src, out, send_sems, recv_sems):
    x, y, c, chips = _place()
    slots = [2 * px + py for px, py in chips]
    for t, owner in enumerate(OWNER_CORE):
        copy = functools.partial(_gather_copy, src, out, send_sems, recv_sems, t)

        @pl.when(c == owner)
        def _():
            for j, (px, py) in enumerate(chips):
                copy(j, 2 * x + y, (px, py, c)).wait_send()
                copy(3 + j, slots[j], (x, y, 1 - c), forward=True).wait_send()

        @pl.when(c != owner)
        def _():
            for j in range(3):
                copy(3 + j, slots[j], (x, y, 1 - c), forward=True).wait_recv()


def _gathered_shapes(shards):
    return [jax.ShapeDtypeStruct((4 * a.shape[0], a.shape[1]), a.dtype) for a in shards]


def _comm_gather_layer(shards, *, name):
    nt = len(shards)

    def body(*refs):
        src, out, sems = refs[:nt], refs[nt:2 * nt], refs[2 * nt:]
        _gather_start(src, out, *sems)
        _gather_forward(src, out, *sems)
        _gather_finish(src, out, *sems)

    return pl.pallas_call(
        body, name=name, in_specs=[ANY] * nt, out_specs=[ANY] * nt, out_shape=_gathered_shapes(shards),
        scratch_shapes=_sems(6, nt),
    )(*shards)


def _comm_swap_sibling(bufs, *, name):
    nt = len(bufs)

    def body(*refs):
        src, out, (send_sems, recv_sems) = refs[:nt], refs[nt:2 * nt], refs[2 * nt:]
        x, y, c, _ = _place()
        cps = [pltpu.make_async_remote_copy(src_ref=src[t], dst_ref=out[t], send_sem=send_sems.at[t], recv_sem=recv_sems.at[t],
                                            device_id=(x, y, 1 - c), device_id_type=MESH) for t in range(nt)]
        for cp in cps:
            cp.start()
        for cp in cps:
            cp.wait()

    return pl.pallas_call(
        body, name=name, in_specs=[ANY] * nt, out_specs=[ANY] * nt,
        out_shape=[jax.ShapeDtypeStruct(a.shape, a.dtype) for a in bufs], scratch_shapes=_sems(nt),
    )(*bufs)


def _scatter_copy(src, out, send_sems, recv_sems, j, t, from_slot, to_slot, to):
    return pltpu.make_async_remote_copy(src_ref=src[t].at[:, from_slot], dst_ref=out[t].at[:, to_slot],
                                        send_sem=send_sems.at[j, t], recv_sem=recv_sems.at[j, t], device_id=to,
                                        device_id_type=MESH)


def _scatter_start(src, out, send_sems, recv_sems):
    x, y, c, chips = _place()
    for j, (px, py) in enumerate(chips):
        for t in range(len(src)):
            _scatter_copy(src, out, send_sems, recv_sems, j, t, 2 * px + py, 2 * x + y, (px, py, c)).start()


def _scatter_finish(src, out, send_sems, recv_sems):
    x, y, c, chips = _place()
    for j, (px, py) in enumerate(chips):
        for t in range(len(src)):
            _scatter_copy(src, out, send_sems, recv_sems, j, t, 2 * x + y, 2 * px + py, (px, py, c)).wait_recv()
    for j, (px, py) in enumerate(chips):
        for t in range(len(src)):
            _scatter_copy(src, out, send_sems, recv_sems, j, t, 2 * px + py, 2 * x + y, (px, py, c)).wait_send()


def _comm_scatter_chips(parts, *, name):
    nt = len(parts)

    def body(*refs):
        src, out, sems = refs[:nt], refs[nt:2 * nt], refs[2 * nt:]
        _scatter_start(src, out, *sems)
        _scatter_finish(src, out, *sems)

    return pl.pallas_call(
        body, name=name, in_specs=[ANY] * nt, out_specs=[ANY] * nt,
        out_shape=[jax.ShapeDtypeStruct(a.shape, a.dtype) for a in parts], scratch_shapes=_sems(3, nt),
    )(*parts)


def _comm_join_halves(bufs, *, name):
    nt = len(bufs)

    def body(*refs):
        src, out, (send_sems, recv_sems) = refs[:nt], refs[nt:2 * nt], refs[2 * nt:]
        x, y, c, _ = _place()
        copy = lambda t, half: pltpu.make_async_remote_copy(
            src_ref=src[t].at[:, c], dst_ref=out[t].at[:, half], send_sem=send_sems.at[t], recv_sem=recv_sems.at[t],
            device_id=(x, y, 1 - c), device_id_type=MESH)
        sends = [copy(t, c) for t in range(nt)]
        for cp in sends:
            cp.start()
        for t in range(nt):
            copy(t, 1 - c).wait_recv()
        for cp in sends:
            cp.wait_send()

    return pl.pallas_call(
        body, name=name, in_specs=[ANY] * nt, out_specs=[ANY] * nt,
        out_shape=[jax.ShapeDtypeStruct(a.shape, a.dtype) for a in bufs],
        input_output_aliases={t: t for t in range(nt)}, scratch_shapes=_sems(nt),
    )(*bufs)


def _comm_allreduce_small(part, *, name):
    rows, cols = part.shape

    def body(p_ref, o_ref, buf, send_sems, recv_sems):
        x, y, c, _ = _place()
        me = 4 * x + 2 * y + c
        buf[me] = p_ref[...]
        flip = lambda v, bit: 1 - v if bit else v
        peers = [(flip(x, d & 4), flip(y, d & 2), flip(c, d & 1)) for d in range(1, 8)]
        sends = []
        for j, peer in enumerate(peers):
            cp = pltpu.make_async_remote_copy(src_ref=buf.at[me], dst_ref=buf.at[me], send_sem=send_sems.at[j],
                                              recv_sem=recv_sems.at[j], device_id=peer, device_id_type=MESH)
            cp.start()
            sends.append(cp)
        for j, (px, py, pc) in enumerate(peers):
            pltpu.make_async_remote_copy(src_ref=buf.at[me], dst_ref=buf.at[4 * px + 2 * py + pc], send_sem=send_sems.at[j],
                                         recv_sem=recv_sems.at[j], device_id=(px, py, pc), device_id_type=MESH).wait_recv()
        for cp in sends:
            cp.wait_send()
        total = buf[0]
        for i in range(1, 8):
            total = total + buf[i]
        o_ref[...] = total

    vm = pl.BlockSpec(memory_space=pltpu.VMEM)
    return pl.pallas_call(
        body, name=name, in_specs=[vm], out_specs=vm, out_shape=jax.ShapeDtypeStruct((rows, cols), F32),
        scratch_shapes=[pltpu.VMEM((8, rows, cols), F32), pltpu.SemaphoreType.DMA((7,)), pltpu.SemaphoreType.DMA((7,))],
    )(part)


def _pad_in_rows(wt):
    r = lambda o, n: wt[..., o:o + n, :]
    kr = r(2944, 64)
    return jnp.concatenate([r(0, 1024), r(1280, 1024), r(3008, 1024), r(2304, Q_RANK), r(1024, 128), r(2688, KV_RANK),
                            r(1152, 128), kr, jnp.zeros_like(kr)], axis=-2)


def _unpad_in_rows(qa, ga, gb, mixed):
    cq, ka, ckv, va, kr = (mixed[..., o - CQ_OFF:o - CQ_OFF + n, :] for o, n in (
        (CQ_OFF, Q_RANK), (KA_OFF, 128), (CKV_OFF, KV_RANK), (VA_OFF, 128), (KR_OFF, 64)))
    return jnp.concatenate([qa, ka, va, ga, cq, ckv, kr, gb], axis=-2)


def _pad_q_rows(wt):
    lead, cols = wt.shape[:-2], wt.shape[-1]
    wt = jnp.pad(wt.reshape(lead + (MLA_HEADS, 192, cols)), [(0, 0)] * (len(lead) + 1) + [(0, 64), (0, 0)])
    return wt.reshape(lead + (MLA_HEADS * 256, cols))


def _unpad_q_rows(wt):
    lead, cols = wt.shape[:-2], wt.shape[-1]
    return wt.reshape(lead + (MLA_HEADS, 256, cols))[..., :192, :].reshape(lead + (MLA_HEADS * 192, cols))


def _perm_kv_rows(wt):
    lead, cols = wt.shape[:-2], wt.shape[-1]
    return jnp.swapaxes(wt.reshape(lead + (MLA_HEADS, 2, 128, cols)), -4, -3).reshape(lead + (2048, cols))


def _unperm_kv_rows(wt):
    lead, cols = wt.shape[:-2], wt.shape[-1]
    return jnp.swapaxes(wt.reshape(lead + (2, MLA_HEADS, 128, cols)), -4, -3).reshape(lead + (2048, cols))


def _t(a):
    return jnp.swapaxes(a, -1, -2)


def _rope_tables(s):
    pos = jnp.arange(s, dtype=F32)
    inv_freq = 10000.0 ** (-jnp.arange(0, 64, 2, dtype=F32) / 64)
    ang = pos[:, None] * inv_freq[None, :]
    cos, sin = jnp.cos(ang), jnp.sin(ang)
    z64 = jnp.zeros((s, 64), F32)
    tk_c = jnp.concatenate([cos, cos, z64], axis=-1)
    tk_s = jnp.concatenate([-sin, sin, z64], axis=-1)
    return tk_c, tk_s


def _layer_weights(gathered, own, chip):
    def with_own_rows(g, o):
        n = o.shape[0]
        return jnp.concatenate([lax.select(chip == j, o, g[j * n:(j + 1) * n]) for j in range(4)], axis=0)

    full_in, full_q, full_kv, full_o = (with_own_rows(g, o) for g, o in zip(gathered, own))
    return _pad_in_rows(full_in), _pad_q_rows(full_q), _perm_kv_rows(full_kv), full_o


def _device_step(xs, tgt, attn_g, sinks, gq, gkv, final_g, shards, where):
    chip = where[0]
    depth = shards[0].shape[0]
    s = xs.shape[0]
    tabs = _rope_tables(s)
    saved = []
    x = xs
    of_layer = lambda l: [a[l] for a in shards]
    gathered = _comm_gather_layer(of_layer(0), name="comm_gather_layer0")
    weights = []
    for l in range(depth):
        w_in_p, w_q_p, w_kv_p, w_o = _layer_weights(gathered, of_layer(l), chip)
        weights.append((w_in_p, w_q_p, w_kv_p, w_o))
        h = _rmsnorm_fwd(x, attn_g[l:l + 1], name=f"norm_fwd{l}")
        proj = _matmul(h, w_in_p, tb=True, name=f"in_proj{l}")
        oa, lse_a = _swa_fwd(proj, sinks[l], name=f"swa_fwd{l}")
        qcat, kcat, v, cqn, ckvn = _mla_qkv_fwd(proj, gq[l:l + 1], gkv[l:l + 1], w_q_p, w_kv_p, *tabs, layer=None,
                                                name=f"mla_qkv_fwd{l}")
        ob, lse_b, gathered = _mla_fwd(qcat, kcat, v, gather=of_layer(l + 1) if l + 1 < depth else None,
                                       name=f"mla_fwd{l}")
        y = _gate_fwd(oa, ob, proj, name=f"gate_fwd{l}")
        x_next = _matmul(y, w_o, add=x, name=f"out_proj{l}")
        saved.append((x, h, proj, oa, lse_a, qcat, kcat, v, cqn, ckvn, ob, lse_b, y))
        x = x_next

    dx, d_final_g, loss = _final_loss(x, final_g, tgt, name="final_loss")

    d_attn_g, d_sinks, d_gq, d_gkv = [None] * depth, [None] * depth, [None] * depth, [None] * depth
    d_w_in, d_w_q, d_w_kv, d_w_o = [None] * depth, [None] * depth, [None] * depth, [None] * depth
    half = depth // 2
    carried = {half - 1: list(range(half, depth)), **{l - 1: [l] for l in range(1, half)}}
    done = {}

    def grads_by_chip(layers):
        stack = lambda per_layer: jnp.stack([per_layer[l] for l in layers])
        in_tiles = [stack([tiles[j] if tiles else None for tiles in d_w_in]) for j in range(4)]
        full = [_unpad_in_rows(*in_tiles), _unpad_q_rows(stack(d_w_q)), _unperm_kv_rows(stack(d_w_kv)), stack(d_w_o)]
        return [g.reshape(g.shape[0], 4, 2, g.shape[1] // 8, g.shape[2]) for g in full]

    def reduce_begin(grads5, tag):
        give = [lax.dynamic_index_in_dim(g, 1 - where[4], axis=2, keepdims=False).astype(WIRE_DTYPE) for g in grads5]
        recv = _comm_swap_sibling(give, name=f"comm_swap_sibling_{tag}")
        return recv, [_pair_sum(where, g, r, name=f"pair_sum_{tag}_{t}") for t, (g, r) in enumerate(zip(grads5, recv))]

    def reduce_end(grads5, recv, parts, tag):
        mine = [_chip_sum(where, g, r, p, name=f"chip_sum_{tag}_{t}") for t, (g, r, p) in enumerate(zip(grads5, recv, parts))]
        joined = _comm_join_halves(mine, name=f"comm_join_halves_{tag}")
        return [a.reshape(a.shape[0], -1, a.shape[-1]) for a in joined]

    for l in reversed(range(depth)):
        x, h, proj, oa, lse_a, qcat, kcat, v, cqn, ckvn, ob, lse_b, y = saved[l]
        w_in_p, w_q_p, w_kv_p, w_o = weights[l]
        dy = _matmul(dx, w_o, tb=True, name=f"out_proj_dx{l}")
        d_w_o[l] = _matmul(y, dx, ta=True, name=f"out_proj_dw{l}")
        doa, dob, dga, dgb, delta_b = _gate_bwd(dy, oa, ob, proj, name=f"gate_bwd{l}")
        dqa, dka, dva, dsink = _swa_bwd(proj, sinks[l], lse_a, doa, name=f"swa_bwd{l}")
        phase, pair = carried.get(l), None
        if phase:
            grads5 = grads_by_chip(phase)
            recv, pair = reduce_begin(grads5, f"l{phase[0]}")
        dqc, dkc, dv, parts = _mla_bwd(qcat, kcat, v, dob, lse_b, delta_b, scatter=pair, name=f"mla_bwd{l}")
        if phase:
            done[phase[0]] = reduce_end(grads5, recv, parts, f"l{phase[0]}")
        mixed, d_w_q[l], d_w_kv[l], dgq_l, dgkv_l = _mla_qkv_bwd(
            proj, cqn, ckvn, dqc, dkc, dv, dka, dva, gq[l:l + 1], gkv[l:l + 1], w_q_p, w_kv_p, *tabs, layer=None,
            name=f"mla_qkv_bwd{l}")
        dproj = [dqa, dga, dgb, mixed]
        dh = _matmul_ktiles(dproj, w_in_p, name=f"in_proj_dx{l}")
        d_w_in[l] = [_matmul(tile, h, ta=True, name=f"in_proj_dw{l}_{j}") for j, tile in enumerate(dproj)]
        dx, dg_l = _rmsnorm_bwd(dh, x, attn_g[l:l + 1], dx, name=f"norm_bwd{l}")
        d_attn_g[l], d_sinks[l], d_gq[l], d_gkv[l] = dg_l, dsink[0:1, :SWA_HEADS], dgq_l, dgkv_l

    grads5 = grads_by_chip([0])
    recv, pair = reduce_begin(grads5, "l0")
    done[0] = reduce_end(grads5, recv, _comm_scatter_chips(pair, name="comm_scatter_chips_l0"), "l0")
    cat = lambda parts: jnp.concatenate(parts, axis=0)
    reduced = [cat([done[first][t] for first in sorted(done)]) for t in range(len(shards))]
    return loss, dx, cat(d_attn_g), cat(d_sinks), cat(d_gq), cat(d_gkv), d_final_g, reduced


def kernel(x, attn_norm_g, w_in, swa_sinks, q_a_norm_g, kv_a_norm_g, w_q_b, w_kv_b, w_out, final_norm_g, loss_target, m_attn_norm_g, m_w_in, m_swa_sinks, m_q_a_norm_g, m_kv_a_norm_g, m_w_q_b, m_w_kv_b, m_w_out, m_final_norm_g, v_attn_norm_g, v_w_in, v_swa_sinks, v_q_a_norm_g, v_kv_a_norm_g, v_w_q_b, v_w_kv_b, v_w_out, v_final_norm_g):
    x_, y_, c = lax.axis_index("x"), lax.axis_index("y"), lax.axis_index("c")
    chip = 2 * x_ + y_
    where = jnp.stack([chip, 2 * (1 - x_) + y_, 2 * x_ + 1 - y_, 2 * (1 - x_) + 1 - y_, c]).astype(jnp.int32)

    sent = [a.astype(WIRE_DTYPE) for a in (_t(w_in), _t(w_q_b), _t(w_kv_b), w_out)]

    loss, dx, d_attn_g, d_sinks, d_gq, d_gkv, d_final_g, reduced = _device_step(
        x[0], loss_target[0], attn_norm_g, swa_sinks, q_a_norm_g, kv_a_norm_g, final_norm_g.reshape(1, -1), sent, where)
    g_w_in, g_w_q_b, g_w_kv_b, g_w_out = _t(reduced[0]), _t(reduced[1]), _t(reduced[2]), reduced[3]

    small = [d_attn_g, d_sinks, d_gq, d_gkv, d_final_g, loss[:, :1]]
    flat = jnp.concatenate([a.reshape(-1) for a in small])
    n_small = flat.shape[0]
    rows = -(-n_small // 1024) * 8
    total = _comm_allreduce_small(jnp.pad(flat, (0, rows * 128 - n_small)).reshape(rows, 128),
                                  name="comm_allreduce_small").reshape(-1)
    outs, at = [], 0
    for a in small:
        outs.append(total[at:at + a.size].reshape(a.shape))
        at += a.size
    g_attn_g, g_sinks, g_gq, g_gkv, g_final_g, loss_total = outs
    g_final_g = g_final_g.reshape(final_norm_g.shape)

    weights = [attn_norm_g, w_in, swa_sinks, q_a_norm_g, kv_a_norm_g, w_q_b, w_kv_b, w_out, final_norm_g]
    grads = [g_attn_g, g_w_in, g_sinks, g_gq, g_gkv, g_w_q_b, g_w_kv_b, g_w_out, g_final_g]
    ms = [m_attn_norm_g, m_w_in, m_swa_sinks, m_q_a_norm_g, m_kv_a_norm_g, m_w_q_b, m_w_kv_b, m_w_out, m_final_norm_g]
    vs = [v_attn_norm_g, v_w_in, v_swa_sinks, v_q_a_norm_g, v_kv_a_norm_g, v_w_q_b, v_w_kv_b, v_w_out, v_final_norm_g]
    as2d = lambda a: a.reshape(1, -1) if a.ndim == 1 else a
    deltas, new_m, new_v = [], [], []
    for i, (w, g, m, v) in enumerate(zip(weights, grads, ms, vs)):
        view = _t if w is w_in else as2d
        d, mn, vn = _adamw(view(w), reduced[0] if w is w_in else view(g), view(m), view(v), name=f"adamw{i}")
        back = _t if w is w_in else (lambda a: a.reshape(w.shape))
        deltas.append(back(d))
        new_m.append(back(mn))
        new_v.append(back(vn))

    return (loss_total.reshape(()), dx[None], *grads, *deltas, *new_m, *new_v)
```

```python
import functools
import math

import jax
import jax.numpy as jnp
from jax import lax
from jax.experimental import pallas as pl
from jax.experimental.pallas import tpu as pltpu

F32 = jnp.float32
MXU_DTYPE = jnp.bfloat16
WIRE_DTYPE = jnp.bfloat16

EPS = 1e-6
NEG = -1e30
BLOCK = 128
D_MODEL = 2048
SWA_HEADS = 16
MLA_HEADS = 8
Q_RANK = 384
KV_RANK = 256
IN_WIDTH = 4032
MLA_SCALE = 192 ** -0.5
SWA_SCALE = 64 ** -0.5
LOG2E = math.log2(math.e)
HEADS_PER_STEP = 2
SLOPES = tuple(2.0 ** (-8.0 * (h + 1) / SWA_HEADS) for h in range(SWA_HEADS))

P_WIDTH = 4096
QA_OFF, GA_OFF, GB_OFF, CQ_OFF, KA_OFF, CKV_OFF, VA_OFF, KR_OFF = 0, 1024, 2048, 3072, 3456, 3584, 3840, 3968

ADAM_LR, ADAM_B1, ADAM_B2, ADAM_EPS, ADAM_WD, ADAM_STEP = 0.001, 0.9, 0.999, 1e-08, 0.01, 10

VMEM_LIMIT = 56 * 1024 * 1024
MESH = pl.DeviceIdType.MESH
ANY = pl.BlockSpec(memory_space=pl.ANY)


def _cparams(*sem):
    return pltpu.CompilerParams(dimension_semantics=sem, vmem_limit_bytes=VMEM_LIMIT)


def _dot(a, b, ca, cb):
    return lax.dot_general(a, b, (((ca,), (cb,)), ((), ())), preferred_element_type=F32)


def _layer_spec(block, index_map, layer):
    if layer is None:
        return pl.BlockSpec(block, index_map)
    return pl.BlockSpec((None,) + tuple(block), lambda *g: (layer,) + tuple(index_map(*g)))


def _matmul(a, b, *, name, ta=False, tb=False, out_dtype=F32, add=None, b_layer=None, tm=1024, tn=1024, tk=2048):
    (kdim, m) = a.shape if ta else a.shape[::-1]
    (n, k2) = b.shape[-2:] if tb else b.shape[-2:][::-1]
    assert kdim == k2, (a.shape, b.shape)
    tm, tn, tk = min(tm, m), min(tn, n), min(tk, kdim)
    assert m % tm == 0 and n % tn == 0 and kdim % tk == 0
    nk = kdim // tk

    def body(*refs):
        a_ref, b_ref = refs[:2]
        add_ref = None if add is None else refs[2]
        o_ref = refs[2 + (add is not None)]
        part = _dot(a_ref[...].astype(MXU_DTYPE), b_ref[...].astype(MXU_DTYPE), 0 if ta else 1, 1 if tb else 0)

        def finish(r):
            o_ref[...] = (r if add is None else add_ref[...] + r).astype(out_dtype)

        if nk == 1:
            finish(part)
            return
        acc = refs[-1]
        k = pl.program_id(2)

        @pl.when(k == 0)
        def _():
            acc[...] = part

        @pl.when((k > 0) & (k < nk - 1))
        def _():
            acc[...] += part

        @pl.when(k == nk - 1)
        def _():
            finish(acc[...] + part)

    a_spec = pl.BlockSpec((tk, tm), lambda i, j, k: (k, i)) if ta else pl.BlockSpec((tm, tk), lambda i, j, k: (i, k))
    b_spec = (_layer_spec((tn, tk), lambda i, j, k: (j, k), b_layer) if tb else
              _layer_spec((tk, tn), lambda i, j, k: (k, j), b_layer))
    in_specs, args = [a_spec, b_spec], [a, b]
    if add is not None:
        in_specs.append(pl.BlockSpec((tm, tn), lambda i, j, k: (i, j)))
        args.append(add)
    return pl.pallas_call(
        body, name=name, grid=(m // tm, n // tn, nk), in_specs=in_specs,
        out_specs=pl.BlockSpec((tm, tn), lambda i, j, k: (i, j)),
        out_shape=jax.ShapeDtypeStruct((m, n), out_dtype),
        scratch_shapes=[pltpu.VMEM((tm, tn), F32)] if nk > 1 else [],
        compiler_params=_cparams("parallel", "parallel", "arbitrary"),
    )(*args)


def _matmul_ktiles(a_tiles, b, *, name, b_layer=None, tm=1024, tn=1024):
    m, kt = a_tiles[0].shape
    n = b.shape[-1]
    nt = len(a_tiles)
    assert b.shape[-2] == nt * kt
    tm, tn = min(tm, m), min(tn, n)
    assert m % tm == 0 and n % tn == 0

    def body(*refs):
        a_refs, b_refs, o_ref = refs[:nt], refs[nt:2 * nt], refs[2 * nt]
        acc = _dot(a_refs[0][...].astype(MXU_DTYPE), b_refs[0][...].astype(MXU_DTYPE), 1, 0)
        for j in range(1, nt):
            acc += _dot(a_refs[j][...].astype(MXU_DTYPE), b_refs[j][...].astype(MXU_DTYPE), 1, 0)
        o_ref[...] = acc

    in_specs = [pl.BlockSpec((tm, kt), lambda jn, i: (i, 0))] * nt
    in_specs += [_layer_spec((kt, tn), lambda jn, i, j=j: (j, jn), b_layer) for j in range(nt)]
    return pl.pallas_call(
        body, name=name, grid=(n // tn, m // tm), in_specs=in_specs,
        out_specs=pl.BlockSpec((tm, tn), lambda jn, i: (i, jn)),
        out_shape=jax.ShapeDtypeStruct((m, n), F32),
        compiler_params=_cparams("parallel", "parallel"),
    )(*a_tiles, *([b] * nt))


def _rmsnorm_fwd(x, g, *, name):
    s, d = x.shape
    tm = min(512, s)

    def body(x_ref, g_ref, h_ref):
        xv = x_ref[...]
        r = lax.rsqrt(jnp.mean(xv * xv, axis=-1, keepdims=True) + EPS)
        h_ref[...] = (xv * r * g_ref[...]).astype(MXU_DTYPE)

    return pl.pallas_call(
        body, name=name, grid=(s // tm,),
        in_specs=[pl.BlockSpec((tm, d), lambda i: (i, 0)), pl.BlockSpec((1, d), lambda i: (0, 0))],
        out_specs=pl.BlockSpec((tm, d), lambda i: (i, 0)),
        out_shape=jax.ShapeDtypeStruct((s, d), MXU_DTYPE),
        compiler_params=_cparams("parallel"),
    )(x, g)


def _rmsnorm_bwd(dh, x, g, dres, *, name):
    s, d = x.shape
    tm = min(512, s)

    def body(dh_ref, x_ref, g_ref, dres_ref, dx_ref, dg_ref):
        @pl.when(pl.program_id(0) == 0)
        def _():
            dg_ref[...] = jnp.zeros_like(dg_ref)

        xv = x_ref[...]
        r = lax.rsqrt(jnp.mean(xv * xv, axis=-1, keepdims=True) + EPS)
        xn = xv * r
        dy = dh_ref[...]
        dg_ref[...] += jnp.sum(dy * xn, axis=0, keepdims=True)
        u = dy * g_ref[...]
        dx_ref[...] = dres_ref[...] + r * (u - xn * jnp.mean(u * xn, axis=-1, keepdims=True))

    row = pl.BlockSpec((tm, d), lambda i: (i, 0))
    vec = pl.BlockSpec((1, d), lambda i: (0, 0))
    return pl.pallas_call(
        body, name=name, grid=(s // tm,), in_specs=[row, row, vec, row], out_specs=[row, vec],
        out_shape=[jax.ShapeDtypeStruct((s, d), F32), jax.ShapeDtypeStruct((1, d), F32)],
        compiler_params=_cparams("arbitrary"),
    )(dh, x, g, dres)


def _final_loss(x, g, tgt, *, name):
    s, d = x.shape
    tm = min(512, s)

    def body(x_ref, g_ref, t_ref, dx_ref, dg_ref, loss_ref):
        @pl.when(pl.program_id(0) == 0)
        def _():
            dg_ref[...] = jnp.zeros_like(dg_ref)
            loss_ref[...] = jnp.zeros_like(loss_ref)

        xv = x_ref[...]
        gv = g_ref[...]
        r = lax.rsqrt(jnp.mean(xv * xv, axis=-1, keepdims=True) + EPS)
        xn = xv * r
        err = xn * gv - t_ref[...]
        sq = jnp.sum(jnp.sum(err * err, axis=-1, keepdims=True), axis=0, keepdims=True)
        loss_ref[...] += (0.5 / d) * sq
        dy = err * (1.0 / d)
        dg_ref[...] += jnp.sum(dy * xn, axis=0, keepdims=True)
        u = dy * gv
        dx_ref[...] = r * (u - xn * jnp.mean(u * xn, axis=-1, keepdims=True))

    row = pl.BlockSpec((tm, d), lambda i: (i, 0))
    vec = pl.BlockSpec((1, d), lambda i: (0, 0))
    return pl.pallas_call(
        body, name=name, grid=(s // tm,), in_specs=[row, vec, row],
        out_specs=[row, vec, pl.BlockSpec((1, 128), lambda i: (0, 0))],
        out_shape=[jax.ShapeDtypeStruct((s, d), F32), jax.ShapeDtypeStruct((1, d), F32),
                   jax.ShapeDtypeStruct((1, 128), F32)],
        compiler_params=_cparams("arbitrary"),
    )(x, g, tgt)


def _swa_keys(kp_ref, kc_ref):
    kk = jnp.concatenate([kp_ref[...], kc_ref[...]], axis=0)
    kr = pltpu.roll(kk, 64, 1)
    lo = lax.broadcasted_iota(jnp.int32, kk.shape, 1) < 64
    return [jnp.where(lo, kk, kr).astype(MXU_DTYPE), jnp.where(lo, kr, kk).astype(MXU_DTYPE)]


GROUP = SWA_HEADS // 2


def _swa_mask(n):
    qi = lax.broadcasted_iota(jnp.int32, (BLOCK, 2 * BLOCK), 0)
    ki = lax.broadcasted_iota(jnp.int32, (BLOCK, 2 * BLOCK), 1)
    delta = BLOCK + qi - ki
    valid = (delta >= 0) & (delta < BLOCK) & ((ki >= BLOCK) | (n > 0))
    return valid, delta.astype(F32)


def _stack_heads(ref, j):
    lo = lax.broadcasted_iota(jnp.int32, (BLOCK, BLOCK), 1) < 64
    parts = []
    for r in range(GROUP):
        pair = (GROUP * j + r) // 2
        blk = ref[:, pair * 128:(pair + 1) * 128].astype(F32)
        parts.append(jnp.where(lo if r % 2 == 0 else ~lo, blk, 0.0).astype(MXU_DTYPE))
    return jnp.concatenate(parts, axis=0)


def _unstack_heads(stacked, ref, j):
    lo = lax.broadcasted_iota(jnp.int32, (BLOCK, BLOCK), 1) < 64
    for i in range(GROUP // 2):
        pair = (GROUP * j) // 2 + i
        even, odd = stacked[2 * i * BLOCK:(2 * i + 1) * BLOCK], stacked[(2 * i + 1) * BLOCK:(2 * i + 2) * BLOCK]
        ref[:, pair * 128:(pair + 1) * 128] = jnp.where(lo, even, odd).astype(ref.dtype)


def _head_rows(stacked, r):
    return stacked[r * BLOCK:(r + 1) * BLOCK]


def _swa_scores(raw, h, valid, deltaf):
    return jnp.where(valid, raw * (SWA_SCALE * LOG2E) - (SLOPES[h] * LOG2E) * deltaf, NEG)


def _swa_specs(nb):
    kcol, vcol = KA_OFF // BLOCK, VA_OFF // BLOCK
    last = nb - 1
    cur = lambda n: jnp.minimum(n, last)
    prev = lambda n: jnp.maximum(jnp.minimum(n, last) - 1, 0)
    return [
        pl.BlockSpec(memory_space=pltpu.SMEM),
        pl.BlockSpec((BLOCK, 1024), lambda n: (cur(n), QA_OFF // 1024)),
        pl.BlockSpec((BLOCK, BLOCK), lambda n: (cur(n), kcol)),
        pl.BlockSpec((BLOCK, BLOCK), lambda n: (prev(n), kcol)),
        pl.BlockSpec((BLOCK, BLOCK), lambda n: (cur(n), vcol)),
        pl.BlockSpec((BLOCK, BLOCK), lambda n: (prev(n), vcol)),
    ]


def _swa_fwd(proj, sinks, *, name):
    s = proj.shape[0]
    nb = s // BLOCK

    def body(sink_ref, q_ref, kc_ref, kp_ref, vc_ref, vp_ref, o_ref, lse_ref):
        n = pl.program_id(0)
        keys = _swa_keys(kp_ref, kc_ref)
        vals = _swa_keys(vp_ref, vc_ref)
        valid, deltaf = _swa_mask(n)
        lane = lax.broadcasted_iota(jnp.int32, (BLOCK, BLOCK), 1)
        lse_acc = jnp.zeros((BLOCK, BLOCK), F32)
        for j in range(2):
            raw = _dot(_stack_heads(q_ref, j), keys[j], 1, 1)
            probs = []
            for r in range(GROUP):
                h = GROUP * j + r
                sc = _swa_scores(_head_rows(raw, r), h, valid, deltaf)
                sink = sink_ref[h] * LOG2E
                m = jnp.maximum(jnp.max(sc, axis=-1, keepdims=True), sink)
                p = jnp.exp2(sc - m)
                l = jnp.sum(p, axis=-1, keepdims=True) + jnp.exp2(sink - m)
                probs.append((p * (1.0 / l)).astype(MXU_DTYPE))
                lse_acc = jnp.where(lane == h, m + jnp.log(l) * LOG2E, lse_acc)
            _unstack_heads(jnp.dot(jnp.concatenate(probs, axis=0), vals[j], preferred_element_type=F32), o_ref, j)
        lse_ref[...] = lse_acc

    return pl.pallas_call(
        body, name=name, grid=(nb,), in_specs=_swa_specs(nb),
        out_specs=[pl.BlockSpec((BLOCK, 1024), lambda n: (n, 0)), pl.BlockSpec((BLOCK, BLOCK), lambda n: (n, 0))],
        out_shape=[jax.ShapeDtypeStruct((s, 1024), F32), jax.ShapeDtypeStruct((s, BLOCK), F32)],
        compiler_params=_cparams("parallel"),
    )(sinks, proj, proj, proj, proj, proj)


def _swa_bwd(proj, sinks, lse, do, *, name):
    s = proj.shape[0]
    nb = s // BLOCK
    last = nb - 1

    def body(sink_ref, q_ref, kc_ref, kp_ref, vc_ref, vp_ref, lse_ref, do_ref,
             dq_ref, dk_ref, dv_ref, dsink_ref, carry_k, carry_v):
        n = pl.program_id(0)

        @pl.when(n == 0)
        def _():
            carry_k[...] = jnp.zeros_like(carry_k)
            carry_v[...] = jnp.zeros_like(carry_v)
            dsink_ref[...] = jnp.zeros_like(dsink_ref)

        @pl.when(n < nb)
        def _():
            keys = _swa_keys(kp_ref, kc_ref)
            vals = _swa_keys(vp_ref, vc_ref)
            valid, deltaf = _swa_mask(n)
            lane = lax.broadcasted_iota(jnp.int32, (BLOCK, BLOCK), 1)
            lane1 = lax.broadcasted_iota(jnp.int32, (1, BLOCK), 1)
            lse_blk = lse_ref[...]
            acc_k, acc_v = [], []
            dsink = jnp.zeros((1, BLOCK), F32)
            for j in range(2):
                q_all, do_all = _stack_heads(q_ref, j), _stack_heads(do_ref, j)
                raw = _dot(q_all, keys[j], 1, 1)
                dp_all = _dot(do_all, vals[j], 1, 1)
                probs, dscores = [], []
                for r in range(GROUP):
                    h = GROUP * j + r
                    lse_h = jnp.sum(jnp.where(lane == h, lse_blk, 0.0), axis=-1, keepdims=True)
                    p = jnp.exp2(_swa_scores(_head_rows(raw, r), h, valid, deltaf) - lse_h)
                    dp = _head_rows(dp_all, r)
                    dlt = jnp.sum(dp * p, axis=-1, keepdims=True)
                    dscores.append((p * (dp - dlt) * SWA_SCALE).astype(MXU_DTYPE))
                    probs.append(p.astype(MXU_DTYPE))
                    sunk = jnp.exp2(sink_ref[h] * LOG2E - lse_h) * dlt
                    dsink = jnp.where(lane1 == h, -jnp.sum(sunk, axis=0, keepdims=True), dsink)
                ds_all = jnp.concatenate(dscores, axis=0)
                _unstack_heads(jnp.dot(ds_all, keys[j], preferred_element_type=F32), dq_ref, j)
                acc_k.append(_dot(ds_all, q_all, 0, 0))
                acc_v.append(_dot(jnp.concatenate(probs, axis=0), do_all, 0, 0))
            lo2 = lax.broadcasted_iota(jnp.int32, (2 * BLOCK, BLOCK), 1) < 64
            fold = lambda acc: jnp.where(lo2, acc[0] + pltpu.roll(acc[0], 64, 1), acc[1] + pltpu.roll(acc[1], 64, 1))
            dkk, dvv = fold(acc_k), fold(acc_v)
            dk_ref[...] = (carry_k[...] + dkk[:BLOCK]).astype(dk_ref.dtype)
            dv_ref[...] = (carry_v[...] + dvv[:BLOCK]).astype(dv_ref.dtype)
            carry_k[...] = dkk[BLOCK:]
            carry_v[...] = dvv[BLOCK:]
            dsink_ref[...] += jnp.broadcast_to(dsink, dsink_ref.shape)

        @pl.when(n == nb)
        def _():
            dk_ref[...] = carry_k[...].astype(dk_ref.dtype)
            dv_ref[...] = carry_v[...].astype(dv_ref.dtype)

    cur = lambda n: jnp.minimum(n, last)
    lag = lambda n: jnp.maximum(n - 1, 0)
    return pl.pallas_call(
        body, name=name, grid=(nb + 1,),
        in_specs=_swa_specs(nb) + [pl.BlockSpec((BLOCK, BLOCK), lambda n: (cur(n), 0)),
                                   pl.BlockSpec((BLOCK, 1024), lambda n: (cur(n), 0))],
        out_specs=[pl.BlockSpec((BLOCK, 1024), lambda n: (cur(n), 0)),
                   pl.BlockSpec((BLOCK, BLOCK), lambda n: (lag(n), 0)),
                   pl.BlockSpec((BLOCK, BLOCK), lambda n: (lag(n), 0)),
                   pl.BlockSpec((8, BLOCK), lambda n: (0, 0))],
        out_shape=[jax.ShapeDtypeStruct((s, 1024), MXU_DTYPE), jax.ShapeDtypeStruct((s, BLOCK), MXU_DTYPE),
                   jax.ShapeDtypeStruct((s, BLOCK), MXU_DTYPE), jax.ShapeDtypeStruct((8, BLOCK), F32)],
        scratch_shapes=[pltpu.VMEM((BLOCK, BLOCK), F32), pltpu.VMEM((BLOCK, BLOCK), F32)],
        compiler_params=_cparams("arbitrary"),
    )(sinks, proj, proj, proj, proj, proj, lse, do)


def _rope_partner(v, first, width):
    lane = lax.broadcasted_iota(jnp.int32, v.shape, 1)
    in_a = (lane >= first) & (lane < first + 32)
    in_b = (lane >= first + 32) & (lane < first + 64)
    return jnp.where(in_a, pltpu.roll(v, width - 32, 1), jnp.where(in_b, pltpu.roll(v, 32, 1), 0.0))


def _mla_qkv_fwd(proj, gq, gkv, wq, wkv, tk_c, tk_s, *, layer, name):
    s = proj.shape[0]
    tm = min(512, s)

    def body(cq_ref, ckv_ref, kr_ref, gq_ref, gkv_ref, wq_ref, wkv_ref, kc_ref, ks_ref,
             qcat_ref, kcat_ref, v_ref, cqn_ref, ckvn_ref):
        cq = cq_ref[...]
        cqn = (cq * lax.rsqrt(jnp.mean(cq * cq, axis=-1, keepdims=True) + EPS) * gq_ref[...]).astype(MXU_DTYPE)
        cqn_ref[...] = cqn
        qpre = _dot(cqn, wq_ref[...], 1, 1)
        kc, ks = kc_ref[...], ks_ref[...]
        for hh in range(MLA_HEADS):
            qcat_ref[:, hh * 256:hh * 256 + 128] = qpre[:, hh * 256:hh * 256 + 128].astype(MXU_DTYPE)
            blk = qpre[:, hh * 256 + 128:(hh + 1) * 256]
            qcat_ref[:, hh * 256 + 128:(hh + 1) * 256] = (blk * kc + _rope_partner(blk, 0, 128) * ks).astype(MXU_DTYPE)
        ckv = ckv_ref[...]
        ckvn = (ckv * lax.rsqrt(jnp.mean(ckv * ckv, axis=-1, keepdims=True) + EPS) * gkv_ref[...]).astype(MXU_DTYPE)
        ckvn_ref[...] = ckvn
        kv = _dot(ckvn, wkv_ref[...], 1, 1)
        kr = kr_ref[...]
        krr = (kr * kc + _rope_partner(kr, 0, 128) * ks).astype(MXU_DTYPE)
        for hh in range(MLA_HEADS):
            kcat_ref[:, hh * 256:hh * 256 + 128] = kv[:, hh * 128:(hh + 1) * 128].astype(MXU_DTYPE)
            kcat_ref[:, hh * 256 + 128:(hh + 1) * 256] = krr
            v_ref[:, hh * 256:hh * 256 + 128] = kv[:, 1024 + hh * 128:1024 + (hh + 1) * 128].astype(MXU_DTYPE)
            v_ref[:, hh * 256 + 128:(hh + 1) * 256] = jnp.ones((tm, 128), MXU_DTYPE)

    row = lambda w, c: pl.BlockSpec((tm, w), lambda i: (i, c))
    full = lambda a: pl.BlockSpec(a.shape, lambda i: (0, 0))
    of_layer = lambda a: _layer_spec(a.shape[-2:], lambda i: (0, 0), layer)
    return pl.pallas_call(
        body, name=name, grid=(s // tm,),
        in_specs=[row(Q_RANK, CQ_OFF // Q_RANK), row(KV_RANK, CKV_OFF // KV_RANK), row(128, KR_OFF // 128),
                  full(gq), full(gkv), of_layer(wq), of_layer(wkv), row(128, 0), row(128, 0)],
        out_specs=[row(2048, 0), row(2048, 0), row(2048, 0), row(Q_RANK, 0), row(KV_RANK, 0)],
        out_shape=[jax.ShapeDtypeStruct((s, 2048), MXU_DTYPE), jax.ShapeDtypeStruct((s, 2048), MXU_DTYPE),
                   jax.ShapeDtypeStruct((s, 2048), MXU_DTYPE), jax.ShapeDtypeStruct((s, Q_RANK), MXU_DTYPE),
                   jax.ShapeDtypeStruct((s, KV_RANK), MXU_DTYPE)],
        compiler_params=_cparams("parallel"),
    )(proj, proj, proj, gq, gkv, wq, wkv, tk_c, tk_s)


def _norm_bwd(x, g, dy):
    r = lax.rsqrt(jnp.mean(x * x, axis=-1, keepdims=True) + EPS)
    xn = x * r
    u = dy * g
    return r * (u - xn * jnp.mean(u * xn, axis=-1, keepdims=True)), jnp.sum(dy * xn, axis=0, keepdims=True)


def _mla_qkv_bwd(proj, cqn, ckvn, dqcat, dkcat, dv, dka, dva, gq, gkv, wq, wkv, tk_c, tk_s, *, layer, name):
    s = proj.shape[0]
    tm = min(512, s)
    t_cq, t_ka, t_ckv, t_va, t_kr = (o - CQ_OFF for o in (CQ_OFF, KA_OFF, CKV_OFF, VA_OFF, KR_OFF))

    def body(cq_ref, ckv_ref, cqn_ref, ckvn_ref, dq_ref, dk_ref, dv_ref, dka_ref, dva_ref, gq_ref, gkv_ref, wq_ref,
             wkv_ref, kc_ref, ks_ref,
             tile_ref, dwq_ref, dwkv_ref, dgq_ref, dgkv_ref, dqpre, dkv, dwq_acc, dwkv_acc):
        dcq_ref = tile_ref.at[:, t_cq:t_cq + Q_RANK]
        dckv_ref = tile_ref.at[:, t_ckv:t_ckv + KV_RANK]
        dkr_ref = tile_ref.at[:, t_kr:t_kr + 128]
        tile_ref[:, t_ka:t_ka + 128] = dka_ref[...]
        tile_ref[:, t_va:t_va + 128] = dva_ref[...]

        @pl.when(pl.program_id(0) == 0)
        def _():
            for r in (dwq_acc, dwkv_acc, dgq_ref, dgkv_ref):
                r[...] = jnp.zeros_like(r)

        kc, ks = kc_ref[...], ks_ref[...]
        dkrr = jnp.zeros((tm, 128), F32)
        for hh in range(MLA_HEADS):
            dqpre[:, hh * 256:hh * 256 + 128] = dq_ref[:, hh * 256:hh * 256 + 128].astype(MXU_DTYPE)
            blk = dq_ref[:, hh * 256 + 128:(hh + 1) * 256]
            dqpre[:, hh * 256 + 128:(hh + 1) * 256] = (blk * kc + _rope_partner(blk * ks, 0, 128)).astype(MXU_DTYPE)
            dkv[:, hh * 128:(hh + 1) * 128] = dk_ref[:, hh * 256:hh * 256 + 128].astype(MXU_DTYPE)
            dkrr = dkrr + dk_ref[:, hh * 256 + 128:(hh + 1) * 256]
        dkv[:, 1024:] = dv_ref[...].astype(MXU_DTYPE)
        dkr_ref[...] = (dkrr * kc + _rope_partner(dkrr * ks, 0, 128)).astype(dkr_ref.dtype)

        dq_b = dqpre[...]
        dwq_acc[...] += _dot(cqn_ref[...], dq_b, 0, 0)
        dcq, dgq = _norm_bwd(cq_ref[...], gq_ref[...], _dot(dq_b, wq_ref[...], 1, 0))
        dcq_ref[...] = dcq.astype(dcq_ref.dtype)
        dgq_ref[...] += dgq

        dkv_b = dkv[...]
        dwkv_acc[...] += _dot(ckvn_ref[...], dkv_b, 0, 0)
        dckv, dgkv = _norm_bwd(ckv_ref[...], gkv_ref[...], _dot(dkv_b, wkv_ref[...], 1, 0))
        dckv_ref[...] = dckv.astype(dckv_ref.dtype)
        dgkv_ref[...] += dgkv

        @pl.when(pl.program_id(0) == s // tm - 1)
        def _():
            dwq_ref[...] = dwq_acc[...].T
            dwkv_ref[...] = dwkv_acc[...].T

    row = lambda w, c: pl.BlockSpec((tm, w), lambda i: (i, c))
    full = lambda shape: pl.BlockSpec(shape, lambda i: (0, 0))
    of_layer = lambda a: _layer_spec(a.shape[-2:], lambda i: (0, 0), layer)
    return pl.pallas_call(
        body, name=name, grid=(s // tm,),
        in_specs=[row(Q_RANK, CQ_OFF // Q_RANK), row(KV_RANK, CKV_OFF // KV_RANK), row(Q_RANK, 0), row(KV_RANK, 0),
                  row(2048, 0), row(2048, 0), row(1024, 0), row(128, 0), row(128, 0), full(gq.shape), full(gkv.shape),
                  of_layer(wq), of_layer(wkv), row(128, 0), row(128, 0)],
        out_specs=[row(1024, 0), full(wq.shape[-2:]), full(wkv.shape[-2:]), full(gq.shape), full(gkv.shape)],
        out_shape=[jax.ShapeDtypeStruct((s, 1024), MXU_DTYPE), jax.ShapeDtypeStruct(wq.shape[-2:], F32),
                   jax.ShapeDtypeStruct(wkv.shape[-2:], F32), jax.ShapeDtypeStruct(gq.shape, F32),
                   jax.ShapeDtypeStruct(gkv.shape, F32)],
        scratch_shapes=[pltpu.VMEM((tm, 2048), MXU_DTYPE), pltpu.VMEM((tm, 2048), MXU_DTYPE),
                        pltpu.VMEM((Q_RANK, 2048), F32), pltpu.VMEM((KV_RANK, 2048), F32)],
        compiler_params=_cparams("arbitrary"),
    )(proj, proj, cqn, ckvn, dqcat, dkcat, dv, dka, dva, gq, gkv, wq, wkv, tk_c, tk_s)


def _loop_by_two(lo, hi, step):
    n = hi - lo

    def four(i, carry):
        for u in range(4):
            step(lo + 4 * i + u)
        return carry

    lax.fori_loop(0, n // 4, four, 0)
    rest = lo + (n // 4) * 4

    @pl.when(n % 4 >= 2)
    def _():
        step(rest)
        step(rest + 1)

    @pl.when(n % 2 == 1)
    def _():
        step(hi - 1)


def _causal_mask(t):
    return lax.broadcasted_iota(jnp.int32, (t, t), 1) <= lax.broadcasted_iota(jnp.int32, (t, t), 0)


def _mla_fwd(qcat, kcat, v, *, name, gather=None):
    s = qcat.shape[0]
    t = min(512, s)
    nq = s // t
    hp = HEADS_PER_STEP
    ng = MLA_HEADS // hp
    c2 = MLA_SCALE * LOG2E

    nw = 0 if gather is None else len(gather)

    def body(q_ref, k_ref, v_ref, *rest):
        src, (o_ref, lse_ref), out = rest[:nw], rest[nw:nw + 2], rest[nw + 2:2 * nw + 2]
        top_s, acc_s, *sems = rest[2 * nw + 2:]
        g, qb = pl.program_id(0), pl.program_id(1)
        if nw:
            @pl.when((g == 0) & (qb == 0))
            def _():
                _gather_start(src, out, *sems)

            @pl.when((g == ng - 1) & (qb == 0))
            def _():
                _gather_forward(src, out, *sems)

        rows = lambda j: pl.ds(pl.multiple_of(j * t, t), t)
        head = lambda e: slice(e * 256, (e + 1) * 256)
        lane = lax.broadcasted_iota(jnp.int32, (t, 128), 1)
        for sub in range(qpb):
            qi = qb * qpb + sub
            own = slice(sub * t, (sub + 1) * t)
            raw = lambda e, j: _dot(q_ref[own, head(e)], k_ref[rows(j), head(e)], 1, 1)

            for e in range(hp):
                top_s[e] = jnp.where(_causal_mask(t), raw(e, qi), NEG)

            def pass1(j):
                for e in range(hp):
                    top_s[e] = jnp.maximum(top_s[e], raw(e, j))

            _loop_by_two(0, qi, pass1)
            m = [jnp.max(top_s[e], axis=-1, keepdims=True) * c2 for e in range(hp)]

            def weighted(e, j, masked):
                sc = raw(e, j) * c2 - m[e]
                if masked:
                    sc = jnp.where(_causal_mask(t), sc, NEG)
                return jnp.dot(jnp.exp2(sc).astype(MXU_DTYPE), v_ref[rows(j), head(e)], preferred_element_type=F32)

            for e in range(hp):
                acc_s[e] = weighted(e, qi, True)

            def pass2(j):
                for e in range(hp):
                    acc_s[e] += weighted(e, j, False)

            _loop_by_two(0, qi, pass2)
            stats = jnp.zeros((t, 128), F32)
            for e in range(hp):
                l = acc_s[e, :, 128:]
                o_ref[own, e * 128:(e + 1) * 128] = acc_s[e, :, :128] / l
                stats = jnp.where(lane == e, m[e] + jnp.log(l) * LOG2E, stats)
            lse_ref[own, :] = stats
        if nw:
            @pl.when((g == ng - 1) & (qb == nq // qpb - 1))
            def _():
                _gather_finish(src, out, *sems)

    qpb = 2 if nq % 2 == 0 else 1
    outs = pl.pallas_call(
        body, name=name, grid=(ng, nq // qpb),
        in_specs=[pl.BlockSpec((qpb * t, 256 * hp), lambda g, qb: (qb, g)), pl.BlockSpec((s, 256 * hp), lambda g, qb: (0, g)),
                  pl.BlockSpec((s, 256 * hp), lambda g, qb: (0, g))] + [ANY] * nw,
        out_specs=[pl.BlockSpec((qpb * t, 128 * hp), lambda g, qb: (qb, g)),
                   pl.BlockSpec((qpb * t, 128), lambda g, qb: (qb, g))] + [ANY] * nw,
        out_shape=[jax.ShapeDtypeStruct((s, 1024), F32), jax.ShapeDtypeStruct((s, 128 * ng), F32)]
        + (_gathered_shapes(gather) if nw else []),
        scratch_shapes=[pltpu.VMEM((hp, t, t), F32), pltpu.VMEM((hp, t, 256), F32)] + (_sems(6, nw) if nw else []),
        compiler_params=_cparams("arbitrary", "arbitrary"),
    )(qcat, kcat, v, *(gather or []))
    return outs[0], outs[1], list(outs[2:])


def _mla_bwd(qcat, kcat, v, do, lse, delta, *, name, scatter=None):
    s = qcat.shape[0]
    t = min(512, s)
    nq = s // t
    hp = HEADS_PER_STEP
    c2 = MLA_SCALE * LOG2E
    nw = 0 if scatter is None else len(scatter)
    kpb = 2 if nq % 2 == 0 else 1

    def body(q_ref, k_ref, v_ref, do_ref, lse_ref, dl_ref, *rest):
        src, (dq_ref, dk_ref, dv_ref), out = rest[:nw], rest[nw:nw + 3], rest[nw + 3:2 * nw + 3]
        dk_acc, dv_acc, *sems = rest[2 * nw + 3:]
        h, kb = pl.program_id(0), pl.program_id(1)
        if nw:
            @pl.when((h == 0) & (kb == 0))
            def _():
                _scatter_start(src, out, *sems)

        @pl.when(kb == 0)
        def _():
            dq_ref[...] = jnp.zeros_like(dq_ref)

        mine = lax.broadcasted_iota(jnp.int32, (t, 128), 1) == h % hp
        for sub in range(kpb):
            ki = kb * kpb + sub
            own = slice(sub * t, (sub + 1) * t)
            dk_acc[...] = jnp.zeros_like(dk_acc)
            dv_acc[...] = jnp.zeros_like(dv_acc)
            k, vv = k_ref[own, :], v_ref[own, :]

            def chunk(qi, masked):
                rows = pl.ds(pl.multiple_of(qi * t, t), t)
                q, dob = q_ref[rows, :], do_ref[rows, :]
                pick = lambda r: jnp.sum(jnp.where(mine, r[rows, :], 0.0), axis=-1, keepdims=True)
                sc = _dot(q, k, 1, 1) * c2
                if masked:
                    sc = jnp.where(_causal_mask(t), sc, NEG)
                p = jnp.exp2(sc - pick(lse_ref))
                dp = _dot(dob, vv, 1, 1)
                ds = (p * (dp - pick(dl_ref)) * MLA_SCALE).astype(MXU_DTYPE)
                dv_acc[...] += _dot(dob, p.astype(MXU_DTYPE), 0, 0)
                dk_acc[...] += _dot(q, ds, 0, 0)
                dq_ref[rows, :] += jnp.dot(ds, k, preferred_element_type=F32)

            chunk(ki, True)
            _loop_by_two(ki + 1, nq, lambda qi: chunk(qi, False))
            dk_ref[own, :] = dk_acc[...].T
            dv_ref[own, :] = dv_acc[...].T
        if nw:
            @pl.when((h == MLA_HEADS - 1) & (kb == nq // kpb - 1))
            def _():
                _scatter_finish(src, out, *sems)

    head = lambda w: pl.BlockSpec((s, w), lambda h, kb: (0, h))
    blk = lambda w: pl.BlockSpec((kpb * t, w), lambda h, kb: (kb, h))
    stat = pl.BlockSpec((s, 128), lambda h, kb: (0, h // hp))
    outs = pl.pallas_call(
        body, name=name, grid=(MLA_HEADS, nq // kpb),
        in_specs=[head(256), blk(256), pl.BlockSpec((kpb * t, 128), lambda h, kb: (kb, 2 * h)), head(128), stat, stat]
        + [ANY] * nw,
        out_specs=[head(256), blk(256), blk(128)] + [ANY] * nw,
        out_shape=[jax.ShapeDtypeStruct((s, 2048), F32), jax.ShapeDtypeStruct((s, 2048), F32),
                   jax.ShapeDtypeStruct((s, 1024), F32)] + [jax.ShapeDtypeStruct(a.shape, a.dtype) for a in scatter or []],
        scratch_shapes=[pltpu.VMEM((256, t), F32), pltpu.VMEM((128, t), F32)] + (_sems(3, nw) if nw else []),
        compiler_params=_cparams("arbitrary", "arbitrary"),
    )(qcat, kcat, v, do, lse, delta, *(scatter or []))
    return outs[0], outs[1], outs[2], list(outs[3:])


def _gate_specs(tm):
    half = lambda c: pl.BlockSpec((tm, 1024), lambda i: (i, c))
    return half(0), half(GA_OFF // 1024), half(GB_OFF // 1024)


def _gate_fwd(oa, ob, proj, *, name):
    s = oa.shape[0]
    tm = min(512, s)

    def body(oa_ref, ob_ref, ga_ref, gb_ref, y_ref):
        ga, gb = ga_ref[...], gb_ref[...]
        y_ref[:, :1024] = (oa_ref[...] * (ga * jax.nn.sigmoid(ga))).astype(MXU_DTYPE)
        y_ref[:, 1024:] = (ob_ref[...] * (gb * jax.nn.sigmoid(gb))).astype(MXU_DTYPE)

    o_spec, ga_spec, gb_spec = _gate_specs(tm)
    return pl.pallas_call(
        body, name=name, grid=(s // tm,), in_specs=[o_spec, o_spec, ga_spec, gb_spec],
        out_specs=pl.BlockSpec((tm, 2048), lambda i: (i, 0)),
        out_shape=jax.ShapeDtypeStruct((s, 2048), MXU_DTYPE),
        compiler_params=_cparams("parallel"),
    )(oa, ob, proj, proj)


def _gate_bwd(dy, oa, ob, proj, *, name):
    s = oa.shape[0]
    tm = min(512, s)

    def body(dy_ref, oa_ref, ob_ref, ga_ref, gb_ref, doa_ref, dob_ref, dga_ref, dgb_ref, dl_ref):
        def branch(dyv, o, g, do_ref, dg_ref):
            sg = jax.nn.sigmoid(g)
            do = dyv * (g * sg)
            do_ref[...] = do.astype(MXU_DTYPE)
            dg_ref[...] = (dyv * o * (sg * (1.0 + g * (1.0 - sg)))).astype(MXU_DTYPE)
            return do

        branch(dy_ref[:, :1024], oa_ref[...], ga_ref[...], doa_ref, dga_ref)
        ob = ob_ref[...]
        prod = branch(dy_ref[:, 1024:], ob, gb_ref[...], dob_ref, dgb_ref) * ob
        lane = lax.broadcasted_iota(jnp.int32, (tm, stat_w), 1)
        acc = jnp.zeros((tm, stat_w), F32)
        for hh in range(MLA_HEADS):
            at = (hh // HEADS_PER_STEP) * 128 + hh % HEADS_PER_STEP
            acc = jnp.where(lane == at, jnp.sum(prod[:, hh * 128:(hh + 1) * 128], axis=-1, keepdims=True), acc)
        dl_ref[...] = acc

    stat_w = 128 * (MLA_HEADS // HEADS_PER_STEP)
    o_spec, ga_spec, gb_spec = _gate_specs(tm)
    return pl.pallas_call(
        body, name=name, grid=(s // tm,),
        in_specs=[pl.BlockSpec((tm, 2048), lambda i: (i, 0)), o_spec, o_spec, ga_spec, gb_spec],
        out_specs=[o_spec, o_spec, o_spec, o_spec, pl.BlockSpec((tm, stat_w), lambda i: (i, 0))],
        out_shape=[jax.ShapeDtypeStruct((s, 1024), MXU_DTYPE)] * 4 + [jax.ShapeDtypeStruct((s, stat_w), F32)],
        compiler_params=_cparams("parallel"),
    )(dy, oa, ob, proj, proj)


def _row_block(rows, cols, itemsize=4, budget=2 << 20):
    fits = [tr for tr in range(16, rows + 1, 16) if rows % tr == 0 and tr * cols * itemsize <= budget]
    return fits[-1] if fits else rows


def _adamw(w, g, m, v, *, name):
    shape = w.shape
    rows, cols = shape[-2:]
    w3, g3, m3, v3 = (a.reshape((-1, rows, cols)) for a in (w, g, m, v))
    lead = w3.shape[0]
    tr = _row_block(rows, cols)

    def body(w_ref, g_ref, m_ref, v_ref, d_ref, mo_ref, vo_ref):
        gv = g_ref[...]
        mn = ADAM_B1 * m_ref[...] + (1.0 - ADAM_B1) * gv
        vn = ADAM_B2 * v_ref[...] + (1.0 - ADAM_B2) * jnp.square(gv)
        m_hat = mn / (1.0 - ADAM_B1 ** ADAM_STEP)
        v_hat = vn / (1.0 - ADAM_B2 ** ADAM_STEP)
        d_ref[...] = -ADAM_LR * (m_hat / (jnp.sqrt(v_hat) + ADAM_EPS) + ADAM_WD * w_ref[...])
        mo_ref[...] = mn
        vo_ref[...] = vn

    spec = pl.BlockSpec((None, tr, cols), lambda a, i: (a, i, 0))
    outs = pl.pallas_call(
        body, name=name, grid=(lead, rows // tr), in_specs=[spec] * 4, out_specs=[spec] * 3,
        out_shape=[jax.ShapeDtypeStruct((lead, rows, cols), F32)] * 3,
        compiler_params=_cparams("parallel", "parallel"),
    )(w3, g3, m3, v3)
    return tuple(o.reshape(shape) for o in outs)


def _pair_sum(where, grads, recv, *, name):
    layers, chips, _, rows, cols = grads.shape
    tr = _row_block(rows, cols)

    def body(where_ref, a_ref, b_ref, o_ref):
        o_ref[...] = (a_ref[...] + b_ref[...].astype(F32)).astype(WIRE_DTYPE)

    spec = pl.BlockSpec((None, None, tr, cols), lambda a, k, i, w: (a, k, i, 0))
    return pl.pallas_call(
        body, name=name,
        grid_spec=pltpu.PrefetchScalarGridSpec(
            num_scalar_prefetch=1, grid=(layers, chips, rows // tr),
            in_specs=[pl.BlockSpec((None, None, None, tr, cols), lambda a, k, i, w: (a, k, w[4], i, 0)), spec],
            out_specs=spec),
        out_shape=jax.ShapeDtypeStruct((layers, chips, rows, cols), WIRE_DTYPE),
        compiler_params=_cparams("parallel", "parallel", "parallel"),
    )(where, grads, recv)


def _chip_sum(where, grads, recv, parts, *, name):
    layers, _, _, rows, cols = grads.shape
    tr = _row_block(rows, cols)

    def body(where_ref, a_ref, b_ref, t0_ref, t1_ref, t2_ref, o_ref):
        total = a_ref[...] + b_ref[...].astype(F32)
        for t_ref in (t0_ref, t1_ref, t2_ref):
            total = total + t_ref[...].astype(F32)
        o_ref[...] = total

    slot = lambda j: pl.BlockSpec((None, None, tr, cols), lambda a, i, w: (a, w[j], i, 0))
    return pl.pallas_call(
        body, name=name,
        grid_spec=pltpu.PrefetchScalarGridSpec(
            num_scalar_prefetch=1, grid=(layers, rows // tr),
            in_specs=[pl.BlockSpec((None, None, None, tr, cols), lambda a, i, w: (a, w[0], w[4], i, 0)), slot(0), slot(1),
                      slot(2), slot(3)],
            out_specs=slot(4)),
        out_shape=jax.ShapeDtypeStruct((layers, 2, rows, cols), F32),
        compiler_params=_cparams("parallel", "parallel"),
    )(where, grads, recv, parts, parts, parts)


def _place():
    x, y, c = lax.axis_index("x"), lax.axis_index("y"), lax.axis_index("c")
    chips = [(1 - x, y), (x, 1 - y), (1 - x, 1 - y)]
    return x, y, c, chips


def _sems(*shape):
    return [pltpu.SemaphoreType.DMA(shape), pltpu.SemaphoreType.DMA(shape)]


OWNER_CORE = (0, 1, 1, 1)


def _gather_copy(src, out, send_sems, recv_sems, t, sem, slot, to, forward=False):
    n = src[t].shape[0]
    rows = out[t].at[pl.ds(pl.multiple_of(slot * n, 16), n)]
    return pltpu.make_async_remote_copy(src_ref=rows if forward else src[t], dst_ref=rows, send_sem=send_sems.at[sem, t],
                                        recv_sem=recv_sems.at[sem, t], device_id=to, device_id_type=MESH)


def _gather_start(src, out, send_sems, recv_sems):
    x, y, c, chips = _place()
    for t, owner in enumerate(OWNER_CORE):
        @pl.when(c == owner)
        def _():
            for j, chip in enumerate(chips):
                _gather_copy(src, out, send_sems, recv_sems, t, j, 2 * x + y, (*chip, c)).start()


def _gather_forward(src, out, send_sems, recv_sems):
    x, y, c, chips = _place()
    for t, owner in enumerate(OWNER_CORE):
        @pl.when(c == owner)
        def _():
            for j, (px, py) in enumerate(chips):
                _gather_copy(src, out, send_sems, recv_sems, t, j, 2 * px + py, (px, py, c)).wait_recv()
                _gather_copy(src, out, send_sems, recv_sems, t, 3 + j, 2 * px + py, (x, y, 1 - c), forward=True).start()


def _gather_finish(src, out, send_sems, recv_sems):
    x, y, c, chips = _place()
    slots = [2 * px + py for px, py in chips]
    for t, owner in enumerate(OWNER_CORE):
        copy = functools.partial(_gather_copy, src, out, send_sems, recv_sems, t)

        @pl.when(c == owner)
        def _():
            for j, (px, py) in enumerate(chips):
                copy(j, 2 * x + y, (px, py, c)).wait_send()
                copy(3 + j, slots[j], (x, y, 1 - c), forward=True).wait_send()

        @pl.when(c != owner)
        def _():
            for j in range(3):
                copy(3 + j, slots[j], (x, y, 1 - c), forward=True).wait_recv()


def _gathered_shapes(shards):
    return [jax.ShapeDtypeStruct((4 * a.shape[0], a.shape[1]), a.dtype) for a in shards]


def _comm_gather_layer(shards, *, name):
    nt = len(shards)

    def body(*refs):
        src, out, sems = refs[:nt], refs[nt:2 * nt], refs[2 * nt:]
        _gather_start(src, out, *sems)
        _gather_forward(src, out, *sems)
        _gather_finish(src, out, *sems)

    return pl.pallas_call(
        body, name=name, in_specs=[ANY] * nt, out_specs=[ANY] * nt, out_shape=_gathered_shapes(shards),
        scratch_shapes=_sems(6, nt),
    )(*shards)


def _comm_swap_sibling(bufs, *, name):
    nt = len(bufs)

    def body(*refs):
        src, out, (send_sems, recv_sems) = refs[:nt], refs[nt:2 * nt], refs[2 * nt:]
        x, y, c, _ = _place()
        cps = [pltpu.make_async_remote_copy(src_ref=src[t], dst_ref=out[t], send_sem=send_sems.at[t], recv_sem=recv_sems.at[t],
                                            device_id=(x, y, 1 - c), device_id_type=MESH) for t in range(nt)]
        for cp in cps:
            cp.start()
        for cp in cps:
            cp.wait()

    return pl.pallas_call(
        body, name=name, in_specs=[ANY] * nt, out_specs=[ANY] * nt,
        out_shape=[jax.ShapeDtypeStruct(a.shape, a.dtype) for a in bufs], scratch_shapes=_sems(nt),
    )(*bufs)


def _scatter_copy(src, out, send_sems, recv_sems, j, t, from_slot, to_slot, to):
    return pltpu.make_async_remote_copy(src_ref=src[t].at[:, from_slot], dst_ref=out[t].at[:, to_slot],
                                        send_sem=send_sems.at[j, t], recv_sem=recv_sems.at[j, t], device_id=to,
                                        device_id_type=MESH)


def _scatter_start(src, out, send_sems, recv_sems):
    x, y, c, chips = _place()
    for j, (px, py) in enumerate(chips):
        for t in range(len(src)):
            _scatter_copy(src, out, send_sems, recv_sems, j, t, 2 * px + py, 2 * x + y, (px, py, c)).start()


def _scatter_finish(src, out, send_sems, recv_sems):
    x, y, c, chips = _place()
    for j, (px, py) in enumerate(chips):
        for t in range(len(src)):
            _scatter_copy(src, out, send_sems, recv_sems, j, t, 2 * x + y, 2 * px + py, (px, py, c)).wait_recv()
    for j, (px, py) in enumerate(chips):
        for t in range(len(src)):
            _scatter_copy(src, out, send_sems, recv_sems, j, t, 2 * px + py, 2 * x + y, (px, py, c)).wait_send()


def _comm_scatter_chips(parts, *, name):
    nt = len(parts)

    def body(*refs):
        src, out, sems = refs[:nt], refs[nt:2 * nt], refs[2 * nt:]
        _scatter_start(src, out, *sems)
        _scatter_finish(src, out, *sems)

    return pl.pallas_call(
        body, name=name, in_specs=[ANY] * nt, out_specs=[ANY] * nt,
        out_shape=[jax.ShapeDtypeStruct(a.shape, a.dtype) for a in parts], scratch_shapes=_sems(3, nt),
    )(*parts)


def _comm_join_halves(bufs, *, name):
    nt = len(bufs)

    def body(*refs):
        src, out, (send_sems, recv_sems) = refs[:nt], refs[nt:2 * nt], refs[2 * nt:]
        x, y, c, _ = _place()
        copy = lambda t, half: pltpu.make_async_remote_copy(
            src_ref=src[t].at[:, c], dst_ref=out[t].at[:, half], send_sem=send_sems.at[t], recv_sem=recv_sems.at[t],
            device_id=(x, y, 1 - c), device_id_type=MESH)
        sends = [copy(t, c) for t in range(nt)]
        for cp in sends:
            cp.start()
        for t in range(nt):
            copy(t, 1 - c).wait_recv()
        for cp in sends:
            cp.wait_send()

    return pl.pallas_call(
        body, name=name, in_specs=[ANY] * nt, out_specs=[ANY] * nt,
        out_shape=[jax.ShapeDtypeStruct(a.shape, a.dtype) for a in bufs],
        input_output_aliases={t: t for t in range(nt)}, scratch_shapes=_sems(nt),
    )(*bufs)


def _comm_allreduce_small(part, *, name):
    rows, cols = part.shape

    def body(p_ref, o_ref, buf, send_sems, recv_sems):
        x, y, c, _ = _place()
        me = 4 * x + 2 * y + c
        buf[me] = p_ref[...]
        flip = lambda v, bit: 1 - v if bit else v
        peers = [(flip(x, d & 4), flip(y, d & 2), flip(c, d & 1)) for d in range(1, 8)]
        sends = []
        for j, peer in enumerate(peers):
            cp = pltpu.make_async_remote_copy(src_ref=buf.at[me], dst_ref=buf.at[me], send_sem=send_sems.at[j],
                                              recv_sem=recv_sems.at[j], device_id=peer, device_id_type=MESH)
            cp.start()
            sends.append(cp)
        for j, (px, py, pc) in enumerate(peers):
            pltpu.make_async_remote_copy(src_ref=buf.at[me], dst_ref=buf.at[4 * px + 2 * py + pc], send_sem=send_sems.at[j],
                                         recv_sem=recv_sems.at[j], device_id=(px, py, pc), device_id_type=MESH).wait_recv()
        for cp in sends:
            cp.wait_send()
        total = buf[0]
        for i in range(1, 8):
            total = total + buf[i]
        o_ref[...] = total

    vm = pl.BlockSpec(memory_space=pltpu.VMEM)
    return pl.pallas_call(
        body, name=name, in_specs=[vm], out_specs=vm, out_shape=jax.ShapeDtypeStruct((rows, cols), F32),
        scratch_shapes=[pltpu.VMEM((8, rows, cols), F32), pltpu.SemaphoreType.DMA((7,)), pltpu.SemaphoreType.DMA((7,))],
    )(part)


def _pad_in_rows(wt):
    r = lambda o, n: wt[..., o:o + n, :]
    kr = r(2944, 64)
    return jnp.concatenate([r(0, 1024), r(1280, 1024), r(3008, 1024), r(2304, Q_RANK), r(1024, 128), r(2688, KV_RANK),
                            r(1152, 128), kr, jnp.zeros_like(kr)], axis=-2)


def _unpad_in_rows(qa, ga, gb, mixed):
    cq, ka, ckv, va, kr = (mixed[..., o - CQ_OFF:o - CQ_OFF + n, :] for o, n in (
        (CQ_OFF, Q_RANK), (KA_OFF, 128), (CKV_OFF, KV_RANK), (VA_OFF, 128), (KR_OFF, 64)))
    return jnp.concatenate([qa, ka, va, ga, cq, ckv, kr, gb], axis=-2)


def _pad_q_rows(wt):
    lead, cols = wt.shape[:-2], wt.shape[-1]
    wt = jnp.pad(wt.reshape(lead + (MLA_HEADS, 192, cols)), [(0, 0)] * (len(lead) + 1) + [(0, 64), (0, 0)])
    return wt.reshape(lead + (MLA_HEADS * 256, cols))


def _unpad_q_rows(wt):
    lead, cols = wt.shape[:-2], wt.shape[-1]
    return wt.reshape(lead + (MLA_HEADS, 256, cols))[..., :192, :].reshape(lead + (MLA_HEADS * 192, cols))


def _perm_kv_rows(wt):
    lead, cols = wt.shape[:-2], wt.shape[-1]
    return jnp.swapaxes(wt.reshape(lead + (MLA_HEADS, 2, 128, cols)), -4, -3).reshape(lead + (2048, cols))


def _unperm_kv_rows(wt):
    lead, cols = wt.shape[:-2], wt.shape[-1]
    return jnp.swapaxes(wt.reshape(lead + (2, MLA_HEADS, 128, cols)), -4, -3).reshape(lead + (2048, cols))


def _t(a):
    return jnp.swapaxes(a, -1, -2)


def _rope_tables(s):
    pos = jnp.arange(s, dtype=F32)
    inv_freq = 10000.0 ** (-jnp.arange(0, 64, 2, dtype=F32) / 64)
    ang = pos[:, None] * inv_freq[None, :]
    cos, sin = jnp.cos(ang), jnp.sin(ang)
    z64 = jnp.zeros((s, 64), F32)
    tk_c = jnp.concatenate([cos, cos, z64], axis=-1)
    tk_s = jnp.concatenate([-sin, sin, z64], axis=-1)
    return tk_c, tk_s


def _layer_weights(gathered, own, chip):
    def with_own_rows(g, o):
        n = o.shape[0]
        return jnp.concatenate([lax.select(chip == j, o, g[j * n:(j + 1) * n]) for j in range(4)], axis=0)

    full_in, full_q, full_kv, full_o = (with_own_rows(g, o) for g, o in zip(gathered, own))
    return _pad_in_rows(full_in), _pad_q_rows(full_q), _perm_kv_rows(full_kv), full_o


def _device_step(xs, tgt, attn_g, sinks, gq, gkv, final_g, shards, where):
    chip = where[0]
    depth = shards[0].shape[0]
    s = xs.shape[0]
    tabs = _rope_tables(s)
    saved = []
    x = xs
    of_layer = lambda l: [a[l] for a in shards]
    gathered = _comm_gather_layer(of_layer(0), name="comm_gather_layer0")
    weights = []
    for l in range(depth):
        w_in_p, w_q_p, w_kv_p, w_o = _layer_weights(gathered, of_layer(l), chip)
        weights.append((w_in_p, w_q_p, w_kv_p, w_o))
        h = _rmsnorm_fwd(x, attn_g[l:l + 1], name=f"norm_fwd{l}")
        proj = _matmul(h, w_in_p, tb=True, name=f"in_proj{l}")
        oa, lse_a = _swa_fwd(proj, sinks[l], name=f"swa_fwd{l}")
        qcat, kcat, v, cqn, ckvn = _mla_qkv_fwd(proj, gq[l:l + 1], gkv[l:l + 1], w_q_p, w_kv_p, *tabs, layer=None,
                                                name=f"mla_qkv_fwd{l}")
        ob, lse_b, gathered = _mla_fwd(qcat, kcat, v, gather=of_layer(l + 1) if l + 1 < depth else None,
                                       name=f"mla_fwd{l}")
        y = _gate_fwd(oa, ob, proj, name=f"gate_fwd{l}")
        x_next = _matmul(y, w_o, add=x, name=f"out_proj{l}")
        saved.append((x, h, proj, oa, lse_a, qcat, kcat, v, cqn, ckvn, ob, lse_b, y))
        x = x_next

    dx, d_final_g, loss = _final_loss(x, final_g, tgt, name="final_loss")

    d_attn_g, d_sinks, d_gq, d_gkv = [None] * depth, [None] * depth, [None] * depth, [None] * depth
    d_w_in, d_w_q, d_w_kv, d_w_o = [None] * depth, [None] * depth, [None] * depth, [None] * depth
    half = depth // 2
    carried = {half - 1: list(range(half, depth)), **{l - 1: [l] for l in range(1, half)}}
    done = {}

    def grads_by_chip(layers):
        stack = lambda per_layer: jnp.stack([per_layer[l] for l in layers])
        in_tiles = [stack([tiles[j] if tiles else None for tiles in d_w_in]) for j in range(4)]
        full = [_unpad_in_rows(*in_tiles), _unpad_q_rows(stack(d_w_q)), _unperm_kv_rows(stack(d_w_kv)), stack(d_w_o)]
        return [g.reshape(g.shape[0], 4, 2, g.shape[1] // 8, g.shape[2]) for g in full]

    def reduce_begin(grads5, tag):
        give = [lax.dynamic_index_in_dim(g, 1 - where[4], axis=2, keepdims=False).astype(WIRE_DTYPE) for g in grads5]
        recv = _comm_swap_sibling(give, name=f"comm_swap_sibling_{tag}")
        return recv, [_pair_sum(where, g, r, name=f"pair_sum_{tag}_{t}") for t, (g, r) in enumerate(zip(grads5, recv))]

    def reduce_end(grads5, recv, parts, tag):
        mine = [_chip_sum(where, g, r, p, name=f"chip_sum_{tag}_{t}") for t, (g, r, p) in enumerate(zip(grads5, recv, parts))]
        joined = _comm_join_halves(mine, name=f"comm_join_halves_{tag}")
        return [a.reshape(a.shape[0], -1, a.shape[-1]) for a in joined]

    for l in reversed(range(depth)):
        x, h, proj, oa, lse_a, qcat, kcat, v, cqn, ckvn, ob, lse_b, y = saved[l]
        w_in_p, w_q_p, w_kv_p, w_o = weights[l]
        dy = _matmul(dx, w_o, tb=True, name=f"out_proj_dx{l}")
        d_w_o[l] = _matmul(y, dx, ta=True, name=f"out_proj_dw{l}")
        doa, dob, dga, dgb, delta_b = _gate_bwd(dy, oa, ob, proj, name=f"gate_bwd{l}")
        dqa, dka, dva, dsink = _swa_bwd(proj, sinks[l], lse_a, doa, name=f"swa_bwd{l}")
        phase, pair = carried.get(l), None
        if phase:
            grads5 = grads_by_chip(phase)
            recv, pair = reduce_begin(grads5, f"l{phase[0]}")
        dqc, dkc, dv, parts = _mla_bwd(qcat, kcat, v, dob, lse_b, delta_b, scatter=pair, name=f"mla_bwd{l}")
        if phase:
            done[phase[0]] = reduce_end(grads5, recv, parts, f"l{phase[0]}")
        mixed, d_w_q[l], d_w_kv[l], dgq_l, dgkv_l = _mla_qkv_bwd(
            proj, cqn, ckvn, dqc, dkc, dv, dka, dva, gq[l:l + 1], gkv[l:l + 1], w_q_p, w_kv_p, *tabs, layer=None,
            name=f"mla_qkv_bwd{l}")
        dproj = [dqa, dga, dgb, mixed]
        dh = _matmul_ktiles(dproj, w_in_p, name=f"in_proj_dx{l}")
        d_w_in[l] = [_matmul(tile, h, ta=True, name=f"in_proj_dw{l}_{j}") for j, tile in enumerate(dproj)]
        dx, dg_l = _rmsnorm_bwd(dh, x, attn_g[l:l + 1], dx, name=f"norm_bwd{l}")
        d_attn_g[l], d_sinks[l], d_gq[l], d_gkv[l] = dg_l, dsink[0:1, :SWA_HEADS], dgq_l, dgkv_l

    grads5 = grads_by_chip([0])
    recv, pair = reduce_begin(grads5, "l0")
    done[0] = reduce_end(grads5, recv, _comm_scatter_chips(pair, name="comm_scatter_chips_l0"), "l0")
    cat = lambda parts: jnp.concatenate(parts, axis=0)
    reduced = [cat([done[first][t] for first in sorted(done)]) for t in range(len(shards))]
    return loss, dx, cat(d_attn_g), cat(d_sinks), cat(d_gq), cat(d_gkv), d_final_g, reduced


def kernel(x, attn_norm_g, w_in, swa_sinks, q_a_norm_g, kv_a_norm_g, w_q_b, w_kv_b, w_out, final_norm_g, loss_target, m_attn_norm_g, m_w_in, m_swa_sinks, m_q_a_norm_g, m_kv_a_norm_g, m_w_q_b, m_w_kv_b, m_w_out, m_final_norm_g, v_attn_norm_g, v_w_in, v_swa_sinks, v_q_a_norm_g, v_kv_a_norm_g, v_w_q_b, v_w_kv_b, v_w_out, v_final_norm_g):
    x_, y_, c = lax.axis_index("x"), lax.axis_index("y"), lax.axis_index("c")
    chip = 2 * x_ + y_
    where = jnp.stack([chip, 2 * (1 - x_) + y_, 2 * x_ + 1 - y_, 2 * (1 - x_) + 1 - y_, c]).astype(jnp.int32)

    sent = [a.astype(WIRE_DTYPE) for a in (_t(w_in), _t(w_q_b), _t(w_kv_b), w_out)]

    loss, dx, d_attn_g, d_sinks, d_gq, d_gkv, d_final_g, reduced = _device_step(
        x[0], loss_target[0], attn_norm_g, swa_sinks, q_a_norm_g, kv_a_norm_g, final_norm_g.reshape(1, -1), sent, where)
    g_w_in, g_w_q_b, g_w_kv_b, g_w_out = _t(reduced[0]), _t(reduced[1]), _t(reduced[2]), reduced[3]

    small = [d_attn_g, d_sinks, d_gq, d_gkv, d_final_g, loss[:, :1]]
    flat = jnp.concatenate([a.reshape(-1) for a in small])
    n_small = flat.shape[0]
    rows = -(-n_small // 1024) * 8
    total = _comm_allreduce_small(jnp.pad(flat, (0, rows * 128 - n_small)).reshape(rows, 128),
                                  name="comm_allreduce_small").reshape(-1)
    outs, at = [], 0
    for a in small:
        outs.append(total[at:at + a.size].reshape(a.shape))
        at += a.size
    g_attn_g, g_sinks, g_gq, g_gkv, g_final_g, loss_total = outs
    g_final_g = g_final_g.reshape(final_norm_g.shape)

    weights = [attn_norm_g, w_in, swa_sinks, q_a_norm_g, kv_a_norm_g, w_q_b, w_kv_b, w_out, final_norm_g]
    grads = [g_attn_g, g_w_in, g_sinks, g_gq, g_gkv, g_w_q_b, g_w_kv_b, g_w_out, g_final_g]
    ms = [m_attn_norm_g, m_w_in, m_swa_sinks, m_q_a_norm_g, m_kv_a_norm_g, m_w_q_b, m_w_kv_b, m_w_out, m_final_norm_g]
    vs = [v_attn_norm_g, v_w_in, v_swa_sinks, v_q_a_norm_g, v_kv_a_norm_g, v_w_q_b, v_w_kv_b, v_w_out, v_final_norm_g]
    as2d = lambda a: a.reshape(1, -1) if a.ndim == 1 else a
    deltas, new_m, new_v = [], [], []
    for i, (w, g, m, v) in enumerate(zip(weights, grads, ms, vs)):
        view = _t if w is w_in else as2d
        d, mn, vn = _adamw(view(w), reduced[0] if w is w_in else view(g), view(m), view(v), name=f"adamw{i}")
        back = _t if w is w_in else (lambda a: a.reshape(w.shape))
        deltas.append(back(d))
        new_m.append(back(mn))
        new_v.append(back(vn))

    return (loss_total.reshape(()), dx[None], *grads, *deltas, *new_m, *new_v)
```

```python
import functools
import math

import jax
import jax.numpy as jnp
from jax import lax
from jax.experimental import pallas as pl
from jax.experimental.pallas import tpu as pltpu

F32 = jnp.float32
MXU_DTYPE = jnp.bfloat16
WIRE_DTYPE = jnp.bfloat16

EPS = 1e-6
NEG = -1e30
BLOCK = 128
D_MODEL = 2048
SWA_HEADS = 16
MLA_HEADS = 8
Q_RANK = 384
KV_RANK = 256
IN_WIDTH = 4032
MLA_SCALE = 192 ** -0.5
SWA_SCALE = 64 ** -0.5
LOG2E = math.log2(math.e)
HEADS_PER_STEP = 2
SLOPES = tuple(2.0 ** (-8.0 * (h + 1) / SWA_HEADS) for h in range(SWA_HEADS))

P_WIDTH = 4096
QA_OFF, GA_OFF, GB_OFF, CQ_OFF, KA_OFF, CKV_OFF, VA_OFF, KR_OFF = 0, 1024, 2048, 3072, 3456, 3584, 3840, 3968

ADAM_LR, ADAM_B1, ADAM_B2, ADAM_EPS, ADAM_WD, ADAM_STEP = 0.001, 0.9, 0.999, 1e-08, 0.01, 10

VMEM_LIMIT = 56 * 1024 * 1024
MESH = pl.DeviceIdType.MESH
ANY = pl.BlockSpec(memory_space=pl.ANY)


def _cparams(*sem):
    return pltpu.CompilerParams(dimension_semantics=sem, vmem_limit_bytes=VMEM_LIMIT)


def _dot(a, b, ca, cb):
    return lax.dot_general(a, b, (((ca,), (cb,)), ((), ())), preferred_element_type=F32)


def _layer_spec(block, index_map, layer):
    if layer is None:
        return pl.BlockSpec(block, index_map)
    return pl.BlockSpec((None,) + tuple(block), lambda *g: (layer,) + tuple(index_map(*g)))


def _matmul(a, b, *, name, ta=False, tb=False, out_dtype=F32, add=None, b_layer=None, tm=1024, tn=1024, tk=2048):
    (kdim, m) = a.shape if ta else a.shape[::-1]
    (n, k2) = b.shape[-2:] if tb else b.shape[-2:][::-1]
    assert kdim == k2, (a.shape, b.shape)
    tm, tn, tk = min(tm, m), min(tn, n), min(tk, kdim)
    assert m % tm == 0 and n % tn == 0 and kdim % tk == 0
    nk = kdim // tk

    def body(*refs):
        a_ref, b_ref = refs[:2]
        add_ref = None if add is None else refs[2]
        o_ref = refs[2 + (add is not None)]
        part = _dot(a_ref[...].astype(MXU_DTYPE), b_ref[...].astype(MXU_DTYPE), 0 if ta else 1, 1 if tb else 0)

        def finish(r):
            o_ref[...] = (r if add is None else add_ref[...] + r).astype(out_dtype)

        if nk == 1:
            finish(part)
            return
        acc = refs[-1]
        k = pl.program_id(2)

        @pl.when(k == 0)
        def _():
            acc[...] = part

        @pl.when((k > 0) & (k < nk - 1))
        def _():
            acc[...] += part

        @pl.when(k == nk - 1)
        def _():
            finish(acc[...] + part)

    a_spec = pl.BlockSpec((tk, tm), lambda i, j, k: (k, i)) if ta else pl.BlockSpec((tm, tk), lambda i, j, k: (i, k))
    b_spec = (_layer_spec((tn, tk), lambda i, j, k: (j, k), b_layer) if tb else
              _layer_spec((tk, tn), lambda i, j, k: (k, j), b_layer))
    in_specs, args = [a_spec, b_spec], [a, b]
    if add is not None:
        in_specs.append(pl.BlockSpec((tm, tn), lambda i, j, k: (i, j)))
        args.append(add)
    return pl.pallas_call(
        body, name=name, grid=(m // tm, n // tn, nk), in_specs=in_specs,
        out_specs=pl.BlockSpec((tm, tn), lambda i, j, k: (i, j)),
        out_shape=jax.ShapeDtypeStruct((m, n), out_dtype),
        scratch_shapes=[pltpu.VMEM((tm, tn), F32)] if nk > 1 else [],
        compiler_params=_cparams("parallel", "parallel", "arbitrary"),
    )(*args)


def _matmul_ktiles(a_tiles, b, *, name, b_layer=None, tm=1024, tn=1024):
    m, kt = a_tiles[0].shape
    n = b.shape[-1]
    nt = len(a_tiles)
    assert b.shape[-2] == nt * kt
    tm, tn = min(tm, m), min(tn, n)
    assert m % tm == 0 and n % tn == 0

    def body(*refs):
        a_refs, b_refs, o_ref = refs[:nt], refs[nt:2 * nt], refs[2 * nt]
        acc = _dot(a_refs[0][...].astype(MXU_DTYPE), b_refs[0][...].astype(MXU_DTYPE), 1, 0)
        for j in range(1, nt):
            acc += _dot(a_refs[j][...].astype(MXU_DTYPE), b_refs[j][...].astype(MXU_DTYPE), 1, 0)
        o_ref[...] = acc

    in_specs = [pl.BlockSpec((tm, kt), lambda jn, i: (i, 0))] * nt
    in_specs += [_layer_spec((kt, tn), lambda jn, i, j=j: (j, jn), b_layer) for j in range(nt)]
    return pl.pallas_call(
        body, name=name, grid=(n // tn, m // tm), in_specs=in_specs,
        out_specs=pl.BlockSpec((tm, tn), lambda jn, i: (i, jn)),
        out_shape=jax.ShapeDtypeStruct((m, n), F32),
        compiler_params=_cparams("parallel", "parallel"),
    )(*a_tiles, *([b] * nt))


def _rmsnorm_fwd(x, g, *, name):
    s, d = x.shape
    tm = min(512, s)

    def body(x_ref, g_ref, h_ref):
        xv = x_ref[...]
        r = lax.rsqrt(jnp.mean(xv * xv, axis=-1, keepdims=True) + EPS)
        h_ref[...] = (xv * r * g_ref[...]).astype(MXU_DTYPE)

    return pl.pallas_call(
        body, name=name, grid=(s // tm,),
        in_specs=[pl.BlockSpec((tm, d), lambda i: (i, 0)), pl.BlockSpec((1, d), lambda i: (0, 0))],
        out_specs=pl.BlockSpec((tm, d), lambda i: (i, 0)),
        out_shape=jax.ShapeDtypeStruct((s, d), MXU_DTYPE),
        compiler_params=_cparams("parallel"),
    )(x, g)


def _rmsnorm_bwd(dh, x, g, dres, *, name):
    s, d = x.shape
    tm = min(512, s)

    def body(dh_ref, x_ref, g_ref, dres_ref, dx_ref, dg_ref):
        @pl.when(pl.program_id(0) == 0)
        def _():
            dg_ref[...] = jnp.zeros_like(dg_ref)

        xv = x_ref[...]
        r = lax.rsqrt(jnp.mean(xv * xv, axis=-1, keepdims=True) + EPS)
        xn = xv * r
        dy = dh_ref[...]
        dg_ref[...] += jnp.sum(dy * xn, axis=0, keepdims=True)
        u = dy * g_ref[...]
        dx_ref[...] = dres_ref[...] + r * (u - xn * jnp.mean(u * xn, axis=-1, keepdims=True))

    row = pl.BlockSpec((tm, d), lambda i: (i, 0))
    vec = pl.BlockSpec((1, d), lambda i: (0, 0))
    return pl.pallas_call(
        body, name=name, grid=(s // tm,), in_specs=[row, row, vec, row], out_specs=[row, vec],
        out_shape=[jax.ShapeDtypeStruct((s, d), F32), jax.ShapeDtypeStruct((1, d), F32)],
        compiler_params=_cparams("arbitrary"),
    )(dh, x, g, dres)


def _final_loss(x, g, tgt, *, name):
    s, d = x.shape
    tm = min(512, s)

    def body(x_ref, g_ref, t_ref, dx_ref, dg_ref, loss_ref):
        @pl.when(pl.program_id(0) == 0)
        def _():
            dg_ref[...] = jnp.zeros_like(dg_ref)
            loss_ref[...] = jnp.zeros_like(loss_ref)

        xv = x_ref[...]
        gv = g_ref[...]
        r = lax.rsqrt(jnp.mean(xv * xv, axis=-1, keepdims=True) + EPS)
        xn = xv * r
        err = xn * gv - t_ref[...]
        sq = jnp.sum(jnp.sum(err * err, axis=-1, keepdims=True), axis=0, keepdims=True)
        loss_ref[...] += (0.5 / d) * sq
        dy = err * (1.0 / d)
        dg_ref[...] += jnp.sum(dy * xn, axis=0, keepdims=True)
        u = dy * gv
        dx_ref[...] = r * (u - xn * jnp.mean(u * xn, axis=-1, keepdims=True))

    row = pl.BlockSpec((tm, d), lambda i: (i, 0))
    vec = pl.BlockSpec((1, d), lambda i: (0, 0))
    return pl.pallas_call(
        body, name=name, grid=(s // tm,), in_specs=[row, vec, row],
        out_specs=[row, vec, pl.BlockSpec((1, 128), lambda i: (0, 0))],
        out_shape=[jax.ShapeDtypeStruct((s, d), F32), jax.ShapeDtypeStruct((1, d), F32),
                   jax.ShapeDtypeStruct((1, 128), F32)],
        compiler_params=_cparams("arbitrary"),
    )(x, g, tgt)


def _swa_keys(kp_ref, kc_ref):
    kk = jnp.concatenate([kp_ref[...], kc_ref[...]], axis=0)
    kr = pltpu.roll(kk, 64, 1)
    lo = lax.broadcasted_iota(jnp.int32, kk.shape, 1) < 64
    return [jnp.where(lo, kk, kr).astype(MXU_DTYPE), jnp.where(lo, kr, kk).astype(MXU_DTYPE)]


GROUP = SWA_HEADS // 2


def _swa_mask(n):
    qi = lax.broadcasted_iota(jnp.int32, (BLOCK, 2 * BLOCK), 0)
    ki = lax.broadcasted_iota(jnp.int32, (BLOCK, 2 * BLOCK), 1)
    delta = BLOCK + qi - ki
    valid = (delta >= 0) & (delta < BLOCK) & ((ki >= BLOCK) | (n > 0))
    return valid, delta.astype(F32)


def _stack_heads(ref, j):
    lo = lax.broadcasted_iota(jnp.int32, (BLOCK, BLOCK), 1) < 64
    parts = []
    for r in range(GROUP):
        pair = (GROUP * j + r) // 2
        blk = ref[:, pair * 128:(pair + 1) * 128].astype(F32)
        parts.append(jnp.where(lo if r % 2 == 0 else ~lo, blk, 0.0).astype(MXU_DTYPE))
    return jnp.concatenate(parts, axis=0)


def _unstack_heads(stacked, ref, j):
    lo = lax.broadcasted_iota(jnp.int32, (BLOCK, BLOCK), 1) < 64
    for i in range(GROUP // 2):
        pair = (GROUP * j) // 2 + i
        even, odd = stacked[2 * i * BLOCK:(2 * i + 1) * BLOCK], stacked[(2 * i + 1) * BLOCK:(2 * i + 2) * BLOCK]
        ref[:, pair * 128:(pair + 1) * 128] = jnp.where(lo, even, odd).astype(ref.dtype)


def _head_rows(stacked, r):
    return stacked[r * BLOCK:(r + 1) * BLOCK]


def _swa_scores(raw, h, valid, deltaf):
    return jnp.where(valid, raw * (SWA_SCALE * LOG2E) - (SLOPES[h] * LOG2E) * deltaf, NEG)


def _swa_specs(nb):
    kcol, vcol = KA_OFF // BLOCK, VA_OFF // BLOCK
    last = nb - 1
    cur = lambda n: jnp.minimum(n, last)
    prev = lambda n: jnp.maximum(jnp.minimum(n, last) - 1, 0)
    return [
        pl.BlockSpec(memory_space=pltpu.SMEM),
        pl.BlockSpec((BLOCK, 1024), lambda n: (cur(n), QA_OFF // 1024)),
        pl.BlockSpec((BLOCK, BLOCK), lambda n: (cur(n), kcol)),
        pl.BlockSpec((BLOCK, BLOCK), lambda n: (prev(n), kcol)),
        pl.BlockSpec((BLOCK, BLOCK), lambda n: (cur(n), vcol)),
        pl.BlockSpec((BLOCK, BLOCK), lambda n: (prev(n), vcol)),
    ]


def _swa_fwd(proj, sinks, *, name):
    s = proj.shape[0]
    nb = s // BLOCK

    def body(sink_ref, q_ref, kc_ref, kp_ref, vc_ref, vp_ref, o_ref, lse_ref):
        n = pl.program_id(0)
        keys = _swa_keys(kp_ref, kc_ref)
        vals = _swa_keys(vp_ref, vc_ref)
        valid, deltaf = _swa_mask(n)
        lane = lax.broadcasted_iota(jnp.int32, (BLOCK, BLOCK), 1)
        lse_acc = jnp.zeros((BLOCK, BLOCK), F32)
        for j in range(2):
            raw = _dot(_stack_heads(q_ref, j), keys[j], 1, 1)
            probs = []
            for r in range(GROUP):
                h = GROUP * j + r
                sc = _swa_scores(_head_rows(raw, r), h, valid, deltaf)
                sink = sink_ref[h] * LOG2E
                m = jnp.maximum(jnp.max(sc, axis=-1, keepdims=True), sink)
                p = jnp.exp2(sc - m)
                l = jnp.sum(p, axis=-1, keepdims=True) + jnp.exp2(sink - m)
                probs.append((p * (1.0 / l)).astype(MXU_DTYPE))
                lse_acc = jnp.where(lane == h, m + jnp.log(l) * LOG2E, lse_acc)
            _unstack_heads(jnp.dot(jnp.concatenate(probs, axis=0), vals[j], preferred_element_type=F32), o_ref, j)
        lse_ref[...] = lse_acc

    return pl.pallas_call(
        body, name=name, grid=(nb,), in_specs=_swa_specs(nb),
        out_specs=[pl.BlockSpec((BLOCK, 1024), lambda n: (n, 0)), pl.BlockSpec((BLOCK, BLOCK), lambda n: (n, 0))],
        out_shape=[jax.ShapeDtypeStruct((s, 1024), F32), jax.ShapeDtypeStruct((s, BLOCK), F32)],
        compiler_params=_cparams("parallel"),
    )(sinks, proj, proj, proj, proj, proj)


def _swa_bwd(proj, sinks, lse, do, *, name):
    s = proj.shape[0]
    nb = s // BLOCK
    last = nb - 1

    def body(sink_ref, q_ref, kc_ref, kp_ref, vc_ref, vp_ref, lse_ref, do_ref,
             dq_ref, dk_ref, dv_ref, dsink_ref, carry_k, carry_v):
        n = pl.program_id(0)

        @pl.when(n == 0)
        def _():
            carry_k[...] = jnp.zeros_like(carry_k)
            carry_v[...] = jnp.zeros_like(carry_v)
            dsink_ref[...] = jnp.zeros_like(dsink_ref)

        @pl.when(n < nb)
        def _():
            keys = _swa_keys(kp_ref, kc_ref)
            vals = _swa_keys(vp_ref, vc_ref)
            valid, deltaf = _swa_mask(n)
            lane = lax.broadcasted_iota(jnp.int32, (BLOCK, BLOCK), 1)
            lane1 = lax.broadcasted_iota(jnp.int32, (1, BLOCK), 1)
            lse_blk = lse_ref[...]
            acc_k, acc_v = [], []
            dsink = jnp.zeros((1, BLOCK), F32)
            for j in range(2):
                q_all, do_all = _stack_heads(q_ref, j), _stack_heads(do_ref, j)
                raw = _dot(q_all, keys[j], 1, 1)
                dp_all = _dot(do_all, vals[j], 1, 1)
                probs, dscores = [], []
                for r in range(GROUP):
                    h = GROUP * j + r
                    lse_h = jnp.sum(jnp.where(lane == h, lse_blk, 0.0), axis=-1, keepdims=True)
                    p = jnp.exp2(_swa_scores(_head_rows(raw, r), h, valid, deltaf) - lse_h)
                    dp = _head_rows(dp_all, r)
                    dlt = jnp.sum(dp * p, axis=-1, keepdims=True)
                    dscores.append((p * (dp - dlt) * SWA_SCALE).astype(MXU_DTYPE))
                    probs.append(p.astype(MXU_DTYPE))
                    sunk = jnp.exp2(sink_ref[h] * LOG2E - lse_h) * dlt
                    dsink = jnp.where(lane1 == h, -jnp.sum(sunk, axis=0, keepdims=True), dsink)
                ds_all = jnp.concatenate(dscores, axis=0)
                _unstack_heads(jnp.dot(ds_all, keys[j], preferred_element_type=F32), dq_ref, j)
                acc_k.append(_dot(ds_all, q_all, 0, 0))
                acc_v.append(_dot(jnp.concatenate(probs, axis=0), do_all, 0, 0))
            lo2 = lax.broadcasted_iota(jnp.int32, (2 * BLOCK, BLOCK), 1) < 64
            fold = lambda acc: jnp.where(lo2, acc[0] + pltpu.roll(acc[0], 64, 1), acc[1] + pltpu.roll(acc[1], 64, 1))
            dkk, dvv = fold(acc_k), fold(acc_v)
            dk_ref[...] = (carry_k[...] + dkk[:BLOCK]).astype(dk_ref.dtype)
            dv_ref[...] = (carry_v[...] + dvv[:BLOCK]).astype(dv_ref.dtype)
            carry_k[...] = dkk[BLOCK:]
            carry_v[...] = dvv[BLOCK:]
            dsink_ref[...] += jnp.broadcast_to(dsink, dsink_ref.shape)

        @pl.when(n == nb)
        def _():
            dk_ref[...] = carry_k[...].astype(dk_ref.dtype)
            dv_ref[...] = carry_v[...].astype(dv_ref.dtype)

    cur = lambda n: jnp.minimum(n, last)
    lag = lambda n: jnp.maximum(n - 1, 0)
    return pl.pallas_call(
        body, name=name, grid=(nb + 1,),
        in_specs=_swa_specs(nb) + [pl.BlockSpec((BLOCK, BLOCK), lambda n: (cur(n), 0)),
                                   pl.BlockSpec((BLOCK, 1024), lambda n: (cur(n), 0))],
        out_specs=[pl.BlockSpec((BLOCK, 1024), lambda n: (cur(n), 0)),
                   pl.BlockSpec((BLOCK, BLOCK), lambda n: (lag(n), 0)),
                   pl.BlockSpec((BLOCK, BLOCK), lambda n: (lag(n), 0)),
                   pl.BlockSpec((8, BLOCK), lambda n: (0, 0))],
        out_shape=[jax.ShapeDtypeStruct((s, 1024), MXU_DTYPE), jax.ShapeDtypeStruct((s, BLOCK), MXU_DTYPE),
                   jax.ShapeDtypeStruct((s, BLOCK), MXU_DTYPE), jax.ShapeDtypeStruct((8, BLOCK), F32)],
        scratch_shapes=[pltpu.VMEM((BLOCK, BLOCK), F32), pltpu.VMEM((BLOCK, BLOCK), F32)],
        compiler_params=_cparams("arbitrary"),
    )(sinks, proj, proj, proj, proj, proj, lse, do)


def _rope_partner(v, first, width):
    lane = lax.broadcasted_iota(jnp.int32, v.shape, 1)
    in_a = (lane >= first) & (lane < first + 32)
    in_b = (lane >= first + 32) & (lane < first + 64)
    return jnp.where(in_a, pltpu.roll(v, width - 32, 1), jnp.where(in_b, pltpu.roll(v, 32, 1), 0.0))


def _mla_qkv_fwd(proj, gq, gkv, wq, wkv, tk_c, tk_s, *, layer, name):
    s = proj.shape[0]
    tm = min(512, s)

    def body(cq_ref, ckv_ref, kr_ref, gq_ref, gkv_ref, wq_ref, wkv_ref, kc_ref, ks_ref,
             qcat_ref, kcat_ref, v_ref, cqn_ref, ckvn_ref):
        cq = cq_ref[...]
        cqn = (cq * lax.rsqrt(jnp.mean(cq * cq, axis=-1, keepdims=True) + EPS) * gq_ref[...]).astype(MXU_DTYPE)
        cqn_ref[...] = cqn
        qpre = _dot(cqn, wq_ref[...], 1, 1)
        kc, ks = kc_ref[...], ks_ref[...]
        for hh in range(MLA_HEADS):
            qcat_ref[:, hh * 256:hh * 256 + 128] = qpre[:, hh * 256:hh * 256 + 128].astype(MXU_DTYPE)
            blk = qpre[:, hh * 256 + 128:(hh + 1) * 256]
            qcat_ref[:, hh * 256 + 128:(hh + 1) * 256] = (blk * kc + _rope_partner(blk, 0, 128) * ks).astype(MXU_DTYPE)
        ckv = ckv_ref[...]
        ckvn = (ckv * lax.rsqrt(jnp.mean(ckv * ckv, axis=-1, keepdims=True) + EPS) * gkv_ref[...]).astype(MXU_DTYPE)
        ckvn_ref[...] = ckvn
        kv = _dot(ckvn, wkv_ref[...], 1, 1)
        kr = kr_ref[...]
        krr = (kr * kc + _rope_partner(kr, 0, 128) * ks).astype(MXU_DTYPE)
        for hh in range(MLA_HEADS):
            kcat_ref[:, hh * 256:hh * 256 + 128] = kv[:, hh * 128:(hh + 1) * 128].astype(MXU_DTYPE)
            kcat_ref[:, hh * 256 + 128:(hh + 1) * 256] = krr
            v_ref[:, hh * 256:hh * 256 + 128] = kv[:, 1024 + hh * 128:1024 + (hh + 1) * 128].astype(MXU_DTYPE)
            v_ref[:, hh * 256 + 128:(hh + 1) * 256] = jnp.ones((tm, 128), MXU_DTYPE)

    row = lambda w, c: pl.BlockSpec((tm, w), lambda i: (i, c))
    full = lambda a: pl.BlockSpec(a.shape, lambda i: (0, 0))
    of_layer = lambda a: _layer_spec(a.shape[-2:], lambda i: (0, 0), layer)
    return pl.pallas_call(
        body, name=name, grid=(s // tm,),
        in_specs=[row(Q_RANK, CQ_OFF // Q_RANK), row(KV_RANK, CKV_OFF // KV_RANK), row(128, KR_OFF // 128),
                  full(gq), full(gkv), of_layer(wq), of_layer(wkv), row(128, 0), row(128, 0)],
        out_specs=[row(2048, 0), row(2048, 0), row(2048, 0), row(Q_RANK, 0), row(KV_RANK, 0)],
        out_shape=[jax.ShapeDtypeStruct((s, 2048), MXU_DTYPE), jax.ShapeDtypeStruct((s, 2048), MXU_DTYPE),
                   jax.ShapeDtypeStruct((s, 2048), MXU_DTYPE), jax.ShapeDtypeStruct((s, Q_RANK), MXU_DTYPE),
                   jax.ShapeDtypeStruct((s, KV_RANK), MXU_DTYPE)],
        compiler_params=_cparams("parallel"),
    )(proj, proj, proj, gq, gkv, wq, wkv, tk_c, tk_s)


def _norm_bwd(x, g, dy):
    r = lax.rsqrt(jnp.mean(x * x, axis=-1, keepdims=True) + EPS)
    xn = x * r
    u = dy * g
    return r * (u - xn * jnp.mean(u * xn, axis=-1, keepdims=True)), jnp.sum(dy * xn, axis=0, keepdims=True)


def _mla_qkv_bwd(proj, cqn, ckvn, dqcat, dkcat, dv, dka, dva, gq, gkv, wq, wkv, tk_c, tk_s, *, layer, name):
    s = proj.shape[0]
    tm = min(512, s)
    t_cq, t_ka, t_ckv, t_va, t_kr = (o - CQ_OFF for o in (CQ_OFF, KA_OFF, CKV_OFF, VA_OFF, KR_OFF))

    def body(cq_ref, ckv_ref, cqn_ref, ckvn_ref, dq_ref, dk_ref, dv_ref, dka_ref, dva_ref, gq_ref, gkv_ref, wq_ref,
             wkv_ref, kc_ref, ks_ref,
             tile_ref, dwq_ref, dwkv_ref, dgq_ref, dgkv_ref, dqpre, dkv, dwq_acc, dwkv_acc):
        dcq_ref = tile_ref.at[:, t_cq:t_cq + Q_RANK]
        dckv_ref = tile_ref.at[:, t_ckv:t_ckv + KV_RANK]
        dkr_ref = tile_ref.at[:, t_kr:t_kr + 128]
        tile_ref[:, t_ka:t_ka + 128] = dka_ref[...]
        tile_ref[:, t_va:t_va + 128] = dva_ref[...]

        @pl.when(pl.program_id(0) == 0)
        def _():
            for r in (dwq_acc, dwkv_acc, dgq_ref, dgkv_ref):
                r[...] = jnp.zeros_like(r)

        kc, ks = kc_ref[...], ks_ref[...]
        dkrr = jnp.zeros((tm, 128), F32)
        for hh in range(MLA_HEADS):
            dqpre[:, hh * 256:hh * 256 + 128] = dq_ref[:, hh * 256:hh * 256 + 128].astype(MXU_DTYPE)
            blk = dq_ref[:, hh * 256 + 128:(hh + 1) * 256]
            dqpre[:, hh * 256 + 128:(hh + 1) * 256] = (blk * kc + _rope_partner(blk * ks, 0, 128)).astype(MXU_DTYPE)
            dkv[:, hh * 128:(hh + 1) * 128] = dk_ref[:, hh * 256:hh * 256 + 128].astype(MXU_DTYPE)
            dkrr = dkrr + dk_ref[:, hh * 256 + 128:(hh + 1) * 256]
        dkv[:, 1024:] = dv_ref[...].astype(MXU_DTYPE)
        dkr_ref[...] = (dkrr * kc + _rope_partner(dkrr * ks, 0, 128)).astype(dkr_ref.dtype)

        dq_b = dqpre[...]
        dwq_acc[...] += _dot(cqn_ref[...], dq_b, 0, 0)
        dcq, dgq = _norm_bwd(cq_ref[...], gq_ref[...], _dot(dq_b, wq_ref[...], 1, 0))
        dcq_ref[...] = dcq.astype(dcq_ref.dtype)
        dgq_ref[...] += dgq

        dkv_b = dkv[...]
        dwkv_acc[...] += _dot(ckvn_ref[...], dkv_b, 0, 0)
        dckv, dgkv = _norm_bwd(ckv_ref[...], gkv_ref[...], _dot(dkv_b, wkv_ref[...], 1, 0))
        dckv_ref[...] = dckv.astype(dckv_ref.dtype)
        dgkv_ref[...] += dgkv

        @pl.when(pl.program_id(0) == s // tm - 1)
        def _():
            dwq_ref[...] = dwq_acc[...].T
            dwkv_ref[...] = dwkv_acc[...].T

    row = lambda w, c: pl.BlockSpec((tm, w), lambda i: (i, c))
    full = lambda shape: pl.BlockSpec(shape, lambda i: (0, 0))
    of_layer = lambda a: _layer_spec(a.shape[-2:], lambda i: (0, 0), layer)
    return pl.pallas_call(
        body, name=name, grid=(s // tm,),
        in_specs=[row(Q_RANK, CQ_OFF // Q_RANK), row(KV_RANK, CKV_OFF // KV_RANK), row(Q_RANK, 0), row(KV_RANK, 0),
                  row(2048, 0), row(2048, 0), row(1024, 0), row(128, 0), row(128, 0), full(gq.shape), full(gkv.shape),
                  of_layer(wq), of_layer(wkv), row(128, 0), row(128, 0)],
        out_specs=[row(1024, 0), full(wq.shape[-2:]), full(wkv.shape[-2:]), full(gq.shape), full(gkv.shape)],
        out_shape=[jax.ShapeDtypeStruct((s, 1024), MXU_DTYPE), jax.ShapeDtypeStruct(wq.shape[-2:], F32),
                   jax.ShapeDtypeStruct(wkv.shape[-2:], F32), jax.ShapeDtypeStruct(gq.shape, F32),
                   jax.ShapeDtypeStruct(gkv.shape, F32)],
        scratch_shapes=[pltpu.VMEM((tm, 2048), MXU_DTYPE), pltpu.VMEM((tm, 2048), MXU_DTYPE),
                        pltpu.VMEM((Q_RANK, 2048), F32), pltpu.VMEM((KV_RANK, 2048), F32)],
        compiler_params=_cparams("arbitrary"),
    )(proj, proj, cqn, ckvn, dqcat, dkcat, dv, dka, dva, gq, gkv, wq, wkv, tk_c, tk_s)


def _loop_by_two(lo, hi, step):
    n = hi - lo

    def four(i, carry):
        for u in range(4):
            step(lo + 4 * i + u)
        return carry

    lax.fori_loop(0, n // 4, four, 0)
    rest = lo + (n // 4) * 4

    @pl.when(n % 4 >= 2)
    def _():
        step(rest)
        step(rest + 1)

    @pl.when(n % 2 == 1)
    def _():
        step(hi - 1)


def _causal_mask(t):
    return lax.broadcasted_iota(jnp.int32, (t, t), 1) <= lax.broadcasted_iota(jnp.int32, (t, t), 0)


def _mla_fwd(qcat, kcat, v, *, name, gather=None):
    s = qcat.shape[0]
    t = min(512, s)
    nq = s // t
    hp = HEADS_PER_STEP
    ng = MLA_HEADS // hp
    c2 = MLA_SCALE * LOG2E

    nw = 0 if gather is None else len(gather)

    def body(q_ref, k_ref, v_ref, *rest):
        src, (o_ref, lse_ref), out = rest[:nw], rest[nw:nw + 2], rest[nw + 2:2 * nw + 2]
        top_s, acc_s, *sems = rest[2 * nw + 2:]
        g, qi = pl.program_id(0), pl.program_id(1)
        if nw:
            @pl.when((g == 0) & (qi == 0))
            def _():
                _gather_start(src, out, *sems)

            @pl.when((g == ng - 1) & (qi == 0))
            def _():
                _gather_forward(src, out, *sems)

        rows = lambda j: pl.ds(pl.multiple_of(j * t, t), t)
        head = lambda e: slice(e * 256, (e + 1) * 256)
        raw = lambda e, j: _dot(q_ref[:, head(e)], k_ref[rows(j), head(e)], 1, 1)

        for e in range(hp):
            top_s[e] = jnp.where(_causal_mask(t), raw(e, qi), NEG)

        def pass1(j):
            for e in range(hp):
                top_s[e] = jnp.maximum(top_s[e], raw(e, j))

        _loop_by_two(0, qi, pass1)
        m = [jnp.max(top_s[e], axis=-1, keepdims=True) * c2 for e in range(hp)]

        def weighted(e, j, masked):
            sc = raw(e, j) * c2 - m[e]
            if masked:
                sc = jnp.where(_causal_mask(t), sc, NEG)
            return jnp.dot(jnp.exp2(sc).astype(MXU_DTYPE), v_ref[rows(j), head(e)], preferred_element_type=F32)

        for e in range(hp):
            acc_s[e] = weighted(e, qi, True)

        def pass2(j):
            for e in range(hp):
                acc_s[e] += weighted(e, j, False)

        _loop_by_two(0, qi, pass2)
        lane = lax.broadcasted_iota(jnp.int32, (t, 128), 1)
        stats = jnp.zeros((t, 128), F32)
        for e in range(hp):
            l = acc_s[e, :, 128:]
            o_ref[:, e * 128:(e + 1) * 128] = acc_s[e, :, :128] / l
            stats = jnp.where(lane == e, m[e] + jnp.log(l) * LOG2E, stats)
        lse_ref[...] = stats
        if nw:
            @pl.when((g == ng - 1) & (qi == nq - 1))
            def _():
                _gather_finish(src, out, *sems)

    outs = pl.pallas_call(
        body, name=name, grid=(ng, nq),
        in_specs=[pl.BlockSpec((t, 256 * hp), lambda g, qi: (qi, g)), pl.BlockSpec((s, 256 * hp), lambda g, qi: (0, g)),
                  pl.BlockSpec((s, 256 * hp), lambda g, qi: (0, g))] + [ANY] * nw,
        out_specs=[pl.BlockSpec((t, 128 * hp), lambda g, qi: (qi, g)), pl.BlockSpec((t, 128), lambda g, qi: (qi, g))]
        + [ANY] * nw,
        out_shape=[jax.ShapeDtypeStruct((s, 1024), F32), jax.ShapeDtypeStruct((s, 128 * ng), F32)]
        + (_gathered_shapes(gather) if nw else []),
        scratch_shapes=[pltpu.VMEM((hp, t, t), F32), pltpu.VMEM((hp, t, 256), F32)] + (_sems(6, nw) if nw else []),
        compiler_params=_cparams("arbitrary", "arbitrary"),
    )(qcat, kcat, v, *(gather or []))
    return outs[0], outs[1], list(outs[2:])


def _mla_bwd(qcat, kcat, v, do, lse, delta, *, name, scatter=None):
    s = qcat.shape[0]
    t = min(512, s)
    nq = s // t
    hp = HEADS_PER_STEP
    c2 = MLA_SCALE * LOG2E
    nw = 0 if scatter is None else len(scatter)
    kpb = 4 if nq % 4 == 0 else 2 if nq % 2 == 0 else 1

    def body(q_ref, k_ref, v_ref, do_ref, lse_ref, dl_ref, *rest):
        src, (dq_ref, dk_ref, dv_ref), out = rest[:nw], rest[nw:nw + 3], rest[nw + 3:2 * nw + 3]
        dk_acc, dv_acc, *sems = rest[2 * nw + 3:]
        h, kb = pl.program_id(0), pl.program_id(1)
        if nw:
            @pl.when((h == 0) & (kb == 0))
            def _():
                _scatter_start(src, out, *sems)

        @pl.when(kb == 0)
        def _():
            dq_ref[...] = jnp.zeros_like(dq_ref)

        mine = lax.broadcasted_iota(jnp.int32, (t, 128), 1) == h % hp
        for sub in range(kpb):
            ki = kb * kpb + sub
            own = slice(sub * t, (sub + 1) * t)
            dk_acc[...] = jnp.zeros_like(dk_acc)
            dv_acc[...] = jnp.zeros_like(dv_acc)
            k, vv = k_ref[own, :], v_ref[own, :]

            def chunk(qi, masked):
                rows = pl.ds(pl.multiple_of(qi * t, t), t)
                q, dob = q_ref[rows, :], do_ref[rows, :]
                pick = lambda r: jnp.sum(jnp.where(mine, r[rows, :], 0.0), axis=-1, keepdims=True)
                sc = _dot(q, k, 1, 1) * c2
                if masked:
                    sc = jnp.where(_causal_mask(t), sc, NEG)
                p = jnp.exp2(sc - pick(lse_ref))
                dp = _dot(dob, vv, 1, 1)
                ds = (p * (dp - pick(dl_ref)) * MLA_SCALE).astype(MXU_DTYPE)
                dv_acc[...] += _dot(dob, p.astype(MXU_DTYPE), 0, 0)
                dk_acc[...] += _dot(q, ds, 0, 0)
                dq_ref[rows, :] += jnp.dot(ds, k, preferred_element_type=F32)

            chunk(ki, True)
            _loop_by_two(ki + 1, nq, lambda qi: chunk(qi, False))
            dk_ref[own, :] = dk_acc[...].T
            dv_ref[own, :] = dv_acc[...].T
        if nw:
            @pl.when((h == MLA_HEADS - 1) & (kb == nq // kpb - 1))
            def _():
                _scatter_finish(src, out, *sems)

    head = lambda w: pl.BlockSpec((s, w), lambda h, kb: (0, h))
    blk = lambda w: pl.BlockSpec((kpb * t, w), lambda h, kb: (kb, h))
    stat = pl.BlockSpec((s, 128), lambda h, kb: (0, h // hp))
    outs = pl.pallas_call(
        body, name=name, grid=(MLA_HEADS, nq // kpb),
        in_specs=[head(256), blk(256), pl.BlockSpec((kpb * t, 128), lambda h, kb: (kb, 2 * h)), head(128), stat, stat]
        + [ANY] * nw,
        out_specs=[head(256), blk(256), blk(128)] + [ANY] * nw,
        out_shape=[jax.ShapeDtypeStruct((s, 2048), F32), jax.ShapeDtypeStruct((s, 2048), F32),
                   jax.ShapeDtypeStruct((s, 1024), F32)] + [jax.ShapeDtypeStruct(a.shape, a.dtype) for a in scatter or []],
        scratch_shapes=[pltpu.VMEM((256, t), F32), pltpu.VMEM((128, t), F32)] + (_sems(3, nw) if nw else []),
        compiler_params=_cparams("arbitrary", "arbitrary"),
    )(qcat, kcat, v, do, lse, delta, *(scatter or []))
    return outs[0], outs[1], outs[2], list(outs[3:])


def _gate_specs(tm):
    half = lambda c: pl.BlockSpec((tm, 1024), lambda i: (i, c))
    return half(0), half(GA_OFF // 1024), half(GB_OFF // 1024)


def _gate_fwd(oa, ob, proj, *, name):
    s = oa.shape[0]
    tm = min(512, s)

    def body(oa_ref, ob_ref, ga_ref, gb_ref, y_ref):
        ga, gb = ga_ref[...], gb_ref[...]
        y_ref[:, :1024] = (oa_ref[...] * (ga * jax.nn.sigmoid(ga))).astype(MXU_DTYPE)
        y_ref[:, 1024:] = (ob_ref[...] * (gb * jax.nn.sigmoid(gb))).astype(MXU_DTYPE)

    o_spec, ga_spec, gb_spec = _gate_specs(tm)
    return pl.pallas_call(
        body, name=name, grid=(s // tm,), in_specs=[o_spec, o_spec, ga_spec, gb_spec],
        out_specs=pl.BlockSpec((tm, 2048), lambda i: (i, 0)),
        out_shape=jax.ShapeDtypeStruct((s, 2048), MXU_DTYPE),
        compiler_params=_cparams("parallel"),
    )(oa, ob, proj, proj)


def _gate_bwd(dy, oa, ob, proj, *, name):
    s = oa.shape[0]
    tm = min(512, s)

    def body(dy_ref, oa_ref, ob_ref, ga_ref, gb_ref, doa_ref, dob_ref, dga_ref, dgb_ref, dl_ref):
        def branch(dyv, o, g, do_ref, dg_ref):
            sg = jax.nn.sigmoid(g)
            do = dyv * (g * sg)
            do_ref[...] = do.astype(MXU_DTYPE)
            dg_ref[...] = (dyv * o * (sg * (1.0 + g * (1.0 - sg)))).astype(MXU_DTYPE)
            return do

        branch(dy_ref[:, :1024], oa_ref[...], ga_ref[...], doa_ref, dga_ref)
        ob = ob_ref[...]
        prod = branch(dy_ref[:, 1024:], ob, gb_ref[...], dob_ref, dgb_ref) * ob
        lane = lax.broadcasted_iota(jnp.int32, (tm, stat_w), 1)
        acc = jnp.zeros((tm, stat_w), F32)
        for hh in range(MLA_HEADS):
            at = (hh // HEADS_PER_STEP) * 128 + hh % HEADS_PER_STEP
            acc = jnp.where(lane == at, jnp.sum(prod[:, hh * 128:(hh + 1) * 128], axis=-1, keepdims=True), acc)
        dl_ref[...] = acc

    stat_w = 128 * (MLA_HEADS // HEADS_PER_STEP)
    o_spec, ga_spec, gb_spec = _gate_specs(tm)
    return pl.pallas_call(
        body, name=name, grid=(s // tm,),
        in_specs=[pl.BlockSpec((tm, 2048), lambda i: (i, 0)), o_spec, o_spec, ga_spec, gb_spec],
        out_specs=[o_spec, o_spec, o_spec, o_spec, pl.BlockSpec((tm, stat_w), lambda i: (i, 0))],
        out_shape=[jax.ShapeDtypeStruct((s, 1024), MXU_DTYPE)] * 4 + [jax.ShapeDtypeStruct((s, stat_w), F32)],
        compiler_params=_cparams("parallel"),
    )(dy, oa, ob, proj, proj)


def _row_block(rows, cols, itemsize=4, budget=2 << 20):
    fits = [tr for tr in range(16, rows + 1, 16) if rows % tr == 0 and tr * cols * itemsize <= budget]
    return fits[-1] if fits else rows


def _adamw(w, g, m, v, *, name):
    shape = w.shape
    rows, cols = shape[-2:]
    w3, g3, m3, v3 = (a.reshape((-1, rows, cols)) for a in (w, g, m, v))
    lead = w3.shape[0]
    tr = _row_block(rows, cols)

    def body(w_ref, g_ref, m_ref, v_ref, d_ref, mo_ref, vo_ref):
        gv = g_ref[...]
        mn = ADAM_B1 * m_ref[...] + (1.0 - ADAM_B1) * gv
        vn = ADAM_B2 * v_ref[...] + (1.0 - ADAM_B2) * jnp.square(gv)
        m_hat = mn / (1.0 - ADAM_B1 ** ADAM_STEP)
        v_hat = vn / (1.0 - ADAM_B2 ** ADAM_STEP)
        d_ref[...] = -ADAM_LR * (m_hat / (jnp.sqrt(v_hat) + ADAM_EPS) + ADAM_WD * w_ref[...])
        mo_ref[...] = mn
        vo_ref[...] = vn

    spec = pl.BlockSpec((None, tr, cols), lambda a, i: (a, i, 0))
    outs = pl.pallas_call(
        body, name=name, grid=(lead, rows // tr), in_specs=[spec] * 4, out_specs=[spec] * 3,
        out_shape=[jax.ShapeDtypeStruct((lead, rows, cols), F32)] * 3,
        compiler_params=_cparams("parallel", "parallel"),
    )(w3, g3, m3, v3)
    return tuple(o.reshape(shape) for o in outs)


def _pair_sum(where, grads, recv, *, name):
    layers, chips, _, rows, cols = grads.shape
    tr = _row_block(rows, cols)

    def body(where_ref, a_ref, b_ref, o_ref):
        o_ref[...] = (a_ref[...] + b_ref[...].astype(F32)).astype(WIRE_DTYPE)

    spec = pl.BlockSpec((None, None, tr, cols), lambda a, k, i, w: (a, k, i, 0))
    return pl.pallas_call(
        body, name=name,
        grid_spec=pltpu.PrefetchScalarGridSpec(
            num_scalar_prefetch=1, grid=(layers, chips, rows // tr),
            in_specs=[pl.BlockSpec((None, None, None, tr, cols), lambda a, k, i, w: (a, k, w[4], i, 0)), spec],
            out_specs=spec),
        out_shape=jax.ShapeDtypeStruct((layers, chips, rows, cols), WIRE_DTYPE),
        compiler_params=_cparams("parallel", "parallel", "parallel"),
    )(where, grads, recv)


def _chip_sum(where, grads, recv, parts, *, name):
    layers, _, _, rows, cols = grads.shape
    tr = _row_block(rows, cols)

    def body(where_ref, a_ref, b_ref, t0_ref, t1_ref, t2_ref, o_ref):
        total = a_ref[...] + b_ref[...].astype(F32)
        for t_ref in (t0_ref, t1_ref, t2_ref):
            total = total + t_ref[...].astype(F32)
        o_ref[...] = total

    slot = lambda j: pl.BlockSpec((None, None, tr, cols), lambda a, i, w: (a, w[j], i, 0))
    return pl.pallas_call(
        body, name=name,
        grid_spec=pltpu.PrefetchScalarGridSpec(
            num_scalar_prefetch=1, grid=(layers, rows // tr),
            in_specs=[pl.BlockSpec((None, None, None, tr, cols), lambda a, i, w: (a, w[0], w[4], i, 0)), slot(0), slot(1),
                      slot(2), slot(3)],
            out_specs=slot(4)),
        out_shape=jax.ShapeDtypeStruct((layers, 2, rows, cols), F32),
        compiler_params=_cparams("parallel", "parallel"),
    )(where, grads, recv, parts, parts, parts)


def _place():
    x, y, c = lax.axis_index("x"), lax.axis_index("y"), lax.axis_index("c")
    chips = [(1 - x, y), (x, 1 - y), (1 - x, 1 - y)]
    return x, y, c, chips


def _sems(*shape):
    return [pltpu.SemaphoreType.DMA(shape), pltpu.SemaphoreType.DMA(shape)]


OWNER_CORE = (0, 1, 1, 1)


def _gather_copy(src, out, send_sems, recv_sems, t, sem, slot, to, forward=False):
    n = src[t].shape[0]
    rows = out[t].at[pl.ds(pl.multiple_of(slot * n, 16), n)]
    return pltpu.make_async_remote_copy(src_ref=rows if forward else src[t], dst_ref=rows, send_sem=send_sems.at[sem, t],
                                        recv_sem=recv_sems.at[sem, t], device_id=to, device_id_type=MESH)


def _gather_start(src, out, send_sems, recv_sems):
    x, y, c, chips = _place()
    for t, owner in enumerate(OWNER_CORE):
        @pl.when(c == owner)
        def _():
            for j, chip in enumerate(chips):
                _gather_copy(src, out, send_sems, recv_sems, t, j, 2 * x + y, (*chip, c)).start()


def _gather_forward(src, out, send_sems, recv_sems):
    x, y, c, chips = _place()
    for t, owner in enumerate(OWNER_CORE):
        @pl.when(c == owner)
        def _():
            for j, (px, py) in enumerate(chips):
                _gather_copy(src, out, send_sems, recv_sems, t, j, 2 * px + py, (px, py, c)).wait_recv()
                _gather_copy(src, out, send_sems, recv_sems, t, 3 + j, 2 * px + py, (x, y, 1 - c), forward=True).start()


def _gather_finish(src, out, send_sems, recv_sems):
    x, y, c, chips = _place()
    slots = [2 * px + py for px, py in chips]
    for t, owner in enumerate(OWNER_CORE):
        copy = functools.partial(_gather_copy, src, out, send_sems, recv_sems, t)

        @pl.when(c == owner)
        def _():
            for j, (px, py) in enumerate(chips):
                copy(j, 2 * x + y, (px, py, c)).wait_send()
                copy(3 + j, slots[j], (x, y, 1 - c), forward=True).wait_send()

        @pl.when(c != owner)
        def _():
            for j in range(3):
                copy(3 + j, slots[j], (x, y, 1 - c), forward=True).wait_recv()


def _gathered_shapes(shards):
    return [jax.ShapeDtypeStruct((4 * a.shape[0], a.shape[1]), a.dtype) for a in shards]


def _comm_gather_layer(shards, *, name):
    nt = len(shards)

    def body(*refs):
        src, out, sems = refs[:nt], refs[nt:2 * nt], refs[2 * nt:]
        _gather_start(src, out, *sems)
        _gather_forward(src, out, *sems)
        _gather_finish(src, out, *sems)

    return pl.pallas_call(
        body, name=name, in_specs=[ANY] * nt, out_specs=[ANY] * nt, out_shape=_gathered_shapes(shards),
        scratch_shapes=_sems(6, nt),
    )(*shards)


def _comm_swap_sibling(bufs, *, name):
    nt = len(bufs)

    def body(*refs):
        src, out, (send_sems, recv_sems) = refs[:nt], refs[nt:2 * nt], refs[2 * nt:]
        x, y, c, _ = _place()
        cps = [pltpu.make_async_remote_copy(src_ref=src[t], dst_ref=out[t], send_sem=send_sems.at[t], recv_sem=recv_sems.at[t],
                                            device_id=(x, y, 1 - c), device_id_type=MESH) for t in range(nt)]
        for cp in cps:
            cp.start()
        for cp in cps:
            cp.wait()

    return pl.pallas_call(
        body, name=name, in_specs=[ANY] * nt, out_specs=[ANY] * nt,
        out_shape=[jax.ShapeDtypeStruct(a.shape, a.dtype) for a in bufs], scratch_shapes=_sems(nt),
    )(*bufs)


def _scatter_copy(src, out, send_sems, recv_sems, j, t, from_slot, to_slot, to):
    return pltpu.make_async_remote_copy(src_ref=src[t].at[:, from_slot], dst_ref=out[t].at[:, to_slot],
                                        send_sem=send_sems.at[j, t], recv_sem=recv_sems.at[j, t], device_id=to,
                                        device_id_type=MESH)


def _scatter_start(src, out, send_sems, recv_sems):
    x, y, c, chips = _place()
    for j, (px, py) in enumerate(chips):
        for t in range(len(src)):
            _scatter_copy(src, out, send_sems, recv_sems, j, t, 2 * px + py, 2 * x + y, (px, py, c)).start()


def _scatter_finish(src, out, send_sems, recv_sems):
    x, y, c, chips = _place()
    for j, (px, py) in enumerate(chips):
        for t in range(len(src)):
            _scatter_copy(src, out, send_sems, recv_sems, j, t, 2 * x + y, 2 * px + py, (px, py, c)).wait_recv()
    for j, (px, py) in enumerate(chips):
        for t in range(len(src)):
            _scatter_copy(src, out, send_sems, recv_sems, j, t, 2 * px + py, 2 * x + y, (px, py, c)).wait_send()


def _comm_scatter_chips(parts, *, name):
    nt = len(parts)

    def body(*refs):
        src, out, sems = refs[:nt], refs[nt:2 * nt], refs[2 * nt:]
        _scatter_start(src, out, *sems)
        _scatter_finish(src, out, *sems)

    return pl.pallas_call(
        body, name=name, in_specs=[ANY] * nt, out_specs=[ANY] * nt,
        out_shape=[jax.ShapeDtypeStruct(a.shape, a.dtype) for a in parts], scratch_shapes=_sems(3, nt),
    )(*parts)


def _comm_join_halves(bufs, *, name):
    nt = len(bufs)

    def body(*refs):
        src, out, (send_sems, recv_sems) = refs[:nt], refs[nt:2 * nt], refs[2 * nt:]
        x, y, c, _ = _place()
        copy = lambda t, half: pltpu.make_async_remote_copy(
            src_ref=src[t].at[:, c], dst_ref=out[t].at[:, half], send_sem=send_sems.at[t], recv_sem=recv_sems.at[t],
            device_id=(x, y, 1 - c), device_id_type=MESH)
        sends = [copy(t, c) for t in range(nt)]
        for cp in sends:
            cp.start()
        for t in range(nt):
            copy(t, 1 - c).wait_recv()
        for cp in sends:
            cp.wait_send()

    return pl.pallas_call(
        body, name=name, in_specs=[ANY] * nt, out_specs=[ANY] * nt,
        out_shape=[jax.ShapeDtypeStruct(a.shape, a.dtype) for a in bufs],
        input_output_aliases={t: t for t in range(nt)}, scratch_shapes=_sems(nt),
    )(*bufs)


def _comm_allreduce_small(part, *, name):
    rows, cols = part.shape

    def body(p_ref, o_ref, buf, send_sems, recv_sems):
        x, y, c, _ = _place()
        me = 4 * x + 2 * y + c
        buf[me] = p_ref[...]
        flip = lambda v, bit: 1 - v if bit else v
        peers = [(flip(x, d & 4), flip(y, d & 2), flip(c, d & 1)) for d in range(1, 8)]
        sends = []
        for j, peer in enumerate(peers):
            cp = pltpu.make_async_remote_copy(src_ref=buf.at[me], dst_ref=buf.at[me], send_sem=send_sems.at[j],
                                              recv_sem=recv_sems.at[j], device_id=peer, device_id_type=MESH)
            cp.start()
            sends.append(cp)
        for j, (px, py, pc) in enumerate(peers):
            pltpu.make_async_remote_copy(src_ref=buf.at[me], dst_ref=buf.at[4 * px + 2 * py + pc], send_sem=send_sems.at[j],
                                         recv_sem=recv_sems.at[j], device_id=(px, py, pc), device_id_type=MESH).wait_recv()
        for cp in sends:
            cp.wait_send()
        total = buf[0]
        for i in range(1, 8):
            total = total + buf[i]
        o_ref[...] = total

    vm = pl.BlockSpec(memory_space=pltpu.VMEM)
    return pl.pallas_call(
        body, name=name, in_specs=[vm], out_specs=vm, out_shape=jax.ShapeDtypeStruct((rows, cols), F32),
        scratch_shapes=[pltpu.VMEM((8, rows, cols), F32), pltpu.SemaphoreType.DMA((7,)), pltpu.SemaphoreType.DMA((7,))],
    )(part)


def _pad_in_rows(wt):
    r = lambda o, n: wt[..., o:o + n, :]
    kr = r(2944, 64)
    return jnp.concatenate([r(0, 1024), r(1280, 1024), r(3008, 1024), r(2304, Q_RANK), r(1024, 128), r(2688, KV_RANK),
                            r(1152, 128), kr, jnp.zeros_like(kr)], axis=-2)


def _unpad_in_rows(qa, ga, gb, mixed):
    cq, ka, ckv, va, kr = (mixed[..., o - CQ_OFF:o - CQ_OFF + n, :] for o, n in (
        (CQ_OFF, Q_RANK), (KA_OFF, 128), (CKV_OFF, KV_RANK), (VA_OFF, 128), (KR_OFF, 64)))
    return jnp.concatenate([qa, ka, va, ga, cq, ckv, kr, gb], axis=-2)


def _pad_q_rows(wt):
    lead, cols = wt.shape[:-2], wt.shape[-1]
    wt = jnp.pad(wt.reshape(lead + (MLA_HEADS, 192, cols)), [(0, 0)] * (len(lead) + 1) + [(0, 64), (0, 0)])
    return wt.reshape(lead + (MLA_HEADS * 256, cols))


def _unpad_q_rows(wt):
    lead, cols = wt.shape[:-2], wt.shape[-1]
    return wt.reshape(lead + (MLA_HEADS, 256, cols))[..., :192, :].reshape(lead + (MLA_HEADS * 192, cols))


def _perm_kv_rows(wt):
    lead, cols = wt.shape[:-2], wt.shape[-1]
    return jnp.swapaxes(wt.reshape(lead + (MLA_HEADS, 2, 128, cols)), -4, -3).reshape(lead + (2048, cols))


def _unperm_kv_rows(wt):
    lead, cols = wt.shape[:-2], wt.shape[-1]
    return jnp.swapaxes(wt.reshape(lead + (2, MLA_HEADS, 128, cols)), -4, -3).reshape(lead + (2048, cols))


def _t(a):
    return jnp.swapaxes(a, -1, -2)


def _rope_tables(s):
    pos = jnp.arange(s, dtype=F32)
    inv_freq = 10000.0 ** (-jnp.arange(0, 64, 2, dtype=F32) / 64)
    ang = pos[:, None] * inv_freq[None, :]
    cos, sin = jnp.cos(ang), jnp.sin(ang)
    z64 = jnp.zeros((s, 64), F32)
    tk_c = jnp.concatenate([cos, cos, z64], axis=-1)
    tk_s = jnp.concatenate([-sin, sin, z64], axis=-1)
    return tk_c, tk_s


def _layer_weights(gathered, own, chip):
    def with_own_rows(g, o):
        n = o.shape[0]
        return jnp.concatenate([lax.select(chip == j, o, g[j * n:(j + 1) * n]) for j in range(4)], axis=0)

    full_in, full_q, full_kv, full_o = (with_own_rows(g, o) for g, o in zip(gathered, own))
    return _pad_in_rows(full_in), _pad_q_rows(full_q), _perm_kv_rows(full_kv), full_o


def _device_step(xs, tgt, attn_g, sinks, gq, gkv, final_g, shards, where):
    chip = where[0]
    depth = shards[0].shape[0]
    s = xs.shape[0]
    tabs = _rope_tables(s)
    saved = []
    x = xs
    of_layer = lambda l: [a[l] for a in shards]
    gathered = _comm_gather_layer(of_layer(0), name="comm_gather_layer0")
    weights = []
    for l in range(depth):
        w_in_p, w_q_p, w_kv_p, w_o = _layer_weights(gathered, of_layer(l), chip)
        weights.append((w_in_p, w_q_p, w_kv_p, w_o))
        h = _rmsnorm_fwd(x, attn_g[l:l + 1], name=f"norm_fwd{l}")
        proj = _matmul(h, w_in_p, tb=True, name=f"in_proj{l}")
        oa, lse_a = _swa_fwd(proj, sinks[l], name=f"swa_fwd{l}")
        qcat, kcat, v, cqn, ckvn = _mla_qkv_fwd(proj, gq[l:l + 1], gkv[l:l + 1], w_q_p, w_kv_p, *tabs, layer=None,
                                                name=f"mla_qkv_fwd{l}")
        ob, lse_b, gathered = _mla_fwd(qcat, kcat, v, gather=of_layer(l + 1) if l + 1 < depth else None,
                                       name=f"mla_fwd{l}")
        y = _gate_fwd(oa, ob, proj, name=f"gate_fwd{l}")
        x_next = _matmul(y, w_o, add=x, name=f"out_proj{l}")
        saved.append((x, h, proj, oa, lse_a, qcat, kcat, v, cqn, ckvn, ob, lse_b, y))
        x = x_next

    dx, d_final_g, loss = _final_loss(x, final_g, tgt, name="final_loss")

    d_attn_g, d_sinks, d_gq, d_gkv = [None] * depth, [None] * depth, [None] * depth, [None] * depth
    d_w_in, d_w_q, d_w_kv, d_w_o = [None] * depth, [None] * depth, [None] * depth, [None] * depth
    half = depth // 2
    carried = {half - 1: list(range(half, depth)), **{l - 1: [l] for l in range(1, half)}}
    done = {}

    def grads_by_chip(layers):
        stack = lambda per_layer: jnp.stack([per_layer[l] for l in layers])
        in_tiles = [stack([tiles[j] if tiles else None for tiles in d_w_in]) for j in range(4)]
        full = [_unpad_in_rows(*in_tiles), _unpad_q_rows(stack(d_w_q)), _unperm_kv_rows(stack(d_w_kv)), stack(d_w_o)]
        return [g.reshape(g.shape[0], 4, 2, g.shape[1] // 8, g.shape[2]) for g in full]

    def reduce_begin(grads5, tag):
        give = [lax.dynamic_index_in_dim(g, 1 - where[4], axis=2, keepdims=False).astype(WIRE_DTYPE) for g in grads5]
        recv = _comm_swap_sibling(give, name=f"comm_swap_sibling_{tag}")
        return recv, [_pair_sum(where, g, r, name=f"pair_sum_{tag}_{t}") for t, (g, r) in enumerate(zip(grads5, recv))]

    def reduce_end(grads5, recv, parts, tag):
        mine = [_chip_sum(where, g, r, p, name=f"chip_sum_{tag}_{t}") for t, (g, r, p) in enumerate(zip(grads5, recv, parts))]
        joined = _comm_join_halves(mine, name=f"comm_join_halves_{tag}")
        return [a.reshape(a.shape[0], -1, a.shape[-1]) for a in joined]

    for l in reversed(range(depth)):
        x, h, proj, oa, lse_a, qcat, kcat, v, cqn, ckvn, ob, lse_b, y = saved[l]
        w_in_p, w_q_p, w_kv_p, w_o = weights[l]
        dy = _matmul(dx, w_o, tb=True, name=f"out_proj_dx{l}")
        d_w_o[l] = _matmul(y, dx, ta=True, name=f"out_proj_dw{l}")
        doa, dob, dga, dgb, delta_b = _gate_bwd(dy, oa, ob, proj, name=f"gate_bwd{l}")
        dqa, dka, dva, dsink = _swa_bwd(proj, sinks[l], lse_a, doa, name=f"swa_bwd{l}")
        phase, pair = carried.get(l), None
        if phase:
            grads5 = grads_by_chip(phase)
            recv, pair = reduce_begin(grads5, f"l{phase[0]}")
        dqc, dkc, dv, parts = _mla_bwd(qcat, kcat, v, dob, lse_b, delta_b, scatter=pair, name=f"mla_bwd{l}")
        if phase:
            done[phase[0]] = reduce_end(grads5, recv, parts, f"l{phase[0]}")
        mixed, d_w_q[l], d_w_kv[l], dgq_l, dgkv_l = _mla_qkv_bwd(
            proj, cqn, ckvn, dqc, dkc, dv, dka, dva, gq[l:l + 1], gkv[l:l + 1], w_q_p, w_kv_p, *tabs, layer=None,
            name=f"mla_qkv_bwd{l}")
        dproj = [dqa, dga, dgb, mixed]
        dh = _matmul_ktiles(dproj, w_in_p, name=f"in_proj_dx{l}")
        d_w_in[l] = [_matmul(tile, h, ta=True, name=f"in_proj_dw{l}_{j}") for j, tile in enumerate(dproj)]
        dx, dg_l = _rmsnorm_bwd(dh, x, attn_g[l:l + 1], dx, name=f"norm_bwd{l}")
        d_attn_g[l], d_sinks[l], d_gq[l], d_gkv[l] = dg_l, dsink[0:1, :SWA_HEADS], dgq_l, dgkv_l

    grads5 = grads_by_chip([0])
    recv, pair = reduce_begin(grads5, "l0")
    done[0] = reduce_end(grads5, recv, _comm_scatter_chips(pair, name="comm_scatter_chips_l0"), "l0")
    cat = lambda parts: jnp.concatenate(parts, axis=0)
    reduced = [cat([done[first][t] for first in sorted(done)]) for t in range(len(shards))]
    return loss, dx, cat(d_attn_g), cat(d_sinks), cat(d_gq), cat(d_gkv), d_final_g, reduced


def kernel(x, attn_norm_g, w_in, swa_sinks, q_a_norm_g, kv_a_norm_g, w_q_b, w_kv_b, w_out, final_norm_g, loss_target, m_attn_norm_g, m_w_in, m_swa_sinks, m_q_a_norm_g, m_kv_a_norm_g, m_w_q_b, m_w_kv_b, m_w_out, m_final_norm_g, v_attn_norm_g, v_w_in, v_swa_sinks, v_q_a_norm_g, v_kv_a_norm_g, v_w_q_b, v_w_kv_b, v_w_out, v_final_norm_g):
    x_, y_, c = lax.axis_index("x"), lax.axis_index("y"), lax.axis_index("c")
    chip = 2 * x_ + y_
    where = jnp.stack([chip, 2 * (1 - x_) + y_, 2 * x_ + 1 - y_, 2 * (1 - x_) + 1 - y_, c]).astype(jnp.int32)

    sent = [a.astype(WIRE_DTYPE) for a in (_t(w_in), _t(w_q_b), _t(w_kv_b), w_out)]

    loss, dx, d_attn_g, d_sinks, d_gq, d_gkv, d_final_g, reduced = _device_step(
        x[0], loss_target[0], attn_norm_g, swa_sinks, q_a_norm_g, kv_a_norm_g, final_norm_g.reshape(1, -1), sent, where)
    g_w_in, g_w_q_b, g_w_kv_b, g_w_out = _t(reduced[0]), _t(reduced[1]), _t(reduced[2]), reduced[3]

    small = [d_attn_g, d_sinks, d_gq, d_gkv, d_final_g, loss[:, :1]]
    flat = jnp.concatenate([a.reshape(-1) for a in small])
    n_small = flat.shape[0]
    rows = -(-n_small // 1024) * 8
    total = _comm_allreduce_small(jnp.pad(flat, (0, rows * 128 - n_small)).reshape(rows, 128),
                                  name="comm_allreduce_small").reshape(-1)
    outs, at = [], 0
    for a in small:
        outs.append(total[at:at + a.size].reshape(a.shape))
        at += a.size
    g_attn_g, g_sinks, g_gq, g_gkv, g_final_g, loss_total = outs
    g_final_g = g_final_g.reshape(final_norm_g.shape)

    weights = [attn_norm_g, w_in, swa_sinks, q_a_norm_g, kv_a_norm_g, w_q_b, w_kv_b, w_out, final_norm_g]
    grads = [g_attn_g, g_w_in, g_sinks, g_gq, g_gkv, g_w_q_b, g_w_kv_b, g_w_out, g_final_g]
    ms = [m_attn_norm_g, m_w_in, m_swa_sinks, m_q_a_norm_g, m_kv_a_norm_g, m_w_q_b, m_w_kv_b, m_w_out, m_final_norm_g]
    vs = [v_attn_norm_g, v_w_in, v_swa_sinks, v_q_a_norm_g, v_kv_a_norm_g, v_w_q_b, v_w_kv_b, v_w_out, v_final_norm_g]
    as2d = lambda a: a.reshape(1, -1) if a.ndim == 1 else a
    deltas, new_m, new_v = [], [], []
    for i, (w, g, m, v) in enumerate(zip(weights, grads, ms, vs)):
        view = _t if w is w_in else as2d
        d, mn, vn = _adamw(view(w), reduced[0] if w is w_in else view(g), view(m), view(v), name=f"adamw{i}")
        back = _t if w is w_in else (lambda a: a.reshape(w.shape))
        deltas.append(back(d))
        new_m.append(back(mn))
        new_v.append(back(vn))

    return (loss_total.reshape(()), dx[None], *grads, *deltas, *new_m, *new_v)
```
